```python
import math
import jax, jax.numpy as jnp
from jax import lax
import numpy as np


D_MODEL = 1024
BATCH = 4
SEQ = 8192
DEPTH = 1

N_GLA_HEADS = 4
GLA_DK = 64
GLA_DV = 128
GLA_GATE_RANK = 16
GLA_GATE_NORM = 16.0
GLA_CHUNK = 64
N_DIFF_HEADS = 4
DIFF_DQK = 64
DIFF_DV = 128
Q_BLOCK = 128
NUM_BUCKETS = 32
MAX_DISTANCE = 128
N_EXPERTS = 32
TOP_K = 4
D_FF = D_MODEL
SWIGLU_LIMIT = 7.0
SWIGLU_ALPHA = 1.702
MOE_BLOCK = 256
EPS = 1e-6

GLA_QK_W = N_GLA_HEADS * GLA_DK
GLA_V_W = N_GLA_HEADS * GLA_DV
DIFF_QK_W = N_DIFF_HEADS * 2 * DIFF_DQK
DIFF_V_W = N_DIFF_HEADS * DIFF_DV
MIX_W = GLA_V_W + DIFF_V_W
IN_SPLITS = (GLA_QK_W, GLA_QK_W, GLA_V_W, GLA_V_W, GLA_GATE_RANK, DIFF_QK_W, DIFF_QK_W, DIFF_V_W)
IN_OFFSETS = tuple(int(v) for v in np.cumsum(IN_SPLITS)[:-1])
IN_W = int(sum(IN_SPLITS))

kernel_name = 'hybrid_gla_diffattn_moe_block'


def _rmsnorm(x, g):
    xf = x.astype(jnp.float32)
    y = xf * lax.rsqrt(jnp.mean(xf * xf, axis=-1, keepdims=True) + EPS)
    return (y * g.astype(jnp.float32)).astype(x.dtype)


def _modulate(h, shift, scale):
    return h * (1 + scale[:, None, :]) + shift[:, None, :]


def _t5_bucket(n):
    max_exact = NUM_BUCKETS // 2
    nf = jnp.maximum(n, 1).astype(jnp.float32)
    large = max_exact + (jnp.log(nf / max_exact) / math.log(MAX_DISTANCE / max_exact)
                         * (NUM_BUCKETS - max_exact)).astype(jnp.int32)
    large = jnp.minimum(large, NUM_BUCKETS - 1)
    return jnp.where(n < max_exact, n, large)


def _gla_chunked(q, k, v, g):
    B, H, S, dk = q.shape
    dv = v.shape[-1]
    n = S // GLA_CHUNK
    f32 = jnp.float32
    q, k, v, g = (t.astype(f32).reshape(B, H, n, GLA_CHUNK, t.shape[-1]) for t in (q, k, v, g))
    gc = jnp.cumsum(g, axis=3)
    g_last = gc[:, :, :, -1:, :]
    q_e = q * jnp.exp(gc)
    k_e = k * jnp.exp(-gc)
    causal = jnp.tril(jnp.ones((GLA_CHUNK, GLA_CHUNK), dtype=bool))
    a = jnp.where(causal, jnp.einsum('bhncd,bhnjd->bhncj', q_e, k_e), 0.0)
    o_intra = jnp.einsum('bhncj,bhnje->bhnce', a, v)
    k_s = k * jnp.exp(g_last - gc)
    u = jnp.einsum('bhncd,bhnce->bhnde', k_s, v)
    decay = jnp.exp(g_last[:, :, :, 0, :])

    def step(state, xs):
        u_n, d_n = xs
        return state * d_n[..., None] + u_n, state

    _, s_prev = lax.scan(step, jnp.zeros((B, H, dk, dv), f32),
                         (jnp.moveaxis(u, 2, 0), jnp.moveaxis(decay, 2, 0)))
    s_prev = jnp.moveaxis(s_prev, 0, 2)
    o_inter = jnp.einsum('bhncd,bhnde->bhnce', q_e, s_prev)
    return (o_intra + o_inter).reshape(B, H, S, dv)


def _diff_attention(q, k, v, lam, bias_by_dist):
    B, H, _, S, dqk = q.shape
    nqb = S // Q_BLOCK
    qb = jnp.moveaxis(q.reshape(B, H, 2, nqb, Q_BLOCK, dqk), 3, 0)
    k_pos = jnp.arange(S, dtype=jnp.int32)

    def block(args):
        q_blk, i = args
        q_pos = i * Q_BLOCK + jnp.arange(Q_BLOCK, dtype=jnp.int32)
        dist = q_pos[:, None] - k_pos[None, :]
        bias = bias_by_dist[jnp.clip(dist, 0, S - 1)].astype(jnp.float32)
        bias = bias.reshape(Q_BLOCK, S, H, 2).transpose(2, 3, 0, 1)
        s = jnp.einsum('bhmqd,bhmkd->bhmqk', q_blk, k).astype(jnp.float32) + bias
        s = jnp.where(dist >= 0, s, -jnp.inf)
        p = jax.nn.softmax(s, axis=-1)
        a = p[:, :, 0] - lam * p[:, :, 1]
        return jnp.einsum('bhqk,bhke->bhqe', a.astype(v.dtype), v)

    out = lax.map(block, (qb, jnp.arange(nqb, dtype=jnp.int32)))
    return jnp.moveaxis(out, 0, 2).reshape(B, H, S, v.shape[-1])


def _token_mixer(h, w_in, w_gk_up, b_gk_up, g_gla_out, g_qnorm, g_knorm, lq1, lk1, lq2, lk2,
                 g_subln, w_out, bias_by_dist, lambda_init):
    B, S, _ = h.shape
    f32 = jnp.float32
    proj = h @ w_in
    q_g, k_g, v_g, r_g, gk_lo, q_d, k_d, v_d = jnp.split(proj, IN_OFFSETS, axis=-1)

    def heads(t, nh):
        return t.reshape(B, S, nh, -1).transpose(0, 2, 1, 3)

    gk = jax.nn.log_sigmoid((gk_lo @ w_gk_up + b_gk_up).astype(f32)) / GLA_GATE_NORM
    o_g = _gla_chunked(heads(q_g, N_GLA_HEADS) * (GLA_DK ** -0.5), heads(k_g, N_GLA_HEADS),
                       heads(v_g, N_GLA_HEADS), heads(gk, N_GLA_HEADS))
    o_g = _rmsnorm(o_g, g_gla_out).transpose(0, 2, 1, 3).reshape(B, S, GLA_V_W).astype(h.dtype)
    o_g = o_g * jax.nn.silu(r_g)

    q_d = _rmsnorm(q_d.reshape(B, S, N_DIFF_HEADS, 2, DIFF_DQK), g_qnorm).transpose(0, 2, 3, 1, 4)
    q_d = q_d * (DIFF_DQK ** -0.5)
    k_d = _rmsnorm(k_d.reshape(B, S, N_DIFF_HEADS, 2, DIFF_DQK), g_knorm).transpose(0, 2, 3, 1, 4)
    lam = (jnp.exp(jnp.sum(lq1.astype(f32) * lk1.astype(f32)))
           - jnp.exp(jnp.sum(lq2.astype(f32) * lk2.astype(f32))) + lambda_init)
    o_d = _diff_attention(q_d, k_d, heads(v_d, N_DIFF_HEADS), lam, bias_by_dist)
    o_d = (_rmsnorm(o_d, g_subln) * (1.0 - lambda_init)).transpose(0, 2, 1, 3).reshape(B, S, DIFF_V_W)

    return jnp.concatenate([o_g, o_d.astype(h.dtype)], axis=-1) @ w_out


def _moe(h, w_router, b_router, w_gate_up, b_gate_up, w_down, b_down):
    B, S, D = h.shape
    T = B * S
    A = T * TOP_K
    hf = h.reshape(T, D)
    logits = (hf @ w_router + b_router).astype(jnp.float32)
    top_v, top_e = lax.top_k(logits, TOP_K)
    top_w = jax.nn.softmax(top_v, axis=-1)
    e_flat = top_e.reshape(A).astype(jnp.int32)
    w_flat = top_w.reshape(A)
    tok_flat = jnp.arange(A, dtype=jnp.int32) // TOP_K
    order = jnp.argsort(e_flat, stable=True)
    e_sorted = e_flat[order]
    counts = jnp.zeros((N_EXPERTS,), jnp.int32).at[e_flat].add(1)
    starts = jnp.cumsum(counts) - counts
    padded = ((counts + MOE_BLOCK - 1) // MOE_BLOCK) * MOE_BLOCK
    p_ends = jnp.cumsum(padded)
    p_starts = p_ends - padded
    dest = p_starts[e_sorted] + (jnp.arange(A, dtype=jnp.int32) - starts[e_sorted])
    n_blocks = -(-A // MOE_BLOCK) + N_EXPERTS
    P = n_blocks * MOE_BLOCK
    buf_tok = jnp.zeros((P,), jnp.int32).at[dest].set(tok_flat[order])
    buf_w = jnp.zeros((P,), jnp.float32).at[dest].set(w_flat[order])
    block_e = jnp.minimum(jnp.searchsorted(p_ends, jnp.arange(n_blocks, dtype=jnp.int32) * MOE_BLOCK,
                                           side='right'), N_EXPERTS - 1).astype(jnp.int32)

    def expert_block(args):
        tok_b, w_b, e = args
        xb = hf[tok_b]
        gu = xb @ w_gate_up[e] + b_gate_up[e]
        gate = jnp.minimum(gu[:, :D_FF], SWIGLU_LIMIT)
        up = jnp.clip(gu[:, D_FF:], -SWIGLU_LIMIT, SWIGLU_LIMIT)
        y = (up + 1) * (gate * jax.nn.sigmoid(SWIGLU_ALPHA * gate))
        out = y @ w_down[e] + b_down[e]
        return out * w_b[:, None].astype(out.dtype)

    ys = lax.map(expert_block, (buf_tok.reshape(n_blocks, MOE_BLOCK),
                                buf_w.reshape(n_blocks, MOE_BLOCK), block_e))
    out = jnp.zeros((T, D), h.dtype).at[buf_tok].add(ys.reshape(P, D).astype(h.dtype))
    return out.reshape(B, S, D)


def setup_inputs(seed: int = 0) -> dict:
    key = jax.random.key(seed)
    ks = jax.random.split(key, 26)
    f32 = jnp.float32
    nrm = lambda k, shape, s: jax.random.normal(k, shape, f32) * s
    D, L, E, F = D_MODEL, DEPTH, N_EXPERTS, D_FF
    return {
        'x': nrm(ks[0], (BATCH, SEQ, D), 1.0),
        'c': nrm(ks[1], (BATCH, D), 1.0),
        'rel_bias_table': nrm(ks[2], (NUM_BUCKETS, 2 * N_DIFF_HEADS), 0.5),
        'w_ada': nrm(ks[3], (L, D, 6 * D), 0.5 * D ** -0.5),
        'b_ada': nrm(ks[4], (L, 6 * D), 0.02),
        'g_norm1': 1.0 + nrm(ks[5], (L, D), 0.02),
        'w_in': nrm(ks[6], (L, D, IN_W), D ** -0.5),
        'w_gk_up': nrm(ks[7], (L, GLA_GATE_RANK, GLA_QK_W), GLA_GATE_RANK ** -0.5),
        'b_gk_up': nrm(ks[8], (L, GLA_QK_W), 0.1),
        'g_gla_out': 1.0 + nrm(ks[9], (L, GLA_DV), 0.02),
        'g_qnorm': 1.0 + nrm(ks[10], (L, DIFF_DQK), 0.02),
        'g_knorm': 1.0 + nrm(ks[11], (L, DIFF_DQK), 0.02),
        'lambda_q1': nrm(ks[12], (L, DIFF_DQK), 0.1),
        'lambda_k1': nrm(ks[13], (L, DIFF_DQK), 0.1),
        'lambda_q2': nrm(ks[14], (L, DIFF_DQK), 0.1),
        'lambda_k2': nrm(ks[15], (L, DIFF_DQK), 0.1),
        'g_subln': 1.0 + nrm(ks[16], (L, DIFF_DV), 0.02),
        'w_out': nrm(ks[17], (L, MIX_W, D), MIX_W ** -0.5),
        'g_norm2': 1.0 + nrm(ks[18], (L, D), 0.02),
        'w_router': nrm(ks[19], (L, D, E), D ** -0.5),
        'b_router': nrm(ks[20], (L, E), 0.01),
        'w_gate_up': nrm(ks[21], (L, E, D, 2 * F), D ** -0.5),
        'b_gate_up': nrm(ks[22], (L, E, 2 * F), 0.02),
        'w_down': nrm(ks[23], (L, E, F, D), F ** -0.5),
        'b_down': nrm(ks[24], (L, E, D), 0.02),
    }


def reference(x, c, rel_bias_table, w_ada, b_ada, g_norm1, w_in, w_gk_up, b_gk_up, g_gla_out,
              g_qnorm, g_knorm, lambda_q1, lambda_k1, lambda_q2, lambda_k2, g_subln, w_out,
              g_norm2, w_router, b_router, w_gate_up, b_gate_up, w_down, b_down):
    S = x.shape[1]
    bias_by_dist = rel_bias_table[_t5_bucket(jnp.arange(S, dtype=jnp.int32))]
    cond = jax.nn.silu(c)
    for l in range(DEPTH):
        lambda_init = 0.8 - 0.6 * math.exp(-0.3 * l)
        mod = cond @ w_ada[l] + b_ada[l]
        sh1, sc1, gt1, sh2, sc2, gt2 = jnp.split(mod, 6, axis=-1)
        h = _modulate(_rmsnorm(x, g_norm1[l]), sh1, sc1)
        x = x + gt1[:, None, :] * _token_mixer(
            h, w_in[l], w_gk_up[l], b_gk_up[l], g_gla_out[l], g_qnorm[l], g_knorm[l],
            lambda_q1[l], lambda_k1[l], lambda_q2[l], lambda_k2[l], g_subln[l], w_out[l],
            bias_by_dist, lambda_init)
        h = _modulate(_rmsnorm(x, g_norm2[l]), sh2, sc2)
        x = x + gt2[:, None, :] * _moe(h, w_router[l], b_router[l], w_gate_up[l], b_gate_up[l],
                                       w_down[l], b_down[l])
    return x
```

```python
import functools
import math

import jax
import jax.numpy as jnp
from jax import lax
from jax.experimental import pallas as pl
from jax.experimental.pallas import tpu as pltpu

f32 = jnp.float32
bf16 = jnp.bfloat16

N_GLA_HEADS = 4
GLA_DK = 64
GLA_DV = 128
GLA_GATE_RANK = 16
GLA_GATE_NORM = 16.0
GLA_CHUNK = 64
N_DIFF_HEADS = 4
DIFF_DQK = 64
DIFF_DV = 128
NUM_BUCKETS = 32
MAX_DISTANCE = 128
TOP_K = 4
SWIGLU_LIMIT = 7.0
SWIGLU_ALPHA = 1.702
EPS = 1e-6

GLA_QK_W = N_GLA_HEADS * GLA_DK
GLA_V_W = N_GLA_HEADS * GLA_DV
DIFF_QK_W = N_DIFF_HEADS * 2 * DIFF_DQK
DIFF_V_W = N_DIFF_HEADS * DIFF_DV

LANES = 128
NEG = -1e30
VMEM_LIMIT = 48 * 1024 * 1024

TM_IN = 512
TG_GLA = 1024
PAIR = 2 * GLA_CHUNK
TQ = 512
TR = 512
TD = 256
FFN_BLK = 512


def _nt(a, b):
    return lax.dot_general(a, b, (((1,), (1,)), ((), ())), preferred_element_type=f32)


def _mm(a, b):
    return jnp.dot(a, b, preferred_element_type=f32)


def _split(x):
    hi = x.astype(bf16)
    lo = (x - hi.astype(f32)).astype(bf16)
    return hi, lo


def _silu(x):
    return x * jax.nn.sigmoid(x)


def _ada_kernel(c_ref, w_ref, b_ref, o_ref):
    c = c_ref[...]
    o_ref[...] = _mm(_silu(c).astype(bf16), w_ref[...].astype(bf16)) + b_ref[...]


def _ada(c, w_ada, b_ada):
    B, D = c.shape
    N = w_ada.shape[1]
    bp = 8
    cp = jnp.zeros((bp, D), f32).at[:B].set(c)
    tn = 1536
    out = pl.pallas_call(
        _ada_kernel,
        out_shape=jax.ShapeDtypeStruct((bp, N), f32),
        grid=(N // tn,),
        in_specs=[pl.BlockSpec((bp, D), lambda j: (0, 0)),
                  pl.BlockSpec((D, tn), lambda j: (0, j)),
                  pl.BlockSpec((1, tn), lambda j: (0, j))],
        out_specs=pl.BlockSpec((bp, tn), lambda j: (0, j)),
        compiler_params=pltpu.CompilerParams(vmem_limit_bytes=VMEM_LIMIT),
        name="ada",
    )(cp, w_ada, b_ada.reshape(1, N))
    return out[:B].reshape(B, 6, D)


def _inproj_kernel(x_ref, mod_ref, g1_ref, wm_ref, wkt_ref, wlo_ref, wup_ref, wupt_ref,
                   bup_ref, bupt_ref, gqk_ref, grp_ref, grpt_ref,
                   qg_ref, kg_ref, gk_ref, kgt_ref, gkt_ref, vg_ref, rg_ref,
                   qd_ref, kd_ref, vd_ref):
    x = x_ref[0]
    ms = jnp.mean(x * x, axis=-1, keepdims=True)
    y = x * lax.rsqrt(ms + EPS) * g1_ref[...]
    h = (y * (1.0 + mod_ref[0, 1:2, :]) + mod_ref[0, 0:1, :]).astype(bf16)

    def proj(a, b):
        return _mm(h, wm_ref[:, a:b])

    o = 0
    qg_ref[0] = proj(o, o + GLA_QK_W); o += GLA_QK_W
    kg_ref[0] = proj(o, o + GLA_QK_W); o += GLA_QK_W
    vg_ref[0] = proj(o, o + GLA_V_W).astype(bf16); o += GLA_V_W
    rg_ref[0] = proj(o, o + GLA_V_W); o += GLA_V_W
    qk = proj(o, o + 2 * DIFF_QK_W); o += 2 * DIFF_QK_W
    vd_ref[0] = proj(o, o + DIFF_V_W).astype(bf16)

    kgt = _nt(wkt_ref[...], h)
    for j in range(kgt.shape[1] // PAIR):
        kgt_ref[0, j] = kgt[:, j * PAIR:(j + 1) * PAIR]

    lo = _mm(h, wlo_ref[...]).astype(bf16)
    z = _mm(lo, wup_ref[...]) + bup_ref[...]
    gk_ref[0] = (jnp.minimum(z, 0.0) - jnp.log1p(jnp.exp(-jnp.abs(z)))) * (1.0 / GLA_GATE_NORM)
    zt = _nt(wupt_ref[...], lo) + bupt_ref[...]
    gkt = (jnp.minimum(zt, 0.0) - jnp.log1p(jnp.exp(-jnp.abs(zt)))) * (1.0 / GLA_GATE_NORM)
    for j in range(gkt.shape[1] // PAIR):
        gkt_ref[0, j] = gkt[:, j * PAIR:(j + 1) * PAIR]

    sq_hi, sq_lo = _split(qk * qk)
    gs = _mm(sq_hi, grp_ref[...]) + _mm(sq_lo, grp_ref[...])
    r = lax.rsqrt(gs * (1.0 / DIFF_DQK) + EPS)
    r_hi, r_lo = _split(r)
    rb = _mm(r_hi, grpt_ref[...]) + _mm(r_lo, grpt_ref[...])
    qkn = qk * rb * gqk_ref[...]
    qd_ref[0] = qkn[:, :DIFF_QK_W].astype(bf16)
    kd_ref[0] = qkn[:, DIFF_QK_W:].astype(bf16)


def _inproj(x, mod, g_norm1, w_in, w_gk_up, b_gk_up, g_qnorm, g_knorm):
    B, S, D = x.shape
    offs = [0]
    for w in (GLA_QK_W, GLA_QK_W, GLA_V_W, GLA_V_W, GLA_GATE_RANK, DIFF_QK_W, DIFF_QK_W, DIFF_V_W):
        offs.append(offs[-1] + w)
    w_main = jnp.concatenate([w_in[:, offs[0]:offs[4]], w_in[:, offs[5]:offs[8]]], axis=1).astype(bf16)
    w_kt = w_in[:, offs[1]:offs[2]].T.astype(bf16)
    w_lo = jnp.zeros((D, LANES), f32).at[:, :GLA_GATE_RANK].set(w_in[:, offs[4]:offs[5]]).astype(bf16)
    w_up = jnp.zeros((LANES, GLA_QK_W), f32).at[:GLA_GATE_RANK].set(w_gk_up).astype(bf16)
    w_upt = w_up.T
    b_up = b_gk_up.reshape(1, GLA_QK_W)
    b_upt = b_gk_up.reshape(GLA_QK_W, 1)
    n_grp = 2 * DIFF_QK_W // DIFF_DQK
    gqk = jnp.concatenate([jnp.tile(g_qnorm, n_grp // 2) * (DIFF_DQK ** -0.5),
                           jnp.tile(g_knorm, n_grp // 2)]).reshape(1, 2 * DIFF_QK_W)
    grp = (jnp.arange(2 * DIFF_QK_W)[:, None] // DIFF_DQK == jnp.arange(LANES)[None, :]).astype(bf16)
    grpt = grp.T
    nw = w_main.shape[1]
    tm = TM_IN
    const = lambda shape: pl.BlockSpec(shape, lambda b, i: (0,) * len(shape))
    row = lambda w: pl.BlockSpec((1, tm, w), lambda b, i: (b, i, 0))
    colT = pl.BlockSpec((1, tm // PAIR, GLA_QK_W, PAIR), lambda b, i: (b, i, 0, 0))
    outs = pl.pallas_call(
        _inproj_kernel,
        out_shape=[jax.ShapeDtypeStruct((B, S, GLA_QK_W), f32),
                   jax.ShapeDtypeStruct((B, S, GLA_QK_W), f32),
                   jax.ShapeDtypeStruct((B, S, GLA_QK_W), f32),
                   jax.ShapeDtypeStruct((B, S // PAIR, GLA_QK_W, PAIR), f32),
                   jax.ShapeDtypeStruct((B, S // PAIR, GLA_QK_W, PAIR), f32),
                   jax.ShapeDtypeStruct((B, S, GLA_V_W), bf16),
                   jax.ShapeDtypeStruct((B, S, GLA_V_W), f32),
                   jax.ShapeDtypeStruct((B, S, DIFF_QK_W), bf16),
                   jax.ShapeDtypeStruct((B, S, DIFF_QK_W), bf16),
                   jax.ShapeDtypeStruct((B, S, DIFF_V_W), bf16)],
        grid=(B, S // tm),
        in_specs=[row(D),
                  pl.BlockSpec((1, 6, D), lambda b, i: (b, 0, 0)),
                  const((1, D)), const((D, nw)), const((GLA_QK_W, D)), const((D, LANES)),
                  const((LANES, GLA_QK_W)), const((GLA_QK_W, LANES)),
                  const((1, GLA_QK_W)), const((GLA_QK_W, 1)),
                  const((1, 2 * DIFF_QK_W)), const((2 * DIFF_QK_W, LANES)),
                  const((LANES, 2 * DIFF_QK_W))],
        out_specs=[row(GLA_QK_W), row(GLA_QK_W), row(GLA_QK_W), colT, colT,
                   row(GLA_V_W), row(GLA_V_W), row(DIFF_QK_W), row(DIFF_QK_W), row(DIFF_V_W)],
        compiler_params=pltpu.CompilerParams(
            dimension_semantics=("arbitrary", "arbitrary"), vmem_limit_bytes=VMEM_LIMIT),
        name="inproj",
    )(x, mod, g_norm1.reshape(1, D), w_main, w_kt, w_lo, w_up, w_upt, b_up, b_upt, gqk, grp, grpt)
    return outs


def _gla_kernel(q_ref, k_ref, g_ref, kt_ref, gt_ref, v_ref, r_ref, gout_ref, tri_ref, trit_ref,
                o_ref, s_ref, *, n_pairs):
    H, DK, DV = N_GLA_HEADS, GLA_DK, GLA_DV

    @pl.when(pl.program_id(1) == 0)
    def _():
        s_ref[...] = jnp.zeros_like(s_ref)

    tri = tri_ref[...]
    trit = trit_ref[...]
    tri_b = tri > 0
    lane_head = lax.broadcasted_iota(jnp.int32, (1, H * DK), 1) // DK
    row_head = lax.broadcasted_iota(jnp.int32, (H * PAIR, 1), 0) // PAIR
    qmask = row_head == lane_head
    row_first = lax.broadcasted_iota(jnp.int32, (PAIR, 1), 0) < GLA_CHUNK
    row_first4 = (lax.broadcasted_iota(jnp.int32, (H * PAIR, 1), 0) % PAIR) < GLA_CHUNK
    lane_first = lax.broadcasted_iota(jnp.int32, (1, PAIR), 1) < GLA_CHUNK
    scale = DK ** -0.5
    gout = gout_ref[...]

    def pair(p, carry):
        r0 = pl.multiple_of(p * PAIR, PAIR)
        q = q_ref[0, pl.ds(r0, PAIR), :]
        k = k_ref[0, pl.ds(r0, PAIR), :]
        g = g_ref[0, pl.ds(r0, PAIR), :]
        kt = kt_ref[0, p]
        gt = gt_ref[0, p]
        v = v_ref[0, pl.ds(r0, PAIR), :]

        g_hi, g_lo = _split(g)
        gc = _mm(tri, g_hi) + _mm(tri, g_lo)
        gt_hi, gt_lo = _split(gt)
        gct = _mm(gt_hi, trit) + _mm(gt_lo, trit)
        g_last = jnp.where(row_first, gc[GLA_CHUNK - 1:GLA_CHUNK, :], gc[PAIR - 1:PAIR, :])
        gl0 = gct[:, GLA_CHUNK - 1:GLA_CHUNK]
        gl1 = gct[:, PAIR - 1:PAIR]
        g_last_t = jnp.where(lane_first, gl0, gl1)

        q_e = (q * (jnp.exp(gc) * scale)).astype(bf16)
        k_e = (k * jnp.exp(-gc)).astype(bf16)
        ks_t = kt * jnp.exp(g_last_t - gct)
        ks_t0 = jnp.where(lane_first, ks_t, 0.0).astype(bf16)
        ks_t1 = jnp.where(lane_first, 0.0, ks_t).astype(bf16)
        del g_last

        qm = jnp.where(qmask, jnp.concatenate([q_e] * H, axis=0), jnp.zeros((), bf16))
        a = _nt(qm, k_e)
        s0 = s_ref[...]

        u0 = []
        u1 = []
        for h in range(H):
            v_h = v[:, h * DV:(h + 1) * DV]
            u0.append(_mm(ks_t0[h * DK:(h + 1) * DK], v_h))
            u1.append(_mm(ks_t1[h * DK:(h + 1) * DK], v_h))
        u0 = jnp.concatenate(u0, axis=0)
        u1 = jnp.concatenate(u1, axis=0)
        s1 = s0 * jnp.exp(gl0) + u0
        s_ref[...] = s1 * jnp.exp(gl1) + u1

        o_inter = jnp.where(row_first4, _mm(qm, s0.astype(bf16)), _mm(qm, s1.astype(bf16)))
        for h in range(H):
            a_h = jnp.where(tri_b, a[h * PAIR:(h + 1) * PAIR], 0.0).astype(bf16)
            o_h = _mm(a_h, v[:, h * DV:(h + 1) * DV]) + o_inter[h * PAIR:(h + 1) * PAIR]
            ms = jnp.mean(o_h * o_h, axis=-1, keepdims=True)
            o_n = o_h * lax.rsqrt(ms + EPS) * gout
            r_h = r_ref[0, pl.ds(r0, PAIR), h * DV:(h + 1) * DV]
            o_ref[0, pl.ds(r0, PAIR), h * DV:(h + 1) * DV] = (o_n * _silu(r_h)).astype(bf16)
        return carry

    lax.fori_loop(0, n_pairs, pair, 0)


def _gla(qg, kg, gk, kgt, gkt, vg, rg, g_gla_out):
    B, S, _ = qg.shape
    tg = min(TG_GLA, S)
    r = jnp.arange(PAIR)
    tri = ((r[:, None] // GLA_CHUNK == r[None, :] // GLA_CHUNK) & (r[None, :] <= r[:, None])).astype(bf16)
    row = lambda w: pl.BlockSpec((1, tg, w), lambda b, i: (b, i, 0))
    colT = pl.BlockSpec((1, tg // PAIR, GLA_QK_W, PAIR), lambda b, i: (b, i, 0, 0))
    const = lambda shape: pl.BlockSpec(shape, lambda b, i: (0,) * len(shape))
    return pl.pallas_call(
        functools.partial(_gla_kernel, n_pairs=tg // PAIR),
        out_shape=jax.ShapeDtypeStruct((B, S, GLA_V_W), bf16),
        grid=(B, S // tg),
        in_specs=[row(GLA_QK_W), row(GLA_QK_W), row(GLA_QK_W), colT, colT,
                  row(GLA_V_W), row(GLA_V_W), const((1, GLA_DV)),
                  const((PAIR, PAIR)), const((PAIR, PAIR))],
        out_specs=row(GLA_V_W),
        scratch_shapes=[pltpu.VMEM((GLA_QK_W, GLA_DV), f32)],
        compiler_params=pltpu.CompilerParams(
            dimension_semantics=("arbitrary", "arbitrary"), vmem_limit_bytes=VMEM_LIMIT),
        name="gla",
    )(qg, kg, gk, kgt, gkt, vg, rg, g_gla_out.reshape(1, GLA_DV), tri, tri.T)


def _attn_kernel(q_ref, k_ref, v_ref, bias_ref, lamv_ref, gsub_ref, o_ref, *, lambda_init):
    qi = pl.program_id(2)
    tq = q_ref.shape[1]
    q = q_ref[0]
    lane = lax.broadcasted_iota(jnp.int32, (1, 2 * DIFF_DQK), 1)
    zero = jnp.zeros((), bf16)
    qs = (jnp.where(lane < DIFF_DQK, q, zero), jnp.where(lane < DIFF_DQK, zero, q))

    def update(state, kb, vb, bias):
        new = []
        for c in range(2):
            m, l, acc = state[c]
            s = _nt(qs[c], kb)
            if bias is not None:
                s = s + bias[c]
            m_new = jnp.maximum(m, jnp.max(s, axis=-1, keepdims=True))
            alpha = jnp.exp(m - m_new)
            p = jnp.exp(s - m_new)
            l = alpha * l + jnp.sum(p, axis=-1, keepdims=True)
            acc = alpha * acc + _mm(p.astype(bf16), vb)
            new.append((m_new, l, acc))
        return tuple(new)

    init = tuple((jnp.full((tq, 1), NEG, f32), jnp.zeros((tq, 1), f32), jnp.zeros((tq, DIFF_DV), f32))
                 for _ in range(2))

    def far(kj, state):
        k0 = pl.multiple_of(kj * tq, tq)
        return update(state, k_ref[0, pl.ds(k0, tq), :], v_ref[0, pl.ds(k0, tq), :], None)

    state = lax.fori_loop(0, jnp.maximum(qi - 1, 0), far, init)

    kd0 = pl.multiple_of(qi * tq, tq)
    state = update(state, k_ref[0, pl.ds(kd0, tq), :], v_ref[0, pl.ds(kd0, tq), :],
                   (bias_ref[0, 0, 1], bias_ref[0, 1, 1]))
    kp0 = pl.multiple_of(jnp.maximum(qi - 1, 0) * tq, tq)
    has_prev = qi > 0
    state = update(state, k_ref[0, pl.ds(kp0, tq), :], v_ref[0, pl.ds(kp0, tq), :],
                   (jnp.where(has_prev, bias_ref[0, 0, 0], NEG), jnp.where(has_prev, bias_ref[0, 1, 0], NEG)))

    lv = lamv_ref[...]
    lam = (jnp.exp(jnp.sum(lv[0:1] * lv[1:2], axis=-1, keepdims=True))
           - jnp.exp(jnp.sum(lv[2:3] * lv[3:4], axis=-1, keepdims=True)) + lambda_init)
    (_, l0, a0), (_, l1, a1) = state
    o = a0 / l0 - lam * (a1 / l1)
    ms = jnp.mean(o * o, axis=-1, keepdims=True)
    o_ref[0] = (o * lax.rsqrt(ms + EPS) * gsub_ref[...] * (1.0 - lambda_init)).astype(bf16)


def _t5_bucket(n):
    max_exact = NUM_BUCKETS // 2
    nf = jnp.maximum(n, 1).astype(f32)
    large = max_exact + (jnp.log(nf / max_exact) / math.log(MAX_DISTANCE / max_exact)
                         * (NUM_BUCKETS - max_exact)).astype(jnp.int32)
    large = jnp.minimum(large, NUM_BUCKETS - 1)
    return jnp.where(n < max_exact, n, large)


def _toeplitz(vec, n):
    w = jnp.concatenate([vec[:, ::-1], jnp.zeros((vec.shape[0], 1), vec.dtype)], axis=1)
    flat = jnp.tile(w, (1, n))[:, :n * (2 * n - 1)].reshape(-1, n, 2 * n - 1)
    return flat[:, :, n - 1:]


def _bias_tiles(rel_bias_table, S, n):
    HM = rel_bias_table.shape[1]
    assert n >= MAX_DISTANCE
    d = jnp.arange(2 * n, dtype=jnp.int32)
    by_dist = rel_bias_table[_t5_bucket(d)].astype(f32).T
    rel = by_dist - rel_bias_table[NUM_BUCKETS - 1].astype(f32)[:, None]
    diag = _toeplitz(jnp.concatenate([jnp.full((HM, n - 1), NEG, f32), rel[:, :n]], axis=1), n)
    prev = _toeplitz(rel[:, 1:2 * n], n)
    return jnp.stack([prev, diag], axis=1).reshape(HM // 2, 2, 2, n, n)


def _attn(qd, kd, vd, bias_tiles, lamv, g_subln, lambda_init):
    B, S, _ = qd.shape
    H = N_DIFF_HEADS
    tq = min(TQ, S)
    return pl.pallas_call(
        functools.partial(_attn_kernel, lambda_init=lambda_init),
        out_shape=jax.ShapeDtypeStruct((B, S, DIFF_V_W), bf16),
        grid=(B, H, S // tq),
        in_specs=[pl.BlockSpec((1, tq, 2 * DIFF_DQK), lambda b, h, i: (b, i, h)),
                  pl.BlockSpec((1, S, 2 * DIFF_DQK), lambda b, h, i: (b, 0, h)),
                  pl.BlockSpec((1, S, DIFF_DV), lambda b, h, i: (b, 0, h)),
                  pl.BlockSpec((1, 2, 2, tq, tq), lambda b, h, i: (h, 0, 0, 0, 0)),
                  pl.BlockSpec((4, DIFF_DQK), lambda b, h, i: (0, 0)),
                  pl.BlockSpec((1, DIFF_DV), lambda b, h, i: (0, 0))],
        out_specs=pl.BlockSpec((1, tq, DIFF_DV), lambda b, h, i: (b, i, h)),
        compiler_params=pltpu.CompilerParams(
            dimension_semantics=("arbitrary", "arbitrary", "arbitrary"), vmem_limit_bytes=VMEM_LIMIT),
        name="attn",
    )(qd, kd, vd, bias_tiles, lamv, g_subln.reshape(1, DIFF_DV))


def _outproj_kernel(og_ref, od_ref, x_ref, mod_ref, wo_ref, g2_ref, wr_ref, br_ref,
                    x1_ref, hp_ref, lg_ref):
    half = og_ref.shape[2]
    mix = _mm(og_ref[0], wo_ref[:half, :]) + _mm(od_ref[0], wo_ref[half:, :])
    x1 = x_ref[0] + mod_ref[0, 2:3, :] * mix
    x1_ref[0] = x1
    ms = jnp.mean(x1 * x1, axis=-1, keepdims=True)
    y = x1 * lax.rsqrt(ms + EPS) * g2_ref[...]
    h = (y * (1.0 + mod_ref[0, 4:5, :]) + mod_ref[0, 3:4, :]).astype(bf16)
    lg_ref[...] = _mm(h, wr_ref[...]) + br_ref[...]
    hf = h.astype(f32)
    dh = hf.shape[1] // 2
    lo_bits = pltpu.bitcast(hf[:, :dh], jnp.uint32) >> 16
    hi_bits = pltpu.bitcast(hf[:, dh:], jnp.uint32) & jnp.uint32(0xFFFF0000)
    hp_ref[...] = lo_bits | hi_bits


def _outproj(og, od, x, mod, w_out, g_norm2, w_router, b_router):
    B, S, D = x.shape
    E = w_router.shape[1]
    tm = TM_IN
    nj = S // tm
    w_r = jnp.zeros((D, LANES), f32).at[:, :E].set(w_router).astype(bf16)
    b_r = jnp.full((1, LANES), NEG, f32).at[0, :E].set(b_router)
    const = lambda shape: pl.BlockSpec(shape, lambda b, i: (0,) * len(shape))
    return pl.pallas_call(
        _outproj_kernel,
        out_shape=[jax.ShapeDtypeStruct((B, S, D), f32),
                   jax.ShapeDtypeStruct((B * S, D // 2), jnp.uint32),
                   jax.ShapeDtypeStruct((B * S, LANES), f32)],
        grid=(B, nj),
        in_specs=[pl.BlockSpec((1, tm, og.shape[2]), lambda b, i: (b, i, 0)),
                  pl.BlockSpec((1, tm, od.shape[2]), lambda b, i: (b, i, 0)),
                  pl.BlockSpec((1, tm, D), lambda b, i: (b, i, 0)),
                  pl.BlockSpec((1, 6, D), lambda b, i: (b, 0, 0)),
                  const((w_out.shape[0], D)), const((1, D)), const((D, LANES)), const((1, LANES))],
        out_specs=[pl.BlockSpec((1, tm, D), lambda b, i: (b, i, 0)),
                   pl.BlockSpec((tm, D // 2), lambda b, i: (b * nj + i, 0)),
                   pl.BlockSpec((tm, LANES), lambda b, i: (b * nj + i, 0))],
        compiler_params=pltpu.CompilerParams(
            dimension_semantics=("arbitrary", "arbitrary"), vmem_limit_bytes=VMEM_LIMIT),
        name="outproj",
    )(og, od, x, mod, w_out.astype(bf16), g_norm2.reshape(1, D), w_r, b_r)


def _route_kernel(lg_ref, lt_ref, ri_ref, rw_ref, cnt_ref, run_ref):
    @pl.when(pl.program_id(0) == 0)
    def _():
        run_ref[...] = jnp.zeros_like(run_ref)

    x = lg_ref[...]
    tr = x.shape[0]
    lane = lax.broadcasted_iota(jnp.int32, (tr, LANES), 1)
    lane_f = lane.astype(f32)
    vals, hots, idxs = [], [], []
    for _ in range(TOP_K):
        m = jnp.max(x, axis=-1, keepdims=True)
        idx = jnp.min(jnp.where(x == m, lane_f, float(LANES)), axis=-1, keepdims=True)
        hot = lane_f == idx
        x = jnp.where(hot, -jnp.inf, x)
        vals.append(m)
        hots.append(hot)
        idxs.append(idx.astype(jnp.int32))
    ex = [jnp.exp(v - vals[0]) for v in vals]
    den = ex[0] + ex[1] + ex[2] + ex[3]
    sel = (hots[0] | hots[1] | hots[2] | hots[3]).astype(f32)
    rank = _mm(lt_ref[...], sel.astype(bf16)) + run_ref[...]
    run_ref[...] = run_ref[...] + jnp.sum(sel, axis=0, keepdims=True)
    cnt_ref[...] = run_ref[...]
    ri = jnp.zeros((tr, LANES), jnp.int32)
    rw = jnp.zeros((tr, LANES), f32)
    for k in range(TOP_K):
        rk = jnp.sum(jnp.where(hots[k], rank, 0.0), axis=-1, keepdims=True).astype(jnp.int32)
        ri = jnp.where(lane == k, rk, ri)
        ri = jnp.where(lane == TOP_K + k, idxs[k], ri)
        rw = jnp.where(lane == k, ex[k] / den, rw)
    ri_ref[...] = ri
    rw_ref[...] = rw


def _route(logits):
    T = logits.shape[0]
    tr = min(TR, T)
    r = jnp.arange(tr)
    lt = (r[None, :] < r[:, None]).astype(bf16)
    return pl.pallas_call(
        _route_kernel,
        out_shape=[jax.ShapeDtypeStruct((T, LANES), jnp.int32),
                   jax.ShapeDtypeStruct((T, LANES), f32),
                   jax.ShapeDtypeStruct((1, LANES), f32)],
        grid=(T // tr,),
        in_specs=[pl.BlockSpec((tr, LANES), lambda i: (i, 0)),
                  pl.BlockSpec((tr, tr), lambda i: (0, 0))],
        out_specs=[pl.BlockSpec((tr, LANES), lambda i: (i, 0)),
                   pl.BlockSpec((tr, LANES), lambda i: (i, 0)),
                   pl.BlockSpec((1, LANES), lambda i: (0, 0))],
        scratch_shapes=[pltpu.VMEM((1, LANES), f32)],
        compiler_params=pltpu.CompilerParams(dimension_semantics=("arbitrary",)),
        name="route",
    )(logits, lt)


def _row_copy(src, s, dst, d, sem):
    return pltpu.make_async_copy(src.at[pl.ds(s, 1)], dst.at[pl.ds(d, 1)], sem)


def _dispatch_kernel(dest_ref, h_ref, xs_in_ref, xs_ref, sem):
    del xs_in_ref
    base = pl.program_id(0) * TD

    def issue(r, c):
        for k in range(TOP_K):
            _row_copy(h_ref, base + r, xs_ref, dest_ref[r * TOP_K + k], sem).start()
        return c

    def drain(r, c):
        for k in range(TOP_K):
            _row_copy(h_ref, 0, xs_ref, 0, sem).wait()
        return c

    lax.fori_loop(0, TD, issue, 0)
    lax.fori_loop(0, TD, drain, 0)


def _dispatch(dest_flat, hp, n_rows):
    T, W = hp.shape
    xs0 = jnp.zeros((n_rows, W), hp.dtype)
    return pl.pallas_call(
        _dispatch_kernel,
        out_shape=jax.ShapeDtypeStruct((n_rows, W), hp.dtype),
        grid=(T // TD,),
        in_specs=[pl.BlockSpec((TD * TOP_K,), lambda i: (i,), memory_space=pltpu.SMEM),
                  pl.BlockSpec(memory_space=pl.ANY),
                  pl.BlockSpec(memory_space=pl.ANY)],
        out_specs=pl.BlockSpec(memory_space=pl.ANY),
        scratch_shapes=[pltpu.SemaphoreType.DMA(())],
        input_output_aliases={2: 0},
        compiler_params=pltpu.CompilerParams(dimension_semantics=("arbitrary",)),
        name="dispatch",
    )(dest_flat, hp, xs0)


def _ffn_kernel(be_ref, nu_ref, xs_ref, wgu_ref, bgu_ref, wd_ref, bd_ref, ys_ref):
    del be_ref

    @pl.when(pl.program_id(0) < nu_ref[0])
    def _():
        u = xs_ref[...]
        dh = u.shape[1]
        F = wd_ref.shape[1]
        xa = pltpu.bitcast(u << 16, f32).astype(bf16)
        xb = pltpu.bitcast(u & jnp.uint32(0xFFFF0000), f32).astype(bf16)
        acc = None
        fc = F // 2
        for c in range(2):
            def gu(col0):
                return (_mm(xa, wgu_ref[0, :dh, col0:col0 + fc]) + _mm(xb, wgu_ref[0, dh:, col0:col0 + fc])
                        + bgu_ref[0, :, col0:col0 + fc])
            gate = jnp.minimum(gu(c * fc), SWIGLU_LIMIT)
            up = jnp.clip(gu(F + c * fc), -SWIGLU_LIMIT, SWIGLU_LIMIT)
            y = (up + 1.0) * (gate * jax.nn.sigmoid(SWIGLU_ALPHA * gate))
            part = _mm(y.astype(bf16), wd_ref[0, c * fc:(c + 1) * fc, :])
            acc = part if acc is None else acc + part
        ys_ref[...] = acc + bd_ref[0]

    @pl.when(pl.program_id(0) >= nu_ref[0])
    def _():
        ys_ref[...] = jnp.zeros_like(ys_ref)


def _ffn(block_e, n_used, xs, w_gate_up, b_gate_up, w_down, b_down):
    P, dh = xs.shape
    E, D, F2 = w_gate_up.shape
    F = F2 // 2
    nb = P // FFN_BLK

    def blk(i, be, nu):
        return jnp.minimum(i, nu[0] - 1)

    grid_spec = pltpu.PrefetchScalarGridSpec(
        num_scalar_prefetch=2,
        grid=(nb,),
        in_specs=[pl.BlockSpec((FFN_BLK, dh), lambda i, be, nu: (blk(i, be, nu), 0)),
                  pl.BlockSpec((1, D, F2), lambda i, be, nu: (be[blk(i, be, nu)], 0, 0)),
                  pl.BlockSpec((1, 1, F2), lambda i, be, nu: (be[blk(i, be, nu)], 0, 0)),
                  pl.BlockSpec((1, F, D), lambda i, be, nu: (be[blk(i, be, nu)], 0, 0)),
                  pl.BlockSpec((1, 1, D), lambda i, be, nu: (be[blk(i, be, nu)], 0, 0))],
        out_specs=pl.BlockSpec((FFN_BLK, D), lambda i, be, nu: (i, 0)),
    )
    return pl.pallas_call(
        _ffn_kernel,
        out_shape=jax.ShapeDtypeStruct((P, D), f32),
        grid_spec=grid_spec,
        compiler_params=pltpu.CompilerParams(
            dimension_semantics=("arbitrary",), vmem_limit_bytes=VMEM_LIMIT),
        name="ffn",
    )(block_e, n_used, xs, w_gate_up.astype(bf16), b_gate_up.reshape(E, 1, F2),
      w_down.astype(bf16), b_down.reshape(E, 1, D))


def _combine_kernel(dcur_ref, dnext_ref, ys_ref, x1_ref, rw_ref, mod_ref, o_ref, buf, sem):
    step = pl.program_id(0) * pl.num_programs(1) + pl.program_id(1)
    n_steps = pl.num_programs(0) * pl.num_programs(1)
    slot = step % 2

    def issue(dref, s):
        def body(r, c):
            for k in range(TOP_K):
                pltpu.make_async_copy(ys_ref.at[pl.ds(dref[r * TOP_K + k], 1)],
                                      buf.at[s, k, pl.ds(r, 1)], sem.at[s]).start()
            return c
        lax.fori_loop(0, TD, body, 0)

    @pl.when(step == 0)
    def _():
        issue(dcur_ref, 0)

    @pl.when(step + 1 < n_steps)
    def _():
        issue(dnext_ref, 1 - slot)

    def drain(r, c):
        for k in range(TOP_K):
            pltpu.make_async_copy(ys_ref.at[pl.ds(0, 1)], buf.at[slot, k, pl.ds(0, 1)], sem.at[slot]).wait()
        return c
    lax.fori_loop(0, TD, drain, 0)

    rw = rw_ref[...]
    moe = rw[:, 0:1] * buf[slot, 0]
    for k in range(1, TOP_K):
        moe = moe + rw[:, k:k + 1] * buf[slot, k]
    o_ref[0] = x1_ref[0] + mod_ref[0, 5:6, :] * moe


def _combine(dest_flat, ys, x1, rw, mod):
    B, S, D = x1.shape
    nj = S // TD
    n_steps = B * nj
    return pl.pallas_call(
        _combine_kernel,
        out_shape=jax.ShapeDtypeStruct((B, S, D), f32),
        grid=(B, nj),
        in_specs=[pl.BlockSpec((TD * TOP_K,), lambda b, j: (b * nj + j,), memory_space=pltpu.SMEM),
                  pl.BlockSpec((TD * TOP_K,), lambda b, j: (jnp.minimum(b * nj + j + 1, n_steps - 1),),
                               memory_space=pltpu.SMEM),
                  pl.BlockSpec(memory_space=pl.ANY),
                  pl.BlockSpec((1, TD, D), lambda b, j: (b, j, 0)),
                  pl.BlockSpec((TD, LANES), lambda b, j: (b * nj + j, 0)),
                  pl.BlockSpec((1, 6, D), lambda b, j: (b, 0, 0))],
        out_specs=pl.BlockSpec((1, TD, D), lambda b, j: (b, j, 0)),
        scratch_shapes=[pltpu.VMEM((2, TOP_K, TD, D), f32), pltpu.SemaphoreType.DMA((2,))],
        compiler_params=pltpu.CompilerParams(
            dimension_semantics=("arbitrary", "arbitrary"), vmem_limit_bytes=VMEM_LIMIT),
        name="combine",
    )(dest_flat, dest_flat, ys, x1, rw, mod)


def _moe(hp, logits, x1, mod, w_gate_up, b_gate_up, w_down, b_down):
    T = hp.shape[0]
    E = w_gate_up.shape[0]
    ri, rw, cnt = _route(logits)
    rank = ri[:, :TOP_K]
    e_sel = ri[:, TOP_K:2 * TOP_K]
    counts = cnt[0, :E].astype(jnp.int32)
    padded = ((counts + FFN_BLK - 1) // FFN_BLK) * FFN_BLK
    p_ends = jnp.cumsum(padded)
    p_starts = p_ends - padded
    nb = -(-T * TOP_K // FFN_BLK) + E
    n_used = jnp.maximum(p_ends[-1:] // FFN_BLK, 1).astype(jnp.int32)
    block_e = jnp.minimum(jnp.searchsorted(p_ends, jnp.arange(nb, dtype=jnp.int32) * FFN_BLK, side='right'),
                          E - 1).astype(jnp.int32)
    onehot = e_sel[:, :, None] == jnp.arange(E, dtype=jnp.int32)[None, None, :]
    dest = (jnp.sum(jnp.where(onehot, p_starts[None, None, :], 0), axis=-1) + rank).reshape(-1)
    xs = _dispatch(dest, hp, nb * FFN_BLK)
    ys = _ffn(block_e, n_used, xs, w_gate_up, b_gate_up, w_down, b_down)
    return _combine(dest, ys, x1, rw, mod)


def kernel(x, c, rel_bias_table, w_ada, b_ada, g_norm1, w_in, w_gk_up, b_gk_up, g_gla_out, g_qnorm, g_knorm, lambda_q1, lambda_k1, lambda_q2, lambda_k2, g_subln, w_out, g_norm2, w_router, b_router, w_gate_up, b_gate_up, w_down, b_down):
    B, S, D = x.shape
    depth = w_ada.shape[0]
    bias_tiles = _bias_tiles(rel_bias_table, S, min(TQ, S))
    for l in range(depth):
        lambda_init = 0.8 - 0.6 * math.exp(-0.3 * l)
        mod = _ada(c, w_ada[l], b_ada[l])
        qg, kg, gk, kgt, gkt, vg, rg, qd, kd, vd = _inproj(
            x, mod, g_norm1[l], w_in[l], w_gk_up[l], b_gk_up[l], g_qnorm[l], g_knorm[l])
        og = _gla(qg, kg, gk, kgt, gkt, vg, rg, g_gla_out[l])
        lamv = jnp.stack([lambda_q1[l], lambda_k1[l], lambda_q2[l], lambda_k2[l]]).astype(f32)
        od = _attn(qd, kd, vd, bias_tiles, lamv, g_subln[l], lambda_init)
        x1, hp, logits = _outproj(og, od, x, mod, w_out[l], g_norm2[l], w_router[l], b_router[l])
        x = _moe(hp, logits, x1, mod, w_gate_up[l], b_gate_up[l], w_down[l], b_down[l])
    return x
```

```python
import functools
import math

import jax
import jax.numpy as jnp
from jax import lax
from jax.experimental import pallas as pl
from jax.experimental.pallas import tpu as pltpu

f32 = jnp.float32
bf16 = jnp.bfloat16

N_GLA_HEADS = 4
GLA_DK = 64
GLA_DV = 128
GLA_GATE_RANK = 16
GLA_GATE_NORM = 16.0
GLA_CHUNK = 64
N_DIFF_HEADS = 4
DIFF_DQK = 64
DIFF_DV = 128
NUM_BUCKETS = 32
MAX_DISTANCE = 128
TOP_K = 4
SWIGLU_LIMIT = 7.0
SWIGLU_ALPHA = 1.702
EPS = 1e-6

GLA_QK_W = N_GLA_HEADS * GLA_DK
GLA_V_W = N_GLA_HEADS * GLA_DV
DIFF_QK_W = N_DIFF_HEADS * 2 * DIFF_DQK
DIFF_V_W = N_DIFF_HEADS * DIFF_DV

LANES = 128
NEG = -1e30
VMEM_LIMIT = 48 * 1024 * 1024

TM_IN = 512
TG_GLA = 1024
PAIR = 2 * GLA_CHUNK
TQ = 512
TR = 512
TD = 256
DMA_UNROLL = 8
FFN_BLK = 512


def _nt(a, b):
    return lax.dot_general(a, b, (((1,), (1,)), ((), ())), preferred_element_type=f32)


def _mm(a, b):
    return jnp.dot(a, b, preferred_element_type=f32)


def _split(x):
    hi = x.astype(bf16)
    lo = (x - hi.astype(f32)).astype(bf16)
    return hi, lo


def _silu(x):
    return x * jax.nn.sigmoid(x)


def _ada_kernel(c_ref, w_ref, b_ref, o_ref):
    c = c_ref[...]
    o_ref[...] = _mm(_silu(c).astype(bf16), w_ref[...].astype(bf16)) + b_ref[...]


def _ada(c, w_ada, b_ada):
    B, D = c.shape
    N = w_ada.shape[1]
    bp = 8
    cp = jnp.zeros((bp, D), f32).at[:B].set(c)
    tn = 1536
    out = pl.pallas_call(
        _ada_kernel,
        out_shape=jax.ShapeDtypeStruct((bp, N), f32),
        grid=(N // tn,),
        in_specs=[pl.BlockSpec((bp, D), lambda j: (0, 0)),
                  pl.BlockSpec((D, tn), lambda j: (0, j)),
                  pl.BlockSpec((1, tn), lambda j: (0, j))],
        out_specs=pl.BlockSpec((bp, tn), lambda j: (0, j)),
        compiler_params=pltpu.CompilerParams(vmem_limit_bytes=VMEM_LIMIT),
        name="ada",
    )(cp, w_ada, b_ada.reshape(1, N))
    return out[:B].reshape(B, 6, D)


def _inproj_kernel(x_ref, mod_ref, g1_ref, wm_ref, wkt_ref, wlo_ref, wup_ref, wupt_ref,
                   bup_ref, bupt_ref, gqk_ref, grp_ref, grpt_ref,
                   qg_ref, kg_ref, gk_ref, kgt_ref, gkt_ref, vg_ref, rg_ref,
                   qd_ref, kd_ref, vd_ref):
    x = x_ref[0]
    ms = jnp.mean(x * x, axis=-1, keepdims=True)
    y = x * lax.rsqrt(ms + EPS) * g1_ref[...]
    h = (y * (1.0 + mod_ref[0, 1:2, :]) + mod_ref[0, 0:1, :]).astype(bf16)

    def proj(a, b):
        return _mm(h, wm_ref[:, a:b])

    o = 0
    qg_ref[0] = proj(o, o + GLA_QK_W); o += GLA_QK_W
    kg_ref[0] = proj(o, o + GLA_QK_W); o += GLA_QK_W
    vg_ref[0] = proj(o, o + GLA_V_W).astype(bf16); o += GLA_V_W
    rg_ref[0] = proj(o, o + GLA_V_W); o += GLA_V_W
    qk = proj(o, o + 2 * DIFF_QK_W); o += 2 * DIFF_QK_W
    vd_ref[0] = proj(o, o + DIFF_V_W).astype(bf16)

    kgt = _nt(wkt_ref[...], h)
    for j in range(kgt.shape[1] // PAIR):
        kgt_ref[0, j] = kgt[:, j * PAIR:(j + 1) * PAIR]

    lo = _mm(h, wlo_ref[...]).astype(bf16)
    z = _mm(lo, wup_ref[...]) + bup_ref[...]
    gk_ref[0] = (jnp.minimum(z, 0.0) - jnp.log1p(jnp.exp(-jnp.abs(z)))) * (1.0 / GLA_GATE_NORM)
    zt = _nt(wupt_ref[...], lo) + bupt_ref[...]
    gkt = (jnp.minimum(zt, 0.0) - jnp.log1p(jnp.exp(-jnp.abs(zt)))) * (1.0 / GLA_GATE_NORM)
    for j in range(gkt.shape[1] // PAIR):
        gkt_ref[0, j] = gkt[:, j * PAIR:(j + 1) * PAIR]

    sq_hi, sq_lo = _split(qk * qk)
    gs = _mm(sq_hi, grp_ref[...]) + _mm(sq_lo, grp_ref[...])
    r = lax.rsqrt(gs * (1.0 / DIFF_DQK) + EPS)
    r_hi, r_lo = _split(r)
    rb = _mm(r_hi, grpt_ref[...]) + _mm(r_lo, grpt_ref[...])
    qkn = qk * rb * gqk_ref[...]
    qd_ref[0] = qkn[:, :DIFF_QK_W].astype(bf16)
    kd_ref[0] = qkn[:, DIFF_QK_W:].astype(bf16)


def _inproj(x, mod, g_norm1, w_in, w_gk_up, b_gk_up, g_qnorm, g_knorm):
    B, S, D = x.shape
    offs = [0]
    for w in (GLA_QK_W, GLA_QK_W, GLA_V_W, GLA_V_W, GLA_GATE_RANK, DIFF_QK_W, DIFF_QK_W, DIFF_V_W):
        offs.append(offs[-1] + w)
    w_main = jnp.concatenate([w_in[:, offs[0]:offs[4]], w_in[:, offs[5]:offs[8]]], axis=1).astype(bf16)
    w_kt = w_in[:, offs[1]:offs[2]].T.astype(bf16)
    w_lo = jnp.zeros((D, LANES), f32).at[:, :GLA_GATE_RANK].set(w_in[:, offs[4]:offs[5]]).astype(bf16)
    w_up = jnp.zeros((LANES, GLA_QK_W), f32).at[:GLA_GATE_RANK].set(w_gk_up).astype(bf16)
    w_upt = w_up.T
    b_up = b_gk_up.reshape(1, GLA_QK_W)
    b_upt = b_gk_up.reshape(GLA_QK_W, 1)
    n_grp = 2 * DIFF_QK_W // DIFF_DQK
    gqk = jnp.concatenate([jnp.tile(g_qnorm, n_grp // 2) * (DIFF_DQK ** -0.5),
                           jnp.tile(g_knorm, n_grp // 2)]).reshape(1, 2 * DIFF_QK_W)
    grp = (jnp.arange(2 * DIFF_QK_W)[:, None] // DIFF_DQK == jnp.arange(LANES)[None, :]).astype(bf16)
    grpt = grp.T
    nw = w_main.shape[1]
    tm = TM_IN
    const = lambda shape: pl.BlockSpec(shape, lambda b, i: (0,) * len(shape))
    row = lambda w: pl.BlockSpec((1, tm, w), lambda b, i: (b, i, 0))
    colT = pl.BlockSpec((1, tm // PAIR, GLA_QK_W, PAIR), lambda b, i: (b, i, 0, 0))
    outs = pl.pallas_call(
        _inproj_kernel,
        out_shape=[jax.ShapeDtypeStruct((B, S, GLA_QK_W), f32),
                   jax.ShapeDtypeStruct((B, S, GLA_QK_W), f32),
                   jax.ShapeDtypeStruct((B, S, GLA_QK_W), f32),
                   jax.ShapeDtypeStruct((B, S // PAIR, GLA_QK_W, PAIR), f32),
                   jax.ShapeDtypeStruct((B, S // PAIR, GLA_QK_W, PAIR), f32),
                   jax.ShapeDtypeStruct((B, S, GLA_V_W), bf16),
                   jax.ShapeDtypeStruct((B, S, GLA_V_W), f32),
                   jax.ShapeDtypeStruct((B, S, DIFF_QK_W), bf16),
                   jax.ShapeDtypeStruct((B, S, DIFF_QK_W), bf16),
                   jax.ShapeDtypeStruct((B, S, DIFF_V_W), bf16)],
        grid=(B, S // tm),
        in_specs=[row(D),
                  pl.BlockSpec((1, 6, D), lambda b, i: (b, 0, 0)),
                  const((1, D)), const((D, nw)), const((GLA_QK_W, D)), const((D, LANES)),
                  const((LANES, GLA_QK_W)), const((GLA_QK_W, LANES)),
                  const((1, GLA_QK_W)), const((GLA_QK_W, 1)),
                  const((1, 2 * DIFF_QK_W)), const((2 * DIFF_QK_W, LANES)),
                  const((LANES, 2 * DIFF_QK_W))],
        out_specs=[row(GLA_QK_W), row(GLA_QK_W), row(GLA_QK_W), colT, colT,
                   row(GLA_V_W), row(GLA_V_W), row(DIFF_QK_W), row(DIFF_QK_W), row(DIFF_V_W)],
        compiler_params=pltpu.CompilerParams(
            dimension_semantics=("arbitrary", "arbitrary"), vmem_limit_bytes=VMEM_LIMIT),
        name="inproj",
    )(x, mod, g_norm1.reshape(1, D), w_main, w_kt, w_lo, w_up, w_upt, b_up, b_upt, gqk, grp, grpt)
    return outs


def _gla_kernel(q_ref, k_ref, g_ref, kt_ref, gt_ref, v_ref, r_ref, gout_ref, tri_ref, trit_ref,
                o_ref, s_ref, *, n_pairs):
    H, DK, DV = N_GLA_HEADS, GLA_DK, GLA_DV

    @pl.when(pl.program_id(1) == 0)
    def _():
        s_ref[...] = jnp.zeros_like(s_ref)

    tri = tri_ref[...]
    trit = trit_ref[...]
    tri_b = tri > 0
    lane_head = lax.broadcasted_iota(jnp.int32, (1, H * DK), 1) // DK
    row_head = lax.broadcasted_iota(jnp.int32, (H * PAIR, 1), 0) // PAIR
    qmask = row_head == lane_head
    row_first = lax.broadcasted_iota(jnp.int32, (PAIR, 1), 0) < GLA_CHUNK
    row_first4 = (lax.broadcasted_iota(jnp.int32, (H * PAIR, 1), 0) % PAIR) < GLA_CHUNK
    lane_first = lax.broadcasted_iota(jnp.int32, (1, PAIR), 1) < GLA_CHUNK
    scale = DK ** -0.5
    gout = gout_ref[...]

    def pair(p, carry):
        r0 = pl.multiple_of(p * PAIR, PAIR)
        q = q_ref[0, pl.ds(r0, PAIR), :]
        k = k_ref[0, pl.ds(r0, PAIR), :]
        g = g_ref[0, pl.ds(r0, PAIR), :]
        kt = kt_ref[0, p]
        gt = gt_ref[0, p]
        v = v_ref[0, pl.ds(r0, PAIR), :]

        g_hi, g_lo = _split(g)
        gc = _mm(tri, g_hi) + _mm(tri, g_lo)
        gt_hi, gt_lo = _split(gt)
        gct = _mm(gt_hi, trit) + _mm(gt_lo, trit)
        g_last = jnp.where(row_first, gc[GLA_CHUNK - 1:GLA_CHUNK, :], gc[PAIR - 1:PAIR, :])
        gl0 = gct[:, GLA_CHUNK - 1:GLA_CHUNK]
        gl1 = gct[:, PAIR - 1:PAIR]
        g_last_t = jnp.where(lane_first, gl0, gl1)

        q_e = (q * (jnp.exp(gc) * scale)).astype(bf16)
        k_e = (k * jnp.exp(-gc)).astype(bf16)
        ks_t = kt * jnp.exp(g_last_t - gct)
        ks_t0 = jnp.where(lane_first, ks_t, 0.0).astype(bf16)
        ks_t1 = jnp.where(lane_first, 0.0, ks_t).astype(bf16)
        del g_last

        qm = jnp.where(qmask, jnp.concatenate([q_e] * H, axis=0), jnp.zeros((), bf16))
        a = _nt(qm, k_e)
        s0 = s_ref[...]

        u0 = []
        u1 = []
        for h in range(H):
            v_h = v[:, h * DV:(h + 1) * DV]
            u0.append(_mm(ks_t0[h * DK:(h + 1) * DK], v_h))
            u1.append(_mm(ks_t1[h * DK:(h + 1) * DK], v_h))
        u0 = jnp.concatenate(u0, axis=0)
        u1 = jnp.concatenate(u1, axis=0)
        s1 = s0 * jnp.exp(gl0) + u0
        s_ref[...] = s1 * jnp.exp(gl1) + u1

        o_inter = jnp.where(row_first4, _mm(qm, s0.astype(bf16)), _mm(qm, s1.astype(bf16)))
        for h in range(H):
            a_h = jnp.where(tri_b, a[h * PAIR:(h + 1) * PAIR], 0.0).astype(bf16)
            o_h = _mm(a_h, v[:, h * DV:(h + 1) * DV]) + o_inter[h * PAIR:(h + 1) * PAIR]
            ms = jnp.mean(o_h * o_h, axis=-1, keepdims=True)
            o_n = o_h * lax.rsqrt(ms + EPS) * gout
            r_h = r_ref[0, pl.ds(r0, PAIR), h * DV:(h + 1) * DV]
            o_ref[0, pl.ds(r0, PAIR), h * DV:(h + 1) * DV] = (o_n * _silu(r_h)).astype(bf16)
        return carry

    lax.fori_loop(0, n_pairs, pair, 0)


def _gla(qg, kg, gk, kgt, gkt, vg, rg, g_gla_out):
    B, S, _ = qg.shape
    tg = min(TG_GLA, S)
    r = jnp.arange(PAIR)
    tri = ((r[:, None] // GLA_CHUNK == r[None, :] // GLA_CHUNK) & (r[None, :] <= r[:, None])).astype(bf16)
    row = lambda w: pl.BlockSpec((1, tg, w), lambda b, i: (b, i, 0))
    colT = pl.BlockSpec((1, tg // PAIR, GLA_QK_W, PAIR), lambda b, i: (b, i, 0, 0))
    const = lambda shape: pl.BlockSpec(shape, lambda b, i: (0,) * len(shape))
    return pl.pallas_call(
        functools.partial(_gla_kernel, n_pairs=tg // PAIR),
        out_shape=jax.ShapeDtypeStruct((B, S, GLA_V_W), bf16),
        grid=(B, S // tg),
        in_specs=[row(GLA_QK_W), row(GLA_QK_W), row(GLA_QK_W), colT, colT,
                  row(GLA_V_W), row(GLA_V_W), const((1, GLA_DV)),
                  const((PAIR, PAIR)), const((PAIR, PAIR))],
        out_specs=row(GLA_V_W),
        scratch_shapes=[pltpu.VMEM((GLA_QK_W, GLA_DV), f32)],
        compiler_params=pltpu.CompilerParams(
            dimension_semantics=("arbitrary", "arbitrary"), vmem_limit_bytes=VMEM_LIMIT),
        name="gla",
    )(qg, kg, gk, kgt, gkt, vg, rg, g_gla_out.reshape(1, GLA_DV), tri, tri.T)


def _attn_kernel(q_ref, k_ref, v_ref, bias_ref, lamv_ref, gsub_ref, o_ref, *, lambda_init):
    qi = pl.program_id(2)
    tq = q_ref.shape[1]
    q = q_ref[0]
    lane = lax.broadcasted_iota(jnp.int32, (1, 2 * DIFF_DQK), 1)
    zero = jnp.zeros((), bf16)
    qs = (jnp.where(lane < DIFF_DQK, q, zero), jnp.where(lane < DIFF_DQK, zero, q))

    def update(state, kb, vb, bias):
        new = []
        for c in range(2):
            m, l, acc = state[c]
            s = _nt(qs[c], kb)
            if bias is not None:
                s = s + bias[c]
            m_new = jnp.maximum(m, jnp.max(s, axis=-1, keepdims=True))
            alpha = jnp.exp(m - m_new)
            p = jnp.exp(s - m_new)
            l = alpha * l + jnp.sum(p, axis=-1, keepdims=True)
            acc = alpha * acc + _mm(p.astype(bf16), vb)
            new.append((m_new, l, acc))
        return tuple(new)

    init = tuple((jnp.full((tq, 1), NEG, f32), jnp.zeros((tq, 1), f32), jnp.zeros((tq, DIFF_DV), f32))
                 for _ in range(2))

    def far(kj, state):
        k0 = pl.multiple_of(kj * tq, tq)
        return update(state, k_ref[0, pl.ds(k0, tq), :], v_ref[0, pl.ds(k0, tq), :], None)

    state = lax.fori_loop(0, jnp.maximum(qi - 1, 0), far, init)

    kd0 = pl.multiple_of(qi * tq, tq)
    state = update(state, k_ref[0, pl.ds(kd0, tq), :], v_ref[0, pl.ds(kd0, tq), :],
                   (bias_ref[0, 0, 1], bias_ref[0, 1, 1]))
    kp0 = pl.multiple_of(jnp.maximum(qi - 1, 0) * tq, tq)
    has_prev = qi > 0
    state = update(state, k_ref[0, pl.ds(kp0, tq), :], v_ref[0, pl.ds(kp0, tq), :],
                   (jnp.where(has_prev, bias_ref[0, 0, 0], NEG), jnp.where(has_prev, bias_ref[0, 1, 0], NEG)))

    lv = lamv_ref[...]
    lam = (jnp.exp(jnp.sum(lv[0:1] * lv[1:2], axis=-1, keepdims=True))
           - jnp.exp(jnp.sum(lv[2:3] * lv[3:4], axis=-1, keepdims=True)) + lambda_init)
    (_, l0, a0), (_, l1, a1) = state
    o = a0 / l0 - lam * (a1 / l1)
    ms = jnp.mean(o * o, axis=-1, keepdims=True)
    o_ref[0] = (o * lax.rsqrt(ms + EPS) * gsub_ref[...] * (1.0 - lambda_init)).astype(bf16)


def _t5_bucket(n):
    max_exact = NUM_BUCKETS // 2
    nf = jnp.maximum(n, 1).astype(f32)
    large = max_exact + (jnp.log(nf / max_exact) / math.log(MAX_DISTANCE / max_exact)
                         * (NUM_BUCKETS - max_exact)).astype(jnp.int32)
    large = jnp.minimum(large, NUM_BUCKETS - 1)
    return jnp.where(n < max_exact, n, large)


def _toeplitz(vec, n):
    w = jnp.concatenate([vec[:, ::-1], jnp.zeros((vec.shape[0], 1), vec.dtype)], axis=1)
    flat = jnp.tile(w, (1, n))[:, :n * (2 * n - 1)].reshape(-1, n, 2 * n - 1)
    return flat[:, :, n - 1:]


def _bias_tiles(rel_bias_table, S, n):
    HM = rel_bias_table.shape[1]
    assert n >= MAX_DISTANCE
    d = jnp.arange(2 * n, dtype=jnp.int32)
    by_dist = rel_bias_table[_t5_bucket(d)].astype(f32).T
    rel = by_dist - rel_bias_table[NUM_BUCKETS - 1].astype(f32)[:, None]
    diag = _toeplitz(jnp.concatenate([jnp.full((HM, n - 1), NEG, f32), rel[:, :n]], axis=1), n)
    prev = _toeplitz(rel[:, 1:2 * n], n)
    return jnp.stack([prev, diag], axis=1).reshape(HM // 2, 2, 2, n, n)


def _attn(qd, kd, vd, bias_tiles, lamv, g_subln, lambda_init):
    B, S, _ = qd.shape
    H = N_DIFF_HEADS
    tq = min(TQ, S)
    return pl.pallas_call(
        functools.partial(_attn_kernel, lambda_init=lambda_init),
        out_shape=jax.ShapeDtypeStruct((B, S, DIFF_V_W), bf16),
        grid=(B, H, S // tq),
        in_specs=[pl.BlockSpec((1, tq, 2 * DIFF_DQK), lambda b, h, i: (b, i, h)),
                  pl.BlockSpec((1, S, 2 * DIFF_DQK), lambda b, h, i: (b, 0, h)),
                  pl.BlockSpec((1, S, DIFF_DV), lambda b, h, i: (b, 0, h)),
                  pl.BlockSpec((1, 2, 2, tq, tq), lambda b, h, i: (h, 0, 0, 0, 0)),
                  pl.BlockSpec((4, DIFF_DQK), lambda b, h, i: (0, 0)),
                  pl.BlockSpec((1, DIFF_DV), lambda b, h, i: (0, 0))],
        out_specs=pl.BlockSpec((1, tq, DIFF_DV), lambda b, h, i: (b, i, h)),
        compiler_params=pltpu.CompilerParams(
            dimension_semantics=("arbitrary", "arbitrary", "arbitrary"), vmem_limit_bytes=VMEM_LIMIT),
        name="attn",
    )(qd, kd, vd, bias_tiles, lamv, g_subln.reshape(1, DIFF_DV))


def _outproj_kernel(og_ref, od_ref, x_ref, mod_ref, wo_ref, g2_ref, wr_ref, br_ref,
                    x1_ref, hp_ref, lg_ref):
    half = og_ref.shape[2]
    mix = _mm(og_ref[0], wo_ref[:half, :]) + _mm(od_ref[0], wo_ref[half:, :])
    x1 = x_ref[0] + mod_ref[0, 2:3, :] * mix
    x1_ref[0] = x1
    ms = jnp.mean(x1 * x1, axis=-1, keepdims=True)
    y = x1 * lax.rsqrt(ms + EPS) * g2_ref[...]
    h = (y * (1.0 + mod_ref[0, 4:5, :]) + mod_ref[0, 3:4, :]).astype(bf16)
    lg_ref[...] = _mm(h, wr_ref[...]) + br_ref[...]
    hf = h.astype(f32)
    dh = hf.shape[1] // 2
    lo_bits = pltpu.bitcast(hf[:, :dh], jnp.uint32) >> 16
    hi_bits = pltpu.bitcast(hf[:, dh:], jnp.uint32) & jnp.uint32(0xFFFF0000)
    hp_ref[...] = lo_bits | hi_bits


def _outproj(og, od, x, mod, w_out, g_norm2, w_router, b_router):
    B, S, D = x.shape
    E = w_router.shape[1]
    tm = TM_IN
    nj = S // tm
    w_r = jnp.zeros((D, LANES), f32).at[:, :E].set(w_router).astype(bf16)
    b_r = jnp.full((1, LANES), NEG, f32).at[0, :E].set(b_router)
    const = lambda shape: pl.BlockSpec(shape, lambda b, i: (0,) * len(shape))
    return pl.pallas_call(
        _outproj_kernel,
        out_shape=[jax.ShapeDtypeStruct((B, S, D), f32),
                   jax.ShapeDtypeStruct((B * S, D // 2), jnp.uint32),
                   jax.ShapeDtypeStruct((B * S, LANES), f32)],
        grid=(B, nj),
        in_specs=[pl.BlockSpec((1, tm, og.shape[2]), lambda b, i: (b, i, 0)),
                  pl.BlockSpec((1, tm, od.shape[2]), lambda b, i: (b, i, 0)),
                  pl.BlockSpec((1, tm, D), lambda b, i: (b, i, 0)),
                  pl.BlockSpec((1, 6, D), lambda b, i: (b, 0, 0)),
                  const((w_out.shape[0], D)), const((1, D)), const((D, LANES)), const((1, LANES))],
        out_specs=[pl.BlockSpec((1, tm, D), lambda b, i: (b, i, 0)),
                   pl.BlockSpec((tm, D // 2), lambda b, i: (b * nj + i, 0)),
                   pl.BlockSpec((tm, LANES), lambda b, i: (b * nj + i, 0))],
        compiler_params=pltpu.CompilerParams(
            dimension_semantics=("arbitrary", "arbitrary"), vmem_limit_bytes=VMEM_LIMIT),
        name="outproj",
    )(og, od, x, mod, w_out.astype(bf16), g_norm2.reshape(1, D), w_r, b_r)


def _route_kernel(lg_ref, lt_ref, ri_ref, rw_ref, cnt_ref, run_ref):
    @pl.when(pl.program_id(0) == 0)
    def _():
        run_ref[...] = jnp.zeros_like(run_ref)

    x = lg_ref[...]
    tr = x.shape[0]
    lane = lax.broadcasted_iota(jnp.int32, (tr, LANES), 1)
    lane_f = lane.astype(f32)
    vals, hots, idxs = [], [], []
    for _ in range(TOP_K):
        m = jnp.max(x, axis=-1, keepdims=True)
        idx = jnp.min(jnp.where(x == m, lane_f, float(LANES)), axis=-1, keepdims=True)
        hot = lane_f == idx
        x = jnp.where(hot, -jnp.inf, x)
        vals.append(m)
        hots.append(hot)
        idxs.append(idx.astype(jnp.int32))
    ex = [jnp.exp(v - vals[0]) for v in vals]
    den = ex[0] + ex[1] + ex[2] + ex[3]
    sel = (hots[0] | hots[1] | hots[2] | hots[3]).astype(f32)
    rank = _mm(lt_ref[...], sel.astype(bf16)) + run_ref[...]
    run_ref[...] = run_ref[...] + jnp.sum(sel, axis=0, keepdims=True)
    cnt_ref[...] = run_ref[...]
    ri = jnp.zeros((tr, LANES), jnp.int32)
    rw = jnp.zeros((tr, LANES), f32)
    for k in range(TOP_K):
        rk = jnp.sum(jnp.where(hots[k], rank, 0.0), axis=-1, keepdims=True).astype(jnp.int32)
        ri = jnp.where(lane == k, rk, ri)
        ri = jnp.where(lane == TOP_K + k, idxs[k], ri)
        rw = jnp.where(lane == k, ex[k] / den, rw)
    ri_ref[...] = ri
    rw_ref[...] = rw


def _route(logits):
    T = logits.shape[0]
    tr = min(TR, T)
    r = jnp.arange(tr)
    lt = (r[None, :] < r[:, None]).astype(bf16)
    return pl.pallas_call(
        _route_kernel,
        out_shape=[jax.ShapeDtypeStruct((T, LANES), jnp.int32),
                   jax.ShapeDtypeStruct((T, LANES), f32),
                   jax.ShapeDtypeStruct((1, LANES), f32)],
        grid=(T // tr,),
        in_specs=[pl.BlockSpec((tr, LANES), lambda i: (i, 0)),
                  pl.BlockSpec((tr, tr), lambda i: (0, 0))],
        out_specs=[pl.BlockSpec((tr, LANES), lambda i: (i, 0)),
                   pl.BlockSpec((tr, LANES), lambda i: (i, 0)),
                   pl.BlockSpec((1, LANES), lambda i: (0, 0))],
        scratch_shapes=[pltpu.VMEM((1, LANES), f32)],
        compiler_params=pltpu.CompilerParams(dimension_semantics=("arbitrary",)),
        name="route",
    )(logits, lt)


def _row_copy(src, s, dst, d, sem):
    return pltpu.make_async_copy(src.at[pl.ds(s, 1)], dst.at[pl.ds(d, 1)], sem)


def _dispatch_kernel(dest_ref, h_ref, xs_in_ref, xs_ref, sem):
    del xs_in_ref
    n_tok = h_ref.shape[0]

    def issue(g, c):
        for u in range(DMA_UNROLL):
            r = g * DMA_UNROLL + u
            for k in range(TOP_K):
                _row_copy(h_ref, r, xs_ref, dest_ref[r * TOP_K + k], sem).start(priority=k % 2)
        return c

    lax.fori_loop(0, n_tok // DMA_UNROLL, issue, 0)
    done = xs_ref.at[pl.ds(0, n_tok * TOP_K)]
    pltpu.make_async_copy(done, done, sem).wait()


def _dispatch(dest_flat, hp, n_rows):
    T, W = hp.shape
    xs0 = jnp.zeros((n_rows, W), hp.dtype)
    return pl.pallas_call(
        _dispatch_kernel,
        out_shape=jax.ShapeDtypeStruct((n_rows, W), hp.dtype),
        grid=(T // TD,),
        in_specs=[pl.BlockSpec((TD * TOP_K,), lambda i: (i,), memory_space=pltpu.SMEM),
                  pl.BlockSpec((TD, W), lambda i: (i, 0)),
                  pl.BlockSpec(memory_space=pl.ANY)],
        out_specs=pl.BlockSpec(memory_space=pl.ANY),
        scratch_shapes=[pltpu.SemaphoreType.DMA(())],
        input_output_aliases={2: 0},
        compiler_params=pltpu.CompilerParams(dimension_semantics=("arbitrary",)),
        name="dispatch",
    )(dest_flat, hp, xs0)


def _ffn_kernel(be_ref, nu_ref, xs_ref, wgu_ref, bgu_ref, wd_ref, bd_ref, ys_ref):
    del be_ref

    @pl.when(pl.program_id(0) < nu_ref[0])
    def _():
        u = xs_ref[...]
        dh = u.shape[1]
        F = wd_ref.shape[1]
        xa = pltpu.bitcast(u << 16, f32).astype(bf16)
        xb = pltpu.bitcast(u & jnp.uint32(0xFFFF0000), f32).astype(bf16)
        acc = None
        fc = F // 2
        for c in range(2):
            def gu(col0):
                return (_mm(xa, wgu_ref[0, :dh, col0:col0 + fc]) + _mm(xb, wgu_ref[0, dh:, col0:col0 + fc])
                        + bgu_ref[0, :, col0:col0 + fc])
            gate = jnp.minimum(gu(c * fc), SWIGLU_LIMIT)
            up = jnp.clip(gu(F + c * fc), -SWIGLU_LIMIT, SWIGLU_LIMIT)
            y = (up + 1.0) * (gate * jax.nn.sigmoid(SWIGLU_ALPHA * gate))
            part = _mm(y.astype(bf16), wd_ref[0, c * fc:(c + 1) * fc, :])
            acc = part if acc is None else acc + part
        ys_ref[...] = acc + bd_ref[0]

    @pl.when(pl.program_id(0) >= nu_ref[0])
    def _():
        ys_ref[...] = jnp.zeros_like(ys_ref)


def _ffn(block_e, n_used, xs, w_gate_up, b_gate_up, w_down, b_down):
    P, dh = xs.shape
    E, D, F2 = w_gate_up.shape
    F = F2 // 2
    nb = P // FFN_BLK

    def blk(i, be, nu):
        return jnp.minimum(i, nu[0] - 1)

    grid_spec = pltpu.PrefetchScalarGridSpec(
        num_scalar_prefetch=2,
        grid=(nb,),
        in_specs=[pl.BlockSpec((FFN_BLK, dh), lambda i, be, nu: (blk(i, be, nu), 0)),
                  pl.BlockSpec((1, D, F2), lambda i, be, nu: (be[blk(i, be, nu)], 0, 0)),
                  pl.BlockSpec((1, 1, F2), lambda i, be, nu: (be[blk(i, be, nu)], 0, 0)),
                  pl.BlockSpec((1, F, D), lambda i, be, nu: (be[blk(i, be, nu)], 0, 0)),
                  pl.BlockSpec((1, 1, D), lambda i, be, nu: (be[blk(i, be, nu)], 0, 0))],
        out_specs=pl.BlockSpec((FFN_BLK, D), lambda i, be, nu: (i, 0)),
    )
    return pl.pallas_call(
        _ffn_kernel,
        out_shape=jax.ShapeDtypeStruct((P, D), f32),
        grid_spec=grid_spec,
        compiler_params=pltpu.CompilerParams(
            dimension_semantics=("arbitrary",), vmem_limit_bytes=VMEM_LIMIT),
        name="ffn",
    )(block_e, n_used, xs, w_gate_up.astype(bf16), b_gate_up.reshape(E, 1, F2),
      w_down.astype(bf16), b_down.reshape(E, 1, D))


def _combine_kernel(dcur_ref, dnext_ref, ys_ref, x1_ref, rw_ref, mod_ref, o_ref, buf, sem):
    step = pl.program_id(0) * pl.num_programs(1) + pl.program_id(1)
    n_steps = pl.num_programs(0) * pl.num_programs(1)
    slot = step % 2

    def issue(dref, s):
        def body(g, c):
            for u in range(DMA_UNROLL):
                r = g * DMA_UNROLL + u
                for k in range(TOP_K):
                    pltpu.make_async_copy(ys_ref.at[pl.ds(dref[r * TOP_K + k], 1)],
                                          buf.at[s, k, pl.ds(r, 1)], sem.at[s]).start(priority=k % 2)
            return c
        lax.fori_loop(0, TD // DMA_UNROLL, body, 0)

    @pl.when(step == 0)
    def _():
        issue(dcur_ref, 0)

    @pl.when(step + 1 < n_steps)
    def _():
        issue(dnext_ref, 1 - slot)

    pltpu.make_async_copy(buf.at[slot], buf.at[slot], sem.at[slot]).wait()

    rw = rw_ref[...]
    moe = rw[:, 0:1] * buf[slot, 0]
    for k in range(1, TOP_K):
        moe = moe + rw[:, k:k + 1] * buf[slot, k]
    o_ref[0] = x1_ref[0] + mod_ref[0, 5:6, :] * moe


def _combine(dest_flat, ys, x1, rw, mod):
    B, S, D = x1.shape
    nj = S // TD
    n_steps = B * nj
    return pl.pallas_call(
        _combine_kernel,
        out_shape=jax.ShapeDtypeStruct((B, S, D), f32),
        grid=(B, nj),
        in_specs=[pl.BlockSpec((TD * TOP_K,), lambda b, j: (b * nj + j,), memory_space=pltpu.SMEM),
                  pl.BlockSpec((TD * TOP_K,), lambda b, j: (jnp.minimum(b * nj + j + 1, n_steps - 1),),
                               memory_space=pltpu.SMEM),
                  pl.BlockSpec(memory_space=pl.ANY),
                  pl.BlockSpec((1, TD, D), lambda b, j: (b, j, 0)),
                  pl.BlockSpec((TD, LANES), lambda b, j: (b * nj + j, 0)),
                  pl.BlockSpec((1, 6, D), lambda b, j: (b, 0, 0))],
        out_specs=pl.BlockSpec((1, TD, D), lambda b, j: (b, j, 0)),
        scratch_shapes=[pltpu.VMEM((2, TOP_K, TD, D), f32), pltpu.SemaphoreType.DMA((2,))],
        compiler_params=pltpu.CompilerParams(
            dimension_semantics=("arbitrary", "arbitrary"), vmem_limit_bytes=VMEM_LIMIT),
        name="combine",
    )(dest_flat, dest_flat, ys, x1, rw, mod)


def _moe(hp, logits, x1, mod, w_gate_up, b_gate_up, w_down, b_down):
    T = hp.shape[0]
    E = w_gate_up.shape[0]
    ri, rw, cnt = _route(logits)
    rank = ri[:, :TOP_K]
    e_sel = ri[:, TOP_K:2 * TOP_K]
    counts = cnt[0, :E].astype(jnp.int32)
    padded = ((counts + FFN_BLK - 1) // FFN_BLK) * FFN_BLK
    p_ends = jnp.cumsum(padded)
    p_starts = p_ends - padded
    nb = -(-T * TOP_K // FFN_BLK) + E
    n_used = jnp.maximum(p_ends[-1:] // FFN_BLK, 1).astype(jnp.int32)
    blk_start = jnp.arange(nb, dtype=jnp.int32) * FFN_BLK
    block_e = jnp.minimum(jnp.sum(p_ends[None, :] <= blk_start[:, None], axis=1), E - 1).astype(jnp.int32)
    onehot = e_sel[:, :, None] == jnp.arange(E, dtype=jnp.int32)[None, None, :]
    dest = (jnp.sum(jnp.where(onehot, p_starts[None, None, :], 0), axis=-1) + rank).reshape(-1)
    xs = _dispatch(dest, hp, nb * FFN_BLK)
    ys = _ffn(block_e, n_used, xs, w_gate_up, b_gate_up, w_down, b_down)
    return _combine(dest, ys, x1, rw, mod)


def kernel(x, c, rel_bias_table, w_ada, b_ada, g_norm1, w_in, w_gk_up, b_gk_up, g_gla_out, g_qnorm, g_knorm, lambda_q1, lambda_k1, lambda_q2, lambda_k2, g_subln, w_out, g_norm2, w_router, b_router, w_gate_up, b_gate_up, w_down, b_down):
    B, S, D = x.shape
    depth = w_ada.shape[0]
    bias_tiles = _bias_tiles(rel_bias_table, S, min(TQ, S))
    for l in range(depth):
        lambda_init = 0.8 - 0.6 * math.exp(-0.3 * l)
        mod = _ada(c, w_ada[l], b_ada[l])
        qg, kg, gk, kgt, gkt, vg, rg, qd, kd, vd = _inproj(
            x, mod, g_norm1[l], w_in[l], w_gk_up[l], b_gk_up[l], g_qnorm[l], g_knorm[l])
        og = _gla(qg, kg, gk, kgt, gkt, vg, rg, g_gla_out[l])
        lamv = jnp.stack([lambda_q1[l], lambda_k1[l], lambda_q2[l], lambda_k2[l]]).astype(f32)
        od = _attn(qd, kd, vd, bias_tiles, lamv, g_subln[l], lambda_init)
        x1, hp, logits = _outproj(og, od, x, mod, w_out[l], g_norm2[l], w_router[l], b_router[l])
        x = _moe(hp, logits, x1, mod, w_gate_up[l], b_gate_up[l], w_down[l], b_down[l])
    return x
```

```python
import functools
import math

import jax
import jax.numpy as jnp
from jax import lax
from jax.experimental import pallas as pl
from jax.experimental.pallas import tpu as pltpu

f32 = jnp.float32
bf16 = jnp.bfloat16

N_GLA_HEADS = 4
GLA_DK = 64
GLA_DV = 128
GLA_GATE_RANK = 16
GLA_GATE_NORM = 16.0
GLA_CHUNK = 64
N_DIFF_HEADS = 4
DIFF_DQK = 64
DIFF_DV = 128
NUM_BUCKETS = 32
MAX_DISTANCE = 128
TOP_K = 4
SWIGLU_LIMIT = 7.0
SWIGLU_ALPHA = 1.702
EPS = 1e-6

GLA_QK_W = N_GLA_HEADS * GLA_DK
GLA_V_W = N_GLA_HEADS * GLA_DV
DIFF_QK_W = N_DIFF_HEADS * 2 * DIFF_DQK
DIFF_V_W = N_DIFF_HEADS * DIFF_DV

LANES = 128
NEG = -1e30
LOG2E = math.log2(math.e)
SAFE_SCORE = 40.0
VMEM_LIMIT = 48 * 1024 * 1024
VMEM_LIMIT_FFN = 58 * 1024 * 1024

TM_IN = 512
TG_GLA = 1024
PAIR = 2 * GLA_CHUNK
TQ = 512
ATTN_UNROLL = 4
TR = 512
TD = 256
DMA_UNROLL = 8
FFN_BLK = 512


def _nt(a, b):
    return lax.dot_general(a, b, (((1,), (1,)), ((), ())), preferred_element_type=f32)


def _mm(a, b):
    return jnp.dot(a, b, preferred_element_type=f32)


def _split(x):
    hi = x.astype(bf16)
    lo = (x - hi.astype(f32)).astype(bf16)
    return hi, lo


def _silu(x):
    return x * jax.nn.sigmoid(x)


def _ada_kernel(c_ref, w_ref, b_ref, o_ref):
    c = c_ref[...]
    o_ref[...] = _mm(_silu(c).astype(bf16), w_ref[...].astype(bf16)) + b_ref[...]


def _ada(c, w_ada, b_ada):
    B, D = c.shape
    N = w_ada.shape[1]
    bp = 8
    cp = jnp.zeros((bp, D), f32).at[:B].set(c)
    tn = 1536
    out = pl.pallas_call(
        _ada_kernel,
        out_shape=jax.ShapeDtypeStruct((bp, N), f32),
        grid=(N // tn,),
        in_specs=[pl.BlockSpec((bp, D), lambda j: (0, 0)),
                  pl.BlockSpec((D, tn), lambda j: (0, j)),
                  pl.BlockSpec((1, tn), lambda j: (0, j))],
        out_specs=pl.BlockSpec((bp, tn), lambda j: (0, j)),
        compiler_params=pltpu.CompilerParams(vmem_limit_bytes=VMEM_LIMIT),
        name="ada",
    )(cp, w_ada, b_ada.reshape(1, N))
    return out[:B].reshape(B, 6, D)


def _inproj_kernel(x_ref, mod_ref, g1_ref, wm_ref, wkt_ref, wlo_ref, wup_ref, wupt_ref,
                   bup_ref, bupt_ref, gqk_ref, grp_ref, grpt_ref,
                   qg_ref, kg_ref, gk_ref, kgt_ref, gkt_ref, vg_ref, rg_ref,
                   qd_ref, kd_ref, vd_ref):
    x = x_ref[0]
    ms = jnp.mean(x * x, axis=-1, keepdims=True)
    y = x * lax.rsqrt(ms + EPS) * g1_ref[...]
    h = (y * (1.0 + mod_ref[0, 1:2, :]) + mod_ref[0, 0:1, :]).astype(bf16)

    def proj(a, b):
        return _mm(h, wm_ref[:, a:b])

    o = 0
    qg_ref[0] = proj(o, o + GLA_QK_W); o += GLA_QK_W
    kg_ref[0] = proj(o, o + GLA_QK_W); o += GLA_QK_W
    vg_ref[0] = proj(o, o + GLA_V_W).astype(bf16); o += GLA_V_W
    rg_ref[0] = proj(o, o + GLA_V_W); o += GLA_V_W
    qk = proj(o, o + 2 * DIFF_QK_W); o += 2 * DIFF_QK_W
    vd_ref[0] = proj(o, o + DIFF_V_W).astype(bf16)

    kgt = _nt(wkt_ref[...], h)
    for j in range(kgt.shape[1] // PAIR):
        kgt_ref[0, j] = kgt[:, j * PAIR:(j + 1) * PAIR]

    lo = _mm(h, wlo_ref[...]).astype(bf16)
    z = _mm(lo, wup_ref[...]) + bup_ref[...]
    gk_ref[0] = (jnp.minimum(z, 0.0) - jnp.log1p(jnp.exp(-jnp.abs(z)))) * (1.0 / GLA_GATE_NORM)
    zt = _nt(wupt_ref[...], lo) + bupt_ref[...]
    gkt = (jnp.minimum(zt, 0.0) - jnp.log1p(jnp.exp(-jnp.abs(zt)))) * (1.0 / GLA_GATE_NORM)
    for j in range(gkt.shape[1] // PAIR):
        gkt_ref[0, j] = gkt[:, j * PAIR:(j + 1) * PAIR]

    sq_hi, sq_lo = _split(qk * qk)
    gs = _mm(sq_hi, grp_ref[...]) + _mm(sq_lo, grp_ref[...])
    r = lax.rsqrt(gs * (1.0 / DIFF_DQK) + EPS)
    r_hi, r_lo = _split(r)
    rb = _mm(r_hi, grpt_ref[...]) + _mm(r_lo, grpt_ref[...])
    qkn = qk * rb * gqk_ref[...]
    qd_ref[0] = qkn[:, :DIFF_QK_W].astype(bf16)
    kd_ref[0] = qkn[:, DIFF_QK_W:].astype(bf16)


def _inproj(x, mod, g_norm1, w_in, w_gk_up, b_gk_up, g_qnorm, g_knorm):
    B, S, D = x.shape
    offs = [0]
    for w in (GLA_QK_W, GLA_QK_W, GLA_V_W, GLA_V_W, GLA_GATE_RANK, DIFF_QK_W, DIFF_QK_W, DIFF_V_W):
        offs.append(offs[-1] + w)
    w_main = jnp.concatenate([w_in[:, offs[0]:offs[4]], w_in[:, offs[5]:offs[8]]], axis=1).astype(bf16)
    w_kt = w_in[:, offs[1]:offs[2]].T.astype(bf16)
    w_lo = jnp.zeros((D, LANES), f32).at[:, :GLA_GATE_RANK].set(w_in[:, offs[4]:offs[5]]).astype(bf16)
    w_up = jnp.zeros((LANES, GLA_QK_W), f32).at[:GLA_GATE_RANK].set(w_gk_up).astype(bf16)
    w_upt = w_up.T
    b_up = b_gk_up.reshape(1, GLA_QK_W)
    b_upt = b_gk_up.reshape(GLA_QK_W, 1)
    n_grp = 2 * DIFF_QK_W // DIFF_DQK
    gqk = jnp.concatenate([jnp.tile(g_qnorm, n_grp // 2) * (DIFF_DQK ** -0.5 * LOG2E),
                           jnp.tile(g_knorm, n_grp // 2)]).reshape(1, 2 * DIFF_QK_W)
    grp = (jnp.arange(2 * DIFF_QK_W)[:, None] // DIFF_DQK == jnp.arange(LANES)[None, :]).astype(bf16)
    grpt = grp.T
    nw = w_main.shape[1]
    tm = TM_IN
    const = lambda shape: pl.BlockSpec(shape, lambda b, i: (0,) * len(shape))
    row = lambda w: pl.BlockSpec((1, tm, w), lambda b, i: (b, i, 0))
    colT = pl.BlockSpec((1, tm // PAIR, GLA_QK_W, PAIR), lambda b, i: (b, i, 0, 0))
    outs = pl.pallas_call(
        _inproj_kernel,
        out_shape=[jax.ShapeDtypeStruct((B, S, GLA_QK_W), f32),
                   jax.ShapeDtypeStruct((B, S, GLA_QK_W), f32),
                   jax.ShapeDtypeStruct((B, S, GLA_QK_W), f32),
                   jax.ShapeDtypeStruct((B, S // PAIR, GLA_QK_W, PAIR), f32),
                   jax.ShapeDtypeStruct((B, S // PAIR, GLA_QK_W, PAIR), f32),
                   jax.ShapeDtypeStruct((B, S, GLA_V_W), bf16),
                   jax.ShapeDtypeStruct((B, S, GLA_V_W), f32),
                   jax.ShapeDtypeStruct((B, S, DIFF_QK_W), bf16),
                   jax.ShapeDtypeStruct((B, S, DIFF_QK_W), bf16),
                   jax.ShapeDtypeStruct((B, S, DIFF_V_W), bf16)],
        grid=(B, S // tm),
        in_specs=[row(D),
                  pl.BlockSpec((1, 6, D), lambda b, i: (b, 0, 0)),
                  const((1, D)), const((D, nw)), const((GLA_QK_W, D)), const((D, LANES)),
                  const((LANES, GLA_QK_W)), const((GLA_QK_W, LANES)),
                  const((1, GLA_QK_W)), const((GLA_QK_W, 1)),
                  const((1, 2 * DIFF_QK_W)), const((2 * DIFF_QK_W, LANES)),
                  const((LANES, 2 * DIFF_QK_W))],
        out_specs=[row(GLA_QK_W), row(GLA_QK_W), row(GLA_QK_W), colT, colT,
                   row(GLA_V_W), row(GLA_V_W), row(DIFF_QK_W), row(DIFF_QK_W), row(DIFF_V_W)],
        compiler_params=pltpu.CompilerParams(
            dimension_semantics=("arbitrary", "arbitrary"), vmem_limit_bytes=VMEM_LIMIT),
        name="inproj",
    )(x, mod, g_norm1.reshape(1, D), w_main, w_kt, w_lo, w_up, w_upt, b_up, b_upt, gqk, grp, grpt)
    return outs


def _gla_kernel(q_ref, k_ref, g_ref, kt_ref, gt_ref, v_ref, r_ref, gout_ref, tri_ref, trit_ref,
                o_ref, s_ref, *, n_pairs):
    H, DK, DV = N_GLA_HEADS, GLA_DK, GLA_DV

    @pl.when(pl.program_id(1) == 0)
    def _():
        s_ref[...] = jnp.zeros_like(s_ref)

    tri = tri_ref[...]
    trit = trit_ref[...]
    tri_b = tri > 0
    lane_head = lax.broadcasted_iota(jnp.int32, (1, H * DK), 1) // DK
    row_head = lax.broadcasted_iota(jnp.int32, (H * PAIR, 1), 0) // PAIR
    qmask = row_head == lane_head
    row_first = lax.broadcasted_iota(jnp.int32, (PAIR, 1), 0) < GLA_CHUNK
    row_first4 = (lax.broadcasted_iota(jnp.int32, (H * PAIR, 1), 0) % PAIR) < GLA_CHUNK
    lane_first = lax.broadcasted_iota(jnp.int32, (1, PAIR), 1) < GLA_CHUNK
    scale = DK ** -0.5
    gout = gout_ref[...]

    def pair(p, carry):
        r0 = pl.multiple_of(p * PAIR, PAIR)
        q = q_ref[0, pl.ds(r0, PAIR), :]
        k = k_ref[0, pl.ds(r0, PAIR), :]
        g = g_ref[0, pl.ds(r0, PAIR), :]
        kt = kt_ref[0, p]
        gt = gt_ref[0, p]
        v = v_ref[0, pl.ds(r0, PAIR), :]

        g_hi, g_lo = _split(g)
        gc = _mm(tri, g_hi) + _mm(tri, g_lo)
        gt_hi, gt_lo = _split(gt)
        gct = _mm(gt_hi, trit) + _mm(gt_lo, trit)
        g_last = jnp.where(row_first, gc[GLA_CHUNK - 1:GLA_CHUNK, :], gc[PAIR - 1:PAIR, :])
        gl0 = gct[:, GLA_CHUNK - 1:GLA_CHUNK]
        gl1 = gct[:, PAIR - 1:PAIR]
        g_last_t = jnp.where(lane_first, gl0, gl1)

        q_e = (q * (jnp.exp(gc) * scale)).astype(bf16)
        k_e = (k * jnp.exp(-gc)).astype(bf16)
        ks_t = kt * jnp.exp(g_last_t - gct)
        ks_t0 = jnp.where(lane_first, ks_t, 0.0).astype(bf16)
        ks_t1 = jnp.where(lane_first, 0.0, ks_t).astype(bf16)
        del g_last

        qm = jnp.where(qmask, jnp.concatenate([q_e] * H, axis=0), jnp.zeros((), bf16))
        a = _nt(qm, k_e)
        s0 = s_ref[...]

        u0 = []
        u1 = []
        for h in range(H):
            v_h = v[:, h * DV:(h + 1) * DV]
            u0.append(_mm(ks_t0[h * DK:(h + 1) * DK], v_h))
            u1.append(_mm(ks_t1[h * DK:(h + 1) * DK], v_h))
        u0 = jnp.concatenate(u0, axis=0)
        u1 = jnp.concatenate(u1, axis=0)
        s1 = s0 * jnp.exp(gl0) + u0
        s_ref[...] = s1 * jnp.exp(gl1) + u1

        o_inter = jnp.where(row_first4, _mm(qm, s0.astype(bf16)), _mm(qm, s1.astype(bf16)))
        for h in range(H):
            a_h = jnp.where(tri_b, a[h * PAIR:(h + 1) * PAIR], 0.0).astype(bf16)
            o_h = _mm(a_h, v[:, h * DV:(h + 1) * DV]) + o_inter[h * PAIR:(h + 1) * PAIR]
            ms = jnp.mean(o_h * o_h, axis=-1, keepdims=True)
            o_n = o_h * lax.rsqrt(ms + EPS) * gout
            r_h = r_ref[0, pl.ds(r0, PAIR), h * DV:(h + 1) * DV]
            o_ref[0, pl.ds(r0, PAIR), h * DV:(h + 1) * DV] = (o_n * _silu(r_h)).astype(bf16)
        return carry

    lax.fori_loop(0, n_pairs, pair, 0)


def _gla(qg, kg, gk, kgt, gkt, vg, rg, g_gla_out):
    B, S, _ = qg.shape
    tg = min(TG_GLA, S)
    r = jnp.arange(PAIR)
    tri = ((r[:, None] // GLA_CHUNK == r[None, :] // GLA_CHUNK) & (r[None, :] <= r[:, None])).astype(bf16)
    row = lambda w: pl.BlockSpec((1, tg, w), lambda b, i: (b, i, 0))
    colT = pl.BlockSpec((1, tg // PAIR, GLA_QK_W, PAIR), lambda b, i: (b, i, 0, 0))
    const = lambda shape: pl.BlockSpec(shape, lambda b, i: (0,) * len(shape))
    return pl.pallas_call(
        functools.partial(_gla_kernel, n_pairs=tg // PAIR),
        out_shape=jax.ShapeDtypeStruct((B, S, GLA_V_W), bf16),
        grid=(B, S // tg),
        in_specs=[row(GLA_QK_W), row(GLA_QK_W), row(GLA_QK_W), colT, colT,
                  row(GLA_V_W), row(GLA_V_W), const((1, GLA_DV)),
                  const((PAIR, PAIR)), const((PAIR, PAIR))],
        out_specs=row(GLA_V_W),
        scratch_shapes=[pltpu.VMEM((GLA_QK_W, GLA_DV), f32)],
        compiler_params=pltpu.CompilerParams(
            dimension_semantics=("arbitrary", "arbitrary"), vmem_limit_bytes=VMEM_LIMIT),
        name="gla",
    )(qg, kg, gk, kgt, gkt, vg, rg, g_gla_out.reshape(1, GLA_DV), tri, tri.T)


def _attn_finish(o, gsub_ref, o_ref, lambda_init):
    ms = jnp.mean(o * o, axis=-1, keepdims=True)
    o_ref[0] = (o * lax.rsqrt(ms + EPS) * gsub_ref[...] * (1.0 - lambda_init)).astype(bf16)


def _attn_lambda(lamv_ref, lambda_init):
    lv = lamv_ref[...]
    return (jnp.exp(jnp.sum(lv[0:1] * lv[1:2], axis=-1, keepdims=True))
            - jnp.exp(jnp.sum(lv[2:3] * lv[3:4], axis=-1, keepdims=True)) + lambda_init)


def _attn_bounded_kernel(q_ref, k_ref, v_ref, bias_ref, lamv_ref, gsub_ref, o_ref, vaug_ref, *, lambda_init):
    qi = pl.program_id(2)
    tq = q_ref.shape[1]
    S = k_ref.shape[1]

    @pl.when(qi == 0)
    def _():
        lane = lax.broadcasted_iota(jnp.int32, (S, DIFF_DV), 1)
        vaug_ref[:, :DIFF_DV] = v_ref[0]
        vaug_ref[:, DIFF_DV:] = jnp.where(lane == 0, 1.0, 0.0).astype(bf16)

    q = q_ref[0]
    lane = lax.broadcasted_iota(jnp.int32, (1, 2 * DIFF_DQK), 1)
    zero = jnp.zeros((), bf16)
    qs = (jnp.where(lane < DIFF_DQK, q, zero), jnp.where(lane < DIFF_DQK, zero, q))

    def update(accs, k0, bias):
        kb = k_ref[0, pl.ds(k0, tq), :]
        vb = vaug_ref[pl.ds(k0, tq), :]
        out = []
        for c in range(2):
            s = _nt(qs[c], kb)
            if bias is not None:
                s = s + bias[c]
            out.append(accs[c] + _mm(jnp.exp2(s).astype(bf16), vb))
        return tuple(out)

    def far(kj, accs):
        return update(accs, pl.multiple_of(kj * tq, tq), None)

    def far_group(g, accs):
        for u in range(ATTN_UNROLL):
            accs = far(g * ATTN_UNROLL + u, accs)
        return accs

    n_far = jnp.maximum(qi - 1, 0)
    n_grp = n_far // ATTN_UNROLL
    accs = lax.fori_loop(0, n_grp, far_group,
                         (jnp.zeros((tq, 2 * DIFF_DV), f32), jnp.zeros((tq, 2 * DIFF_DV), f32)))
    accs = lax.fori_loop(n_grp * ATTN_UNROLL, n_far, far, accs)
    accs = update(accs, pl.multiple_of(qi * tq, tq), (bias_ref[0, 0, 1], bias_ref[0, 1, 1]))
    has_prev = qi > 0
    accs = update(accs, pl.multiple_of(jnp.maximum(qi - 1, 0) * tq, tq),
                  (jnp.where(has_prev, bias_ref[0, 0, 0], NEG), jnp.where(has_prev, bias_ref[0, 1, 0], NEG)))
    a0, a1 = accs
    o = (a0[:, :DIFF_DV] / a0[:, DIFF_DV:DIFF_DV + 1]
         - _attn_lambda(lamv_ref, lambda_init) * (a1[:, :DIFF_DV] / a1[:, DIFF_DV:DIFF_DV + 1]))
    _attn_finish(o, gsub_ref, o_ref, lambda_init)


def _attn_kernel(q_ref, k_ref, v_ref, bias_ref, lamv_ref, gsub_ref, o_ref, *, lambda_init):
    qi = pl.program_id(2)
    tq = q_ref.shape[1]
    q = q_ref[0]
    lane = lax.broadcasted_iota(jnp.int32, (1, 2 * DIFF_DQK), 1)
    zero = jnp.zeros((), bf16)
    qs = (jnp.where(lane < DIFF_DQK, q, zero), jnp.where(lane < DIFF_DQK, zero, q))

    def update(state, kb, vb, bias):
        new = []
        for c in range(2):
            m, l, acc = state[c]
            s = _nt(qs[c], kb)
            if bias is not None:
                s = s + bias[c]
            m_new = jnp.maximum(m, jnp.max(s, axis=-1, keepdims=True))
            alpha = jnp.exp2(m - m_new)
            p = jnp.exp2(s - m_new)
            l = alpha * l + jnp.sum(p, axis=-1, keepdims=True)
            acc = alpha * acc + _mm(p.astype(bf16), vb)
            new.append((m_new, l, acc))
        return tuple(new)

    init = tuple((jnp.full((tq, 1), NEG, f32), jnp.zeros((tq, 1), f32), jnp.zeros((tq, DIFF_DV), f32))
                 for _ in range(2))

    def far(kj, state):
        k0 = pl.multiple_of(kj * tq, tq)
        return update(state, k_ref[0, pl.ds(k0, tq), :], v_ref[0, pl.ds(k0, tq), :], None)

    state = lax.fori_loop(0, jnp.maximum(qi - 1, 0), far, init)

    kd0 = pl.multiple_of(qi * tq, tq)
    state = update(state, k_ref[0, pl.ds(kd0, tq), :], v_ref[0, pl.ds(kd0, tq), :],
                   (bias_ref[0, 0, 1], bias_ref[0, 1, 1]))
    kp0 = pl.multiple_of(jnp.maximum(qi - 1, 0) * tq, tq)
    has_prev = qi > 0
    state = update(state, k_ref[0, pl.ds(kp0, tq), :], v_ref[0, pl.ds(kp0, tq), :],
                   (jnp.where(has_prev, bias_ref[0, 0, 0], NEG), jnp.where(has_prev, bias_ref[0, 1, 0], NEG)))

    (_, l0, a0), (_, l1, a1) = state
    o = a0 / l0 - _attn_lambda(lamv_ref, lambda_init) * (a1 / l1)
    _attn_finish(o, gsub_ref, o_ref, lambda_init)


def _t5_bucket(n):
    max_exact = NUM_BUCKETS // 2
    nf = jnp.maximum(n, 1).astype(f32)
    large = max_exact + (jnp.log(nf / max_exact) / math.log(MAX_DISTANCE / max_exact)
                         * (NUM_BUCKETS - max_exact)).astype(jnp.int32)
    large = jnp.minimum(large, NUM_BUCKETS - 1)
    return jnp.where(n < max_exact, n, large)


def _toeplitz(vec, n):
    w = jnp.concatenate([vec[:, ::-1], jnp.zeros((vec.shape[0], 1), vec.dtype)], axis=1)
    flat = jnp.tile(w, (1, n))[:, :n * (2 * n - 1)].reshape(-1, n, 2 * n - 1)
    return flat[:, :, n - 1:]


def _bias_tiles(rel_bias_table, S, n):
    HM = rel_bias_table.shape[1]
    assert n >= MAX_DISTANCE
    d = jnp.arange(2 * n, dtype=jnp.int32)
    by_dist = rel_bias_table[_t5_bucket(d)].astype(f32).T
    rel = (by_dist - rel_bias_table[NUM_BUCKETS - 1].astype(f32)[:, None]) * LOG2E
    diag = _toeplitz(jnp.concatenate([jnp.full((HM, n - 1), NEG, f32), rel[:, :n]], axis=1), n)
    prev = _toeplitz(rel[:, 1:2 * n], n)
    return jnp.stack([prev, diag], axis=1).reshape(HM // 2, 2, 2, n, n)


def _attn(qd, kd, vd, bias_tiles, lamv, g_subln, lambda_init, bounded):
    B, S, _ = qd.shape
    H = N_DIFF_HEADS
    tq = min(TQ, S)
    body = _attn_bounded_kernel if bounded else _attn_kernel
    scratch = [pltpu.VMEM((S, 2 * DIFF_DV), bf16)] if bounded else []
    return pl.pallas_call(
        functools.partial(body, lambda_init=lambda_init),
        out_shape=jax.ShapeDtypeStruct((B, S, DIFF_V_W), bf16),
        scratch_shapes=scratch,
        grid=(B, H, S // tq),
        in_specs=[pl.BlockSpec((1, tq, 2 * DIFF_DQK), lambda b, h, i: (b, i, h)),
                  pl.BlockSpec((1, S, 2 * DIFF_DQK), lambda b, h, i: (b, 0, h)),
                  pl.BlockSpec((1, S, DIFF_DV), lambda b, h, i: (b, 0, h)),
                  pl.BlockSpec((1, 2, 2, tq, tq), lambda b, h, i: (h, 0, 0, 0, 0)),
                  pl.BlockSpec((4, DIFF_DQK), lambda b, h, i: (0, 0)),
                  pl.BlockSpec((1, DIFF_DV), lambda b, h, i: (0, 0))],
        out_specs=pl.BlockSpec((1, tq, DIFF_DV), lambda b, h, i: (b, i, h)),
        compiler_params=pltpu.CompilerParams(
            dimension_semantics=("arbitrary", "arbitrary", "arbitrary"), vmem_limit_bytes=VMEM_LIMIT),
        name="attn_bounded" if bounded else "attn_online",
    )(qd, kd, vd, bias_tiles, lamv, g_subln.reshape(1, DIFF_DV))


def _scores_bounded(rel_bias_table, g_qnorm, g_knorm):
    qk = DIFF_DQK ** 0.5 * jnp.max(jnp.abs(g_qnorm)) * jnp.max(jnp.abs(g_knorm)) * 1.02
    rel = jnp.max(jnp.abs(rel_bias_table - rel_bias_table[NUM_BUCKETS - 1:]))
    return qk + rel <= SAFE_SCORE


def _outproj_kernel(og_ref, od_ref, x_ref, mod_ref, wo_ref, g2_ref, wr_ref, br_ref,
                    x1_ref, hp_ref, lg_ref):
    half = og_ref.shape[2]
    mix = _mm(og_ref[0], wo_ref[:half, :]) + _mm(od_ref[0], wo_ref[half:, :])
    x1 = x_ref[0] + mod_ref[0, 2:3, :] * mix
    x1_ref[0] = x1
    ms = jnp.mean(x1 * x1, axis=-1, keepdims=True)
    y = x1 * lax.rsqrt(ms + EPS) * g2_ref[...]
    h = (y * (1.0 + mod_ref[0, 4:5, :]) + mod_ref[0, 3:4, :]).astype(bf16)
    lg_ref[...] = _mm(h, wr_ref[...]) + br_ref[...]
    hf = h.astype(f32)
    dh = hf.shape[1] // 2
    lo_bits = pltpu.bitcast(hf[:, :dh], jnp.uint32) >> 16
    hi_bits = pltpu.bitcast(hf[:, dh:], jnp.uint32) & jnp.uint32(0xFFFF0000)
    hp_ref[...] = lo_bits | hi_bits


def _outproj(og, od, x, mod, w_out, g_norm2, w_router, b_router):
    B, S, D = x.shape
    E = w_router.shape[1]
    tm = TM_IN
    nj = S // tm
    w_r = jnp.zeros((D, LANES), f32).at[:, :E].set(w_router).astype(bf16)
    b_r = jnp.full((1, LANES), NEG, f32).at[0, :E].set(b_router)
    const = lambda shape: pl.BlockSpec(shape, lambda b, i: (0,) * len(shape))
    return pl.pallas_call(
        _outproj_kernel,
        out_shape=[jax.ShapeDtypeStruct((B, S, D), f32),
                   jax.ShapeDtypeStruct((B * S, D // 2), jnp.uint32),
                   jax.ShapeDtypeStruct((B * S, LANES), f32)],
        grid=(B, nj),
        in_specs=[pl.BlockSpec((1, tm, og.shape[2]), lambda b, i: (b, i, 0)),
                  pl.BlockSpec((1, tm, od.shape[2]), lambda b, i: (b, i, 0)),
                  pl.BlockSpec((1, tm, D), lambda b, i: (b, i, 0)),
                  pl.BlockSpec((1, 6, D), lambda b, i: (b, 0, 0)),
                  const((w_out.shape[0], D)), const((1, D)), const((D, LANES)), const((1, LANES))],
        out_specs=[pl.BlockSpec((1, tm, D), lambda b, i: (b, i, 0)),
                   pl.BlockSpec((tm, D // 2), lambda b, i: (b * nj + i, 0)),
                   pl.BlockSpec((tm, LANES), lambda b, i: (b * nj + i, 0))],
        compiler_params=pltpu.CompilerParams(
            dimension_semantics=("arbitrary", "arbitrary"), vmem_limit_bytes=VMEM_LIMIT),
        name="outproj",
    )(og, od, x, mod, w_out.astype(bf16), g_norm2.reshape(1, D), w_r, b_r)


def _route_kernel(lg_ref, lt_ref, ri_ref, rw_ref, cnt_ref, run_ref):
    @pl.when(pl.program_id(0) == 0)
    def _():
        run_ref[...] = jnp.zeros_like(run_ref)

    x = lg_ref[...]
    tr = x.shape[0]
    lane = lax.broadcasted_iota(jnp.int32, (tr, LANES), 1)
    lane_f = lane.astype(f32)
    vals, hots, idxs = [], [], []
    for _ in range(TOP_K):
        m = jnp.max(x, axis=-1, keepdims=True)
        idx = jnp.min(jnp.where(x == m, lane_f, float(LANES)), axis=-1, keepdims=True)
        hot = lane_f == idx
        x = jnp.where(hot, -jnp.inf, x)
        vals.append(m)
        hots.append(hot)
        idxs.append(idx.astype(jnp.int32))
    ex = [jnp.exp(v - vals[0]) for v in vals]
    den = ex[0] + ex[1] + ex[2] + ex[3]
    sel = (hots[0] | hots[1] | hots[2] | hots[3]).astype(f32)
    rank = _mm(lt_ref[...], sel.astype(bf16)) + run_ref[...]
    run_ref[...] = run_ref[...] + jnp.sum(sel, axis=0, keepdims=True)
    cnt_ref[...] = run_ref[...]
    ri = jnp.zeros((tr, LANES), jnp.int32)
    rw = jnp.zeros((tr, LANES), f32)
    for k in range(TOP_K):
        rk = jnp.sum(jnp.where(hots[k], rank, 0.0), axis=-1, keepdims=True).astype(jnp.int32)
        ri = jnp.where(lane == k, rk, ri)
        ri = jnp.where(lane == TOP_K + k, idxs[k], ri)
        rw = jnp.where(lane == k, ex[k] / den, rw)
    ri_ref[...] = ri
    rw_ref[...] = rw


def _route(logits):
    T = logits.shape[0]
    tr = min(TR, T)
    r = jnp.arange(tr)
    lt = (r[None, :] < r[:, None]).astype(bf16)
    return pl.pallas_call(
        _route_kernel,
        out_shape=[jax.ShapeDtypeStruct((T, LANES), jnp.int32),
                   jax.ShapeDtypeStruct((T, LANES), f32),
                   jax.ShapeDtypeStruct((1, LANES), f32)],
        grid=(T // tr,),
        in_specs=[pl.BlockSpec((tr, LANES), lambda i: (i, 0)),
                  pl.BlockSpec((tr, tr), lambda i: (0, 0))],
        out_specs=[pl.BlockSpec((tr, LANES), lambda i: (i, 0)),
                   pl.BlockSpec((tr, LANES), lambda i: (i, 0)),
                   pl.BlockSpec((1, LANES), lambda i: (0, 0))],
        scratch_shapes=[pltpu.VMEM((1, LANES), f32)],
        compiler_params=pltpu.CompilerParams(dimension_semantics=("arbitrary",)),
        name="route",
    )(logits, lt)


def _row_copy(src, s, dst, d, sem):
    return pltpu.make_async_copy(src.at[pl.ds(s, 1)], dst.at[pl.ds(d, 1)], sem)


def _dispatch_kernel(dest_ref, h_ref, xs_in_ref, xs_ref, sem):
    del xs_in_ref
    n_tok = h_ref.shape[0]

    def issue(g, c):
        for u in range(DMA_UNROLL):
            r = g * DMA_UNROLL + u
            for k in range(TOP_K):
                _row_copy(h_ref, r, xs_ref, dest_ref[r * TOP_K + k], sem).start(priority=k % 2)
        return c

    lax.fori_loop(0, n_tok // DMA_UNROLL, issue, 0)
    done = xs_ref.at[pl.ds(0, n_tok * TOP_K)]
    pltpu.make_async_copy(done, done, sem).wait()


def _dispatch(dest_flat, hp, n_rows):
    T, W = hp.shape
    xs0 = jnp.zeros((n_rows, W), hp.dtype)
    return pl.pallas_call(
        _dispatch_kernel,
        out_shape=jax.ShapeDtypeStruct((n_rows, W), hp.dtype),
        grid=(T // TD,),
        in_specs=[pl.BlockSpec((TD * TOP_K,), lambda i: (i,), memory_space=pltpu.SMEM),
                  pl.BlockSpec((TD, W), lambda i: (i, 0)),
                  pl.BlockSpec(memory_space=pl.ANY)],
        out_specs=pl.BlockSpec(memory_space=pl.ANY),
        scratch_shapes=[pltpu.SemaphoreType.DMA(())],
        input_output_aliases={2: 0},
        compiler_params=pltpu.CompilerParams(dimension_semantics=("arbitrary",)),
        name="dispatch",
    )(dest_flat, hp, xs0)


def _ffn_kernel(be_ref, nu_ref, xs_ref, wgu32_ref, bgu_ref, wd32_ref, bd_ref, ys_ref, wgu_ref, wd_ref):
    i = pl.program_id(0)
    used = i < nu_ref[0]
    new_expert = (i == 0) | (be_ref[i] != be_ref[jnp.maximum(i - 1, 0)])

    @pl.when(used & new_expert)
    def _():
        rows = 128

        def cast(src, dst):
            def body(r, c):
                r0 = pl.multiple_of(r * rows, rows)
                dst[pl.ds(r0, rows), :] = src[0, pl.ds(r0, rows), :].astype(bf16)
                return c
            lax.fori_loop(0, src.shape[1] // rows, body, 0)
        cast(wgu32_ref, wgu_ref)
        cast(wd32_ref, wd_ref)

    @pl.when(used)
    def _():
        u = xs_ref[...]
        dh = u.shape[1]
        F = wd_ref.shape[0]
        xa = pltpu.bitcast(u << 16, f32).astype(bf16)
        xb = pltpu.bitcast(u & jnp.uint32(0xFFFF0000), f32).astype(bf16)
        acc = None
        fc = F // 2
        for c in range(2):
            def gu(col0):
                return (_mm(xa, wgu_ref[:dh, col0:col0 + fc]) + _mm(xb, wgu_ref[dh:, col0:col0 + fc])
                        + bgu_ref[0, :, col0:col0 + fc])
            gate = jnp.minimum(gu(c * fc), SWIGLU_LIMIT)
            up = jnp.clip(gu(F + c * fc), -SWIGLU_LIMIT, SWIGLU_LIMIT)
            y = (up + 1.0) * (gate * jax.nn.sigmoid(SWIGLU_ALPHA * gate))
            part = _mm(y.astype(bf16), wd_ref[c * fc:(c + 1) * fc, :])
            acc = part if acc is None else acc + part
        ys_ref[...] = acc + bd_ref[0]

    @pl.when(jnp.logical_not(used))
    def _():
        ys_ref[...] = jnp.zeros_like(ys_ref)


def _ffn(block_e, n_used, xs, w_gate_up, b_gate_up, w_down, b_down):
    P, dh = xs.shape
    E, D, F2 = w_gate_up.shape
    F = F2 // 2
    nb = P // FFN_BLK

    def blk(i, be, nu):
        return jnp.minimum(i, nu[0] - 1)

    grid_spec = pltpu.PrefetchScalarGridSpec(
        num_scalar_prefetch=2,
        grid=(nb,),
        in_specs=[pl.BlockSpec((FFN_BLK, dh), lambda i, be, nu: (blk(i, be, nu), 0)),
                  pl.BlockSpec((1, D, F2), lambda i, be, nu: (be[blk(i, be, nu)], 0, 0)),
                  pl.BlockSpec((1, 1, F2), lambda i, be, nu: (be[blk(i, be, nu)], 0, 0)),
                  pl.BlockSpec((1, F, D), lambda i, be, nu: (be[blk(i, be, nu)], 0, 0)),
                  pl.BlockSpec((1, 1, D), lambda i, be, nu: (be[blk(i, be, nu)], 0, 0))],
        out_specs=pl.BlockSpec((FFN_BLK, D), lambda i, be, nu: (i, 0)),
        scratch_shapes=[pltpu.VMEM((D, F2), bf16), pltpu.VMEM((F, D), bf16)],
    )
    return pl.pallas_call(
        _ffn_kernel,
        out_shape=jax.ShapeDtypeStruct((P, D), f32),
        grid_spec=grid_spec,
        compiler_params=pltpu.CompilerParams(
            dimension_semantics=("arbitrary",), vmem_limit_bytes=VMEM_LIMIT_FFN),
        name="ffn",
    )(block_e, n_used, xs, w_gate_up, b_gate_up.reshape(E, 1, F2), w_down, b_down.reshape(E, 1, D))


def _combine_kernel(dcur_ref, dnext_ref, ys_ref, x1_ref, rw_ref, mod_ref, o_ref, buf, sem):
    step = pl.program_id(0) * pl.num_programs(1) + pl.program_id(1)
    n_steps = pl.num_programs(0) * pl.num_programs(1)
    slot = step % 2

    def issue(dref, s):
        def body(g, c):
            for u in range(DMA_UNROLL):
                r = g * DMA_UNROLL + u
                for k in range(TOP_K):
                    pltpu.make_async_copy(ys_ref.at[pl.ds(dref[r * TOP_K + k], 1)],
                                          buf.at[s, k, pl.ds(r, 1)], sem.at[s]).start(priority=k % 2)
            return c
        lax.fori_loop(0, TD // DMA_UNROLL, body, 0)

    @pl.when(step == 0)
    def _():
        issue(dcur_ref, 0)

    @pl.when(step + 1 < n_steps)
    def _():
        issue(dnext_ref, 1 - slot)

    pltpu.make_async_copy(buf.at[slot], buf.at[slot], sem.at[slot]).wait()

    rw = rw_ref[...]
    moe = rw[:, 0:1] * buf[slot, 0]
    for k in range(1, TOP_K):
        moe = moe + rw[:, k:k + 1] * buf[slot, k]
    o_ref[0] = x1_ref[0] + mod_ref[0, 5:6, :] * moe


def _combine(dest_flat, ys, x1, rw, mod):
    B, S, D = x1.shape
    nj = S // TD
    n_steps = B * nj
    return pl.pallas_call(
        _combine_kernel,
        out_shape=jax.ShapeDtypeStruct((B, S, D), f32),
        grid=(B, nj),
        in_specs=[pl.BlockSpec((TD * TOP_K,), lambda b, j: (b * nj + j,), memory_space=pltpu.SMEM),
                  pl.BlockSpec((TD * TOP_K,), lambda b, j: (jnp.minimum(b * nj + j + 1, n_steps - 1),),
                               memory_space=pltpu.SMEM),
                  pl.BlockSpec(memory_space=pl.ANY),
                  pl.BlockSpec((1, TD, D), lambda b, j: (b, j, 0)),
                  pl.BlockSpec((TD, LANES), lambda b, j: (b * nj + j, 0)),
                  pl.BlockSpec((1, 6, D), lambda b, j: (b, 0, 0))],
        out_specs=pl.BlockSpec((1, TD, D), lambda b, j: (b, j, 0)),
        scratch_shapes=[pltpu.VMEM((2, TOP_K, TD, D), f32), pltpu.SemaphoreType.DMA((2,))],
        compiler_params=pltpu.CompilerParams(
            dimension_semantics=("arbitrary", "arbitrary"), vmem_limit_bytes=VMEM_LIMIT),
        name="combine",
    )(dest_flat, dest_flat, ys, x1, rw, mod)


def _moe(hp, logits, x1, mod, w_gate_up, b_gate_up, w_down, b_down):
    T = hp.shape[0]
    E = w_gate_up.shape[0]
    ri, rw, cnt = _route(logits)
    rank = ri[:, :TOP_K]
    e_sel = ri[:, TOP_K:2 * TOP_K]
    counts = cnt[0, :E].astype(jnp.int32)
    padded = ((counts + FFN_BLK - 1) // FFN_BLK) * FFN_BLK
    p_ends = jnp.cumsum(padded)
    p_starts = p_ends - padded
    nb = -(-T * TOP_K // FFN_BLK) + E
    n_used = jnp.maximum(p_ends[-1:] // FFN_BLK, 1).astype(jnp.int32)
    blk_start = jnp.arange(nb, dtype=jnp.int32) * FFN_BLK
    block_e = jnp.minimum(jnp.sum(p_ends[None, :] <= blk_start[:, None], axis=1), E - 1).astype(jnp.int32)
    onehot = e_sel[:, :, None] == jnp.arange(E, dtype=jnp.int32)[None, None, :]
    dest = (jnp.sum(jnp.where(onehot, p_starts[None, None, :], 0), axis=-1) + rank).reshape(-1)
    xs = _dispatch(dest, hp, nb * FFN_BLK)
    ys = _ffn(block_e, n_used, xs, w_gate_up, b_gate_up, w_down, b_down)
    return _combine(dest, ys, x1, rw, mod)


def kernel(x, c, rel_bias_table, w_ada, b_ada, g_norm1, w_in, w_gk_up, b_gk_up, g_gla_out, g_qnorm, g_knorm, lambda_q1, lambda_k1, lambda_q2, lambda_k2, g_subln, w_out, g_norm2, w_router, b_router, w_gate_up, b_gate_up, w_down, b_down):
    B, S, D = x.shape
    depth = w_ada.shape[0]
    bias_tiles = _bias_tiles(rel_bias_table, S, min(TQ, S))
    for l in range(depth):
        lambda_init = 0.8 - 0.6 * math.exp(-0.3 * l)
        mod = _ada(c, w_ada[l], b_ada[l])
        qg, kg, gk, kgt, gkt, vg, rg, qd, kd, vd = _inproj(
            x, mod, g_norm1[l], w_in[l], w_gk_up[l], b_gk_up[l], g_qnorm[l], g_knorm[l])
        og = _gla(qg, kg, gk, kgt, gkt, vg, rg, g_gla_out[l])
        lamv = jnp.stack([lambda_q1[l], lambda_k1[l], lambda_q2[l], lambda_k2[l]]).astype(f32)
        od = lax.cond(_scores_bounded(rel_bias_table, g_qnorm[l], g_knorm[l]),
                      functools.partial(_attn, lambda_init=lambda_init, bounded=True),
                      functools.partial(_attn, lambda_init=lambda_init, bounded=False),
                      qd, kd, vd, bias_tiles, lamv, g_subln[l])
        x1, hp, logits = _outproj(og, od, x, mod, w_out[l], g_norm2[l], w_router[l], b_router[l])
        x = _moe(hp, logits, x1, mod, w_gate_up[l], b_gate_up[l], w_down[l], b_down[l])
    return x
```

```python
import functools
import math

import jax
import jax.numpy as jnp
from jax import lax
from jax.experimental import pallas as pl
from jax.experimental.pallas import tpu as pltpu

f32 = jnp.float32
bf16 = jnp.bfloat16

N_GLA_HEADS = 4
GLA_DK = 64
GLA_DV = 128
GLA_GATE_RANK = 16
GLA_GATE_NORM = 16.0
GLA_CHUNK = 64
N_DIFF_HEADS = 4
DIFF_DQK = 64
DIFF_DV = 128
NUM_BUCKETS = 32
MAX_DISTANCE = 128
TOP_K = 4
SWIGLU_LIMIT = 7.0
SWIGLU_ALPHA = 1.702
EPS = 1e-6

GLA_QK_W = N_GLA_HEADS * GLA_DK
GLA_V_W = N_GLA_HEADS * GLA_DV
DIFF_QK_W = N_DIFF_HEADS * 2 * DIFF_DQK
DIFF_V_W = N_DIFF_HEADS * DIFF_DV

LANES = 128
NEG = -1e30
LOG2E = math.log2(math.e)
SAFE_SCORE = 40.0
VMEM_LIMIT = 48 * 1024 * 1024
VMEM_LIMIT_FFN = 58 * 1024 * 1024

TM_IN = 512
TG_GLA = 1024
PAIR = 2 * GLA_CHUNK
TQ = 512
ATTN_UNROLL = 4
TR = 512
TD = 256
ROW_TILE = 8
DMA_UNROLL = 8
FFN_BLK = 512


def _nt(a, b):
    return lax.dot_general(a, b, (((1,), (1,)), ((), ())), preferred_element_type=f32)


def _mm(a, b):
    return jnp.dot(a, b, preferred_element_type=f32)


def _split(x):
    hi = x.astype(bf16)
    lo = (x - hi.astype(f32)).astype(bf16)
    return hi, lo


def _silu(x):
    return x * jax.nn.sigmoid(x)


def _ada_kernel(c_ref, w_ref, b_ref, o_ref):
    c = c_ref[...]
    o_ref[...] = _mm(_silu(c).astype(bf16), w_ref[...].astype(bf16)) + b_ref[...]


def _ada(c, w_ada, b_ada):
    B, D = c.shape
    N = w_ada.shape[1]
    bp = 8
    cp = jnp.zeros((bp, D), f32).at[:B].set(c)
    tn = 1536
    out = pl.pallas_call(
        _ada_kernel,
        out_shape=jax.ShapeDtypeStruct((bp, N), f32),
        grid=(N // tn,),
        in_specs=[pl.BlockSpec((bp, D), lambda j: (0, 0)),
                  pl.BlockSpec((D, tn), lambda j: (0, j)),
                  pl.BlockSpec((1, tn), lambda j: (0, j))],
        out_specs=pl.BlockSpec((bp, tn), lambda j: (0, j)),
        compiler_params=pltpu.CompilerParams(vmem_limit_bytes=VMEM_LIMIT),
        name="ada",
    )(cp, w_ada, b_ada.reshape(1, N))
    return out[:B].reshape(B, 6, D)


def _inproj_kernel(x_ref, mod_ref, g1_ref, wm_ref, wkt_ref, wlo_ref, wup_ref, wupt_ref,
                   bup_ref, bupt_ref, gqk_ref, grp_ref, grpt_ref,
                   qg_ref, kg_ref, gk_ref, kgt_ref, gkt_ref, vg_ref, rg_ref,
                   qd_ref, kd_ref, vd_ref):
    x = x_ref[0]
    ms = jnp.mean(x * x, axis=-1, keepdims=True)
    y = x * lax.rsqrt(ms + EPS) * g1_ref[...]
    h = (y * (1.0 + mod_ref[0, 1:2, :]) + mod_ref[0, 0:1, :]).astype(bf16)

    def proj(a, b):
        return _mm(h, wm_ref[:, a:b])

    o = 0
    qg_ref[0] = proj(o, o + GLA_QK_W); o += GLA_QK_W
    kg_ref[0] = proj(o, o + GLA_QK_W); o += GLA_QK_W
    vg_ref[0] = proj(o, o + GLA_V_W).astype(bf16); o += GLA_V_W
    rg_ref[0] = proj(o, o + GLA_V_W); o += GLA_V_W
    qk = proj(o, o + 2 * DIFF_QK_W); o += 2 * DIFF_QK_W
    vd_ref[0] = proj(o, o + DIFF_V_W).astype(bf16)

    kgt = _nt(wkt_ref[...], h)
    for j in range(kgt.shape[1] // PAIR):
        kgt_ref[0, j] = kgt[:, j * PAIR:(j + 1) * PAIR]

    lo = _mm(h, wlo_ref[...]).astype(bf16)
    z = _mm(lo, wup_ref[...]) + bup_ref[...]
    gk_ref[0] = (jnp.minimum(z, 0.0) - jnp.log1p(jnp.exp(-jnp.abs(z)))) * (1.0 / GLA_GATE_NORM)
    zt = _nt(wupt_ref[...], lo) + bupt_ref[...]
    gkt = (jnp.minimum(zt, 0.0) - jnp.log1p(jnp.exp(-jnp.abs(zt)))) * (1.0 / GLA_GATE_NORM)
    for j in range(gkt.shape[1] // PAIR):
        gkt_ref[0, j] = gkt[:, j * PAIR:(j + 1) * PAIR]

    sq_hi, sq_lo = _split(qk * qk)
    gs = _mm(sq_hi, grp_ref[...]) + _mm(sq_lo, grp_ref[...])
    r = lax.rsqrt(gs * (1.0 / DIFF_DQK) + EPS)
    r_hi, r_lo = _split(r)
    rb = _mm(r_hi, grpt_ref[...]) + _mm(r_lo, grpt_ref[...])
    qkn = qk * rb * gqk_ref[...]
    qd_ref[0] = qkn[:, :DIFF_QK_W].astype(bf16)
    kd_ref[0] = qkn[:, DIFF_QK_W:].astype(bf16)


def _inproj(x, mod, g_norm1, w_in, w_gk_up, b_gk_up, g_qnorm, g_knorm):
    B, S, D = x.shape
    offs = [0]
    for w in (GLA_QK_W, GLA_QK_W, GLA_V_W, GLA_V_W, GLA_GATE_RANK, DIFF_QK_W, DIFF_QK_W, DIFF_V_W):
        offs.append(offs[-1] + w)
    w_main = jnp.concatenate([w_in[:, offs[0]:offs[4]], w_in[:, offs[5]:offs[8]]], axis=1).astype(bf16)
    w_kt = w_in[:, offs[1]:offs[2]].T.astype(bf16)
    w_lo = jnp.zeros((D, LANES), f32).at[:, :GLA_GATE_RANK].set(w_in[:, offs[4]:offs[5]]).astype(bf16)
    w_up = jnp.zeros((LANES, GLA_QK_W), f32).at[:GLA_GATE_RANK].set(w_gk_up).astype(bf16)
    w_upt = w_up.T
    b_up = b_gk_up.reshape(1, GLA_QK_W)
    b_upt = b_gk_up.reshape(GLA_QK_W, 1)
    n_grp = 2 * DIFF_QK_W // DIFF_DQK
    gqk = jnp.concatenate([jnp.tile(g_qnorm, n_grp // 2) * (DIFF_DQK ** -0.5 * LOG2E),
                           jnp.tile(g_knorm, n_grp // 2)]).reshape(1, 2 * DIFF_QK_W)
    grp = (jnp.arange(2 * DIFF_QK_W)[:, None] // DIFF_DQK == jnp.arange(LANES)[None, :]).astype(bf16)
    grpt = grp.T
    nw = w_main.shape[1]
    tm = TM_IN
    const = lambda shape: pl.BlockSpec(shape, lambda b, i: (0,) * len(shape))
    row = lambda w: pl.BlockSpec((1, tm, w), lambda b, i: (b, i, 0))
    colT = pl.BlockSpec((1, tm // PAIR, GLA_QK_W, PAIR), lambda b, i: (b, i, 0, 0))
    outs = pl.pallas_call(
        _inproj_kernel,
        out_shape=[jax.ShapeDtypeStruct((B, S, GLA_QK_W), f32),
                   jax.ShapeDtypeStruct((B, S, GLA_QK_W), f32),
                   jax.ShapeDtypeStruct((B, S, GLA_QK_W), f32),
                   jax.ShapeDtypeStruct((B, S // PAIR, GLA_QK_W, PAIR), f32),
                   jax.ShapeDtypeStruct((B, S // PAIR, GLA_QK_W, PAIR), f32),
                   jax.ShapeDtypeStruct((B, S, GLA_V_W), bf16),
                   jax.ShapeDtypeStruct((B, S, GLA_V_W), f32),
                   jax.ShapeDtypeStruct((B, S, DIFF_QK_W), bf16),
                   jax.ShapeDtypeStruct((B, S, DIFF_QK_W), bf16),
                   jax.ShapeDtypeStruct((B, S, DIFF_V_W), bf16)],
        grid=(B, S // tm),
        in_specs=[row(D),
                  pl.BlockSpec((1, 6, D), lambda b, i: (b, 0, 0)),
                  const((1, D)), const((D, nw)), const((GLA_QK_W, D)), const((D, LANES)),
                  const((LANES, GLA_QK_W)), const((GLA_QK_W, LANES)),
                  const((1, GLA_QK_W)), const((GLA_QK_W, 1)),
                  const((1, 2 * DIFF_QK_W)), const((2 * DIFF_QK_W, LANES)),
                  const((LANES, 2 * DIFF_QK_W))],
        out_specs=[row(GLA_QK_W), row(GLA_QK_W), row(GLA_QK_W), colT, colT,
                   row(GLA_V_W), row(GLA_V_W), row(DIFF_QK_W), row(DIFF_QK_W), row(DIFF_V_W)],
        compiler_params=pltpu.CompilerParams(
            dimension_semantics=("arbitrary", "arbitrary"), vmem_limit_bytes=VMEM_LIMIT),
        name="inproj",
    )(x, mod, g_norm1.reshape(1, D), w_main, w_kt, w_lo, w_up, w_upt, b_up, b_upt, gqk, grp, grpt)
    return outs


def _gla_kernel(q_ref, k_ref, g_ref, kt_ref, gt_ref, v_ref, r_ref, gout_ref, tri_ref, trit_ref,
                o_ref, s_ref, *, n_pairs):
    H, DK, DV = N_GLA_HEADS, GLA_DK, GLA_DV

    @pl.when(pl.program_id(1) == 0)
    def _():
        s_ref[...] = jnp.zeros_like(s_ref)

    tri = tri_ref[...]
    trit = trit_ref[...]
    tri_b = tri > 0
    lane_head = lax.broadcasted_iota(jnp.int32, (1, H * DK), 1) // DK
    row_head = lax.broadcasted_iota(jnp.int32, (H * PAIR, 1), 0) // PAIR
    qmask = row_head == lane_head
    row_first = lax.broadcasted_iota(jnp.int32, (PAIR, 1), 0) < GLA_CHUNK
    row_first4 = (lax.broadcasted_iota(jnp.int32, (H * PAIR, 1), 0) % PAIR) < GLA_CHUNK
    lane_first = lax.broadcasted_iota(jnp.int32, (1, PAIR), 1) < GLA_CHUNK
    scale = DK ** -0.5
    gout = gout_ref[...]

    def pair(p, carry):
        r0 = pl.multiple_of(p * PAIR, PAIR)
        q = q_ref[0, pl.ds(r0, PAIR), :]
        k = k_ref[0, pl.ds(r0, PAIR), :]
        g = g_ref[0, pl.ds(r0, PAIR), :]
        kt = kt_ref[0, p]
        gt = gt_ref[0, p]
        v = v_ref[0, pl.ds(r0, PAIR), :]

        g_hi, g_lo = _split(g)
        gc = _mm(tri, g_hi) + _mm(tri, g_lo)
        gt_hi, gt_lo = _split(gt)
        gct = _mm(gt_hi, trit) + _mm(gt_lo, trit)
        g_last = jnp.where(row_first, gc[GLA_CHUNK - 1:GLA_CHUNK, :], gc[PAIR - 1:PAIR, :])
        gl0 = gct[:, GLA_CHUNK - 1:GLA_CHUNK]
        gl1 = gct[:, PAIR - 1:PAIR]
        g_last_t = jnp.where(lane_first, gl0, gl1)

        q_e = (q * (jnp.exp(gc) * scale)).astype(bf16)
        k_e = (k * jnp.exp(-gc)).astype(bf16)
        ks_t = kt * jnp.exp(g_last_t - gct)
        ks_t0 = jnp.where(lane_first, ks_t, 0.0).astype(bf16)
        ks_t1 = jnp.where(lane_first, 0.0, ks_t).astype(bf16)
        del g_last

        qm = jnp.where(qmask, jnp.concatenate([q_e] * H, axis=0), jnp.zeros((), bf16))
        a = _nt(qm, k_e)
        s0 = s_ref[...]

        u0 = []
        u1 = []
        for h in range(H):
            v_h = v[:, h * DV:(h + 1) * DV]
            u0.append(_mm(ks_t0[h * DK:(h + 1) * DK], v_h))
            u1.append(_mm(ks_t1[h * DK:(h + 1) * DK], v_h))
        u0 = jnp.concatenate(u0, axis=0)
        u1 = jnp.concatenate(u1, axis=0)
        s1 = s0 * jnp.exp(gl0) + u0
        s_ref[...] = s1 * jnp.exp(gl1) + u1

        o_inter = jnp.where(row_first4, _mm(qm, s0.astype(bf16)), _mm(qm, s1.astype(bf16)))
        for h in range(H):
            a_h = jnp.where(tri_b, a[h * PAIR:(h + 1) * PAIR], 0.0).astype(bf16)
            o_h = _mm(a_h, v[:, h * DV:(h + 1) * DV]) + o_inter[h * PAIR:(h + 1) * PAIR]
            ms = jnp.mean(o_h * o_h, axis=-1, keepdims=True)
            o_n = o_h * lax.rsqrt(ms + EPS) * gout
            r_h = r_ref[0, pl.ds(r0, PAIR), h * DV:(h + 1) * DV]
            o_ref[0, pl.ds(r0, PAIR), h * DV:(h + 1) * DV] = (o_n * _silu(r_h)).astype(bf16)
        return carry

    lax.fori_loop(0, n_pairs, pair, 0)


def _gla(qg, kg, gk, kgt, gkt, vg, rg, g_gla_out):
    B, S, _ = qg.shape
    tg = min(TG_GLA, S)
    r = jnp.arange(PAIR)
    tri = ((r[:, None] // GLA_CHUNK == r[None, :] // GLA_CHUNK) & (r[None, :] <= r[:, None])).astype(bf16)
    row = lambda w: pl.BlockSpec((1, tg, w), lambda b, i: (b, i, 0))
    colT = pl.BlockSpec((1, tg // PAIR, GLA_QK_W, PAIR), lambda b, i: (b, i, 0, 0))
    const = lambda shape: pl.BlockSpec(shape, lambda b, i: (0,) * len(shape))
    return pl.pallas_call(
        functools.partial(_gla_kernel, n_pairs=tg // PAIR),
        out_shape=jax.ShapeDtypeStruct((B, S, GLA_V_W), bf16),
        grid=(B, S // tg),
        in_specs=[row(GLA_QK_W), row(GLA_QK_W), row(GLA_QK_W), colT, colT,
                  row(GLA_V_W), row(GLA_V_W), const((1, GLA_DV)),
                  const((PAIR, PAIR)), const((PAIR, PAIR))],
        out_specs=row(GLA_V_W),
        scratch_shapes=[pltpu.VMEM((GLA_QK_W, GLA_DV), f32)],
        compiler_params=pltpu.CompilerParams(
            dimension_semantics=("arbitrary", "arbitrary"), vmem_limit_bytes=VMEM_LIMIT),
        name="gla",
    )(qg, kg, gk, kgt, gkt, vg, rg, g_gla_out.reshape(1, GLA_DV), tri, tri.T)


def _attn_finish(o, gsub_ref, o_ref, lambda_init):
    ms = jnp.mean(o * o, axis=-1, keepdims=True)
    o_ref[0] = (o * lax.rsqrt(ms + EPS) * gsub_ref[...] * (1.0 - lambda_init)).astype(bf16)


def _attn_lambda(lamv_ref, lambda_init):
    lv = lamv_ref[...]
    return (jnp.exp(jnp.sum(lv[0:1] * lv[1:2], axis=-1, keepdims=True))
            - jnp.exp(jnp.sum(lv[2:3] * lv[3:4], axis=-1, keepdims=True)) + lambda_init)


def _attn_bounded_kernel(q_ref, k_ref, v_ref, bias_ref, lamv_ref, gsub_ref, o_ref, vaug_ref, *, lambda_init):
    qi = pl.program_id(2)
    tq = q_ref.shape[1]
    S = k_ref.shape[1]

    @pl.when(qi == 0)
    def _():
        lane = lax.broadcasted_iota(jnp.int32, (S, DIFF_DV), 1)
        vaug_ref[:, :DIFF_DV] = v_ref[0]
        vaug_ref[:, DIFF_DV:] = jnp.where(lane == 0, 1.0, 0.0).astype(bf16)

    q = q_ref[0]
    lane = lax.broadcasted_iota(jnp.int32, (1, 2 * DIFF_DQK), 1)
    zero = jnp.zeros((), bf16)
    qs = (jnp.where(lane < DIFF_DQK, q, zero), jnp.where(lane < DIFF_DQK, zero, q))

    def update(accs, k0, bias):
        kb = k_ref[0, pl.ds(k0, tq), :]
        vb = vaug_ref[pl.ds(k0, tq), :]
        out = []
        for c in range(2):
            s = _nt(qs[c], kb)
            if bias is not None:
                s = s + bias[c]
            out.append(accs[c] + _mm(jnp.exp2(s).astype(bf16), vb))
        return tuple(out)

    def far(kj, accs):
        return update(accs, pl.multiple_of(kj * tq, tq), None)

    def far_group(g, accs):
        for u in range(ATTN_UNROLL):
            accs = far(g * ATTN_UNROLL + u, accs)
        return accs

    n_far = jnp.maximum(qi - 1, 0)
    n_grp = n_far // ATTN_UNROLL
    accs = lax.fori_loop(0, n_grp, far_group,
                         (jnp.zeros((tq, 2 * DIFF_DV), f32), jnp.zeros((tq, 2 * DIFF_DV), f32)))
    accs = lax.fori_loop(n_grp * ATTN_UNROLL, n_far, far, accs)
    accs = update(accs, pl.multiple_of(qi * tq, tq), (bias_ref[0, 0, 1], bias_ref[0, 1, 1]))
    has_prev = qi > 0
    accs = update(accs, pl.multiple_of(jnp.maximum(qi - 1, 0) * tq, tq),
                  (jnp.where(has_prev, bias_ref[0, 0, 0], NEG), jnp.where(has_prev, bias_ref[0, 1, 0], NEG)))
    a0, a1 = accs
    o = (a0[:, :DIFF_DV] / a0[:, DIFF_DV:DIFF_DV + 1]
         - _attn_lambda(lamv_ref, lambda_init) * (a1[:, :DIFF_DV] / a1[:, DIFF_DV:DIFF_DV + 1]))
    _attn_finish(o, gsub_ref, o_ref, lambda_init)


def _attn_kernel(q_ref, k_ref, v_ref, bias_ref, lamv_ref, gsub_ref, o_ref, *, lambda_init):
    qi = pl.program_id(2)
    tq = q_ref.shape[1]
    q = q_ref[0]
    lane = lax.broadcasted_iota(jnp.int32, (1, 2 * DIFF_DQK), 1)
    zero = jnp.zeros((), bf16)
    qs = (jnp.where(lane < DIFF_DQK, q, zero), jnp.where(lane < DIFF_DQK, zero, q))

    def update(state, kb, vb, bias):
        new = []
        for c in range(2):
            m, l, acc = state[c]
            s = _nt(qs[c], kb)
            if bias is not None:
                s = s + bias[c]
            m_new = jnp.maximum(m, jnp.max(s, axis=-1, keepdims=True))
            alpha = jnp.exp2(m - m_new)
            p = jnp.exp2(s - m_new)
            l = alpha * l + jnp.sum(p, axis=-1, keepdims=True)
            acc = alpha * acc + _mm(p.astype(bf16), vb)
            new.append((m_new, l, acc))
        return tuple(new)

    init = tuple((jnp.full((tq, 1), NEG, f32), jnp.zeros((tq, 1), f32), jnp.zeros((tq, DIFF_DV), f32))
                 for _ in range(2))

    def far(kj, state):
        k0 = pl.multiple_of(kj * tq, tq)
        return update(state, k_ref[0, pl.ds(k0, tq), :], v_ref[0, pl.ds(k0, tq), :], None)

    state = lax.fori_loop(0, jnp.maximum(qi - 1, 0), far, init)

    kd0 = pl.multiple_of(qi * tq, tq)
    state = update(state, k_ref[0, pl.ds(kd0, tq), :], v_ref[0, pl.ds(kd0, tq), :],
                   (bias_ref[0, 0, 1], bias_ref[0, 1, 1]))
    kp0 = pl.multiple_of(jnp.maximum(qi - 1, 0) * tq, tq)
    has_prev = qi > 0
    state = update(state, k_ref[0, pl.ds(kp0, tq), :], v_ref[0, pl.ds(kp0, tq), :],
                   (jnp.where(has_prev, bias_ref[0, 0, 0], NEG), jnp.where(has_prev, bias_ref[0, 1, 0], NEG)))

    (_, l0, a0), (_, l1, a1) = state
    o = a0 / l0 - _attn_lambda(lamv_ref, lambda_init) * (a1 / l1)
    _attn_finish(o, gsub_ref, o_ref, lambda_init)


def _t5_bucket(n):
    max_exact = NUM_BUCKETS // 2
    nf = jnp.maximum(n, 1).astype(f32)
    large = max_exact + (jnp.log(nf / max_exact) / math.log(MAX_DISTANCE / max_exact)
                         * (NUM_BUCKETS - max_exact)).astype(jnp.int32)
    large = jnp.minimum(large, NUM_BUCKETS - 1)
    return jnp.where(n < max_exact, n, large)


def _toeplitz(vec, n):
    w = jnp.concatenate([vec[:, ::-1], jnp.zeros((vec.shape[0], 1), vec.dtype)], axis=1)
    flat = jnp.tile(w, (1, n))[:, :n * (2 * n - 1)].reshape(-1, n, 2 * n - 1)
    return flat[:, :, n - 1:]


def _bias_tiles(rel_bias_table, S, n):
    HM = rel_bias_table.shape[1]
    assert n >= MAX_DISTANCE
    d = jnp.arange(2 * n, dtype=jnp.int32)
    by_dist = rel_bias_table[_t5_bucket(d)].astype(f32).T
    rel = (by_dist - rel_bias_table[NUM_BUCKETS - 1].astype(f32)[:, None]) * LOG2E
    diag = _toeplitz(jnp.concatenate([jnp.full((HM, n - 1), NEG, f32), rel[:, :n]], axis=1), n)
    prev = _toeplitz(rel[:, 1:2 * n], n)
    return jnp.stack([prev, diag], axis=1).reshape(HM // 2, 2, 2, n, n)


def _attn(qd, kd, vd, bias_tiles, lamv, g_subln, lambda_init, bounded):
    B, S, _ = qd.shape
    H = N_DIFF_HEADS
    tq = min(TQ, S)
    body = _attn_bounded_kernel if bounded else _attn_kernel
    scratch = [pltpu.VMEM((S, 2 * DIFF_DV), bf16)] if bounded else []
    return pl.pallas_call(
        functools.partial(body, lambda_init=lambda_init),
        out_shape=jax.ShapeDtypeStruct((B, S, DIFF_V_W), bf16),
        scratch_shapes=scratch,
        grid=(B, H, S // tq),
        in_specs=[pl.BlockSpec((1, tq, 2 * DIFF_DQK), lambda b, h, i: (b, i, h)),
                  pl.BlockSpec((1, S, 2 * DIFF_DQK), lambda b, h, i: (b, 0, h)),
                  pl.BlockSpec((1, S, DIFF_DV), lambda b, h, i: (b, 0, h)),
                  pl.BlockSpec((1, 2, 2, tq, tq), lambda b, h, i: (h, 0, 0, 0, 0)),
                  pl.BlockSpec((4, DIFF_DQK), lambda b, h, i: (0, 0)),
                  pl.BlockSpec((1, DIFF_DV), lambda b, h, i: (0, 0))],
        out_specs=pl.BlockSpec((1, tq, DIFF_DV), lambda b, h, i: (b, i, h)),
        compiler_params=pltpu.CompilerParams(
            dimension_semantics=("arbitrary", "arbitrary", "arbitrary"), vmem_limit_bytes=VMEM_LIMIT),
        name="attn_bounded" if bounded else "attn_online",
    )(qd, kd, vd, bias_tiles, lamv, g_subln.reshape(1, DIFF_DV))


def _scores_bounded(rel_bias_table, g_qnorm, g_knorm):
    qk = DIFF_DQK ** 0.5 * jnp.max(jnp.abs(g_qnorm)) * jnp.max(jnp.abs(g_knorm)) * 1.02
    rel = jnp.max(jnp.abs(rel_bias_table - rel_bias_table[NUM_BUCKETS - 1:]))
    return qk + rel <= SAFE_SCORE


def _rows_to_tiles(x, ref):
    n = x.shape[0]
    for c in range(ROW_TILE):
        ref[pl.ds(c, n, stride=ROW_TILE), :] = x[:, c * LANES:(c + 1) * LANES]


def _tiles_to_rows(ref, n):
    return jnp.concatenate([ref[pl.ds(c, n, stride=ROW_TILE), :] for c in range(ROW_TILE)], axis=1)


def _outproj_kernel(og_ref, od_ref, x_ref, mod_ref, wo_ref, g2_ref, wr_ref, br_ref,
                    x1_ref, hp_ref, lg_ref):
    half = og_ref.shape[2]
    mix = _mm(og_ref[0], wo_ref[:half, :]) + _mm(od_ref[0], wo_ref[half:, :])
    x1 = x_ref[0] + mod_ref[0, 2:3, :] * mix
    x1_ref[0] = x1
    ms = jnp.mean(x1 * x1, axis=-1, keepdims=True)
    y = x1 * lax.rsqrt(ms + EPS) * g2_ref[...]
    h = (y * (1.0 + mod_ref[0, 4:5, :]) + mod_ref[0, 3:4, :]).astype(bf16)
    lg_ref[...] = _mm(h, wr_ref[...]) + br_ref[...]
    _rows_to_tiles(h.astype(f32), hp_ref)


def _outproj(og, od, x, mod, w_out, g_norm2, w_router, b_router):
    B, S, D = x.shape
    E = w_router.shape[1]
    tm = TM_IN
    nj = S // tm
    w_r = jnp.zeros((D, LANES), f32).at[:, :E].set(w_router).astype(bf16)
    b_r = jnp.full((1, LANES), NEG, f32).at[0, :E].set(b_router)
    const = lambda shape: pl.BlockSpec(shape, lambda b, i: (0,) * len(shape))
    return pl.pallas_call(
        _outproj_kernel,
        out_shape=[jax.ShapeDtypeStruct((B, S, D), f32),
                   jax.ShapeDtypeStruct((B * S * ROW_TILE, LANES), f32),
                   jax.ShapeDtypeStruct((B * S, LANES), f32)],
        grid=(B, nj),
        in_specs=[pl.BlockSpec((1, tm, og.shape[2]), lambda b, i: (b, i, 0)),
                  pl.BlockSpec((1, tm, od.shape[2]), lambda b, i: (b, i, 0)),
                  pl.BlockSpec((1, tm, D), lambda b, i: (b, i, 0)),
                  pl.BlockSpec((1, 6, D), lambda b, i: (b, 0, 0)),
                  const((w_out.shape[0], D)), const((1, D)), const((D, LANES)), const((1, LANES))],
        out_specs=[pl.BlockSpec((1, tm, D), lambda b, i: (b, i, 0)),
                   pl.BlockSpec((tm * ROW_TILE, LANES), lambda b, i: (b * nj + i, 0)),
                   pl.BlockSpec((tm, LANES), lambda b, i: (b * nj + i, 0))],
        compiler_params=pltpu.CompilerParams(
            dimension_semantics=("arbitrary", "arbitrary"), vmem_limit_bytes=VMEM_LIMIT),
        name="outproj",
    )(og, od, x, mod, w_out.astype(bf16), g_norm2.reshape(1, D), w_r, b_r)


def _route_kernel(lg_ref, lt_ref, ri_ref, rw_ref, cnt_ref, run_ref):
    @pl.when(pl.program_id(0) == 0)
    def _():
        run_ref[...] = jnp.zeros_like(run_ref)

    x = lg_ref[...]
    tr = x.shape[0]
    lane = lax.broadcasted_iota(jnp.int32, (tr, LANES), 1)
    lane_f = lane.astype(f32)
    vals, hots, idxs = [], [], []
    for _ in range(TOP_K):
        m = jnp.max(x, axis=-1, keepdims=True)
        idx = jnp.min(jnp.where(x == m, lane_f, float(LANES)), axis=-1, keepdims=True)
        hot = lane_f == idx
        x = jnp.where(hot, -jnp.inf, x)
        vals.append(m)
        hots.append(hot)
        idxs.append(idx.astype(jnp.int32))
    ex = [jnp.exp(v - vals[0]) for v in vals]
    den = ex[0] + ex[1] + ex[2] + ex[3]
    sel = (hots[0] | hots[1] | hots[2] | hots[3]).astype(f32)
    rank = _mm(lt_ref[...], sel.astype(bf16)) + run_ref[...]
    run_ref[...] = run_ref[...] + jnp.sum(sel, axis=0, keepdims=True)
    cnt_ref[...] = run_ref[...]
    ri = jnp.zeros((tr, LANES), jnp.int32)
    rw = jnp.zeros((tr, LANES), f32)
    for k in range(TOP_K):
        rk = jnp.sum(jnp.where(hots[k], rank, 0.0), axis=-1, keepdims=True).astype(jnp.int32)
        ri = jnp.where(lane == k, rk, ri)
        ri = jnp.where(lane == TOP_K + k, idxs[k], ri)
        rw = jnp.where(lane == k, ex[k] / den, rw)
    ri_ref[...] = ri
    rw_ref[...] = rw


def _route(logits):
    T = logits.shape[0]
    tr = min(TR, T)
    r = jnp.arange(tr)
    lt = (r[None, :] < r[:, None]).astype(bf16)
    return pl.pallas_call(
        _route_kernel,
        out_shape=[jax.ShapeDtypeStruct((T, LANES), jnp.int32),
                   jax.ShapeDtypeStruct((T, LANES), f32),
                   jax.ShapeDtypeStruct((1, LANES), f32)],
        grid=(T // tr,),
        in_specs=[pl.BlockSpec((tr, LANES), lambda i: (i, 0)),
                  pl.BlockSpec((tr, tr), lambda i: (0, 0))],
        out_specs=[pl.BlockSpec((tr, LANES), lambda i: (i, 0)),
                   pl.BlockSpec((tr, LANES), lambda i: (i, 0)),
                   pl.BlockSpec((1, LANES), lambda i: (0, 0))],
        scratch_shapes=[pltpu.VMEM((1, LANES), f32)],
        compiler_params=pltpu.CompilerParams(dimension_semantics=("arbitrary",)),
        name="route",
    )(logits, lt)


def _tile(ref, t):
    return ref.at[pl.ds(pl.multiple_of(t * ROW_TILE, ROW_TILE), ROW_TILE)]


def _dispatch_kernel(pend_ref, cnt_ref, dest_ref, h_ref, xs_ref, zero_ref, sem, zsem):
    n_tok = h_ref.shape[0] // ROW_TILE
    blk_rows = FFN_BLK * ROW_TILE

    @pl.when(pl.program_id(0) == 0)
    def _():
        zero_ref[...] = jnp.zeros_like(zero_ref)
        n_exp = pend_ref.shape[0]

        def last_block(e):
            return xs_ref.at[pl.ds(pl.multiple_of((pend_ref[e] - FFN_BLK) * ROW_TILE, blk_rows), blk_rows)]

        def zfill(e, c):
            @pl.when(cnt_ref[e] > 0)
            def _():
                pltpu.make_async_copy(zero_ref, last_block(e), zsem).start()
            return c

        def zwait(e, c):
            @pl.when(cnt_ref[e] > 0)
            def _():
                pltpu.make_async_copy(zero_ref, last_block(e), zsem).wait()
            return c

        lax.fori_loop(0, n_exp, zfill, 0)
        lax.fori_loop(0, n_exp, zwait, 0)

    def issue(g, c):
        for u in range(DMA_UNROLL):
            r = g * DMA_UNROLL + u
            for k in range(TOP_K):
                pltpu.make_async_copy(_tile(h_ref, r), _tile(xs_ref, dest_ref[r * TOP_K + k]),
                                      sem).start(priority=k % 2)
        return c

    lax.fori_loop(0, n_tok // DMA_UNROLL, issue, 0)
    done = xs_ref.at[pl.ds(0, n_tok * TOP_K * ROW_TILE)]
    pltpu.make_async_copy(done, done, sem).wait()


def _dispatch(p_ends, counts, dest_flat, hp, n_rows):
    T = hp.shape[0] // ROW_TILE
    grid_spec = pltpu.PrefetchScalarGridSpec(
        num_scalar_prefetch=2,
        grid=(T // TD,),
        in_specs=[pl.BlockSpec((TD * TOP_K,), lambda i, pe, cn: (i,), memory_space=pltpu.SMEM),
                  pl.BlockSpec((TD * ROW_TILE, LANES), lambda i, pe, cn: (i, 0))],
        out_specs=pl.BlockSpec(memory_space=pl.ANY),
        scratch_shapes=[pltpu.VMEM((FFN_BLK * ROW_TILE, LANES), f32),
                        pltpu.SemaphoreType.DMA(()), pltpu.SemaphoreType.DMA(())],
    )
    return pl.pallas_call(
        _dispatch_kernel,
        out_shape=jax.ShapeDtypeStruct((n_rows * ROW_TILE, LANES), f32),
        grid_spec=grid_spec,
        compiler_params=pltpu.CompilerParams(dimension_semantics=("arbitrary",)),
        name="dispatch",
    )(p_ends, counts, dest_flat, hp)


def _ffn_kernel(be_ref, nu_ref, xs_ref, wgu32_ref, bgu_ref, wd32_ref, bd_ref, ys_ref, wgu_ref, wd_ref):
    i = pl.program_id(0)
    used = i < nu_ref[0]
    new_expert = (i == 0) | (be_ref[i] != be_ref[jnp.maximum(i - 1, 0)])

    @pl.when(used & new_expert)
    def _():
        rows = 128

        def cast(src, dst):
            def body(r, c):
                r0 = pl.multiple_of(r * rows, rows)
                dst[pl.ds(r0, rows), :] = src[0, pl.ds(r0, rows), :].astype(bf16)
                return c
            lax.fori_loop(0, src.shape[1] // rows, body, 0)
        cast(wgu32_ref, wgu_ref)
        cast(wd32_ref, wd_ref)

    @pl.when(used)
    def _():
        F = wd_ref.shape[0]
        xrow = _tiles_to_rows(xs_ref, FFN_BLK).astype(bf16)
        acc = None
        fc = F // 2
        for c in range(2):
            def gu(col0):
                return _mm(xrow, wgu_ref[:, col0:col0 + fc]) + bgu_ref[0, :, col0:col0 + fc]
            gate = jnp.minimum(gu(c * fc), SWIGLU_LIMIT)
            up = jnp.clip(gu(F + c * fc), -SWIGLU_LIMIT, SWIGLU_LIMIT)
            y = (up + 1.0) * (gate * jax.nn.sigmoid(SWIGLU_ALPHA * gate))
            part = _mm(y.astype(bf16), wd_ref[c * fc:(c + 1) * fc, :])
            acc = part if acc is None else acc + part
        _rows_to_tiles(acc + bd_ref[0], ys_ref)

    @pl.when(jnp.logical_not(used))
    def _():
        ys_ref[...] = jnp.zeros_like(ys_ref)


def _ffn(block_e, n_used, xs, w_gate_up, b_gate_up, w_down, b_down):
    E, D, F2 = w_gate_up.shape
    F = F2 // 2
    P = xs.shape[0] // ROW_TILE
    nb = P // FFN_BLK
    rows = FFN_BLK * ROW_TILE

    def blk(i, be, nu):
        return jnp.minimum(i, nu[0] - 1)

    grid_spec = pltpu.PrefetchScalarGridSpec(
        num_scalar_prefetch=2,
        grid=(nb,),
        in_specs=[pl.BlockSpec((rows, LANES), lambda i, be, nu: (blk(i, be, nu), 0)),
                  pl.BlockSpec((1, D, F2), lambda i, be, nu: (be[blk(i, be, nu)], 0, 0)),
                  pl.BlockSpec((1, 1, F2), lambda i, be, nu: (be[blk(i, be, nu)], 0, 0)),
                  pl.BlockSpec((1, F, D), lambda i, be, nu: (be[blk(i, be, nu)], 0, 0)),
                  pl.BlockSpec((1, 1, D), lambda i, be, nu: (be[blk(i, be, nu)], 0, 0))],
        out_specs=pl.BlockSpec((rows, LANES), lambda i, be, nu: (i, 0)),
        scratch_shapes=[pltpu.VMEM((D, F2), bf16), pltpu.VMEM((F, D), bf16)],
    )
    return pl.pallas_call(
        _ffn_kernel,
        out_shape=jax.ShapeDtypeStruct((P * ROW_TILE, LANES), f32),
        grid_spec=grid_spec,
        compiler_params=pltpu.CompilerParams(
            dimension_semantics=("arbitrary",), vmem_limit_bytes=VMEM_LIMIT_FFN),
        name="ffn",
    )(block_e, n_used, xs, w_gate_up, b_gate_up.reshape(E, 1, F2), w_down, b_down.reshape(E, 1, D))


def _combine_kernel(dcur_ref, dnext_ref, ys_ref, x1_ref, rw_ref, rep_ref, mod_ref, o_ref, buf, acc_ref, sem):
    step = pl.program_id(0) * pl.num_programs(1) + pl.program_id(1)
    n_steps = pl.num_programs(0) * pl.num_programs(1)
    slot = step % 2

    def issue(dref, s):
        def body(g, c):
            for u in range(DMA_UNROLL):
                r = g * DMA_UNROLL + u
                for k in range(TOP_K):
                    pltpu.make_async_copy(_tile(ys_ref, dref[r * TOP_K + k]),
                                          _tile(buf.at[s, k], r), sem.at[s]).start(priority=k % 2)
            return c
        lax.fori_loop(0, TD // DMA_UNROLL, body, 0)

    @pl.when(step == 0)
    def _():
        issue(dcur_ref, 0)

    @pl.when(step + 1 < n_steps)
    def _():
        issue(dnext_ref, 1 - slot)

    pltpu.make_async_copy(buf.at[slot], buf.at[slot], sem.at[slot]).wait()

    rw_hi, rw_lo = _split(rw_ref[...])
    rw8 = _mm(rep_ref[...], rw_hi) + _mm(rep_ref[...], rw_lo)
    moe = rw8[:, 0:1] * buf[slot, 0]
    for k in range(1, TOP_K):
        moe = moe + rw8[:, k:k + 1] * buf[slot, k]
    acc_ref[...] = moe
    o_ref[0] = x1_ref[0] + mod_ref[0, 5:6, :] * _tiles_to_rows(acc_ref, TD)


def _combine(dest_flat, ys, x1, rw, mod):
    B, S, D = x1.shape
    nj = S // TD
    n_steps = B * nj
    rep = (jnp.arange(TD * ROW_TILE)[:, None] // ROW_TILE == jnp.arange(TD)[None, :]).astype(bf16)
    return pl.pallas_call(
        _combine_kernel,
        out_shape=jax.ShapeDtypeStruct((B, S, D), f32),
        grid=(B, nj),
        in_specs=[pl.BlockSpec((TD * TOP_K,), lambda b, j: (b * nj + j,), memory_space=pltpu.SMEM),
                  pl.BlockSpec((TD * TOP_K,), lambda b, j: (jnp.minimum(b * nj + j + 1, n_steps - 1),),
                               memory_space=pltpu.SMEM),
                  pl.BlockSpec(memory_space=pl.ANY),
                  pl.BlockSpec((1, TD, D), lambda b, j: (b, j, 0)),
                  pl.BlockSpec((TD, LANES), lambda b, j: (b * nj + j, 0)),
                  pl.BlockSpec((TD * ROW_TILE, TD), lambda b, j: (0, 0)),
                  pl.BlockSpec((1, 6, D), lambda b, j: (b, 0, 0))],
        out_specs=pl.BlockSpec((1, TD, D), lambda b, j: (b, j, 0)),
        scratch_shapes=[pltpu.VMEM((2, TOP_K, TD * ROW_TILE, LANES), f32),
                        pltpu.VMEM((TD * ROW_TILE, LANES), f32), pltpu.SemaphoreType.DMA((2,))],
        compiler_params=pltpu.CompilerParams(
            dimension_semantics=("arbitrary", "arbitrary"), vmem_limit_bytes=VMEM_LIMIT),
        name="combine",
    )(dest_flat, dest_flat, ys, x1, rw, rep, mod)


def _moe(hp, logits, x1, mod, w_gate_up, b_gate_up, w_down, b_down):
    T = logits.shape[0]
    E = w_gate_up.shape[0]
    ri, rw, cnt = _route(logits)
    rank = ri[:, :TOP_K]
    e_sel = ri[:, TOP_K:2 * TOP_K]
    counts = cnt[0, :E].astype(jnp.int32)
    padded = ((counts + FFN_BLK - 1) // FFN_BLK) * FFN_BLK
    p_ends = jnp.cumsum(padded)
    p_starts = p_ends - padded
    nb = -(-T * TOP_K // FFN_BLK) + E
    n_used = jnp.maximum(p_ends[-1:] // FFN_BLK, 1).astype(jnp.int32)
    blk_start = jnp.arange(nb, dtype=jnp.int32) * FFN_BLK
    block_e = jnp.minimum(jnp.sum(p_ends[None, :] <= blk_start[:, None], axis=1), E - 1).astype(jnp.int32)
    onehot = e_sel[:, :, None] == jnp.arange(E, dtype=jnp.int32)[None, None, :]
    dest = (jnp.sum(jnp.where(onehot, p_starts[None, None, :], 0), axis=-1) + rank).reshape(-1)
    xs = _dispatch(p_ends.astype(jnp.int32), counts, dest, hp, nb * FFN_BLK)
    ys = _ffn(block_e, n_used, xs, w_gate_up, b_gate_up, w_down, b_down)
    return _combine(dest, ys, x1, rw, mod)


def kernel(x, c, rel_bias_table, w_ada, b_ada, g_norm1, w_in, w_gk_up, b_gk_up, g_gla_out, g_qnorm, g_knorm, lambda_q1, lambda_k1, lambda_q2, lambda_k2, g_subln, w_out, g_norm2, w_router, b_router, w_gate_up, b_gate_up, w_down, b_down):
    B, S, D = x.shape
    depth = w_ada.shape[0]
    bias_tiles = _bias_tiles(rel_bias_table, S, min(TQ, S))
    for l in range(depth):
        lambda_init = 0.8 - 0.6 * math.exp(-0.3 * l)
        mod = _ada(c, w_ada[l], b_ada[l])
        qg, kg, gk, kgt, gkt, vg, rg, qd, kd, vd = _inproj(
            x, mod, g_norm1[l], w_in[l], w_gk_up[l], b_gk_up[l], g_qnorm[l], g_knorm[l])
        og = _gla(qg, kg, gk, kgt, gkt, vg, rg, g_gla_out[l])
        lamv = jnp.stack([lambda_q1[l], lambda_k1[l], lambda_q2[l], lambda_k2[l]]).astype(f32)
        od = lax.cond(_scores_bounded(rel_bias_table, g_qnorm[l], g_knorm[l]),
                      functools.partial(_attn, lambda_init=lambda_init, bounded=True),
                      functools.partial(_attn, lambda_init=lambda_init, bounded=False),
                      qd, kd, vd, bias_tiles, lamv, g_subln[l])
        x1, hp, logits = _outproj(og, od, x, mod, w_out[l], g_norm2[l], w_router[l], b_router[l])
        x = _moe(hp, logits, x1, mod, w_gate_up[l], b_gate_up[l], w_down[l], b_down[l])
    return x
```

```python
import functools
import math

import jax
import jax.numpy as jnp
from jax import lax
from jax.experimental import pallas as pl
from jax.experimental.pallas import tpu as pltpu

f32 = jnp.float32
bf16 = jnp.bfloat16

N_GLA_HEADS = 4
GLA_DK = 64
GLA_DV = 128
GLA_GATE_RANK = 16
GLA_GATE_NORM = 16.0
GLA_CHUNK = 64
N_DIFF_HEADS = 4
DIFF_DQK = 64
DIFF_DV = 128
NUM_BUCKETS = 32
MAX_DISTANCE = 128
TOP_K = 4
SWIGLU_LIMIT = 7.0
SWIGLU_ALPHA = 1.702
EPS = 1e-6

GLA_QK_W = N_GLA_HEADS * GLA_DK
GLA_V_W = N_GLA_HEADS * GLA_DV
DIFF_QK_W = N_DIFF_HEADS * 2 * DIFF_DQK
DIFF_V_W = N_DIFF_HEADS * DIFF_DV

LANES = 128
NEG = -1e30
LOG2E = math.log2(math.e)
SAFE_SCORE = 40.0
VMEM_LIMIT = 48 * 1024 * 1024
VMEM_LIMIT_FFN = 58 * 1024 * 1024

TM_IN = 512
TG_GLA = 1024
PAIR = 2 * GLA_CHUNK
TQ = 512
ATTN_UNROLL = 4
TR = 512
TD = 256
ROW_TILE = 8
DMA_UNROLL = 8
FFN_BLK = 512


def _nt(a, b):
    return lax.dot_general(a, b, (((1,), (1,)), ((), ())), preferred_element_type=f32)


def _mm(a, b):
    return jnp.dot(a, b, preferred_element_type=f32)


def _split(x):
    hi = x.astype(bf16)
    lo = (x - hi.astype(f32)).astype(bf16)
    return hi, lo


def _silu(x):
    return x * jax.nn.sigmoid(x)


def _ada_kernel(c_ref, w_ref, b_ref, o_ref):
    c = c_ref[...]
    o_ref[...] = _mm(_silu(c).astype(bf16), w_ref[...].astype(bf16)) + b_ref[...]


def _ada(c, w_ada, b_ada):
    B, D = c.shape
    N = w_ada.shape[1]
    bp = 8
    cp = jnp.zeros((bp, D), f32).at[:B].set(c)
    tn = 1536
    out = pl.pallas_call(
        _ada_kernel,
        out_shape=jax.ShapeDtypeStruct((bp, N), f32),
        grid=(N // tn,),
        in_specs=[pl.BlockSpec((bp, D), lambda j: (0, 0)),
                  pl.BlockSpec((D, tn), lambda j: (0, j)),
                  pl.BlockSpec((1, tn), lambda j: (0, j))],
        out_specs=pl.BlockSpec((bp, tn), lambda j: (0, j)),
        compiler_params=pltpu.CompilerParams(vmem_limit_bytes=VMEM_LIMIT),
        name="ada",
    )(cp, w_ada, b_ada.reshape(1, N))
    return out[:B].reshape(B, 6, D)


def _inproj_kernel(x_ref, mod_ref, g1_ref, wm_ref, wkt_ref, wlo_ref, wup_ref, wupt_ref,
                   bup_ref, bupt_ref, gqk_ref, grp_ref, grpt_ref,
                   qg_ref, kg_ref, gk_ref, kgt_ref, gkt_ref, vg_ref, rg_ref,
                   qd_ref, kd_ref, vd_ref):
    x = x_ref[0]
    ms = jnp.mean(x * x, axis=-1, keepdims=True)
    y = x * lax.rsqrt(ms + EPS) * g1_ref[...]
    h = (y * (1.0 + mod_ref[0, 1:2, :]) + mod_ref[0, 0:1, :]).astype(bf16)

    def proj(a, b):
        return _mm(h, wm_ref[:, a:b])

    o = 0
    qg_ref[0] = proj(o, o + GLA_QK_W); o += GLA_QK_W
    kg_ref[0] = proj(o, o + GLA_QK_W); o += GLA_QK_W
    vg_ref[0] = proj(o, o + GLA_V_W).astype(bf16); o += GLA_V_W
    rg_ref[0] = proj(o, o + GLA_V_W); o += GLA_V_W
    qk = proj(o, o + 2 * DIFF_QK_W); o += 2 * DIFF_QK_W
    vd_ref[0] = proj(o, o + DIFF_V_W).astype(bf16)

    kgt = _nt(wkt_ref[...], h)
    for j in range(kgt.shape[1] // PAIR):
        kgt_ref[0, j] = kgt[:, j * PAIR:(j + 1) * PAIR]

    lo = _mm(h, wlo_ref[...]).astype(bf16)
    z = _mm(lo, wup_ref[...]) + bup_ref[...]
    gk_ref[0] = (jnp.minimum(z, 0.0) - jnp.log1p(jnp.exp(-jnp.abs(z)))) * (1.0 / GLA_GATE_NORM)
    zt = _nt(wupt_ref[...], lo) + bupt_ref[...]
    gkt = (jnp.minimum(zt, 0.0) - jnp.log1p(jnp.exp(-jnp.abs(zt)))) * (1.0 / GLA_GATE_NORM)
    for j in range(gkt.shape[1] // PAIR):
        gkt_ref[0, j] = gkt[:, j * PAIR:(j + 1) * PAIR]

    sq_hi, sq_lo = _split(qk * qk)
    gs = _mm(sq_hi, grp_ref[...]) + _mm(sq_lo, grp_ref[...])
    r = lax.rsqrt(gs * (1.0 / DIFF_DQK) + EPS)
    r_hi, r_lo = _split(r)
    rb = _mm(r_hi, grpt_ref[...]) + _mm(r_lo, grpt_ref[...])
    qkn = qk * rb * gqk_ref[...]
    qd_ref[0] = qkn[:, :DIFF_QK_W].astype(bf16)
    kd_ref[0] = qkn[:, DIFF_QK_W:].astype(bf16)


def _inproj(x, mod, g_norm1, w_in, w_gk_up, b_gk_up, g_qnorm, g_knorm):
    B, S, D = x.shape
    offs = [0]
    for w in (GLA_QK_W, GLA_QK_W, GLA_V_W, GLA_V_W, GLA_GATE_RANK, DIFF_QK_W, DIFF_QK_W, DIFF_V_W):
        offs.append(offs[-1] + w)
    w_main = jnp.concatenate([w_in[:, offs[0]:offs[4]], w_in[:, offs[5]:offs[8]]], axis=1).astype(bf16)
    w_kt = w_in[:, offs[1]:offs[2]].T.astype(bf16)
    w_lo = jnp.zeros((D, LANES), f32).at[:, :GLA_GATE_RANK].set(w_in[:, offs[4]:offs[5]]).astype(bf16)
    w_up = jnp.zeros((LANES, GLA_QK_W), f32).at[:GLA_GATE_RANK].set(w_gk_up).astype(bf16)
    w_upt = w_up.T
    b_up = b_gk_up.reshape(1, GLA_QK_W)
    b_upt = b_gk_up.reshape(GLA_QK_W, 1)
    n_grp = 2 * DIFF_QK_W // DIFF_DQK
    gqk = jnp.concatenate([jnp.tile(g_qnorm, n_grp // 2) * (DIFF_DQK ** -0.5 * LOG2E),
                           jnp.tile(g_knorm, n_grp // 2)]).reshape(1, 2 * DIFF_QK_W)
    grp = (jnp.arange(2 * DIFF_QK_W)[:, None] // DIFF_DQK == jnp.arange(LANES)[None, :]).astype(bf16)
    grpt = grp.T
    nw = w_main.shape[1]
    tm = TM_IN
    const = lambda shape: pl.BlockSpec(shape, lambda b, i: (0,) * len(shape))
    row = lambda w: pl.BlockSpec((1, tm, w), lambda b, i: (b, i, 0))
    colT = pl.BlockSpec((1, tm // PAIR, GLA_QK_W, PAIR), lambda b, i: (b, i, 0, 0))
    outs = pl.pallas_call(
        _inproj_kernel,
        out_shape=[jax.ShapeDtypeStruct((B, S, GLA_QK_W), f32),
                   jax.ShapeDtypeStruct((B, S, GLA_QK_W), f32),
                   jax.ShapeDtypeStruct((B, S, GLA_QK_W), f32),
                   jax.ShapeDtypeStruct((B, S // PAIR, GLA_QK_W, PAIR), f32),
                   jax.ShapeDtypeStruct((B, S // PAIR, GLA_QK_W, PAIR), f32),
                   jax.ShapeDtypeStruct((B, S, GLA_V_W), bf16),
                   jax.ShapeDtypeStruct((B, S, GLA_V_W), f32),
                   jax.ShapeDtypeStruct((B, S, DIFF_QK_W), bf16),
                   jax.ShapeDtypeStruct((B, S, DIFF_QK_W), bf16),
                   jax.ShapeDtypeStruct((B, S, DIFF_V_W), bf16)],
        grid=(B, S // tm),
        in_specs=[row(D),
                  pl.BlockSpec((1, 6, D), lambda b, i: (b, 0, 0)),
                  const((1, D)), const((D, nw)), const((GLA_QK_W, D)), const((D, LANES)),
                  const((LANES, GLA_QK_W)), const((GLA_QK_W, LANES)),
                  const((1, GLA_QK_W)), const((GLA_QK_W, 1)),
                  const((1, 2 * DIFF_QK_W)), const((2 * DIFF_QK_W, LANES)),
                  const((LANES, 2 * DIFF_QK_W))],
        out_specs=[row(GLA_QK_W), row(GLA_QK_W), row(GLA_QK_W), colT, colT,
                   row(GLA_V_W), row(GLA_V_W), row(DIFF_QK_W), row(DIFF_QK_W), row(DIFF_V_W)],
        compiler_params=pltpu.CompilerParams(
            dimension_semantics=("arbitrary", "arbitrary"), vmem_limit_bytes=VMEM_LIMIT),
        name="inproj",
    )(x, mod, g_norm1.reshape(1, D), w_main, w_kt, w_lo, w_up, w_upt, b_up, b_upt, gqk, grp, grpt)
    return outs


def _gla_kernel(q_ref, k_ref, g_ref, kt_ref, gt_ref, v_ref, r_ref, gout_ref, tri_ref, trit_ref,
                o_ref, s_ref, *, n_pairs):
    H, DK, DV = N_GLA_HEADS, GLA_DK, GLA_DV

    @pl.when(pl.program_id(1) == 0)
    def _():
        s_ref[...] = jnp.zeros_like(s_ref)

    tri = tri_ref[...]
    trit = trit_ref[...]
    tri_b = tri > 0
    lane_head = lax.broadcasted_iota(jnp.int32, (1, H * DK), 1) // DK
    row_head = lax.broadcasted_iota(jnp.int32, (H * PAIR, 1), 0) // PAIR
    qmask = row_head == lane_head
    row_first = lax.broadcasted_iota(jnp.int32, (PAIR, 1), 0) < GLA_CHUNK
    row_first4 = (lax.broadcasted_iota(jnp.int32, (H * PAIR, 1), 0) % PAIR) < GLA_CHUNK
    lane_first = lax.broadcasted_iota(jnp.int32, (1, PAIR), 1) < GLA_CHUNK
    scale = DK ** -0.5
    gout = gout_ref[...]

    def pair(p, carry):
        r0 = pl.multiple_of(p * PAIR, PAIR)
        q = q_ref[0, pl.ds(r0, PAIR), :]
        k = k_ref[0, pl.ds(r0, PAIR), :]
        g = g_ref[0, pl.ds(r0, PAIR), :]
        kt = kt_ref[0, p]
        gt = gt_ref[0, p]
        v = v_ref[0, pl.ds(r0, PAIR), :]

        g_hi, g_lo = _split(g)
        gc = _mm(tri, g_hi) + _mm(tri, g_lo)
        gt_hi, gt_lo = _split(gt)
        gct = _mm(gt_hi, trit) + _mm(gt_lo, trit)
        g_last = jnp.where(row_first, gc[GLA_CHUNK - 1:GLA_CHUNK, :], gc[PAIR - 1:PAIR, :])
        gl0 = gct[:, GLA_CHUNK - 1:GLA_CHUNK]
        gl1 = gct[:, PAIR - 1:PAIR]
        g_last_t = jnp.where(lane_first, gl0, gl1)

        q_e = (q * (jnp.exp(gc) * scale)).astype(bf16)
        k_e = (k * jnp.exp(-gc)).astype(bf16)
        ks_t = kt * jnp.exp(g_last_t - gct)
        ks_t0 = jnp.where(lane_first, ks_t, 0.0).astype(bf16)
        ks_t1 = jnp.where(lane_first, 0.0, ks_t).astype(bf16)
        del g_last

        qm = jnp.where(qmask, jnp.concatenate([q_e] * H, axis=0), jnp.zeros((), bf16))
        a = _nt(qm, k_e)
        s0 = s_ref[...]

        u0 = []
        u1 = []
        for h in range(H):
            v_h = v[:, h * DV:(h + 1) * DV]
            u0.append(_mm(ks_t0[h * DK:(h + 1) * DK], v_h))
            u1.append(_mm(ks_t1[h * DK:(h + 1) * DK], v_h))
        u0 = jnp.concatenate(u0, axis=0)
        u1 = jnp.concatenate(u1, axis=0)
        s1 = s0 * jnp.exp(gl0) + u0
        s_ref[...] = s1 * jnp.exp(gl1) + u1

        o_inter = jnp.where(row_first4, _mm(qm, s0.astype(bf16)), _mm(qm, s1.astype(bf16)))
        for h in range(H):
            a_h = jnp.where(tri_b, a[h * PAIR:(h + 1) * PAIR], 0.0).astype(bf16)
            o_h = _mm(a_h, v[:, h * DV:(h + 1) * DV]) + o_inter[h * PAIR:(h + 1) * PAIR]
            ms = jnp.mean(o_h * o_h, axis=-1, keepdims=True)
            o_n = o_h * lax.rsqrt(ms + EPS) * gout
            r_h = r_ref[0, pl.ds(r0, PAIR), h * DV:(h + 1) * DV]
            o_ref[0, pl.ds(r0, PAIR), h * DV:(h + 1) * DV] = (o_n * _silu(r_h)).astype(bf16)
        return carry

    lax.fori_loop(0, n_pairs, pair, 0)


def _gla(qg, kg, gk, kgt, gkt, vg, rg, g_gla_out):
    B, S, _ = qg.shape
    tg = min(TG_GLA, S)
    r = jnp.arange(PAIR)
    tri = ((r[:, None] // GLA_CHUNK == r[None, :] // GLA_CHUNK) & (r[None, :] <= r[:, None])).astype(bf16)
    row = lambda w: pl.BlockSpec((1, tg, w), lambda b, i: (b, i, 0))
    colT = pl.BlockSpec((1, tg // PAIR, GLA_QK_W, PAIR), lambda b, i: (b, i, 0, 0))
    const = lambda shape: pl.BlockSpec(shape, lambda b, i: (0,) * len(shape))
    return pl.pallas_call(
        functools.partial(_gla_kernel, n_pairs=tg // PAIR),
        out_shape=jax.ShapeDtypeStruct((B, S, GLA_V_W), bf16),
        grid=(B, S // tg),
        in_specs=[row(GLA_QK_W), row(GLA_QK_W), row(GLA_QK_W), colT, colT,
                  row(GLA_V_W), row(GLA_V_W), const((1, GLA_DV)),
                  const((PAIR, PAIR)), const((PAIR, PAIR))],
        out_specs=row(GLA_V_W),
        scratch_shapes=[pltpu.VMEM((GLA_QK_W, GLA_DV), f32)],
        compiler_params=pltpu.CompilerParams(
            dimension_semantics=("arbitrary", "arbitrary"), vmem_limit_bytes=VMEM_LIMIT),
        name="gla",
    )(qg, kg, gk, kgt, gkt, vg, rg, g_gla_out.reshape(1, GLA_DV), tri, tri.T)


def _attn_finish(o, gsub_ref, o_ref, lambda_init):
    ms = jnp.mean(o * o, axis=-1, keepdims=True)
    o_ref[0] = (o * lax.rsqrt(ms + EPS) * gsub_ref[...] * (1.0 - lambda_init)).astype(bf16)


def _attn_lambda(lamv_ref, lambda_init):
    lv = lamv_ref[...]
    return (jnp.exp(jnp.sum(lv[0:1] * lv[1:2], axis=-1, keepdims=True))
            - jnp.exp(jnp.sum(lv[2:3] * lv[3:4], axis=-1, keepdims=True)) + lambda_init)


def _attn_bounded_kernel(q_ref, k_ref, v_ref, bias_ref, lamv_ref, gsub_ref, o_ref, vaug_ref, *, lambda_init):
    qi = pl.program_id(2)
    tq = q_ref.shape[1]
    S = k_ref.shape[1]

    @pl.when(qi == 0)
    def _():
        lane = lax.broadcasted_iota(jnp.int32, (S, DIFF_DV), 1)
        vaug_ref[:, :DIFF_DV] = v_ref[0]
        vaug_ref[:, DIFF_DV:] = jnp.where(lane == 0, 1.0, 0.0).astype(bf16)

    q = q_ref[0]
    lane = lax.broadcasted_iota(jnp.int32, (1, 2 * DIFF_DQK), 1)
    zero = jnp.zeros((), bf16)
    qs = (jnp.where(lane < DIFF_DQK, q, zero), jnp.where(lane < DIFF_DQK, zero, q))

    def update(accs, k0, bias):
        kb = k_ref[0, pl.ds(k0, tq), :]
        vb = vaug_ref[pl.ds(k0, tq), :]
        out = []
        for c in range(2):
            s = _nt(qs[c], kb)
            if bias is not None:
                s = s + bias[c]
            out.append(accs[c] + _mm(jnp.exp2(s).astype(bf16), vb))
        return tuple(out)

    def far(kj, accs):
        return update(accs, pl.multiple_of(kj * tq, tq), None)

    def far_group(g, accs):
        for u in range(ATTN_UNROLL):
            accs = far(g * ATTN_UNROLL + u, accs)
        return accs

    def block_or_masked(accs, kj, bias):
        exists = kj >= 0
        k0 = pl.multiple_of(jnp.maximum(kj, 0) * tq, tq)
        if bias is None:
            tiles = (jnp.where(exists, 0.0, NEG),) * 2
        else:
            tiles = tuple(jnp.where(exists, b, NEG) for b in bias)
        return update(accs, k0, tiles)

    accs = (jnp.zeros((tq, 2 * DIFF_DV), f32), jnp.zeros((tq, 2 * DIFF_DV), f32))
    accs = update(accs, pl.multiple_of(qi * tq, tq), (bias_ref[0, 0, 1], bias_ref[0, 1, 1]))
    accs = block_or_masked(accs, qi - 1, (bias_ref[0, 0, 0], bias_ref[0, 1, 0]))
    for u in range(2, ATTN_UNROLL):
        accs = block_or_masked(accs, qi - u, None)
    n_far = jnp.maximum(qi + 1 - ATTN_UNROLL, 0)
    n_grp = n_far // ATTN_UNROLL
    accs = lax.fori_loop(0, n_grp, far_group, accs)
    accs = lax.fori_loop(n_grp * ATTN_UNROLL, n_far, far, accs)
    a0, a1 = accs
    o = (a0[:, :DIFF_DV] / a0[:, DIFF_DV:DIFF_DV + 1]
         - _attn_lambda(lamv_ref, lambda_init) * (a1[:, :DIFF_DV] / a1[:, DIFF_DV:DIFF_DV + 1]))
    _attn_finish(o, gsub_ref, o_ref, lambda_init)


def _attn_kernel(q_ref, k_ref, v_ref, bias_ref, lamv_ref, gsub_ref, o_ref, *, lambda_init):
    qi = pl.program_id(2)
    tq = q_ref.shape[1]
    q = q_ref[0]
    lane = lax.broadcasted_iota(jnp.int32, (1, 2 * DIFF_DQK), 1)
    zero = jnp.zeros((), bf16)
    qs = (jnp.where(lane < DIFF_DQK, q, zero), jnp.where(lane < DIFF_DQK, zero, q))

    def update(state, kb, vb, bias):
        new = []
        for c in range(2):
            m, l, acc = state[c]
            s = _nt(qs[c], kb)
            if bias is not None:
                s = s + bias[c]
            m_new = jnp.maximum(m, jnp.max(s, axis=-1, keepdims=True))
            alpha = jnp.exp2(m - m_new)
            p = jnp.exp2(s - m_new)
            l = alpha * l + jnp.sum(p, axis=-1, keepdims=True)
            acc = alpha * acc + _mm(p.astype(bf16), vb)
            new.append((m_new, l, acc))
        return tuple(new)

    init = tuple((jnp.full((tq, 1), NEG, f32), jnp.zeros((tq, 1), f32), jnp.zeros((tq, DIFF_DV), f32))
                 for _ in range(2))

    def far(kj, state):
        k0 = pl.multiple_of(kj * tq, tq)
        return update(state, k_ref[0, pl.ds(k0, tq), :], v_ref[0, pl.ds(k0, tq), :], None)

    state = lax.fori_loop(0, jnp.maximum(qi - 1, 0), far, init)

    kd0 = pl.multiple_of(qi * tq, tq)
    state = update(state, k_ref[0, pl.ds(kd0, tq), :], v_ref[0, pl.ds(kd0, tq), :],
                   (bias_ref[0, 0, 1], bias_ref[0, 1, 1]))
    kp0 = pl.multiple_of(jnp.maximum(qi - 1, 0) * tq, tq)
    has_prev = qi > 0
    state = update(state, k_ref[0, pl.ds(kp0, tq), :], v_ref[0, pl.ds(kp0, tq), :],
                   (jnp.where(has_prev, bias_ref[0, 0, 0], NEG), jnp.where(has_prev, bias_ref[0, 1, 0], NEG)))

    (_, l0, a0), (_, l1, a1) = state
    o = a0 / l0 - _attn_lambda(lamv_ref, lambda_init) * (a1 / l1)
    _attn_finish(o, gsub_ref, o_ref, lambda_init)


def _t5_bucket(n):
    max_exact = NUM_BUCKETS // 2
    nf = jnp.maximum(n, 1).astype(f32)
    large = max_exact + (jnp.log(nf / max_exact) / math.log(MAX_DISTANCE / max_exact)
                         * (NUM_BUCKETS - max_exact)).astype(jnp.int32)
    large = jnp.minimum(large, NUM_BUCKETS - 1)
    return jnp.where(n < max_exact, n, large)


def _toeplitz_kernel(w_ref, o_ref):
    n = o_ref.shape[-1]
    for t in range(2):
        rows = jnp.broadcast_to(w_ref[0, t:t + 1, :], (n, 2 * n))
        o_ref[0, 0, t] = pltpu.roll(rows, 0, 1, stride=1, stride_axis=0)[:, n:]


def _bias_tiles(rel_bias_table, S, n):
    HM = rel_bias_table.shape[1]
    assert n >= MAX_DISTANCE
    d = jnp.arange(2 * n, dtype=jnp.int32)
    by_dist = rel_bias_table[_t5_bucket(d)].astype(f32).T
    rel = (by_dist - rel_bias_table[NUM_BUCKETS - 1].astype(f32)[:, None]) * LOG2E
    i = jnp.arange(2 * n)
    w_diag = jnp.where(i[None, :] <= n, rel[:, jnp.clip(n - i, 0, 2 * n - 1)], NEG)
    w_prev = rel[:, jnp.clip(2 * n - i, 0, 2 * n - 1)]
    w = jnp.stack([w_prev, w_diag], axis=1)
    return pl.pallas_call(
        _toeplitz_kernel,
        out_shape=jax.ShapeDtypeStruct((HM // 2, 2, 2, n, n), f32),
        grid=(HM // 2, 2),
        in_specs=[pl.BlockSpec((1, 2, 2 * n), lambda h, m: (h * 2 + m, 0, 0))],
        out_specs=pl.BlockSpec((1, 1, 2, n, n), lambda h, m: (h, m, 0, 0, 0)),
        compiler_params=pltpu.CompilerParams(vmem_limit_bytes=VMEM_LIMIT),
        name="bias_tiles",
    )(w)


def _attn(qd, kd, vd, bias_tiles, lamv, g_subln, lambda_init, bounded):
    B, S, _ = qd.shape
    H = N_DIFF_HEADS
    tq = min(TQ, S)
    body = _attn_bounded_kernel if bounded else _attn_kernel
    scratch = [pltpu.VMEM((S, 2 * DIFF_DV), bf16)] if bounded else []
    return pl.pallas_call(
        functools.partial(body, lambda_init=lambda_init),
        out_shape=jax.ShapeDtypeStruct((B, S, DIFF_V_W), bf16),
        scratch_shapes=scratch,
        grid=(B, H, S // tq),
        in_specs=[pl.BlockSpec((1, tq, 2 * DIFF_DQK), lambda b, h, i: (b, i, h)),
                  pl.BlockSpec((1, S, 2 * DIFF_DQK), lambda b, h, i: (b, 0, h)),
                  pl.BlockSpec((1, S, DIFF_DV), lambda b, h, i: (b, 0, h)),
                  pl.BlockSpec((1, 2, 2, tq, tq), lambda b, h, i: (h, 0, 0, 0, 0)),
                  pl.BlockSpec((4, DIFF_DQK), lambda b, h, i: (0, 0)),
                  pl.BlockSpec((1, DIFF_DV), lambda b, h, i: (0, 0))],
        out_specs=pl.BlockSpec((1, tq, DIFF_DV), lambda b, h, i: (b, i, h)),
        compiler_params=pltpu.CompilerParams(
            dimension_semantics=("arbitrary", "arbitrary", "arbitrary"), vmem_limit_bytes=VMEM_LIMIT),
        name="attn_bounded" if bounded else "attn_online",
    )(qd, kd, vd, bias_tiles, lamv, g_subln.reshape(1, DIFF_DV))


def _scores_bounded(rel_bias_table, g_qnorm, g_knorm):
    qk = DIFF_DQK ** 0.5 * jnp.max(jnp.abs(g_qnorm)) * jnp.max(jnp.abs(g_knorm)) * 1.02
    rel = jnp.max(jnp.abs(rel_bias_table - rel_bias_table[NUM_BUCKETS - 1:]))
    return qk + rel <= SAFE_SCORE


def _rows_to_tiles(x, ref):
    n = x.shape[0]
    for c in range(ROW_TILE):
        ref[pl.ds(c, n, stride=ROW_TILE), :] = x[:, c * LANES:(c + 1) * LANES]


def _tiles_to_rows(ref, n):
    return jnp.concatenate([ref[pl.ds(c, n, stride=ROW_TILE), :] for c in range(ROW_TILE)], axis=1)


def _outproj_kernel(og_ref, od_ref, x_ref, mod_ref, wo_ref, g2_ref, wr_ref, br_ref,
                    x1_ref, hp_ref, lg_ref):
    half = og_ref.shape[2]
    mix = _mm(og_ref[0], wo_ref[:half, :]) + _mm(od_ref[0], wo_ref[half:, :])
    x1 = x_ref[0] + mod_ref[0, 2:3, :] * mix
    x1_ref[0] = x1
    ms = jnp.mean(x1 * x1, axis=-1, keepdims=True)
    y = x1 * lax.rsqrt(ms + EPS) * g2_ref[...]
    h = (y * (1.0 + mod_ref[0, 4:5, :]) + mod_ref[0, 3:4, :]).astype(bf16)
    lg_ref[...] = _mm(h, wr_ref[...]) + br_ref[...]
    _rows_to_tiles(h.astype(f32), hp_ref)


def _outproj(og, od, x, mod, w_out, g_norm2, w_router, b_router):
    B, S, D = x.shape
    E = w_router.shape[1]
    tm = TM_IN
    nj = S // tm
    w_r = jnp.zeros((D, LANES), f32).at[:, :E].set(w_router).astype(bf16)
    b_r = jnp.full((1, LANES), NEG, f32).at[0, :E].set(b_router)
    const = lambda shape: pl.BlockSpec(shape, lambda b, i: (0,) * len(shape))
    return pl.pallas_call(
        _outproj_kernel,
        out_shape=[jax.ShapeDtypeStruct((B, S, D), f32),
                   jax.ShapeDtypeStruct((B * S * ROW_TILE, LANES), f32),
                   jax.ShapeDtypeStruct((B * S, LANES), f32)],
        grid=(B, nj),
        in_specs=[pl.BlockSpec((1, tm, og.shape[2]), lambda b, i: (b, i, 0)),
                  pl.BlockSpec((1, tm, od.shape[2]), lambda b, i: (b, i, 0)),
                  pl.BlockSpec((1, tm, D), lambda b, i: (b, i, 0)),
                  pl.BlockSpec((1, 6, D), lambda b, i: (b, 0, 0)),
                  const((w_out.shape[0], D)), const((1, D)), const((D, LANES)), const((1, LANES))],
        out_specs=[pl.BlockSpec((1, tm, D), lambda b, i: (b, i, 0)),
                   pl.BlockSpec((tm * ROW_TILE, LANES), lambda b, i: (b * nj + i, 0)),
                   pl.BlockSpec((tm, LANES), lambda b, i: (b * nj + i, 0))],
        compiler_params=pltpu.CompilerParams(
            dimension_semantics=("arbitrary", "arbitrary"), vmem_limit_bytes=VMEM_LIMIT),
        name="outproj",
    )(og, od, x, mod, w_out.astype(bf16), g_norm2.reshape(1, D), w_r, b_r)


def _route_kernel(lg_ref, lt_ref, ri_ref, rw_ref, cnt_ref, run_ref):
    @pl.when(pl.program_id(0) == 0)
    def _():
        run_ref[...] = jnp.zeros_like(run_ref)

    x = lg_ref[...]
    tr = x.shape[0]
    lane = lax.broadcasted_iota(jnp.int32, (tr, LANES), 1)
    lane_f = lane.astype(f32)
    vals, hots, idxs = [], [], []
    for _ in range(TOP_K):
        m = jnp.max(x, axis=-1, keepdims=True)
        idx = jnp.min(jnp.where(x == m, lane_f, float(LANES)), axis=-1, keepdims=True)
        hot = lane_f == idx
        x = jnp.where(hot, -jnp.inf, x)
        vals.append(m)
        hots.append(hot)
        idxs.append(idx.astype(jnp.int32))
    ex = [jnp.exp(v - vals[0]) for v in vals]
    den = ex[0] + ex[1] + ex[2] + ex[3]
    sel = (hots[0] | hots[1] | hots[2] | hots[3]).astype(f32)
    rank = _mm(lt_ref[...], sel.astype(bf16)) + run_ref[...]
    run_ref[...] = run_ref[...] + jnp.sum(sel, axis=0, keepdims=True)
    cnt_ref[...] = run_ref[...]
    ri = jnp.zeros((tr, LANES), jnp.int32)
    rw = jnp.zeros((tr, LANES), f32)
    for k in range(TOP_K):
        rk = jnp.sum(jnp.where(hots[k], rank, 0.0), axis=-1, keepdims=True).astype(jnp.int32)
        ri = jnp.where(lane == k, rk, ri)
        ri = jnp.where(lane == TOP_K + k, idxs[k], ri)
        rw = jnp.where(lane == k, ex[k] / den, rw)
    ri_ref[...] = ri
    rw_ref[...] = rw


def _route(logits):
    T = logits.shape[0]
    tr = min(TR, T)
    r = jnp.arange(tr)
    lt = (r[None, :] < r[:, None]).astype(bf16)
    return pl.pallas_call(
        _route_kernel,
        out_shape=[jax.ShapeDtypeStruct((T, LANES), jnp.int32),
                   jax.ShapeDtypeStruct((T, LANES), f32),
                   jax.ShapeDtypeStruct((1, LANES), f32)],
        grid=(T // tr,),
        in_specs=[pl.BlockSpec((tr, LANES), lambda i: (i, 0)),
                  pl.BlockSpec((tr, tr), lambda i: (0, 0))],
        out_specs=[pl.BlockSpec((tr, LANES), lambda i: (i, 0)),
                   pl.BlockSpec((tr, LANES), lambda i: (i, 0)),
                   pl.BlockSpec((1, LANES), lambda i: (0, 0))],
        scratch_shapes=[pltpu.VMEM((1, LANES), f32)],
        compiler_params=pltpu.CompilerParams(dimension_semantics=("arbitrary",)),
        name="route",
    )(logits, lt)


def _tile(ref, t):
    return ref.at[pl.ds(pl.multiple_of(t * ROW_TILE, ROW_TILE), ROW_TILE)]


def _dispatch_kernel(pend_ref, cnt_ref, nu_ref, dest_ref, h_ref, xs_ref, zero_ref, sem, zsem):
    n_tok = h_ref.shape[0] // ROW_TILE
    blk_rows = FFN_BLK * ROW_TILE

    @pl.when(pl.program_id(0) == 0)
    def _():
        zero_ref[...] = jnp.zeros_like(zero_ref)
        n_exp = pend_ref.shape[0]

        def last_block(e):
            return xs_ref.at[pl.ds(pl.multiple_of((pend_ref[e] - FFN_BLK) * ROW_TILE, blk_rows), blk_rows)]

        def zfill(e, c):
            @pl.when(cnt_ref[e] > 0)
            def _():
                pltpu.make_async_copy(zero_ref, last_block(e), zsem).start()
            return c

        def zwait(e, c):
            @pl.when(cnt_ref[e] > 0)
            def _():
                pltpu.make_async_copy(zero_ref, last_block(e), zsem).wait()
            return c

        lax.fori_loop(0, n_exp, zfill, 0)
        lax.fori_loop(0, n_exp, zwait, 0)

        def tail_block(i):
            return xs_ref.at[pl.ds(pl.multiple_of(i * blk_rows, blk_rows), blk_rows)]

        def tfill(i, c):
            pltpu.make_async_copy(zero_ref, tail_block(i), zsem).start()
            return c

        def twait(i, c):
            pltpu.make_async_copy(zero_ref, tail_block(i), zsem).wait()
            return c

        n_blk = xs_ref.shape[0] // blk_rows
        lax.fori_loop(nu_ref[0], n_blk, tfill, 0)
        lax.fori_loop(nu_ref[0], n_blk, twait, 0)

    def issue(g, c):
        for u in range(DMA_UNROLL):
            r = g * DMA_UNROLL + u
            for k in range(TOP_K):
                pltpu.make_async_copy(_tile(h_ref, r), _tile(xs_ref, dest_ref[r * TOP_K + k]),
                                      sem).start(priority=k % 2)
        return c

    lax.fori_loop(0, n_tok // DMA_UNROLL, issue, 0)
    done = xs_ref.at[pl.ds(0, n_tok * TOP_K * ROW_TILE)]
    pltpu.make_async_copy(done, done, sem).wait()


def _dispatch(p_ends, counts, n_used, dest_flat, hp, n_rows):
    T = hp.shape[0] // ROW_TILE
    grid_spec = pltpu.PrefetchScalarGridSpec(
        num_scalar_prefetch=3,
        grid=(T // TD,),
        in_specs=[pl.BlockSpec((TD * TOP_K,), lambda i, pe, cn, nu: (i,), memory_space=pltpu.SMEM),
                  pl.BlockSpec((TD * ROW_TILE, LANES), lambda i, pe, cn, nu: (i, 0))],
        out_specs=pl.BlockSpec(memory_space=pl.ANY),
        scratch_shapes=[pltpu.VMEM((FFN_BLK * ROW_TILE, LANES), f32),
                        pltpu.SemaphoreType.DMA(()), pltpu.SemaphoreType.DMA(())],
    )
    return pl.pallas_call(
        _dispatch_kernel,
        out_shape=jax.ShapeDtypeStruct((n_rows * ROW_TILE, LANES), f32),
        grid_spec=grid_spec,
        compiler_params=pltpu.CompilerParams(dimension_semantics=("arbitrary",)),
        name="dispatch",
    )(p_ends, counts, n_used, dest_flat, hp)


def _ffn_kernel(be_ref, nu_ref, xs_ref, wgu32_ref, bgu_ref, wd32_ref, bd_ref, ys_ref, wgu_ref, wd_ref):
    i = pl.program_id(0)
    used = i < nu_ref[0]
    new_expert = (i == 0) | (be_ref[i] != be_ref[jnp.maximum(i - 1, 0)])

    @pl.when(used & new_expert)
    def _():
        rows = 128

        def cast(src, dst):
            def body(r, c):
                r0 = pl.multiple_of(r * rows, rows)
                dst[pl.ds(r0, rows), :] = src[0, pl.ds(r0, rows), :].astype(bf16)
                return c
            lax.fori_loop(0, src.shape[1] // rows, body, 0)
        cast(wgu32_ref, wgu_ref)
        cast(wd32_ref, wd_ref)

    @pl.when(used)
    def _():
        F = wd_ref.shape[0]
        xrow = _tiles_to_rows(xs_ref, FFN_BLK).astype(bf16)
        acc = None
        fc = F // 2
        for c in range(2):
            def gu(col0):
                return _mm(xrow, wgu_ref[:, col0:col0 + fc]) + bgu_ref[0, :, col0:col0 + fc]
            gate = jnp.minimum(gu(c * fc), SWIGLU_LIMIT)
            up = jnp.clip(gu(F + c * fc), -SWIGLU_LIMIT, SWIGLU_LIMIT)
            y = (up + 1.0) * (gate * jax.nn.sigmoid(SWIGLU_ALPHA * gate))
            part = _mm(y.astype(bf16), wd_ref[c * fc:(c + 1) * fc, :])
            acc = part if acc is None else acc + part
        _rows_to_tiles(acc + bd_ref[0], ys_ref)

    @pl.when(jnp.logical_not(used))
    def _():
        ys_ref[...] = jnp.zeros_like(ys_ref)


def _ffn(block_e, n_used, xs, w_gate_up, b_gate_up, w_down, b_down):
    E, D, F2 = w_gate_up.shape
    F = F2 // 2
    P = xs.shape[0] // ROW_TILE
    nb = P // FFN_BLK
    rows = FFN_BLK * ROW_TILE

    def blk(i, be, nu):
        return jnp.minimum(i, nu[0] - 1)

    grid_spec = pltpu.PrefetchScalarGridSpec(
        num_scalar_prefetch=2,
        grid=(nb,),
        in_specs=[pl.BlockSpec((rows, LANES), lambda i, be, nu: (blk(i, be, nu), 0)),
                  pl.BlockSpec((1, D, F2), lambda i, be, nu: (be[blk(i, be, nu)], 0, 0)),
                  pl.BlockSpec((1, 1, F2), lambda i, be, nu: (be[blk(i, be, nu)], 0, 0)),
                  pl.BlockSpec((1, F, D), lambda i, be, nu: (be[blk(i, be, nu)], 0, 0)),
                  pl.BlockSpec((1, 1, D), lambda i, be, nu: (be[blk(i, be, nu)], 0, 0))],
        out_specs=pl.BlockSpec((rows, LANES), lambda i, be, nu: (i, 0)),
        scratch_shapes=[pltpu.VMEM((D, F2), bf16), pltpu.VMEM((F, D), bf16)],
    )
    return pl.pallas_call(
        _ffn_kernel,
        out_shape=jax.ShapeDtypeStruct((P * ROW_TILE, LANES), f32),
        grid_spec=grid_spec,
        compiler_params=pltpu.CompilerParams(
            dimension_semantics=("arbitrary",), vmem_limit_bytes=VMEM_LIMIT_FFN),
        name="ffn",
    )(block_e, n_used, xs, w_gate_up, b_gate_up.reshape(E, 1, F2), w_down, b_down.reshape(E, 1, D))


def _combine_kernel(dcur_ref, dnext_ref, ys_ref, x1_ref, rw_ref, rep_ref, mod_ref, o_ref, buf, acc_ref, sem):
    step = pl.program_id(0) * pl.num_programs(1) + pl.program_id(1)
    n_steps = pl.num_programs(0) * pl.num_programs(1)
    slot = step % 2

    def issue(dref, s):
        def body(g, c):
            for u in range(DMA_UNROLL):
                r = g * DMA_UNROLL + u
                for k in range(TOP_K):
                    pltpu.make_async_copy(_tile(ys_ref, dref[r * TOP_K + k]),
                                          _tile(buf.at[s, k], r), sem.at[s]).start(priority=k % 2)
            return c
        lax.fori_loop(0, TD // DMA_UNROLL, body, 0)

    @pl.when(step == 0)
    def _():
        issue(dcur_ref, 0)

    @pl.when(step + 1 < n_steps)
    def _():
        issue(dnext_ref, 1 - slot)

    pltpu.make_async_copy(buf.at[slot], buf.at[slot], sem.at[slot]).wait()

    rw_hi, rw_lo = _split(rw_ref[...])
    rw8 = _mm(rep_ref[...], rw_hi) + _mm(rep_ref[...], rw_lo)
    moe = rw8[:, 0:1] * buf[slot, 0]
    for k in range(1, TOP_K):
        moe = moe + rw8[:, k:k + 1] * buf[slot, k]
    acc_ref[...] = moe
    o_ref[0] = x1_ref[0] + mod_ref[0, 5:6, :] * _tiles_to_rows(acc_ref, TD)


def _combine(dest_flat, ys, x1, rw, mod):
    B, S, D = x1.shape
    nj = S // TD
    n_steps = B * nj
    rep = (jnp.arange(TD * ROW_TILE)[:, None] // ROW_TILE == jnp.arange(TD)[None, :]).astype(bf16)
    return pl.pallas_call(
        _combine_kernel,
        out_shape=jax.ShapeDtypeStruct((B, S, D), f32),
        grid=(B, nj),
        in_specs=[pl.BlockSpec((TD * TOP_K,), lambda b, j: (b * nj + j,), memory_space=pltpu.SMEM),
                  pl.BlockSpec((TD * TOP_K,), lambda b, j: (jnp.minimum(b * nj + j + 1, n_steps - 1),),
                               memory_space=pltpu.SMEM),
                  pl.BlockSpec(memory_space=pl.ANY),
                  pl.BlockSpec((1, TD, D), lambda b, j: (b, j, 0)),
                  pl.BlockSpec((TD, LANES), lambda b, j: (b * nj + j, 0)),
                  pl.BlockSpec((TD * ROW_TILE, TD), lambda b, j: (0, 0)),
                  pl.BlockSpec((1, 6, D), lambda b, j: (b, 0, 0))],
        out_specs=pl.BlockSpec((1, TD, D), lambda b, j: (b, j, 0)),
        scratch_shapes=[pltpu.VMEM((2, TOP_K, TD * ROW_TILE, LANES), f32),
                        pltpu.VMEM((TD * ROW_TILE, LANES), f32), pltpu.SemaphoreType.DMA((2,))],
        compiler_params=pltpu.CompilerParams(
            dimension_semantics=("arbitrary", "arbitrary"), vmem_limit_bytes=VMEM_LIMIT),
        name="combine",
    )(dest_flat, dest_flat, ys, x1, rw, rep, mod)


def _moe(hp, logits, x1, mod, w_gate_up, b_gate_up, w_down, b_down):
    T = logits.shape[0]
    E = w_gate_up.shape[0]
    ri, rw, cnt = _route(logits)
    rank = ri[:, :TOP_K]
    e_sel = ri[:, TOP_K:2 * TOP_K]
    counts = cnt[0, :E].astype(jnp.int32)
    padded = ((counts + FFN_BLK - 1) // FFN_BLK) * FFN_BLK
    p_ends = jnp.cumsum(padded)
    p_starts = p_ends - padded
    nb = -(-T * TOP_K // FFN_BLK) + E
    n_used = jnp.maximum(p_ends[-1:] // FFN_BLK, 1).astype(jnp.int32)
    blk_start = jnp.arange(nb, dtype=jnp.int32) * FFN_BLK
    block_e = jnp.minimum(jnp.sum(p_ends[None, :] <= blk_start[:, None], axis=1), E - 1).astype(jnp.int32)
    onehot = e_sel[:, :, None] == jnp.arange(E, dtype=jnp.int32)[None, None, :]
    dest = (jnp.sum(jnp.where(onehot, p_starts[None, None, :], 0), axis=-1) + rank).reshape(-1)
    xs = _dispatch(p_ends.astype(jnp.int32), counts, n_used, dest, hp, nb * FFN_BLK)
    ys = _ffn(block_e, n_used, xs, w_gate_up, b_gate_up, w_down, b_down)
    return _combine(dest, ys, x1, rw, mod)


def kernel(x, c, rel_bias_table, w_ada, b_ada, g_norm1, w_in, w_gk_up, b_gk_up, g_gla_out, g_qnorm, g_knorm, lambda_q1, lambda_k1, lambda_q2, lambda_k2, g_subln, w_out, g_norm2, w_router, b_router, w_gate_up, b_gate_up, w_down, b_down):
    B, S, D = x.shape
    depth = w_ada.shape[0]
    bias_tiles = _bias_tiles(rel_bias_table, S, min(TQ, S))
    for l in range(depth):
        lambda_init = 0.8 - 0.6 * math.exp(-0.3 * l)
        mod = _ada(c, w_ada[l], b_ada[l])
        qg, kg, gk, kgt, gkt, vg, rg, qd, kd, vd = _inproj(
            x, mod, g_norm1[l], w_in[l], w_gk_up[l], b_gk_up[l], g_qnorm[l], g_knorm[l])
        og = _gla(qg, kg, gk, kgt, gkt, vg, rg, g_gla_out[l])
        lamv = jnp.stack([lambda_q1[l], lambda_k1[l], lambda_q2[l], lambda_k2[l]]).astype(f32)
        od = lax.cond(_scores_bounded(rel_bias_table, g_qnorm[l], g_knorm[l]),
                      functools.partial(_attn, lambda_init=lambda_init, bounded=True),
                      functools.partial(_attn, lambda_init=lambda_init, bounded=False),
                      qd, kd, vd, bias_tiles, lamv, g_subln[l])
        x1, hp, logits = _outproj(og, od, x, mod, w_out[l], g_norm2[l], w_router[l], b_router[l])
        x = _moe(hp, logits, x1, mod, w_gate_up[l], b_gate_up[l], w_down[l], b_down[l])
    return x
```

```python
import functools
import math

import jax
import jax.numpy as jnp
from jax import lax
from jax.experimental import pallas as pl
from jax.experimental.pallas import tpu as pltpu

f32 = jnp.float32
bf16 = jnp.bfloat16

N_GLA_HEADS = 4
GLA_DK = 64
GLA_DV = 128
GLA_GATE_RANK = 16
GLA_GATE_NORM = 16.0
GLA_CHUNK = 64
N_DIFF_HEADS = 4
DIFF_DQK = 64
DIFF_DV = 128
NUM_BUCKETS = 32
MAX_DISTANCE = 128
TOP_K = 4
SWIGLU_LIMIT = 7.0
SWIGLU_ALPHA = 1.702
EPS = 1e-6

GLA_QK_W = N_GLA_HEADS * GLA_DK
GLA_V_W = N_GLA_HEADS * GLA_DV
DIFF_QK_W = N_DIFF_HEADS * 2 * DIFF_DQK
DIFF_V_W = N_DIFF_HEADS * DIFF_DV

LANES = 128
NEG = -1e30
LOG2E = math.log2(math.e)
SAFE_SCORE = 40.0
VMEM_LIMIT = 48 * 1024 * 1024
VMEM_LIMIT_FFN = 58 * 1024 * 1024

TM_IN = 512
TG_GLA = 1024
PAIR = 2 * GLA_CHUNK
TQ = 512
ATTN_UNROLL = 4
TR = 512
TD = 256
ROW_TILE = 8
DMA_UNROLL = 8
FFN_BLK = 512


def _nt(a, b):
    return lax.dot_general(a, b, (((1,), (1,)), ((), ())), preferred_element_type=f32)


def _mm(a, b):
    return jnp.dot(a, b, preferred_element_type=f32)


def _split(x):
    hi = x.astype(bf16)
    lo = (x - hi.astype(f32)).astype(bf16)
    return hi, lo


def _silu(x):
    return x * jax.nn.sigmoid(x)


def _ada_kernel(c_ref, w_ref, b_ref, o_ref):
    c = c_ref[...]
    o_ref[...] = _mm(_silu(c).astype(bf16), w_ref[...].astype(bf16)) + b_ref[...]


def _ada(c, w_ada, b_ada):
    B, D = c.shape
    N = w_ada.shape[1]
    bp = 8
    cp = jnp.zeros((bp, D), f32).at[:B].set(c)
    tn = 1536
    out = pl.pallas_call(
        _ada_kernel,
        out_shape=jax.ShapeDtypeStruct((bp, N), f32),
        grid=(N // tn,),
        in_specs=[pl.BlockSpec((bp, D), lambda j: (0, 0)),
                  pl.BlockSpec((D, tn), lambda j: (0, j)),
                  pl.BlockSpec((1, tn), lambda j: (0, j))],
        out_specs=pl.BlockSpec((bp, tn), lambda j: (0, j)),
        compiler_params=pltpu.CompilerParams(vmem_limit_bytes=VMEM_LIMIT),
        name="ada",
    )(cp, w_ada, b_ada.reshape(1, N))
    return out[:B].reshape(B, 6, D)


def _inproj_kernel(x_ref, mod_ref, g1_ref, wm_ref, wkt_ref, wlo_ref, wup_ref, wupt_ref,
                   bup_ref, bupt_ref, gqk_ref, grp_ref, grpt_ref,
                   qg_ref, kg_ref, gk_ref, kgt_ref, gkt_ref, vg_ref, rg_ref,
                   qd_ref, kd_ref, vd_ref):
    x = x_ref[0]
    ms = jnp.mean(x * x, axis=-1, keepdims=True)
    y = x * lax.rsqrt(ms + EPS) * g1_ref[...]
    h = (y * (1.0 + mod_ref[0, 1:2, :]) + mod_ref[0, 0:1, :]).astype(bf16)

    def proj(a, b):
        return _mm(h, wm_ref[:, a:b])

    o = 0
    qg_ref[0] = proj(o, o + GLA_QK_W); o += GLA_QK_W
    kg_ref[0] = proj(o, o + GLA_QK_W); o += GLA_QK_W
    vg_ref[0] = proj(o, o + GLA_V_W).astype(bf16); o += GLA_V_W
    rg_ref[0] = proj(o, o + GLA_V_W); o += GLA_V_W
    qk = proj(o, o + 2 * DIFF_QK_W); o += 2 * DIFF_QK_W
    vd_ref[0] = proj(o, o + DIFF_V_W).astype(bf16)

    kgt = _nt(wkt_ref[...], h)
    for j in range(kgt.shape[1] // PAIR):
        kgt_ref[0, j] = kgt[:, j * PAIR:(j + 1) * PAIR]

    lo = _mm(h, wlo_ref[...]).astype(bf16)
    z = _mm(lo, wup_ref[...]) + bup_ref[...]
    gk_ref[0] = (jnp.minimum(z, 0.0) - jnp.log1p(jnp.exp(-jnp.abs(z)))) * (1.0 / GLA_GATE_NORM)
    zt = _nt(wupt_ref[...], lo) + bupt_ref[...]
    gkt = (jnp.minimum(zt, 0.0) - jnp.log1p(jnp.exp(-jnp.abs(zt)))) * (1.0 / GLA_GATE_NORM)
    for j in range(gkt.shape[1] // PAIR):
        gkt_ref[0, j] = gkt[:, j * PAIR:(j + 1) * PAIR]

    sq_hi, sq_lo = _split(qk * qk)
    gs = _mm(sq_hi, grp_ref[...]) + _mm(sq_lo, grp_ref[...])
    r = lax.rsqrt(gs * (1.0 / DIFF_DQK) + EPS)
    r_hi, r_lo = _split(r)
    rb = _mm(r_hi, grpt_ref[...]) + _mm(r_lo, grpt_ref[...])
    qkn = qk * rb * gqk_ref[...]
    qd_ref[0] = qkn[:, :DIFF_QK_W].astype(bf16)
    kd_ref[0] = qkn[:, DIFF_QK_W:].astype(bf16)


def _inproj(x, mod, g_norm1, w_in, w_gk_up, b_gk_up, g_qnorm, g_knorm):
    B, S, D = x.shape
    offs = [0]
    for w in (GLA_QK_W, GLA_QK_W, GLA_V_W, GLA_V_W, GLA_GATE_RANK, DIFF_QK_W, DIFF_QK_W, DIFF_V_W):
        offs.append(offs[-1] + w)
    w_main = jnp.concatenate([w_in[:, offs[0]:offs[4]], w_in[:, offs[5]:offs[8]]], axis=1).astype(bf16)
    w_kt = w_in[:, offs[1]:offs[2]].T.astype(bf16)
    w_lo = jnp.zeros((D, LANES), f32).at[:, :GLA_GATE_RANK].set(w_in[:, offs[4]:offs[5]]).astype(bf16)
    w_up = jnp.zeros((LANES, GLA_QK_W), f32).at[:GLA_GATE_RANK].set(w_gk_up).astype(bf16)
    w_upt = w_up.T
    b_up = b_gk_up.reshape(1, GLA_QK_W)
    b_upt = b_gk_up.reshape(GLA_QK_W, 1)
    n_grp = 2 * DIFF_QK_W // DIFF_DQK
    gqk = jnp.concatenate([jnp.tile(g_qnorm, n_grp // 2) * (DIFF_DQK ** -0.5 * LOG2E),
                           jnp.tile(g_knorm, n_grp // 2)]).reshape(1, 2 * DIFF_QK_W)
    grp = (jnp.arange(2 * DIFF_QK_W)[:, None] // DIFF_DQK == jnp.arange(LANES)[None, :]).astype(bf16)
    grpt = grp.T
    nw = w_main.shape[1]
    tm = TM_IN
    const = lambda shape: pl.BlockSpec(shape, lambda b, i: (0,) * len(shape))
    row = lambda w: pl.BlockSpec((1, tm, w), lambda b, i: (b, i, 0))
    colT = pl.BlockSpec((1, tm // PAIR, GLA_QK_W, PAIR), lambda b, i: (b, i, 0, 0))
    outs = pl.pallas_call(
        _inproj_kernel,
        out_shape=[jax.ShapeDtypeStruct((B, S, GLA_QK_W), f32),
                   jax.ShapeDtypeStruct((B, S, GLA_QK_W), f32),
                   jax.ShapeDtypeStruct((B, S, GLA_QK_W), f32),
                   jax.ShapeDtypeStruct((B, S // PAIR, GLA_QK_W, PAIR), f32),
                   jax.ShapeDtypeStruct((B, S // PAIR, GLA_QK_W, PAIR), f32),
                   jax.ShapeDtypeStruct((B, S, GLA_V_W), bf16),
                   jax.ShapeDtypeStruct((B, S, GLA_V_W), f32),
                   jax.ShapeDtypeStruct((B, S, DIFF_QK_W), bf16),
                   jax.ShapeDtypeStruct((B, S, DIFF_QK_W), bf16),
                   jax.ShapeDtypeStruct((B, S, DIFF_V_W), bf16)],
        grid=(B, S // tm),
        in_specs=[row(D),
                  pl.BlockSpec((1, 6, D), lambda b, i: (b, 0, 0)),
                  const((1, D)), const((D, nw)), const((GLA_QK_W, D)), const((D, LANES)),
                  const((LANES, GLA_QK_W)), const((GLA_QK_W, LANES)),
                  const((1, GLA_QK_W)), const((GLA_QK_W, 1)),
                  const((1, 2 * DIFF_QK_W)), const((2 * DIFF_QK_W, LANES)),
                  const((LANES, 2 * DIFF_QK_W))],
        out_specs=[row(GLA_QK_W), row(GLA_QK_W), row(GLA_QK_W), colT, colT,
                   row(GLA_V_W), row(GLA_V_W), row(DIFF_QK_W), row(DIFF_QK_W), row(DIFF_V_W)],
        compiler_params=pltpu.CompilerParams(
            dimension_semantics=("arbitrary", "arbitrary"), vmem_limit_bytes=VMEM_LIMIT),
        name="inproj",
    )(x, mod, g_norm1.reshape(1, D), w_main, w_kt, w_lo, w_up, w_upt, b_up, b_upt, gqk, grp, grpt)
    return outs


def _gla_kernel(q_ref, k_ref, g_ref, kt_ref, gt_ref, v_ref, r_ref, gout_ref, tri_ref, trit_ref,
                o_ref, s_ref, *, n_pairs):
    H, DK, DV = N_GLA_HEADS, GLA_DK, GLA_DV

    @pl.when(pl.program_id(1) == 0)
    def _():
        s_ref[...] = jnp.zeros_like(s_ref)

    tri = tri_ref[...]
    trit = trit_ref[...]
    tri_b = tri > 0
    lane_head = lax.broadcasted_iota(jnp.int32, (1, H * DK), 1) // DK
    row_head = lax.broadcasted_iota(jnp.int32, (H * PAIR, 1), 0) // PAIR
    qmask = row_head == lane_head
    row_first = lax.broadcasted_iota(jnp.int32, (PAIR, 1), 0) < GLA_CHUNK
    row_first4 = (lax.broadcasted_iota(jnp.int32, (H * PAIR, 1), 0) % PAIR) < GLA_CHUNK
    lane_first = lax.broadcasted_iota(jnp.int32, (1, PAIR), 1) < GLA_CHUNK
    scale = DK ** -0.5
    gout = gout_ref[...]

    def pair(p, carry):
        r0 = pl.multiple_of(p * PAIR, PAIR)
        q = q_ref[0, pl.ds(r0, PAIR), :]
        k = k_ref[0, pl.ds(r0, PAIR), :]
        g = g_ref[0, pl.ds(r0, PAIR), :]
        kt = kt_ref[0, p]
        gt = gt_ref[0, p]
        v = v_ref[0, pl.ds(r0, PAIR), :]

        g_hi, g_lo = _split(g)
        gc = _mm(tri, g_hi) + _mm(tri, g_lo)
        gt_hi, gt_lo = _split(gt)
        gct = _mm(gt_hi, trit) + _mm(gt_lo, trit)
        g_last = jnp.where(row_first, gc[GLA_CHUNK - 1:GLA_CHUNK, :], gc[PAIR - 1:PAIR, :])
        gl0 = gct[:, GLA_CHUNK - 1:GLA_CHUNK]
        gl1 = gct[:, PAIR - 1:PAIR]
        g_last_t = jnp.where(lane_first, gl0, gl1)

        q_e = (q * (jnp.exp(gc) * scale)).astype(bf16)
        k_e = (k * jnp.exp(-gc)).astype(bf16)
        ks_t = kt * jnp.exp(g_last_t - gct)
        ks_t0 = jnp.where(lane_first, ks_t, 0.0).astype(bf16)
        ks_t1 = jnp.where(lane_first, 0.0, ks_t).astype(bf16)
        del g_last

        qm = jnp.where(qmask, jnp.concatenate([q_e] * H, axis=0), jnp.zeros((), bf16))
        a = _nt(qm, k_e)
        s0 = s_ref[...]

        u0 = []
        u1 = []
        for h in range(H):
            v_h = v[:, h * DV:(h + 1) * DV]
            u0.append(_mm(ks_t0[h * DK:(h + 1) * DK], v_h))
            u1.append(_mm(ks_t1[h * DK:(h + 1) * DK], v_h))
        u0 = jnp.concatenate(u0, axis=0)
        u1 = jnp.concatenate(u1, axis=0)
        s1 = s0 * jnp.exp(gl0) + u0
        s_ref[...] = s1 * jnp.exp(gl1) + u1

        o_inter = jnp.where(row_first4, _mm(qm, s0.astype(bf16)), _mm(qm, s1.astype(bf16)))
        for h in range(H):
            a_h = jnp.where(tri_b, a[h * PAIR:(h + 1) * PAIR], 0.0).astype(bf16)
            o_h = _mm(a_h, v[:, h * DV:(h + 1) * DV]) + o_inter[h * PAIR:(h + 1) * PAIR]
            ms = jnp.mean(o_h * o_h, axis=-1, keepdims=True)
            o_n = o_h * lax.rsqrt(ms + EPS) * gout
            r_h = r_ref[0, pl.ds(r0, PAIR), h * DV:(h + 1) * DV]
            o_ref[0, pl.ds(r0, PAIR), h * DV:(h + 1) * DV] = (o_n * _silu(r_h)).astype(bf16)
        return carry

    lax.fori_loop(0, n_pairs, pair, 0)


def _gla(qg, kg, gk, kgt, gkt, vg, rg, g_gla_out):
    B, S, _ = qg.shape
    tg = min(TG_GLA, S)
    r = jnp.arange(PAIR)
    tri = ((r[:, None] // GLA_CHUNK == r[None, :] // GLA_CHUNK) & (r[None, :] <= r[:, None])).astype(bf16)
    row = lambda w: pl.BlockSpec((1, tg, w), lambda b, i: (b, i, 0))
    colT = pl.BlockSpec((1, tg // PAIR, GLA_QK_W, PAIR), lambda b, i: (b, i, 0, 0))
    const = lambda shape: pl.BlockSpec(shape, lambda b, i: (0,) * len(shape))
    return pl.pallas_call(
        functools.partial(_gla_kernel, n_pairs=tg // PAIR),
        out_shape=jax.ShapeDtypeStruct((B, S, GLA_V_W), bf16),
        grid=(B, S // tg),
        in_specs=[row(GLA_QK_W), row(GLA_QK_W), row(GLA_QK_W), colT, colT,
                  row(GLA_V_W), row(GLA_V_W), const((1, GLA_DV)),
                  const((PAIR, PAIR)), const((PAIR, PAIR))],
        out_specs=row(GLA_V_W),
        scratch_shapes=[pltpu.VMEM((GLA_QK_W, GLA_DV), f32)],
        compiler_params=pltpu.CompilerParams(
            dimension_semantics=("arbitrary", "arbitrary"), vmem_limit_bytes=VMEM_LIMIT),
        name="gla",
    )(qg, kg, gk, kgt, gkt, vg, rg, g_gla_out.reshape(1, GLA_DV), tri, tri.T)


def _attn_finish(o, gsub_ref, o_ref, lambda_init):
    ms = jnp.mean(o * o, axis=-1, keepdims=True)
    o_ref[0] = (o * lax.rsqrt(ms + EPS) * gsub_ref[...] * (1.0 - lambda_init)).astype(bf16)


def _attn_lambda(lamv_ref, lambda_init):
    lv = lamv_ref[...]
    return (jnp.exp(jnp.sum(lv[0:1] * lv[1:2], axis=-1, keepdims=True))
            - jnp.exp(jnp.sum(lv[2:3] * lv[3:4], axis=-1, keepdims=True)) + lambda_init)


def _attn_bounded_kernel(q_ref, k_ref, v_ref, bias_ref, lamv_ref, gsub_ref, o_ref, vaug_ref, *, lambda_init):
    qi = pl.program_id(2)
    tq = q_ref.shape[1]
    S = k_ref.shape[1]

    @pl.when(qi == 0)
    def _():
        lane = lax.broadcasted_iota(jnp.int32, (S, DIFF_DV), 1)
        vaug_ref[:, :DIFF_DV] = v_ref[0]
        vaug_ref[:, DIFF_DV:] = jnp.where(lane == 0, 1.0, 0.0).astype(bf16)

    q = q_ref[0]
    lane = lax.broadcasted_iota(jnp.int32, (1, 2 * DIFF_DQK), 1)
    zero = jnp.zeros((), bf16)
    qs = (jnp.where(lane < DIFF_DQK, q, zero), jnp.where(lane < DIFF_DQK, zero, q))

    def update(accs, k0, bias):
        kb = k_ref[0, pl.ds(k0, tq), :]
        vb = vaug_ref[pl.ds(k0, tq), :]
        out = []
        for c in range(2):
            s = _nt(qs[c], kb)
            if bias is not None:
                s = s + bias[c]
            out.append(accs[c] + _mm(jnp.exp2(s).astype(bf16), vb))
        return tuple(out)

    def far(kj, accs):
        return update(accs, pl.multiple_of(kj * tq, tq), None)

    def far_group(g, accs):
        for u in range(ATTN_UNROLL):
            accs = far(g * ATTN_UNROLL + u, accs)
        return accs

    def block_or_masked(accs, kj, bias):
        exists = kj >= 0
        k0 = pl.multiple_of(jnp.maximum(kj, 0) * tq, tq)
        if bias is None:
            tiles = (jnp.where(exists, 0.0, NEG),) * 2
        else:
            tiles = tuple(jnp.where(exists, b, NEG) for b in bias)
        return update(accs, k0, tiles)

    accs = (jnp.zeros((tq, 2 * DIFF_DV), f32), jnp.zeros((tq, 2 * DIFF_DV), f32))
    accs = update(accs, pl.multiple_of(qi * tq, tq), (bias_ref[0, 0, 1], bias_ref[0, 1, 1]))
    accs = block_or_masked(accs, qi - 1, (bias_ref[0, 0, 0], bias_ref[0, 1, 0]))
    for u in range(2, ATTN_UNROLL):
        accs = block_or_masked(accs, qi - u, None)
    n_far = jnp.maximum(qi + 1 - ATTN_UNROLL, 0)
    n_grp = n_far // ATTN_UNROLL
    accs = lax.fori_loop(0, n_grp, far_group, accs)
    accs = lax.fori_loop(n_grp * ATTN_UNROLL, n_far, far, accs)
    a0, a1 = accs
    o = (a0[:, :DIFF_DV] / a0[:, DIFF_DV:DIFF_DV + 1]
         - _attn_lambda(lamv_ref, lambda_init) * (a1[:, :DIFF_DV] / a1[:, DIFF_DV:DIFF_DV + 1]))
    _attn_finish(o, gsub_ref, o_ref, lambda_init)


def _attn_kernel(q_ref, k_ref, v_ref, bias_ref, lamv_ref, gsub_ref, o_ref, *, lambda_init):
    qi = pl.program_id(2)
    tq = q_ref.shape[1]
    q = q_ref[0]
    lane = lax.broadcasted_iota(jnp.int32, (1, 2 * DIFF_DQK), 1)
    zero = jnp.zeros((), bf16)
    qs = (jnp.where(lane < DIFF_DQK, q, zero), jnp.where(lane < DIFF_DQK, zero, q))

    def update(state, kb, vb, bias):
        new = []
        for c in range(2):
            m, l, acc = state[c]
            s = _nt(qs[c], kb)
            if bias is not None:
                s = s + bias[c]
            m_new = jnp.maximum(m, jnp.max(s, axis=-1, keepdims=True))
            alpha = jnp.exp2(m - m_new)
            p = jnp.exp2(s - m_new)
            l = alpha * l + jnp.sum(p, axis=-1, keepdims=True)
            acc = alpha * acc + _mm(p.astype(bf16), vb)
            new.append((m_new, l, acc))
        return tuple(new)

    init = tuple((jnp.full((tq, 1), NEG, f32), jnp.zeros((tq, 1), f32), jnp.zeros((tq, DIFF_DV), f32))
                 for _ in range(2))

    def far(kj, state):
        k0 = pl.multiple_of(kj * tq, tq)
        return update(state, k_ref[0, pl.ds(k0, tq), :], v_ref[0, pl.ds(k0, tq), :], None)

    state = lax.fori_loop(0, jnp.maximum(qi - 1, 0), far, init)

    kd0 = pl.multiple_of(qi * tq, tq)
    state = update(state, k_ref[0, pl.ds(kd0, tq), :], v_ref[0, pl.ds(kd0, tq), :],
                   (bias_ref[0, 0, 1], bias_ref[0, 1, 1]))
    kp0 = pl.multiple_of(jnp.maximum(qi - 1, 0) * tq, tq)
    has_prev = qi > 0
    state = update(state, k_ref[0, pl.ds(kp0, tq), :], v_ref[0, pl.ds(kp0, tq), :],
                   (jnp.where(has_prev, bias_ref[0, 0, 0], NEG), jnp.where(has_prev, bias_ref[0, 1, 0], NEG)))

    (_, l0, a0), (_, l1, a1) = state
    o = a0 / l0 - _attn_lambda(lamv_ref, lambda_init) * (a1 / l1)
    _attn_finish(o, gsub_ref, o_ref, lambda_init)


def _t5_bucket(n):
    max_exact = NUM_BUCKETS // 2
    nf = jnp.maximum(n, 1).astype(f32)
    large = max_exact + (jnp.log(nf / max_exact) / math.log(MAX_DISTANCE / max_exact)
                         * (NUM_BUCKETS - max_exact)).astype(jnp.int32)
    large = jnp.minimum(large, NUM_BUCKETS - 1)
    return jnp.where(n < max_exact, n, large)


def _toeplitz_kernel(w_ref, o_ref):
    n = o_ref.shape[-1]
    for t in range(2):
        rows = jnp.broadcast_to(w_ref[0, t:t + 1, :], (n, 2 * n))
        o_ref[0, 0, t] = pltpu.roll(rows, 0, 1, stride=1, stride_axis=0)[:, n:]


def _bias_tiles(rel_bias_table, S, n):
    HM = rel_bias_table.shape[1]
    assert n >= MAX_DISTANCE
    d = jnp.arange(2 * n, dtype=jnp.int32)
    by_dist = rel_bias_table[_t5_bucket(d)].astype(f32).T
    rel = (by_dist - rel_bias_table[NUM_BUCKETS - 1].astype(f32)[:, None]) * LOG2E
    i = jnp.arange(2 * n)
    w_diag = jnp.where(i[None, :] <= n, rel[:, jnp.clip(n - i, 0, 2 * n - 1)], NEG)
    w_prev = rel[:, jnp.clip(2 * n - i, 0, 2 * n - 1)]
    w = jnp.stack([w_prev, w_diag], axis=1)
    return pl.pallas_call(
        _toeplitz_kernel,
        out_shape=jax.ShapeDtypeStruct((HM // 2, 2, 2, n, n), f32),
        grid=(HM // 2, 2),
        in_specs=[pl.BlockSpec((1, 2, 2 * n), lambda h, m: (h * 2 + m, 0, 0))],
        out_specs=pl.BlockSpec((1, 1, 2, n, n), lambda h, m: (h, m, 0, 0, 0)),
        compiler_params=pltpu.CompilerParams(vmem_limit_bytes=VMEM_LIMIT),
        name="bias_tiles",
    )(w)


def _attn(qd, kd, vd, bias_tiles, lamv, g_subln, lambda_init, bounded):
    B, S, _ = qd.shape
    H = N_DIFF_HEADS
    tq = min(TQ, S)
    body = _attn_bounded_kernel if bounded else _attn_kernel
    scratch = [pltpu.VMEM((S, 2 * DIFF_DV), bf16)] if bounded else []
    return pl.pallas_call(
        functools.partial(body, lambda_init=lambda_init),
        out_shape=jax.ShapeDtypeStruct((B, S, DIFF_V_W), bf16),
        scratch_shapes=scratch,
        grid=(B, H, S // tq),
        in_specs=[pl.BlockSpec((1, tq, 2 * DIFF_DQK), lambda b, h, i: (b, i, h)),
                  pl.BlockSpec((1, S, 2 * DIFF_DQK), lambda b, h, i: (b, 0, h)),
                  pl.BlockSpec((1, S, DIFF_DV), lambda b, h, i: (b, 0, h)),
                  pl.BlockSpec((1, 2, 2, tq, tq), lambda b, h, i: (h, 0, 0, 0, 0)),
                  pl.BlockSpec((4, DIFF_DQK), lambda b, h, i: (0, 0)),
                  pl.BlockSpec((1, DIFF_DV), lambda b, h, i: (0, 0))],
        out_specs=pl.BlockSpec((1, tq, DIFF_DV), lambda b, h, i: (b, i, h)),
        compiler_params=pltpu.CompilerParams(
            dimension_semantics=("arbitrary", "arbitrary", "arbitrary"), vmem_limit_bytes=VMEM_LIMIT),
        name="attn_bounded" if bounded else "attn_online",
    )(qd, kd, vd, bias_tiles, lamv, g_subln.reshape(1, DIFF_DV))


def _scores_bounded(rel_bias_table, g_qnorm, g_knorm):
    qk = DIFF_DQK ** 0.5 * jnp.max(jnp.abs(g_qnorm)) * jnp.max(jnp.abs(g_knorm)) * 1.02
    rel = jnp.max(jnp.abs(rel_bias_table - rel_bias_table[NUM_BUCKETS - 1:]))
    return qk + rel <= SAFE_SCORE


def _rows_to_tiles(x, ref):
    n = x.shape[0]
    for c in range(ROW_TILE):
        ref[pl.ds(c, n, stride=ROW_TILE), :] = x[:, c * LANES:(c + 1) * LANES]


def _tiles_to_rows(ref, n):
    return jnp.concatenate([ref[pl.ds(c, n, stride=ROW_TILE), :] for c in range(ROW_TILE)], axis=1)


def _outproj_kernel(og_ref, od_ref, x_ref, mod_ref, wo_ref, g2_ref, wr_ref, br_ref,
                    x1_ref, hp_ref, lg_ref):
    half = og_ref.shape[2]
    mix = _mm(og_ref[0], wo_ref[:half, :]) + _mm(od_ref[0], wo_ref[half:, :])
    x1 = x_ref[0] + mod_ref[0, 2:3, :] * mix
    x1_ref[0] = x1
    ms = jnp.mean(x1 * x1, axis=-1, keepdims=True)
    y = x1 * lax.rsqrt(ms + EPS) * g2_ref[...]
    h = (y * (1.0 + mod_ref[0, 4:5, :]) + mod_ref[0, 3:4, :]).astype(bf16)
    lg_ref[...] = _mm(h, wr_ref[...]) + br_ref[...]
    _rows_to_tiles(h.astype(f32), hp_ref)


def _outproj(og, od, x, mod, w_out, g_norm2, w_router, b_router):
    B, S, D = x.shape
    E = w_router.shape[1]
    tm = TM_IN
    nj = S // tm
    w_r = jnp.zeros((D, LANES), f32).at[:, :E].set(w_router).astype(bf16)
    b_r = jnp.full((1, LANES), NEG, f32).at[0, :E].set(b_router)
    const = lambda shape: pl.BlockSpec(shape, lambda b, i: (0,) * len(shape))
    return pl.pallas_call(
        _outproj_kernel,
        out_shape=[jax.ShapeDtypeStruct((B, S, D), f32),
                   jax.ShapeDtypeStruct((B * S * ROW_TILE, LANES), f32),
                   jax.ShapeDtypeStruct((B * S, LANES), f32)],
        grid=(B, nj),
        in_specs=[pl.BlockSpec((1, tm, og.shape[2]), lambda b, i: (b, i, 0)),
                  pl.BlockSpec((1, tm, od.shape[2]), lambda b, i: (b, i, 0)),
                  pl.BlockSpec((1, tm, D), lambda b, i: (b, i, 0)),
                  pl.BlockSpec((1, 6, D), lambda b, i: (b, 0, 0)),
                  const((w_out.shape[0], D)), const((1, D)), const((D, LANES)), const((1, LANES))],
        out_specs=[pl.BlockSpec((1, tm, D), lambda b, i: (b, i, 0)),
                   pl.BlockSpec((tm * ROW_TILE, LANES), lambda b, i: (b * nj + i, 0)),
                   pl.BlockSpec((tm, LANES), lambda b, i: (b * nj + i, 0))],
        compiler_params=pltpu.CompilerParams(
            dimension_semantics=("arbitrary", "arbitrary"), vmem_limit_bytes=VMEM_LIMIT),
        name="outproj",
    )(og, od, x, mod, w_out.astype(bf16), g_norm2.reshape(1, D), w_r, b_r)


def _route_kernel(lg_ref, lt_ref, ri_ref, rw_ref, cnt_ref, run_ref):
    @pl.when(pl.program_id(0) == 0)
    def _():
        run_ref[...] = jnp.zeros_like(run_ref)

    x = lg_ref[...]
    tr = x.shape[0]
    lane = lax.broadcasted_iota(jnp.int32, (tr, LANES), 1)
    lane_f = lane.astype(f32)
    vals, hots, idxs = [], [], []
    for _ in range(TOP_K):
        m = jnp.max(x, axis=-1, keepdims=True)
        idx = jnp.min(jnp.where(x == m, lane_f, float(LANES)), axis=-1, keepdims=True)
        hot = lane_f == idx
        x = jnp.where(hot, -jnp.inf, x)
        vals.append(m)
        hots.append(hot)
        idxs.append(idx.astype(jnp.int32))
    ex = [jnp.exp(v - vals[0]) for v in vals]
    den = ex[0] + ex[1] + ex[2] + ex[3]
    sel = (hots[0] | hots[1] | hots[2] | hots[3]).astype(f32)
    rank = _mm(lt_ref[...], sel.astype(bf16)) + run_ref[...]
    run_ref[...] = run_ref[...] + jnp.sum(sel, axis=0, keepdims=True)
    cnt_ref[...] = run_ref[...]
    ri = jnp.zeros((tr, LANES), jnp.int32)
    rw = jnp.zeros((tr, LANES), f32)
    for k in range(TOP_K):
        rk = jnp.sum(jnp.where(hots[k], rank, 0.0), axis=-1, keepdims=True).astype(jnp.int32)
        ri = jnp.where(lane == k, rk, ri)
        ri = jnp.where(lane == TOP_K + k, idxs[k], ri)
        rw = jnp.where(lane == k, ex[k] / den, rw)
    ri_ref[...] = ri
    rw_ref[...] = rw


def _route(logits):
    T = logits.shape[0]
    tr = min(TR, T)
    r = jnp.arange(tr)
    lt = (r[None, :] < r[:, None]).astype(bf16)
    return pl.pallas_call(
        _route_kernel,
        out_shape=[jax.ShapeDtypeStruct((T, LANES), jnp.int32),
                   jax.ShapeDtypeStruct((T, LANES), f32),
                   jax.ShapeDtypeStruct((1, LANES), f32)],
        grid=(T // tr,),
        in_specs=[pl.BlockSpec((tr, LANES), lambda i: (i, 0)),
                  pl.BlockSpec((tr, tr), lambda i: (0, 0))],
        out_specs=[pl.BlockSpec((tr, LANES), lambda i: (i, 0)),
                   pl.BlockSpec((tr, LANES), lambda i: (i, 0)),
                   pl.BlockSpec((1, LANES), lambda i: (0, 0))],
        scratch_shapes=[pltpu.VMEM((1, LANES), f32)],
        compiler_params=pltpu.CompilerParams(dimension_semantics=("arbitrary",)),
        name="route",
    )(logits, lt)


def _tile(ref, t):
    return ref.at[pl.ds(pl.multiple_of(t * ROW_TILE, ROW_TILE), ROW_TILE)]


def _dispatch_kernel(pend_ref, cnt_ref, nu_ref, dest_ref, h_ref, xs_ref, zero_ref, sem, zsem):
    n_tok = h_ref.shape[0] // ROW_TILE
    blk_rows = FFN_BLK * ROW_TILE

    @pl.when(pl.program_id(0) == 0)
    def _():
        zero_ref[...] = jnp.zeros_like(zero_ref)
        n_exp = pend_ref.shape[0]

        def last_block(e):
            return xs_ref.at[pl.ds(pl.multiple_of((pend_ref[e] - FFN_BLK) * ROW_TILE, blk_rows), blk_rows)]

        def zfill(e, c):
            @pl.when(cnt_ref[e] > 0)
            def _():
                pltpu.make_async_copy(zero_ref, last_block(e), zsem).start()
            return c

        def zwait(e, c):
            @pl.when(cnt_ref[e] > 0)
            def _():
                pltpu.make_async_copy(zero_ref, last_block(e), zsem).wait()
            return c

        lax.fori_loop(0, n_exp, zfill, 0)
        lax.fori_loop(0, n_exp, zwait, 0)

        def tail_block(i):
            return xs_ref.at[pl.ds(pl.multiple_of(i * blk_rows, blk_rows), blk_rows)]

        def tfill(i, c):
            pltpu.make_async_copy(zero_ref, tail_block(i), zsem).start()
            return c

        def twait(i, c):
            pltpu.make_async_copy(zero_ref, tail_block(i), zsem).wait()
            return c

        n_blk = xs_ref.shape[0] // blk_rows
        lax.fori_loop(nu_ref[0], n_blk, tfill, 0)
        lax.fori_loop(nu_ref[0], n_blk, twait, 0)

    def issue(g, c):
        for u in range(DMA_UNROLL):
            r = g * DMA_UNROLL + u
            for k in range(TOP_K):
                pltpu.make_async_copy(_tile(h_ref, r), _tile(xs_ref, dest_ref[r * TOP_K + k]),
                                      sem).start(priority=k % 2)
        return c

    lax.fori_loop(0, n_tok // DMA_UNROLL, issue, 0)
    done = xs_ref.at[pl.ds(0, n_tok * TOP_K * ROW_TILE)]
    pltpu.make_async_copy(done, done, sem).wait()


def _dispatch(p_ends, counts, n_used, dest_flat, hp, n_rows):
    T = hp.shape[0] // ROW_TILE
    grid_spec = pltpu.PrefetchScalarGridSpec(
        num_scalar_prefetch=3,
        grid=(T // TD,),
        in_specs=[pl.BlockSpec((TD * TOP_K,), lambda i, pe, cn, nu: (i,), memory_space=pltpu.SMEM),
                  pl.BlockSpec((TD * ROW_TILE, LANES), lambda i, pe, cn, nu: (i, 0))],
        out_specs=pl.BlockSpec(memory_space=pl.ANY),
        scratch_shapes=[pltpu.VMEM((FFN_BLK * ROW_TILE, LANES), f32),
                        pltpu.SemaphoreType.DMA(()), pltpu.SemaphoreType.DMA(())],
    )
    return pl.pallas_call(
        _dispatch_kernel,
        out_shape=jax.ShapeDtypeStruct((n_rows * ROW_TILE, LANES), f32),
        grid_spec=grid_spec,
        compiler_params=pltpu.CompilerParams(dimension_semantics=("arbitrary",)),
        name="dispatch",
    )(p_ends, counts, n_used, dest_flat, hp)


def _ffn_kernel(be_ref, nu_ref, nx_ref, par_ref, xs_ref, wgu_hbm, bgu_ref, wd_hbm, bd_ref, ys_ref,
                wgu32_ref, wd32_ref, wgu_ref, wd_ref, sem):
    i = pl.program_id(0)
    used = i < nu_ref[0]
    new_expert = (i == 0) | (be_ref[i] != be_ref[jnp.maximum(i - 1, 0)])
    slot = par_ref[i]

    def weight_copies(e, s):
        return (pltpu.make_async_copy(wgu_hbm.at[e], wgu32_ref.at[s], sem.at[0, s]),
                pltpu.make_async_copy(wd_hbm.at[e], wd32_ref.at[s], sem.at[1, s]))

    @pl.when(i == 0)
    def _():
        for cp in weight_copies(be_ref[0], 0):
            cp.start()

    @pl.when(used & new_expert)
    def _():
        for cp in weight_copies(be_ref[i], slot):
            cp.wait()

        @pl.when(nx_ref[i] >= 0)
        def _():
            for cp in weight_copies(nx_ref[i], 1 - slot):
                cp.start()

        rows = 128

        def cast(src, dst):
            def body(r, c):
                r0 = pl.multiple_of(r * rows, rows)
                dst[pl.ds(r0, rows), :] = src[slot, pl.ds(r0, rows), :].astype(bf16)
                return c
            lax.fori_loop(0, src.shape[1] // rows, body, 0)
        cast(wgu32_ref, wgu_ref)
        cast(wd32_ref, wd_ref)

    @pl.when(used)
    def _():
        F = wd_ref.shape[0]
        xrow = _tiles_to_rows(xs_ref, FFN_BLK).astype(bf16)
        acc = None
        fc = F // 2
        for c in range(2):
            def gu(col0):
                return _mm(xrow, wgu_ref[:, col0:col0 + fc]) + bgu_ref[0, :, col0:col0 + fc]
            gate = jnp.minimum(gu(c * fc), SWIGLU_LIMIT)
            up = jnp.clip(gu(F + c * fc), -SWIGLU_LIMIT, SWIGLU_LIMIT)
            y = (up + 1.0) * (gate * jax.nn.sigmoid(SWIGLU_ALPHA * gate))
            part = _mm(y.astype(bf16), wd_ref[c * fc:(c + 1) * fc, :])
            acc = part if acc is None else acc + part
        _rows_to_tiles(acc + bd_ref[0], ys_ref)

    @pl.when(jnp.logical_not(used))
    def _():
        ys_ref[...] = jnp.zeros_like(ys_ref)


def _ffn(block_e, n_used, xs, w_gate_up, b_gate_up, w_down, b_down):
    E, D, F2 = w_gate_up.shape
    F = F2 // 2
    P = xs.shape[0] // ROW_TILE
    nb = P // FFN_BLK
    rows = FFN_BLK * ROW_TILE

    idx = jnp.arange(nb, dtype=jnp.int32)
    live = idx < n_used[0]
    later_other = (block_e[None, :] != block_e[:, None]) & (idx[None, :] > idx[:, None]) & live[None, :]
    nxt = jnp.where(jnp.any(later_other, axis=1), block_e[jnp.argmax(later_other, axis=1)], -1).astype(jnp.int32)
    starts = jnp.concatenate([jnp.ones((1,), jnp.int32), (block_e[1:] != block_e[:-1]).astype(jnp.int32)])
    parity = ((jnp.cumsum(starts) - 1) % 2).astype(jnp.int32)

    def blk(i, nu):
        return jnp.minimum(i, nu[0] - 1)

    grid_spec = pltpu.PrefetchScalarGridSpec(
        num_scalar_prefetch=4,
        grid=(nb,),
        in_specs=[pl.BlockSpec((rows, LANES), lambda i, be, nu, nx, pa: (blk(i, nu), 0)),
                  pl.BlockSpec(memory_space=pl.ANY),
                  pl.BlockSpec((1, 1, F2), lambda i, be, nu, nx, pa: (be[blk(i, nu)], 0, 0)),
                  pl.BlockSpec(memory_space=pl.ANY),
                  pl.BlockSpec((1, 1, D), lambda i, be, nu, nx, pa: (be[blk(i, nu)], 0, 0))],
        out_specs=pl.BlockSpec((rows, LANES), lambda i, be, nu, nx, pa: (i, 0)),
        scratch_shapes=[pltpu.VMEM((2, D, F2), f32), pltpu.VMEM((2, F, D), f32),
                        pltpu.VMEM((D, F2), bf16), pltpu.VMEM((F, D), bf16),
                        pltpu.SemaphoreType.DMA((2, 2))],
    )
    return pl.pallas_call(
        _ffn_kernel,
        out_shape=jax.ShapeDtypeStruct((P * ROW_TILE, LANES), f32),
        grid_spec=grid_spec,
        compiler_params=pltpu.CompilerParams(
            dimension_semantics=("arbitrary",), vmem_limit_bytes=VMEM_LIMIT_FFN),
        name="ffn",
    )(block_e, n_used, nxt, parity, xs, w_gate_up, b_gate_up.reshape(E, 1, F2), w_down, b_down.reshape(E, 1, D))


def _combine_kernel(dcur_ref, dnext_ref, ys_ref, x1_ref, rw_ref, rep_ref, mod_ref, o_ref, buf, acc_ref, sem):
    step = pl.program_id(0) * pl.num_programs(1) + pl.program_id(1)
    n_steps = pl.num_programs(0) * pl.num_programs(1)
    slot = step % 2

    def issue(dref, s):
        def body(g, c):
            for u in range(DMA_UNROLL):
                r = g * DMA_UNROLL + u
                for k in range(TOP_K):
                    pltpu.make_async_copy(_tile(ys_ref, dref[r * TOP_K + k]),
                                          _tile(buf.at[s, k], r), sem.at[s]).start(priority=k % 2)
            return c
        lax.fori_loop(0, TD // DMA_UNROLL, body, 0)

    @pl.when(step == 0)
    def _():
        issue(dcur_ref, 0)

    @pl.when(step + 1 < n_steps)
    def _():
        issue(dnext_ref, 1 - slot)

    pltpu.make_async_copy(buf.at[slot], buf.at[slot], sem.at[slot]).wait()

    rw_hi, rw_lo = _split(rw_ref[...])
    rw8 = _mm(rep_ref[...], rw_hi) + _mm(rep_ref[...], rw_lo)
    moe = rw8[:, 0:1] * buf[slot, 0]
    for k in range(1, TOP_K):
        moe = moe + rw8[:, k:k + 1] * buf[slot, k]
    acc_ref[...] = moe
    o_ref[0] = x1_ref[0] + mod_ref[0, 5:6, :] * _tiles_to_rows(acc_ref, TD)


def _combine(dest_flat, ys, x1, rw, mod):
    B, S, D = x1.shape
    nj = S // TD
    n_steps = B * nj
    rep = (jnp.arange(TD * ROW_TILE)[:, None] // ROW_TILE == jnp.arange(TD)[None, :]).astype(bf16)
    return pl.pallas_call(
        _combine_kernel,
        out_shape=jax.ShapeDtypeStruct((B, S, D), f32),
        grid=(B, nj),
        in_specs=[pl.BlockSpec((TD * TOP_K,), lambda b, j: (b * nj + j,), memory_space=pltpu.SMEM),
                  pl.BlockSpec((TD * TOP_K,), lambda b, j: (jnp.minimum(b * nj + j + 1, n_steps - 1),),
                               memory_space=pltpu.SMEM),
                  pl.BlockSpec(memory_space=pl.ANY),
                  pl.BlockSpec((1, TD, D), lambda b, j: (b, j, 0)),
                  pl.BlockSpec((TD, LANES), lambda b, j: (b * nj + j, 0)),
                  pl.BlockSpec((TD * ROW_TILE, TD), lambda b, j: (0, 0)),
                  pl.BlockSpec((1, 6, D), lambda b, j: (b, 0, 0))],
        out_specs=pl.BlockSpec((1, TD, D), lambda b, j: (b, j, 0)),
        scratch_shapes=[pltpu.VMEM((2, TOP_K, TD * ROW_TILE, LANES), f32),
                        pltpu.VMEM((TD * ROW_TILE, LANES), f32), pltpu.SemaphoreType.DMA((2,))],
        compiler_params=pltpu.CompilerParams(
            dimension_semantics=("arbitrary", "arbitrary"), vmem_limit_bytes=VMEM_LIMIT),
        name="combine",
    )(dest_flat, dest_flat, ys, x1, rw, rep, mod)


def _moe(hp, logits, x1, mod, w_gate_up, b_gate_up, w_down, b_down):
    T = logits.shape[0]
    E = w_gate_up.shape[0]
    ri, rw, cnt = _route(logits)
    rank = ri[:, :TOP_K]
    e_sel = ri[:, TOP_K:2 * TOP_K]
    counts = cnt[0, :E].astype(jnp.int32)
    padded = ((counts + FFN_BLK - 1) // FFN_BLK) * FFN_BLK
    p_ends = jnp.cumsum(padded)
    p_starts = p_ends - padded
    nb = -(-T * TOP_K // FFN_BLK) + E
    n_used = jnp.maximum(p_ends[-1:] // FFN_BLK, 1).astype(jnp.int32)
    blk_start = jnp.arange(nb, dtype=jnp.int32) * FFN_BLK
    block_e = jnp.minimum(jnp.sum(p_ends[None, :] <= blk_start[:, None], axis=1), E - 1).astype(jnp.int32)
    onehot = e_sel[:, :, None] == jnp.arange(E, dtype=jnp.int32)[None, None, :]
    dest = (jnp.sum(jnp.where(onehot, p_starts[None, None, :], 0), axis=-1) + rank).reshape(-1)
    xs = _dispatch(p_ends.astype(jnp.int32), counts, n_used, dest, hp, nb * FFN_BLK)
    ys = _ffn(block_e, n_used, xs, w_gate_up, b_gate_up, w_down, b_down)
    return _combine(dest, ys, x1, rw, mod)


def kernel(x, c, rel_bias_table, w_ada, b_ada, g_norm1, w_in, w_gk_up, b_gk_up, g_gla_out, g_qnorm, g_knorm, lambda_q1, lambda_k1, lambda_q2, lambda_k2, g_subln, w_out, g_norm2, w_router, b_router, w_gate_up, b_gate_up, w_down, b_down):
    B, S, D = x.shape
    depth = w_ada.shape[0]
    bias_tiles = _bias_tiles(rel_bias_table, S, min(TQ, S))
    for l in range(depth):
        lambda_init = 0.8 - 0.6 * math.exp(-0.3 * l)
        mod = _ada(c, w_ada[l], b_ada[l])
        qg, kg, gk, kgt, gkt, vg, rg, qd, kd, vd = _inproj(
            x, mod, g_norm1[l], w_in[l], w_gk_up[l], b_gk_up[l], g_qnorm[l], g_knorm[l])
        og = _gla(qg, kg, gk, kgt, gkt, vg, rg, g_gla_out[l])
        lamv = jnp.stack([lambda_q1[l], lambda_k1[l], lambda_q2[l], lambda_k2[l]]).astype(f32)
        od = lax.cond(_scores_bounded(rel_bias_table, g_qnorm[l], g_knorm[l]),
                      functools.partial(_attn, lambda_init=lambda_init, bounded=True),
                      functools.partial(_attn, lambda_init=lambda_init, bounded=False),
                      qd, kd, vd, bias_tiles, lamv, g_subln[l])
        x1, hp, logits = _outproj(og, od, x, mod, w_out[l], g_norm2[l], w_router[l], b_router[l])
        x = _moe(hp, logits, x1, mod, w_gate_up[l], b_gate_up[l], w_down[l], b_down[l])
    return x
```

```python
import functools
import math

import jax
import jax.numpy as jnp
from jax import lax
from jax.experimental import pallas as pl
from jax.experimental.pallas import tpu as pltpu

f32 = jnp.float32
bf16 = jnp.bfloat16

N_GLA_HEADS = 4
GLA_DK = 64
GLA_DV = 128
GLA_GATE_RANK = 16
GLA_GATE_NORM = 16.0
GLA_CHUNK = 64
N_DIFF_HEADS = 4
DIFF_DQK = 64
DIFF_DV = 128
NUM_BUCKETS = 32
MAX_DISTANCE = 128
TOP_K = 4
SWIGLU_LIMIT = 7.0
SWIGLU_ALPHA = 1.702
EPS = 1e-6

GLA_QK_W = N_GLA_HEADS * GLA_DK
GLA_V_W = N_GLA_HEADS * GLA_DV
DIFF_QK_W = N_DIFF_HEADS * 2 * DIFF_DQK
DIFF_V_W = N_DIFF_HEADS * DIFF_DV

LANES = 128
NEG = -1e30
LOG2E = math.log2(math.e)
SAFE_SCORE = 40.0
VMEM_LIMIT = 48 * 1024 * 1024
VMEM_LIMIT_FFN = 58 * 1024 * 1024

TM_IN = 512
INPROJ_SUB = 2
TG_GLA = 1024
PAIR = 2 * GLA_CHUNK
GLA_UNROLL = 4
TQ = 512
ATTN_UNROLL = 4
TR = 512
TD = 256
ROW_TILE = 8
DMA_UNROLL = 8
FFN_BLK = 512


def _nt(a, b):
    return lax.dot_general(a, b, (((1,), (1,)), ((), ())), preferred_element_type=f32)


def _mm(a, b):
    return jnp.dot(a, b, preferred_element_type=f32)


def _split(x):
    hi = x.astype(bf16)
    lo = (x - hi.astype(f32)).astype(bf16)
    return hi, lo


def _silu(x):
    return x * jax.nn.sigmoid(x)


def _ada_kernel(c_ref, w_ref, b_ref, o_ref):
    c = c_ref[...]
    o_ref[...] = _mm(_silu(c).astype(bf16), w_ref[...].astype(bf16)) + b_ref[...]


def _ada(c, w_ada, b_ada):
    B, D = c.shape
    N = w_ada.shape[1]
    bp = 8
    cp = jnp.zeros((bp, D), f32).at[:B].set(c)
    tn = 1536
    out = pl.pallas_call(
        _ada_kernel,
        out_shape=jax.ShapeDtypeStruct((bp, N), f32),
        grid=(N // tn,),
        in_specs=[pl.BlockSpec((bp, D), lambda j: (0, 0)),
                  pl.BlockSpec((D, tn), lambda j: (0, j)),
                  pl.BlockSpec((1, tn), lambda j: (0, j))],
        out_specs=pl.BlockSpec((bp, tn), lambda j: (0, j)),
        compiler_params=pltpu.CompilerParams(vmem_limit_bytes=VMEM_LIMIT),
        name="ada",
    )(cp, w_ada, b_ada.reshape(1, N))
    return out[:B].reshape(B, 6, D)


def _inproj_kernel(x_ref, mod_ref, g1_ref, wm_ref, wkt_ref, wlo_ref, wup_ref, wupt_ref,
                   bup_ref, bupt_ref, gqk_ref, grp_ref, grpt_ref,
                   qg_ref, kg_ref, gk_ref, kgt_ref, gkt_ref, vg_ref, rg_ref,
                   qd_ref, kd_ref, vd_ref):
    tm = x_ref.shape[1]
    sub = tm // INPROJ_SUB
    for t in range(INPROJ_SUB):
        _inproj_rows(slice(t * sub, (t + 1) * sub), x_ref, mod_ref, g1_ref, wm_ref, wkt_ref, wlo_ref, wup_ref,
                     wupt_ref, bup_ref, bupt_ref, gqk_ref, grp_ref, grpt_ref, qg_ref, kg_ref, gk_ref, kgt_ref,
                     gkt_ref, vg_ref, rg_ref, qd_ref, kd_ref, vd_ref)


def _inproj_rows(rows, x_ref, mod_ref, g1_ref, wm_ref, wkt_ref, wlo_ref, wup_ref, wupt_ref,
                 bup_ref, bupt_ref, gqk_ref, grp_ref, grpt_ref,
                 qg_ref, kg_ref, gk_ref, kgt_ref, gkt_ref, vg_ref, rg_ref, qd_ref, kd_ref, vd_ref):
    x = x_ref[0, rows, :]
    ms = jnp.mean(x * x, axis=-1, keepdims=True)
    y = x * lax.rsqrt(ms + EPS) * g1_ref[...]
    h = (y * (1.0 + mod_ref[0, 1:2, :]) + mod_ref[0, 0:1, :]).astype(bf16)

    def proj(a, b):
        return _mm(h, wm_ref[:, a:b])

    o = 0
    qg_ref[0, rows, :] = proj(o, o + GLA_QK_W); o += GLA_QK_W
    kg_ref[0, rows, :] = proj(o, o + GLA_QK_W); o += GLA_QK_W
    vg_ref[0, rows, :] = proj(o, o + GLA_V_W).astype(bf16); o += GLA_V_W
    rg_ref[0, rows, :] = proj(o, o + GLA_V_W); o += GLA_V_W
    qk = proj(o, o + 2 * DIFF_QK_W); o += 2 * DIFF_QK_W
    vd_ref[0, rows, :] = proj(o, o + DIFF_V_W).astype(bf16)

    slab0 = rows.start // PAIR
    kgt = _nt(wkt_ref[...], h)
    for j in range(kgt.shape[1] // PAIR):
        kgt_ref[0, slab0 + j] = kgt[:, j * PAIR:(j + 1) * PAIR]

    lo = _mm(h, wlo_ref[...]).astype(bf16)
    z = _mm(lo, wup_ref[...]) + bup_ref[...]
    gk_ref[0, rows, :] = (jnp.minimum(z, 0.0) - jnp.log1p(jnp.exp(-jnp.abs(z)))) * (1.0 / GLA_GATE_NORM)
    zt = _nt(wupt_ref[...], lo) + bupt_ref[...]
    gkt = (jnp.minimum(zt, 0.0) - jnp.log1p(jnp.exp(-jnp.abs(zt)))) * (1.0 / GLA_GATE_NORM)
    for j in range(gkt.shape[1] // PAIR):
        gkt_ref[0, slab0 + j] = gkt[:, j * PAIR:(j + 1) * PAIR]

    sq_hi, sq_lo = _split(qk * qk)
    gs = _mm(sq_hi, grp_ref[...]) + _mm(sq_lo, grp_ref[...])
    r = lax.rsqrt(gs * (1.0 / DIFF_DQK) + EPS)
    r_hi, r_lo = _split(r)
    rb = _mm(r_hi, grpt_ref[...]) + _mm(r_lo, grpt_ref[...])
    qkn = qk * rb * gqk_ref[...]
    qd_ref[0, rows, :] = qkn[:, :DIFF_QK_W].astype(bf16)
    kd_ref[0, rows, :] = qkn[:, DIFF_QK_W:].astype(bf16)


def _inproj(x, mod, g_norm1, w_in, w_gk_up, b_gk_up, g_qnorm, g_knorm):
    B, S, D = x.shape
    offs = [0]
    for w in (GLA_QK_W, GLA_QK_W, GLA_V_W, GLA_V_W, GLA_GATE_RANK, DIFF_QK_W, DIFF_QK_W, DIFF_V_W):
        offs.append(offs[-1] + w)
    w_main = jnp.concatenate([w_in[:, offs[0]:offs[4]], w_in[:, offs[5]:offs[8]]], axis=1).astype(bf16)
    w_kt = w_in[:, offs[1]:offs[2]].T.astype(bf16)
    w_lo = jnp.zeros((D, LANES), f32).at[:, :GLA_GATE_RANK].set(w_in[:, offs[4]:offs[5]]).astype(bf16)
    w_up = jnp.zeros((LANES, GLA_QK_W), f32).at[:GLA_GATE_RANK].set(w_gk_up).astype(bf16)
    w_upt = w_up.T
    b_up = b_gk_up.reshape(1, GLA_QK_W)
    b_upt = b_gk_up.reshape(GLA_QK_W, 1)
    n_grp = 2 * DIFF_QK_W // DIFF_DQK
    gqk = jnp.concatenate([jnp.tile(g_qnorm, n_grp // 2) * (DIFF_DQK ** -0.5 * LOG2E),
                           jnp.tile(g_knorm, n_grp // 2)]).reshape(1, 2 * DIFF_QK_W)
    grp = (jnp.arange(2 * DIFF_QK_W)[:, None] // DIFF_DQK == jnp.arange(LANES)[None, :]).astype(bf16)
    grpt = grp.T
    nw = w_main.shape[1]
    tm = TM_IN
    const = lambda shape: pl.BlockSpec(shape, lambda b, i: (0,) * len(shape))
    row = lambda w: pl.BlockSpec((1, tm, w), lambda b, i: (b, i, 0))
    colT = pl.BlockSpec((1, tm // PAIR, GLA_QK_W, PAIR), lambda b, i: (b, i, 0, 0))
    outs = pl.pallas_call(
        _inproj_kernel,
        out_shape=[jax.ShapeDtypeStruct((B, S, GLA_QK_W), f32),
                   jax.ShapeDtypeStruct((B, S, GLA_QK_W), f32),
                   jax.ShapeDtypeStruct((B, S, GLA_QK_W), f32),
                   jax.ShapeDtypeStruct((B, S // PAIR, GLA_QK_W, PAIR), f32),
                   jax.ShapeDtypeStruct((B, S // PAIR, GLA_QK_W, PAIR), f32),
                   jax.ShapeDtypeStruct((B, S, GLA_V_W), bf16),
                   jax.ShapeDtypeStruct((B, S, GLA_V_W), f32),
                   jax.ShapeDtypeStruct((B, S, DIFF_QK_W), bf16),
                   jax.ShapeDtypeStruct((B, S, DIFF_QK_W), bf16),
                   jax.ShapeDtypeStruct((B, S, DIFF_V_W), bf16)],
        grid=(B, S // tm),
        in_specs=[row(D),
                  pl.BlockSpec((1, 6, D), lambda b, i: (b, 0, 0)),
                  const((1, D)), const((D, nw)), const((GLA_QK_W, D)), const((D, LANES)),
                  const((LANES, GLA_QK_W)), const((GLA_QK_W, LANES)),
                  const((1, GLA_QK_W)), const((GLA_QK_W, 1)),
                  const((1, 2 * DIFF_QK_W)), const((2 * DIFF_QK_W, LANES)),
                  const((LANES, 2 * DIFF_QK_W))],
        out_specs=[row(GLA_QK_W), row(GLA_QK_W), row(GLA_QK_W), colT, colT,
                   row(GLA_V_W), row(GLA_V_W), row(DIFF_QK_W), row(DIFF_QK_W), row(DIFF_V_W)],
        compiler_params=pltpu.CompilerParams(
            dimension_semantics=("arbitrary", "arbitrary"), vmem_limit_bytes=VMEM_LIMIT),
        name="inproj",
    )(x, mod, g_norm1.reshape(1, D), w_main, w_kt, w_lo, w_up, w_upt, b_up, b_upt, gqk, grp, grpt)
    return outs


def _gla_kernel(q_ref, k_ref, g_ref, kt_ref, gt_ref, v_ref, r_ref, gout_ref, tri_ref, trit_ref,
                o_ref, s_ref, *, n_pairs):
    H, DK, DV = N_GLA_HEADS, GLA_DK, GLA_DV

    @pl.when(pl.program_id(1) == 0)
    def _():
        s_ref[...] = jnp.zeros_like(s_ref)

    tri = tri_ref[...]
    trit = trit_ref[...]
    tri_b = tri > 0
    lane_head = lax.broadcasted_iota(jnp.int32, (1, H * DK), 1) // DK
    row_head = lax.broadcasted_iota(jnp.int32, (H * PAIR, 1), 0) // PAIR
    qmask = row_head == lane_head
    row_first = lax.broadcasted_iota(jnp.int32, (PAIR, 1), 0) < GLA_CHUNK
    row_first4 = (lax.broadcasted_iota(jnp.int32, (H * PAIR, 1), 0) % PAIR) < GLA_CHUNK
    lane_first = lax.broadcasted_iota(jnp.int32, (1, PAIR), 1) < GLA_CHUNK
    scale = DK ** -0.5
    gout = gout_ref[...]

    def pair(p, carry):
        r0 = pl.multiple_of(p * PAIR, PAIR)
        q = q_ref[0, pl.ds(r0, PAIR), :]
        k = k_ref[0, pl.ds(r0, PAIR), :]
        g = g_ref[0, pl.ds(r0, PAIR), :]
        kt = kt_ref[0, p]
        gt = gt_ref[0, p]
        v = v_ref[0, pl.ds(r0, PAIR), :]

        g_hi, g_lo = _split(g)
        gc = _mm(tri, g_hi) + _mm(tri, g_lo)
        gt_hi, gt_lo = _split(gt)
        gct = _mm(gt_hi, trit) + _mm(gt_lo, trit)
        g_last = jnp.where(row_first, gc[GLA_CHUNK - 1:GLA_CHUNK, :], gc[PAIR - 1:PAIR, :])
        gl0 = gct[:, GLA_CHUNK - 1:GLA_CHUNK]
        gl1 = gct[:, PAIR - 1:PAIR]
        g_last_t = jnp.where(lane_first, gl0, gl1)

        q_e = (q * (jnp.exp(gc) * scale)).astype(bf16)
        k_e = (k * jnp.exp(-gc)).astype(bf16)
        ks_t = kt * jnp.exp(g_last_t - gct)
        ks_t0 = jnp.where(lane_first, ks_t, 0.0).astype(bf16)
        ks_t1 = jnp.where(lane_first, 0.0, ks_t).astype(bf16)
        del g_last

        qm = jnp.where(qmask, jnp.concatenate([q_e] * H, axis=0), jnp.zeros((), bf16))
        a = _nt(qm, k_e)
        s0 = s_ref[...]

        u0 = []
        u1 = []
        for h in range(H):
            v_h = v[:, h * DV:(h + 1) * DV]
            u0.append(_mm(ks_t0[h * DK:(h + 1) * DK], v_h))
            u1.append(_mm(ks_t1[h * DK:(h + 1) * DK], v_h))
        u0 = jnp.concatenate(u0, axis=0)
        u1 = jnp.concatenate(u1, axis=0)
        s1 = s0 * jnp.exp(gl0) + u0
        s_ref[...] = s1 * jnp.exp(gl1) + u1

        o_inter = jnp.where(row_first4, _mm(qm, s0.astype(bf16)), _mm(qm, s1.astype(bf16)))
        for h in range(H):
            a_h = jnp.where(tri_b, a[h * PAIR:(h + 1) * PAIR], 0.0).astype(bf16)
            o_h = _mm(a_h, v[:, h * DV:(h + 1) * DV]) + o_inter[h * PAIR:(h + 1) * PAIR]
            ms = jnp.mean(o_h * o_h, axis=-1, keepdims=True)
            o_n = o_h * lax.rsqrt(ms + EPS) * gout
            r_h = r_ref[0, pl.ds(r0, PAIR), h * DV:(h + 1) * DV]
            o_ref[0, pl.ds(r0, PAIR), h * DV:(h + 1) * DV] = (o_n * _silu(r_h)).astype(bf16)
        return carry

    lax.fori_loop(0, n_pairs, pair, 0, unroll=GLA_UNROLL)


def _gla(qg, kg, gk, kgt, gkt, vg, rg, g_gla_out):
    B, S, _ = qg.shape
    tg = min(TG_GLA, S)
    r = jnp.arange(PAIR)
    tri = ((r[:, None] // GLA_CHUNK == r[None, :] // GLA_CHUNK) & (r[None, :] <= r[:, None])).astype(bf16)
    row = lambda w: pl.BlockSpec((1, tg, w), lambda b, i: (b, i, 0))
    colT = pl.BlockSpec((1, tg // PAIR, GLA_QK_W, PAIR), lambda b, i: (b, i, 0, 0))
    const = lambda shape: pl.BlockSpec(shape, lambda b, i: (0,) * len(shape))
    return pl.pallas_call(
        functools.partial(_gla_kernel, n_pairs=tg // PAIR),
        out_shape=jax.ShapeDtypeStruct((B, S, GLA_V_W), bf16),
        grid=(B, S // tg),
        in_specs=[row(GLA_QK_W), row(GLA_QK_W), row(GLA_QK_W), colT, colT,
                  row(GLA_V_W), row(GLA_V_W), const((1, GLA_DV)),
                  const((PAIR, PAIR)), const((PAIR, PAIR))],
        out_specs=row(GLA_V_W),
        scratch_shapes=[pltpu.VMEM((GLA_QK_W, GLA_DV), f32)],
        compiler_params=pltpu.CompilerParams(
            dimension_semantics=("arbitrary", "arbitrary"), vmem_limit_bytes=VMEM_LIMIT),
        name="gla",
    )(qg, kg, gk, kgt, gkt, vg, rg, g_gla_out.reshape(1, GLA_DV), tri, tri.T)


def _attn_finish(o, gsub_ref, o_ref, lambda_init):
    ms = jnp.mean(o * o, axis=-1, keepdims=True)
    o_ref[0] = (o * lax.rsqrt(ms + EPS) * gsub_ref[...] * (1.0 - lambda_init)).astype(bf16)


def _attn_lambda(lamv_ref, lambda_init):
    lv = lamv_ref[...]
    return (jnp.exp(jnp.sum(lv[0:1] * lv[1:2], axis=-1, keepdims=True))
            - jnp.exp(jnp.sum(lv[2:3] * lv[3:4], axis=-1, keepdims=True)) + lambda_init)


def _attn_bounded_kernel(q_ref, k_ref, v_ref, bias_ref, lamv_ref, gsub_ref, o_ref, vaug_ref, *, lambda_init):
    qi = pl.program_id(2)
    tq = q_ref.shape[1]
    S = k_ref.shape[1]

    @pl.when(qi == 0)
    def _():
        lane = lax.broadcasted_iota(jnp.int32, (S, DIFF_DV), 1)
        vaug_ref[:, :DIFF_DV] = v_ref[0]
        vaug_ref[:, DIFF_DV:] = jnp.where(lane == 0, 1.0, 0.0).astype(bf16)

    q = q_ref[0]
    lane = lax.broadcasted_iota(jnp.int32, (1, 2 * DIFF_DQK), 1)
    zero = jnp.zeros((), bf16)
    qs = (jnp.where(lane < DIFF_DQK, q, zero), jnp.where(lane < DIFF_DQK, zero, q))

    def update(accs, k0, bias):
        kb = k_ref[0, pl.ds(k0, tq), :]
        vb = vaug_ref[pl.ds(k0, tq), :]
        out = []
        for c in range(2):
            s = _nt(qs[c], kb)
            if bias is not None:
                s = s + bias[c]
            out.append(accs[c] + _mm(jnp.exp2(s).astype(bf16), vb))
        return tuple(out)

    def far(kj, accs):
        return update(accs, pl.multiple_of(kj * tq, tq), None)

    def far_group(g, accs):
        for u in range(ATTN_UNROLL):
            accs = far(g * ATTN_UNROLL + u, accs)
        return accs

    def block_or_masked(accs, kj, bias):
        exists = kj >= 0
        k0 = pl.multiple_of(jnp.maximum(kj, 0) * tq, tq)
        if bias is None:
            tiles = (jnp.where(exists, 0.0, NEG),) * 2
        else:
            tiles = tuple(jnp.where(exists, b, NEG) for b in bias)
        return update(accs, k0, tiles)

    accs = (jnp.zeros((tq, 2 * DIFF_DV), f32), jnp.zeros((tq, 2 * DIFF_DV), f32))
    accs = update(accs, pl.multiple_of(qi * tq, tq), (bias_ref[0, 0, 1], bias_ref[0, 1, 1]))
    accs = block_or_masked(accs, qi - 1, (bias_ref[0, 0, 0], bias_ref[0, 1, 0]))
    for u in range(2, ATTN_UNROLL):
        accs = block_or_masked(accs, qi - u, None)
    n_far = jnp.maximum(qi + 1 - ATTN_UNROLL, 0)
    n_grp = n_far // ATTN_UNROLL
    accs = lax.fori_loop(0, n_grp, far_group, accs)
    done = n_grp * ATTN_UNROLL
    n_pair = (n_far - done) // 2
    accs = lax.fori_loop(0, n_pair, lambda g, a: far(done + 2 * g + 1, far(done + 2 * g, a)), accs)
    accs = lax.fori_loop(done + 2 * n_pair, n_far, far, accs)
    a0, a1 = accs
    o = (a0[:, :DIFF_DV] / a0[:, DIFF_DV:DIFF_DV + 1]
         - _attn_lambda(lamv_ref, lambda_init) * (a1[:, :DIFF_DV] / a1[:, DIFF_DV:DIFF_DV + 1]))
    _attn_finish(o, gsub_ref, o_ref, lambda_init)


def _attn_kernel(q_ref, k_ref, v_ref, bias_ref, lamv_ref, gsub_ref, o_ref, *, lambda_init):
    qi = pl.program_id(2)
    tq = q_ref.shape[1]
    q = q_ref[0]
    lane = lax.broadcasted_iota(jnp.int32, (1, 2 * DIFF_DQK), 1)
    zero = jnp.zeros((), bf16)
    qs = (jnp.where(lane < DIFF_DQK, q, zero), jnp.where(lane < DIFF_DQK, zero, q))

    def update(state, kb, vb, bias):
        new = []
        for c in range(2):
            m, l, acc = state[c]
            s = _nt(qs[c], kb)
            if bias is not None:
                s = s + bias[c]
            m_new = jnp.maximum(m, jnp.max(s, axis=-1, keepdims=True))
            alpha = jnp.exp2(m - m_new)
            p = jnp.exp2(s - m_new)
            l = alpha * l + jnp.sum(p, axis=-1, keepdims=True)
            acc = alpha * acc + _mm(p.astype(bf16), vb)
            new.append((m_new, l, acc))
        return tuple(new)

    init = tuple((jnp.full((tq, 1), NEG, f32), jnp.zeros((tq, 1), f32), jnp.zeros((tq, DIFF_DV), f32))
                 for _ in range(2))

    def far(kj, state):
        k0 = pl.multiple_of(kj * tq, tq)
        return update(state, k_ref[0, pl.ds(k0, tq), :], v_ref[0, pl.ds(k0, tq), :], None)

    state = lax.fori_loop(0, jnp.maximum(qi - 1, 0), far, init)

    kd0 = pl.multiple_of(qi * tq, tq)
    state = update(state, k_ref[0, pl.ds(kd0, tq), :], v_ref[0, pl.ds(kd0, tq), :],
                   (bias_ref[0, 0, 1], bias_ref[0, 1, 1]))
    kp0 = pl.multiple_of(jnp.maximum(qi - 1, 0) * tq, tq)
    has_prev = qi > 0
    state = update(state, k_ref[0, pl.ds(kp0, tq), :], v_ref[0, pl.ds(kp0, tq), :],
                   (jnp.where(has_prev, bias_ref[0, 0, 0], NEG), jnp.where(has_prev, bias_ref[0, 1, 0], NEG)))

    (_, l0, a0), (_, l1, a1) = state
    o = a0 / l0 - _attn_lambda(lamv_ref, lambda_init) * (a1 / l1)
    _attn_finish(o, gsub_ref, o_ref, lambda_init)


def _t5_bucket(n):
    max_exact = NUM_BUCKETS // 2
    nf = jnp.maximum(n, 1).astype(f32)
    large = max_exact + (jnp.log(nf / max_exact) / math.log(MAX_DISTANCE / max_exact)
                         * (NUM_BUCKETS - max_exact)).astype(jnp.int32)
    large = jnp.minimum(large, NUM_BUCKETS - 1)
    return jnp.where(n < max_exact, n, large)


def _toeplitz_kernel(w_ref, o_ref):
    n = o_ref.shape[-1]
    for t in range(2):
        rows = jnp.broadcast_to(w_ref[0, t:t + 1, :], (n, 2 * n))
        o_ref[0, 0, t] = pltpu.roll(rows, 0, 1, stride=1, stride_axis=0)[:, n:]


def _bias_tiles(rel_bias_table, S, n):
    HM = rel_bias_table.shape[1]
    assert n >= MAX_DISTANCE
    d = jnp.arange(2 * n, dtype=jnp.int32)
    by_dist = rel_bias_table[_t5_bucket(d)].astype(f32).T
    rel = (by_dist - rel_bias_table[NUM_BUCKETS - 1].astype(f32)[:, None]) * LOG2E
    i = jnp.arange(2 * n)
    w_diag = jnp.where(i[None, :] <= n, rel[:, jnp.clip(n - i, 0, 2 * n - 1)], NEG)
    w_prev = rel[:, jnp.clip(2 * n - i, 0, 2 * n - 1)]
    w = jnp.stack([w_prev, w_diag], axis=1)
    return pl.pallas_call(
        _toeplitz_kernel,
        out_shape=jax.ShapeDtypeStruct((HM // 2, 2, 2, n, n), f32),
        grid=(HM // 2, 2),
        in_specs=[pl.BlockSpec((1, 2, 2 * n), lambda h, m: (h * 2 + m, 0, 0))],
        out_specs=pl.BlockSpec((1, 1, 2, n, n), lambda h, m: (h, m, 0, 0, 0)),
        compiler_params=pltpu.CompilerParams(vmem_limit_bytes=VMEM_LIMIT),
        name="bias_tiles",
    )(w)


def _attn(qd, kd, vd, bias_tiles, lamv, g_subln, lambda_init, bounded):
    B, S, _ = qd.shape
    H = N_DIFF_HEADS
    tq = min(TQ, S)
    body = _attn_bounded_kernel if bounded else _attn_kernel
    scratch = [pltpu.VMEM((S, 2 * DIFF_DV), bf16)] if bounded else []
    return pl.pallas_call(
        functools.partial(body, lambda_init=lambda_init),
        out_shape=jax.ShapeDtypeStruct((B, S, DIFF_V_W), bf16),
        scratch_shapes=scratch,
        grid=(B, H, S // tq),
        in_specs=[pl.BlockSpec((1, tq, 2 * DIFF_DQK), lambda b, h, i: (b, i, h)),
                  pl.BlockSpec((1, S, 2 * DIFF_DQK), lambda b, h, i: (b, 0, h)),
                  pl.BlockSpec((1, S, DIFF_DV), lambda b, h, i: (b, 0, h)),
                  pl.BlockSpec((1, 2, 2, tq, tq), lambda b, h, i: (h, 0, 0, 0, 0)),
                  pl.BlockSpec((4, DIFF_DQK), lambda b, h, i: (0, 0)),
                  pl.BlockSpec((1, DIFF_DV), lambda b, h, i: (0, 0))],
        out_specs=pl.BlockSpec((1, tq, DIFF_DV), lambda b, h, i: (b, i, h)),
        compiler_params=pltpu.CompilerParams(
            dimension_semantics=("arbitrary", "arbitrary", "arbitrary"), vmem_limit_bytes=VMEM_LIMIT),
        name="attn_bounded" if bounded else "attn_online",
    )(qd, kd, vd, bias_tiles, lamv, g_subln.reshape(1, DIFF_DV))


def _scores_bounded(rel_bias_table, g_qnorm, g_knorm):
    qk = DIFF_DQK ** 0.5 * jnp.max(jnp.abs(g_qnorm)) * jnp.max(jnp.abs(g_knorm)) * 1.02
    rel = jnp.max(jnp.abs(rel_bias_table - rel_bias_table[NUM_BUCKETS - 1:]))
    return qk + rel <= SAFE_SCORE


def _rows_to_tiles(x, ref):
    n = x.shape[0]
    for c in range(ROW_TILE):
        ref[pl.ds(c, n, stride=ROW_TILE), :] = x[:, c * LANES:(c + 1) * LANES]


def _tiles_to_rows(ref, n):
    return jnp.concatenate([ref[pl.ds(c, n, stride=ROW_TILE), :] for c in range(ROW_TILE)], axis=1)


def _outproj_kernel(og_ref, od_ref, x_ref, mod_ref, wo_ref, g2_ref, wr_ref, br_ref,
                    x1_ref, hp_ref, lg_ref):
    half = og_ref.shape[2]
    sub = og_ref.shape[1] // INPROJ_SUB
    for t in range(INPROJ_SUB):
        rows = slice(t * sub, (t + 1) * sub)
        mix = _mm(og_ref[0, rows, :], wo_ref[:half, :]) + _mm(od_ref[0, rows, :], wo_ref[half:, :])
        x1 = x_ref[0, rows, :] + mod_ref[0, 2:3, :] * mix
        x1_ref[0, rows, :] = x1
        ms = jnp.mean(x1 * x1, axis=-1, keepdims=True)
        y = x1 * lax.rsqrt(ms + EPS) * g2_ref[...]
        h = (y * (1.0 + mod_ref[0, 4:5, :]) + mod_ref[0, 3:4, :]).astype(bf16)
        lg_ref[rows, :] = _mm(h, wr_ref[...]) + br_ref[...]
        _rows_to_tiles(h.astype(f32), hp_ref.at[pl.ds(t * sub * ROW_TILE, sub * ROW_TILE)])


def _outproj(og, od, x, mod, w_out, g_norm2, w_router, b_router):
    B, S, D = x.shape
    E = w_router.shape[1]
    tm = TM_IN
    nj = S // tm
    w_r = jnp.zeros((D, LANES), f32).at[:, :E].set(w_router).astype(bf16)
    b_r = jnp.full((1, LANES), NEG, f32).at[0, :E].set(b_router)
    const = lambda shape: pl.BlockSpec(shape, lambda b, i: (0,) * len(shape))
    return pl.pallas_call(
        _outproj_kernel,
        out_shape=[jax.ShapeDtypeStruct((B, S, D), f32),
                   jax.ShapeDtypeStruct((B * S * ROW_TILE, LANES), f32),
                   jax.ShapeDtypeStruct((B * S, LANES), f32)],
        grid=(B, nj),
        in_specs=[pl.BlockSpec((1, tm, og.shape[2]), lambda b, i: (b, i, 0)),
                  pl.BlockSpec((1, tm, od.shape[2]), lambda b, i: (b, i, 0)),
                  pl.BlockSpec((1, tm, D), lambda b, i: (b, i, 0)),
                  pl.BlockSpec((1, 6, D), lambda b, i: (b, 0, 0)),
                  const((w_out.shape[0], D)), const((1, D)), const((D, LANES)), const((1, LANES))],
        out_specs=[pl.BlockSpec((1, tm, D), lambda b, i: (b, i, 0)),
                   pl.BlockSpec((tm * ROW_TILE, LANES), lambda b, i: (b * nj + i, 0)),
                   pl.BlockSpec((tm, LANES), lambda b, i: (b * nj + i, 0))],
        compiler_params=pltpu.CompilerParams(
            dimension_semantics=("arbitrary", "arbitrary"), vmem_limit_bytes=VMEM_LIMIT),
        name="outproj",
    )(og, od, x, mod, w_out.astype(bf16), g_norm2.reshape(1, D), w_r, b_r)


def _route_kernel(lg_ref, lt_ref, ri_ref, rw_ref, cnt_ref, run_ref):
    @pl.when(pl.program_id(0) == 0)
    def _():
        run_ref[...] = jnp.zeros_like(run_ref)

    x = lg_ref[...]
    tr = x.shape[0]
    lane = lax.broadcasted_iota(jnp.int32, (tr, LANES), 1)
    lane_f = lane.astype(f32)
    vals, hots, idxs = [], [], []
    for _ in range(TOP_K):
        m = jnp.max(x, axis=-1, keepdims=True)
        idx = jnp.min(jnp.where(x == m, lane_f, float(LANES)), axis=-1, keepdims=True)
        hot = lane_f == idx
        x = jnp.where(hot, -jnp.inf, x)
        vals.append(m)
        hots.append(hot)
        idxs.append(idx.astype(jnp.int32))
    ex = [jnp.exp(v - vals[0]) for v in vals]
    den = ex[0] + ex[1] + ex[2] + ex[3]
    sel = (hots[0] | hots[1] | hots[2] | hots[3]).astype(f32)
    rank = _mm(lt_ref[...], sel.astype(bf16)) + run_ref[...]
    run_ref[...] = run_ref[...] + jnp.sum(sel, axis=0, keepdims=True)
    cnt_ref[...] = run_ref[...]
    ri = jnp.zeros((tr, LANES), jnp.int32)
    rw = jnp.zeros((tr, LANES), f32)
    for k in range(TOP_K):
        rk = jnp.sum(jnp.where(hots[k], rank, 0.0), axis=-1, keepdims=True).astype(jnp.int32)
        ri = jnp.where(lane == k, rk, ri)
        ri = jnp.where(lane == TOP_K + k, idxs[k], ri)
        rw = jnp.where(lane == k, ex[k] / den, rw)
    ri_ref[...] = ri
    rw_ref[...] = rw


def _route(logits):
    T = logits.shape[0]
    tr = min(TR, T)
    r = jnp.arange(tr)
    lt = (r[None, :] < r[:, None]).astype(bf16)
    return pl.pallas_call(
        _route_kernel,
        out_shape=[jax.ShapeDtypeStruct((T, LANES), jnp.int32),
                   jax.ShapeDtypeStruct((T, LANES), f32),
                   jax.ShapeDtypeStruct((1, LANES), f32)],
        grid=(T // tr,),
        in_specs=[pl.BlockSpec((tr, LANES), lambda i: (i, 0)),
                  pl.BlockSpec((tr, tr), lambda i: (0, 0))],
        out_specs=[pl.BlockSpec((tr, LANES), lambda i: (i, 0)),
                   pl.BlockSpec((tr, LANES), lambda i: (i, 0)),
                   pl.BlockSpec((1, LANES), lambda i: (0, 0))],
        scratch_shapes=[pltpu.VMEM((1, LANES), f32)],
        compiler_params=pltpu.CompilerParams(dimension_semantics=("arbitrary",)),
        name="route",
    )(logits, lt)


def _tile(ref, t):
    return ref.at[pl.ds(pl.multiple_of(t * ROW_TILE, ROW_TILE), ROW_TILE)]


def _dispatch_kernel(pend_ref, cnt_ref, nu_ref, dest_ref, h_ref, xs_ref, zero_ref, sem, zsem):
    n_tok = h_ref.shape[0] // ROW_TILE
    blk_rows = FFN_BLK * ROW_TILE

    @pl.when(pl.program_id(0) == 0)
    def _():
        zero_ref[...] = jnp.zeros_like(zero_ref)
        n_exp = pend_ref.shape[0]

        def last_block(e):
            return xs_ref.at[pl.ds(pl.multiple_of((pend_ref[e] - FFN_BLK) * ROW_TILE, blk_rows), blk_rows)]

        def zfill(e, c):
            @pl.when(cnt_ref[e] > 0)
            def _():
                pltpu.make_async_copy(zero_ref, last_block(e), zsem).start()
            return c

        def zwait(e, c):
            @pl.when(cnt_ref[e] > 0)
            def _():
                pltpu.make_async_copy(zero_ref, last_block(e), zsem).wait()
            return c

        lax.fori_loop(0, n_exp, zfill, 0)
        lax.fori_loop(0, n_exp, zwait, 0)

        def tail_block(i):
            return xs_ref.at[pl.ds(pl.multiple_of(i * blk_rows, blk_rows), blk_rows)]

        def tfill(i, c):
            pltpu.make_async_copy(zero_ref, tail_block(i), zsem).start()
            return c

        def twait(i, c):
            pltpu.make_async_copy(zero_ref, tail_block(i), zsem).wait()
            return c

        n_blk = xs_ref.shape[0] // blk_rows
        lax.fori_loop(nu_ref[0], n_blk, tfill, 0)
        lax.fori_loop(nu_ref[0], n_blk, twait, 0)

    def issue(g, c):
        for u in range(DMA_UNROLL):
            r = g * DMA_UNROLL + u
            for k in range(TOP_K):
                pltpu.make_async_copy(_tile(h_ref, r), _tile(xs_ref, dest_ref[r * TOP_K + k]),
                                      sem).start(priority=k % 2)
        return c

    lax.fori_loop(0, n_tok // DMA_UNROLL, issue, 0)
    done = xs_ref.at[pl.ds(0, n_tok * TOP_K * ROW_TILE)]
    pltpu.make_async_copy(done, done, sem).wait()


def _dispatch(p_ends, counts, n_used, dest_flat, hp, n_rows):
    T = hp.shape[0] // ROW_TILE
    grid_spec = pltpu.PrefetchScalarGridSpec(
        num_scalar_prefetch=3,
        grid=(T // TD,),
        in_specs=[pl.BlockSpec((TD * TOP_K,), lambda i, pe, cn, nu: (i,), memory_space=pltpu.SMEM),
                  pl.BlockSpec((TD * ROW_TILE, LANES), lambda i, pe, cn, nu: (i, 0))],
        out_specs=pl.BlockSpec(memory_space=pl.ANY),
        scratch_shapes=[pltpu.VMEM((FFN_BLK * ROW_TILE, LANES), f32),
                        pltpu.SemaphoreType.DMA(()), pltpu.SemaphoreType.DMA(())],
    )
    return pl.pallas_call(
        _dispatch_kernel,
        out_shape=jax.ShapeDtypeStruct((n_rows * ROW_TILE, LANES), f32),
        grid_spec=grid_spec,
        compiler_params=pltpu.CompilerParams(dimension_semantics=("arbitrary",)),
        name="dispatch",
    )(p_ends, counts, n_used, dest_flat, hp)


def _ffn_kernel(be_ref, nu_ref, nx_ref, par_ref, xs_ref, wgu_hbm, bgu_ref, wd_hbm, bd_ref, ys_ref,
                wgu32_ref, wd32_ref, wgu_ref, wd_ref, y_ref, sem):
    i = pl.program_id(0)
    n_used = nu_ref[0]
    used = i < n_used
    cur = jnp.minimum(i, be_ref.shape[0] - 1)
    new_expert = (i == 0) | (be_ref[cur] != be_ref[jnp.maximum(cur - 1, 0)])
    slot = par_ref[cur]

    def weight_copies(e, s):
        return (pltpu.make_async_copy(wgu_hbm.at[e], wgu32_ref.at[s], sem.at[0, s]),
                pltpu.make_async_copy(wd_hbm.at[e], wd32_ref.at[s], sem.at[1, s]))

    @pl.when(i == 0)
    def _():
        for cp in weight_copies(be_ref[0], 0):
            cp.start()

    @pl.when(used & new_expert)
    def _():
        for cp in weight_copies(be_ref[cur], slot):
            cp.wait()

        @pl.when(nx_ref[cur] >= 0)
        def _():
            for cp in weight_copies(nx_ref[cur], 1 - slot):
                cp.start()

        rows = 128

        def cast(src, dst):
            def body(r, c):
                r0 = pl.multiple_of(r * rows, rows)
                dst[pl.ds(r0, rows), :] = src[slot, pl.ds(r0, rows), :].astype(bf16)
                return c
            lax.fori_loop(0, src.shape[1] // rows, body, 0)
        cast(wgu32_ref, wgu_ref)
        cast(wd32_ref, wd_ref.at[slot])

    F = wd_ref.shape[1]
    fc = F // 2

    def gate_up():
        xrow = _tiles_to_rows(xs_ref, FFN_BLK).astype(bf16)
        for c in range(2):
            def gu(col0):
                return _mm(xrow, wgu_ref[:, col0:col0 + fc]) + bgu_ref[0, :, col0:col0 + fc]
            gate = jnp.minimum(gu(c * fc), SWIGLU_LIMIT)
            up = jnp.clip(gu(F + c * fc), -SWIGLU_LIMIT, SWIGLU_LIMIT)
            y = (up + 1.0) * (gate * jax.nn.sigmoid(SWIGLU_ALPHA * gate))
            y_ref[i % 2, :, c * fc:(c + 1) * fc] = y.astype(bf16)

    def down():
        prev = jnp.maximum(i - 1, 0)
        out = _mm(y_ref[prev % 2], wd_ref[par_ref[prev]]) + bd_ref[0]
        _rows_to_tiles(out, ys_ref)

    @pl.when(i == 0)
    def _():
        gate_up()

    @pl.when((i >= 1) & used)
    def _():
        gate_up()
        down()

    @pl.when((i >= 1) & (i == n_used))
    def _():
        down()

    @pl.when(i > n_used)
    def _():
        ys_ref[...] = jnp.zeros_like(ys_ref)


def _ffn(block_e, n_used, xs, w_gate_up, b_gate_up, w_down, b_down):
    E, D, F2 = w_gate_up.shape
    F = F2 // 2
    P = xs.shape[0] // ROW_TILE
    nb = P // FFN_BLK
    rows = FFN_BLK * ROW_TILE

    idx = jnp.arange(nb, dtype=jnp.int32)
    live = idx < n_used[0]
    later_other = (block_e[None, :] != block_e[:, None]) & (idx[None, :] > idx[:, None]) & live[None, :]
    nxt = jnp.where(jnp.any(later_other, axis=1), block_e[jnp.argmax(later_other, axis=1)], -1).astype(jnp.int32)
    starts = jnp.concatenate([jnp.ones((1,), jnp.int32), (block_e[1:] != block_e[:-1]).astype(jnp.int32)])
    parity = ((jnp.cumsum(starts) - 1) % 2).astype(jnp.int32)

    def blk(i, nu):
        return jnp.minimum(i, nu[0] - 1)

    grid_spec = pltpu.PrefetchScalarGridSpec(
        num_scalar_prefetch=4,
        grid=(nb + 1,),
        in_specs=[pl.BlockSpec((rows, LANES), lambda i, be, nu, nx, pa: (blk(i, nu), 0)),
                  pl.BlockSpec(memory_space=pl.ANY),
                  pl.BlockSpec((1, 1, F2), lambda i, be, nu, nx, pa: (be[blk(i, nu)], 0, 0)),
                  pl.BlockSpec(memory_space=pl.ANY),
                  pl.BlockSpec((1, 1, D), lambda i, be, nu, nx, pa: (be[blk(jnp.maximum(i - 1, 0), nu)], 0, 0))],
        out_specs=pl.BlockSpec((rows, LANES), lambda i, be, nu, nx, pa: (jnp.maximum(i - 1, 0), 0)),
        scratch_shapes=[pltpu.VMEM((2, D, F2), f32), pltpu.VMEM((2, F, D), f32),
                        pltpu.VMEM((D, F2), bf16), pltpu.VMEM((2, F, D), bf16),
                        pltpu.VMEM((2, FFN_BLK, F), bf16),
                        pltpu.SemaphoreType.DMA((2, 2))],
    )
    return pl.pallas_call(
        _ffn_kernel,
        out_shape=jax.ShapeDtypeStruct((P * ROW_TILE, LANES), f32),
        grid_spec=grid_spec,
        compiler_params=pltpu.CompilerParams(
            dimension_semantics=("arbitrary",), vmem_limit_bytes=VMEM_LIMIT_FFN),
        name="ffn",
    )(block_e, n_used, nxt, parity, xs, w_gate_up, b_gate_up.reshape(E, 1, F2), w_down, b_down.reshape(E, 1, D))


def _combine_kernel(dcur_ref, dnext_ref, ys_ref, x1_ref, rw_ref, rep_ref, mod_ref, o_ref, buf, acc_ref, sem):
    step = pl.program_id(0) * pl.num_programs(1) + pl.program_id(1)
    n_steps = pl.num_programs(0) * pl.num_programs(1)
    slot = step % 2

    def issue(dref, s):
        def body(g, c):
            for u in range(DMA_UNROLL):
                r = g * DMA_UNROLL + u
                for k in range(TOP_K):
                    pltpu.make_async_copy(_tile(ys_ref, dref[r * TOP_K + k]),
                                          _tile(buf.at[s, k], r), sem.at[s]).start(priority=k % 2)
            return c
        lax.fori_loop(0, TD // DMA_UNROLL, body, 0)

    @pl.when(step == 0)
    def _():
        issue(dcur_ref, 0)

    @pl.when(step + 1 < n_steps)
    def _():
        issue(dnext_ref, 1 - slot)

    pltpu.make_async_copy(buf.at[slot], buf.at[slot], sem.at[slot]).wait()

    rw_hi, rw_lo = _split(rw_ref[...])
    rw8 = _mm(rep_ref[...], rw_hi) + _mm(rep_ref[...], rw_lo)
    moe = rw8[:, 0:1] * buf[slot, 0]
    for k in range(1, TOP_K):
        moe = moe + rw8[:, k:k + 1] * buf[slot, k]
    acc_ref[...] = moe
    o_ref[0] = x1_ref[0] + mod_ref[0, 5:6, :] * _tiles_to_rows(acc_ref, TD)


def _combine(dest_flat, ys, x1, rw, mod):
    B, S, D = x1.shape
    nj = S // TD
    n_steps = B * nj
    rep = (jnp.arange(TD * ROW_TILE)[:, None] // ROW_TILE == jnp.arange(TD)[None, :]).astype(bf16)
    return pl.pallas_call(
        _combine_kernel,
        out_shape=jax.ShapeDtypeStruct((B, S, D), f32),
        grid=(B, nj),
        in_specs=[pl.BlockSpec((TD * TOP_K,), lambda b, j: (b * nj + j,), memory_space=pltpu.SMEM),
                  pl.BlockSpec((TD * TOP_K,), lambda b, j: (jnp.minimum(b * nj + j + 1, n_steps - 1),),
                               memory_space=pltpu.SMEM),
                  pl.BlockSpec(memory_space=pl.ANY),
                  pl.BlockSpec((1, TD, D), lambda b, j: (b, j, 0)),
                  pl.BlockSpec((TD, LANES), lambda b, j: (b * nj + j, 0)),
                  pl.BlockSpec((TD * ROW_TILE, TD), lambda b, j: (0, 0)),
                  pl.BlockSpec((1, 6, D), lambda b, j: (b, 0, 0))],
        out_specs=pl.BlockSpec((1, TD, D), lambda b, j: (b, j, 0)),
        scratch_shapes=[pltpu.VMEM((2, TOP_K, TD * ROW_TILE, LANES), f32),
                        pltpu.VMEM((TD * ROW_TILE, LANES), f32), pltpu.SemaphoreType.DMA((2,))],
        compiler_params=pltpu.CompilerParams(
            dimension_semantics=("arbitrary", "arbitrary"), vmem_limit_bytes=VMEM_LIMIT),
        name="combine",
    )(dest_flat, dest_flat, ys, x1, rw, rep, mod)


def _moe(hp, logits, x1, mod, w_gate_up, b_gate_up, w_down, b_down):
    T = logits.shape[0]
    E = w_gate_up.shape[0]
    ri, rw, cnt = _route(logits)
    rank = ri[:, :TOP_K]
    e_sel = ri[:, TOP_K:2 * TOP_K]
    counts = cnt[0, :E].astype(jnp.int32)
    padded = ((counts + FFN_BLK - 1) // FFN_BLK) * FFN_BLK
    p_ends = jnp.cumsum(padded)
    p_starts = p_ends - padded
    nb = -(-T * TOP_K // FFN_BLK) + E
    n_used = jnp.maximum(p_ends[-1:] // FFN_BLK, 1).astype(jnp.int32)
    blk_start = jnp.arange(nb, dtype=jnp.int32) * FFN_BLK
    block_e = jnp.minimum(jnp.sum(p_ends[None, :] <= blk_start[:, None], axis=1), E - 1).astype(jnp.int32)
    onehot = e_sel[:, :, None] == jnp.arange(E, dtype=jnp.int32)[None, None, :]
    dest = (jnp.sum(jnp.where(onehot, p_starts[None, None, :], 0), axis=-1) + rank).reshape(-1)
    xs = _dispatch(p_ends.astype(jnp.int32), counts, n_used, dest, hp, nb * FFN_BLK)
    ys = _ffn(block_e, n_used, xs, w_gate_up, b_gate_up, w_down, b_down)
    return _combine(dest, ys, x1, rw, mod)


def kernel(x, c, rel_bias_table, w_ada, b_ada, g_norm1, w_in, w_gk_up, b_gk_up, g_gla_out, g_qnorm, g_knorm, lambda_q1, lambda_k1, lambda_q2, lambda_k2, g_subln, w_out, g_norm2, w_router, b_router, w_gate_up, b_gate_up, w_down, b_down):
    B, S, D = x.shape
    depth = w_ada.shape[0]
    bias_tiles = _bias_tiles(rel_bias_table, S, min(TQ, S))
    for l in range(depth):
        lambda_init = 0.8 - 0.6 * math.exp(-0.3 * l)
        mod = _ada(c, w_ada[l], b_ada[l])
        qg, kg, gk, kgt, gkt, vg, rg, qd, kd, vd = _inproj(
            x, mod, g_norm1[l], w_in[l], w_gk_up[l], b_gk_up[l], g_qnorm[l], g_knorm[l])
        og = _gla(qg, kg, gk, kgt, gkt, vg, rg, g_gla_out[l])
        lamv = jnp.stack([lambda_q1[l], lambda_k1[l], lambda_q2[l], lambda_k2[l]]).astype(f32)
        od = lax.cond(_scores_bounded(rel_bias_table, g_qnorm[l], g_knorm[l]),
                      functools.partial(_attn, lambda_init=lambda_init, bounded=True),
                      functools.partial(_attn, lambda_init=lambda_init, bounded=False),
                      qd, kd, vd, bias_tiles, lamv, g_subln[l])
        x1, hp, logits = _outproj(og, od, x, mod, w_out[l], g_norm2[l], w_router[l], b_router[l])
        x = _moe(hp, logits, x1, mod, w_gate_up[l], b_gate_up[l], w_down[l], b_down[l])
    return x
```

```python
import functools
import math

import jax
import jax.numpy as jnp
from jax import lax
from jax.experimental import pallas as pl
from jax.experimental.pallas import tpu as pltpu

f32 = jnp.float32
bf16 = jnp.bfloat16

N_GLA_HEADS = 4
GLA_DK = 64
GLA_DV = 128
GLA_GATE_RANK = 16
GLA_GATE_NORM = 16.0
GLA_CHUNK = 64
N_DIFF_HEADS = 4
DIFF_DQK = 64
DIFF_DV = 128
NUM_BUCKETS = 32
MAX_DISTANCE = 128
TOP_K = 4
SWIGLU_LIMIT = 7.0
SWIGLU_ALPHA = 1.702
EPS = 1e-6

GLA_QK_W = N_GLA_HEADS * GLA_DK
GLA_V_W = N_GLA_HEADS * GLA_DV
DIFF_QK_W = N_DIFF_HEADS * 2 * DIFF_DQK
DIFF_V_W = N_DIFF_HEADS * DIFF_DV

LANES = 128
NEG = -1e30
LOG2E = math.log2(math.e)
SAFE_SCORE = 40.0
VMEM_LIMIT = 48 * 1024 * 1024
VMEM_LIMIT_FFN = 58 * 1024 * 1024

TM_IN = 512
INPROJ_SUB = 2
TG_GLA = 1024
PAIR = 2 * GLA_CHUNK
GLA_UNROLL = 4
TQ = 512
ATTN_UNROLL = 4
TR = 512
TD = 256
ROW_TILE = 8
DMA_UNROLL = 8
FFN_BLK = 512


def _nt(a, b):
    return lax.dot_general(a, b, (((1,), (1,)), ((), ())), preferred_element_type=f32)


def _mm(a, b):
    return jnp.dot(a, b, preferred_element_type=f32)


def _split(x):
    hi = x.astype(bf16)
    lo = (x - hi.astype(f32)).astype(bf16)
    return hi, lo


def _silu(x):
    return x * jax.nn.sigmoid(x)


def _ada_kernel(c_ref, w_ref, b_ref, o_ref):
    c = c_ref[...]
    o_ref[...] = _mm(_silu(c).astype(bf16), w_ref[...].astype(bf16)) + b_ref[...]


def _ada(c, w_ada, b_ada):
    B, D = c.shape
    N = w_ada.shape[1]
    bp = 8
    cp = jnp.zeros((bp, D), f32).at[:B].set(c)
    tn = 1536
    out = pl.pallas_call(
        _ada_kernel,
        out_shape=jax.ShapeDtypeStruct((bp, N), f32),
        grid=(N // tn,),
        in_specs=[pl.BlockSpec((bp, D), lambda j: (0, 0)),
                  pl.BlockSpec((D, tn), lambda j: (0, j)),
                  pl.BlockSpec((1, tn), lambda j: (0, j))],
        out_specs=pl.BlockSpec((bp, tn), lambda j: (0, j)),
        compiler_params=pltpu.CompilerParams(vmem_limit_bytes=VMEM_LIMIT),
        name="ada",
    )(cp, w_ada, b_ada.reshape(1, N))
    return out[:B].reshape(B, 6, D)


def _inproj_kernel(x_ref, mod_ref, g1_ref, wm_ref, wkt_ref, wlo_ref, wup_ref, wupt_ref,
                   bup_ref, bupt_ref, gqk_ref, grp_ref, grpt_ref,
                   qg_ref, kg_ref, gk_ref, kgt_ref, gkt_ref, vg_ref, rg_ref,
                   qd_ref, kd_ref, vd_ref):
    tm = x_ref.shape[1]
    sub = tm // INPROJ_SUB
    for t in range(INPROJ_SUB):
        _inproj_rows(slice(t * sub, (t + 1) * sub), x_ref, mod_ref, g1_ref, wm_ref, wkt_ref, wlo_ref, wup_ref,
                     wupt_ref, bup_ref, bupt_ref, gqk_ref, grp_ref, grpt_ref, qg_ref, kg_ref, gk_ref, kgt_ref,
                     gkt_ref, vg_ref, rg_ref, qd_ref, kd_ref, vd_ref)


def _inproj_rows(rows, x_ref, mod_ref, g1_ref, wm_ref, wkt_ref, wlo_ref, wup_ref, wupt_ref,
                 bup_ref, bupt_ref, gqk_ref, grp_ref, grpt_ref,
                 qg_ref, kg_ref, gk_ref, kgt_ref, gkt_ref, vg_ref, rg_ref, qd_ref, kd_ref, vd_ref):
    x = x_ref[0, rows, :]
    ms = jnp.mean(x * x, axis=-1, keepdims=True)
    y = x * lax.rsqrt(ms + EPS) * g1_ref[...]
    h = (y * (1.0 + mod_ref[0, 1:2, :]) + mod_ref[0, 0:1, :]).astype(bf16)

    def proj(a, b):
        return _mm(h, wm_ref[:, a:b])

    o = 0
    qg_ref[0, rows, :] = proj(o, o + GLA_QK_W); o += GLA_QK_W
    kg_ref[0, rows, :] = proj(o, o + GLA_QK_W); o += GLA_QK_W
    vg_ref[0, rows, :] = proj(o, o + GLA_V_W).astype(bf16); o += GLA_V_W
    rg_ref[0, rows, :] = proj(o, o + GLA_V_W); o += GLA_V_W
    qk = proj(o, o + 2 * DIFF_QK_W); o += 2 * DIFF_QK_W
    vd_ref[0, rows, :] = proj(o, o + DIFF_V_W).astype(bf16)

    slab0 = rows.start // PAIR
    kgt = _nt(wkt_ref[...], h)
    for j in range(kgt.shape[1] // PAIR):
        kgt_ref[0, slab0 + j] = kgt[:, j * PAIR:(j + 1) * PAIR]

    lo = _mm(h, wlo_ref[...]).astype(bf16)
    z = _mm(lo, wup_ref[...]) + bup_ref[...]
    gk_ref[0, rows, :] = (jnp.minimum(z, 0.0) - jnp.log1p(jnp.exp(-jnp.abs(z)))) * (1.0 / GLA_GATE_NORM)
    zt = _nt(wupt_ref[...], lo) + bupt_ref[...]
    gkt = (jnp.minimum(zt, 0.0) - jnp.log1p(jnp.exp(-jnp.abs(zt)))) * (1.0 / GLA_GATE_NORM)
    for j in range(gkt.shape[1] // PAIR):
        gkt_ref[0, slab0 + j] = gkt[:, j * PAIR:(j + 1) * PAIR]

    sq_hi, sq_lo = _split(qk * qk)
    gs = _mm(sq_hi, grp_ref[...]) + _mm(sq_lo, grp_ref[...])
    r = lax.rsqrt(gs * (1.0 / DIFF_DQK) + EPS)
    r_hi, r_lo = _split(r)
    rb = _mm(r_hi, grpt_ref[...]) + _mm(r_lo, grpt_ref[...])
    qkn = qk * rb * gqk_ref[...]
    qd_ref[0, rows, :] = qkn[:, :DIFF_QK_W].astype(bf16)
    kd_ref[0, rows, :] = qkn[:, DIFF_QK_W:].astype(bf16)


def _inproj(x, mod, g_norm1, w_in, w_gk_up, b_gk_up, g_qnorm, g_knorm):
    B, S, D = x.shape
    offs = [0]
    for w in (GLA_QK_W, GLA_QK_W, GLA_V_W, GLA_V_W, GLA_GATE_RANK, DIFF_QK_W, DIFF_QK_W, DIFF_V_W):
        offs.append(offs[-1] + w)
    w_main = jnp.concatenate([w_in[:, offs[0]:offs[4]], w_in[:, offs[5]:offs[8]]], axis=1).astype(bf16)
    w_kt = w_in[:, offs[1]:offs[2]].T.astype(bf16)
    w_lo = jnp.zeros((D, LANES), f32).at[:, :GLA_GATE_RANK].set(w_in[:, offs[4]:offs[5]]).astype(bf16)
    w_up = jnp.zeros((LANES, GLA_QK_W), f32).at[:GLA_GATE_RANK].set(w_gk_up).astype(bf16)
    w_upt = w_up.T
    b_up = b_gk_up.reshape(1, GLA_QK_W)
    b_upt = b_gk_up.reshape(GLA_QK_W, 1)
    n_grp = 2 * DIFF_QK_W // DIFF_DQK
    gqk = jnp.concatenate([jnp.tile(g_qnorm, n_grp // 2) * (DIFF_DQK ** -0.5 * LOG2E),
                           jnp.tile(g_knorm, n_grp // 2)]).reshape(1, 2 * DIFF_QK_W)
    grp = (jnp.arange(2 * DIFF_QK_W)[:, None] // DIFF_DQK == jnp.arange(LANES)[None, :]).astype(bf16)
    grpt = grp.T
    nw = w_main.shape[1]
    tm = TM_IN
    const = lambda shape: pl.BlockSpec(shape, lambda b, i: (0,) * len(shape))
    row = lambda w: pl.BlockSpec((1, tm, w), lambda b, i: (b, i, 0))
    colT = pl.BlockSpec((1, tm // PAIR, GLA_QK_W, PAIR), lambda b, i: (b, i, 0, 0))
    outs = pl.pallas_call(
        _inproj_kernel,
        out_shape=[jax.ShapeDtypeStruct((B, S, GLA_QK_W), f32),
                   jax.ShapeDtypeStruct((B, S, GLA_QK_W), f32),
                   jax.ShapeDtypeStruct((B, S, GLA_QK_W), f32),
                   jax.ShapeDtypeStruct((B, S // PAIR, GLA_QK_W, PAIR), f32),
                   jax.ShapeDtypeStruct((B, S // PAIR, GLA_QK_W, PAIR), f32),
                   jax.ShapeDtypeStruct((B, S, GLA_V_W), bf16),
                   jax.ShapeDtypeStruct((B, S, GLA_V_W), f32),
                   jax.ShapeDtypeStruct((B, S, DIFF_QK_W), bf16),
                   jax.ShapeDtypeStruct((B, S, DIFF_QK_W), bf16),
                   jax.ShapeDtypeStruct((B, S, DIFF_V_W), bf16)],
        grid=(B, S // tm),
        in_specs=[row(D),
                  pl.BlockSpec((1, 6, D), lambda b, i: (b, 0, 0)),
                  const((1, D)), const((D, nw)), const((GLA_QK_W, D)), const((D, LANES)),
                  const((LANES, GLA_QK_W)), const((GLA_QK_W, LANES)),
                  const((1, GLA_QK_W)), const((GLA_QK_W, 1)),
                  const((1, 2 * DIFF_QK_W)), const((2 * DIFF_QK_W, LANES)),
                  const((LANES, 2 * DIFF_QK_W))],
        out_specs=[row(GLA_QK_W), row(GLA_QK_W), row(GLA_QK_W), colT, colT,
                   row(GLA_V_W), row(GLA_V_W), row(DIFF_QK_W), row(DIFF_QK_W), row(DIFF_V_W)],
        compiler_params=pltpu.CompilerParams(
            dimension_semantics=("arbitrary", "arbitrary"), vmem_limit_bytes=VMEM_LIMIT),
        name="inproj",
    )(x, mod, g_norm1.reshape(1, D), w_main, w_kt, w_lo, w_up, w_upt, b_up, b_upt, gqk, grp, grpt)
    return outs


def _gla_kernel(q_ref, k_ref, g_ref, kt_ref, gt_ref, v_ref, r_ref, gout_ref, tri_ref, trit_ref,
                o_ref, s_ref, *, n_pairs):
    H, DK, DV = N_GLA_HEADS, GLA_DK, GLA_DV

    @pl.when(pl.program_id(1) == 0)
    def _():
        s_ref[...] = jnp.zeros_like(s_ref)

    tri = tri_ref[...]
    trit = trit_ref[...]
    tri_b = tri > 0
    lane_head = lax.broadcasted_iota(jnp.int32, (1, H * DK), 1) // DK
    row_head = lax.broadcasted_iota(jnp.int32, (H * PAIR, 1), 0) // PAIR
    qmask = row_head == lane_head
    row_first = lax.broadcasted_iota(jnp.int32, (PAIR, 1), 0) < GLA_CHUNK
    row_first4 = (lax.broadcasted_iota(jnp.int32, (H * PAIR, 1), 0) % PAIR) < GLA_CHUNK
    lane_first = lax.broadcasted_iota(jnp.int32, (1, PAIR), 1) < GLA_CHUNK
    scale = DK ** -0.5
    gout = gout_ref[...]

    def pair(p, carry):
        r0 = pl.multiple_of(p * PAIR, PAIR)
        q = q_ref[0, pl.ds(r0, PAIR), :]
        k = k_ref[0, pl.ds(r0, PAIR), :]
        g = g_ref[0, pl.ds(r0, PAIR), :]
        kt = kt_ref[0, p]
        gt = gt_ref[0, p]
        v = v_ref[0, pl.ds(r0, PAIR), :]

        g_hi, g_lo = _split(g)
        gc = _mm(tri, g_hi) + _mm(tri, g_lo)
        gt_hi, gt_lo = _split(gt)
        gct = _mm(gt_hi, trit) + _mm(gt_lo, trit)
        g_last = jnp.where(row_first, gc[GLA_CHUNK - 1:GLA_CHUNK, :], gc[PAIR - 1:PAIR, :])
        gl0 = gct[:, GLA_CHUNK - 1:GLA_CHUNK]
        gl1 = gct[:, PAIR - 1:PAIR]
        g_last_t = jnp.where(lane_first, gl0, gl1)

        q_e = (q * (jnp.exp(gc) * scale)).astype(bf16)
        k_e = (k * jnp.exp(-gc)).astype(bf16)
        ks_t = kt * jnp.exp(g_last_t - gct)
        ks_t0 = jnp.where(lane_first, ks_t, 0.0).astype(bf16)
        ks_t1 = jnp.where(lane_first, 0.0, ks_t).astype(bf16)
        del g_last

        qm = jnp.where(qmask, jnp.concatenate([q_e] * H, axis=0), jnp.zeros((), bf16))
        a = _nt(qm, k_e)
        s0 = s_ref[...]

        u0 = []
        u1 = []
        for h in range(H):
            v_h = v[:, h * DV:(h + 1) * DV]
            u0.append(_mm(ks_t0[h * DK:(h + 1) * DK], v_h))
            u1.append(_mm(ks_t1[h * DK:(h + 1) * DK], v_h))
        u0 = jnp.concatenate(u0, axis=0)
        u1 = jnp.concatenate(u1, axis=0)
        s1 = s0 * jnp.exp(gl0) + u0
        s_ref[...] = s1 * jnp.exp(gl1) + u1

        o_inter = jnp.where(row_first4, _mm(qm, s0.astype(bf16)), _mm(qm, s1.astype(bf16)))
        for h in range(H):
            a_h = jnp.where(tri_b, a[h * PAIR:(h + 1) * PAIR], 0.0).astype(bf16)
            o_h = _mm(a_h, v[:, h * DV:(h + 1) * DV]) + o_inter[h * PAIR:(h + 1) * PAIR]
            ms = jnp.mean(o_h * o_h, axis=-1, keepdims=True)
            o_n = o_h * lax.rsqrt(ms + EPS) * gout
            r_h = r_ref[0, pl.ds(r0, PAIR), h * DV:(h + 1) * DV]
            o_ref[0, pl.ds(r0, PAIR), h * DV:(h + 1) * DV] = (o_n * _silu(r_h)).astype(bf16)
        return carry

    lax.fori_loop(0, n_pairs, pair, 0, unroll=GLA_UNROLL)


def _gla(qg, kg, gk, kgt, gkt, vg, rg, g_gla_out):
    B, S, _ = qg.shape
    tg = min(TG_GLA, S)
    r = jnp.arange(PAIR)
    tri = ((r[:, None] // GLA_CHUNK == r[None, :] // GLA_CHUNK) & (r[None, :] <= r[:, None])).astype(bf16)
    row = lambda w: pl.BlockSpec((1, tg, w), lambda b, i: (b, i, 0))
    colT = pl.BlockSpec((1, tg // PAIR, GLA_QK_W, PAIR), lambda b, i: (b, i, 0, 0))
    const = lambda shape: pl.BlockSpec(shape, lambda b, i: (0,) * len(shape))
    return pl.pallas_call(
        functools.partial(_gla_kernel, n_pairs=tg // PAIR),
        out_shape=jax.ShapeDtypeStruct((B, S, GLA_V_W), bf16),
        grid=(B, S // tg),
        in_specs=[row(GLA_QK_W), row(GLA_QK_W), row(GLA_QK_W), colT, colT,
                  row(GLA_V_W), row(GLA_V_W), const((1, GLA_DV)),
                  const((PAIR, PAIR)), const((PAIR, PAIR))],
        out_specs=row(GLA_V_W),
        scratch_shapes=[pltpu.VMEM((GLA_QK_W, GLA_DV), f32)],
        compiler_params=pltpu.CompilerParams(
            dimension_semantics=("arbitrary", "arbitrary"), vmem_limit_bytes=VMEM_LIMIT),
        name="gla",
    )(qg, kg, gk, kgt, gkt, vg, rg, g_gla_out.reshape(1, GLA_DV), tri, tri.T)


def _attn_finish(o, gsub_ref, o_ref, lambda_init):
    ms = jnp.mean(o * o, axis=-1, keepdims=True)
    o_ref[0] = (o * lax.rsqrt(ms + EPS) * gsub_ref[...] * (1.0 - lambda_init)).astype(bf16)


def _attn_lambda(lamv_ref, lambda_init):
    lv = lamv_ref[...]
    return (jnp.exp(jnp.sum(lv[0:1] * lv[1:2], axis=-1, keepdims=True))
            - jnp.exp(jnp.sum(lv[2:3] * lv[3:4], axis=-1, keepdims=True)) + lambda_init)


def _attn_bounded_kernel(q_ref, k_ref, v_ref, bias_ref, lamv_ref, gsub_ref, o_ref, vaug_ref, *, lambda_init):
    qi = pl.program_id(2)
    tq = q_ref.shape[1]
    S = k_ref.shape[1]

    @pl.when(qi == 0)
    def _():
        lane = lax.broadcasted_iota(jnp.int32, (S, DIFF_DV), 1)
        vaug_ref[:, :DIFF_DV] = v_ref[0]
        vaug_ref[:, DIFF_DV:] = jnp.where(lane == 0, 1.0, 0.0).astype(bf16)

    q = q_ref[0]
    lane = lax.broadcasted_iota(jnp.int32, (1, 2 * DIFF_DQK), 1)
    zero = jnp.zeros((), bf16)
    qs = (jnp.where(lane < DIFF_DQK, q, zero), jnp.where(lane < DIFF_DQK, zero, q))

    def update(accs, k0, bias):
        kb = k_ref[0, pl.ds(k0, tq), :]
        vb = vaug_ref[pl.ds(k0, tq), :]
        out = []
        for c in range(2):
            s = _nt(qs[c], kb)
            if bias is not None:
                s = s + bias[c]
            out.append(accs[c] + _mm(jnp.exp2(s).astype(bf16), vb))
        return tuple(out)

    def far(kj, accs):
        return update(accs, pl.multiple_of(kj * tq, tq), None)

    def far_group(g, accs):
        for u in range(ATTN_UNROLL):
            accs = far(g * ATTN_UNROLL + u, accs)
        return accs

    def block_or_masked(accs, kj, bias):
        exists = kj >= 0
        k0 = pl.multiple_of(jnp.maximum(kj, 0) * tq, tq)
        if bias is None:
            tiles = (jnp.where(exists, 0.0, NEG),) * 2
        else:
            tiles = tuple(jnp.where(exists, b, NEG) for b in bias)
        return update(accs, k0, tiles)

    accs = (jnp.zeros((tq, 2 * DIFF_DV), f32), jnp.zeros((tq, 2 * DIFF_DV), f32))
    accs = update(accs, pl.multiple_of(qi * tq, tq), (bias_ref[0, 0, 1], bias_ref[0, 1, 1]))
    accs = block_or_masked(accs, qi - 1, (bias_ref[0, 0, 0], bias_ref[0, 1, 0]))
    for u in range(2, ATTN_UNROLL):
        accs = block_or_masked(accs, qi - u, None)
    n_far = jnp.maximum(qi + 1 - ATTN_UNROLL, 0)
    n_grp = n_far // ATTN_UNROLL
    accs = lax.fori_loop(0, n_grp, far_group, accs)
    accs = lax.fori_loop(n_grp * ATTN_UNROLL, n_far, far, accs)
    a0, a1 = accs
    o = (a0[:, :DIFF_DV] / a0[:, DIFF_DV:DIFF_DV + 1]
         - _attn_lambda(lamv_ref, lambda_init) * (a1[:, :DIFF_DV] / a1[:, DIFF_DV:DIFF_DV + 1]))
    _attn_finish(o, gsub_ref, o_ref, lambda_init)


def _attn_kernel(q_ref, k_ref, v_ref, bias_ref, lamv_ref, gsub_ref, o_ref, *, lambda_init):
    qi = pl.program_id(2)
    tq = q_ref.shape[1]
    q = q_ref[0]
    lane = lax.broadcasted_iota(jnp.int32, (1, 2 * DIFF_DQK), 1)
    zero = jnp.zeros((), bf16)
    qs = (jnp.where(lane < DIFF_DQK, q, zero), jnp.where(lane < DIFF_DQK, zero, q))

    def update(state, kb, vb, bias):
        new = []
        for c in range(2):
            m, l, acc = state[c]
            s = _nt(qs[c], kb)
            if bias is not None:
                s = s + bias[c]
            m_new = jnp.maximum(m, jnp.max(s, axis=-1, keepdims=True))
            alpha = jnp.exp2(m - m_new)
            p = jnp.exp2(s - m_new)
            l = alpha * l + jnp.sum(p, axis=-1, keepdims=True)
            acc = alpha * acc + _mm(p.astype(bf16), vb)
            new.append((m_new, l, acc))
        return tuple(new)

    init = tuple((jnp.full((tq, 1), NEG, f32), jnp.zeros((tq, 1), f32), jnp.zeros((tq, DIFF_DV), f32))
                 for _ in range(2))

    def far(kj, state):
        k0 = pl.multiple_of(kj * tq, tq)
        return update(state, k_ref[0, pl.ds(k0, tq), :], v_ref[0, pl.ds(k0, tq), :], None)

    state = lax.fori_loop(0, jnp.maximum(qi - 1, 0), far, init)

    kd0 = pl.multiple_of(qi * tq, tq)
    state = update(state, k_ref[0, pl.ds(kd0, tq), :], v_ref[0, pl.ds(kd0, tq), :],
                   (bias_ref[0, 0, 1], bias_ref[0, 1, 1]))
    kp0 = pl.multiple_of(jnp.maximum(qi - 1, 0) * tq, tq)
    has_prev = qi > 0
    state = update(state, k_ref[0, pl.ds(kp0, tq), :], v_ref[0, pl.ds(kp0, tq), :],
                   (jnp.where(has_prev, bias_ref[0, 0, 0], NEG), jnp.where(has_prev, bias_ref[0, 1, 0], NEG)))

    (_, l0, a0), (_, l1, a1) = state
    o = a0 / l0 - _attn_lambda(lamv_ref, lambda_init) * (a1 / l1)
    _attn_finish(o, gsub_ref, o_ref, lambda_init)


def _t5_bucket(n):
    max_exact = NUM_BUCKETS // 2
    nf = jnp.maximum(n, 1).astype(f32)
    large = max_exact + (jnp.log(nf / max_exact) / math.log(MAX_DISTANCE / max_exact)
                         * (NUM_BUCKETS - max_exact)).astype(jnp.int32)
    large = jnp.minimum(large, NUM_BUCKETS - 1)
    return jnp.where(n < max_exact, n, large)


def _toeplitz_kernel(w_ref, o_ref):
    n = o_ref.shape[-1]
    for t in range(2):
        rows = jnp.broadcast_to(w_ref[0, t:t + 1, :], (n, 2 * n))
        o_ref[0, 0, t] = pltpu.roll(rows, 0, 1, stride=1, stride_axis=0)[:, n:]


def _bias_tiles(rel_bias_table, S, n):
    HM = rel_bias_table.shape[1]
    assert n >= MAX_DISTANCE
    d = jnp.arange(2 * n, dtype=jnp.int32)
    by_dist = rel_bias_table[_t5_bucket(d)].astype(f32).T
    rel = (by_dist - rel_bias_table[NUM_BUCKETS - 1].astype(f32)[:, None]) * LOG2E
    i = jnp.arange(2 * n)
    w_diag = jnp.where(i[None, :] <= n, rel[:, jnp.clip(n - i, 0, 2 * n - 1)], NEG)
    w_prev = rel[:, jnp.clip(2 * n - i, 0, 2 * n - 1)]
    w = jnp.stack([w_prev, w_diag], axis=1)
    return pl.pallas_call(
        _toeplitz_kernel,
        out_shape=jax.ShapeDtypeStruct((HM // 2, 2, 2, n, n), f32),
        grid=(HM // 2, 2),
        in_specs=[pl.BlockSpec((1, 2, 2 * n), lambda h, m: (h * 2 + m, 0, 0))],
        out_specs=pl.BlockSpec((1, 1, 2, n, n), lambda h, m: (h, m, 0, 0, 0)),
        compiler_params=pltpu.CompilerParams(vmem_limit_bytes=VMEM_LIMIT),
        name="bias_tiles",
    )(w)


def _attn(qd, kd, vd, bias_tiles, lamv, g_subln, lambda_init, bounded):
    B, S, _ = qd.shape
    H = N_DIFF_HEADS
    tq = min(TQ, S)
    body = _attn_bounded_kernel if bounded else _attn_kernel
    scratch = [pltpu.VMEM((S, 2 * DIFF_DV), bf16)] if bounded else []
    return pl.pallas_call(
        functools.partial(body, lambda_init=lambda_init),
        out_shape=jax.ShapeDtypeStruct((B, S, DIFF_V_W), bf16),
        scratch_shapes=scratch,
        grid=(B, H, S // tq),
        in_specs=[pl.BlockSpec((1, tq, 2 * DIFF_DQK), lambda b, h, i: (b, i, h)),
                  pl.BlockSpec((1, S, 2 * DIFF_DQK), lambda b, h, i: (b, 0, h)),
                  pl.BlockSpec((1, S, DIFF_DV), lambda b, h, i: (b, 0, h)),
                  pl.BlockSpec((1, 2, 2, tq, tq), lambda b, h, i: (h, 0, 0, 0, 0)),
                  pl.BlockSpec((4, DIFF_DQK), lambda b, h, i: (0, 0)),
                  pl.BlockSpec((1, DIFF_DV), lambda b, h, i: (0, 0))],
        out_specs=pl.BlockSpec((1, tq, DIFF_DV), lambda b, h, i: (b, i, h)),
        compiler_params=pltpu.CompilerParams(
            dimension_semantics=("arbitrary", "arbitrary", "arbitrary"), vmem_limit_bytes=VMEM_LIMIT),
        name="attn_bounded" if bounded else "attn_online",
    )(qd, kd, vd, bias_tiles, lamv, g_subln.reshape(1, DIFF_DV))


def _scores_bounded(rel_bias_table, g_qnorm, g_knorm):
    qk = DIFF_DQK ** 0.5 * jnp.max(jnp.abs(g_qnorm)) * jnp.max(jnp.abs(g_knorm)) * 1.02
    rel = jnp.max(jnp.abs(rel_bias_table - rel_bias_table[NUM_BUCKETS - 1:]))
    return qk + rel <= SAFE_SCORE


def _rows_to_tiles(x, ref):
    n = x.shape[0]
    for c in range(ROW_TILE):
        ref[pl.ds(c, n, stride=ROW_TILE), :] = x[:, c * LANES:(c + 1) * LANES]


def _tiles_to_rows(ref, n):
    return jnp.concatenate([ref[pl.ds(c, n, stride=ROW_TILE), :] for c in range(ROW_TILE)], axis=1)


def _outproj_kernel(og_ref, od_ref, x_ref, mod_ref, wo_ref, g2_ref, wr_ref, br_ref,
                    x1_ref, hp_ref, lg_ref):
    half = og_ref.shape[2]
    sub = og_ref.shape[1] // INPROJ_SUB
    for t in range(INPROJ_SUB):
        rows = slice(t * sub, (t + 1) * sub)
        mix = _mm(og_ref[0, rows, :], wo_ref[:half, :]) + _mm(od_ref[0, rows, :], wo_ref[half:, :])
        x1 = x_ref[0, rows, :] + mod_ref[0, 2:3, :] * mix
        x1_ref[0, rows, :] = x1
        ms = jnp.mean(x1 * x1, axis=-1, keepdims=True)
        y = x1 * lax.rsqrt(ms + EPS) * g2_ref[...]
        h = (y * (1.0 + mod_ref[0, 4:5, :]) + mod_ref[0, 3:4, :]).astype(bf16)
        lg_ref[rows, :] = _mm(h, wr_ref[...]) + br_ref[...]
        _rows_to_tiles(h.astype(f32), hp_ref.at[pl.ds(t * sub * ROW_TILE, sub * ROW_TILE)])


def _outproj(og, od, x, mod, w_out, g_norm2, w_router, b_router):
    B, S, D = x.shape
    E = w_router.shape[1]
    tm = TM_IN
    nj = S // tm
    w_r = jnp.zeros((D, LANES), f32).at[:, :E].set(w_router).astype(bf16)
    b_r = jnp.full((1, LANES), NEG, f32).at[0, :E].set(b_router)
    const = lambda shape: pl.BlockSpec(shape, lambda b, i: (0,) * len(shape))
    return pl.pallas_call(
        _outproj_kernel,
        out_shape=[jax.ShapeDtypeStruct((B, S, D), f32),
                   jax.ShapeDtypeStruct((B * S * ROW_TILE, LANES), f32),
                   jax.ShapeDtypeStruct((B * S, LANES), f32)],
        grid=(B, nj),
        in_specs=[pl.BlockSpec((1, tm, og.shape[2]), lambda b, i: (b, i, 0)),
                  pl.BlockSpec((1, tm, od.shape[2]), lambda b, i: (b, i, 0)),
                  pl.BlockSpec((1, tm, D), lambda b, i: (b, i, 0)),
                  pl.BlockSpec((1, 6, D), lambda b, i: (b, 0, 0)),
                  const((w_out.shape[0], D)), const((1, D)), const((D, LANES)), const((1, LANES))],
        out_specs=[pl.BlockSpec((1, tm, D), lambda b, i: (b, i, 0)),
                   pl.BlockSpec((tm * ROW_TILE, LANES), lambda b, i: (b * nj + i, 0)),
                   pl.BlockSpec((tm, LANES), lambda b, i: (b * nj + i, 0))],
        compiler_params=pltpu.CompilerParams(
            dimension_semantics=("arbitrary", "arbitrary"), vmem_limit_bytes=VMEM_LIMIT),
        name="outproj",
    )(og, od, x, mod, w_out.astype(bf16), g_norm2.reshape(1, D), w_r, b_r)


def _route_kernel(lg_ref, lt_ref, ri_ref, rw_ref, cnt_ref, run_ref):
    @pl.when(pl.program_id(0) == 0)
    def _():
        run_ref[...] = jnp.zeros_like(run_ref)

    x = lg_ref[...]
    tr = x.shape[0]
    lane = lax.broadcasted_iota(jnp.int32, (tr, LANES), 1)
    lane_f = lane.astype(f32)
    vals, hots, idxs = [], [], []
    for _ in range(TOP_K):
        m = jnp.max(x, axis=-1, keepdims=True)
        idx = jnp.min(jnp.where(x == m, lane_f, float(LANES)), axis=-1, keepdims=True)
        hot = lane_f == idx
        x = jnp.where(hot, -jnp.inf, x)
        vals.append(m)
        hots.append(hot)
        idxs.append(idx.astype(jnp.int32))
    ex = [jnp.exp(v - vals[0]) for v in vals]
    den = ex[0] + ex[1] + ex[2] + ex[3]
    sel = (hots[0] | hots[1] | hots[2] | hots[3]).astype(f32)
    rank = _mm(lt_ref[...], sel.astype(bf16)) + run_ref[...]
    run_ref[...] = run_ref[...] + jnp.sum(sel, axis=0, keepdims=True)
    cnt_ref[...] = run_ref[...]
    ri = jnp.zeros((tr, LANES), jnp.int32)
    rw = jnp.zeros((tr, LANES), f32)
    for k in range(TOP_K):
        rk = jnp.sum(jnp.where(hots[k], rank, 0.0), axis=-1, keepdims=True).astype(jnp.int32)
        ri = jnp.where(lane == k, rk, ri)
        ri = jnp.where(lane == TOP_K + k, idxs[k], ri)
        rw = jnp.where(lane == k, ex[k] / den, rw)
    ri_ref[...] = ri
    rw_ref[...] = rw


def _route(logits):
    T = logits.shape[0]
    tr = min(TR, T)
    r = jnp.arange(tr)
    lt = (r[None, :] < r[:, None]).astype(bf16)
    return pl.pallas_call(
        _route_kernel,
        out_shape=[jax.ShapeDtypeStruct((T, LANES), jnp.int32),
                   jax.ShapeDtypeStruct((T, LANES), f32),
                   jax.ShapeDtypeStruct((1, LANES), f32)],
        grid=(T // tr,),
        in_specs=[pl.BlockSpec((tr, LANES), lambda i: (i, 0)),
                  pl.BlockSpec((tr, tr), lambda i: (0, 0))],
        out_specs=[pl.BlockSpec((tr, LANES), lambda i: (i, 0)),
                   pl.BlockSpec((tr, LANES), lambda i: (i, 0)),
                   pl.BlockSpec((1, LANES), lambda i: (0, 0))],
        scratch_shapes=[pltpu.VMEM((1, LANES), f32)],
        compiler_params=pltpu.CompilerParams(dimension_semantics=("arbitrary",)),
        name="route",
    )(logits, lt)


def _tile(ref, t):
    return ref.at[pl.ds(pl.multiple_of(t * ROW_TILE, ROW_TILE), ROW_TILE)]


def _dispatch_kernel(pend_ref, cnt_ref, nu_ref, dest_ref, h_ref, xs_ref, zero_ref, sem, zsem):
    n_tok = h_ref.shape[0] // ROW_TILE
    blk_rows = FFN_BLK * ROW_TILE

    @pl.when(pl.program_id(0) == 0)
    def _():
        zero_ref[...] = jnp.zeros_like(zero_ref)
        n_exp = pend_ref.shape[0]

        def last_block(e):
            return xs_ref.at[pl.ds(pl.multiple_of((pend_ref[e] - FFN_BLK) * ROW_TILE, blk_rows), blk_rows)]

        def zfill(e, c):
            @pl.when(cnt_ref[e] > 0)
            def _():
                pltpu.make_async_copy(zero_ref, last_block(e), zsem).start()
            return c

        def zwait(e, c):
            @pl.when(cnt_ref[e] > 0)
            def _():
                pltpu.make_async_copy(zero_ref, last_block(e), zsem).wait()
            return c

        lax.fori_loop(0, n_exp, zfill, 0)
        lax.fori_loop(0, n_exp, zwait, 0)

        def tail_block(i):
            return xs_ref.at[pl.ds(pl.multiple_of(i * blk_rows, blk_rows), blk_rows)]

        def tfill(i, c):
            pltpu.make_async_copy(zero_ref, tail_block(i), zsem).start()
            return c

        def twait(i, c):
            pltpu.make_async_copy(zero_ref, tail_block(i), zsem).wait()
            return c

        n_blk = xs_ref.shape[0] // blk_rows
        lax.fori_loop(nu_ref[0], n_blk, tfill, 0)
        lax.fori_loop(nu_ref[0], n_blk, twait, 0)

    def issue(g, c):
        for u in range(DMA_UNROLL):
            r = g * DMA_UNROLL + u
            for k in range(TOP_K):
                pltpu.make_async_copy(_tile(h_ref, r), _tile(xs_ref, dest_ref[r * TOP_K + k]),
                                      sem).start(priority=k % 2)
        return c

    lax.fori_loop(0, n_tok // DMA_UNROLL, issue, 0)
    done = xs_ref.at[pl.ds(0, n_tok * TOP_K * ROW_TILE)]
    pltpu.make_async_copy(done, done, sem).wait()


def _dispatch(p_ends, counts, n_used, dest_flat, hp, n_rows):
    T = hp.shape[0] // ROW_TILE
    grid_spec = pltpu.PrefetchScalarGridSpec(
        num_scalar_prefetch=3,
        grid=(T // TD,),
        in_specs=[pl.BlockSpec((TD * TOP_K,), lambda i, pe, cn, nu: (i,), memory_space=pltpu.SMEM),
                  pl.BlockSpec((TD * ROW_TILE, LANES), lambda i, pe, cn, nu: (i, 0))],
        out_specs=pl.BlockSpec(memory_space=pl.ANY),
        scratch_shapes=[pltpu.VMEM((FFN_BLK * ROW_TILE, LANES), f32),
                        pltpu.SemaphoreType.DMA(()), pltpu.SemaphoreType.DMA(())],
    )
    return pl.pallas_call(
        _dispatch_kernel,
        out_shape=jax.ShapeDtypeStruct((n_rows * ROW_TILE, LANES), f32),
        grid_spec=grid_spec,
        compiler_params=pltpu.CompilerParams(dimension_semantics=("arbitrary",)),
        name="dispatch",
    )(p_ends, counts, n_used, dest_flat, hp)


def _ffn_kernel(be_ref, nu_ref, nx_ref, par_ref, xs_ref, wgu_hbm, bgu_ref, wd_hbm, bd_ref, ys_ref,
                wgu32_ref, wd32_ref, wgu_ref, wd_ref, sem):
    i = pl.program_id(0)
    used = i < nu_ref[0]
    new_expert = (i == 0) | (be_ref[i] != be_ref[jnp.maximum(i - 1, 0)])
    slot = par_ref[i]

    def weight_copies(e, s):
        return (pltpu.make_async_copy(wgu_hbm.at[e], wgu32_ref.at[s], sem.at[0, s]),
                pltpu.make_async_copy(wd_hbm.at[e], wd32_ref.at[s], sem.at[1, s]))

    @pl.when(i == 0)
    def _():
        for cp in weight_copies(be_ref[0], 0):
            cp.start()

    @pl.when(used & new_expert)
    def _():
        for cp in weight_copies(be_ref[i], slot):
            cp.wait()

        @pl.when(nx_ref[i] >= 0)
        def _():
            for cp in weight_copies(nx_ref[i], 1 - slot):
                cp.start()

        rows = 128

        def cast(src, dst):
            def body(r, c):
                r0 = pl.multiple_of(r * rows, rows)
                dst[pl.ds(r0, rows), :] = src[slot, pl.ds(r0, rows), :].astype(bf16)
                return c
            lax.fori_loop(0, src.shape[1] // rows, body, 0)
        cast(wgu32_ref, wgu_ref)
        cast(wd32_ref, wd_ref)

    @pl.when(used)
    def _():
        F = wd_ref.shape[0]
        xrow = _tiles_to_rows(xs_ref, FFN_BLK).astype(bf16)
        acc = None
        fc = F // 2
        for c in range(2):
            def gu(col0):
                return _mm(xrow, wgu_ref[:, col0:col0 + fc]) + bgu_ref[0, :, col0:col0 + fc]
            gate = jnp.minimum(gu(c * fc), SWIGLU_LIMIT)
            up = jnp.clip(gu(F + c * fc), -SWIGLU_LIMIT, SWIGLU_LIMIT)
            y = (up + 1.0) * (gate * jax.nn.sigmoid(SWIGLU_ALPHA * gate))
            part = _mm(y.astype(bf16), wd_ref[c * fc:(c + 1) * fc, :])
            acc = part if acc is None else acc + part
        _rows_to_tiles(acc + bd_ref[0], ys_ref)

    @pl.when(jnp.logical_not(used))
    def _():
        ys_ref[...] = jnp.zeros_like(ys_ref)


def _ffn(block_e, n_used, xs, w_gate_up, b_gate_up, w_down, b_down):
    E, D, F2 = w_gate_up.shape
    F = F2 // 2
    P = xs.shape[0] // ROW_TILE
    nb = P // FFN_BLK
    rows = FFN_BLK * ROW_TILE

    idx = jnp.arange(nb, dtype=jnp.int32)
    live = idx < n_used[0]
    later_other = (block_e[None, :] != block_e[:, None]) & (idx[None, :] > idx[:, None]) & live[None, :]
    nxt = jnp.where(jnp.any(later_other, axis=1), block_e[jnp.argmax(later_other, axis=1)], -1).astype(jnp.int32)
    starts = jnp.concatenate([jnp.ones((1,), jnp.int32), (block_e[1:] != block_e[:-1]).astype(jnp.int32)])
    parity = ((jnp.cumsum(starts) - 1) % 2).astype(jnp.int32)

    def blk(i, nu):
        return jnp.minimum(i, nu[0] - 1)

    grid_spec = pltpu.PrefetchScalarGridSpec(
        num_scalar_prefetch=4,
        grid=(nb,),
        in_specs=[pl.BlockSpec((rows, LANES), lambda i, be, nu, nx, pa: (blk(i, nu), 0)),
                  pl.BlockSpec(memory_space=pl.ANY),
                  pl.BlockSpec((1, 1, F2), lambda i, be, nu, nx, pa: (be[blk(i, nu)], 0, 0)),
                  pl.BlockSpec(memory_space=pl.ANY),
                  pl.BlockSpec((1, 1, D), lambda i, be, nu, nx, pa: (be[blk(i, nu)], 0, 0))],
        out_specs=pl.BlockSpec((rows, LANES), lambda i, be, nu, nx, pa: (i, 0)),
        scratch_shapes=[pltpu.VMEM((2, D, F2), f32), pltpu.VMEM((2, F, D), f32),
                        pltpu.VMEM((D, F2), bf16), pltpu.VMEM((F, D), bf16),
                        pltpu.SemaphoreType.DMA((2, 2))],
    )
    return pl.pallas_call(
        _ffn_kernel,
        out_shape=jax.ShapeDtypeStruct((P * ROW_TILE, LANES), f32),
        grid_spec=grid_spec,
        compiler_params=pltpu.CompilerParams(
            dimension_semantics=("arbitrary",), vmem_limit_bytes=VMEM_LIMIT_FFN),
        name="ffn",
    )(block_e, n_used, nxt, parity, xs, w_gate_up, b_gate_up.reshape(E, 1, F2), w_down, b_down.reshape(E, 1, D))


def _combine_kernel(dcur_ref, dnext_ref, ys_ref, x1_ref, rw_ref, rep_ref, mod_ref, o_ref, buf, acc_ref, sem):
    step = pl.program_id(0) * pl.num_programs(1) + pl.program_id(1)
    n_steps = pl.num_programs(0) * pl.num_programs(1)
    slot = step % 2

    def issue(dref, s):
        def body(g, c):
            for u in range(DMA_UNROLL):
                r = g * DMA_UNROLL + u
                for k in range(TOP_K):
                    pltpu.make_async_copy(_tile(ys_ref, dref[r * TOP_K + k]),
                                          _tile(buf.at[s, k], r), sem.at[s]).start(priority=k % 2)
            return c
        lax.fori_loop(0, TD // DMA_UNROLL, body, 0)

    @pl.when(step == 0)
    def _():
        issue(dcur_ref, 0)

    @pl.when(step + 1 < n_steps)
    def _():
        issue(dnext_ref, 1 - slot)

    pltpu.make_async_copy(buf.at[slot], buf.at[slot], sem.at[slot]).wait()

    rw_hi, rw_lo = _split(rw_ref[...])
    rw8 = _mm(rep_ref[...], rw_hi) + _mm(rep_ref[...], rw_lo)
    moe = rw8[:, 0:1] * buf[slot, 0]
    for k in range(1, TOP_K):
        moe = moe + rw8[:, k:k + 1] * buf[slot, k]
    acc_ref[...] = moe
    o_ref[0] = x1_ref[0] + mod_ref[0, 5:6, :] * _tiles_to_rows(acc_ref, TD)


def _combine(dest_flat, ys, x1, rw, mod):
    B, S, D = x1.shape
    nj = S // TD
    n_steps = B * nj
    rep = (jnp.arange(TD * ROW_TILE)[:, None] // ROW_TILE == jnp.arange(TD)[None, :]).astype(bf16)
    return pl.pallas_call(
        _combine_kernel,
        out_shape=jax.ShapeDtypeStruct((B, S, D), f32),
        grid=(B, nj),
        in_specs=[pl.BlockSpec((TD * TOP_K,), lambda b, j: (b * nj + j,), memory_space=pltpu.SMEM),
                  pl.BlockSpec((TD * TOP_K,), lambda b, j: (jnp.minimum(b * nj + j + 1, n_steps - 1),),
                               memory_space=pltpu.SMEM),
                  pl.BlockSpec(memory_space=pl.ANY),
                  pl.BlockSpec((1, TD, D), lambda b, j: (b, j, 0)),
                  pl.BlockSpec((TD, LANES), lambda b, j: (b * nj + j, 0)),
                  pl.BlockSpec((TD * ROW_TILE, TD), lambda b, j: (0, 0)),
                  pl.BlockSpec((1, 6, D), lambda b, j: (b, 0, 0))],
        out_specs=pl.BlockSpec((1, TD, D), lambda b, j: (b, j, 0)),
        scratch_shapes=[pltpu.VMEM((2, TOP_K, TD * ROW_TILE, LANES), f32),
                        pltpu.VMEM((TD * ROW_TILE, LANES), f32), pltpu.SemaphoreType.DMA((2,))],
        compiler_params=pltpu.CompilerParams(
            dimension_semantics=("arbitrary", "arbitrary"), vmem_limit_bytes=VMEM_LIMIT),
        name="combine",
    )(dest_flat, dest_flat, ys, x1, rw, rep, mod)


def _moe(hp, logits, x1, mod, w_gate_up, b_gate_up, w_down, b_down):
    T = logits.shape[0]
    E = w_gate_up.shape[0]
    ri, rw, cnt = _route(logits)
    rank = ri[:, :TOP_K]
    e_sel = ri[:, TOP_K:2 * TOP_K]
    counts = cnt[0, :E].astype(jnp.int32)
    padded = ((counts + FFN_BLK - 1) // FFN_BLK) * FFN_BLK
    p_ends = jnp.cumsum(padded)
    p_starts = p_ends - padded
    nb = -(-T * TOP_K // FFN_BLK) + E
    n_used = jnp.maximum(p_ends[-1:] // FFN_BLK, 1).astype(jnp.int32)
    blk_start = jnp.arange(nb, dtype=jnp.int32) * FFN_BLK
    block_e = jnp.minimum(jnp.sum(p_ends[None, :] <= blk_start[:, None], axis=1), E - 1).astype(jnp.int32)
    onehot = e_sel[:, :, None] == jnp.arange(E, dtype=jnp.int32)[None, None, :]
    dest = (jnp.sum(jnp.where(onehot, p_starts[None, None, :], 0), axis=-1) + rank).reshape(-1)
    xs = _dispatch(p_ends.astype(jnp.int32), counts, n_used, dest, hp, nb * FFN_BLK)
    ys = _ffn(block_e, n_used, xs, w_gate_up, b_gate_up, w_down, b_down)
    return _combine(dest, ys, x1, rw, mod)


def kernel(x, c, rel_bias_table, w_ada, b_ada, g_norm1, w_in, w_gk_up, b_gk_up, g_gla_out, g_qnorm, g_knorm, lambda_q1, lambda_k1, lambda_q2, lambda_k2, g_subln, w_out, g_norm2, w_router, b_router, w_gate_up, b_gate_up, w_down, b_down):
    B, S, D = x.shape
    depth = w_ada.shape[0]
    bias_tiles = _bias_tiles(rel_bias_table, S, min(TQ, S))
    for l in range(depth):
        lambda_init = 0.8 - 0.6 * math.exp(-0.3 * l)
        mod = _ada(c, w_ada[l], b_ada[l])
        qg, kg, gk, kgt, gkt, vg, rg, qd, kd, vd = _inproj(
            x, mod, g_norm1[l], w_in[l], w_gk_up[l], b_gk_up[l], g_qnorm[l], g_knorm[l])
        og = _gla(qg, kg, gk, kgt, gkt, vg, rg, g_gla_out[l])
        lamv = jnp.stack([lambda_q1[l], lambda_k1[l], lambda_q2[l], lambda_k2[l]]).astype(f32)
        od = lax.cond(_scores_bounded(rel_bias_table, g_qnorm[l], g_knorm[l]),
                      functools.partial(_attn, lambda_init=lambda_init, bounded=True),
                      functools.partial(_attn, lambda_init=lambda_init, bounded=False),
                      qd, kd, vd, bias_tiles, lamv, g_subln[l])
        x1, hp, logits = _outproj(og, od, x, mod, w_out[l], g_norm2[l], w_router[l], b_router[l])
        x = _moe(hp, logits, x1, mod, w_gate_up[l], b_gate_up[l], w_down[l], b_down[l])
    return x
```

```python
import functools
import math

import jax
import jax.numpy as jnp
from jax import lax
from jax.experimental import pallas as pl
from jax.experimental.pallas import tpu as pltpu

f32 = jnp.float32
bf16 = jnp.bfloat16

N_GLA_HEADS = 4
GLA_DK = 64
GLA_DV = 128
GLA_GATE_RANK = 16
GLA_GATE_NORM = 16.0
GLA_CHUNK = 64
N_DIFF_HEADS = 4
DIFF_DQK = 64
DIFF_DV = 128
NUM_BUCKETS = 32
MAX_DISTANCE = 128
TOP_K = 4
SWIGLU_LIMIT = 7.0
SWIGLU_ALPHA = 1.702
EPS = 1e-6

GLA_QK_W = N_GLA_HEADS * GLA_DK
GLA_V_W = N_GLA_HEADS * GLA_DV
DIFF_QK_W = N_DIFF_HEADS * 2 * DIFF_DQK
DIFF_V_W = N_DIFF_HEADS * DIFF_DV

LANES = 128
NEG = -1e30
LOG2E = math.log2(math.e)
SAFE_SCORE = 40.0
NORM_SLACK = 1.02
VMEM_LIMIT = 48 * 1024 * 1024
VMEM_LIMIT_FFN = 58 * 1024 * 1024

TM_IN = 512
INPROJ_SUB = 2
TG_GLA = 1024
PAIR = 2 * GLA_CHUNK
GLA_UNROLL = 4
TQ = 512
ATTN_UNROLL = 4
TR = 512
TD = 256
ROW_TILE = 8
DMA_UNROLL = 8
FFN_BLK = 512


def _nt(a, b):
    return lax.dot_general(a, b, (((1,), (1,)), ((), ())), preferred_element_type=f32)


def _mm(a, b):
    return jnp.dot(a, b, preferred_element_type=f32)


def _split(x):
    hi = x.astype(bf16)
    lo = (x - hi.astype(f32)).astype(bf16)
    return hi, lo


def _silu(x):
    return x * jax.nn.sigmoid(x)


def _ada_kernel(c_ref, w_ref, b_ref, o_ref):
    c = c_ref[...]
    o_ref[...] = _mm(_silu(c).astype(bf16), w_ref[...].astype(bf16)) + b_ref[...]


def _ada(c, w_ada, b_ada):
    B, D = c.shape
    N = w_ada.shape[1]
    bp = ROW_TILE
    assert B <= bp
    cp = jnp.zeros((bp, D), f32).at[:B].set(c)
    tn = N // 4
    out = pl.pallas_call(
        _ada_kernel,
        out_shape=jax.ShapeDtypeStruct((bp, N), f32),
        grid=(N // tn,),
        in_specs=[pl.BlockSpec((bp, D), lambda j: (0, 0)),
                  pl.BlockSpec((D, tn), lambda j: (0, j)),
                  pl.BlockSpec((1, tn), lambda j: (0, j))],
        out_specs=pl.BlockSpec((bp, tn), lambda j: (0, j)),
        compiler_params=pltpu.CompilerParams(vmem_limit_bytes=VMEM_LIMIT),
        name="ada",
    )(cp, w_ada, b_ada.reshape(1, N))
    return out[:B].reshape(B, 6, D)


def _inproj_kernel(x_ref, mod_ref, g1_ref, wm_ref, wkt_ref, wlo_ref, wup_ref, wupt_ref,
                   bup_ref, bupt_ref, gqk_ref, grp_ref, grpt_ref,
                   qg_ref, kg_ref, gk_ref, kgt_ref, gkt_ref, vg_ref, rg_ref,
                   qd_ref, kd_ref, vd_ref):
    tm = x_ref.shape[1]
    sub = tm // INPROJ_SUB
    for t in range(INPROJ_SUB):
        _inproj_rows(slice(t * sub, (t + 1) * sub), x_ref, mod_ref, g1_ref, wm_ref, wkt_ref, wlo_ref, wup_ref,
                     wupt_ref, bup_ref, bupt_ref, gqk_ref, grp_ref, grpt_ref, qg_ref, kg_ref, gk_ref, kgt_ref,
                     gkt_ref, vg_ref, rg_ref, qd_ref, kd_ref, vd_ref)


def _inproj_rows(rows, x_ref, mod_ref, g1_ref, wm_ref, wkt_ref, wlo_ref, wup_ref, wupt_ref,
                 bup_ref, bupt_ref, gqk_ref, grp_ref, grpt_ref,
                 qg_ref, kg_ref, gk_ref, kgt_ref, gkt_ref, vg_ref, rg_ref, qd_ref, kd_ref, vd_ref):
    x = x_ref[0, rows, :]
    ms = jnp.mean(x * x, axis=-1, keepdims=True)
    y = x * lax.rsqrt(ms + EPS) * g1_ref[...]
    h = (y * (1.0 + mod_ref[0, 1:2, :]) + mod_ref[0, 0:1, :]).astype(bf16)

    def proj(a, b):
        return _mm(h, wm_ref[:, a:b])

    o = 0
    qg_ref[0, rows, :] = proj(o, o + GLA_QK_W); o += GLA_QK_W
    kg_ref[0, rows, :] = proj(o, o + GLA_QK_W); o += GLA_QK_W
    vg_ref[0, rows, :] = proj(o, o + GLA_V_W).astype(bf16); o += GLA_V_W
    rg_ref[0, rows, :] = proj(o, o + GLA_V_W); o += GLA_V_W
    qk = proj(o, o + 2 * DIFF_QK_W); o += 2 * DIFF_QK_W
    vd_ref[0, rows, :] = proj(o, o + DIFF_V_W).astype(bf16)

    slab0 = rows.start // PAIR
    kgt = _nt(wkt_ref[...], h)
    for j in range(kgt.shape[1] // PAIR):
        kgt_ref[0, slab0 + j] = kgt[:, j * PAIR:(j + 1) * PAIR]

    lo = _mm(h, wlo_ref[...]).astype(bf16)
    z = _mm(lo, wup_ref[...]) + bup_ref[...]
    gk_ref[0, rows, :] = (jnp.minimum(z, 0.0) - jnp.log1p(jnp.exp(-jnp.abs(z)))) * (1.0 / GLA_GATE_NORM)
    zt = _nt(wupt_ref[...], lo) + bupt_ref[...]
    gkt = (jnp.minimum(zt, 0.0) - jnp.log1p(jnp.exp(-jnp.abs(zt)))) * (1.0 / GLA_GATE_NORM)
    for j in range(gkt.shape[1] // PAIR):
        gkt_ref[0, slab0 + j] = gkt[:, j * PAIR:(j + 1) * PAIR]

    sq_hi, sq_lo = _split(qk * qk)
    gs = _mm(sq_hi, grp_ref[...]) + _mm(sq_lo, grp_ref[...])
    r = lax.rsqrt(gs * (1.0 / DIFF_DQK) + EPS)
    r_hi, r_lo = _split(r)
    rb = _mm(r_hi, grpt_ref[...]) + _mm(r_lo, grpt_ref[...])
    qkn = qk * rb * gqk_ref[...]
    qd_ref[0, rows, :] = qkn[:, :DIFF_QK_W].astype(bf16)
    kd_ref[0, rows, :] = qkn[:, DIFF_QK_W:].astype(bf16)


def _inproj(x, mod, g_norm1, w_in, w_gk_up, b_gk_up, g_qnorm, g_knorm):
    B, S, D = x.shape
    offs = [0]
    for w in (GLA_QK_W, GLA_QK_W, GLA_V_W, GLA_V_W, GLA_GATE_RANK, DIFF_QK_W, DIFF_QK_W, DIFF_V_W):
        offs.append(offs[-1] + w)
    w_main = jnp.concatenate([w_in[:, offs[0]:offs[4]], w_in[:, offs[5]:offs[8]]], axis=1).astype(bf16)
    w_kt = w_in[:, offs[1]:offs[2]].T.astype(bf16)
    w_lo = jnp.zeros((D, LANES), f32).at[:, :GLA_GATE_RANK].set(w_in[:, offs[4]:offs[5]]).astype(bf16)
    w_up = jnp.zeros((LANES, GLA_QK_W), f32).at[:GLA_GATE_RANK].set(w_gk_up).astype(bf16)
    w_upt = w_up.T
    b_up = b_gk_up.reshape(1, GLA_QK_W)
    b_upt = b_gk_up.reshape(GLA_QK_W, 1)
    n_grp = 2 * DIFF_QK_W // DIFF_DQK
    gqk = jnp.concatenate([jnp.tile(g_qnorm, n_grp // 2) * (DIFF_DQK ** -0.5 * LOG2E),
                           jnp.tile(g_knorm, n_grp // 2)]).reshape(1, 2 * DIFF_QK_W)
    grp = (jnp.arange(2 * DIFF_QK_W)[:, None] // DIFF_DQK == jnp.arange(LANES)[None, :]).astype(bf16)
    grpt = grp.T
    nw = w_main.shape[1]
    tm = TM_IN
    const = lambda shape: pl.BlockSpec(shape, lambda b, i: (0,) * len(shape))
    row = lambda w: pl.BlockSpec((1, tm, w), lambda b, i: (b, i, 0))
    colT = pl.BlockSpec((1, tm // PAIR, GLA_QK_W, PAIR), lambda b, i: (b, i, 0, 0))
    outs = pl.pallas_call(
        _inproj_kernel,
        out_shape=[jax.ShapeDtypeStruct((B, S, GLA_QK_W), f32),
                   jax.ShapeDtypeStruct((B, S, GLA_QK_W), f32),
                   jax.ShapeDtypeStruct((B, S, GLA_QK_W), f32),
                   jax.ShapeDtypeStruct((B, S // PAIR, GLA_QK_W, PAIR), f32),
                   jax.ShapeDtypeStruct((B, S // PAIR, GLA_QK_W, PAIR), f32),
                   jax.ShapeDtypeStruct((B, S, GLA_V_W), bf16),
                   jax.ShapeDtypeStruct((B, S, GLA_V_W), f32),
                   jax.ShapeDtypeStruct((B, S, DIFF_QK_W), bf16),
                   jax.ShapeDtypeStruct((B, S, DIFF_QK_W), bf16),
                   jax.ShapeDtypeStruct((B, S, DIFF_V_W), bf16)],
        grid=(B, S // tm),
        in_specs=[row(D),
                  pl.BlockSpec((1, 6, D), lambda b, i: (b, 0, 0)),
                  const((1, D)), const((D, nw)), const((GLA_QK_W, D)), const((D, LANES)),
                  const((LANES, GLA_QK_W)), const((GLA_QK_W, LANES)),
                  const((1, GLA_QK_W)), const((GLA_QK_W, 1)),
                  const((1, 2 * DIFF_QK_W)), const((2 * DIFF_QK_W, LANES)),
                  const((LANES, 2 * DIFF_QK_W))],
        out_specs=[row(GLA_QK_W), row(GLA_QK_W), row(GLA_QK_W), colT, colT,
                   row(GLA_V_W), row(GLA_V_W), row(DIFF_QK_W), row(DIFF_QK_W), row(DIFF_V_W)],
        compiler_params=pltpu.CompilerParams(
            dimension_semantics=("arbitrary", "arbitrary"), vmem_limit_bytes=VMEM_LIMIT),
        name="inproj",
    )(x, mod, g_norm1.reshape(1, D), w_main, w_kt, w_lo, w_up, w_upt, b_up, b_upt, gqk, grp, grpt)
    return outs


def _gla_kernel(q_ref, k_ref, g_ref, kt_ref, gt_ref, v_ref, r_ref, gout_ref, tri_ref, trit_ref,
                o_ref, s_ref, *, n_pairs):
    H, DK, DV = N_GLA_HEADS, GLA_DK, GLA_DV

    @pl.when(pl.program_id(1) == 0)
    def _():
        s_ref[...] = jnp.zeros_like(s_ref)

    tri = tri_ref[...]
    trit = trit_ref[...]
    tri_b = tri > 0
    lane_head = lax.broadcasted_iota(jnp.int32, (1, H * DK), 1) // DK
    row_head = lax.broadcasted_iota(jnp.int32, (H * PAIR, 1), 0) // PAIR
    qmask = row_head == lane_head
    row_first = lax.broadcasted_iota(jnp.int32, (PAIR, 1), 0) < GLA_CHUNK
    row_first4 = (lax.broadcasted_iota(jnp.int32, (H * PAIR, 1), 0) % PAIR) < GLA_CHUNK
    lane_first = lax.broadcasted_iota(jnp.int32, (1, PAIR), 1) < GLA_CHUNK
    scale = DK ** -0.5
    gout = gout_ref[...]

    def pair(p, carry):
        r0 = pl.multiple_of(p * PAIR, PAIR)
        q = q_ref[0, pl.ds(r0, PAIR), :]
        k = k_ref[0, pl.ds(r0, PAIR), :]
        g = g_ref[0, pl.ds(r0, PAIR), :]
        kt = kt_ref[0, p]
        gt = gt_ref[0, p]
        v = v_ref[0, pl.ds(r0, PAIR), :]

        g_hi, g_lo = _split(g)
        gc = _mm(tri, g_hi) + _mm(tri, g_lo)
        gt_hi, gt_lo = _split(gt)
        gct = _mm(gt_hi, trit) + _mm(gt_lo, trit)
        g_last = jnp.where(row_first, gc[GLA_CHUNK - 1:GLA_CHUNK, :], gc[PAIR - 1:PAIR, :])
        gl0 = gct[:, GLA_CHUNK - 1:GLA_CHUNK]
        gl1 = gct[:, PAIR - 1:PAIR]
        g_last_t = jnp.where(lane_first, gl0, gl1)

        q_e = (q * (jnp.exp(gc) * scale)).astype(bf16)
        k_e = (k * jnp.exp(-gc)).astype(bf16)
        ks_t = kt * jnp.exp(g_last_t - gct)
        ks_t0 = jnp.where(lane_first, ks_t, 0.0).astype(bf16)
        ks_t1 = jnp.where(lane_first, 0.0, ks_t).astype(bf16)
        del g_last

        qm = jnp.where(qmask, jnp.concatenate([q_e] * H, axis=0), jnp.zeros((), bf16))
        a = _nt(qm, k_e)
        s0 = s_ref[...]

        u0 = []
        u1 = []
        for h in range(H):
            v_h = v[:, h * DV:(h + 1) * DV]
            u0.append(_mm(ks_t0[h * DK:(h + 1) * DK], v_h))
            u1.append(_mm(ks_t1[h * DK:(h + 1) * DK], v_h))
        u0 = jnp.concatenate(u0, axis=0)
        u1 = jnp.concatenate(u1, axis=0)
        s1 = s0 * jnp.exp(gl0) + u0
        s_ref[...] = s1 * jnp.exp(gl1) + u1

        o_inter = jnp.where(row_first4, _mm(qm, s0.astype(bf16)), _mm(qm, s1.astype(bf16)))
        for h in range(H):
            a_h = jnp.where(tri_b, a[h * PAIR:(h + 1) * PAIR], 0.0).astype(bf16)
            o_h = _mm(a_h, v[:, h * DV:(h + 1) * DV]) + o_inter[h * PAIR:(h + 1) * PAIR]
            ms = jnp.mean(o_h * o_h, axis=-1, keepdims=True)
            o_n = o_h * lax.rsqrt(ms + EPS) * gout
            r_h = r_ref[0, pl.ds(r0, PAIR), h * DV:(h + 1) * DV]
            o_ref[0, pl.ds(r0, PAIR), h * DV:(h + 1) * DV] = (o_n * _silu(r_h)).astype(bf16)
        return carry

    lax.fori_loop(0, n_pairs, pair, 0, unroll=GLA_UNROLL)


def _gla(qg, kg, gk, kgt, gkt, vg, rg, g_gla_out):
    B, S, _ = qg.shape
    tg = min(TG_GLA, S)
    r = jnp.arange(PAIR)
    tri = ((r[:, None] // GLA_CHUNK == r[None, :] // GLA_CHUNK) & (r[None, :] <= r[:, None])).astype(bf16)
    row = lambda w: pl.BlockSpec((1, tg, w), lambda b, i: (b, i, 0))
    colT = pl.BlockSpec((1, tg // PAIR, GLA_QK_W, PAIR), lambda b, i: (b, i, 0, 0))
    const = lambda shape: pl.BlockSpec(shape, lambda b, i: (0,) * len(shape))
    return pl.pallas_call(
        functools.partial(_gla_kernel, n_pairs=tg // PAIR),
        out_shape=jax.ShapeDtypeStruct((B, S, GLA_V_W), bf16),
        grid=(B, S // tg),
        in_specs=[row(GLA_QK_W), row(GLA_QK_W), row(GLA_QK_W), colT, colT,
                  row(GLA_V_W), row(GLA_V_W), const((1, GLA_DV)),
                  const((PAIR, PAIR)), const((PAIR, PAIR))],
        out_specs=row(GLA_V_W),
        scratch_shapes=[pltpu.VMEM((GLA_QK_W, GLA_DV), f32)],
        compiler_params=pltpu.CompilerParams(
            dimension_semantics=("arbitrary", "arbitrary"), vmem_limit_bytes=VMEM_LIMIT),
        name="gla",
    )(qg, kg, gk, kgt, gkt, vg, rg, g_gla_out.reshape(1, GLA_DV), tri, tri.T)


def _attn_finish(o, gsub_ref, o_ref, lambda_init):
    ms = jnp.mean(o * o, axis=-1, keepdims=True)
    o_ref[0] = (o * lax.rsqrt(ms + EPS) * gsub_ref[...] * (1.0 - lambda_init)).astype(bf16)


def _attn_lambda(lamv_ref, lambda_init):
    lv = lamv_ref[...]
    return (jnp.exp(jnp.sum(lv[0:1] * lv[1:2], axis=-1, keepdims=True))
            - jnp.exp(jnp.sum(lv[2:3] * lv[3:4], axis=-1, keepdims=True)) + lambda_init)


def _attn_bounded_kernel(q_ref, k_ref, v_ref, bias_ref, lamv_ref, gsub_ref, o_ref, vaug_ref, *, lambda_init):
    qi = pl.program_id(2)
    tq = q_ref.shape[1]
    S = k_ref.shape[1]

    @pl.when(qi == 0)
    def _():
        lane = lax.broadcasted_iota(jnp.int32, (S, DIFF_DV), 1)
        vaug_ref[:, :DIFF_DV] = v_ref[0]
        vaug_ref[:, DIFF_DV:] = jnp.where(lane == 0, 1.0, 0.0).astype(bf16)

    q = q_ref[0]
    lane = lax.broadcasted_iota(jnp.int32, (1, 2 * DIFF_DQK), 1)
    zero = jnp.zeros((), bf16)
    qs = (jnp.where(lane < DIFF_DQK, q, zero), jnp.where(lane < DIFF_DQK, zero, q))

    def update(accs, k0, bias):
        kb = k_ref[0, pl.ds(k0, tq), :]
        vb = vaug_ref[pl.ds(k0, tq), :]
        out = []
        for c in range(2):
            s = _nt(qs[c], kb)
            if bias is not None:
                s = s + bias[c]
            out.append(accs[c] + _mm(jnp.exp2(s).astype(bf16), vb))
        return tuple(out)

    def far(kj, accs):
        return update(accs, pl.multiple_of(kj * tq, tq), None)

    def far_group(g, accs):
        for u in range(ATTN_UNROLL):
            accs = far(g * ATTN_UNROLL + u, accs)
        return accs

    def block_or_masked(accs, kj, bias):
        exists = kj >= 0
        k0 = pl.multiple_of(jnp.maximum(kj, 0) * tq, tq)
        if bias is None:
            tiles = (jnp.where(exists, 0.0, NEG),) * 2
        else:
            tiles = tuple(jnp.where(exists, b, NEG) for b in bias)
        return update(accs, k0, tiles)

    accs = (jnp.zeros((tq, 2 * DIFF_DV), f32), jnp.zeros((tq, 2 * DIFF_DV), f32))
    accs = update(accs, pl.multiple_of(qi * tq, tq), (bias_ref[0, 0, 1], bias_ref[0, 1, 1]))
    accs = block_or_masked(accs, qi - 1, (bias_ref[0, 0, 0], bias_ref[0, 1, 0]))
    for u in range(2, ATTN_UNROLL):
        accs = block_or_masked(accs, qi - u, None)
    n_far = jnp.maximum(qi + 1 - ATTN_UNROLL, 0)
    n_grp = n_far // ATTN_UNROLL
    accs = lax.fori_loop(0, n_grp, far_group, accs)
    accs = lax.fori_loop(n_grp * ATTN_UNROLL, n_far, far, accs)
    a0, a1 = accs
    o = (a0[:, :DIFF_DV] / a0[:, DIFF_DV:DIFF_DV + 1]
         - _attn_lambda(lamv_ref, lambda_init) * (a1[:, :DIFF_DV] / a1[:, DIFF_DV:DIFF_DV + 1]))
    _attn_finish(o, gsub_ref, o_ref, lambda_init)


def _attn_kernel(q_ref, k_ref, v_ref, bias_ref, lamv_ref, gsub_ref, o_ref, *, lambda_init):
    qi = pl.program_id(2)
    tq = q_ref.shape[1]
    q = q_ref[0]
    lane = lax.broadcasted_iota(jnp.int32, (1, 2 * DIFF_DQK), 1)
    zero = jnp.zeros((), bf16)
    qs = (jnp.where(lane < DIFF_DQK, q, zero), jnp.where(lane < DIFF_DQK, zero, q))

    def update(state, kb, vb, bias):
        new = []
        for c in range(2):
            m, l, acc = state[c]
            s = _nt(qs[c], kb)
            if bias is not None:
                s = s + bias[c]
            m_new = jnp.maximum(m, jnp.max(s, axis=-1, keepdims=True))
            alpha = jnp.exp2(m - m_new)
            p = jnp.exp2(s - m_new)
            l = alpha * l + jnp.sum(p, axis=-1, keepdims=True)
            acc = alpha * acc + _mm(p.astype(bf16), vb)
            new.append((m_new, l, acc))
        return tuple(new)

    init = tuple((jnp.full((tq, 1), NEG, f32), jnp.zeros((tq, 1), f32), jnp.zeros((tq, DIFF_DV), f32))
                 for _ in range(2))

    def far(kj, state):
        k0 = pl.multiple_of(kj * tq, tq)
        return update(state, k_ref[0, pl.ds(k0, tq), :], v_ref[0, pl.ds(k0, tq), :], None)

    state = lax.fori_loop(0, jnp.maximum(qi - 1, 0), far, init)

    kd0 = pl.multiple_of(qi * tq, tq)
    state = update(state, k_ref[0, pl.ds(kd0, tq), :], v_ref[0, pl.ds(kd0, tq), :],
                   (bias_ref[0, 0, 1], bias_ref[0, 1, 1]))
    kp0 = pl.multiple_of(jnp.maximum(qi - 1, 0) * tq, tq)
    has_prev = qi > 0
    state = update(state, k_ref[0, pl.ds(kp0, tq), :], v_ref[0, pl.ds(kp0, tq), :],
                   (jnp.where(has_prev, bias_ref[0, 0, 0], NEG), jnp.where(has_prev, bias_ref[0, 1, 0], NEG)))

    (_, l0, a0), (_, l1, a1) = state
    o = a0 / l0 - _attn_lambda(lamv_ref, lambda_init) * (a1 / l1)
    _attn_finish(o, gsub_ref, o_ref, lambda_init)


def _t5_bucket(n):
    max_exact = NUM_BUCKETS // 2
    nf = jnp.maximum(n, 1).astype(f32)
    large = max_exact + (jnp.log(nf / max_exact) / math.log(MAX_DISTANCE / max_exact)
                         * (NUM_BUCKETS - max_exact)).astype(jnp.int32)
    large = jnp.minimum(large, NUM_BUCKETS - 1)
    return jnp.where(n < max_exact, n, large)


def _toeplitz_kernel(w_ref, o_ref):
    n = o_ref.shape[-1]
    for t in range(2):
        rows = jnp.broadcast_to(w_ref[0, t:t + 1, :], (n, 2 * n))
        o_ref[0, 0, t] = pltpu.roll(rows, 0, 1, stride=1, stride_axis=0)[:, n:]


def _bias_tiles(rel_bias_table, S, n):
    HM = rel_bias_table.shape[1]
    assert n >= MAX_DISTANCE
    d = jnp.arange(2 * n, dtype=jnp.int32)
    by_dist = rel_bias_table[_t5_bucket(d)].astype(f32).T
    rel = (by_dist - rel_bias_table[NUM_BUCKETS - 1].astype(f32)[:, None]) * LOG2E
    i = jnp.arange(2 * n)
    w_diag = jnp.where(i[None, :] <= n, rel[:, jnp.clip(n - i, 0, 2 * n - 1)], NEG)
    w_prev = rel[:, jnp.clip(2 * n - i, 0, 2 * n - 1)]
    w = jnp.stack([w_prev, w_diag], axis=1)
    return pl.pallas_call(
        _toeplitz_kernel,
        out_shape=jax.ShapeDtypeStruct((HM // 2, 2, 2, n, n), f32),
        grid=(HM // 2, 2),
        in_specs=[pl.BlockSpec((1, 2, 2 * n), lambda h, m: (h * 2 + m, 0, 0))],
        out_specs=pl.BlockSpec((1, 1, 2, n, n), lambda h, m: (h, m, 0, 0, 0)),
        compiler_params=pltpu.CompilerParams(vmem_limit_bytes=VMEM_LIMIT),
        name="bias_tiles",
    )(w)


def _attn(qd, kd, vd, bias_tiles, lamv, g_subln, lambda_init, bounded):
    B, S, _ = qd.shape
    H = N_DIFF_HEADS
    tq = min(TQ, S)
    body = _attn_bounded_kernel if bounded else _attn_kernel
    scratch = [pltpu.VMEM((S, 2 * DIFF_DV), bf16)] if bounded else []
    return pl.pallas_call(
        functools.partial(body, lambda_init=lambda_init),
        out_shape=jax.ShapeDtypeStruct((B, S, DIFF_V_W), bf16),
        scratch_shapes=scratch,
        grid=(B, H, S // tq),
        in_specs=[pl.BlockSpec((1, tq, 2 * DIFF_DQK), lambda b, h, i: (b, i, h)),
                  pl.BlockSpec((1, S, 2 * DIFF_DQK), lambda b, h, i: (b, 0, h)),
                  pl.BlockSpec((1, S, DIFF_DV), lambda b, h, i: (b, 0, h)),
                  pl.BlockSpec((1, 2, 2, tq, tq), lambda b, h, i: (h, 0, 0, 0, 0)),
                  pl.BlockSpec((4, DIFF_DQK), lambda b, h, i: (0, 0)),
                  pl.BlockSpec((1, DIFF_DV), lambda b, h, i: (0, 0))],
        out_specs=pl.BlockSpec((1, tq, DIFF_DV), lambda b, h, i: (b, i, h)),
        compiler_params=pltpu.CompilerParams(
            dimension_semantics=("arbitrary", "arbitrary", "arbitrary"), vmem_limit_bytes=VMEM_LIMIT),
        name="attn_bounded" if bounded else "attn_online",
    )(qd, kd, vd, bias_tiles, lamv, g_subln.reshape(1, DIFF_DV))


def _scores_bounded(rel_bias_table, g_qnorm, g_knorm):
    qk = DIFF_DQK ** 0.5 * jnp.max(jnp.abs(g_qnorm)) * jnp.max(jnp.abs(g_knorm)) * NORM_SLACK
    rel = jnp.max(jnp.abs(rel_bias_table - rel_bias_table[NUM_BUCKETS - 1:]))
    return qk + rel <= SAFE_SCORE


def _rows_to_tiles(x, ref):
    n = x.shape[0]
    for c in range(ROW_TILE):
        ref[pl.ds(c, n, stride=ROW_TILE), :] = x[:, c * LANES:(c + 1) * LANES]


def _tiles_to_rows(ref, n):
    return jnp.concatenate([ref[pl.ds(c, n, stride=ROW_TILE), :] for c in range(ROW_TILE)], axis=1)


def _outproj_kernel(og_ref, od_ref, x_ref, mod_ref, wo_ref, g2_ref, wr_ref, br_ref,
                    x1_ref, hp_ref, lg_ref):
    half = og_ref.shape[2]
    sub = og_ref.shape[1] // INPROJ_SUB
    for t in range(INPROJ_SUB):
        rows = slice(t * sub, (t + 1) * sub)
        mix = _mm(og_ref[0, rows, :], wo_ref[:half, :]) + _mm(od_ref[0, rows, :], wo_ref[half:, :])
        x1 = x_ref[0, rows, :] + mod_ref[0, 2:3, :] * mix
        x1_ref[0, rows, :] = x1
        ms = jnp.mean(x1 * x1, axis=-1, keepdims=True)
        y = x1 * lax.rsqrt(ms + EPS) * g2_ref[...]
        h = (y * (1.0 + mod_ref[0, 4:5, :]) + mod_ref[0, 3:4, :]).astype(bf16)
        lg_ref[rows, :] = _mm(h, wr_ref[...]) + br_ref[...]
        _rows_to_tiles(h.astype(f32), hp_ref.at[pl.ds(t * sub * ROW_TILE, sub * ROW_TILE)])


def _outproj(og, od, x, mod, w_out, g_norm2, w_router, b_router):
    B, S, D = x.shape
    assert D == ROW_TILE * LANES, "the token-tile layout needs a model row to fill one (8,128) tile"
    E = w_router.shape[1]
    tm = TM_IN
    nj = S // tm
    w_r = jnp.zeros((D, LANES), f32).at[:, :E].set(w_router).astype(bf16)
    b_r = jnp.full((1, LANES), NEG, f32).at[0, :E].set(b_router)
    const = lambda shape: pl.BlockSpec(shape, lambda b, i: (0,) * len(shape))
    return pl.pallas_call(
        _outproj_kernel,
        out_shape=[jax.ShapeDtypeStruct((B, S, D), f32),
                   jax.ShapeDtypeStruct((B * S * ROW_TILE, LANES), f32),
                   jax.ShapeDtypeStruct((B * S, LANES), f32)],
        grid=(B, nj),
        in_specs=[pl.BlockSpec((1, tm, og.shape[2]), lambda b, i: (b, i, 0)),
                  pl.BlockSpec((1, tm, od.shape[2]), lambda b, i: (b, i, 0)),
                  pl.BlockSpec((1, tm, D), lambda b, i: (b, i, 0)),
                  pl.BlockSpec((1, 6, D), lambda b, i: (b, 0, 0)),
                  const((w_out.shape[0], D)), const((1, D)), const((D, LANES)), const((1, LANES))],
        out_specs=[pl.BlockSpec((1, tm, D), lambda b, i: (b, i, 0)),
                   pl.BlockSpec((tm * ROW_TILE, LANES), lambda b, i: (b * nj + i, 0)),
                   pl.BlockSpec((tm, LANES), lambda b, i: (b * nj + i, 0))],
        compiler_params=pltpu.CompilerParams(
            dimension_semantics=("arbitrary", "arbitrary"), vmem_limit_bytes=VMEM_LIMIT),
        name="outproj",
    )(og, od, x, mod, w_out.astype(bf16), g_norm2.reshape(1, D), w_r, b_r)


def _route_kernel(lg_ref, lt_ref, ri_ref, rw_ref, cnt_ref, run_ref):
    @pl.when(pl.program_id(0) == 0)
    def _():
        run_ref[...] = jnp.zeros_like(run_ref)

    x = lg_ref[...]
    tr = x.shape[0]
    lane = lax.broadcasted_iota(jnp.int32, (tr, LANES), 1)
    lane_f = lane.astype(f32)
    vals, hots, idxs = [], [], []
    for _ in range(TOP_K):
        m = jnp.max(x, axis=-1, keepdims=True)
        idx = jnp.min(jnp.where(x == m, lane_f, float(LANES)), axis=-1, keepdims=True)
        hot = lane_f == idx
        x = jnp.where(hot, -jnp.inf, x)
        vals.append(m)
        hots.append(hot)
        idxs.append(idx.astype(jnp.int32))
    ex = [jnp.exp(v - vals[0]) for v in vals]
    den = ex[0] + ex[1] + ex[2] + ex[3]
    sel = (hots[0] | hots[1] | hots[2] | hots[3]).astype(f32)
    rank = _mm(lt_ref[...], sel.astype(bf16)) + run_ref[...]
    run_ref[...] = run_ref[...] + jnp.sum(sel, axis=0, keepdims=True)
    cnt_ref[...] = run_ref[...]
    ri = jnp.zeros((tr, LANES), jnp.int32)
    rw = jnp.zeros((tr, LANES), f32)
    for k in range(TOP_K):
        rk = jnp.sum(jnp.where(hots[k], rank, 0.0), axis=-1, keepdims=True).astype(jnp.int32)
        ri = jnp.where(lane == k, rk, ri)
        ri = jnp.where(lane == TOP_K + k, idxs[k], ri)
        rw = jnp.where(lane == k, ex[k] / den, rw)
    ri_ref[...] = ri
    rw_ref[...] = rw


def _route(logits):
    T = logits.shape[0]
    tr = min(TR, T)
    r = jnp.arange(tr)
    lt = (r[None, :] < r[:, None]).astype(bf16)
    return pl.pallas_call(
        _route_kernel,
        out_shape=[jax.ShapeDtypeStruct((T, LANES), jnp.int32),
                   jax.ShapeDtypeStruct((T, LANES), f32),
                   jax.ShapeDtypeStruct((1, LANES), f32)],
        grid=(T // tr,),
        in_specs=[pl.BlockSpec((tr, LANES), lambda i: (i, 0)),
                  pl.BlockSpec((tr, tr), lambda i: (0, 0))],
        out_specs=[pl.BlockSpec((tr, LANES), lambda i: (i, 0)),
                   pl.BlockSpec((tr, LANES), lambda i: (i, 0)),
                   pl.BlockSpec((1, LANES), lambda i: (0, 0))],
        scratch_shapes=[pltpu.VMEM((1, LANES), f32)],
        compiler_params=pltpu.CompilerParams(dimension_semantics=("arbitrary",)),
        name="route",
    )(logits, lt)


def _tile(ref, t):
    return ref.at[pl.ds(pl.multiple_of(t * ROW_TILE, ROW_TILE), ROW_TILE)]


def _dispatch_kernel(pend_ref, cnt_ref, nu_ref, dest_ref, h_ref, xs_ref, zero_ref, sem, zsem):
    n_tok = h_ref.shape[0] // ROW_TILE
    blk_rows = FFN_BLK * ROW_TILE

    @pl.when(pl.program_id(0) == 0)
    def _():
        zero_ref[...] = jnp.zeros_like(zero_ref)
        n_exp = pend_ref.shape[0]

        def last_block(e):
            return xs_ref.at[pl.ds(pl.multiple_of((pend_ref[e] - FFN_BLK) * ROW_TILE, blk_rows), blk_rows)]

        def zfill(e, c):
            @pl.when(cnt_ref[e] > 0)
            def _():
                pltpu.make_async_copy(zero_ref, last_block(e), zsem).start()
            return c

        def zwait(e, c):
            @pl.when(cnt_ref[e] > 0)
            def _():
                pltpu.make_async_copy(zero_ref, last_block(e), zsem).wait()
            return c

        lax.fori_loop(0, n_exp, zfill, 0)
        lax.fori_loop(0, n_exp, zwait, 0)

        def tail_block(i):
            return xs_ref.at[pl.ds(pl.multiple_of(i * blk_rows, blk_rows), blk_rows)]

        def tfill(i, c):
            pltpu.make_async_copy(zero_ref, tail_block(i), zsem).start()
            return c

        def twait(i, c):
            pltpu.make_async_copy(zero_ref, tail_block(i), zsem).wait()
            return c

        n_blk = xs_ref.shape[0] // blk_rows
        lax.fori_loop(nu_ref[0], n_blk, tfill, 0)
        lax.fori_loop(nu_ref[0], n_blk, twait, 0)

    def issue(g, c):
        for u in range(DMA_UNROLL):
            r = g * DMA_UNROLL + u
            for k in range(TOP_K):
                pltpu.make_async_copy(_tile(h_ref, r), _tile(xs_ref, dest_ref[r * TOP_K + k]),
                                      sem).start(priority=k % 2)
        return c

    lax.fori_loop(0, n_tok // DMA_UNROLL, issue, 0)
    done = xs_ref.at[pl.ds(0, n_tok * TOP_K * ROW_TILE)]
    pltpu.make_async_copy(done, done, sem).wait()


def _dispatch(p_ends, counts, n_used, dest_flat, hp, n_rows):
    T = hp.shape[0] // ROW_TILE
    grid_spec = pltpu.PrefetchScalarGridSpec(
        num_scalar_prefetch=3,
        grid=(T // TD,),
        in_specs=[pl.BlockSpec((TD * TOP_K,), lambda i, pe, cn, nu: (i,), memory_space=pltpu.SMEM),
                  pl.BlockSpec((TD * ROW_TILE, LANES), lambda i, pe, cn, nu: (i, 0))],
        out_specs=pl.BlockSpec(memory_space=pl.ANY),
        scratch_shapes=[pltpu.VMEM((FFN_BLK * ROW_TILE, LANES), f32),
                        pltpu.SemaphoreType.DMA(()), pltpu.SemaphoreType.DMA(())],
    )
    return pl.pallas_call(
        _dispatch_kernel,
        out_shape=jax.ShapeDtypeStruct((n_rows * ROW_TILE, LANES), f32),
        grid_spec=grid_spec,
        compiler_params=pltpu.CompilerParams(dimension_semantics=("arbitrary",)),
        name="dispatch",
    )(p_ends, counts, n_used, dest_flat, hp)


def _ffn_kernel(be_ref, nu_ref, nx_ref, par_ref, xs_ref, wgu_hbm, bgu_ref, wd_hbm, bd_ref, ys_ref,
                wgu32_ref, wd32_ref, wgu_ref, wd_ref, sem):
    i = pl.program_id(0)
    used = i < nu_ref[0]
    new_expert = (i == 0) | (be_ref[i] != be_ref[jnp.maximum(i - 1, 0)])
    slot = par_ref[i]

    def weight_copies(e, s):
        return (pltpu.make_async_copy(wgu_hbm.at[e], wgu32_ref.at[s], sem.at[0, s]),
                pltpu.make_async_copy(wd_hbm.at[e], wd32_ref.at[s], sem.at[1, s]))

    @pl.when(i == 0)
    def _():
        for cp in weight_copies(be_ref[0], 0):
            cp.start()

    @pl.when(used & new_expert)
    def _():
        for cp in weight_copies(be_ref[i], slot):
            cp.wait()

        @pl.when(nx_ref[i] >= 0)
        def _():
            for cp in weight_copies(nx_ref[i], 1 - slot):
                cp.start()

        rows = 128

        def cast(src, dst):
            def body(r, c):
                r0 = pl.multiple_of(r * rows, rows)
                dst[pl.ds(r0, rows), :] = src[slot, pl.ds(r0, rows), :].astype(bf16)
                return c
            lax.fori_loop(0, src.shape[1] // rows, body, 0)
        cast(wgu32_ref, wgu_ref)
        cast(wd32_ref, wd_ref)

    @pl.when(used)
    def _():
        F = wd_ref.shape[0]
        xrow = _tiles_to_rows(xs_ref, FFN_BLK).astype(bf16)
        acc = None
        fc = F // 2
        for c in range(2):
            def gu(col0):
                return _mm(xrow, wgu_ref[:, col0:col0 + fc]) + bgu_ref[0, :, col0:col0 + fc]
            gate = jnp.minimum(gu(c * fc), SWIGLU_LIMIT)
            up = jnp.clip(gu(F + c * fc), -SWIGLU_LIMIT, SWIGLU_LIMIT)
            y = (up + 1.0) * (gate * jax.nn.sigmoid(SWIGLU_ALPHA * gate))
            part = _mm(y.astype(bf16), wd_ref[c * fc:(c + 1) * fc, :])
            acc = part if acc is None else acc + part
        _rows_to_tiles(acc + bd_ref[0], ys_ref)

    @pl.when(jnp.logical_not(used))
    def _():
        ys_ref[...] = jnp.zeros_like(ys_ref)


def _ffn(block_e, n_used, xs, w_gate_up, b_gate_up, w_down, b_down):
    E, D, F2 = w_gate_up.shape
    F = F2 // 2
    P = xs.shape[0] // ROW_TILE
    nb = P // FFN_BLK
    rows = FFN_BLK * ROW_TILE

    idx = jnp.arange(nb, dtype=jnp.int32)
    live = idx < n_used[0]
    later_other = (block_e[None, :] != block_e[:, None]) & (idx[None, :] > idx[:, None]) & live[None, :]
    nxt = jnp.where(jnp.any(later_other, axis=1), block_e[jnp.argmax(later_other, axis=1)], -1).astype(jnp.int32)
    starts = jnp.concatenate([jnp.ones((1,), jnp.int32), (block_e[1:] != block_e[:-1]).astype(jnp.int32)])
    parity = ((jnp.cumsum(starts) - 1) % 2).astype(jnp.int32)

    def blk(i, nu):
        return jnp.minimum(i, nu[0] - 1)

    grid_spec = pltpu.PrefetchScalarGridSpec(
        num_scalar_prefetch=4,
        grid=(nb,),
        in_specs=[pl.BlockSpec((rows, LANES), lambda i, be, nu, nx, pa: (blk(i, nu), 0)),
                  pl.BlockSpec(memory_space=pl.ANY),
                  pl.BlockSpec((1, 1, F2), lambda i, be, nu, nx, pa: (be[blk(i, nu)], 0, 0)),
                  pl.BlockSpec(memory_space=pl.ANY),
                  pl.BlockSpec((1, 1, D), lambda i, be, nu, nx, pa: (be[blk(i, nu)], 0, 0))],
        out_specs=pl.BlockSpec((rows, LANES), lambda i, be, nu, nx, pa: (i, 0)),
        scratch_shapes=[pltpu.VMEM((2, D, F2), f32), pltpu.VMEM((2, F, D), f32),
                        pltpu.VMEM((D, F2), bf16), pltpu.VMEM((F, D), bf16),
                        pltpu.SemaphoreType.DMA((2, 2))],
    )
    return pl.pallas_call(
        _ffn_kernel,
        out_shape=jax.ShapeDtypeStruct((P * ROW_TILE, LANES), f32),
        grid_spec=grid_spec,
        compiler_params=pltpu.CompilerParams(
            dimension_semantics=("arbitrary",), vmem_limit_bytes=VMEM_LIMIT_FFN),
        name="ffn",
    )(block_e, n_used, nxt, parity, xs, w_gate_up, b_gate_up.reshape(E, 1, F2), w_down, b_down.reshape(E, 1, D))


def _combine_kernel(dcur_ref, dnext_ref, ys_ref, x1_ref, rw_ref, mod_ref, o_ref, buf, acc_ref, wb_ref, sem):
    step = pl.program_id(0) * pl.num_programs(1) + pl.program_id(1)
    n_steps = pl.num_programs(0) * pl.num_programs(1)
    slot = step % 2

    def issue(dref, s):
        def body(g, c):
            for u in range(DMA_UNROLL):
                r = g * DMA_UNROLL + u
                for k in range(TOP_K):
                    pltpu.make_async_copy(_tile(ys_ref, dref[r * TOP_K + k]),
                                          _tile(buf.at[s, k], r), sem.at[s]).start(priority=k % 2)
            return c
        lax.fori_loop(0, TD // DMA_UNROLL, body, 0)

    @pl.when(step == 0)
    def _():
        issue(dcur_ref, 0)

    @pl.when(step + 1 < n_steps)
    def _():
        issue(dnext_ref, 1 - slot)

    pltpu.make_async_copy(buf.at[slot], buf.at[slot], sem.at[slot]).wait()

    rw = rw_ref[...]
    for k in range(TOP_K):
        wb_ref[k] = jnp.broadcast_to(rw[:, k:k + 1], (TD, LANES))

    def token(r, c):
        t0 = pl.multiple_of(r * ROW_TILE, ROW_TILE)
        acc = wb_ref[0, pl.ds(r, 1), :] * buf[slot, 0, pl.ds(t0, ROW_TILE), :]
        for k in range(1, TOP_K):
            acc = acc + wb_ref[k, pl.ds(r, 1), :] * buf[slot, k, pl.ds(t0, ROW_TILE), :]
        acc_ref[pl.ds(t0, ROW_TILE), :] = acc
        return c
    lax.fori_loop(0, TD, token, 0, unroll=DMA_UNROLL)
    o_ref[0] = x1_ref[0] + mod_ref[0, 5:6, :] * _tiles_to_rows(acc_ref, TD)


def _combine(dest_flat, ys, x1, rw, mod):
    B, S, D = x1.shape
    nj = S // TD
    n_steps = B * nj
    return pl.pallas_call(
        _combine_kernel,
        out_shape=jax.ShapeDtypeStruct((B, S, D), f32),
        grid=(B, nj),
        in_specs=[pl.BlockSpec((TD * TOP_K,), lambda b, j: (b * nj + j,), memory_space=pltpu.SMEM),
                  pl.BlockSpec((TD * TOP_K,), lambda b, j: (jnp.minimum(b * nj + j + 1, n_steps - 1),),
                               memory_space=pltpu.SMEM),
                  pl.BlockSpec(memory_space=pl.ANY),
                  pl.BlockSpec((1, TD, D), lambda b, j: (b, j, 0)),
                  pl.BlockSpec((TD, LANES), lambda b, j: (b * nj + j, 0)),
                  pl.BlockSpec((1, 6, D), lambda b, j: (b, 0, 0))],
        out_specs=pl.BlockSpec((1, TD, D), lambda b, j: (b, j, 0)),
        scratch_shapes=[pltpu.VMEM((2, TOP_K, TD * ROW_TILE, LANES), f32),
                        pltpu.VMEM((TD * ROW_TILE, LANES), f32), pltpu.VMEM((TOP_K, TD, LANES), f32),
                        pltpu.SemaphoreType.DMA((2,))],
        compiler_params=pltpu.CompilerParams(
            dimension_semantics=("arbitrary", "arbitrary"), vmem_limit_bytes=VMEM_LIMIT),
        name="combine",
    )(dest_flat, dest_flat, ys, x1, rw, mod)


def _moe(hp, logits, x1, mod, w_gate_up, b_gate_up, w_down, b_down):
    T = logits.shape[0]
    E = w_gate_up.shape[0]
    ri, rw, cnt = _route(logits)
    rank = ri[:, :TOP_K]
    e_sel = ri[:, TOP_K:2 * TOP_K]
    counts = cnt[0, :E].astype(jnp.int32)
    padded = ((counts + FFN_BLK - 1) // FFN_BLK) * FFN_BLK
    p_ends = jnp.cumsum(padded)
    p_starts = p_ends - padded
    nb = -(-T * TOP_K // FFN_BLK) + E
    n_used = jnp.maximum(p_ends[-1:] // FFN_BLK, 1).astype(jnp.int32)
    blk_start = jnp.arange(nb, dtype=jnp.int32) * FFN_BLK
    block_e = jnp.minimum(jnp.sum(p_ends[None, :] <= blk_start[:, None], axis=1), E - 1).astype(jnp.int32)
    onehot = e_sel[:, :, None] == jnp.arange(E, dtype=jnp.int32)[None, None, :]
    dest = (jnp.sum(jnp.where(onehot, p_starts[None, None, :], 0), axis=-1) + rank).reshape(-1)
    xs = _dispatch(p_ends.astype(jnp.int32), counts, n_used, dest, hp, nb * FFN_BLK)
    ys = _ffn(block_e, n_used, xs, w_gate_up, b_gate_up, w_down, b_down)
    return _combine(dest, ys, x1, rw, mod)


def kernel(x, c, rel_bias_table, w_ada, b_ada, g_norm1, w_in, w_gk_up, b_gk_up, g_gla_out, g_qnorm, g_knorm, lambda_q1, lambda_k1, lambda_q2, lambda_k2, g_subln, w_out, g_norm2, w_router, b_router, w_gate_up, b_gate_up, w_down, b_down):
    B, S, D = x.shape
    depth = w_ada.shape[0]
    bias_tiles = _bias_tiles(rel_bias_table, S, min(TQ, S))
    for l in range(depth):
        lambda_init = 0.8 - 0.6 * math.exp(-0.3 * l)
        mod = _ada(c, w_ada[l], b_ada[l])
        qg, kg, gk, kgt, gkt, vg, rg, qd, kd, vd = _inproj(
            x, mod, g_norm1[l], w_in[l], w_gk_up[l], b_gk_up[l], g_qnorm[l], g_knorm[l])
        og = _gla(qg, kg, gk, kgt, gkt, vg, rg, g_gla_out[l])
        lamv = jnp.stack([lambda_q1[l], lambda_k1[l], lambda_q2[l], lambda_k2[l]]).astype(f32)
        od = lax.cond(_scores_bounded(rel_bias_table, g_qnorm[l], g_knorm[l]),
                      functools.partial(_attn, lambda_init=lambda_init, bounded=True),
                      functools.partial(_attn, lambda_init=lambda_init, bounded=False),
                      qd, kd, vd, bias_tiles, lamv, g_subln[l])
        x1, hp, logits = _outproj(og, od, x, mod, w_out[l], g_norm2[l], w_router[l], b_router[l])
        x = _moe(hp, logits, x1, mod, w_gate_up[l], b_gate_up[l], w_down[l], b_down[l])
    return x
```

```python
import functools
import math

import jax
import jax.numpy as jnp
from jax import lax
from jax.experimental import pallas as pl
from jax.experimental.pallas import tpu as pltpu

f32 = jnp.float32
bf16 = jnp.bfloat16

N_GLA_HEADS = 4
GLA_DK = 64
GLA_DV = 128
GLA_GATE_RANK = 16
GLA_GATE_NORM = 16.0
GLA_CHUNK = 64
N_DIFF_HEADS = 4
DIFF_DQK = 64
DIFF_DV = 128
NUM_BUCKETS = 32
MAX_DISTANCE = 128
TOP_K = 4
SWIGLU_LIMIT = 7.0
SWIGLU_ALPHA = 1.702
EPS = 1e-6

GLA_QK_W = N_GLA_HEADS * GLA_DK
GLA_V_W = N_GLA_HEADS * GLA_DV
DIFF_QK_W = N_DIFF_HEADS * 2 * DIFF_DQK
DIFF_V_W = N_DIFF_HEADS * DIFF_DV

LANES = 128
NEG = -1e30
LOG2E = math.log2(math.e)
SAFE_SCORE = 40.0
NORM_SLACK = 1.02
VMEM_LIMIT = 48 * 1024 * 1024
VMEM_LIMIT_FFN = 58 * 1024 * 1024

TM_IN = 512
INPROJ_SUB = 2
TG_GLA = 1024
PAIR = 2 * GLA_CHUNK
GLA_UNROLL = 4
TQ = 512
ATTN_UNROLL = 4
TR = 512
TD = 256
ROW_TILE = 8
DMA_UNROLL = 8
N_RUN = 32
RUN_SIZES = tuple(TD >> b for b in range(TD.bit_length()))
LIST_DST = len(RUN_SIZES) * N_RUN
LIST_CNT = 2 * LIST_DST
LIST_LEN = 1024
FFN_BLK = 512


def _nt(a, b):
    return lax.dot_general(a, b, (((1,), (1,)), ((), ())), preferred_element_type=f32)


def _mm(a, b):
    return jnp.dot(a, b, preferred_element_type=f32)


def _split(x):
    hi = x.astype(bf16)
    lo = (x - hi.astype(f32)).astype(bf16)
    return hi, lo


def _silu(x):
    return x * jax.nn.sigmoid(x)


def _ada_kernel(c_ref, w_ref, b_ref, o_ref):
    c = c_ref[...]
    o_ref[...] = _mm(_silu(c).astype(bf16), w_ref[...].astype(bf16)) + b_ref[...]


def _ada(c, w_ada, b_ada):
    B, D = c.shape
    N = w_ada.shape[1]
    bp = ROW_TILE
    assert B <= bp
    cp = jnp.zeros((bp, D), f32).at[:B].set(c)
    tn = N // 4
    out = pl.pallas_call(
        _ada_kernel,
        out_shape=jax.ShapeDtypeStruct((bp, N), f32),
        grid=(N // tn,),
        in_specs=[pl.BlockSpec((bp, D), lambda j: (0, 0)),
                  pl.BlockSpec((D, tn), lambda j: (0, j)),
                  pl.BlockSpec((1, tn), lambda j: (0, j))],
        out_specs=pl.BlockSpec((bp, tn), lambda j: (0, j)),
        compiler_params=pltpu.CompilerParams(vmem_limit_bytes=VMEM_LIMIT),
        name="ada",
    )(cp, w_ada, b_ada.reshape(1, N))
    return out[:B].reshape(B, 6, D)


def _inproj_kernel(x_ref, mod_ref, g1_ref, wm_ref, wkt_ref, wlo_ref, wup_ref, wupt_ref,
                   bup_ref, bupt_ref, gqk_ref, grp_ref, grpt_ref,
                   qg_ref, kg_ref, gk_ref, kgt_ref, gkt_ref, vg_ref, rg_ref,
                   qd_ref, kd_ref, vd_ref):
    tm = x_ref.shape[1]
    sub = tm // INPROJ_SUB
    for t in range(INPROJ_SUB):
        _inproj_rows(slice(t * sub, (t + 1) * sub), x_ref, mod_ref, g1_ref, wm_ref, wkt_ref, wlo_ref, wup_ref,
                     wupt_ref, bup_ref, bupt_ref, gqk_ref, grp_ref, grpt_ref, qg_ref, kg_ref, gk_ref, kgt_ref,
                     gkt_ref, vg_ref, rg_ref, qd_ref, kd_ref, vd_ref)


def _inproj_rows(rows, x_ref, mod_ref, g1_ref, wm_ref, wkt_ref, wlo_ref, wup_ref, wupt_ref,
                 bup_ref, bupt_ref, gqk_ref, grp_ref, grpt_ref,
                 qg_ref, kg_ref, gk_ref, kgt_ref, gkt_ref, vg_ref, rg_ref, qd_ref, kd_ref, vd_ref):
    x = x_ref[0, rows, :]
    ms = jnp.mean(x * x, axis=-1, keepdims=True)
    y = x * lax.rsqrt(ms + EPS) * g1_ref[...]
    h = (y * (1.0 + mod_ref[0, 1:2, :]) + mod_ref[0, 0:1, :]).astype(bf16)

    def proj(a, b):
        return _mm(h, wm_ref[:, a:b])

    o = 0
    qg_ref[0, rows, :] = proj(o, o + GLA_QK_W); o += GLA_QK_W
    kg_ref[0, rows, :] = proj(o, o + GLA_QK_W); o += GLA_QK_W
    vg_ref[0, rows, :] = proj(o, o + GLA_V_W).astype(bf16); o += GLA_V_W
    rg_ref[0, rows, :] = proj(o, o + GLA_V_W); o += GLA_V_W
    qk = proj(o, o + 2 * DIFF_QK_W); o += 2 * DIFF_QK_W
    vd_ref[0, rows, :] = proj(o, o + DIFF_V_W).astype(bf16)

    slab0 = rows.start // PAIR
    kgt = _nt(wkt_ref[...], h)
    for j in range(kgt.shape[1] // PAIR):
        kgt_ref[0, slab0 + j] = kgt[:, j * PAIR:(j + 1) * PAIR]

    lo = _mm(h, wlo_ref[...]).astype(bf16)
    z = _mm(lo, wup_ref[...]) + bup_ref[...]
    gk_ref[0, rows, :] = (jnp.minimum(z, 0.0) - jnp.log1p(jnp.exp(-jnp.abs(z)))) * (1.0 / GLA_GATE_NORM)
    zt = _nt(wupt_ref[...], lo) + bupt_ref[...]
    gkt = (jnp.minimum(zt, 0.0) - jnp.log1p(jnp.exp(-jnp.abs(zt)))) * (1.0 / GLA_GATE_NORM)
    for j in range(gkt.shape[1] // PAIR):
        gkt_ref[0, slab0 + j] = gkt[:, j * PAIR:(j + 1) * PAIR]

    sq_hi, sq_lo = _split(qk * qk)
    gs = _mm(sq_hi, grp_ref[...]) + _mm(sq_lo, grp_ref[...])
    r = lax.rsqrt(gs * (1.0 / DIFF_DQK) + EPS)
    r_hi, r_lo = _split(r)
    rb = _mm(r_hi, grpt_ref[...]) + _mm(r_lo, grpt_ref[...])
    qkn = qk * rb * gqk_ref[...]
    qd_ref[0, rows, :] = qkn[:, :DIFF_QK_W].astype(bf16)
    kd_ref[0, rows, :] = qkn[:, DIFF_QK_W:].astype(bf16)


def _inproj(x, mod, g_norm1, w_in, w_gk_up, b_gk_up, g_qnorm, g_knorm):
    B, S, D = x.shape
    offs = [0]
    for w in (GLA_QK_W, GLA_QK_W, GLA_V_W, GLA_V_W, GLA_GATE_RANK, DIFF_QK_W, DIFF_QK_W, DIFF_V_W):
        offs.append(offs[-1] + w)
    w_main = jnp.concatenate([w_in[:, offs[0]:offs[4]], w_in[:, offs[5]:offs[8]]], axis=1).astype(bf16)
    w_kt = w_in[:, offs[1]:offs[2]].T.astype(bf16)
    w_lo = jnp.zeros((D, LANES), f32).at[:, :GLA_GATE_RANK].set(w_in[:, offs[4]:offs[5]]).astype(bf16)
    w_up = jnp.zeros((LANES, GLA_QK_W), f32).at[:GLA_GATE_RANK].set(w_gk_up).astype(bf16)
    w_upt = w_up.T
    b_up = b_gk_up.reshape(1, GLA_QK_W)
    b_upt = b_gk_up.reshape(GLA_QK_W, 1)
    n_grp = 2 * DIFF_QK_W // DIFF_DQK
    gqk = jnp.concatenate([jnp.tile(g_qnorm, n_grp // 2) * (DIFF_DQK ** -0.5 * LOG2E),
                           jnp.tile(g_knorm, n_grp // 2)]).reshape(1, 2 * DIFF_QK_W)
    grp = (jnp.arange(2 * DIFF_QK_W)[:, None] // DIFF_DQK == jnp.arange(LANES)[None, :]).astype(bf16)
    grpt = grp.T
    nw = w_main.shape[1]
    tm = TM_IN
    const = lambda shape: pl.BlockSpec(shape, lambda b, i: (0,) * len(shape))
    row = lambda w: pl.BlockSpec((1, tm, w), lambda b, i: (b, i, 0))
    colT = pl.BlockSpec((1, tm // PAIR, GLA_QK_W, PAIR), lambda b, i: (b, i, 0, 0))
    outs = pl.pallas_call(
        _inproj_kernel,
        out_shape=[jax.ShapeDtypeStruct((B, S, GLA_QK_W), f32),
                   jax.ShapeDtypeStruct((B, S, GLA_QK_W), f32),
                   jax.ShapeDtypeStruct((B, S, GLA_QK_W), f32),
                   jax.ShapeDtypeStruct((B, S // PAIR, GLA_QK_W, PAIR), f32),
                   jax.ShapeDtypeStruct((B, S // PAIR, GLA_QK_W, PAIR), f32),
                   jax.ShapeDtypeStruct((B, S, GLA_V_W), bf16),
                   jax.ShapeDtypeStruct((B, S, GLA_V_W), f32),
                   jax.ShapeDtypeStruct((B, S, DIFF_QK_W), bf16),
                   jax.ShapeDtypeStruct((B, S, DIFF_QK_W), bf16),
                   jax.ShapeDtypeStruct((B, S, DIFF_V_W), bf16)],
        grid=(B, S // tm),
        in_specs=[row(D),
                  pl.BlockSpec((1, 6, D), lambda b, i: (b, 0, 0)),
                  const((1, D)), const((D, nw)), const((GLA_QK_W, D)), const((D, LANES)),
                  const((LANES, GLA_QK_W)), const((GLA_QK_W, LANES)),
                  const((1, GLA_QK_W)), const((GLA_QK_W, 1)),
                  const((1, 2 * DIFF_QK_W)), const((2 * DIFF_QK_W, LANES)),
                  const((LANES, 2 * DIFF_QK_W))],
        out_specs=[row(GLA_QK_W), row(GLA_QK_W), row(GLA_QK_W), colT, colT,
                   row(GLA_V_W), row(GLA_V_W), row(DIFF_QK_W), row(DIFF_QK_W), row(DIFF_V_W)],
        compiler_params=pltpu.CompilerParams(
            dimension_semantics=("arbitrary", "arbitrary"), vmem_limit_bytes=VMEM_LIMIT),
        name="inproj",
    )(x, mod, g_norm1.reshape(1, D), w_main, w_kt, w_lo, w_up, w_upt, b_up, b_upt, gqk, grp, grpt)
    return outs


def _gla_kernel(q_ref, k_ref, g_ref, kt_ref, gt_ref, v_ref, r_ref, gout_ref, tri_ref, trit_ref,
                o_ref, s_ref, *, n_pairs):
    H, DK, DV = N_GLA_HEADS, GLA_DK, GLA_DV

    @pl.when(pl.program_id(1) == 0)
    def _():
        s_ref[...] = jnp.zeros_like(s_ref)

    tri = tri_ref[...]
    trit = trit_ref[...]
    tri_b = tri > 0
    lane_head = lax.broadcasted_iota(jnp.int32, (1, H * DK), 1) // DK
    row_head = lax.broadcasted_iota(jnp.int32, (H * PAIR, 1), 0) // PAIR
    qmask = row_head == lane_head
    row_first = lax.broadcasted_iota(jnp.int32, (PAIR, 1), 0) < GLA_CHUNK
    row_first4 = (lax.broadcasted_iota(jnp.int32, (H * PAIR, 1), 0) % PAIR) < GLA_CHUNK
    lane_first = lax.broadcasted_iota(jnp.int32, (1, PAIR), 1) < GLA_CHUNK
    scale = DK ** -0.5
    gout = gout_ref[...]

    def pair(p, carry):
        r0 = pl.multiple_of(p * PAIR, PAIR)
        q = q_ref[0, pl.ds(r0, PAIR), :]
        k = k_ref[0, pl.ds(r0, PAIR), :]
        g = g_ref[0, pl.ds(r0, PAIR), :]
        kt = kt_ref[0, p]
        gt = gt_ref[0, p]
        v = v_ref[0, pl.ds(r0, PAIR), :]

        g_hi, g_lo = _split(g)
        gc = _mm(tri, g_hi) + _mm(tri, g_lo)
        gt_hi, gt_lo = _split(gt)
        gct = _mm(gt_hi, trit) + _mm(gt_lo, trit)
        g_last = jnp.where(row_first, gc[GLA_CHUNK - 1:GLA_CHUNK, :], gc[PAIR - 1:PAIR, :])
        gl0 = gct[:, GLA_CHUNK - 1:GLA_CHUNK]
        gl1 = gct[:, PAIR - 1:PAIR]
        g_last_t = jnp.where(lane_first, gl0, gl1)

        q_e = (q * (jnp.exp(gc) * scale)).astype(bf16)
        k_e = (k * jnp.exp(-gc)).astype(bf16)
        ks_t = kt * jnp.exp(g_last_t - gct)
        ks_t0 = jnp.where(lane_first, ks_t, 0.0).astype(bf16)
        ks_t1 = jnp.where(lane_first, 0.0, ks_t).astype(bf16)
        del g_last

        qm = jnp.where(qmask, jnp.concatenate([q_e] * H, axis=0), jnp.zeros((), bf16))
        a = _nt(qm, k_e)
        s0 = s_ref[...]

        u0 = []
        u1 = []
        for h in range(H):
            v_h = v[:, h * DV:(h + 1) * DV]
            u0.append(_mm(ks_t0[h * DK:(h + 1) * DK], v_h))
            u1.append(_mm(ks_t1[h * DK:(h + 1) * DK], v_h))
        u0 = jnp.concatenate(u0, axis=0)
        u1 = jnp.concatenate(u1, axis=0)
        s1 = s0 * jnp.exp(gl0) + u0
        s_ref[...] = s1 * jnp.exp(gl1) + u1

        o_inter = jnp.where(row_first4, _mm(qm, s0.astype(bf16)), _mm(qm, s1.astype(bf16)))
        for h in range(H):
            a_h = jnp.where(tri_b, a[h * PAIR:(h + 1) * PAIR], 0.0).astype(bf16)
            o_h = _mm(a_h, v[:, h * DV:(h + 1) * DV]) + o_inter[h * PAIR:(h + 1) * PAIR]
            ms = jnp.mean(o_h * o_h, axis=-1, keepdims=True)
            o_n = o_h * lax.rsqrt(ms + EPS) * gout
            r_h = r_ref[0, pl.ds(r0, PAIR), h * DV:(h + 1) * DV]
            o_ref[0, pl.ds(r0, PAIR), h * DV:(h + 1) * DV] = (o_n * _silu(r_h)).astype(bf16)
        return carry

    lax.fori_loop(0, n_pairs, pair, 0, unroll=GLA_UNROLL)


def _gla(qg, kg, gk, kgt, gkt, vg, rg, g_gla_out):
    B, S, _ = qg.shape
    tg = min(TG_GLA, S)
    r = jnp.arange(PAIR)
    tri = ((r[:, None] // GLA_CHUNK == r[None, :] // GLA_CHUNK) & (r[None, :] <= r[:, None])).astype(bf16)
    row = lambda w: pl.BlockSpec((1, tg, w), lambda b, i: (b, i, 0))
    colT = pl.BlockSpec((1, tg // PAIR, GLA_QK_W, PAIR), lambda b, i: (b, i, 0, 0))
    const = lambda shape: pl.BlockSpec(shape, lambda b, i: (0,) * len(shape))
    return pl.pallas_call(
        functools.partial(_gla_kernel, n_pairs=tg // PAIR),
        out_shape=jax.ShapeDtypeStruct((B, S, GLA_V_W), bf16),
        grid=(B, S // tg),
        in_specs=[row(GLA_QK_W), row(GLA_QK_W), row(GLA_QK_W), colT, colT,
                  row(GLA_V_W), row(GLA_V_W), const((1, GLA_DV)),
                  const((PAIR, PAIR)), const((PAIR, PAIR))],
        out_specs=row(GLA_V_W),
        scratch_shapes=[pltpu.VMEM((GLA_QK_W, GLA_DV), f32)],
        compiler_params=pltpu.CompilerParams(
            dimension_semantics=("arbitrary", "arbitrary"), vmem_limit_bytes=VMEM_LIMIT),
        name="gla",
    )(qg, kg, gk, kgt, gkt, vg, rg, g_gla_out.reshape(1, GLA_DV), tri, tri.T)


def _attn_finish(o, gsub_ref, o_ref, lambda_init):
    ms = jnp.mean(o * o, axis=-1, keepdims=True)
    o_ref[0] = (o * lax.rsqrt(ms + EPS) * gsub_ref[...] * (1.0 - lambda_init)).astype(bf16)


def _attn_lambda(lamv_ref, lambda_init):
    lv = lamv_ref[...]
    return (jnp.exp(jnp.sum(lv[0:1] * lv[1:2], axis=-1, keepdims=True))
            - jnp.exp(jnp.sum(lv[2:3] * lv[3:4], axis=-1, keepdims=True)) + lambda_init)


def _attn_bounded_kernel(q_ref, k_ref, v_ref, bias_ref, lamv_ref, gsub_ref, o_ref, vaug_ref, *, lambda_init):
    qi = pl.program_id(2)
    tq = q_ref.shape[1]
    S = k_ref.shape[1]

    @pl.when(qi == 0)
    def _():
        lane = lax.broadcasted_iota(jnp.int32, (S, DIFF_DV), 1)
        vaug_ref[:, :DIFF_DV] = v_ref[0]
        vaug_ref[:, DIFF_DV:] = jnp.where(lane == 0, 1.0, 0.0).astype(bf16)

    q = q_ref[0]
    lane = lax.broadcasted_iota(jnp.int32, (1, 2 * DIFF_DQK), 1)
    zero = jnp.zeros((), bf16)
    qs = (jnp.where(lane < DIFF_DQK, q, zero), jnp.where(lane < DIFF_DQK, zero, q))

    def update(accs, k0, bias):
        kb = k_ref[0, pl.ds(k0, tq), :]
        vb = vaug_ref[pl.ds(k0, tq), :]
        out = []
        for c in range(2):
            s = _nt(qs[c], kb)
            if bias is not None:
                s = s + bias[c]
            out.append(accs[c] + _mm(jnp.exp2(s).astype(bf16), vb))
        return tuple(out)

    def far(kj, accs):
        return update(accs, pl.multiple_of(kj * tq, tq), None)

    def far_group(g, accs):
        for u in range(ATTN_UNROLL):
            accs = far(g * ATTN_UNROLL + u, accs)
        return accs

    def block_or_masked(accs, kj, bias):
        exists = kj >= 0
        k0 = pl.multiple_of(jnp.maximum(kj, 0) * tq, tq)
        if bias is None:
            tiles = (jnp.where(exists, 0.0, NEG),) * 2
        else:
            tiles = tuple(jnp.where(exists, b, NEG) for b in bias)
        return update(accs, k0, tiles)

    accs = (jnp.zeros((tq, 2 * DIFF_DV), f32), jnp.zeros((tq, 2 * DIFF_DV), f32))
    accs = update(accs, pl.multiple_of(qi * tq, tq), (bias_ref[0, 0, 1], bias_ref[0, 1, 1]))
    accs = block_or_masked(accs, qi - 1, (bias_ref[0, 0, 0], bias_ref[0, 1, 0]))
    for u in range(2, ATTN_UNROLL):
        accs = block_or_masked(accs, qi - u, None)
    n_far = jnp.maximum(qi + 1 - ATTN_UNROLL, 0)
    n_grp = n_far // ATTN_UNROLL
    accs = lax.fori_loop(0, n_grp, far_group, accs)
    accs = lax.fori_loop(n_grp * ATTN_UNROLL, n_far, far, accs)
    a0, a1 = accs
    o = (a0[:, :DIFF_DV] / a0[:, DIFF_DV:DIFF_DV + 1]
         - _attn_lambda(lamv_ref, lambda_init) * (a1[:, :DIFF_DV] / a1[:, DIFF_DV:DIFF_DV + 1]))
    _attn_finish(o, gsub_ref, o_ref, lambda_init)


def _attn_kernel(q_ref, k_ref, v_ref, bias_ref, lamv_ref, gsub_ref, o_ref, *, lambda_init):
    qi = pl.program_id(2)
    tq = q_ref.shape[1]
    q = q_ref[0]
    lane = lax.broadcasted_iota(jnp.int32, (1, 2 * DIFF_DQK), 1)
    zero = jnp.zeros((), bf16)
    qs = (jnp.where(lane < DIFF_DQK, q, zero), jnp.where(lane < DIFF_DQK, zero, q))

    def update(state, kb, vb, bias):
        new = []
        for c in range(2):
            m, l, acc = state[c]
            s = _nt(qs[c], kb)
            if bias is not None:
                s = s + bias[c]
            m_new = jnp.maximum(m, jnp.max(s, axis=-1, keepdims=True))
            alpha = jnp.exp2(m - m_new)
            p = jnp.exp2(s - m_new)
            l = alpha * l + jnp.sum(p, axis=-1, keepdims=True)
            acc = alpha * acc + _mm(p.astype(bf16), vb)
            new.append((m_new, l, acc))
        return tuple(new)

    init = tuple((jnp.full((tq, 1), NEG, f32), jnp.zeros((tq, 1), f32), jnp.zeros((tq, DIFF_DV), f32))
                 for _ in range(2))

    def far(kj, state):
        k0 = pl.multiple_of(kj * tq, tq)
        return update(state, k_ref[0, pl.ds(k0, tq), :], v_ref[0, pl.ds(k0, tq), :], None)

    state = lax.fori_loop(0, jnp.maximum(qi - 1, 0), far, init)

    kd0 = pl.multiple_of(qi * tq, tq)
    state = update(state, k_ref[0, pl.ds(kd0, tq), :], v_ref[0, pl.ds(kd0, tq), :],
                   (bias_ref[0, 0, 1], bias_ref[0, 1, 1]))
    kp0 = pl.multiple_of(jnp.maximum(qi - 1, 0) * tq, tq)
    has_prev = qi > 0
    state = update(state, k_ref[0, pl.ds(kp0, tq), :], v_ref[0, pl.ds(kp0, tq), :],
                   (jnp.where(has_prev, bias_ref[0, 0, 0], NEG), jnp.where(has_prev, bias_ref[0, 1, 0], NEG)))

    (_, l0, a0), (_, l1, a1) = state
    o = a0 / l0 - _attn_lambda(lamv_ref, lambda_init) * (a1 / l1)
    _attn_finish(o, gsub_ref, o_ref, lambda_init)


def _t5_bucket(n):
    max_exact = NUM_BUCKETS // 2
    nf = jnp.maximum(n, 1).astype(f32)
    large = max_exact + (jnp.log(nf / max_exact) / math.log(MAX_DISTANCE / max_exact)
                         * (NUM_BUCKETS - max_exact)).astype(jnp.int32)
    large = jnp.minimum(large, NUM_BUCKETS - 1)
    return jnp.where(n < max_exact, n, large)


def _toeplitz_kernel(w_ref, o_ref):
    n = o_ref.shape[-1]
    for t in range(2):
        rows = jnp.broadcast_to(w_ref[0, t:t + 1, :], (n, 2 * n))
        o_ref[0, 0, t] = pltpu.roll(rows, 0, 1, stride=1, stride_axis=0)[:, n:]


def _bias_tiles(rel_bias_table, S, n):
    HM = rel_bias_table.shape[1]
    assert n >= MAX_DISTANCE
    d = jnp.arange(2 * n, dtype=jnp.int32)
    by_dist = rel_bias_table[_t5_bucket(d)].astype(f32).T
    rel = (by_dist - rel_bias_table[NUM_BUCKETS - 1].astype(f32)[:, None]) * LOG2E
    i = jnp.arange(2 * n)
    w_diag = jnp.where(i[None, :] <= n, rel[:, jnp.clip(n - i, 0, 2 * n - 1)], NEG)
    w_prev = rel[:, jnp.clip(2 * n - i, 0, 2 * n - 1)]
    w = jnp.stack([w_prev, w_diag], axis=1)
    return pl.pallas_call(
        _toeplitz_kernel,
        out_shape=jax.ShapeDtypeStruct((HM // 2, 2, 2, n, n), f32),
        grid=(HM // 2, 2),
        in_specs=[pl.BlockSpec((1, 2, 2 * n), lambda h, m: (h * 2 + m, 0, 0))],
        out_specs=pl.BlockSpec((1, 1, 2, n, n), lambda h, m: (h, m, 0, 0, 0)),
        compiler_params=pltpu.CompilerParams(vmem_limit_bytes=VMEM_LIMIT),
        name="bias_tiles",
    )(w)


def _attn(qd, kd, vd, bias_tiles, lamv, g_subln, lambda_init, bounded):
    B, S, _ = qd.shape
    H = N_DIFF_HEADS
    tq = min(TQ, S)
    body = _attn_bounded_kernel if bounded else _attn_kernel
    scratch = [pltpu.VMEM((S, 2 * DIFF_DV), bf16)] if bounded else []
    return pl.pallas_call(
        functools.partial(body, lambda_init=lambda_init),
        out_shape=jax.ShapeDtypeStruct((B, S, DIFF_V_W), bf16),
        scratch_shapes=scratch,
        grid=(B, H, S // tq),
        in_specs=[pl.BlockSpec((1, tq, 2 * DIFF_DQK), lambda b, h, i: (b, i, h)),
                  pl.BlockSpec((1, S, 2 * DIFF_DQK), lambda b, h, i: (b, 0, h)),
                  pl.BlockSpec((1, S, DIFF_DV), lambda b, h, i: (b, 0, h)),
                  pl.BlockSpec((1, 2, 2, tq, tq), lambda b, h, i: (h, 0, 0, 0, 0)),
                  pl.BlockSpec((4, DIFF_DQK), lambda b, h, i: (0, 0)),
                  pl.BlockSpec((1, DIFF_DV), lambda b, h, i: (0, 0))],
        out_specs=pl.BlockSpec((1, tq, DIFF_DV), lambda b, h, i: (b, i, h)),
        compiler_params=pltpu.CompilerParams(
            dimension_semantics=("arbitrary", "arbitrary", "arbitrary"), vmem_limit_bytes=VMEM_LIMIT),
        name="attn_bounded" if bounded else "attn_online",
    )(qd, kd, vd, bias_tiles, lamv, g_subln.reshape(1, DIFF_DV))


def _scores_bounded(rel_bias_table, g_qnorm, g_knorm):
    qk = DIFF_DQK ** 0.5 * jnp.max(jnp.abs(g_qnorm)) * jnp.max(jnp.abs(g_knorm)) * NORM_SLACK
    rel = jnp.max(jnp.abs(rel_bias_table - rel_bias_table[NUM_BUCKETS - 1:]))
    return qk + rel <= SAFE_SCORE


def _rows_to_tiles(x, ref):
    n = x.shape[0]
    for c in range(ROW_TILE):
        ref[pl.ds(c, n, stride=ROW_TILE), :] = x[:, c * LANES:(c + 1) * LANES]


def _tiles_to_rows(ref, n):
    return jnp.concatenate([ref[pl.ds(c, n, stride=ROW_TILE), :] for c in range(ROW_TILE)], axis=1)


def _outproj_kernel(og_ref, od_ref, x_ref, mod_ref, wo_ref, g2_ref, wr_ref, br_ref,
                    x1_ref, hp_ref, lg_ref):
    half = og_ref.shape[2]
    sub = og_ref.shape[1] // INPROJ_SUB
    for t in range(INPROJ_SUB):
        rows = slice(t * sub, (t + 1) * sub)
        mix = _mm(og_ref[0, rows, :], wo_ref[:half, :]) + _mm(od_ref[0, rows, :], wo_ref[half:, :])
        x1 = x_ref[0, rows, :] + mod_ref[0, 2:3, :] * mix
        x1_ref[0, rows, :] = x1
        ms = jnp.mean(x1 * x1, axis=-1, keepdims=True)
        y = x1 * lax.rsqrt(ms + EPS) * g2_ref[...]
        h = (y * (1.0 + mod_ref[0, 4:5, :]) + mod_ref[0, 3:4, :]).astype(bf16)
        lg_ref[rows, :] = _mm(h, wr_ref[...]) + br_ref[...]
        _rows_to_tiles(h.astype(f32), hp_ref.at[pl.ds(t * sub * ROW_TILE, sub * ROW_TILE)])


def _outproj(og, od, x, mod, w_out, g_norm2, w_router, b_router):
    B, S, D = x.shape
    assert D == ROW_TILE * LANES, "the token-tile layout needs a model row to fill one (8,128) tile"
    E = w_router.shape[1]
    tm = TM_IN
    nj = S // tm
    w_r = jnp.zeros((D, LANES), f32).at[:, :E].set(w_router).astype(bf16)
    b_r = jnp.full((1, LANES), NEG, f32).at[0, :E].set(b_router)
    const = lambda shape: pl.BlockSpec(shape, lambda b, i: (0,) * len(shape))
    return pl.pallas_call(
        _outproj_kernel,
        out_shape=[jax.ShapeDtypeStruct((B, S, D), f32),
                   jax.ShapeDtypeStruct((B * S * ROW_TILE, LANES), f32),
                   jax.ShapeDtypeStruct((B * S, LANES), f32)],
        grid=(B, nj),
        in_specs=[pl.BlockSpec((1, tm, og.shape[2]), lambda b, i: (b, i, 0)),
                  pl.BlockSpec((1, tm, od.shape[2]), lambda b, i: (b, i, 0)),
                  pl.BlockSpec((1, tm, D), lambda b, i: (b, i, 0)),
                  pl.BlockSpec((1, 6, D), lambda b, i: (b, 0, 0)),
                  const((w_out.shape[0], D)), const((1, D)), const((D, LANES)), const((1, LANES))],
        out_specs=[pl.BlockSpec((1, tm, D), lambda b, i: (b, i, 0)),
                   pl.BlockSpec((tm * ROW_TILE, LANES), lambda b, i: (b * nj + i, 0)),
                   pl.BlockSpec((tm, LANES), lambda b, i: (b * nj + i, 0))],
        compiler_params=pltpu.CompilerParams(
            dimension_semantics=("arbitrary", "arbitrary"), vmem_limit_bytes=VMEM_LIMIT),
        name="outproj",
    )(og, od, x, mod, w_out.astype(bf16), g_norm2.reshape(1, D), w_r, b_r)


def _route_kernel(lg_ref, lt_ref, ri_ref, rw_ref, cnt_ref, snap_ref, run_ref):
    @pl.when(pl.program_id(0) == 0)
    def _():
        run_ref[...] = jnp.zeros_like(run_ref)

    x = lg_ref[...]
    tr = x.shape[0]
    lane = lax.broadcasted_iota(jnp.int32, (tr, LANES), 1)
    lane_f = lane.astype(f32)
    vals, hots, idxs = [], [], []
    for _ in range(TOP_K):
        m = jnp.max(x, axis=-1, keepdims=True)
        idx = jnp.min(jnp.where(x == m, lane_f, float(LANES)), axis=-1, keepdims=True)
        hot = lane_f == idx
        x = jnp.where(hot, -jnp.inf, x)
        vals.append(m)
        hots.append(hot)
        idxs.append(idx.astype(jnp.int32))
    ex = [jnp.exp(v - vals[0]) for v in vals]
    den = ex[0] + ex[1] + ex[2] + ex[3]
    sel = (hots[0] | hots[1] | hots[2] | hots[3]).astype(f32)
    rank = _mm(lt_ref[...], sel.astype(bf16)) + run_ref[...]
    run_ref[...] = run_ref[...] + jnp.sum(sel, axis=0, keepdims=True)
    cnt_ref[...] = run_ref[...]
    for t in range(tr // TD):
        snap_ref[0, t:t + 1, :] = rank[t * TD:t * TD + 1, :]
    ri = jnp.zeros((tr, LANES), jnp.int32)
    rw = jnp.zeros((tr, LANES), f32)
    for k in range(TOP_K):
        rk = jnp.sum(jnp.where(hots[k], rank, 0.0), axis=-1, keepdims=True).astype(jnp.int32)
        ri = jnp.where(lane == k, rk, ri)
        ri = jnp.where(lane == TOP_K + k, idxs[k], ri)
        rw = jnp.where(lane == k, ex[k] / den, rw)
    ri_ref[...] = ri
    rw_ref[...] = rw


def _route(logits):
    T = logits.shape[0]
    tr = min(TR, T)
    r = jnp.arange(tr)
    lt = (r[None, :] < r[:, None]).astype(bf16)
    return pl.pallas_call(
        _route_kernel,
        out_shape=[jax.ShapeDtypeStruct((T, LANES), jnp.int32),
                   jax.ShapeDtypeStruct((T, LANES), f32),
                   jax.ShapeDtypeStruct((1, LANES), f32),
                   jax.ShapeDtypeStruct((T // tr, tr // TD, LANES), f32)],
        grid=(T // tr,),
        in_specs=[pl.BlockSpec((tr, LANES), lambda i: (i, 0)),
                  pl.BlockSpec((tr, tr), lambda i: (0, 0))],
        out_specs=[pl.BlockSpec((tr, LANES), lambda i: (i, 0)),
                   pl.BlockSpec((tr, LANES), lambda i: (i, 0)),
                   pl.BlockSpec((1, LANES), lambda i: (0, 0)),
                   pl.BlockSpec((1, tr // TD, LANES), lambda i: (i, 0, 0))],
        scratch_shapes=[pltpu.VMEM((1, LANES), f32)],
        compiler_params=pltpu.CompilerParams(dimension_semantics=("arbitrary",)),
        name="route",
    )(logits, lt)


def _run_copies(list_ref, hbm_ref, stage_ref, sem, to_hbm):
    for c, size in enumerate(RUN_SIZES):
        def one(i, carry, c=c, size=size):
            s0 = list_ref[c * N_RUN + i]
            d0 = list_ref[LIST_DST + c * N_RUN + i]
            stage = stage_ref.at[pl.ds(pl.multiple_of(s0, ROW_TILE), size * ROW_TILE)]
            rows = hbm_ref.at[pl.ds(pl.multiple_of(d0, ROW_TILE), size * ROW_TILE)]
            src, dst = (stage, rows) if to_hbm else (rows, stage)
            pltpu.make_async_copy(src, dst, sem).start(priority=c % 2)
            return carry
        lax.fori_loop(0, list_ref[LIST_CNT + c], one, 0)


def _copy_lists(run_dst, run_n, run_off):
    n_tiles = run_n.shape[0]
    size = jnp.asarray(RUN_SIZES, jnp.int32)[None, :, None]
    n = run_n[:, None, :]
    bit = (n & size) != 0
    before = n & ~(2 * size - 1)
    order = jnp.argsort(jnp.logical_not(bit), axis=-1, stable=True)
    src = jnp.take_along_axis(run_off[:, None, :] + before, order, axis=-1) * ROW_TILE
    dst = jnp.take_along_axis(run_dst[:, None, :] + before, order, axis=-1) * ROW_TILE
    cnt = jnp.sum(bit, axis=-1).astype(jnp.int32)
    pad = jnp.zeros((n_tiles, LIST_LEN - LIST_CNT - len(RUN_SIZES)), jnp.int32)
    lists = jnp.concatenate([src.reshape(n_tiles, -1), dst.reshape(n_tiles, -1), cnt, pad], axis=1)
    return lists.reshape(-1).astype(jnp.int32)


def _dispatch_kernel(pend_ref, cnt_ref, nu_ref, lpos_ref, list_ref, h_ref, xs_ref,
                     zero_ref, stage_ref, sem, zsem):
    n_tok = h_ref.shape[0] // ROW_TILE
    blk_rows = FFN_BLK * ROW_TILE

    @pl.when(pl.program_id(0) == 0)
    def _():
        zero_ref[...] = jnp.zeros_like(zero_ref)
        n_exp = pend_ref.shape[0]

        def last_block(e):
            return xs_ref.at[pl.ds(pl.multiple_of((pend_ref[e] - FFN_BLK) * ROW_TILE, blk_rows), blk_rows)]

        def zfill(e, c):
            @pl.when(cnt_ref[e] > 0)
            def _():
                pltpu.make_async_copy(zero_ref, last_block(e), zsem).start()
            return c

        def zwait(e, c):
            @pl.when(cnt_ref[e] > 0)
            def _():
                pltpu.make_async_copy(zero_ref, last_block(e), zsem).wait()
            return c

        lax.fori_loop(0, n_exp, zfill, 0)
        lax.fori_loop(0, n_exp, zwait, 0)

        def tail_block(i):
            return xs_ref.at[pl.ds(pl.multiple_of(i * blk_rows, blk_rows), blk_rows)]

        def tfill(i, c):
            pltpu.make_async_copy(zero_ref, tail_block(i), zsem).start()
            return c

        def twait(i, c):
            pltpu.make_async_copy(zero_ref, tail_block(i), zsem).wait()
            return c

        n_blk = xs_ref.shape[0] // blk_rows
        lax.fori_loop(nu_ref[0], n_blk, tfill, 0)
        lax.fori_loop(nu_ref[0], n_blk, twait, 0)

    def place(g, c):
        for u in range(DMA_UNROLL):
            r = g * DMA_UNROLL + u
            row = h_ref[pl.ds(pl.multiple_of(r * ROW_TILE, ROW_TILE), ROW_TILE), :]
            for k in range(TOP_K):
                p = lpos_ref[r * TOP_K + k]
                stage_ref[pl.ds(pl.multiple_of(p, ROW_TILE), ROW_TILE), :] = row
        return c
    lax.fori_loop(0, n_tok // DMA_UNROLL, place, 0)

    _run_copies(list_ref, xs_ref, stage_ref, sem, to_hbm=True)
    pltpu.make_async_copy(stage_ref, stage_ref, sem).wait()


def _dispatch(p_ends, counts, n_used, lpos_flat, lists, hp, n_rows):
    T = hp.shape[0] // ROW_TILE
    grid_spec = pltpu.PrefetchScalarGridSpec(
        num_scalar_prefetch=3,
        grid=(T // TD,),
        in_specs=[pl.BlockSpec((TD * TOP_K,), lambda i, *_: (i,), memory_space=pltpu.SMEM),
                  pl.BlockSpec((LIST_LEN,), lambda i, *_: (i,), memory_space=pltpu.SMEM),
                  pl.BlockSpec((TD * ROW_TILE, LANES), lambda i, *_: (i, 0))],
        out_specs=pl.BlockSpec(memory_space=pl.ANY),
        scratch_shapes=[pltpu.VMEM((FFN_BLK * ROW_TILE, LANES), f32),
                        pltpu.VMEM((TD * TOP_K * ROW_TILE, LANES), f32),
                        pltpu.SemaphoreType.DMA(()), pltpu.SemaphoreType.DMA(())],
    )
    return pl.pallas_call(
        _dispatch_kernel,
        out_shape=jax.ShapeDtypeStruct((n_rows * ROW_TILE, LANES), f32),
        grid_spec=grid_spec,
        compiler_params=pltpu.CompilerParams(dimension_semantics=("arbitrary",), vmem_limit_bytes=VMEM_LIMIT),
        name="dispatch",
    )(p_ends, counts, n_used, lpos_flat, lists, hp)


def _ffn_kernel(be_ref, nu_ref, nx_ref, par_ref, xs_ref, wgu_hbm, bgu_ref, wd_hbm, bd_ref, ys_ref,
                wgu32_ref, wd32_ref, wgu_ref, wd_ref, sem):
    i = pl.program_id(0)
    used = i < nu_ref[0]
    new_expert = (i == 0) | (be_ref[i] != be_ref[jnp.maximum(i - 1, 0)])
    slot = par_ref[i]

    def weight_copies(e, s):
        return (pltpu.make_async_copy(wgu_hbm.at[e], wgu32_ref.at[s], sem.at[0, s]),
                pltpu.make_async_copy(wd_hbm.at[e], wd32_ref.at[s], sem.at[1, s]))

    @pl.when(i == 0)
    def _():
        for cp in weight_copies(be_ref[0], 0):
            cp.start()

    @pl.when(used & new_expert)
    def _():
        for cp in weight_copies(be_ref[i], slot):
            cp.wait()

        @pl.when(nx_ref[i] >= 0)
        def _():
            for cp in weight_copies(nx_ref[i], 1 - slot):
                cp.start()

        rows = 128

        def cast(src, dst):
            def body(r, c):
                r0 = pl.multiple_of(r * rows, rows)
                dst[pl.ds(r0, rows), :] = src[slot, pl.ds(r0, rows), :].astype(bf16)
                return c
            lax.fori_loop(0, src.shape[1] // rows, body, 0)
        cast(wgu32_ref, wgu_ref)
        cast(wd32_ref, wd_ref)

    @pl.when(used)
    def _():
        F = wd_ref.shape[0]
        xrow = _tiles_to_rows(xs_ref, FFN_BLK).astype(bf16)
        acc = None
        fc = F // 2
        for c in range(2):
            def gu(col0):
                return _mm(xrow, wgu_ref[:, col0:col0 + fc]) + bgu_ref[0, :, col0:col0 + fc]
            gate = jnp.minimum(gu(c * fc), SWIGLU_LIMIT)
            up = jnp.clip(gu(F + c * fc), -SWIGLU_LIMIT, SWIGLU_LIMIT)
            y = (up + 1.0) * (gate * jax.nn.sigmoid(SWIGLU_ALPHA * gate))
            part = _mm(y.astype(bf16), wd_ref[c * fc:(c + 1) * fc, :])
            acc = part if acc is None else acc + part
        _rows_to_tiles(acc + bd_ref[0], ys_ref)

    @pl.when(jnp.logical_not(used))
    def _():
        ys_ref[...] = jnp.zeros_like(ys_ref)


def _ffn(block_e, n_used, xs, w_gate_up, b_gate_up, w_down, b_down):
    E, D, F2 = w_gate_up.shape
    F = F2 // 2
    P = xs.shape[0] // ROW_TILE
    nb = P // FFN_BLK
    rows = FFN_BLK * ROW_TILE

    idx = jnp.arange(nb, dtype=jnp.int32)
    live = idx < n_used[0]
    later_other = (block_e[None, :] != block_e[:, None]) & (idx[None, :] > idx[:, None]) & live[None, :]
    nxt = jnp.where(jnp.any(later_other, axis=1), block_e[jnp.argmax(later_other, axis=1)], -1).astype(jnp.int32)
    starts = jnp.concatenate([jnp.ones((1,), jnp.int32), (block_e[1:] != block_e[:-1]).astype(jnp.int32)])
    parity = ((jnp.cumsum(starts) - 1) % 2).astype(jnp.int32)

    def blk(i, nu):
        return jnp.minimum(i, nu[0] - 1)

    grid_spec = pltpu.PrefetchScalarGridSpec(
        num_scalar_prefetch=4,
        grid=(nb,),
        in_specs=[pl.BlockSpec((rows, LANES), lambda i, be, nu, nx, pa: (blk(i, nu), 0)),
                  pl.BlockSpec(memory_space=pl.ANY),
                  pl.BlockSpec((1, 1, F2), lambda i, be, nu, nx, pa: (be[blk(i, nu)], 0, 0)),
                  pl.BlockSpec(memory_space=pl.ANY),
                  pl.BlockSpec((1, 1, D), lambda i, be, nu, nx, pa: (be[blk(i, nu)], 0, 0))],
        out_specs=pl.BlockSpec((rows, LANES), lambda i, be, nu, nx, pa: (i, 0)),
        scratch_shapes=[pltpu.VMEM((2, D, F2), f32), pltpu.VMEM((2, F, D), f32),
                        pltpu.VMEM((D, F2), bf16), pltpu.VMEM((F, D), bf16),
                        pltpu.SemaphoreType.DMA((2, 2))],
    )
    return pl.pallas_call(
        _ffn_kernel,
        out_shape=jax.ShapeDtypeStruct((P * ROW_TILE, LANES), f32),
        grid_spec=grid_spec,
        compiler_params=pltpu.CompilerParams(
            dimension_semantics=("arbitrary",), vmem_limit_bytes=VMEM_LIMIT_FFN),
        name="ffn",
    )(block_e, n_used, nxt, parity, xs, w_gate_up, b_gate_up.reshape(E, 1, F2), w_down, b_down.reshape(E, 1, D))


def _combine_kernel(lpos_ref, lcur_ref, lnext_ref, ys_ref, x1_ref, rw_ref, mod_ref, o_ref,
                    stage_ref, acc_ref, wb_ref, sem):
    step = pl.program_id(0) * pl.num_programs(1) + pl.program_id(1)
    n_steps = pl.num_programs(0) * pl.num_programs(1)
    slot = step % 2

    def fetch(list_ref, s):
        _run_copies(list_ref, ys_ref, stage_ref.at[s], sem.at[s], to_hbm=False)

    @pl.when(step == 0)
    def _():
        fetch(lcur_ref, 0)

    @pl.when(step + 1 < n_steps)
    def _():
        fetch(lnext_ref, 1 - slot)

    pltpu.make_async_copy(stage_ref.at[slot], stage_ref.at[slot], sem.at[slot]).wait()

    rw = rw_ref[...]
    for k in range(TOP_K):
        wb_ref[k] = jnp.broadcast_to(rw[:, k:k + 1], (TD, LANES))

    def staged(r, k):
        p = lpos_ref[r * TOP_K + k]
        return stage_ref[slot, pl.ds(pl.multiple_of(p, ROW_TILE), ROW_TILE), :]

    def token(r, c):
        acc = wb_ref[0, pl.ds(r, 1), :] * staged(r, 0)
        for k in range(1, TOP_K):
            acc = acc + wb_ref[k, pl.ds(r, 1), :] * staged(r, k)
        acc_ref[pl.ds(pl.multiple_of(r * ROW_TILE, ROW_TILE), ROW_TILE), :] = acc
        return c
    lax.fori_loop(0, TD, token, 0, unroll=DMA_UNROLL)
    o_ref[0] = x1_ref[0] + mod_ref[0, 5:6, :] * _tiles_to_rows(acc_ref, TD)


def _combine(lpos_flat, lists, ys, x1, rw, mod):
    B, S, D = x1.shape
    nj = S // TD
    n_steps = B * nj
    return pl.pallas_call(
        _combine_kernel,
        out_shape=jax.ShapeDtypeStruct((B, S, D), f32),
        grid=(B, nj),
        in_specs=[pl.BlockSpec((TD * TOP_K,), lambda b, j: (b * nj + j,), memory_space=pltpu.SMEM),
                  pl.BlockSpec((LIST_LEN,), lambda b, j: (b * nj + j,), memory_space=pltpu.SMEM),
                  pl.BlockSpec((LIST_LEN,), lambda b, j: (jnp.minimum(b * nj + j + 1, n_steps - 1),),
                               memory_space=pltpu.SMEM),
                  pl.BlockSpec(memory_space=pl.ANY),
                  pl.BlockSpec((1, TD, D), lambda b, j: (b, j, 0)),
                  pl.BlockSpec((TD, LANES), lambda b, j: (b * nj + j, 0)),
                  pl.BlockSpec((1, 6, D), lambda b, j: (b, 0, 0))],
        out_specs=pl.BlockSpec((1, TD, D), lambda b, j: (b, j, 0)),
        scratch_shapes=[pltpu.VMEM((2, TD * TOP_K * ROW_TILE, LANES), f32),
                        pltpu.VMEM((TD * ROW_TILE, LANES), f32), pltpu.VMEM((TOP_K, TD, LANES), f32),
                        pltpu.SemaphoreType.DMA((2,))],
        compiler_params=pltpu.CompilerParams(
            dimension_semantics=("arbitrary", "arbitrary"), vmem_limit_bytes=VMEM_LIMIT),
        name="combine",
    )(lpos_flat, lists, lists, ys, x1, rw, mod)


def _moe(hp, logits, x1, mod, w_gate_up, b_gate_up, w_down, b_down):
    T = logits.shape[0]
    E = w_gate_up.shape[0]
    ri, rw, cnt, snap = _route(logits)
    rank = ri[:, :TOP_K]
    e_sel = ri[:, TOP_K:2 * TOP_K]
    counts = cnt[0, :E].astype(jnp.int32)
    padded = ((counts + FFN_BLK - 1) // FFN_BLK) * FFN_BLK
    p_ends = jnp.cumsum(padded)
    p_starts = p_ends - padded
    nb = -(-T * TOP_K // FFN_BLK) + E
    n_used = jnp.maximum(p_ends[-1:] // FFN_BLK, 1).astype(jnp.int32)
    blk_start = jnp.arange(nb, dtype=jnp.int32) * FFN_BLK
    block_e = jnp.minimum(jnp.sum(p_ends[None, :] <= blk_start[:, None], axis=1), E - 1).astype(jnp.int32)
    assert E == N_RUN
    base = snap.reshape(T // TD, LANES)[:, :E].astype(jnp.int32)
    run_n = jnp.concatenate([base[1:], counts[None, :]], axis=0) - base
    run_off = jnp.cumsum(run_n, axis=1) - run_n
    run_dst = p_starts[None, :].astype(jnp.int32) + base
    shift = jnp.repeat(run_off - base, TD, axis=0)
    onehot = e_sel[:, :, None] == jnp.arange(E, dtype=jnp.int32)[None, None, :]
    lpos = ((jnp.sum(jnp.where(onehot, shift[:, None, :], 0), axis=-1) + rank) * ROW_TILE).reshape(-1)
    lists = _copy_lists(run_dst, run_n, run_off)
    xs = _dispatch(p_ends.astype(jnp.int32), counts, n_used, lpos, lists, hp, nb * FFN_BLK)
    ys = _ffn(block_e, n_used, xs, w_gate_up, b_gate_up, w_down, b_down)
    return _combine(lpos, lists, ys, x1, rw, mod)


def kernel(x, c, rel_bias_table, w_ada, b_ada, g_norm1, w_in, w_gk_up, b_gk_up, g_gla_out, g_qnorm, g_knorm, lambda_q1, lambda_k1, lambda_q2, lambda_k2, g_subln, w_out, g_norm2, w_router, b_router, w_gate_up, b_gate_up, w_down, b_down):
    B, S, D = x.shape
    depth = w_ada.shape[0]
    bias_tiles = _bias_tiles(rel_bias_table, S, min(TQ, S))
    for l in range(depth):
        lambda_init = 0.8 - 0.6 * math.exp(-0.3 * l)
        mod = _ada(c, w_ada[l], b_ada[l])
        qg, kg, gk, kgt, gkt, vg, rg, qd, kd, vd = _inproj(
            x, mod, g_norm1[l], w_in[l], w_gk_up[l], b_gk_up[l], g_qnorm[l], g_knorm[l])
        og = _gla(qg, kg, gk, kgt, gkt, vg, rg, g_gla_out[l])
        lamv = jnp.stack([lambda_q1[l], lambda_k1[l], lambda_q2[l], lambda_k2[l]]).astype(f32)
        od = lax.cond(_scores_bounded(rel_bias_table, g_qnorm[l], g_knorm[l]),
                      functools.partial(_attn, lambda_init=lambda_init, bounded=True),
                      functools.partial(_attn, lambda_init=lambda_init, bounded=False),
                      qd, kd, vd, bias_tiles, lamv, g_subln[l])
        x1, hp, logits = _outproj(og, od, x, mod, w_out[l], g_norm2[l], w_router[l], b_router[l])
        x = _moe(hp, logits, x1, mod, w_gate_up[l], b_gate_up[l], w_down[l], b_down[l])
    return x
```

```python
import functools
import math

import jax
import jax.numpy as jnp
from jax import lax
from jax.experimental import pallas as pl
from jax.experimental.pallas import tpu as pltpu

f32 = jnp.float32
bf16 = jnp.bfloat16

N_GLA_HEADS = 4
GLA_DK = 64
GLA_DV = 128
GLA_GATE_RANK = 16
GLA_GATE_NORM = 16.0
GLA_CHUNK = 64
N_DIFF_HEADS = 4
DIFF_DQK = 64
DIFF_DV = 128
NUM_BUCKETS = 32
MAX_DISTANCE = 128
TOP_K = 4
SWIGLU_LIMIT = 7.0
SWIGLU_ALPHA = 1.702
EPS = 1e-6

GLA_QK_W = N_GLA_HEADS * GLA_DK
GLA_V_W = N_GLA_HEADS * GLA_DV
DIFF_QK_W = N_DIFF_HEADS * 2 * DIFF_DQK
DIFF_V_W = N_DIFF_HEADS * DIFF_DV

LANES = 128
NEG = -1e30
LOG2E = math.log2(math.e)
SAFE_SCORE = 40.0
NORM_SLACK = 1.02
VMEM_LIMIT = 48 * 1024 * 1024
VMEM_LIMIT_FFN = 58 * 1024 * 1024

TM_IN = 512
INPROJ_SUB = 2
TG_GLA = 1024
PAIR = 2 * GLA_CHUNK
GLA_UNROLL = 4
TQ = 512
ATTN_UNROLL = 4
TR = 512
TD = 256
ROW_TILE = 8
DMA_UNROLL = 8
N_RUN = 32
RUN_SIZES = tuple(TD >> b for b in range(TD.bit_length()))
LIST_DST = len(RUN_SIZES) * N_RUN
LIST_CNT = 2 * LIST_DST
LIST_LEN = 1024
FFN_BLK = 512


def _nt(a, b):
    return lax.dot_general(a, b, (((1,), (1,)), ((), ())), preferred_element_type=f32)


def _mm(a, b):
    return jnp.dot(a, b, preferred_element_type=f32)


def _split(x):
    hi = x.astype(bf16)
    lo = (x - hi.astype(f32)).astype(bf16)
    return hi, lo


def _silu(x):
    return x * jax.nn.sigmoid(x)


def _ada_kernel(c_ref, w_ref, b_ref, o_ref):
    c = c_ref[...]
    o_ref[...] = _mm(_silu(c).astype(bf16), w_ref[...].astype(bf16)) + b_ref[...]


def _ada(c, w_ada, b_ada):
    B, D = c.shape
    N = w_ada.shape[1]
    bp = ROW_TILE
    assert B <= bp
    cp = jnp.zeros((bp, D), f32).at[:B].set(c)
    tn = N // 4
    out = pl.pallas_call(
        _ada_kernel,
        out_shape=jax.ShapeDtypeStruct((bp, N), f32),
        grid=(N // tn,),
        in_specs=[pl.BlockSpec((bp, D), lambda j: (0, 0)),
                  pl.BlockSpec((D, tn), lambda j: (0, j)),
                  pl.BlockSpec((1, tn), lambda j: (0, j))],
        out_specs=pl.BlockSpec((bp, tn), lambda j: (0, j)),
        compiler_params=pltpu.CompilerParams(vmem_limit_bytes=VMEM_LIMIT),
        name="ada",
    )(cp, w_ada, b_ada.reshape(1, N))
    return out[:B].reshape(B, 6, D)


def _inproj_kernel(x_ref, mod_ref, g1_ref, wm_ref, wkt_ref, wlo_ref, wup_ref, wupt_ref,
                   bup_ref, bupt_ref, gqk_ref, grp_ref, grpt_ref,
                   qg_ref, kg_ref, gk_ref, kgt_ref, gkt_ref, vg_ref, rg_ref,
                   qd_ref, kd_ref, vd_ref):
    tm = x_ref.shape[1]
    sub = tm // INPROJ_SUB
    for t in range(INPROJ_SUB):
        _inproj_rows(slice(t * sub, (t + 1) * sub), x_ref, mod_ref, g1_ref, wm_ref, wkt_ref, wlo_ref, wup_ref,
                     wupt_ref, bup_ref, bupt_ref, gqk_ref, grp_ref, grpt_ref, qg_ref, kg_ref, gk_ref, kgt_ref,
                     gkt_ref, vg_ref, rg_ref, qd_ref, kd_ref, vd_ref)


def _inproj_rows(rows, x_ref, mod_ref, g1_ref, wm_ref, wkt_ref, wlo_ref, wup_ref, wupt_ref,
                 bup_ref, bupt_ref, gqk_ref, grp_ref, grpt_ref,
                 qg_ref, kg_ref, gk_ref, kgt_ref, gkt_ref, vg_ref, rg_ref, qd_ref, kd_ref, vd_ref):
    x = x_ref[0, rows, :]
    ms = jnp.mean(x * x, axis=-1, keepdims=True)
    y = x * lax.rsqrt(ms + EPS) * g1_ref[...]
    h = (y * (1.0 + mod_ref[0, 1:2, :]) + mod_ref[0, 0:1, :]).astype(bf16)

    def proj(a, b):
        return _mm(h, wm_ref[:, a:b])

    o = 0
    qg_ref[0, rows, :] = proj(o, o + GLA_QK_W); o += GLA_QK_W
    kg_ref[0, rows, :] = proj(o, o + GLA_QK_W); o += GLA_QK_W
    vg_ref[0, rows, :] = proj(o, o + GLA_V_W).astype(bf16); o += GLA_V_W
    rg_ref[0, rows, :] = proj(o, o + GLA_V_W); o += GLA_V_W
    qk = proj(o, o + 2 * DIFF_QK_W); o += 2 * DIFF_QK_W
    vd_ref[0, rows, :] = proj(o, o + DIFF_V_W).astype(bf16)

    slab0 = rows.start // PAIR
    kgt = _nt(wkt_ref[...], h)
    for j in range(kgt.shape[1] // PAIR):
        kgt_ref[0, slab0 + j] = kgt[:, j * PAIR:(j + 1) * PAIR]

    lo = _mm(h, wlo_ref[...]).astype(bf16)
    z = _mm(lo, wup_ref[...]) + bup_ref[...]
    gk_ref[0, rows, :] = (jnp.minimum(z, 0.0) - jnp.log1p(jnp.exp(-jnp.abs(z)))) * (1.0 / GLA_GATE_NORM)
    zt = _nt(wupt_ref[...], lo) + bupt_ref[...]
    gkt = (jnp.minimum(zt, 0.0) - jnp.log1p(jnp.exp(-jnp.abs(zt)))) * (1.0 / GLA_GATE_NORM)
    for j in range(gkt.shape[1] // PAIR):
        gkt_ref[0, slab0 + j] = gkt[:, j * PAIR:(j + 1) * PAIR]

    sq_hi, sq_lo = _split(qk * qk)
    gs = _mm(sq_hi, grp_ref[...]) + _mm(sq_lo, grp_ref[...])
    r = lax.rsqrt(gs * (1.0 / DIFF_DQK) + EPS)
    r_hi, r_lo = _split(r)
    rb = _mm(r_hi, grpt_ref[...]) + _mm(r_lo, grpt_ref[...])
    qkn = qk * rb * gqk_ref[...]
    qd_ref[0, rows, :] = qkn[:, :DIFF_QK_W].astype(bf16)
    kd_ref[0, rows, :] = qkn[:, DIFF_QK_W:].astype(bf16)


def _inproj(x, mod, g_norm1, w_in, w_gk_up, b_gk_up, g_qnorm, g_knorm):
    B, S, D = x.shape
    offs = [0]
    for w in (GLA_QK_W, GLA_QK_W, GLA_V_W, GLA_V_W, GLA_GATE_RANK, DIFF_QK_W, DIFF_QK_W, DIFF_V_W):
        offs.append(offs[-1] + w)
    w_main = jnp.concatenate([w_in[:, offs[0]:offs[4]], w_in[:, offs[5]:offs[8]]], axis=1).astype(bf16)
    w_kt = w_in[:, offs[1]:offs[2]].T.astype(bf16)
    w_lo = jnp.zeros((D, LANES), f32).at[:, :GLA_GATE_RANK].set(w_in[:, offs[4]:offs[5]]).astype(bf16)
    w_up = jnp.zeros((LANES, GLA_QK_W), f32).at[:GLA_GATE_RANK].set(w_gk_up).astype(bf16)
    w_upt = w_up.T
    b_up = b_gk_up.reshape(1, GLA_QK_W)
    b_upt = b_gk_up.reshape(GLA_QK_W, 1)
    n_grp = 2 * DIFF_QK_W // DIFF_DQK
    gqk = jnp.concatenate([jnp.tile(g_qnorm, n_grp // 2) * (DIFF_DQK ** -0.5 * LOG2E),
                           jnp.tile(g_knorm, n_grp // 2)]).reshape(1, 2 * DIFF_QK_W)
    grp = (jnp.arange(2 * DIFF_QK_W)[:, None] // DIFF_DQK == jnp.arange(LANES)[None, :]).astype(bf16)
    grpt = grp.T
    nw = w_main.shape[1]
    tm = TM_IN
    const = lambda shape: pl.BlockSpec(shape, lambda b, i: (0,) * len(shape))
    row = lambda w: pl.BlockSpec((1, tm, w), lambda b, i: (b, i, 0))
    colT = pl.BlockSpec((1, tm // PAIR, GLA_QK_W, PAIR), lambda b, i: (b, i, 0, 0))
    outs = pl.pallas_call(
        _inproj_kernel,
        out_shape=[jax.ShapeDtypeStruct((B, S, GLA_QK_W), f32),
                   jax.ShapeDtypeStruct((B, S, GLA_QK_W), f32),
                   jax.ShapeDtypeStruct((B, S, GLA_QK_W), f32),
                   jax.ShapeDtypeStruct((B, S // PAIR, GLA_QK_W, PAIR), f32),
                   jax.ShapeDtypeStruct((B, S // PAIR, GLA_QK_W, PAIR), f32),
                   jax.ShapeDtypeStruct((B, S, GLA_V_W), bf16),
                   jax.ShapeDtypeStruct((B, S, GLA_V_W), f32),
                   jax.ShapeDtypeStruct((B, S, DIFF_QK_W), bf16),
                   jax.ShapeDtypeStruct((B, S, DIFF_QK_W), bf16),
                   jax.ShapeDtypeStruct((B, S, DIFF_V_W), bf16)],
        grid=(B, S // tm),
        in_specs=[row(D),
                  pl.BlockSpec((1, 6, D), lambda b, i: (b, 0, 0)),
                  const((1, D)), const((D, nw)), const((GLA_QK_W, D)), const((D, LANES)),
                  const((LANES, GLA_QK_W)), const((GLA_QK_W, LANES)),
                  const((1, GLA_QK_W)), const((GLA_QK_W, 1)),
                  const((1, 2 * DIFF_QK_W)), const((2 * DIFF_QK_W, LANES)),
                  const((LANES, 2 * DIFF_QK_W))],
        out_specs=[row(GLA_QK_W), row(GLA_QK_W), row(GLA_QK_W), colT, colT,
                   row(GLA_V_W), row(GLA_V_W), row(DIFF_QK_W), row(DIFF_QK_W), row(DIFF_V_W)],
        compiler_params=pltpu.CompilerParams(
            dimension_semantics=("arbitrary", "arbitrary"), vmem_limit_bytes=VMEM_LIMIT),
        name="inproj",
    )(x, mod, g_norm1.reshape(1, D), w_main, w_kt, w_lo, w_up, w_upt, b_up, b_upt, gqk, grp, grpt)
    return outs


def _gla_kernel(q_ref, k_ref, g_ref, kt_ref, gt_ref, v_ref, r_ref, gout_ref, tri_ref, trit_ref,
                o_ref, s_ref, *, n_pairs):
    H, DK, DV = N_GLA_HEADS, GLA_DK, GLA_DV

    @pl.when(pl.program_id(1) == 0)
    def _():
        s_ref[...] = jnp.zeros_like(s_ref)

    tri = tri_ref[...]
    trit = trit_ref[...]
    tri_b = tri > 0
    lane_head = lax.broadcasted_iota(jnp.int32, (1, H * DK), 1) // DK
    row_head = lax.broadcasted_iota(jnp.int32, (H * PAIR, 1), 0) // PAIR
    qmask = row_head == lane_head
    row_first = lax.broadcasted_iota(jnp.int32, (PAIR, 1), 0) < GLA_CHUNK
    row_first4 = (lax.broadcasted_iota(jnp.int32, (H * PAIR, 1), 0) % PAIR) < GLA_CHUNK
    lane_first = lax.broadcasted_iota(jnp.int32, (1, PAIR), 1) < GLA_CHUNK
    scale = DK ** -0.5
    gout = gout_ref[...]

    def pair(p, carry):
        r0 = pl.multiple_of(p * PAIR, PAIR)
        q = q_ref[0, pl.ds(r0, PAIR), :]
        k = k_ref[0, pl.ds(r0, PAIR), :]
        g = g_ref[0, pl.ds(r0, PAIR), :]
        kt = kt_ref[0, p]
        gt = gt_ref[0, p]
        v = v_ref[0, pl.ds(r0, PAIR), :]

        g_hi, g_lo = _split(g)
        gc = _mm(tri, g_hi) + _mm(tri, g_lo)
        gt_hi, gt_lo = _split(gt)
        gct = _mm(gt_hi, trit) + _mm(gt_lo, trit)
        g_last = jnp.where(row_first, gc[GLA_CHUNK - 1:GLA_CHUNK, :], gc[PAIR - 1:PAIR, :])
        gl0 = gct[:, GLA_CHUNK - 1:GLA_CHUNK]
        gl1 = gct[:, PAIR - 1:PAIR]
        g_last_t = jnp.where(lane_first, gl0, gl1)

        q_e = (q * (jnp.exp(gc) * scale)).astype(bf16)
        k_e = (k * jnp.exp(-gc)).astype(bf16)
        ks_t = kt * jnp.exp(g_last_t - gct)
        ks_t0 = jnp.where(lane_first, ks_t, 0.0).astype(bf16)
        ks_t1 = jnp.where(lane_first, 0.0, ks_t).astype(bf16)
        del g_last

        qm = jnp.where(qmask, jnp.concatenate([q_e] * H, axis=0), jnp.zeros((), bf16))
        a = _nt(qm, k_e)
        s0 = s_ref[...]

        u0 = []
        u1 = []
        for h in range(H):
            v_h = v[:, h * DV:(h + 1) * DV]
            u0.append(_mm(ks_t0[h * DK:(h + 1) * DK], v_h))
            u1.append(_mm(ks_t1[h * DK:(h + 1) * DK], v_h))
        u0 = jnp.concatenate(u0, axis=0)
        u1 = jnp.concatenate(u1, axis=0)
        s1 = s0 * jnp.exp(gl0) + u0
        s_ref[...] = s1 * jnp.exp(gl1) + u1

        o_inter = jnp.where(row_first4, _mm(qm, s0.astype(bf16)), _mm(qm, s1.astype(bf16)))
        for h in range(H):
            a_h = jnp.where(tri_b, a[h * PAIR:(h + 1) * PAIR], 0.0).astype(bf16)
            o_h = _mm(a_h, v[:, h * DV:(h + 1) * DV]) + o_inter[h * PAIR:(h + 1) * PAIR]
            ms = jnp.mean(o_h * o_h, axis=-1, keepdims=True)
            o_n = o_h * lax.rsqrt(ms + EPS) * gout
            r_h = r_ref[0, pl.ds(r0, PAIR), h * DV:(h + 1) * DV]
            o_ref[0, pl.ds(r0, PAIR), h * DV:(h + 1) * DV] = (o_n * _silu(r_h)).astype(bf16)
        return carry

    lax.fori_loop(0, n_pairs, pair, 0, unroll=GLA_UNROLL)


def _gla(qg, kg, gk, kgt, gkt, vg, rg, g_gla_out):
    B, S, _ = qg.shape
    tg = min(TG_GLA, S)
    r = jnp.arange(PAIR)
    tri = ((r[:, None] // GLA_CHUNK == r[None, :] // GLA_CHUNK) & (r[None, :] <= r[:, None])).astype(bf16)
    row = lambda w: pl.BlockSpec((1, tg, w), lambda b, i: (b, i, 0))
    colT = pl.BlockSpec((1, tg // PAIR, GLA_QK_W, PAIR), lambda b, i: (b, i, 0, 0))
    const = lambda shape: pl.BlockSpec(shape, lambda b, i: (0,) * len(shape))
    return pl.pallas_call(
        functools.partial(_gla_kernel, n_pairs=tg // PAIR),
        out_shape=jax.ShapeDtypeStruct((B, S, GLA_V_W), bf16),
        grid=(B, S // tg),
        in_specs=[row(GLA_QK_W), row(GLA_QK_W), row(GLA_QK_W), colT, colT,
                  row(GLA_V_W), row(GLA_V_W), const((1, GLA_DV)),
                  const((PAIR, PAIR)), const((PAIR, PAIR))],
        out_specs=row(GLA_V_W),
        scratch_shapes=[pltpu.VMEM((GLA_QK_W, GLA_DV), f32)],
        compiler_params=pltpu.CompilerParams(
            dimension_semantics=("arbitrary", "arbitrary"), vmem_limit_bytes=VMEM_LIMIT),
        name="gla",
    )(qg, kg, gk, kgt, gkt, vg, rg, g_gla_out.reshape(1, GLA_DV), tri, tri.T)


def _attn_finish(o, gsub_ref, o_ref, lambda_init):
    ms = jnp.mean(o * o, axis=-1, keepdims=True)
    o_ref[0] = (o * lax.rsqrt(ms + EPS) * gsub_ref[...] * (1.0 - lambda_init)).astype(bf16)


def _attn_lambda(lamv_ref, lambda_init):
    lv = lamv_ref[...]
    return (jnp.exp(jnp.sum(lv[0:1] * lv[1:2], axis=-1, keepdims=True))
            - jnp.exp(jnp.sum(lv[2:3] * lv[3:4], axis=-1, keepdims=True)) + lambda_init)


def _attn_bounded_kernel(q_ref, k_ref, v_ref, bias_ref, lamv_ref, gsub_ref, o_ref, vaug_ref, *, lambda_init):
    qi = pl.program_id(2)
    tq = q_ref.shape[1]
    S = k_ref.shape[1]

    @pl.when(qi == 0)
    def _():
        lane = lax.broadcasted_iota(jnp.int32, (S, DIFF_DV), 1)
        vaug_ref[:, :DIFF_DV] = v_ref[0]
        vaug_ref[:, DIFF_DV:] = jnp.where(lane == 0, 1.0, 0.0).astype(bf16)

    q = q_ref[0]
    lane = lax.broadcasted_iota(jnp.int32, (1, 2 * DIFF_DQK), 1)
    zero = jnp.zeros((), bf16)
    qs = (jnp.where(lane < DIFF_DQK, q, zero), jnp.where(lane < DIFF_DQK, zero, q))

    def update(accs, k0, bias):
        kb = k_ref[0, pl.ds(k0, tq), :]
        vb = vaug_ref[pl.ds(k0, tq), :]
        out = []
        for c in range(2):
            s = _nt(qs[c], kb)
            if bias is not None:
                s = s + bias[c]
            out.append(accs[c] + _mm(jnp.exp2(s).astype(bf16), vb))
        return tuple(out)

    def far(kj, accs):
        return update(accs, pl.multiple_of(kj * tq, tq), None)

    def far_group(g, accs):
        for u in range(ATTN_UNROLL):
            accs = far(g * ATTN_UNROLL + u, accs)
        return accs

    def block_or_masked(accs, kj, bias):
        exists = kj >= 0
        k0 = pl.multiple_of(jnp.maximum(kj, 0) * tq, tq)
        if bias is None:
            tiles = (jnp.where(exists, 0.0, NEG),) * 2
        else:
            tiles = tuple(jnp.where(exists, b, NEG) for b in bias)
        return update(accs, k0, tiles)

    accs = (jnp.zeros((tq, 2 * DIFF_DV), f32), jnp.zeros((tq, 2 * DIFF_DV), f32))
    accs = update(accs, pl.multiple_of(qi * tq, tq), (bias_ref[0, 0, 1], bias_ref[0, 1, 1]))
    accs = block_or_masked(accs, qi - 1, (bias_ref[0, 0, 0], bias_ref[0, 1, 0]))
    for u in range(2, ATTN_UNROLL):
        accs = block_or_masked(accs, qi - u, None)
    n_far = jnp.maximum(qi + 1 - ATTN_UNROLL, 0)
    n_grp = n_far // ATTN_UNROLL
    accs = lax.fori_loop(0, n_grp, far_group, accs)
    accs = lax.fori_loop(n_grp * ATTN_UNROLL, n_far, far, accs)
    a0, a1 = accs
    o = (a0[:, :DIFF_DV] / a0[:, DIFF_DV:DIFF_DV + 1]
         - _attn_lambda(lamv_ref, lambda_init) * (a1[:, :DIFF_DV] / a1[:, DIFF_DV:DIFF_DV + 1]))
    _attn_finish(o, gsub_ref, o_ref, lambda_init)


def _attn_kernel(q_ref, k_ref, v_ref, bias_ref, lamv_ref, gsub_ref, o_ref, *, lambda_init):
    qi = pl.program_id(2)
    tq = q_ref.shape[1]
    q = q_ref[0]
    lane = lax.broadcasted_iota(jnp.int32, (1, 2 * DIFF_DQK), 1)
    zero = jnp.zeros((), bf16)
    qs = (jnp.where(lane < DIFF_DQK, q, zero), jnp.where(lane < DIFF_DQK, zero, q))

    def update(state, kb, vb, bias):
        new = []
        for c in range(2):
            m, l, acc = state[c]
            s = _nt(qs[c], kb)
            if bias is not None:
                s = s + bias[c]
            m_new = jnp.maximum(m, jnp.max(s, axis=-1, keepdims=True))
            alpha = jnp.exp2(m - m_new)
            p = jnp.exp2(s - m_new)
            l = alpha * l + jnp.sum(p, axis=-1, keepdims=True)
            acc = alpha * acc + _mm(p.astype(bf16), vb)
            new.append((m_new, l, acc))
        return tuple(new)

    init = tuple((jnp.full((tq, 1), NEG, f32), jnp.zeros((tq, 1), f32), jnp.zeros((tq, DIFF_DV), f32))
                 for _ in range(2))

    def far(kj, state):
        k0 = pl.multiple_of(kj * tq, tq)
        return update(state, k_ref[0, pl.ds(k0, tq), :], v_ref[0, pl.ds(k0, tq), :], None)

    state = lax.fori_loop(0, jnp.maximum(qi - 1, 0), far, init)

    kd0 = pl.multiple_of(qi * tq, tq)
    state = update(state, k_ref[0, pl.ds(kd0, tq), :], v_ref[0, pl.ds(kd0, tq), :],
                   (bias_ref[0, 0, 1], bias_ref[0, 1, 1]))
    kp0 = pl.multiple_of(jnp.maximum(qi - 1, 0) * tq, tq)
    has_prev = qi > 0
    state = update(state, k_ref[0, pl.ds(kp0, tq), :], v_ref[0, pl.ds(kp0, tq), :],
                   (jnp.where(has_prev, bias_ref[0, 0, 0], NEG), jnp.where(has_prev, bias_ref[0, 1, 0], NEG)))

    (_, l0, a0), (_, l1, a1) = state
    o = a0 / l0 - _attn_lambda(lamv_ref, lambda_init) * (a1 / l1)
    _attn_finish(o, gsub_ref, o_ref, lambda_init)


def _t5_bucket(n):
    max_exact = NUM_BUCKETS // 2
    nf = jnp.maximum(n, 1).astype(f32)
    large = max_exact + (jnp.log(nf / max_exact) / math.log(MAX_DISTANCE / max_exact)
                         * (NUM_BUCKETS - max_exact)).astype(jnp.int32)
    large = jnp.minimum(large, NUM_BUCKETS - 1)
    return jnp.where(n < max_exact, n, large)


def _toeplitz_kernel(w_ref, o_ref):
    n = o_ref.shape[-1]
    for t in range(2):
        rows = jnp.broadcast_to(w_ref[0, t:t + 1, :], (n, 2 * n))
        o_ref[0, 0, t] = pltpu.roll(rows, 0, 1, stride=1, stride_axis=0)[:, n:]


def _bias_tiles(rel_bias_table, S, n):
    HM = rel_bias_table.shape[1]
    assert n >= MAX_DISTANCE
    d = jnp.arange(2 * n, dtype=jnp.int32)
    by_dist = rel_bias_table[_t5_bucket(d)].astype(f32).T
    rel = (by_dist - rel_bias_table[NUM_BUCKETS - 1].astype(f32)[:, None]) * LOG2E
    i = jnp.arange(2 * n)
    w_diag = jnp.where(i[None, :] <= n, rel[:, jnp.clip(n - i, 0, 2 * n - 1)], NEG)
    w_prev = rel[:, jnp.clip(2 * n - i, 0, 2 * n - 1)]
    w = jnp.stack([w_prev, w_diag], axis=1)
    return pl.pallas_call(
        _toeplitz_kernel,
        out_shape=jax.ShapeDtypeStruct((HM // 2, 2, 2, n, n), f32),
        grid=(HM // 2, 2),
        in_specs=[pl.BlockSpec((1, 2, 2 * n), lambda h, m: (h * 2 + m, 0, 0))],
        out_specs=pl.BlockSpec((1, 1, 2, n, n), lambda h, m: (h, m, 0, 0, 0)),
        compiler_params=pltpu.CompilerParams(vmem_limit_bytes=VMEM_LIMIT),
        name="bias_tiles",
    )(w)


def _attn(qd, kd, vd, bias_tiles, lamv, g_subln, lambda_init, bounded):
    B, S, _ = qd.shape
    H = N_DIFF_HEADS
    tq = min(TQ, S)
    body = _attn_bounded_kernel if bounded else _attn_kernel
    scratch = [pltpu.VMEM((S, 2 * DIFF_DV), bf16)] if bounded else []
    return pl.pallas_call(
        functools.partial(body, lambda_init=lambda_init),
        out_shape=jax.ShapeDtypeStruct((B, S, DIFF_V_W), bf16),
        scratch_shapes=scratch,
        grid=(B, H, S // tq),
        in_specs=[pl.BlockSpec((1, tq, 2 * DIFF_DQK), lambda b, h, i: (b, i, h)),
                  pl.BlockSpec((1, S, 2 * DIFF_DQK), lambda b, h, i: (b, 0, h)),
                  pl.BlockSpec((1, S, DIFF_DV), lambda b, h, i: (b, 0, h)),
                  pl.BlockSpec((1, 2, 2, tq, tq), lambda b, h, i: (h, 0, 0, 0, 0)),
                  pl.BlockSpec((4, DIFF_DQK), lambda b, h, i: (0, 0)),
                  pl.BlockSpec((1, DIFF_DV), lambda b, h, i: (0, 0))],
        out_specs=pl.BlockSpec((1, tq, DIFF_DV), lambda b, h, i: (b, i, h)),
        compiler_params=pltpu.CompilerParams(
            dimension_semantics=("arbitrary", "arbitrary", "arbitrary"), vmem_limit_bytes=VMEM_LIMIT),
        name="attn_bounded" if bounded else "attn_online",
    )(qd, kd, vd, bias_tiles, lamv, g_subln.reshape(1, DIFF_DV))


def _scores_bounded(rel_bias_table, g_qnorm, g_knorm):
    qk = DIFF_DQK ** 0.5 * jnp.max(jnp.abs(g_qnorm)) * jnp.max(jnp.abs(g_knorm)) * NORM_SLACK
    rel = jnp.max(jnp.abs(rel_bias_table - rel_bias_table[NUM_BUCKETS - 1:]))
    return qk + rel <= SAFE_SCORE


def _rows_to_tiles(x, ref):
    n = x.shape[0]
    for c in range(ROW_TILE):
        ref[pl.ds(c, n, stride=ROW_TILE), :] = x[:, c * LANES:(c + 1) * LANES]


def _tiles_to_rows(ref, n):
    return jnp.concatenate([ref[pl.ds(c, n, stride=ROW_TILE), :] for c in range(ROW_TILE)], axis=1)


def _outproj_kernel(og_ref, od_ref, x_ref, mod_ref, wo_ref, g2_ref, wr_ref, br_ref,
                    x1_ref, hp_ref, lg_ref):
    half = og_ref.shape[2]
    sub = og_ref.shape[1] // INPROJ_SUB
    for t in range(INPROJ_SUB):
        rows = slice(t * sub, (t + 1) * sub)
        mix = _mm(og_ref[0, rows, :], wo_ref[:half, :]) + _mm(od_ref[0, rows, :], wo_ref[half:, :])
        x1 = x_ref[0, rows, :] + mod_ref[0, 2:3, :] * mix
        x1_ref[0, rows, :] = x1
        ms = jnp.mean(x1 * x1, axis=-1, keepdims=True)
        y = x1 * lax.rsqrt(ms + EPS) * g2_ref[...]
        h = (y * (1.0 + mod_ref[0, 4:5, :]) + mod_ref[0, 3:4, :]).astype(bf16)
        lg_ref[rows, :] = _mm(h, wr_ref[...]) + br_ref[...]
        _rows_to_tiles(h.astype(f32), hp_ref.at[pl.ds(t * sub * ROW_TILE, sub * ROW_TILE)])


def _outproj(og, od, x, mod, w_out, g_norm2, w_router, b_router):
    B, S, D = x.shape
    assert D == ROW_TILE * LANES, "the token-tile layout needs a model row to fill one (8,128) tile"
    E = w_router.shape[1]
    tm = TM_IN
    nj = S // tm
    w_r = jnp.zeros((D, LANES), f32).at[:, :E].set(w_router).astype(bf16)
    b_r = jnp.full((1, LANES), NEG, f32).at[0, :E].set(b_router)
    const = lambda shape: pl.BlockSpec(shape, lambda b, i: (0,) * len(shape))
    return pl.pallas_call(
        _outproj_kernel,
        out_shape=[jax.ShapeDtypeStruct((B, S, D), f32),
                   jax.ShapeDtypeStruct((B * S * ROW_TILE, LANES), f32),
                   jax.ShapeDtypeStruct((B * S, LANES), f32)],
        grid=(B, nj),
        in_specs=[pl.BlockSpec((1, tm, og.shape[2]), lambda b, i: (b, i, 0)),
                  pl.BlockSpec((1, tm, od.shape[2]), lambda b, i: (b, i, 0)),
                  pl.BlockSpec((1, tm, D), lambda b, i: (b, i, 0)),
                  pl.BlockSpec((1, 6, D), lambda b, i: (b, 0, 0)),
                  const((w_out.shape[0], D)), const((1, D)), const((D, LANES)), const((1, LANES))],
        out_specs=[pl.BlockSpec((1, tm, D), lambda b, i: (b, i, 0)),
                   pl.BlockSpec((tm * ROW_TILE, LANES), lambda b, i: (b * nj + i, 0)),
                   pl.BlockSpec((tm, LANES), lambda b, i: (b * nj + i, 0))],
        compiler_params=pltpu.CompilerParams(
            dimension_semantics=("arbitrary", "arbitrary"), vmem_limit_bytes=VMEM_LIMIT),
        name="outproj",
    )(og, od, x, mod, w_out.astype(bf16), g_norm2.reshape(1, D), w_r, b_r)


def _route_kernel(lg_ref, lt_ref, ri_ref, rw_ref, cnt_ref, snap_ref, run_ref):
    @pl.when(pl.program_id(0) == 0)
    def _():
        run_ref[...] = jnp.zeros_like(run_ref)

    x = lg_ref[...]
    tr = x.shape[0]
    lane = lax.broadcasted_iota(jnp.int32, (tr, LANES), 1)
    lane_f = lane.astype(f32)
    vals, hots, idxs = [], [], []
    for _ in range(TOP_K):
        m = jnp.max(x, axis=-1, keepdims=True)
        idx = jnp.min(jnp.where(x == m, lane_f, float(LANES)), axis=-1, keepdims=True)
        hot = lane_f == idx
        x = jnp.where(hot, -jnp.inf, x)
        vals.append(m)
        hots.append(hot)
        idxs.append(idx.astype(jnp.int32))
    ex = [jnp.exp(v - vals[0]) for v in vals]
    den = ex[0] + ex[1] + ex[2] + ex[3]
    sel = (hots[0] | hots[1] | hots[2] | hots[3]).astype(f32)
    rank = _mm(lt_ref[...], sel.astype(bf16)) + run_ref[...]
    run_ref[...] = run_ref[...] + jnp.sum(sel, axis=0, keepdims=True)
    cnt_ref[...] = run_ref[...]
    for t in range(tr // TD):
        snap_ref[0, t:t + 1, :] = rank[t * TD:t * TD + 1, :]
    ri = jnp.zeros((tr, LANES), jnp.int32)
    rw = jnp.zeros((tr, LANES), f32)
    for k in range(TOP_K):
        rk = jnp.sum(jnp.where(hots[k], rank, 0.0), axis=-1, keepdims=True).astype(jnp.int32)
        ri = jnp.where(lane == k, rk, ri)
        ri = jnp.where(lane == TOP_K + k, idxs[k], ri)
        rw = jnp.where(lane == k, ex[k] / den, rw)
    ri_ref[...] = ri
    rw_ref[...] = rw


def _route(logits):
    T = logits.shape[0]
    tr = min(TR, T)
    r = jnp.arange(tr)
    lt = (r[None, :] < r[:, None]).astype(bf16)
    return pl.pallas_call(
        _route_kernel,
        out_shape=[jax.ShapeDtypeStruct((T, LANES), jnp.int32),
                   jax.ShapeDtypeStruct((T, LANES), f32),
                   jax.ShapeDtypeStruct((1, LANES), f32),
                   jax.ShapeDtypeStruct((T // tr, tr // TD, LANES), f32)],
        grid=(T // tr,),
        in_specs=[pl.BlockSpec((tr, LANES), lambda i: (i, 0)),
                  pl.BlockSpec((tr, tr), lambda i: (0, 0))],
        out_specs=[pl.BlockSpec((tr, LANES), lambda i: (i, 0)),
                   pl.BlockSpec((tr, LANES), lambda i: (i, 0)),
                   pl.BlockSpec((1, LANES), lambda i: (0, 0)),
                   pl.BlockSpec((1, tr // TD, LANES), lambda i: (i, 0, 0))],
        scratch_shapes=[pltpu.VMEM((1, LANES), f32)],
        compiler_params=pltpu.CompilerParams(dimension_semantics=("arbitrary",)),
        name="route",
    )(logits, lt)


def _run_copies(list_ref, hbm_ref, stage_ref, sem, to_hbm):
    for c, size in enumerate(RUN_SIZES):
        def one(i, carry, c=c, size=size):
            s0 = list_ref[c * N_RUN + i]
            d0 = list_ref[LIST_DST + c * N_RUN + i]
            stage = stage_ref.at[pl.ds(pl.multiple_of(s0, ROW_TILE), size * ROW_TILE)]
            rows = hbm_ref.at[pl.ds(pl.multiple_of(d0, ROW_TILE), size * ROW_TILE)]
            src, dst = (stage, rows) if to_hbm else (rows, stage)
            pltpu.make_async_copy(src, dst, sem).start(priority=c % 2)
            return carry
        lax.fori_loop(0, list_ref[LIST_CNT + c], one, 0)


def _copy_lists(run_dst, run_n, run_off):
    n_tiles = run_n.shape[0]
    size = jnp.asarray(RUN_SIZES, jnp.int32)[None, :, None]
    n = run_n[:, None, :]
    bit = (n & size) != 0
    before = n & ~(2 * size - 1)
    place = jnp.cumsum(bit, axis=-1) - 1
    pick = bit[:, :, None, :] & (place[:, :, None, :] == jnp.arange(N_RUN)[None, None, :, None])

    def compact(v):
        return jnp.sum(jnp.where(pick, v[:, :, None, :], 0), axis=-1) * ROW_TILE
    src = compact(run_off[:, None, :] + before)
    dst = compact(run_dst[:, None, :] + before)
    cnt = jnp.sum(bit, axis=-1).astype(jnp.int32)
    pad = jnp.zeros((n_tiles, LIST_LEN - LIST_CNT - len(RUN_SIZES)), jnp.int32)
    lists = jnp.concatenate([src.reshape(n_tiles, -1), dst.reshape(n_tiles, -1), cnt, pad], axis=1)
    return lists.reshape(-1).astype(jnp.int32)


def _dispatch_kernel(pend_ref, cnt_ref, nu_ref, lpos_ref, list_ref, h_ref, xs_ref,
                     zero_ref, stage_ref, sem, zsem):
    n_tok = h_ref.shape[0] // ROW_TILE
    blk_rows = FFN_BLK * ROW_TILE

    @pl.when(pl.program_id(0) == 0)
    def _():
        zero_ref[...] = jnp.zeros_like(zero_ref)
        n_exp = pend_ref.shape[0]

        def last_block(e):
            return xs_ref.at[pl.ds(pl.multiple_of((pend_ref[e] - FFN_BLK) * ROW_TILE, blk_rows), blk_rows)]

        def zfill(e, c):
            @pl.when(cnt_ref[e] > 0)
            def _():
                pltpu.make_async_copy(zero_ref, last_block(e), zsem).start()
            return c

        def zwait(e, c):
            @pl.when(cnt_ref[e] > 0)
            def _():
                pltpu.make_async_copy(zero_ref, last_block(e), zsem).wait()
            return c

        lax.fori_loop(0, n_exp, zfill, 0)
        lax.fori_loop(0, n_exp, zwait, 0)

        def tail_block(i):
            return xs_ref.at[pl.ds(pl.multiple_of(i * blk_rows, blk_rows), blk_rows)]

        def tfill(i, c):
            pltpu.make_async_copy(zero_ref, tail_block(i), zsem).start()
            return c

        def twait(i, c):
            pltpu.make_async_copy(zero_ref, tail_block(i), zsem).wait()
            return c

        n_blk = xs_ref.shape[0] // blk_rows
        lax.fori_loop(nu_ref[0], n_blk, tfill, 0)
        lax.fori_loop(nu_ref[0], n_blk, twait, 0)

    step = pl.program_id(0)
    slot = step % 2
    stage = stage_ref.at[slot]

    def place(g, c):
        for u in range(DMA_UNROLL):
            r = g * DMA_UNROLL + u
            row = h_ref[pl.ds(pl.multiple_of(r * ROW_TILE, ROW_TILE), ROW_TILE), :]
            for k in range(TOP_K):
                p = lpos_ref[r * TOP_K + k]
                stage[pl.ds(pl.multiple_of(p, ROW_TILE), ROW_TILE), :] = row
        return c
    lax.fori_loop(0, n_tok // DMA_UNROLL, place, 0)

    _run_copies(list_ref, xs_ref, stage, sem.at[slot], to_hbm=True)

    def drain(s):
        pltpu.make_async_copy(stage_ref.at[s], stage_ref.at[s], sem.at[s]).wait()

    @pl.when(step > 0)
    def _():
        drain(1 - slot)

    @pl.when(step == pl.num_programs(0) - 1)
    def _():
        drain(slot)


def _dispatch(p_ends, counts, n_used, lpos_flat, lists, hp, n_rows):
    T = hp.shape[0] // ROW_TILE
    grid_spec = pltpu.PrefetchScalarGridSpec(
        num_scalar_prefetch=3,
        grid=(T // TD,),
        in_specs=[pl.BlockSpec((TD * TOP_K,), lambda i, *_: (i,), memory_space=pltpu.SMEM),
                  pl.BlockSpec((LIST_LEN,), lambda i, *_: (i,), memory_space=pltpu.SMEM),
                  pl.BlockSpec((TD * ROW_TILE, LANES), lambda i, *_: (i, 0))],
        out_specs=pl.BlockSpec(memory_space=pl.ANY),
        scratch_shapes=[pltpu.VMEM((FFN_BLK * ROW_TILE, LANES), f32),
                        pltpu.VMEM((2, TD * TOP_K * ROW_TILE, LANES), f32),
                        pltpu.SemaphoreType.DMA((2,)), pltpu.SemaphoreType.DMA(())],
    )
    return pl.pallas_call(
        _dispatch_kernel,
        out_shape=jax.ShapeDtypeStruct((n_rows * ROW_TILE, LANES), f32),
        grid_spec=grid_spec,
        compiler_params=pltpu.CompilerParams(dimension_semantics=("arbitrary",), vmem_limit_bytes=VMEM_LIMIT),
        name="dispatch",
    )(p_ends, counts, n_used, lpos_flat, lists, hp)


def _ffn_kernel(be_ref, nu_ref, nx_ref, par_ref, xs_ref, wgu_hbm, bgu_ref, wd_hbm, bd_ref, ys_ref,
                wgu32_ref, wd32_ref, wgu_ref, wd_ref, sem):
    i = pl.program_id(0)
    used = i < nu_ref[0]
    new_expert = (i == 0) | (be_ref[i] != be_ref[jnp.maximum(i - 1, 0)])
    slot = par_ref[i]

    def weight_copies(e, s):
        return (pltpu.make_async_copy(wgu_hbm.at[e], wgu32_ref.at[s], sem.at[0, s]),
                pltpu.make_async_copy(wd_hbm.at[e], wd32_ref.at[s], sem.at[1, s]))

    @pl.when(i == 0)
    def _():
        for cp in weight_copies(be_ref[0], 0):
            cp.start()

    @pl.when(used & new_expert)
    def _():
        for cp in weight_copies(be_ref[i], slot):
            cp.wait()

        @pl.when(nx_ref[i] >= 0)
        def _():
            for cp in weight_copies(nx_ref[i], 1 - slot):
                cp.start()

        rows = 128

        def cast(src, dst):
            def body(r, c):
                r0 = pl.multiple_of(r * rows, rows)
                dst[pl.ds(r0, rows), :] = src[slot, pl.ds(r0, rows), :].astype(bf16)
                return c
            lax.fori_loop(0, src.shape[1] // rows, body, 0)
        cast(wgu32_ref, wgu_ref)
        cast(wd32_ref, wd_ref)

    @pl.when(used)
    def _():
        F = wd_ref.shape[0]
        xrow = _tiles_to_rows(xs_ref, FFN_BLK).astype(bf16)
        acc = None
        fc = F // 2
        for c in range(2):
            def gu(col0):
                return _mm(xrow, wgu_ref[:, col0:col0 + fc]) + bgu_ref[0, :, col0:col0 + fc]
            gate = jnp.minimum(gu(c * fc), SWIGLU_LIMIT)
            up = jnp.clip(gu(F + c * fc), -SWIGLU_LIMIT, SWIGLU_LIMIT)
            y = (up + 1.0) * (gate * jax.nn.sigmoid(SWIGLU_ALPHA * gate))
            part = _mm(y.astype(bf16), wd_ref[c * fc:(c + 1) * fc, :])
            acc = part if acc is None else acc + part
        _rows_to_tiles(acc + bd_ref[0], ys_ref)

    @pl.when(jnp.logical_not(used))
    def _():
        ys_ref[...] = jnp.zeros_like(ys_ref)


def _ffn(block_e, n_used, xs, w_gate_up, b_gate_up, w_down, b_down):
    E, D, F2 = w_gate_up.shape
    F = F2 // 2
    P = xs.shape[0] // ROW_TILE
    nb = P // FFN_BLK
    rows = FFN_BLK * ROW_TILE

    idx = jnp.arange(nb, dtype=jnp.int32)
    live = idx < n_used[0]
    later_other = (block_e[None, :] != block_e[:, None]) & (idx[None, :] > idx[:, None]) & live[None, :]
    nxt = jnp.where(jnp.any(later_other, axis=1), block_e[jnp.argmax(later_other, axis=1)], -1).astype(jnp.int32)
    starts = jnp.concatenate([jnp.ones((1,), jnp.int32), (block_e[1:] != block_e[:-1]).astype(jnp.int32)])
    parity = ((jnp.cumsum(starts) - 1) % 2).astype(jnp.int32)

    def blk(i, nu):
        return jnp.minimum(i, nu[0] - 1)

    grid_spec = pltpu.PrefetchScalarGridSpec(
        num_scalar_prefetch=4,
        grid=(nb,),
        in_specs=[pl.BlockSpec((rows, LANES), lambda i, be, nu, nx, pa: (blk(i, nu), 0)),
                  pl.BlockSpec(memory_space=pl.ANY),
                  pl.BlockSpec((1, 1, F2), lambda i, be, nu, nx, pa: (be[blk(i, nu)], 0, 0)),
                  pl.BlockSpec(memory_space=pl.ANY),
                  pl.BlockSpec((1, 1, D), lambda i, be, nu, nx, pa: (be[blk(i, nu)], 0, 0))],
        out_specs=pl.BlockSpec((rows, LANES), lambda i, be, nu, nx, pa: (i, 0)),
        scratch_shapes=[pltpu.VMEM((2, D, F2), f32), pltpu.VMEM((2, F, D), f32),
                        pltpu.VMEM((D, F2), bf16), pltpu.VMEM((F, D), bf16),
                        pltpu.SemaphoreType.DMA((2, 2))],
    )
    return pl.pallas_call(
        _ffn_kernel,
        out_shape=jax.ShapeDtypeStruct((P * ROW_TILE, LANES), f32),
        grid_spec=grid_spec,
        compiler_params=pltpu.CompilerParams(
            dimension_semantics=("arbitrary",), vmem_limit_bytes=VMEM_LIMIT_FFN),
        name="ffn",
    )(block_e, n_used, nxt, parity, xs, w_gate_up, b_gate_up.reshape(E, 1, F2), w_down, b_down.reshape(E, 1, D))


def _combine_kernel(lpos_ref, lcur_ref, lnext_ref, ys_ref, x1_ref, rw_ref, mod_ref, o_ref,
                    stage_ref, acc_ref, wb_ref, sem):
    step = pl.program_id(0) * pl.num_programs(1) + pl.program_id(1)
    n_steps = pl.num_programs(0) * pl.num_programs(1)
    slot = step % 2

    def fetch(list_ref, s):
        _run_copies(list_ref, ys_ref, stage_ref.at[s], sem.at[s], to_hbm=False)

    @pl.when(step == 0)
    def _():
        fetch(lcur_ref, 0)

    @pl.when(step + 1 < n_steps)
    def _():
        fetch(lnext_ref, 1 - slot)

    pltpu.make_async_copy(stage_ref.at[slot], stage_ref.at[slot], sem.at[slot]).wait()

    rw = rw_ref[...]
    for k in range(TOP_K):
        wb_ref[k] = jnp.broadcast_to(rw[:, k:k + 1], (TD, LANES))

    def staged(r, k):
        p = lpos_ref[r * TOP_K + k]
        return stage_ref[slot, pl.ds(pl.multiple_of(p, ROW_TILE), ROW_TILE), :]

    def token(r, c):
        acc = wb_ref[0, pl.ds(r, 1), :] * staged(r, 0)
        for k in range(1, TOP_K):
            acc = acc + wb_ref[k, pl.ds(r, 1), :] * staged(r, k)
        acc_ref[pl.ds(pl.multiple_of(r * ROW_TILE, ROW_TILE), ROW_TILE), :] = acc
        return c
    lax.fori_loop(0, TD, token, 0, unroll=DMA_UNROLL)
    o_ref[0] = x1_ref[0] + mod_ref[0, 5:6, :] * _tiles_to_rows(acc_ref, TD)


def _combine(lpos_flat, lists, ys, x1, rw, mod):
    B, S, D = x1.shape
    nj = S // TD
    n_steps = B * nj
    return pl.pallas_call(
        _combine_kernel,
        out_shape=jax.ShapeDtypeStruct((B, S, D), f32),
        grid=(B, nj),
        in_specs=[pl.BlockSpec((TD * TOP_K,), lambda b, j: (b * nj + j,), memory_space=pltpu.SMEM),
                  pl.BlockSpec((LIST_LEN,), lambda b, j: (b * nj + j,), memory_space=pltpu.SMEM),
                  pl.BlockSpec((LIST_LEN,), lambda b, j: (jnp.minimum(b * nj + j + 1, n_steps - 1),),
                               memory_space=pltpu.SMEM),
                  pl.BlockSpec(memory_space=pl.ANY),
                  pl.BlockSpec((1, TD, D), lambda b, j: (b, j, 0)),
                  pl.BlockSpec((TD, LANES), lambda b, j: (b * nj + j, 0)),
                  pl.BlockSpec((1, 6, D), lambda b, j: (b, 0, 0))],
        out_specs=pl.BlockSpec((1, TD, D), lambda b, j: (b, j, 0)),
        scratch_shapes=[pltpu.VMEM((2, TD * TOP_K * ROW_TILE, LANES), f32),
                        pltpu.VMEM((TD * ROW_TILE, LANES), f32), pltpu.VMEM((TOP_K, TD, LANES), f32),
                        pltpu.SemaphoreType.DMA((2,))],
        compiler_params=pltpu.CompilerParams(
            dimension_semantics=("arbitrary", "arbitrary"), vmem_limit_bytes=VMEM_LIMIT),
        name="combine",
    )(lpos_flat, lists, lists, ys, x1, rw, mod)


def _moe(hp, logits, x1, mod, w_gate_up, b_gate_up, w_down, b_down):
    T = logits.shape[0]
    E = w_gate_up.shape[0]
    ri, rw, cnt, snap = _route(logits)
    rank = ri[:, :TOP_K]
    e_sel = ri[:, TOP_K:2 * TOP_K]
    counts = cnt[0, :E].astype(jnp.int32)
    padded = ((counts + FFN_BLK - 1) // FFN_BLK) * FFN_BLK
    p_ends = jnp.cumsum(padded)
    p_starts = p_ends - padded
    nb = -(-T * TOP_K // FFN_BLK) + E
    n_used = jnp.maximum(p_ends[-1:] // FFN_BLK, 1).astype(jnp.int32)
    blk_start = jnp.arange(nb, dtype=jnp.int32) * FFN_BLK
    block_e = jnp.minimum(jnp.sum(p_ends[None, :] <= blk_start[:, None], axis=1), E - 1).astype(jnp.int32)
    assert E == N_RUN
    base = snap.reshape(T // TD, LANES)[:, :E].astype(jnp.int32)
    run_n = jnp.concatenate([base[1:], counts[None, :]], axis=0) - base
    run_off = jnp.cumsum(run_n, axis=1) - run_n
    run_dst = p_starts[None, :].astype(jnp.int32) + base
    shift = jnp.repeat(run_off - base, TD, axis=0)
    onehot = e_sel[:, :, None] == jnp.arange(E, dtype=jnp.int32)[None, None, :]
    lpos = ((jnp.sum(jnp.where(onehot, shift[:, None, :], 0), axis=-1) + rank) * ROW_TILE).reshape(-1)
    lists = _copy_lists(run_dst, run_n, run_off)
    xs = _dispatch(p_ends.astype(jnp.int32), counts, n_used, lpos, lists, hp, nb * FFN_BLK)
    ys = _ffn(block_e, n_used, xs, w_gate_up, b_gate_up, w_down, b_down)
    return _combine(lpos, lists, ys, x1, rw, mod)


def kernel(x, c, rel_bias_table, w_ada, b_ada, g_norm1, w_in, w_gk_up, b_gk_up, g_gla_out, g_qnorm, g_knorm, lambda_q1, lambda_k1, lambda_q2, lambda_k2, g_subln, w_out, g_norm2, w_router, b_router, w_gate_up, b_gate_up, w_down, b_down):
    B, S, D = x.shape
    depth = w_ada.shape[0]
    bias_tiles = _bias_tiles(rel_bias_table, S, min(TQ, S))
    for l in range(depth):
        lambda_init = 0.8 - 0.6 * math.exp(-0.3 * l)
        mod = _ada(c, w_ada[l], b_ada[l])
        qg, kg, gk, kgt, gkt, vg, rg, qd, kd, vd = _inproj(
            x, mod, g_norm1[l], w_in[l], w_gk_up[l], b_gk_up[l], g_qnorm[l], g_knorm[l])
        og = _gla(qg, kg, gk, kgt, gkt, vg, rg, g_gla_out[l])
        lamv = jnp.stack([lambda_q1[l], lambda_k1[l], lambda_q2[l], lambda_k2[l]]).astype(f32)
        od = lax.cond(_scores_bounded(rel_bias_table, g_qnorm[l], g_knorm[l]),
                      functools.partial(_attn, lambda_init=lambda_init, bounded=True),
                      functools.partial(_attn, lambda_init=lambda_init, bounded=False),
                      qd, kd, vd, bias_tiles, lamv, g_subln[l])
        x1, hp, logits = _outproj(og, od, x, mod, w_out[l], g_norm2[l], w_router[l], b_router[l])
        x = _moe(hp, logits, x1, mod, w_gate_up[l], b_gate_up[l], w_down[l], b_down[l])
    return x
```

```python
import functools
import math

import jax
import jax.numpy as jnp
from jax import lax
from jax.experimental import pallas as pl
from jax.experimental.pallas import tpu as pltpu

f32 = jnp.float32
bf16 = jnp.bfloat16

N_GLA_HEADS = 4
GLA_DK = 64
GLA_DV = 128
GLA_GATE_RANK = 16
GLA_GATE_NORM = 16.0
GLA_CHUNK = 64
N_DIFF_HEADS = 4
DIFF_DQK = 64
DIFF_DV = 128
NUM_BUCKETS = 32
MAX_DISTANCE = 128
TOP_K = 4
SWIGLU_LIMIT = 7.0
SWIGLU_ALPHA = 1.702
EPS = 1e-6

GLA_QK_W = N_GLA_HEADS * GLA_DK
GLA_V_W = N_GLA_HEADS * GLA_DV
DIFF_QK_W = N_DIFF_HEADS * 2 * DIFF_DQK
DIFF_V_W = N_DIFF_HEADS * DIFF_DV

LANES = 128
NEG = -1e30
LOG2E = math.log2(math.e)
SAFE_SCORE = 40.0
NORM_SLACK = 1.02
VMEM_LIMIT = 48 * 1024 * 1024
VMEM_LIMIT_FFN = 58 * 1024 * 1024

TM_IN = 512
INPROJ_SUB = 2
TG_GLA = 1024
PAIR = 2 * GLA_CHUNK
GLA_UNROLL = 4
TQ = 512
ATTN_UNROLL = 4
TR = 512
TD = 512
ROW_TILE = 8
DMA_UNROLL = 8
N_RUN = 32
RUN_SIZES = tuple(TD >> b for b in range(TD.bit_length()))
LIST_DST = len(RUN_SIZES) * N_RUN
LIST_CNT = 2 * LIST_DST
LIST_LEN = 1024
FFN_BLK = 512


def _nt(a, b):
    return lax.dot_general(a, b, (((1,), (1,)), ((), ())), preferred_element_type=f32)


def _mm(a, b):
    return jnp.dot(a, b, preferred_element_type=f32)


def _split(x):
    hi = x.astype(bf16)
    lo = (x - hi.astype(f32)).astype(bf16)
    return hi, lo


def _silu(x):
    return x * jax.nn.sigmoid(x)


def _ada_kernel(c_ref, w_ref, b_ref, o_ref):
    c = c_ref[...]
    o_ref[...] = _mm(_silu(c).astype(bf16), w_ref[...].astype(bf16)) + b_ref[...]


def _ada(c, w_ada, b_ada):
    B, D = c.shape
    N = w_ada.shape[1]
    bp = ROW_TILE
    assert B <= bp
    cp = jnp.zeros((bp, D), f32).at[:B].set(c)
    tn = N // 4
    out = pl.pallas_call(
        _ada_kernel,
        out_shape=jax.ShapeDtypeStruct((bp, N), f32),
        grid=(N // tn,),
        in_specs=[pl.BlockSpec((bp, D), lambda j: (0, 0)),
                  pl.BlockSpec((D, tn), lambda j: (0, j)),
                  pl.BlockSpec((1, tn), lambda j: (0, j))],
        out_specs=pl.BlockSpec((bp, tn), lambda j: (0, j)),
        compiler_params=pltpu.CompilerParams(vmem_limit_bytes=VMEM_LIMIT),
        name="ada",
    )(cp, w_ada, b_ada.reshape(1, N))
    return out[:B].reshape(B, 6, D)


def _inproj_kernel(x_ref, mod_ref, g1_ref, wm_ref, wkt_ref, wlo_ref, wup_ref, wupt_ref,
                   bup_ref, bupt_ref, gqk_ref, grp_ref, grpt_ref,
                   qg_ref, kg_ref, gk_ref, kgt_ref, gkt_ref, vg_ref, rg_ref,
                   qd_ref, kd_ref, vd_ref):
    tm = x_ref.shape[1]
    sub = tm // INPROJ_SUB
    for t in range(INPROJ_SUB):
        _inproj_rows(slice(t * sub, (t + 1) * sub), x_ref, mod_ref, g1_ref, wm_ref, wkt_ref, wlo_ref, wup_ref,
                     wupt_ref, bup_ref, bupt_ref, gqk_ref, grp_ref, grpt_ref, qg_ref, kg_ref, gk_ref, kgt_ref,
                     gkt_ref, vg_ref, rg_ref, qd_ref, kd_ref, vd_ref)


def _inproj_rows(rows, x_ref, mod_ref, g1_ref, wm_ref, wkt_ref, wlo_ref, wup_ref, wupt_ref,
                 bup_ref, bupt_ref, gqk_ref, grp_ref, grpt_ref,
                 qg_ref, kg_ref, gk_ref, kgt_ref, gkt_ref, vg_ref, rg_ref, qd_ref, kd_ref, vd_ref):
    x = x_ref[0, rows, :]
    ms = jnp.mean(x * x, axis=-1, keepdims=True)
    y = x * lax.rsqrt(ms + EPS) * g1_ref[...]
    h = (y * (1.0 + mod_ref[0, 1:2, :]) + mod_ref[0, 0:1, :]).astype(bf16)

    def proj(a, b):
        return _mm(h, wm_ref[:, a:b])

    o = 0
    qg_ref[0, rows, :] = proj(o, o + GLA_QK_W); o += GLA_QK_W
    kg_ref[0, rows, :] = proj(o, o + GLA_QK_W); o += GLA_QK_W
    vg_ref[0, rows, :] = proj(o, o + GLA_V_W).astype(bf16); o += GLA_V_W
    rg_ref[0, rows, :] = proj(o, o + GLA_V_W); o += GLA_V_W
    qk = proj(o, o + 2 * DIFF_QK_W); o += 2 * DIFF_QK_W
    vd_ref[0, rows, :] = proj(o, o + DIFF_V_W).astype(bf16)

    slab0 = rows.start // PAIR
    kgt = _nt(wkt_ref[...], h)
    for j in range(kgt.shape[1] // PAIR):
        kgt_ref[0, slab0 + j] = kgt[:, j * PAIR:(j + 1) * PAIR]

    lo = _mm(h, wlo_ref[...]).astype(bf16)
    z = _mm(lo, wup_ref[...]) + bup_ref[...]
    gk_ref[0, rows, :] = (jnp.minimum(z, 0.0) - jnp.log1p(jnp.exp(-jnp.abs(z)))) * (1.0 / GLA_GATE_NORM)
    zt = _nt(wupt_ref[...], lo) + bupt_ref[...]
    gkt = (jnp.minimum(zt, 0.0) - jnp.log1p(jnp.exp(-jnp.abs(zt)))) * (1.0 / GLA_GATE_NORM)
    for j in range(gkt.shape[1] // PAIR):
        gkt_ref[0, slab0 + j] = gkt[:, j * PAIR:(j + 1) * PAIR]

    sq_hi, sq_lo = _split(qk * qk)
    gs = _mm(sq_hi, grp_ref[...]) + _mm(sq_lo, grp_ref[...])
    r = lax.rsqrt(gs * (1.0 / DIFF_DQK) + EPS)
    r_hi, r_lo = _split(r)
    rb = _mm(r_hi, grpt_ref[...]) + _mm(r_lo, grpt_ref[...])
    qkn = qk * rb * gqk_ref[...]
    qd_ref[0, rows, :] = qkn[:, :DIFF_QK_W].astype(bf16)
    kd_ref[0, rows, :] = qkn[:, DIFF_QK_W:].astype(bf16)


def _inproj(x, mod, g_norm1, w_in, w_gk_up, b_gk_up, g_qnorm, g_knorm):
    B, S, D = x.shape
    offs = [0]
    for w in (GLA_QK_W, GLA_QK_W, GLA_V_W, GLA_V_W, GLA_GATE_RANK, DIFF_QK_W, DIFF_QK_W, DIFF_V_W):
        offs.append(offs[-1] + w)
    w_main = jnp.concatenate([w_in[:, offs[0]:offs[4]], w_in[:, offs[5]:offs[8]]], axis=1).astype(bf16)
    w_kt = w_in[:, offs[1]:offs[2]].T.astype(bf16)
    w_lo = jnp.zeros((D, LANES), f32).at[:, :GLA_GATE_RANK].set(w_in[:, offs[4]:offs[5]]).astype(bf16)
    w_up = jnp.zeros((LANES, GLA_QK_W), f32).at[:GLA_GATE_RANK].set(w_gk_up).astype(bf16)
    w_upt = w_up.T
    b_up = b_gk_up.reshape(1, GLA_QK_W)
    b_upt = b_gk_up.reshape(GLA_QK_W, 1)
    n_grp = 2 * DIFF_QK_W // DIFF_DQK
    gqk = jnp.concatenate([jnp.tile(g_qnorm, n_grp // 2) * (DIFF_DQK ** -0.5 * LOG2E),
                           jnp.tile(g_knorm, n_grp // 2)]).reshape(1, 2 * DIFF_QK_W)
    grp = (jnp.arange(2 * DIFF_QK_W)[:, None] // DIFF_DQK == jnp.arange(LANES)[None, :]).astype(bf16)
    grpt = grp.T
    nw = w_main.shape[1]
    tm = TM_IN
    const = lambda shape: pl.BlockSpec(shape, lambda b, i: (0,) * len(shape))
    row = lambda w: pl.BlockSpec((1, tm, w), lambda b, i: (b, i, 0))
    colT = pl.BlockSpec((1, tm // PAIR, GLA_QK_W, PAIR), lambda b, i: (b, i, 0, 0))
    outs = pl.pallas_call(
        _inproj_kernel,
        out_shape=[jax.ShapeDtypeStruct((B, S, GLA_QK_W), f32),
                   jax.ShapeDtypeStruct((B, S, GLA_QK_W), f32),
                   jax.ShapeDtypeStruct((B, S, GLA_QK_W), f32),
                   jax.ShapeDtypeStruct((B, S // PAIR, GLA_QK_W, PAIR), f32),
                   jax.ShapeDtypeStruct((B, S // PAIR, GLA_QK_W, PAIR), f32),
                   jax.ShapeDtypeStruct((B, S, GLA_V_W), bf16),
                   jax.ShapeDtypeStruct((B, S, GLA_V_W), f32),
                   jax.ShapeDtypeStruct((B, S, DIFF_QK_W), bf16),
                   jax.ShapeDtypeStruct((B, S, DIFF_QK_W), bf16),
                   jax.ShapeDtypeStruct((B, S, DIFF_V_W), bf16)],
        grid=(B, S // tm),
        in_specs=[row(D),
                  pl.BlockSpec((1, 6, D), lambda b, i: (b, 0, 0)),
                  const((1, D)), const((D, nw)), const((GLA_QK_W, D)), const((D, LANES)),
                  const((LANES, GLA_QK_W)), const((GLA_QK_W, LANES)),
                  const((1, GLA_QK_W)), const((GLA_QK_W, 1)),
                  const((1, 2 * DIFF_QK_W)), const((2 * DIFF_QK_W, LANES)),
                  const((LANES, 2 * DIFF_QK_W))],
        out_specs=[row(GLA_QK_W), row(GLA_QK_W), row(GLA_QK_W), colT, colT,
                   row(GLA_V_W), row(GLA_V_W), row(DIFF_QK_W), row(DIFF_QK_W), row(DIFF_V_W)],
        compiler_params=pltpu.CompilerParams(
            dimension_semantics=("arbitrary", "arbitrary"), vmem_limit_bytes=VMEM_LIMIT),
        name="inproj",
    )(x, mod, g_norm1.reshape(1, D), w_main, w_kt, w_lo, w_up, w_upt, b_up, b_upt, gqk, grp, grpt)
    return outs


def _gla_kernel(q_ref, k_ref, g_ref, kt_ref, gt_ref, v_ref, r_ref, gout_ref, tri_ref, trit_ref,
                o_ref, s_ref, *, n_pairs):
    H, DK, DV = N_GLA_HEADS, GLA_DK, GLA_DV

    @pl.when(pl.program_id(1) == 0)
    def _():
        s_ref[...] = jnp.zeros_like(s_ref)

    tri = tri_ref[...]
    trit = trit_ref[...]
    tri_b = tri > 0
    lane_head = lax.broadcasted_iota(jnp.int32, (1, H * DK), 1) // DK
    row_head = lax.broadcasted_iota(jnp.int32, (H * PAIR, 1), 0) // PAIR
    qmask = row_head == lane_head
    row_first = lax.broadcasted_iota(jnp.int32, (PAIR, 1), 0) < GLA_CHUNK
    row_first4 = (lax.broadcasted_iota(jnp.int32, (H * PAIR, 1), 0) % PAIR) < GLA_CHUNK
    lane_first = lax.broadcasted_iota(jnp.int32, (1, PAIR), 1) < GLA_CHUNK
    scale = DK ** -0.5
    gout = gout_ref[...]

    def pair(p, carry):
        r0 = pl.multiple_of(p * PAIR, PAIR)
        q = q_ref[0, pl.ds(r0, PAIR), :]
        k = k_ref[0, pl.ds(r0, PAIR), :]
        g = g_ref[0, pl.ds(r0, PAIR), :]
        kt = kt_ref[0, p]
        gt = gt_ref[0, p]
        v = v_ref[0, pl.ds(r0, PAIR), :]

        g_hi, g_lo = _split(g)
        gc = _mm(tri, g_hi) + _mm(tri, g_lo)
        gt_hi, gt_lo = _split(gt)
        gct = _mm(gt_hi, trit) + _mm(gt_lo, trit)
        g_last = jnp.where(row_first, gc[GLA_CHUNK - 1:GLA_CHUNK, :], gc[PAIR - 1:PAIR, :])
        gl0 = gct[:, GLA_CHUNK - 1:GLA_CHUNK]
        gl1 = gct[:, PAIR - 1:PAIR]
        g_last_t = jnp.where(lane_first, gl0, gl1)

        q_e = (q * (jnp.exp(gc) * scale)).astype(bf16)
        k_e = (k * jnp.exp(-gc)).astype(bf16)
        ks_t = kt * jnp.exp(g_last_t - gct)
        ks_t0 = jnp.where(lane_first, ks_t, 0.0).astype(bf16)
        ks_t1 = jnp.where(lane_first, 0.0, ks_t).astype(bf16)
        del g_last

        qm = jnp.where(qmask, jnp.concatenate([q_e] * H, axis=0), jnp.zeros((), bf16))
        a = _nt(qm, k_e)
        s0 = s_ref[...]

        u0 = []
        u1 = []
        for h in range(H):
            v_h = v[:, h * DV:(h + 1) * DV]
            u0.append(_mm(ks_t0[h * DK:(h + 1) * DK], v_h))
            u1.append(_mm(ks_t1[h * DK:(h + 1) * DK], v_h))
        u0 = jnp.concatenate(u0, axis=0)
        u1 = jnp.concatenate(u1, axis=0)
        s1 = s0 * jnp.exp(gl0) + u0
        s_ref[...] = s1 * jnp.exp(gl1) + u1

        o_inter = jnp.where(row_first4, _mm(qm, s0.astype(bf16)), _mm(qm, s1.astype(bf16)))
        for h in range(H):
            a_h = jnp.where(tri_b, a[h * PAIR:(h + 1) * PAIR], 0.0).astype(bf16)
            o_h = _mm(a_h, v[:, h * DV:(h + 1) * DV]) + o_inter[h * PAIR:(h + 1) * PAIR]
            ms = jnp.mean(o_h * o_h, axis=-1, keepdims=True)
            o_n = o_h * lax.rsqrt(ms + EPS) * gout
            r_h = r_ref[0, pl.ds(r0, PAIR), h * DV:(h + 1) * DV]
            o_ref[0, pl.ds(r0, PAIR), h * DV:(h + 1) * DV] = (o_n * _silu(r_h)).astype(bf16)
        return carry

    lax.fori_loop(0, n_pairs, pair, 0, unroll=GLA_UNROLL)


def _gla(qg, kg, gk, kgt, gkt, vg, rg, g_gla_out):
    B, S, _ = qg.shape
    tg = min(TG_GLA, S)
    r = jnp.arange(PAIR)
    tri = ((r[:, None] // GLA_CHUNK == r[None, :] // GLA_CHUNK) & (r[None, :] <= r[:, None])).astype(bf16)
    row = lambda w: pl.BlockSpec((1, tg, w), lambda b, i: (b, i, 0))
    colT = pl.BlockSpec((1, tg // PAIR, GLA_QK_W, PAIR), lambda b, i: (b, i, 0, 0))
    const = lambda shape: pl.BlockSpec(shape, lambda b, i: (0,) * len(shape))
    return pl.pallas_call(
        functools.partial(_gla_kernel, n_pairs=tg // PAIR),
        out_shape=jax.ShapeDtypeStruct((B, S, GLA_V_W), bf16),
        grid=(B, S // tg),
        in_specs=[row(GLA_QK_W), row(GLA_QK_W), row(GLA_QK_W), colT, colT,
                  row(GLA_V_W), row(GLA_V_W), const((1, GLA_DV)),
                  const((PAIR, PAIR)), const((PAIR, PAIR))],
        out_specs=row(GLA_V_W),
        scratch_shapes=[pltpu.VMEM((GLA_QK_W, GLA_DV), f32)],
        compiler_params=pltpu.CompilerParams(
            dimension_semantics=("arbitrary", "arbitrary"), vmem_limit_bytes=VMEM_LIMIT),
        name="gla",
    )(qg, kg, gk, kgt, gkt, vg, rg, g_gla_out.reshape(1, GLA_DV), tri, tri.T)


def _attn_finish(o, gsub_ref, o_ref, lambda_init):
    ms = jnp.mean(o * o, axis=-1, keepdims=True)
    o_ref[0] = (o * lax.rsqrt(ms + EPS) * gsub_ref[...] * (1.0 - lambda_init)).astype(bf16)


def _attn_lambda(lamv_ref, lambda_init):
    lv = lamv_ref[...]
    return (jnp.exp(jnp.sum(lv[0:1] * lv[1:2], axis=-1, keepdims=True))
            - jnp.exp(jnp.sum(lv[2:3] * lv[3:4], axis=-1, keepdims=True)) + lambda_init)


def _attn_bounded_kernel(q_ref, k_ref, v_ref, bias_ref, lamv_ref, gsub_ref, o_ref, vaug_ref, *, lambda_init):
    qi = pl.program_id(2)
    tq = q_ref.shape[1]
    S = k_ref.shape[1]

    @pl.when(qi == 0)
    def _():
        lane = lax.broadcasted_iota(jnp.int32, (S, DIFF_DV), 1)
        vaug_ref[:, :DIFF_DV] = v_ref[0]
        vaug_ref[:, DIFF_DV:] = jnp.where(lane == 0, 1.0, 0.0).astype(bf16)

    q = q_ref[0]
    lane = lax.broadcasted_iota(jnp.int32, (1, 2 * DIFF_DQK), 1)
    zero = jnp.zeros((), bf16)
    qs = (jnp.where(lane < DIFF_DQK, q, zero), jnp.where(lane < DIFF_DQK, zero, q))

    def update(accs, k0, bias):
        kb = k_ref[0, pl.ds(k0, tq), :]
        vb = vaug_ref[pl.ds(k0, tq), :]
        out = []
        for c in range(2):
            s = _nt(qs[c], kb)
            if bias is not None:
                s = s + bias[c]
            out.append(accs[c] + _mm(jnp.exp2(s).astype(bf16), vb))
        return tuple(out)

    def far(kj, accs):
        return update(accs, pl.multiple_of(kj * tq, tq), None)

    def far_group(g, accs):
        for u in range(ATTN_UNROLL):
            accs = far(g * ATTN_UNROLL + u, accs)
        return accs

    def block_or_masked(accs, kj, bias):
        exists = kj >= 0
        k0 = pl.multiple_of(jnp.maximum(kj, 0) * tq, tq)
        if bias is None:
            tiles = (jnp.where(exists, 0.0, NEG),) * 2
        else:
            tiles = tuple(jnp.where(exists, b, NEG) for b in bias)
        return update(accs, k0, tiles)

    accs = (jnp.zeros((tq, 2 * DIFF_DV), f32), jnp.zeros((tq, 2 * DIFF_DV), f32))
    accs = update(accs, pl.multiple_of(qi * tq, tq), (bias_ref[0, 0, 1], bias_ref[0, 1, 1]))
    accs = block_or_masked(accs, qi - 1, (bias_ref[0, 0, 0], bias_ref[0, 1, 0]))
    for u in range(2, ATTN_UNROLL):
        accs = block_or_masked(accs, qi - u, None)
    n_far = jnp.maximum(qi + 1 - ATTN_UNROLL, 0)
    n_grp = n_far // ATTN_UNROLL
    accs = lax.fori_loop(0, n_grp, far_group, accs)
    accs = lax.fori_loop(n_grp * ATTN_UNROLL, n_far, far, accs)
    a0, a1 = accs
    o = (a0[:, :DIFF_DV] / a0[:, DIFF_DV:DIFF_DV + 1]
         - _attn_lambda(lamv_ref, lambda_init) * (a1[:, :DIFF_DV] / a1[:, DIFF_DV:DIFF_DV + 1]))
    _attn_finish(o, gsub_ref, o_ref, lambda_init)


def _attn_kernel(q_ref, k_ref, v_ref, bias_ref, lamv_ref, gsub_ref, o_ref, *, lambda_init):
    qi = pl.program_id(2)
    tq = q_ref.shape[1]
    q = q_ref[0]
    lane = lax.broadcasted_iota(jnp.int32, (1, 2 * DIFF_DQK), 1)
    zero = jnp.zeros((), bf16)
    qs = (jnp.where(lane < DIFF_DQK, q, zero), jnp.where(lane < DIFF_DQK, zero, q))

    def update(state, kb, vb, bias):
        new = []
        for c in range(2):
            m, l, acc = state[c]
            s = _nt(qs[c], kb)
            if bias is not None:
                s = s + bias[c]
            m_new = jnp.maximum(m, jnp.max(s, axis=-1, keepdims=True))
            alpha = jnp.exp2(m - m_new)
            p = jnp.exp2(s - m_new)
            l = alpha * l + jnp.sum(p, axis=-1, keepdims=True)
            acc = alpha * acc + _mm(p.astype(bf16), vb)
            new.append((m_new, l, acc))
        return tuple(new)

    init = tuple((jnp.full((tq, 1), NEG, f32), jnp.zeros((tq, 1), f32), jnp.zeros((tq, DIFF_DV), f32))
                 for _ in range(2))

    def far(kj, state):
        k0 = pl.multiple_of(kj * tq, tq)
        return update(state, k_ref[0, pl.ds(k0, tq), :], v_ref[0, pl.ds(k0, tq), :], None)

    state = lax.fori_loop(0, jnp.maximum(qi - 1, 0), far, init)

    kd0 = pl.multiple_of(qi * tq, tq)
    state = update(state, k_ref[0, pl.ds(kd0, tq), :], v_ref[0, pl.ds(kd0, tq), :],
                   (bias_ref[0, 0, 1], bias_ref[0, 1, 1]))
    kp0 = pl.multiple_of(jnp.maximum(qi - 1, 0) * tq, tq)
    has_prev = qi > 0
    state = update(state, k_ref[0, pl.ds(kp0, tq), :], v_ref[0, pl.ds(kp0, tq), :],
                   (jnp.where(has_prev, bias_ref[0, 0, 0], NEG), jnp.where(has_prev, bias_ref[0, 1, 0], NEG)))

    (_, l0, a0), (_, l1, a1) = state
    o = a0 / l0 - _attn_lambda(lamv_ref, lambda_init) * (a1 / l1)
    _attn_finish(o, gsub_ref, o_ref, lambda_init)


def _t5_bucket(n):
    max_exact = NUM_BUCKETS // 2
    nf = jnp.maximum(n, 1).astype(f32)
    large = max_exact + (jnp.log(nf / max_exact) / math.log(MAX_DISTANCE / max_exact)
                         * (NUM_BUCKETS - max_exact)).astype(jnp.int32)
    large = jnp.minimum(large, NUM_BUCKETS - 1)
    return jnp.where(n < max_exact, n, large)


def _toeplitz_kernel(w_ref, o_ref):
    n = o_ref.shape[-1]
    for t in range(2):
        rows = jnp.broadcast_to(w_ref[0, t:t + 1, :], (n, 2 * n))
        o_ref[0, 0, t] = pltpu.roll(rows, 0, 1, stride=1, stride_axis=0)[:, n:]


def _bias_tiles(rel_bias_table, S, n):
    HM = rel_bias_table.shape[1]
    assert n >= MAX_DISTANCE
    d = jnp.arange(2 * n, dtype=jnp.int32)
    by_dist = rel_bias_table[_t5_bucket(d)].astype(f32).T
    rel = (by_dist - rel_bias_table[NUM_BUCKETS - 1].astype(f32)[:, None]) * LOG2E
    i = jnp.arange(2 * n)
    w_diag = jnp.where(i[None, :] <= n, rel[:, jnp.clip(n - i, 0, 2 * n - 1)], NEG)
    w_prev = rel[:, jnp.clip(2 * n - i, 0, 2 * n - 1)]
    w = jnp.stack([w_prev, w_diag], axis=1)
    return pl.pallas_call(
        _toeplitz_kernel,
        out_shape=jax.ShapeDtypeStruct((HM // 2, 2, 2, n, n), f32),
        grid=(HM // 2, 2),
        in_specs=[pl.BlockSpec((1, 2, 2 * n), lambda h, m: (h * 2 + m, 0, 0))],
        out_specs=pl.BlockSpec((1, 1, 2, n, n), lambda h, m: (h, m, 0, 0, 0)),
        compiler_params=pltpu.CompilerParams(vmem_limit_bytes=VMEM_LIMIT),
        name="bias_tiles",
    )(w)


def _attn(qd, kd, vd, bias_tiles, lamv, g_subln, lambda_init, bounded):
    B, S, _ = qd.shape
    H = N_DIFF_HEADS
    tq = min(TQ, S)
    body = _attn_bounded_kernel if bounded else _attn_kernel
    scratch = [pltpu.VMEM((S, 2 * DIFF_DV), bf16)] if bounded else []
    return pl.pallas_call(
        functools.partial(body, lambda_init=lambda_init),
        out_shape=jax.ShapeDtypeStruct((B, S, DIFF_V_W), bf16),
        scratch_shapes=scratch,
        grid=(B, H, S // tq),
        in_specs=[pl.BlockSpec((1, tq, 2 * DIFF_DQK), lambda b, h, i: (b, i, h)),
                  pl.BlockSpec((1, S, 2 * DIFF_DQK), lambda b, h, i: (b, 0, h)),
                  pl.BlockSpec((1, S, DIFF_DV), lambda b, h, i: (b, 0, h)),
                  pl.BlockSpec((1, 2, 2, tq, tq), lambda b, h, i: (h, 0, 0, 0, 0)),
                  pl.BlockSpec((4, DIFF_DQK), lambda b, h, i: (0, 0)),
                  pl.BlockSpec((1, DIFF_DV), lambda b, h, i: (0, 0))],
        out_specs=pl.BlockSpec((1, tq, DIFF_DV), lambda b, h, i: (b, i, h)),
        compiler_params=pltpu.CompilerParams(
            dimension_semantics=("arbitrary", "arbitrary", "arbitrary"), vmem_limit_bytes=VMEM_LIMIT),
        name="attn_bounded" if bounded else "attn_online",
    )(qd, kd, vd, bias_tiles, lamv, g_subln.reshape(1, DIFF_DV))


def _scores_bounded(rel_bias_table, g_qnorm, g_knorm):
    qk = DIFF_DQK ** 0.5 * jnp.max(jnp.abs(g_qnorm)) * jnp.max(jnp.abs(g_knorm)) * NORM_SLACK
    rel = jnp.max(jnp.abs(rel_bias_table - rel_bias_table[NUM_BUCKETS - 1:]))
    return qk + rel <= SAFE_SCORE


def _rows_to_tiles(x, ref):
    n = x.shape[0]
    for c in range(ROW_TILE):
        ref[pl.ds(c, n, stride=ROW_TILE), :] = x[:, c * LANES:(c + 1) * LANES]


def _tiles_to_rows(ref, n):
    return jnp.concatenate([ref[pl.ds(c, n, stride=ROW_TILE), :] for c in range(ROW_TILE)], axis=1)


def _outproj_kernel(og_ref, od_ref, x_ref, mod_ref, wo_ref, g2_ref, wr_ref, br_ref,
                    x1_ref, hp_ref, lg_ref):
    half = og_ref.shape[2]
    sub = og_ref.shape[1] // INPROJ_SUB
    for t in range(INPROJ_SUB):
        rows = slice(t * sub, (t + 1) * sub)
        mix = _mm(og_ref[0, rows, :], wo_ref[:half, :]) + _mm(od_ref[0, rows, :], wo_ref[half:, :])
        x1 = x_ref[0, rows, :] + mod_ref[0, 2:3, :] * mix
        x1_ref[0, rows, :] = x1
        ms = jnp.mean(x1 * x1, axis=-1, keepdims=True)
        y = x1 * lax.rsqrt(ms + EPS) * g2_ref[...]
        h = (y * (1.0 + mod_ref[0, 4:5, :]) + mod_ref[0, 3:4, :]).astype(bf16)
        lg_ref[rows, :] = _mm(h, wr_ref[...]) + br_ref[...]
        _rows_to_tiles(h.astype(f32), hp_ref.at[pl.ds(t * sub * ROW_TILE, sub * ROW_TILE)])


def _outproj(og, od, x, mod, w_out, g_norm2, w_router, b_router):
    B, S, D = x.shape
    assert D == ROW_TILE * LANES, "the token-tile layout needs a model row to fill one (8,128) tile"
    E = w_router.shape[1]
    tm = TM_IN
    nj = S // tm
    w_r = jnp.zeros((D, LANES), f32).at[:, :E].set(w_router).astype(bf16)
    b_r = jnp.full((1, LANES), NEG, f32).at[0, :E].set(b_router)
    const = lambda shape: pl.BlockSpec(shape, lambda b, i: (0,) * len(shape))
    return pl.pallas_call(
        _outproj_kernel,
        out_shape=[jax.ShapeDtypeStruct((B, S, D), f32),
                   jax.ShapeDtypeStruct((B * S * ROW_TILE, LANES), f32),
                   jax.ShapeDtypeStruct((B * S, LANES), f32)],
        grid=(B, nj),
        in_specs=[pl.BlockSpec((1, tm, og.shape[2]), lambda b, i: (b, i, 0)),
                  pl.BlockSpec((1, tm, od.shape[2]), lambda b, i: (b, i, 0)),
                  pl.BlockSpec((1, tm, D), lambda b, i: (b, i, 0)),
                  pl.BlockSpec((1, 6, D), lambda b, i: (b, 0, 0)),
                  const((w_out.shape[0], D)), const((1, D)), const((D, LANES)), const((1, LANES))],
        out_specs=[pl.BlockSpec((1, tm, D), lambda b, i: (b, i, 0)),
                   pl.BlockSpec((tm * ROW_TILE, LANES), lambda b, i: (b * nj + i, 0)),
                   pl.BlockSpec((tm, LANES), lambda b, i: (b * nj + i, 0))],
        compiler_params=pltpu.CompilerParams(
            dimension_semantics=("arbitrary", "arbitrary"), vmem_limit_bytes=VMEM_LIMIT),
        name="outproj",
    )(og, od, x, mod, w_out.astype(bf16), g_norm2.reshape(1, D), w_r, b_r)


def _route_kernel(lg_ref, lt_ref, ri_ref, rw_ref, cnt_ref, snap_ref, run_ref):
    @pl.when(pl.program_id(0) == 0)
    def _():
        run_ref[...] = jnp.zeros_like(run_ref)

    x = lg_ref[...]
    tr = x.shape[0]
    lane = lax.broadcasted_iota(jnp.int32, (tr, LANES), 1)
    lane_f = lane.astype(f32)
    vals, hots, idxs = [], [], []
    for _ in range(TOP_K):
        m = jnp.max(x, axis=-1, keepdims=True)
        idx = jnp.min(jnp.where(x == m, lane_f, float(LANES)), axis=-1, keepdims=True)
        hot = lane_f == idx
        x = jnp.where(hot, -jnp.inf, x)
        vals.append(m)
        hots.append(hot)
        idxs.append(idx.astype(jnp.int32))
    ex = [jnp.exp(v - vals[0]) for v in vals]
    den = ex[0] + ex[1] + ex[2] + ex[3]
    sel = (hots[0] | hots[1] | hots[2] | hots[3]).astype(f32)
    rank = _mm(lt_ref[...], sel.astype(bf16)) + run_ref[...]
    run_ref[...] = run_ref[...] + jnp.sum(sel, axis=0, keepdims=True)
    cnt_ref[...] = run_ref[...]
    for t in range(tr // TD):
        snap_ref[0, t:t + 1, :] = rank[t * TD:t * TD + 1, :]
    ri = jnp.zeros((tr, LANES), jnp.int32)
    rw = jnp.zeros((tr, LANES), f32)
    for k in range(TOP_K):
        rk = jnp.sum(jnp.where(hots[k], rank, 0.0), axis=-1, keepdims=True).astype(jnp.int32)
        ri = jnp.where(lane == k, rk, ri)
        ri = jnp.where(lane == TOP_K + k, idxs[k], ri)
        rw = jnp.where(lane == k, ex[k] / den, rw)
    ri_ref[...] = ri
    rw_ref[...] = rw


def _route(logits):
    T = logits.shape[0]
    tr = min(TR, T)
    r = jnp.arange(tr)
    lt = (r[None, :] < r[:, None]).astype(bf16)
    return pl.pallas_call(
        _route_kernel,
        out_shape=[jax.ShapeDtypeStruct((T, LANES), jnp.int32),
                   jax.ShapeDtypeStruct((T, LANES), f32),
                   jax.ShapeDtypeStruct((1, LANES), f32),
                   jax.ShapeDtypeStruct((T // tr, tr // TD, LANES), f32)],
        grid=(T // tr,),
        in_specs=[pl.BlockSpec((tr, LANES), lambda i: (i, 0)),
                  pl.BlockSpec((tr, tr), lambda i: (0, 0))],
        out_specs=[pl.BlockSpec((tr, LANES), lambda i: (i, 0)),
                   pl.BlockSpec((tr, LANES), lambda i: (i, 0)),
                   pl.BlockSpec((1, LANES), lambda i: (0, 0)),
                   pl.BlockSpec((1, tr // TD, LANES), lambda i: (i, 0, 0))],
        scratch_shapes=[pltpu.VMEM((1, LANES), f32)],
        compiler_params=pltpu.CompilerParams(dimension_semantics=("arbitrary",)),
        name="route",
    )(logits, lt)


def _run_copies(list_ref, hbm_ref, stage_ref, sem, to_hbm):
    for c, size in enumerate(RUN_SIZES):
        def one(i, carry, c=c, size=size):
            s0 = list_ref[c * N_RUN + i]
            d0 = list_ref[LIST_DST + c * N_RUN + i]
            stage = stage_ref.at[pl.ds(pl.multiple_of(s0, ROW_TILE), size * ROW_TILE)]
            rows = hbm_ref.at[pl.ds(pl.multiple_of(d0, ROW_TILE), size * ROW_TILE)]
            src, dst = (stage, rows) if to_hbm else (rows, stage)
            pltpu.make_async_copy(src, dst, sem).start(priority=c % 2)
            return carry
        lax.fori_loop(0, list_ref[LIST_CNT + c], one, 0)


def _copy_lists(run_dst, run_n, run_off):
    n_tiles = run_n.shape[0]
    size = jnp.asarray(RUN_SIZES, jnp.int32)[None, :, None]
    n = run_n[:, None, :]
    bit = (n & size) != 0
    before = n & ~(2 * size - 1)
    place = jnp.cumsum(bit, axis=-1) - 1
    pick = bit[:, :, None, :] & (place[:, :, None, :] == jnp.arange(N_RUN)[None, None, :, None])

    def compact(v):
        return jnp.sum(jnp.where(pick, v[:, :, None, :], 0), axis=-1) * ROW_TILE
    src = compact(run_off[:, None, :] + before)
    dst = compact(run_dst[:, None, :] + before)
    cnt = jnp.sum(bit, axis=-1).astype(jnp.int32)
    pad = jnp.zeros((n_tiles, LIST_LEN - LIST_CNT - len(RUN_SIZES)), jnp.int32)
    lists = jnp.concatenate([src.reshape(n_tiles, -1), dst.reshape(n_tiles, -1), cnt, pad], axis=1)
    return lists.reshape(-1).astype(jnp.int32)


def _dispatch_kernel(pend_ref, cnt_ref, nu_ref, lpos_ref, list_ref, h_ref, xs_ref,
                     zero_ref, stage_ref, sem, zsem):
    n_tok = h_ref.shape[0] // ROW_TILE
    blk_rows = FFN_BLK * ROW_TILE

    @pl.when(pl.program_id(0) == 0)
    def _():
        zero_ref[...] = jnp.zeros_like(zero_ref)
        n_exp = pend_ref.shape[0]

        def last_block(e):
            return xs_ref.at[pl.ds(pl.multiple_of((pend_ref[e] - FFN_BLK) * ROW_TILE, blk_rows), blk_rows)]

        def zfill(e, c):
            @pl.when(cnt_ref[e] > 0)
            def _():
                pltpu.make_async_copy(zero_ref, last_block(e), zsem).start()
            return c

        def zwait(e, c):
            @pl.when(cnt_ref[e] > 0)
            def _():
                pltpu.make_async_copy(zero_ref, last_block(e), zsem).wait()
            return c

        lax.fori_loop(0, n_exp, zfill, 0)
        lax.fori_loop(0, n_exp, zwait, 0)

        def tail_block(i):
            return xs_ref.at[pl.ds(pl.multiple_of(i * blk_rows, blk_rows), blk_rows)]

        def tfill(i, c):
            pltpu.make_async_copy(zero_ref, tail_block(i), zsem).start()
            return c

        def twait(i, c):
            pltpu.make_async_copy(zero_ref, tail_block(i), zsem).wait()
            return c

        n_blk = xs_ref.shape[0] // blk_rows
        lax.fori_loop(nu_ref[0], n_blk, tfill, 0)
        lax.fori_loop(nu_ref[0], n_blk, twait, 0)

    step = pl.program_id(0)
    slot = step % 2
    stage = stage_ref.at[slot]

    def place(g, c):
        for u in range(DMA_UNROLL):
            r = g * DMA_UNROLL + u
            row = h_ref[pl.ds(pl.multiple_of(r * ROW_TILE, ROW_TILE), ROW_TILE), :]
            for k in range(TOP_K):
                p = lpos_ref[r * TOP_K + k]
                stage[pl.ds(pl.multiple_of(p, ROW_TILE), ROW_TILE), :] = row
        return c
    lax.fori_loop(0, n_tok // DMA_UNROLL, place, 0)

    _run_copies(list_ref, xs_ref, stage, sem.at[slot], to_hbm=True)

    def drain(s):
        pltpu.make_async_copy(stage_ref.at[s], stage_ref.at[s], sem.at[s]).wait()

    @pl.when(step > 0)
    def _():
        drain(1 - slot)

    @pl.when(step == pl.num_programs(0) - 1)
    def _():
        drain(slot)


def _dispatch(p_ends, counts, n_used, lpos_flat, lists, hp, n_rows):
    T = hp.shape[0] // ROW_TILE
    grid_spec = pltpu.PrefetchScalarGridSpec(
        num_scalar_prefetch=3,
        grid=(T // TD,),
        in_specs=[pl.BlockSpec((TD * TOP_K,), lambda i, *_: (i,), memory_space=pltpu.SMEM),
                  pl.BlockSpec((LIST_LEN,), lambda i, *_: (i,), memory_space=pltpu.SMEM),
                  pl.BlockSpec((TD * ROW_TILE, LANES), lambda i, *_: (i, 0))],
        out_specs=pl.BlockSpec(memory_space=pl.ANY),
        scratch_shapes=[pltpu.VMEM((FFN_BLK * ROW_TILE, LANES), f32),
                        pltpu.VMEM((2, TD * TOP_K * ROW_TILE, LANES), f32),
                        pltpu.SemaphoreType.DMA((2,)), pltpu.SemaphoreType.DMA(())],
    )
    return pl.pallas_call(
        _dispatch_kernel,
        out_shape=jax.ShapeDtypeStruct((n_rows * ROW_TILE, LANES), f32),
        grid_spec=grid_spec,
        compiler_params=pltpu.CompilerParams(dimension_semantics=("arbitrary",), vmem_limit_bytes=VMEM_LIMIT),
        name="dispatch",
    )(p_ends, counts, n_used, lpos_flat, lists, hp)


def _ffn_kernel(be_ref, nu_ref, nx_ref, par_ref, xs_ref, wgu_hbm, bgu_ref, wd_hbm, bd_ref, ys_ref,
                wgu32_ref, wd32_ref, wgu_ref, wd_ref, sem):
    i = pl.program_id(0)
    used = i < nu_ref[0]
    new_expert = (i == 0) | (be_ref[i] != be_ref[jnp.maximum(i - 1, 0)])
    slot = par_ref[i]

    def weight_copies(e, s):
        return (pltpu.make_async_copy(wgu_hbm.at[e], wgu32_ref.at[s], sem.at[0, s]),
                pltpu.make_async_copy(wd_hbm.at[e], wd32_ref.at[s], sem.at[1, s]))

    @pl.when(i == 0)
    def _():
        for cp in weight_copies(be_ref[0], 0):
            cp.start()

    @pl.when(used & new_expert)
    def _():
        for cp in weight_copies(be_ref[i], slot):
            cp.wait()

        @pl.when(nx_ref[i] >= 0)
        def _():
            for cp in weight_copies(nx_ref[i], 1 - slot):
                cp.start()

        rows = 128

        def cast(src, dst):
            def body(r, c):
                r0 = pl.multiple_of(r * rows, rows)
                dst[pl.ds(r0, rows), :] = src[slot, pl.ds(r0, rows), :].astype(bf16)
                return c
            lax.fori_loop(0, src.shape[1] // rows, body, 0)
        cast(wgu32_ref, wgu_ref)
        cast(wd32_ref, wd_ref)

    @pl.when(used)
    def _():
        F = wd_ref.shape[0]
        xrow = _tiles_to_rows(xs_ref, FFN_BLK).astype(bf16)
        acc = None
        fc = F // 2
        for c in range(2):
            def gu(col0):
                return _mm(xrow, wgu_ref[:, col0:col0 + fc]) + bgu_ref[0, :, col0:col0 + fc]
            gate = jnp.minimum(gu(c * fc), SWIGLU_LIMIT)
            up = jnp.clip(gu(F + c * fc), -SWIGLU_LIMIT, SWIGLU_LIMIT)
            y = (up + 1.0) * (gate * jax.nn.sigmoid(SWIGLU_ALPHA * gate))
            part = _mm(y.astype(bf16), wd_ref[c * fc:(c + 1) * fc, :])
            acc = part if acc is None else acc + part
        _rows_to_tiles(acc + bd_ref[0], ys_ref)

    @pl.when(jnp.logical_not(used))
    def _():
        ys_ref[...] = jnp.zeros_like(ys_ref)


def _ffn(block_e, n_used, xs, w_gate_up, b_gate_up, w_down, b_down):
    E, D, F2 = w_gate_up.shape
    F = F2 // 2
    P = xs.shape[0] // ROW_TILE
    nb = P // FFN_BLK
    rows = FFN_BLK * ROW_TILE

    idx = jnp.arange(nb, dtype=jnp.int32)
    live = idx < n_used[0]
    later_other = (block_e[None, :] != block_e[:, None]) & (idx[None, :] > idx[:, None]) & live[None, :]
    nxt = jnp.where(jnp.any(later_other, axis=1), block_e[jnp.argmax(later_other, axis=1)], -1).astype(jnp.int32)
    starts = jnp.concatenate([jnp.ones((1,), jnp.int32), (block_e[1:] != block_e[:-1]).astype(jnp.int32)])
    parity = ((jnp.cumsum(starts) - 1) % 2).astype(jnp.int32)

    def blk(i, nu):
        return jnp.minimum(i, nu[0] - 1)

    grid_spec = pltpu.PrefetchScalarGridSpec(
        num_scalar_prefetch=4,
        grid=(nb,),
        in_specs=[pl.BlockSpec((rows, LANES), lambda i, be, nu, nx, pa: (blk(i, nu), 0)),
                  pl.BlockSpec(memory_space=pl.ANY),
                  pl.BlockSpec((1, 1, F2), lambda i, be, nu, nx, pa: (be[blk(i, nu)], 0, 0)),
                  pl.BlockSpec(memory_space=pl.ANY),
                  pl.BlockSpec((1, 1, D), lambda i, be, nu, nx, pa: (be[blk(i, nu)], 0, 0))],
        out_specs=pl.BlockSpec((rows, LANES), lambda i, be, nu, nx, pa: (i, 0)),
        scratch_shapes=[pltpu.VMEM((2, D, F2), f32), pltpu.VMEM((2, F, D), f32),
                        pltpu.VMEM((D, F2), bf16), pltpu.VMEM((F, D), bf16),
                        pltpu.SemaphoreType.DMA((2, 2))],
    )
    return pl.pallas_call(
        _ffn_kernel,
        out_shape=jax.ShapeDtypeStruct((P * ROW_TILE, LANES), f32),
        grid_spec=grid_spec,
        compiler_params=pltpu.CompilerParams(
            dimension_semantics=("arbitrary",), vmem_limit_bytes=VMEM_LIMIT_FFN),
        name="ffn",
    )(block_e, n_used, nxt, parity, xs, w_gate_up, b_gate_up.reshape(E, 1, F2), w_down, b_down.reshape(E, 1, D))


def _combine_kernel(lpos_ref, lcur_ref, lnext_ref, ys_ref, x1_ref, rw_ref, mod_ref, o_ref,
                    stage_ref, acc_ref, wb_ref, sem):
    step = pl.program_id(0) * pl.num_programs(1) + pl.program_id(1)
    n_steps = pl.num_programs(0) * pl.num_programs(1)
    slot = step % 2

    def fetch(list_ref, s):
        _run_copies(list_ref, ys_ref, stage_ref.at[s], sem.at[s], to_hbm=False)

    @pl.when(step == 0)
    def _():
        fetch(lcur_ref, 0)

    @pl.when(step + 1 < n_steps)
    def _():
        fetch(lnext_ref, 1 - slot)

    pltpu.make_async_copy(stage_ref.at[slot], stage_ref.at[slot], sem.at[slot]).wait()

    rw = rw_ref[...]
    for k in range(TOP_K):
        wb_ref[k] = jnp.broadcast_to(rw[:, k:k + 1], (TD, LANES))

    def staged(r, k):
        p = lpos_ref[r * TOP_K + k]
        return stage_ref[slot, pl.ds(pl.multiple_of(p, ROW_TILE), ROW_TILE), :]

    def token(r, c):
        acc = wb_ref[0, pl.ds(r, 1), :] * staged(r, 0)
        for k in range(1, TOP_K):
            acc = acc + wb_ref[k, pl.ds(r, 1), :] * staged(r, k)
        acc_ref[pl.ds(pl.multiple_of(r * ROW_TILE, ROW_TILE), ROW_TILE), :] = acc
        return c
    lax.fori_loop(0, TD, token, 0, unroll=DMA_UNROLL)
    o_ref[0] = x1_ref[0] + mod_ref[0, 5:6, :] * _tiles_to_rows(acc_ref, TD)


def _combine(lpos_flat, lists, ys, x1, rw, mod):
    B, S, D = x1.shape
    nj = S // TD
    n_steps = B * nj
    return pl.pallas_call(
        _combine_kernel,
        out_shape=jax.ShapeDtypeStruct((B, S, D), f32),
        grid=(B, nj),
        in_specs=[pl.BlockSpec((TD * TOP_K,), lambda b, j: (b * nj + j,), memory_space=pltpu.SMEM),
                  pl.BlockSpec((LIST_LEN,), lambda b, j: (b * nj + j,), memory_space=pltpu.SMEM),
                  pl.BlockSpec((LIST_LEN,), lambda b, j: (jnp.minimum(b * nj + j + 1, n_steps - 1),),
                               memory_space=pltpu.SMEM),
                  pl.BlockSpec(memory_space=pl.ANY),
                  pl.BlockSpec((1, TD, D), lambda b, j: (b, j, 0)),
                  pl.BlockSpec((TD, LANES), lambda b, j: (b * nj + j, 0)),
                  pl.BlockSpec((1, 6, D), lambda b, j: (b, 0, 0))],
        out_specs=pl.BlockSpec((1, TD, D), lambda b, j: (b, j, 0)),
        scratch_shapes=[pltpu.VMEM((2, TD * TOP_K * ROW_TILE, LANES), f32),
                        pltpu.VMEM((TD * ROW_TILE, LANES), f32), pltpu.VMEM((TOP_K, TD, LANES), f32),
                        pltpu.SemaphoreType.DMA((2,))],
        compiler_params=pltpu.CompilerParams(
            dimension_semantics=("arbitrary", "arbitrary"), vmem_limit_bytes=VMEM_LIMIT),
        name="combine",
    )(lpos_flat, lists, lists, ys, x1, rw, mod)


def _moe(hp, logits, x1, mod, w_gate_up, b_gate_up, w_down, b_down):
    T = logits.shape[0]
    E = w_gate_up.shape[0]
    ri, rw, cnt, snap = _route(logits)
    rank = ri[:, :TOP_K]
    e_sel = ri[:, TOP_K:2 * TOP_K]
    counts = cnt[0, :E].astype(jnp.int32)
    padded = ((counts + FFN_BLK - 1) // FFN_BLK) * FFN_BLK
    p_ends = jnp.cumsum(padded)
    p_starts = p_ends - padded
    nb = -(-T * TOP_K // FFN_BLK) + E
    n_used = jnp.maximum(p_ends[-1:] // FFN_BLK, 1).astype(jnp.int32)
    blk_start = jnp.arange(nb, dtype=jnp.int32) * FFN_BLK
    block_e = jnp.minimum(jnp.sum(p_ends[None, :] <= blk_start[:, None], axis=1), E - 1).astype(jnp.int32)
    assert E == N_RUN
    base = snap.reshape(T // TD, LANES)[:, :E].astype(jnp.int32)
    run_n = jnp.concatenate([base[1:], counts[None, :]], axis=0) - base
    run_off = jnp.cumsum(run_n, axis=1) - run_n
    run_dst = p_starts[None, :].astype(jnp.int32) + base
    shift = jnp.repeat(run_off - base, TD, axis=0)
    onehot = e_sel[:, :, None] == jnp.arange(E, dtype=jnp.int32)[None, None, :]
    lpos = ((jnp.sum(jnp.where(onehot, shift[:, None, :], 0), axis=-1) + rank) * ROW_TILE).reshape(-1)
    lists = _copy_lists(run_dst, run_n, run_off)
    xs = _dispatch(p_ends.astype(jnp.int32), counts, n_used, lpos, lists, hp, nb * FFN_BLK)
    ys = _ffn(block_e, n_used, xs, w_gate_up, b_gate_up, w_down, b_down)
    return _combine(lpos, lists, ys, x1, rw, mod)


def kernel(x, c, rel_bias_table, w_ada, b_ada, g_norm1, w_in, w_gk_up, b_gk_up, g_gla_out, g_qnorm, g_knorm, lambda_q1, lambda_k1, lambda_q2, lambda_k2, g_subln, w_out, g_norm2, w_router, b_router, w_gate_up, b_gate_up, w_down, b_down):
    B, S, D = x.shape
    depth = w_ada.shape[0]
    bias_tiles = _bias_tiles(rel_bias_table, S, min(TQ, S))
    for l in range(depth):
        lambda_init = 0.8 - 0.6 * math.exp(-0.3 * l)
        mod = _ada(c, w_ada[l], b_ada[l])
        qg, kg, gk, kgt, gkt, vg, rg, qd, kd, vd = _inproj(
            x, mod, g_norm1[l], w_in[l], w_gk_up[l], b_gk_up[l], g_qnorm[l], g_knorm[l])
        og = _gla(qg, kg, gk, kgt, gkt, vg, rg, g_gla_out[l])
        lamv = jnp.stack([lambda_q1[l], lambda_k1[l], lambda_q2[l], lambda_k2[l]]).astype(f32)
        od = lax.cond(_scores_bounded(rel_bias_table, g_qnorm[l], g_knorm[l]),
                      functools.partial(_attn, lambda_init=lambda_init, bounded=True),
                      functools.partial(_attn, lambda_init=lambda_init, bounded=False),
                      qd, kd, vd, bias_tiles, lamv, g_subln[l])
        x1, hp, logits = _outproj(og, od, x, mod, w_out[l], g_norm2[l], w_router[l], b_router[l])
        x = _moe(hp, logits, x1, mod, w_gate_up[l], b_gate_up[l], w_down[l], b_down[l])
    return x
```

```python
import functools
import math

import jax
import jax.numpy as jnp
from jax import lax
from jax.experimental import pallas as pl
from jax.experimental.pallas import tpu as pltpu

f32 = jnp.float32
bf16 = jnp.bfloat16

N_GLA_HEADS = 4
GLA_DK = 64
GLA_DV = 128
GLA_GATE_RANK = 16
GLA_GATE_NORM = 16.0
GLA_CHUNK = 64
N_DIFF_HEADS = 4
DIFF_DQK = 64
DIFF_DV = 128
NUM_BUCKETS = 32
MAX_DISTANCE = 128
TOP_K = 4
SWIGLU_LIMIT = 7.0
SWIGLU_ALPHA = 1.702
EPS = 1e-6

GLA_QK_W = N_GLA_HEADS * GLA_DK
GLA_V_W = N_GLA_HEADS * GLA_DV
DIFF_QK_W = N_DIFF_HEADS * 2 * DIFF_DQK
DIFF_V_W = N_DIFF_HEADS * DIFF_DV

LANES = 128
NEG = -1e30
LOG2E = math.log2(math.e)
SAFE_SCORE = 40.0
NORM_SLACK = 1.02
VMEM_LIMIT = 48 * 1024 * 1024
VMEM_LIMIT_FFN = 58 * 1024 * 1024

TM_IN = 512
INPROJ_SUB = 2
TG_GLA = 1024
PAIR = 2 * GLA_CHUNK
GLA_UNROLL = 4
TQ = 512
ATTN_UNROLL = 4
TR = 512
TD = 512
ROW_TILE = 8
DMA_UNROLL = 8
N_RUN = 32
RUN_SIZES = tuple(TD >> b for b in range(TD.bit_length()))
LIST_DST = len(RUN_SIZES) * N_RUN
LIST_CNT = 2 * LIST_DST
LIST_LEN = 1024
FFN_BLK = 512


def _nt(a, b):
    return lax.dot_general(a, b, (((1,), (1,)), ((), ())), preferred_element_type=f32)


def _mm(a, b):
    return jnp.dot(a, b, preferred_element_type=f32)


def _split(x):
    hi = x.astype(bf16)
    lo = (x - hi.astype(f32)).astype(bf16)
    return hi, lo


def _silu(x):
    return x * jax.nn.sigmoid(x)


def _ada_kernel(c_ref, w_ref, b_ref, o_ref):
    c = c_ref[...]
    o_ref[...] = _mm(_silu(c).astype(bf16), w_ref[...].astype(bf16)) + b_ref[...]


def _ada(c, w_ada, b_ada):
    B, D = c.shape
    N = w_ada.shape[1]
    bp = ROW_TILE
    assert B <= bp
    cp = jnp.zeros((bp, D), f32).at[:B].set(c)
    tn = N // 4
    out = pl.pallas_call(
        _ada_kernel,
        out_shape=jax.ShapeDtypeStruct((bp, N), f32),
        grid=(N // tn,),
        in_specs=[pl.BlockSpec((bp, D), lambda j: (0, 0)),
                  pl.BlockSpec((D, tn), lambda j: (0, j)),
                  pl.BlockSpec((1, tn), lambda j: (0, j))],
        out_specs=pl.BlockSpec((bp, tn), lambda j: (0, j)),
        compiler_params=pltpu.CompilerParams(vmem_limit_bytes=VMEM_LIMIT),
        name="ada",
    )(cp, w_ada, b_ada.reshape(1, N))
    return out[:B].reshape(B, 6, D)


def _inproj_kernel(x_ref, mod_ref, g1_ref, wm_ref, wkt_ref, wlo_ref, wup_ref, wupt_ref,
                   bup_ref, bupt_ref, gqk_ref, grp_ref, grpt_ref,
                   qg_ref, kg_ref, gk_ref, kgt_ref, gkt_ref, vg_ref, rg_ref,
                   qd_ref, kd_ref, vd_ref):
    tm = x_ref.shape[1]
    sub = tm // INPROJ_SUB
    for t in range(INPROJ_SUB):
        _inproj_rows(slice(t * sub, (t + 1) * sub), x_ref, mod_ref, g1_ref, wm_ref, wkt_ref, wlo_ref, wup_ref,
                     wupt_ref, bup_ref, bupt_ref, gqk_ref, grp_ref, grpt_ref, qg_ref, kg_ref, gk_ref, kgt_ref,
                     gkt_ref, vg_ref, rg_ref, qd_ref, kd_ref, vd_ref)


def _inproj_rows(rows, x_ref, mod_ref, g1_ref, wm_ref, wkt_ref, wlo_ref, wup_ref, wupt_ref,
                 bup_ref, bupt_ref, gqk_ref, grp_ref, grpt_ref,
                 qg_ref, kg_ref, gk_ref, kgt_ref, gkt_ref, vg_ref, rg_ref, qd_ref, kd_ref, vd_ref):
    x = x_ref[0, rows, :]
    ms = jnp.mean(x * x, axis=-1, keepdims=True)
    y = x * lax.rsqrt(ms + EPS) * g1_ref[...]
    h = (y * (1.0 + mod_ref[0, 1:2, :]) + mod_ref[0, 0:1, :]).astype(bf16)

    def proj(a, b):
        return _mm(h, wm_ref[:, a:b])

    o = 0
    qg_ref[0, rows, :] = proj(o, o + GLA_QK_W); o += GLA_QK_W
    kg_ref[0, rows, :] = proj(o, o + GLA_QK_W); o += GLA_QK_W
    vg_ref[0, rows, :] = proj(o, o + GLA_V_W).astype(bf16); o += GLA_V_W
    rg_ref[0, rows, :] = proj(o, o + GLA_V_W); o += GLA_V_W
    qk = proj(o, o + 2 * DIFF_QK_W); o += 2 * DIFF_QK_W
    vd_ref[0, rows, :] = proj(o, o + DIFF_V_W).astype(bf16)

    slab0 = rows.start // PAIR
    kgt = _nt(wkt_ref[...], h)
    for j in range(kgt.shape[1] // PAIR):
        kgt_ref[0, slab0 + j] = kgt[:, j * PAIR:(j + 1) * PAIR]

    lo = _mm(h, wlo_ref[...]).astype(bf16)
    z = _mm(lo, wup_ref[...]) + bup_ref[...]
    gk_ref[0, rows, :] = (jnp.minimum(z, 0.0) - jnp.log1p(jnp.exp(-jnp.abs(z)))) * (1.0 / GLA_GATE_NORM)
    zt = _nt(wupt_ref[...], lo) + bupt_ref[...]
    gkt = (jnp.minimum(zt, 0.0) - jnp.log1p(jnp.exp(-jnp.abs(zt)))) * (1.0 / GLA_GATE_NORM)
    for j in range(gkt.shape[1] // PAIR):
        gkt_ref[0, slab0 + j] = gkt[:, j * PAIR:(j + 1) * PAIR]

    sq_hi, sq_lo = _split(qk * qk)
    gs = _mm(sq_hi, grp_ref[...]) + _mm(sq_lo, grp_ref[...])
    r = lax.rsqrt(gs * (1.0 / DIFF_DQK) + EPS)
    r_hi, r_lo = _split(r)
    rb = _mm(r_hi, grpt_ref[...]) + _mm(r_lo, grpt_ref[...])
    qkn = qk * rb * gqk_ref[...]
    qd_ref[0, rows, :] = qkn[:, :DIFF_QK_W].astype(bf16)
    kd_ref[0, rows, :] = qkn[:, DIFF_QK_W:].astype(bf16)


def _inproj(x, mod, g_norm1, w_in, w_gk_up, b_gk_up, g_qnorm, g_knorm):
    B, S, D = x.shape
    offs = [0]
    for w in (GLA_QK_W, GLA_QK_W, GLA_V_W, GLA_V_W, GLA_GATE_RANK, DIFF_QK_W, DIFF_QK_W, DIFF_V_W):
        offs.append(offs[-1] + w)
    w_main = jnp.concatenate([w_in[:, offs[0]:offs[4]], w_in[:, offs[5]:offs[8]]], axis=1).astype(bf16)
    w_kt = w_in[:, offs[1]:offs[2]].T.astype(bf16)
    w_lo = jnp.zeros((D, LANES), f32).at[:, :GLA_GATE_RANK].set(w_in[:, offs[4]:offs[5]]).astype(bf16)
    w_up = jnp.zeros((LANES, GLA_QK_W), f32).at[:GLA_GATE_RANK].set(w_gk_up).astype(bf16)
    w_upt = w_up.T
    b_up = b_gk_up.reshape(1, GLA_QK_W)
    b_upt = b_gk_up.reshape(GLA_QK_W, 1)
    n_grp = 2 * DIFF_QK_W // DIFF_DQK
    gqk = jnp.concatenate([jnp.tile(g_qnorm, n_grp // 2) * (DIFF_DQK ** -0.5 * LOG2E),
                           jnp.tile(g_knorm, n_grp // 2)]).reshape(1, 2 * DIFF_QK_W)
    grp = (jnp.arange(2 * DIFF_QK_W)[:, None] // DIFF_DQK == jnp.arange(LANES)[None, :]).astype(bf16)
    grpt = grp.T
    nw = w_main.shape[1]
    tm = TM_IN
    const = lambda shape: pl.BlockSpec(shape, lambda b, i: (0,) * len(shape))
    row = lambda w: pl.BlockSpec((1, tm, w), lambda b, i: (b, i, 0))
    colT = pl.BlockSpec((1, tm // PAIR, GLA_QK_W, PAIR), lambda b, i: (b, i, 0, 0))
    outs = pl.pallas_call(
        _inproj_kernel,
        out_shape=[jax.ShapeDtypeStruct((B, S, GLA_QK_W), f32),
                   jax.ShapeDtypeStruct((B, S, GLA_QK_W), f32),
                   jax.ShapeDtypeStruct((B, S, GLA_QK_W), f32),
                   jax.ShapeDtypeStruct((B, S // PAIR, GLA_QK_W, PAIR), f32),
                   jax.ShapeDtypeStruct((B, S // PAIR, GLA_QK_W, PAIR), f32),
                   jax.ShapeDtypeStruct((B, S, GLA_V_W), bf16),
                   jax.ShapeDtypeStruct((B, S, GLA_V_W), f32),
                   jax.ShapeDtypeStruct((B, S, DIFF_QK_W), bf16),
                   jax.ShapeDtypeStruct((B, S, DIFF_QK_W), bf16),
                   jax.ShapeDtypeStruct((B, S, DIFF_V_W), bf16)],
        grid=(B, S // tm),
        in_specs=[row(D),
                  pl.BlockSpec((1, 6, D), lambda b, i: (b, 0, 0)),
                  const((1, D)), const((D, nw)), const((GLA_QK_W, D)), const((D, LANES)),
                  const((LANES, GLA_QK_W)), const((GLA_QK_W, LANES)),
                  const((1, GLA_QK_W)), const((GLA_QK_W, 1)),
                  const((1, 2 * DIFF_QK_W)), const((2 * DIFF_QK_W, LANES)),
                  const((LANES, 2 * DIFF_QK_W))],
        out_specs=[row(GLA_QK_W), row(GLA_QK_W), row(GLA_QK_W), colT, colT,
                   row(GLA_V_W), row(GLA_V_W), row(DIFF_QK_W), row(DIFF_QK_W), row(DIFF_V_W)],
        compiler_params=pltpu.CompilerParams(
            dimension_semantics=("arbitrary", "arbitrary"), vmem_limit_bytes=VMEM_LIMIT),
        name="inproj",
    )(x, mod, g_norm1.reshape(1, D), w_main, w_kt, w_lo, w_up, w_upt, b_up, b_upt, gqk, grp, grpt)
    return outs


def _gla_kernel(q_ref, k_ref, g_ref, kt_ref, gt_ref, v_ref, r_ref, gout_ref, tri_ref, trit_ref,
                o_ref, s_ref, *, n_pairs):
    H, DK, DV = N_GLA_HEADS, GLA_DK, GLA_DV

    @pl.when(pl.program_id(1) == 0)
    def _():
        s_ref[...] = jnp.zeros_like(s_ref)

    tri = tri_ref[...]
    trit = trit_ref[...]
    tri_b = tri > 0
    lane_head = lax.broadcasted_iota(jnp.int32, (1, H * DK), 1) // DK
    row_head = lax.broadcasted_iota(jnp.int32, (H * PAIR, 1), 0) // PAIR
    qmask = row_head == lane_head
    row_first = lax.broadcasted_iota(jnp.int32, (PAIR, 1), 0) < GLA_CHUNK
    row_first4 = (lax.broadcasted_iota(jnp.int32, (H * PAIR, 1), 0) % PAIR) < GLA_CHUNK
    lane_first = lax.broadcasted_iota(jnp.int32, (1, PAIR), 1) < GLA_CHUNK
    scale = DK ** -0.5
    gout = gout_ref[...]

    def pair(p, carry):
        r0 = pl.multiple_of(p * PAIR, PAIR)
        q = q_ref[0, pl.ds(r0, PAIR), :]
        k = k_ref[0, pl.ds(r0, PAIR), :]
        g = g_ref[0, pl.ds(r0, PAIR), :]
        kt = kt_ref[0, p]
        gt = gt_ref[0, p]
        v = v_ref[0, pl.ds(r0, PAIR), :]

        g_hi, g_lo = _split(g)
        gc = _mm(tri, g_hi) + _mm(tri, g_lo)
        gt_hi, gt_lo = _split(gt)
        gct = _mm(gt_hi, trit) + _mm(gt_lo, trit)
        g_last = jnp.where(row_first, gc[GLA_CHUNK - 1:GLA_CHUNK, :], gc[PAIR - 1:PAIR, :])
        gl0 = gct[:, GLA_CHUNK - 1:GLA_CHUNK]
        gl1 = gct[:, PAIR - 1:PAIR]
        g_last_t = jnp.where(lane_first, gl0, gl1)

        q_e = (q * (jnp.exp(gc) * scale)).astype(bf16)
        k_e = (k * jnp.exp(-gc)).astype(bf16)
        ks_t = kt * jnp.exp(g_last_t - gct)
        ks_t0 = jnp.where(lane_first, ks_t, 0.0).astype(bf16)
        ks_t1 = jnp.where(lane_first, 0.0, ks_t).astype(bf16)
        del g_last

        qm = jnp.where(qmask, jnp.concatenate([q_e] * H, axis=0), jnp.zeros((), bf16))
        a = _nt(qm, k_e)
        s0 = s_ref[...]

        u0 = []
        u1 = []
        for h in range(H):
            v_h = v[:, h * DV:(h + 1) * DV]
            u0.append(_mm(ks_t0[h * DK:(h + 1) * DK], v_h))
            u1.append(_mm(ks_t1[h * DK:(h + 1) * DK], v_h))
        u0 = jnp.concatenate(u0, axis=0)
        u1 = jnp.concatenate(u1, axis=0)
        s1 = s0 * jnp.exp(gl0) + u0
        s_ref[...] = s1 * jnp.exp(gl1) + u1

        o_inter = jnp.where(row_first4, _mm(qm, s0.astype(bf16)), _mm(qm, s1.astype(bf16)))
        for h in range(H):
            a_h = jnp.where(tri_b, a[h * PAIR:(h + 1) * PAIR], 0.0).astype(bf16)
            o_h = _mm(a_h, v[:, h * DV:(h + 1) * DV]) + o_inter[h * PAIR:(h + 1) * PAIR]
            ms = jnp.mean(o_h * o_h, axis=-1, keepdims=True)
            o_n = o_h * lax.rsqrt(ms + EPS) * gout
            r_h = r_ref[0, pl.ds(r0, PAIR), h * DV:(h + 1) * DV]
            o_ref[0, pl.ds(r0, PAIR), h * DV:(h + 1) * DV] = (o_n * _silu(r_h)).astype(bf16)
        return carry

    lax.fori_loop(0, n_pairs, pair, 0, unroll=GLA_UNROLL)


def _gla(qg, kg, gk, kgt, gkt, vg, rg, g_gla_out):
    B, S, _ = qg.shape
    tg = min(TG_GLA, S)
    r = jnp.arange(PAIR)
    tri = ((r[:, None] // GLA_CHUNK == r[None, :] // GLA_CHUNK) & (r[None, :] <= r[:, None])).astype(bf16)
    row = lambda w: pl.BlockSpec((1, tg, w), lambda b, i: (b, i, 0))
    colT = pl.BlockSpec((1, tg // PAIR, GLA_QK_W, PAIR), lambda b, i: (b, i, 0, 0))
    const = lambda shape: pl.BlockSpec(shape, lambda b, i: (0,) * len(shape))
    return pl.pallas_call(
        functools.partial(_gla_kernel, n_pairs=tg // PAIR),
        out_shape=jax.ShapeDtypeStruct((B, S, GLA_V_W), bf16),
        grid=(B, S // tg),
        in_specs=[row(GLA_QK_W), row(GLA_QK_W), row(GLA_QK_W), colT, colT,
                  row(GLA_V_W), row(GLA_V_W), const((1, GLA_DV)),
                  const((PAIR, PAIR)), const((PAIR, PAIR))],
        out_specs=row(GLA_V_W),
        scratch_shapes=[pltpu.VMEM((GLA_QK_W, GLA_DV), f32)],
        compiler_params=pltpu.CompilerParams(
            dimension_semantics=("arbitrary", "arbitrary"), vmem_limit_bytes=VMEM_LIMIT),
        name="gla",
    )(qg, kg, gk, kgt, gkt, vg, rg, g_gla_out.reshape(1, GLA_DV), tri, tri.T)


def _attn_finish(o, gsub_ref, o_ref, lambda_init):
    ms = jnp.mean(o * o, axis=-1, keepdims=True)
    o_ref[0] = (o * lax.rsqrt(ms + EPS) * gsub_ref[...] * (1.0 - lambda_init)).astype(bf16)


def _attn_lambda(lamv_ref, lambda_init):
    lv = lamv_ref[...]
    return (jnp.exp(jnp.sum(lv[0:1] * lv[1:2], axis=-1, keepdims=True))
            - jnp.exp(jnp.sum(lv[2:3] * lv[3:4], axis=-1, keepdims=True)) + lambda_init)


def _attn_bounded_kernel(q_ref, k_ref, v_ref, bias_ref, lamv_ref, gsub_ref, o_ref, vaug_ref, *, lambda_init):
    qi = pl.program_id(2)
    tq = q_ref.shape[1]
    S = k_ref.shape[1]

    @pl.when(qi == 0)
    def _():
        lane = lax.broadcasted_iota(jnp.int32, (S, DIFF_DV), 1)
        vaug_ref[:, :DIFF_DV] = v_ref[0]
        vaug_ref[:, DIFF_DV:] = jnp.where(lane == 0, 1.0, 0.0).astype(bf16)

    q = q_ref[0]
    lane = lax.broadcasted_iota(jnp.int32, (1, 2 * DIFF_DQK), 1)
    zero = jnp.zeros((), bf16)
    qs = (jnp.where(lane < DIFF_DQK, q, zero), jnp.where(lane < DIFF_DQK, zero, q))

    def update(accs, k0, bias):
        kb = k_ref[0, pl.ds(k0, tq), :]
        vb = vaug_ref[pl.ds(k0, tq), :]
        out = []
        for c in range(2):
            s = _nt(qs[c], kb)
            if bias is not None:
                s = s + bias[c]
            out.append(accs[c] + _mm(jnp.exp2(s).astype(bf16), vb))
        return tuple(out)

    def far(kj, accs):
        return update(accs, pl.multiple_of(kj * tq, tq), None)

    def far_group(g, accs):
        for u in range(ATTN_UNROLL):
            accs = far(g * ATTN_UNROLL + u, accs)
        return accs

    def block_or_masked(accs, kj, bias):
        exists = kj >= 0
        k0 = pl.multiple_of(jnp.maximum(kj, 0) * tq, tq)
        if bias is None:
            tiles = (jnp.where(exists, 0.0, NEG),) * 2
        else:
            tiles = tuple(jnp.where(exists, b, NEG) for b in bias)
        return update(accs, k0, tiles)

    accs = (jnp.zeros((tq, 2 * DIFF_DV), f32), jnp.zeros((tq, 2 * DIFF_DV), f32))
    accs = update(accs, pl.multiple_of(qi * tq, tq), (bias_ref[0, 0, 1], bias_ref[0, 1, 1]))
    accs = block_or_masked(accs, qi - 1, (bias_ref[0, 0, 0], bias_ref[0, 1, 0]))
    for u in range(2, ATTN_UNROLL):
        accs = block_or_masked(accs, qi - u, None)
    n_far = jnp.maximum(qi + 1 - ATTN_UNROLL, 0)
    n_grp = n_far // ATTN_UNROLL
    accs = lax.fori_loop(0, n_grp, far_group, accs)
    accs = lax.fori_loop(n_grp * ATTN_UNROLL, n_far, far, accs)
    a0, a1 = accs
    o = (a0[:, :DIFF_DV] / a0[:, DIFF_DV:DIFF_DV + 1]
         - _attn_lambda(lamv_ref, lambda_init) * (a1[:, :DIFF_DV] / a1[:, DIFF_DV:DIFF_DV + 1]))
    _attn_finish(o, gsub_ref, o_ref, lambda_init)


def _attn_kernel(q_ref, k_ref, v_ref, bias_ref, lamv_ref, gsub_ref, o_ref, *, lambda_init):
    qi = pl.program_id(2)
    tq = q_ref.shape[1]
    q = q_ref[0]
    lane = lax.broadcasted_iota(jnp.int32, (1, 2 * DIFF_DQK), 1)
    zero = jnp.zeros((), bf16)
    qs = (jnp.where(lane < DIFF_DQK, q, zero), jnp.where(lane < DIFF_DQK, zero, q))

    def update(state, kb, vb, bias):
        new = []
        for c in range(2):
            m, l, acc = state[c]
            s = _nt(qs[c], kb)
            if bias is not None:
                s = s + bias[c]
            m_new = jnp.maximum(m, jnp.max(s, axis=-1, keepdims=True))
            alpha = jnp.exp2(m - m_new)
            p = jnp.exp2(s - m_new)
            l = alpha * l + jnp.sum(p, axis=-1, keepdims=True)
            acc = alpha * acc + _mm(p.astype(bf16), vb)
            new.append((m_new, l, acc))
        return tuple(new)

    init = tuple((jnp.full((tq, 1), NEG, f32), jnp.zeros((tq, 1), f32), jnp.zeros((tq, DIFF_DV), f32))
                 for _ in range(2))

    def far(kj, state):
        k0 = pl.multiple_of(kj * tq, tq)
        return update(state, k_ref[0, pl.ds(k0, tq), :], v_ref[0, pl.ds(k0, tq), :], None)

    state = lax.fori_loop(0, jnp.maximum(qi - 1, 0), far, init)

    kd0 = pl.multiple_of(qi * tq, tq)
    state = update(state, k_ref[0, pl.ds(kd0, tq), :], v_ref[0, pl.ds(kd0, tq), :],
                   (bias_ref[0, 0, 1], bias_ref[0, 1, 1]))
    kp0 = pl.multiple_of(jnp.maximum(qi - 1, 0) * tq, tq)
    has_prev = qi > 0
    state = update(state, k_ref[0, pl.ds(kp0, tq), :], v_ref[0, pl.ds(kp0, tq), :],
                   (jnp.where(has_prev, bias_ref[0, 0, 0], NEG), jnp.where(has_prev, bias_ref[0, 1, 0], NEG)))

    (_, l0, a0), (_, l1, a1) = state
    o = a0 / l0 - _attn_lambda(lamv_ref, lambda_init) * (a1 / l1)
    _attn_finish(o, gsub_ref, o_ref, lambda_init)


def _t5_bucket(n):
    max_exact = NUM_BUCKETS // 2
    nf = jnp.maximum(n, 1).astype(f32)
    large = max_exact + (jnp.log(nf / max_exact) / math.log(MAX_DISTANCE / max_exact)
                         * (NUM_BUCKETS - max_exact)).astype(jnp.int32)
    large = jnp.minimum(large, NUM_BUCKETS - 1)
    return jnp.where(n < max_exact, n, large)


def _toeplitz_kernel(w_ref, o_ref):
    n = o_ref.shape[-1]
    for t in range(2):
        rows = jnp.broadcast_to(w_ref[0, t:t + 1, :], (n, 2 * n))
        o_ref[0, 0, t] = pltpu.roll(rows, 0, 1, stride=1, stride_axis=0)[:, n:]


def _bias_tiles(rel_bias_table, S, n):
    HM = rel_bias_table.shape[1]
    assert n >= MAX_DISTANCE
    d = jnp.arange(2 * n, dtype=jnp.int32)
    by_dist = rel_bias_table[_t5_bucket(d)].astype(f32).T
    rel = (by_dist - rel_bias_table[NUM_BUCKETS - 1].astype(f32)[:, None]) * LOG2E
    i = jnp.arange(2 * n)
    w_diag = jnp.where(i[None, :] <= n, rel[:, jnp.clip(n - i, 0, 2 * n - 1)], NEG)
    w_prev = rel[:, jnp.clip(2 * n - i, 0, 2 * n - 1)]
    w = jnp.stack([w_prev, w_diag], axis=1)
    return pl.pallas_call(
        _toeplitz_kernel,
        out_shape=jax.ShapeDtypeStruct((HM // 2, 2, 2, n, n), f32),
        grid=(HM // 2, 2),
        in_specs=[pl.BlockSpec((1, 2, 2 * n), lambda h, m: (h * 2 + m, 0, 0))],
        out_specs=pl.BlockSpec((1, 1, 2, n, n), lambda h, m: (h, m, 0, 0, 0)),
        compiler_params=pltpu.CompilerParams(vmem_limit_bytes=VMEM_LIMIT),
        name="bias_tiles",
    )(w)


def _attn(qd, kd, vd, bias_tiles, lamv, g_subln, lambda_init, bounded):
    B, S, _ = qd.shape
    H = N_DIFF_HEADS
    tq = min(TQ, S)
    body = _attn_bounded_kernel if bounded else _attn_kernel
    scratch = [pltpu.VMEM((S, 2 * DIFF_DV), bf16)] if bounded else []
    return pl.pallas_call(
        functools.partial(body, lambda_init=lambda_init),
        out_shape=jax.ShapeDtypeStruct((B, S, DIFF_V_W), bf16),
        scratch_shapes=scratch,
        grid=(B, H, S // tq),
        in_specs=[pl.BlockSpec((1, tq, 2 * DIFF_DQK), lambda b, h, i: (b, i, h)),
                  pl.BlockSpec((1, S, 2 * DIFF_DQK), lambda b, h, i: (b, 0, h)),
                  pl.BlockSpec((1, S, DIFF_DV), lambda b, h, i: (b, 0, h)),
                  pl.BlockSpec((1, 2, 2, tq, tq), lambda b, h, i: (h, 0, 0, 0, 0)),
                  pl.BlockSpec((4, DIFF_DQK), lambda b, h, i: (0, 0)),
                  pl.BlockSpec((1, DIFF_DV), lambda b, h, i: (0, 0))],
        out_specs=pl.BlockSpec((1, tq, DIFF_DV), lambda b, h, i: (b, i, h)),
        compiler_params=pltpu.CompilerParams(
            dimension_semantics=("arbitrary", "arbitrary", "arbitrary"), vmem_limit_bytes=VMEM_LIMIT),
        name="attn_bounded" if bounded else "attn_online",
    )(qd, kd, vd, bias_tiles, lamv, g_subln.reshape(1, DIFF_DV))


def _scores_bounded(rel_bias_table, g_qnorm, g_knorm):
    qk = DIFF_DQK ** 0.5 * jnp.max(jnp.abs(g_qnorm)) * jnp.max(jnp.abs(g_knorm)) * NORM_SLACK
    rel = jnp.max(jnp.abs(rel_bias_table - rel_bias_table[NUM_BUCKETS - 1:]))
    return qk + rel <= SAFE_SCORE


def _rows_to_tiles(x, ref):
    n = x.shape[0]
    for c in range(ROW_TILE):
        ref[pl.ds(c, n, stride=ROW_TILE), :] = x[:, c * LANES:(c + 1) * LANES]


def _tiles_to_rows(ref, n):
    return jnp.concatenate([ref[pl.ds(c, n, stride=ROW_TILE), :] for c in range(ROW_TILE)], axis=1)


def _outproj_kernel(og_ref, od_ref, x_ref, mod_ref, wo_ref, g2_ref, wr_ref, br_ref,
                    x1_ref, hp_ref, lg_ref):
    half = og_ref.shape[2]
    sub = og_ref.shape[1] // INPROJ_SUB
    for t in range(INPROJ_SUB):
        rows = slice(t * sub, (t + 1) * sub)
        mix = _mm(og_ref[0, rows, :], wo_ref[:half, :]) + _mm(od_ref[0, rows, :], wo_ref[half:, :])
        x1 = x_ref[0, rows, :] + mod_ref[0, 2:3, :] * mix
        x1_ref[0, rows, :] = x1
        ms = jnp.mean(x1 * x1, axis=-1, keepdims=True)
        y = x1 * lax.rsqrt(ms + EPS) * g2_ref[...]
        h = (y * (1.0 + mod_ref[0, 4:5, :]) + mod_ref[0, 3:4, :]).astype(bf16)
        lg_ref[rows, :] = _mm(h, wr_ref[...]) + br_ref[...]
        _rows_to_tiles(h.astype(f32), hp_ref.at[pl.ds(t * sub * ROW_TILE, sub * ROW_TILE)])


def _outproj(og, od, x, mod, w_out, g_norm2, w_router, b_router):
    B, S, D = x.shape
    assert D == ROW_TILE * LANES, "the token-tile layout needs a model row to fill one (8,128) tile"
    E = w_router.shape[1]
    tm = TM_IN
    nj = S // tm
    w_r = jnp.zeros((D, LANES), f32).at[:, :E].set(w_router).astype(bf16)
    b_r = jnp.full((1, LANES), NEG, f32).at[0, :E].set(b_router)
    const = lambda shape: pl.BlockSpec(shape, lambda b, i: (0,) * len(shape))
    return pl.pallas_call(
        _outproj_kernel,
        out_shape=[jax.ShapeDtypeStruct((B, S, D), f32),
                   jax.ShapeDtypeStruct((B * S * ROW_TILE, LANES), f32),
                   jax.ShapeDtypeStruct((B * S, LANES), f32)],
        grid=(B, nj),
        in_specs=[pl.BlockSpec((1, tm, og.shape[2]), lambda b, i: (b, i, 0)),
                  pl.BlockSpec((1, tm, od.shape[2]), lambda b, i: (b, i, 0)),
                  pl.BlockSpec((1, tm, D), lambda b, i: (b, i, 0)),
                  pl.BlockSpec((1, 6, D), lambda b, i: (b, 0, 0)),
                  const((w_out.shape[0], D)), const((1, D)), const((D, LANES)), const((1, LANES))],
        out_specs=[pl.BlockSpec((1, tm, D), lambda b, i: (b, i, 0)),
                   pl.BlockSpec((tm * ROW_TILE, LANES), lambda b, i: (b * nj + i, 0)),
                   pl.BlockSpec((tm, LANES), lambda b, i: (b * nj + i, 0))],
        compiler_params=pltpu.CompilerParams(
            dimension_semantics=("arbitrary", "arbitrary"), vmem_limit_bytes=VMEM_LIMIT),
        name="outproj",
    )(og, od, x, mod, w_out.astype(bf16), g_norm2.reshape(1, D), w_r, b_r)


def _route_kernel(lg_ref, lt_ref, ut_ref, ri_ref, rw_ref, cnt_ref, snap_ref, run_ref):
    @pl.when(pl.program_id(0) == 0)
    def _():
        run_ref[...] = jnp.zeros_like(run_ref)

    x = lg_ref[...]
    tr = x.shape[0]
    lane = lax.broadcasted_iota(jnp.int32, (tr, LANES), 1)
    lane_f = lane.astype(f32)
    vals, hots = [], []
    for _ in range(TOP_K):
        m = jnp.max(x, axis=-1, keepdims=True)
        idx = jnp.min(jnp.where(x == m, lane_f, float(LANES)), axis=-1, keepdims=True)
        hot = lane_f == idx
        x = jnp.where(hot, -jnp.inf, x)
        vals.append(m)
        hots.append(hot)
    ex = [jnp.exp(v - vals[0]) for v in vals]
    den = ex[0] + ex[1] + ex[2] + ex[3]
    sel = (hots[0] | hots[1] | hots[2] | hots[3]).astype(f32)
    rank = _mm(lt_ref[...], sel.astype(bf16)) + run_ref[...]
    run_ref[...] = run_ref[...] + jnp.sum(sel, axis=0, keepdims=True)
    cnt_ref[...] = run_ref[...]
    base = rank[0:1, :]
    snap_ref[0] = base
    n_hi, n_lo = _split(jnp.sum(sel, axis=0, keepdims=True))
    start = _mm(n_hi, ut_ref[...]) + _mm(n_lo, ut_ref[...])
    place = rank - base + start
    ri = jnp.zeros((tr, LANES), jnp.int32)
    rw = jnp.zeros((tr, LANES), f32)
    for k in range(TOP_K):
        rk = jnp.sum(jnp.where(hots[k], place, 0.0), axis=-1, keepdims=True).astype(jnp.int32)
        ri = jnp.where(lane == k, rk, ri)
        rw = jnp.where(lane == k, ex[k] / den, rw)
    ri_ref[...] = ri
    rw_ref[...] = rw


def _route(logits):
    T = logits.shape[0]
    tr = min(TR, T)
    assert tr == TD, "routing and dispatch share one token tile"
    r = jnp.arange(tr)
    lt = (r[None, :] < r[:, None]).astype(bf16)
    e = jnp.arange(LANES)
    ut = (e[:, None] < e[None, :]).astype(bf16)
    return pl.pallas_call(
        _route_kernel,
        out_shape=[jax.ShapeDtypeStruct((T, LANES), jnp.int32),
                   jax.ShapeDtypeStruct((T, LANES), f32),
                   jax.ShapeDtypeStruct((1, LANES), f32),
                   jax.ShapeDtypeStruct((T // tr, 1, LANES), f32)],
        grid=(T // tr,),
        in_specs=[pl.BlockSpec((tr, LANES), lambda i: (i, 0)),
                  pl.BlockSpec((tr, tr), lambda i: (0, 0)),
                  pl.BlockSpec((LANES, LANES), lambda i: (0, 0))],
        out_specs=[pl.BlockSpec((tr, LANES), lambda i: (i, 0)),
                   pl.BlockSpec((tr, LANES), lambda i: (i, 0)),
                   pl.BlockSpec((1, LANES), lambda i: (0, 0)),
                   pl.BlockSpec((1, 1, LANES), lambda i: (i, 0, 0))],
        scratch_shapes=[pltpu.VMEM((1, LANES), f32)],
        compiler_params=pltpu.CompilerParams(dimension_semantics=("arbitrary",)),
        name="route",
    )(logits, lt, ut)


def _run_copies(list_ref, hbm_ref, stage_ref, sem, to_hbm):
    for c, size in enumerate(RUN_SIZES):
        def one(i, carry, c=c, size=size):
            s0 = list_ref[c * N_RUN + i]
            d0 = list_ref[LIST_DST + c * N_RUN + i]
            stage = stage_ref.at[pl.ds(pl.multiple_of(s0, ROW_TILE), size * ROW_TILE)]
            rows = hbm_ref.at[pl.ds(pl.multiple_of(d0, ROW_TILE), size * ROW_TILE)]
            src, dst = (stage, rows) if to_hbm else (rows, stage)
            pltpu.make_async_copy(src, dst, sem).start(priority=c % 2)
            return carry
        lax.fori_loop(0, list_ref[LIST_CNT + c], one, 0)


def _copy_lists(run_dst, run_n, run_off):
    n_tiles = run_n.shape[0]
    size = jnp.asarray(RUN_SIZES, jnp.int32)[None, :, None]
    n = run_n[:, None, :]
    bit = (n & size) != 0
    before = n & ~(2 * size - 1)
    place = jnp.cumsum(bit, axis=-1) - 1
    pick = bit[:, :, None, :] & (place[:, :, None, :] == jnp.arange(N_RUN)[None, None, :, None])

    def compact(v):
        return jnp.sum(jnp.where(pick, v[:, :, None, :], 0), axis=-1) * ROW_TILE
    src = compact(run_off[:, None, :] + before)
    dst = compact(run_dst[:, None, :] + before)
    cnt = jnp.sum(bit, axis=-1).astype(jnp.int32)
    pad = jnp.zeros((n_tiles, LIST_LEN - LIST_CNT - len(RUN_SIZES)), jnp.int32)
    lists = jnp.concatenate([src.reshape(n_tiles, -1), dst.reshape(n_tiles, -1), cnt, pad], axis=1)
    return lists.reshape(-1).astype(jnp.int32)


def _dispatch_kernel(pend_ref, cnt_ref, nu_ref, lpos_ref, list_ref, h_ref, xs_ref,
                     zero_ref, stage_ref, sem, zsem):
    n_tok = h_ref.shape[0] // ROW_TILE
    blk_rows = FFN_BLK * ROW_TILE

    @pl.when(pl.program_id(0) == 0)
    def _():
        zero_ref[...] = jnp.zeros_like(zero_ref)
        n_exp = pend_ref.shape[0]

        def last_block(e):
            return xs_ref.at[pl.ds(pl.multiple_of((pend_ref[e] - FFN_BLK) * ROW_TILE, blk_rows), blk_rows)]

        def zfill(e, c):
            @pl.when(cnt_ref[e] > 0)
            def _():
                pltpu.make_async_copy(zero_ref, last_block(e), zsem).start()
            return c

        def zwait(e, c):
            @pl.when(cnt_ref[e] > 0)
            def _():
                pltpu.make_async_copy(zero_ref, last_block(e), zsem).wait()
            return c

        lax.fori_loop(0, n_exp, zfill, 0)
        lax.fori_loop(0, n_exp, zwait, 0)

        def tail_block(i):
            return xs_ref.at[pl.ds(pl.multiple_of(i * blk_rows, blk_rows), blk_rows)]

        def tfill(i, c):
            pltpu.make_async_copy(zero_ref, tail_block(i), zsem).start()
            return c

        def twait(i, c):
            pltpu.make_async_copy(zero_ref, tail_block(i), zsem).wait()
            return c

        n_blk = xs_ref.shape[0] // blk_rows
        lax.fori_loop(nu_ref[0], n_blk, tfill, 0)
        lax.fori_loop(nu_ref[0], n_blk, twait, 0)

    step = pl.program_id(0)
    slot = step % 2
    stage = stage_ref.at[slot]

    def place(g, c):
        for u in range(DMA_UNROLL):
            r = g * DMA_UNROLL + u
            row = h_ref[pl.ds(pl.multiple_of(r * ROW_TILE, ROW_TILE), ROW_TILE), :]
            for k in range(TOP_K):
                p = lpos_ref[r * TOP_K + k]
                stage[pl.ds(pl.multiple_of(p, ROW_TILE), ROW_TILE), :] = row
        return c
    lax.fori_loop(0, n_tok // DMA_UNROLL, place, 0)

    _run_copies(list_ref, xs_ref, stage, sem.at[slot], to_hbm=True)

    def drain(s):
        pltpu.make_async_copy(stage_ref.at[s], stage_ref.at[s], sem.at[s]).wait()

    @pl.when(step > 0)
    def _():
        drain(1 - slot)

    @pl.when(step == pl.num_programs(0) - 1)
    def _():
        drain(slot)


def _dispatch(p_ends, counts, n_used, lpos_flat, lists, hp, n_rows):
    T = hp.shape[0] // ROW_TILE
    grid_spec = pltpu.PrefetchScalarGridSpec(
        num_scalar_prefetch=3,
        grid=(T // TD,),
        in_specs=[pl.BlockSpec((TD * TOP_K,), lambda i, *_: (i,), memory_space=pltpu.SMEM),
                  pl.BlockSpec((LIST_LEN,), lambda i, *_: (i,), memory_space=pltpu.SMEM),
                  pl.BlockSpec((TD * ROW_TILE, LANES), lambda i, *_: (i, 0))],
        out_specs=pl.BlockSpec(memory_space=pl.ANY),
        scratch_shapes=[pltpu.VMEM((FFN_BLK * ROW_TILE, LANES), f32),
                        pltpu.VMEM((2, TD * TOP_K * ROW_TILE, LANES), f32),
                        pltpu.SemaphoreType.DMA((2,)), pltpu.SemaphoreType.DMA(())],
    )
    return pl.pallas_call(
        _dispatch_kernel,
        out_shape=jax.ShapeDtypeStruct((n_rows * ROW_TILE, LANES), f32),
        grid_spec=grid_spec,
        compiler_params=pltpu.CompilerParams(dimension_semantics=("arbitrary",), vmem_limit_bytes=VMEM_LIMIT),
        name="dispatch",
    )(p_ends, counts, n_used, lpos_flat, lists, hp)


def _ffn_kernel(be_ref, nu_ref, nx_ref, par_ref, val_ref, xs_ref, wgu_hbm, bgu_ref, wd_hbm, bd_ref, ys_ref,
                wgu32_ref, wd32_ref, wgu_ref, wd_ref, sem):
    i = pl.program_id(0)
    used = i < nu_ref[0]
    new_expert = (i == 0) | (be_ref[i] != be_ref[jnp.maximum(i - 1, 0)])
    slot = par_ref[i]

    def weight_copies(e, s):
        return (pltpu.make_async_copy(wgu_hbm.at[e], wgu32_ref.at[s], sem.at[0, s]),
                pltpu.make_async_copy(wd_hbm.at[e], wd32_ref.at[s], sem.at[1, s]))

    @pl.when(i == 0)
    def _():
        for cp in weight_copies(be_ref[0], 0):
            cp.start()

    @pl.when(used & new_expert)
    def _():
        for cp in weight_copies(be_ref[i], slot):
            cp.wait()

        @pl.when(nx_ref[i] >= 0)
        def _():
            for cp in weight_copies(nx_ref[i], 1 - slot):
                cp.start()

        rows = 128

        def cast(src, dst):
            def body(r, c):
                r0 = pl.multiple_of(r * rows, rows)
                dst[pl.ds(r0, rows), :] = src[slot, pl.ds(r0, rows), :].astype(bf16)
                return c
            lax.fori_loop(0, src.shape[1] // rows, body, 0)
        cast(wgu32_ref, wgu_ref)
        cast(wd32_ref, wd_ref)

    def ffn_rows(n):
        tiles = pl.ds(0, n * ROW_TILE)
        F = wd_ref.shape[0]
        xrow = _tiles_to_rows(xs_ref.at[tiles], n).astype(bf16)
        acc = None
        fc = F // 2
        for c in range(2):
            def gu(col0):
                return _mm(xrow, wgu_ref[:, col0:col0 + fc]) + bgu_ref[0, :, col0:col0 + fc]
            gate = jnp.minimum(gu(c * fc), SWIGLU_LIMIT)
            up = jnp.clip(gu(F + c * fc), -SWIGLU_LIMIT, SWIGLU_LIMIT)
            y = (up + 1.0) * (gate * jax.nn.sigmoid(SWIGLU_ALPHA * gate))
            part = _mm(y.astype(bf16), wd_ref[c * fc:(c + 1) * fc, :])
            acc = part if acc is None else acc + part
        _rows_to_tiles(acc + bd_ref[0], ys_ref.at[tiles])

    half = FFN_BLK // 2
    occupied = val_ref[i]

    @pl.when(used & (occupied > half))
    def _():
        ffn_rows(FFN_BLK)

    @pl.when(used & (occupied <= half))
    def _():
        ffn_rows(half)
        ys_ref[pl.ds(half * ROW_TILE, half * ROW_TILE), :] = jnp.zeros((half * ROW_TILE, LANES), f32)

    @pl.when(jnp.logical_not(used))
    def _():
        ys_ref[...] = jnp.zeros_like(ys_ref)


def _ffn(block_e, n_used, occupied, xs, w_gate_up, b_gate_up, w_down, b_down):
    E, D, F2 = w_gate_up.shape
    F = F2 // 2
    P = xs.shape[0] // ROW_TILE
    nb = P // FFN_BLK
    rows = FFN_BLK * ROW_TILE

    idx = jnp.arange(nb, dtype=jnp.int32)
    live = idx < n_used[0]
    later_other = (block_e[None, :] != block_e[:, None]) & (idx[None, :] > idx[:, None]) & live[None, :]
    nxt = jnp.where(jnp.any(later_other, axis=1), block_e[jnp.argmax(later_other, axis=1)], -1).astype(jnp.int32)
    starts = jnp.concatenate([jnp.ones((1,), jnp.int32), (block_e[1:] != block_e[:-1]).astype(jnp.int32)])
    parity = ((jnp.cumsum(starts) - 1) % 2).astype(jnp.int32)

    def blk(i, nu):
        return jnp.minimum(i, nu[0] - 1)

    grid_spec = pltpu.PrefetchScalarGridSpec(
        num_scalar_prefetch=5,
        grid=(nb,),
        in_specs=[pl.BlockSpec((rows, LANES), lambda i, be, nu, *_: (blk(i, nu), 0)),
                  pl.BlockSpec(memory_space=pl.ANY),
                  pl.BlockSpec((1, 1, F2), lambda i, be, nu, *_: (be[blk(i, nu)], 0, 0)),
                  pl.BlockSpec(memory_space=pl.ANY),
                  pl.BlockSpec((1, 1, D), lambda i, be, nu, *_: (be[blk(i, nu)], 0, 0))],
        out_specs=pl.BlockSpec((rows, LANES), lambda i, *_: (i, 0)),
        scratch_shapes=[pltpu.VMEM((2, D, F2), f32), pltpu.VMEM((2, F, D), f32),
                        pltpu.VMEM((D, F2), bf16), pltpu.VMEM((F, D), bf16),
                        pltpu.SemaphoreType.DMA((2, 2))],
    )
    return pl.pallas_call(
        _ffn_kernel,
        out_shape=jax.ShapeDtypeStruct((P * ROW_TILE, LANES), f32),
        grid_spec=grid_spec,
        compiler_params=pltpu.CompilerParams(
            dimension_semantics=("arbitrary",), vmem_limit_bytes=VMEM_LIMIT_FFN),
        name="ffn",
    )(block_e, n_used, nxt, parity, occupied, xs, w_gate_up, b_gate_up.reshape(E, 1, F2), w_down,
      b_down.reshape(E, 1, D))


def _combine_kernel(lpos_ref, lcur_ref, lnext_ref, ys_ref, x1_ref, rw_ref, mod_ref, o_ref,
                    stage_ref, acc_ref, wb_ref, sem):
    step = pl.program_id(0) * pl.num_programs(1) + pl.program_id(1)
    n_steps = pl.num_programs(0) * pl.num_programs(1)
    slot = step % 2

    def fetch(list_ref, s):
        _run_copies(list_ref, ys_ref, stage_ref.at[s], sem.at[s], to_hbm=False)

    @pl.when(step == 0)
    def _():
        fetch(lcur_ref, 0)

    @pl.when(step + 1 < n_steps)
    def _():
        fetch(lnext_ref, 1 - slot)

    pltpu.make_async_copy(stage_ref.at[slot], stage_ref.at[slot], sem.at[slot]).wait()

    rw = rw_ref[...]
    for k in range(TOP_K):
        wb_ref[k] = jnp.broadcast_to(rw[:, k:k + 1], (TD, LANES))

    def staged(r, k):
        p = lpos_ref[r * TOP_K + k]
        return stage_ref[slot, pl.ds(pl.multiple_of(p, ROW_TILE), ROW_TILE), :]

    def token(r, c):
        acc = wb_ref[0, pl.ds(r, 1), :] * staged(r, 0)
        for k in range(1, TOP_K):
            acc = acc + wb_ref[k, pl.ds(r, 1), :] * staged(r, k)
        acc_ref[pl.ds(pl.multiple_of(r * ROW_TILE, ROW_TILE), ROW_TILE), :] = acc
        return c
    lax.fori_loop(0, TD, token, 0, unroll=DMA_UNROLL)
    o_ref[0] = x1_ref[0] + mod_ref[0, 5:6, :] * _tiles_to_rows(acc_ref, TD)


def _combine(lpos_flat, lists, ys, x1, rw, mod):
    B, S, D = x1.shape
    nj = S // TD
    n_steps = B * nj
    return pl.pallas_call(
        _combine_kernel,
        out_shape=jax.ShapeDtypeStruct((B, S, D), f32),
        grid=(B, nj),
        in_specs=[pl.BlockSpec((TD * TOP_K,), lambda b, j: (b * nj + j,), memory_space=pltpu.SMEM),
                  pl.BlockSpec((LIST_LEN,), lambda b, j: (b * nj + j,), memory_space=pltpu.SMEM),
                  pl.BlockSpec((LIST_LEN,), lambda b, j: (jnp.minimum(b * nj + j + 1, n_steps - 1),),
                               memory_space=pltpu.SMEM),
                  pl.BlockSpec(memory_space=pl.ANY),
                  pl.BlockSpec((1, TD, D), lambda b, j: (b, j, 0)),
                  pl.BlockSpec((TD, LANES), lambda b, j: (b * nj + j, 0)),
                  pl.BlockSpec((1, 6, D), lambda b, j: (b, 0, 0))],
        out_specs=pl.BlockSpec((1, TD, D), lambda b, j: (b, j, 0)),
        scratch_shapes=[pltpu.VMEM((2, TD * TOP_K * ROW_TILE, LANES), f32),
                        pltpu.VMEM((TD * ROW_TILE, LANES), f32), pltpu.VMEM((TOP_K, TD, LANES), f32),
                        pltpu.SemaphoreType.DMA((2,))],
        compiler_params=pltpu.CompilerParams(
            dimension_semantics=("arbitrary", "arbitrary"), vmem_limit_bytes=VMEM_LIMIT),
        name="combine",
    )(lpos_flat, lists, lists, ys, x1, rw, mod)


def _moe(hp, logits, x1, mod, w_gate_up, b_gate_up, w_down, b_down):
    T = logits.shape[0]
    E = w_gate_up.shape[0]
    ri, rw, cnt, snap = _route(logits)
    lpos = (ri[:, :TOP_K] * ROW_TILE).reshape(-1)
    counts = cnt[0, :E].astype(jnp.int32)
    padded = ((counts + FFN_BLK - 1) // FFN_BLK) * FFN_BLK
    p_ends = jnp.cumsum(padded)
    p_starts = p_ends - padded
    nb = -(-T * TOP_K // FFN_BLK) + E
    n_used = jnp.maximum(p_ends[-1:] // FFN_BLK, 1).astype(jnp.int32)
    blk_start = jnp.arange(nb, dtype=jnp.int32) * FFN_BLK
    block_e = jnp.minimum(jnp.sum(p_ends[None, :] <= blk_start[:, None], axis=1), E - 1).astype(jnp.int32)
    assert E == N_RUN
    base = snap.reshape(T // TD, LANES)[:, :E].astype(jnp.int32)
    run_n = jnp.concatenate([base[1:], counts[None, :]], axis=0) - base
    run_off = jnp.cumsum(run_n, axis=1) - run_n
    run_dst = p_starts[None, :].astype(jnp.int32) + base
    lists = _copy_lists(run_dst, run_n, run_off)
    xs = _dispatch(p_ends.astype(jnp.int32), counts, n_used, lpos, lists, hp, nb * FFN_BLK)
    occupied = jnp.clip(counts[block_e] - (blk_start - p_starts[block_e]), 0, FFN_BLK).astype(jnp.int32)
    ys = _ffn(block_e, n_used, occupied, xs, w_gate_up, b_gate_up, w_down, b_down)
    return _combine(lpos, lists, ys, x1, rw, mod)


def kernel(x, c, rel_bias_table, w_ada, b_ada, g_norm1, w_in, w_gk_up, b_gk_up, g_gla_out, g_qnorm, g_knorm, lambda_q1, lambda_k1, lambda_q2, lambda_k2, g_subln, w_out, g_norm2, w_router, b_router, w_gate_up, b_gate_up, w_down, b_down):
    B, S, D = x.shape
    depth = w_ada.shape[0]
    bias_tiles = _bias_tiles(rel_bias_table, S, min(TQ, S))
    for l in range(depth):
        lambda_init = 0.8 - 0.6 * math.exp(-0.3 * l)
        mod = _ada(c, w_ada[l], b_ada[l])
        qg, kg, gk, kgt, gkt, vg, rg, qd, kd, vd = _inproj(
            x, mod, g_norm1[l], w_in[l], w_gk_up[l], b_gk_up[l], g_qnorm[l], g_knorm[l])
        og = _gla(qg, kg, gk, kgt, gkt, vg, rg, g_gla_out[l])
        lamv = jnp.stack([lambda_q1[l], lambda_k1[l], lambda_q2[l], lambda_k2[l]]).astype(f32)
        od = lax.cond(_scores_bounded(rel_bias_table, g_qnorm[l], g_knorm[l]),
                      functools.partial(_attn, lambda_init=lambda_init, bounded=True),
                      functools.partial(_attn, lambda_init=lambda_init, bounded=False),
                      qd, kd, vd, bias_tiles, lamv, g_subln[l])
        x1, hp, logits = _outproj(og, od, x, mod, w_out[l], g_norm2[l], w_router[l], b_router[l])
        x = _moe(hp, logits, x1, mod, w_gate_up[l], b_gate_up[l], w_down[l], b_down[l])
    return x
```

```python
import functools
import math

import jax
import jax.numpy as jnp
from jax import lax
from jax.experimental import pallas as pl
from jax.experimental.pallas import tpu as pltpu

f32 = jnp.float32
bf16 = jnp.bfloat16

N_GLA_HEADS = 4
GLA_DK = 64
GLA_DV = 128
GLA_GATE_RANK = 16
GLA_GATE_NORM = 16.0
GLA_CHUNK = 64
N_DIFF_HEADS = 4
DIFF_DQK = 64
DIFF_DV = 128
NUM_BUCKETS = 32
MAX_DISTANCE = 128
TOP_K = 4
SWIGLU_LIMIT = 7.0
SWIGLU_ALPHA = 1.702
EPS = 1e-6

GLA_QK_W = N_GLA_HEADS * GLA_DK
GLA_V_W = N_GLA_HEADS * GLA_DV
DIFF_QK_W = N_DIFF_HEADS * 2 * DIFF_DQK
DIFF_V_W = N_DIFF_HEADS * DIFF_DV

LANES = 128
NEG = -1e30
LOG2E = math.log2(math.e)
SAFE_SCORE = 40.0
NORM_SLACK = 1.02
VMEM_LIMIT = 48 * 1024 * 1024
VMEM_LIMIT_FFN = 58 * 1024 * 1024
VMEM_LIMIT_ATTN = 58 * 1024 * 1024

TM_IN = 512
INPROJ_SUB = 2
TG_GLA = 1024
PAIR = 2 * GLA_CHUNK
GLA_UNROLL = 4
TQ = 512
ATTN_UNROLL = 4
ATTN_HEADS = 2
TR = 512
TD = 512
ROW_TILE = 8
DMA_UNROLL = 8
N_RUN = 32
RUN_SIZES = tuple(TD >> b for b in range(TD.bit_length()))
LIST_DST = len(RUN_SIZES) * N_RUN
LIST_CNT = 2 * LIST_DST
LIST_LEN = 1024
assert LIST_CNT + len(RUN_SIZES) <= LIST_LEN
FFN_BLK = 512


def _nt(a, b):
    return lax.dot_general(a, b, (((1,), (1,)), ((), ())), preferred_element_type=f32)


def _mm(a, b):
    return jnp.dot(a, b, preferred_element_type=f32)


def _split(x):
    hi = x.astype(bf16)
    lo = (x - hi.astype(f32)).astype(bf16)
    return hi, lo


def _silu(x):
    return x * jax.nn.sigmoid(x)


def _ada_kernel(c_ref, w_ref, b_ref, o_ref):
    c = c_ref[...]
    o_ref[...] = _mm(_silu(c).astype(bf16), w_ref[...].astype(bf16)) + b_ref[...]


def _ada(c, w_ada, b_ada):
    B, D = c.shape
    N = w_ada.shape[1]
    bp = ROW_TILE
    assert B <= bp
    cp = jnp.zeros((bp, D), f32).at[:B].set(c)
    tn = N // 4
    out = pl.pallas_call(
        _ada_kernel,
        out_shape=jax.ShapeDtypeStruct((bp, N), f32),
        grid=(N // tn,),
        in_specs=[pl.BlockSpec((bp, D), lambda j: (0, 0)),
                  pl.BlockSpec((D, tn), lambda j: (0, j)),
                  pl.BlockSpec((1, tn), lambda j: (0, j))],
        out_specs=pl.BlockSpec((bp, tn), lambda j: (0, j)),
        compiler_params=pltpu.CompilerParams(vmem_limit_bytes=VMEM_LIMIT),
        name="ada",
    )(cp, w_ada, b_ada.reshape(1, N))
    return out[:B].reshape(B, 6, D)


def _inproj_kernel(x_ref, mod_ref, g1_ref, wm_ref, wkt_ref, wlo_ref, wup_ref, wupt_ref,
                   bup_ref, bupt_ref, gqk_ref, grp_ref, grpt_ref,
                   qg_ref, kg_ref, gk_ref, kgt_ref, gkt_ref, vg_ref, rg_ref,
                   qd_ref, kd_ref, vd_ref):
    tm = x_ref.shape[1]
    sub = tm // INPROJ_SUB
    for t in range(INPROJ_SUB):
        _inproj_rows(slice(t * sub, (t + 1) * sub), x_ref, mod_ref, g1_ref, wm_ref, wkt_ref, wlo_ref, wup_ref,
                     wupt_ref, bup_ref, bupt_ref, gqk_ref, grp_ref, grpt_ref, qg_ref, kg_ref, gk_ref, kgt_ref,
                     gkt_ref, vg_ref, rg_ref, qd_ref, kd_ref, vd_ref)


def _inproj_rows(rows, x_ref, mod_ref, g1_ref, wm_ref, wkt_ref, wlo_ref, wup_ref, wupt_ref,
                 bup_ref, bupt_ref, gqk_ref, grp_ref, grpt_ref,
                 qg_ref, kg_ref, gk_ref, kgt_ref, gkt_ref, vg_ref, rg_ref, qd_ref, kd_ref, vd_ref):
    x = x_ref[0, rows, :]
    ms = jnp.mean(x * x, axis=-1, keepdims=True)
    y = x * lax.rsqrt(ms + EPS) * g1_ref[...]
    h = (y * (1.0 + mod_ref[0, 1:2, :]) + mod_ref[0, 0:1, :]).astype(bf16)

    def proj(a, b):
        return _mm(h, wm_ref[:, a:b])

    o = 0
    qg_ref[0, rows, :] = proj(o, o + GLA_QK_W); o += GLA_QK_W
    kg_ref[0, rows, :] = proj(o, o + GLA_QK_W); o += GLA_QK_W
    vg_ref[0, rows, :] = proj(o, o + GLA_V_W).astype(bf16); o += GLA_V_W
    rg_ref[0, rows, :] = proj(o, o + GLA_V_W); o += GLA_V_W
    qk = proj(o, o + 2 * DIFF_QK_W); o += 2 * DIFF_QK_W
    vd_ref[0, rows, :] = proj(o, o + DIFF_V_W).astype(bf16)

    slab0 = rows.start // PAIR
    kgt = _nt(wkt_ref[...], h)
    for j in range(kgt.shape[1] // PAIR):
        kgt_ref[0, slab0 + j] = kgt[:, j * PAIR:(j + 1) * PAIR]

    lo = _mm(h, wlo_ref[...]).astype(bf16)
    z = _mm(lo, wup_ref[...]) + bup_ref[...]
    gk_ref[0, rows, :] = (jnp.minimum(z, 0.0) - jnp.log1p(jnp.exp(-jnp.abs(z)))) * (1.0 / GLA_GATE_NORM)
    zt = _nt(wupt_ref[...], lo) + bupt_ref[...]
    gkt = (jnp.minimum(zt, 0.0) - jnp.log1p(jnp.exp(-jnp.abs(zt)))) * (1.0 / GLA_GATE_NORM)
    for j in range(gkt.shape[1] // PAIR):
        gkt_ref[0, slab0 + j] = gkt[:, j * PAIR:(j + 1) * PAIR]

    sq_hi, sq_lo = _split(qk * qk)
    gs = _mm(sq_hi, grp_ref[...]) + _mm(sq_lo, grp_ref[...])
    r = lax.rsqrt(gs * (1.0 / DIFF_DQK) + EPS)
    r_hi, r_lo = _split(r)
    rb = _mm(r_hi, grpt_ref[...]) + _mm(r_lo, grpt_ref[...])
    qkn = qk * rb * gqk_ref[...]
    qd_ref[0, rows, :] = qkn[:, :DIFF_QK_W].astype(bf16)
    kd_ref[0, rows, :] = qkn[:, DIFF_QK_W:].astype(bf16)


def _inproj(x, mod, g_norm1, w_in, w_gk_up, b_gk_up, g_qnorm, g_knorm):
    B, S, D = x.shape
    offs = [0]
    for w in (GLA_QK_W, GLA_QK_W, GLA_V_W, GLA_V_W, GLA_GATE_RANK, DIFF_QK_W, DIFF_QK_W, DIFF_V_W):
        offs.append(offs[-1] + w)
    w_main = jnp.concatenate([w_in[:, offs[0]:offs[4]], w_in[:, offs[5]:offs[8]]], axis=1).astype(bf16)
    w_kt = w_in[:, offs[1]:offs[2]].T.astype(bf16)
    w_lo = jnp.zeros((D, LANES), f32).at[:, :GLA_GATE_RANK].set(w_in[:, offs[4]:offs[5]]).astype(bf16)
    w_up = jnp.zeros((LANES, GLA_QK_W), f32).at[:GLA_GATE_RANK].set(w_gk_up).astype(bf16)
    w_upt = w_up.T
    b_up = b_gk_up.reshape(1, GLA_QK_W)
    b_upt = b_gk_up.reshape(GLA_QK_W, 1)
    n_grp = 2 * DIFF_QK_W // DIFF_DQK
    gqk = jnp.concatenate([jnp.tile(g_qnorm, n_grp // 2) * (DIFF_DQK ** -0.5 * LOG2E),
                           jnp.tile(g_knorm, n_grp // 2)]).reshape(1, 2 * DIFF_QK_W)
    grp = (jnp.arange(2 * DIFF_QK_W)[:, None] // DIFF_DQK == jnp.arange(LANES)[None, :]).astype(bf16)
    grpt = grp.T
    nw = w_main.shape[1]
    tm = TM_IN
    const = lambda shape: pl.BlockSpec(shape, lambda b, i: (0,) * len(shape))
    row = lambda w: pl.BlockSpec((1, tm, w), lambda b, i: (b, i, 0))
    colT = pl.BlockSpec((1, tm // PAIR, GLA_QK_W, PAIR), lambda b, i: (b, i, 0, 0))
    outs = pl.pallas_call(
        _inproj_kernel,
        out_shape=[jax.ShapeDtypeStruct((B, S, GLA_QK_W), f32),
                   jax.ShapeDtypeStruct((B, S, GLA_QK_W), f32),
                   jax.ShapeDtypeStruct((B, S, GLA_QK_W), f32),
                   jax.ShapeDtypeStruct((B, S // PAIR, GLA_QK_W, PAIR), f32),
                   jax.ShapeDtypeStruct((B, S // PAIR, GLA_QK_W, PAIR), f32),
                   jax.ShapeDtypeStruct((B, S, GLA_V_W), bf16),
                   jax.ShapeDtypeStruct((B, S, GLA_V_W), f32),
                   jax.ShapeDtypeStruct((B, S, DIFF_QK_W), bf16),
                   jax.ShapeDtypeStruct((B, S, DIFF_QK_W), bf16),
                   jax.ShapeDtypeStruct((B, S, DIFF_V_W), bf16)],
        grid=(B, S // tm),
        in_specs=[row(D),
                  pl.BlockSpec((1, 6, D), lambda b, i: (b, 0, 0)),
                  const((1, D)), const((D, nw)), const((GLA_QK_W, D)), const((D, LANES)),
                  const((LANES, GLA_QK_W)), const((GLA_QK_W, LANES)),
                  const((1, GLA_QK_W)), const((GLA_QK_W, 1)),
                  const((1, 2 * DIFF_QK_W)), const((2 * DIFF_QK_W, LANES)),
                  const((LANES, 2 * DIFF_QK_W))],
        out_specs=[row(GLA_QK_W), row(GLA_QK_W), row(GLA_QK_W), colT, colT,
                   row(GLA_V_W), row(GLA_V_W), row(DIFF_QK_W), row(DIFF_QK_W), row(DIFF_V_W)],
        compiler_params=pltpu.CompilerParams(
            dimension_semantics=("arbitrary", "arbitrary"), vmem_limit_bytes=VMEM_LIMIT),
        name="inproj",
    )(x, mod, g_norm1.reshape(1, D), w_main, w_kt, w_lo, w_up, w_upt, b_up, b_upt, gqk, grp, grpt)
    return outs


def _gla_kernel(q_ref, k_ref, g_ref, kt_ref, gt_ref, v_ref, r_ref, gout_ref, tri_ref, trit_ref,
                o_ref, s_ref, *, n_pairs):
    H, DK, DV = N_GLA_HEADS, GLA_DK, GLA_DV

    @pl.when(pl.program_id(1) == 0)
    def _():
        s_ref[...] = jnp.zeros_like(s_ref)

    tri = tri_ref[...]
    trit = trit_ref[...]
    tri_b = tri > 0
    lane_head = lax.broadcasted_iota(jnp.int32, (1, H * DK), 1) // DK
    row_head = lax.broadcasted_iota(jnp.int32, (H * PAIR, 1), 0) // PAIR
    qmask = row_head == lane_head
    row_first = lax.broadcasted_iota(jnp.int32, (PAIR, 1), 0) < GLA_CHUNK
    row_first4 = (lax.broadcasted_iota(jnp.int32, (H * PAIR, 1), 0) % PAIR) < GLA_CHUNK
    lane_first = lax.broadcasted_iota(jnp.int32, (1, PAIR), 1) < GLA_CHUNK
    scale = DK ** -0.5
    gout = gout_ref[...]

    def pair(p, carry):
        r0 = pl.multiple_of(p * PAIR, PAIR)
        q = q_ref[0, pl.ds(r0, PAIR), :]
        k = k_ref[0, pl.ds(r0, PAIR), :]
        g = g_ref[0, pl.ds(r0, PAIR), :]
        kt = kt_ref[0, p]
        gt = gt_ref[0, p]
        v = v_ref[0, pl.ds(r0, PAIR), :]

        g_hi, g_lo = _split(g)
        gc = _mm(tri, g_hi) + _mm(tri, g_lo)
        gt_hi, gt_lo = _split(gt)
        gct = _mm(gt_hi, trit) + _mm(gt_lo, trit)
        g_last = jnp.where(row_first, gc[GLA_CHUNK - 1:GLA_CHUNK, :], gc[PAIR - 1:PAIR, :])
        gl0 = gct[:, GLA_CHUNK - 1:GLA_CHUNK]
        gl1 = gct[:, PAIR - 1:PAIR]
        g_last_t = jnp.where(lane_first, gl0, gl1)

        q_e = (q * (jnp.exp(gc) * scale)).astype(bf16)
        k_e = (k * jnp.exp(-gc)).astype(bf16)
        ks_t = kt * jnp.exp(g_last_t - gct)
        ks_t0 = jnp.where(lane_first, ks_t, 0.0).astype(bf16)
        ks_t1 = jnp.where(lane_first, 0.0, ks_t).astype(bf16)
        del g_last

        qm = jnp.where(qmask, jnp.concatenate([q_e] * H, axis=0), jnp.zeros((), bf16))
        a = _nt(qm, k_e)
        s0 = s_ref[...]

        u0 = []
        u1 = []
        for h in range(H):
            v_h = v[:, h * DV:(h + 1) * DV]
            u0.append(_mm(ks_t0[h * DK:(h + 1) * DK], v_h))
            u1.append(_mm(ks_t1[h * DK:(h + 1) * DK], v_h))
        u0 = jnp.concatenate(u0, axis=0)
        u1 = jnp.concatenate(u1, axis=0)
        s1 = s0 * jnp.exp(gl0) + u0
        s_ref[...] = s1 * jnp.exp(gl1) + u1

        o_inter = jnp.where(row_first4, _mm(qm, s0.astype(bf16)), _mm(qm, s1.astype(bf16)))
        for h in range(H):
            a_h = jnp.where(tri_b, a[h * PAIR:(h + 1) * PAIR], 0.0).astype(bf16)
            o_h = _mm(a_h, v[:, h * DV:(h + 1) * DV]) + o_inter[h * PAIR:(h + 1) * PAIR]
            ms = jnp.mean(o_h * o_h, axis=-1, keepdims=True)
            o_n = o_h * lax.rsqrt(ms + EPS) * gout
            r_h = r_ref[0, pl.ds(r0, PAIR), h * DV:(h + 1) * DV]
            o_ref[0, pl.ds(r0, PAIR), h * DV:(h + 1) * DV] = (o_n * _silu(r_h)).astype(bf16)
        return carry

    lax.fori_loop(0, n_pairs, pair, 0, unroll=GLA_UNROLL)


def _gla(qg, kg, gk, kgt, gkt, vg, rg, g_gla_out):
    B, S, _ = qg.shape
    tg = min(TG_GLA, S)
    r = jnp.arange(PAIR)
    tri = ((r[:, None] // GLA_CHUNK == r[None, :] // GLA_CHUNK) & (r[None, :] <= r[:, None])).astype(bf16)
    row = lambda w: pl.BlockSpec((1, tg, w), lambda b, i: (b, i, 0))
    colT = pl.BlockSpec((1, tg // PAIR, GLA_QK_W, PAIR), lambda b, i: (b, i, 0, 0))
    const = lambda shape: pl.BlockSpec(shape, lambda b, i: (0,) * len(shape))
    return pl.pallas_call(
        functools.partial(_gla_kernel, n_pairs=tg // PAIR),
        out_shape=jax.ShapeDtypeStruct((B, S, GLA_V_W), bf16),
        grid=(B, S // tg),
        in_specs=[row(GLA_QK_W), row(GLA_QK_W), row(GLA_QK_W), colT, colT,
                  row(GLA_V_W), row(GLA_V_W), const((1, GLA_DV)),
                  const((PAIR, PAIR)), const((PAIR, PAIR))],
        out_specs=row(GLA_V_W),
        scratch_shapes=[pltpu.VMEM((GLA_QK_W, GLA_DV), f32)],
        compiler_params=pltpu.CompilerParams(
            dimension_semantics=("arbitrary", "arbitrary"), vmem_limit_bytes=VMEM_LIMIT),
        name="gla",
    )(qg, kg, gk, kgt, gkt, vg, rg, g_gla_out.reshape(1, GLA_DV), tri, tri.T)


def _attn_finish(o, gsub_ref, o_ref, lambda_init):
    ms = jnp.mean(o * o, axis=-1, keepdims=True)
    o_ref[0] = (o * lax.rsqrt(ms + EPS) * gsub_ref[...] * (1.0 - lambda_init)).astype(bf16)


def _attn_lambda(lamv_ref, lambda_init):
    lv = lamv_ref[...]
    return (jnp.exp(jnp.sum(lv[0:1] * lv[1:2], axis=-1, keepdims=True))
            - jnp.exp(jnp.sum(lv[2:3] * lv[3:4], axis=-1, keepdims=True)) + lambda_init)


def _attn_bounded_kernel(q_ref, k_ref, v_ref, bias_ref, lamv_ref, gsub_ref, o_ref, vaug_ref, *, lambda_init):
    qi = pl.program_id(2)
    tq = q_ref.shape[1]
    S = k_ref.shape[1]
    n_head = q_ref.shape[2] // (2 * DIFF_DQK)

    @pl.when(qi == 0)
    def _():
        lane = lax.broadcasted_iota(jnp.int32, (S, DIFF_DV), 1)
        for h in range(n_head):
            vaug_ref[h, :, :DIFF_DV] = v_ref[0, :, h * DIFF_DV:(h + 1) * DIFF_DV]
            vaug_ref[h, :, DIFF_DV:] = jnp.where(lane == 0, 1.0, 0.0).astype(bf16)

    lane = lax.broadcasted_iota(jnp.int32, (1, 2 * DIFF_DQK), 1)
    zero = jnp.zeros((), bf16)
    qs = []
    for h in range(n_head):
        q = q_ref[0, :, h * 2 * DIFF_DQK:(h + 1) * 2 * DIFF_DQK]
        qs.append((jnp.where(lane < DIFF_DQK, q, zero), jnp.where(lane < DIFF_DQK, zero, q)))

    def update(accs, k0, bias):
        out = []
        for h in range(n_head):
            kb = k_ref[0, pl.ds(k0, tq), h * 2 * DIFF_DQK:(h + 1) * 2 * DIFF_DQK]
            vb = vaug_ref[h, pl.ds(k0, tq), :]
            for c in range(2):
                s = _nt(qs[h][c], kb)
                if bias is not None:
                    s = s + bias[h][c]
                out.append(accs[2 * h + c] + _mm(jnp.exp2(s).astype(bf16), vb))
        return tuple(out)

    def far(kj, accs):
        return update(accs, pl.multiple_of(kj * tq, tq), None)

    def far_group(g, accs):
        for u in range(ATTN_UNROLL):
            accs = far(g * ATTN_UNROLL + u, accs)
        return accs

    def block_or_masked(accs, kj, rel):
        exists = kj >= 0
        k0 = pl.multiple_of(jnp.maximum(kj, 0) * tq, tq)
        if rel is None:
            tiles = [(jnp.where(exists, 0.0, NEG),) * 2] * n_head
        else:
            tiles = [tuple(jnp.where(exists, bias_ref[h, c, rel], NEG) for c in range(2)) for h in range(n_head)]
        return update(accs, k0, tiles)

    accs = tuple(jnp.zeros((tq, 2 * DIFF_DV), f32) for _ in range(2 * n_head))
    accs = block_or_masked(accs, qi, 1)
    accs = block_or_masked(accs, qi - 1, 0)
    for u in range(2, ATTN_UNROLL):
        accs = block_or_masked(accs, qi - u, None)
    n_far = jnp.maximum(qi + 1 - ATTN_UNROLL, 0)
    n_grp = n_far // ATTN_UNROLL
    accs = lax.fori_loop(0, n_grp, far_group, accs)
    accs = lax.fori_loop(n_grp * ATTN_UNROLL, n_far, far, accs)
    lam = _attn_lambda(lamv_ref, lambda_init)
    for h in range(n_head):
        a0, a1 = accs[2 * h], accs[2 * h + 1]
        o = a0[:, :DIFF_DV] / a0[:, DIFF_DV:DIFF_DV + 1] - lam * (a1[:, :DIFF_DV] / a1[:, DIFF_DV:DIFF_DV + 1])
        ms = jnp.mean(o * o, axis=-1, keepdims=True)
        o_ref[0, :, h * DIFF_DV:(h + 1) * DIFF_DV] = (
            o * lax.rsqrt(ms + EPS) * gsub_ref[...] * (1.0 - lambda_init)).astype(bf16)


def _attn_kernel(q_ref, k_ref, v_ref, bias_ref, lamv_ref, gsub_ref, o_ref, *, lambda_init):
    qi = pl.program_id(2)
    tq = q_ref.shape[1]
    q = q_ref[0]
    lane = lax.broadcasted_iota(jnp.int32, (1, 2 * DIFF_DQK), 1)
    zero = jnp.zeros((), bf16)
    qs = (jnp.where(lane < DIFF_DQK, q, zero), jnp.where(lane < DIFF_DQK, zero, q))

    def update(state, kb, vb, bias):
        new = []
        for c in range(2):
            m, l, acc = state[c]
            s = _nt(qs[c], kb)
            if bias is not None:
                s = s + bias[c]
            m_new = jnp.maximum(m, jnp.max(s, axis=-1, keepdims=True))
            alpha = jnp.exp2(m - m_new)
            p = jnp.exp2(s - m_new)
            l = alpha * l + jnp.sum(p, axis=-1, keepdims=True)
            acc = alpha * acc + _mm(p.astype(bf16), vb)
            new.append((m_new, l, acc))
        return tuple(new)

    init = tuple((jnp.full((tq, 1), NEG, f32), jnp.zeros((tq, 1), f32), jnp.zeros((tq, DIFF_DV), f32))
                 for _ in range(2))

    def far(kj, state):
        k0 = pl.multiple_of(kj * tq, tq)
        return update(state, k_ref[0, pl.ds(k0, tq), :], v_ref[0, pl.ds(k0, tq), :], None)

    state = lax.fori_loop(0, jnp.maximum(qi - 1, 0), far, init)

    kd0 = pl.multiple_of(qi * tq, tq)
    state = update(state, k_ref[0, pl.ds(kd0, tq), :], v_ref[0, pl.ds(kd0, tq), :],
                   (bias_ref[0, 0, 1], bias_ref[0, 1, 1]))
    kp0 = pl.multiple_of(jnp.maximum(qi - 1, 0) * tq, tq)
    has_prev = qi > 0
    state = update(state, k_ref[0, pl.ds(kp0, tq), :], v_ref[0, pl.ds(kp0, tq), :],
                   (jnp.where(has_prev, bias_ref[0, 0, 0], NEG), jnp.where(has_prev, bias_ref[0, 1, 0], NEG)))

    (_, l0, a0), (_, l1, a1) = state
    o = a0 / l0 - _attn_lambda(lamv_ref, lambda_init) * (a1 / l1)
    _attn_finish(o, gsub_ref, o_ref, lambda_init)


def _t5_bucket(n):
    max_exact = NUM_BUCKETS // 2
    nf = jnp.maximum(n, 1).astype(f32)
    large = max_exact + (jnp.log(nf / max_exact) / math.log(MAX_DISTANCE / max_exact)
                         * (NUM_BUCKETS - max_exact)).astype(jnp.int32)
    large = jnp.minimum(large, NUM_BUCKETS - 1)
    return jnp.where(n < max_exact, n, large)


def _toeplitz_kernel(w_ref, o_ref):
    n = o_ref.shape[-1]
    for t in range(2):
        rows = jnp.broadcast_to(w_ref[0, t:t + 1, :], (n, 2 * n))
        o_ref[0, 0, t] = pltpu.roll(rows, 0, 1, stride=1, stride_axis=0)[:, n:]


def _bias_tiles(rel_bias_table, S, n):
    HM = rel_bias_table.shape[1]
    assert n >= MAX_DISTANCE
    d = jnp.arange(2 * n, dtype=jnp.int32)
    by_dist = rel_bias_table[_t5_bucket(d)].astype(f32).T
    rel = (by_dist - rel_bias_table[NUM_BUCKETS - 1].astype(f32)[:, None]) * LOG2E
    i = jnp.arange(2 * n)
    w_diag = jnp.where(i[None, :] <= n, rel[:, jnp.clip(n - i, 0, 2 * n - 1)], NEG)
    w_prev = rel[:, jnp.clip(2 * n - i, 0, 2 * n - 1)]
    w = jnp.stack([w_prev, w_diag], axis=1)
    return pl.pallas_call(
        _toeplitz_kernel,
        out_shape=jax.ShapeDtypeStruct((HM // 2, 2, 2, n, n), f32),
        grid=(HM // 2, 2),
        in_specs=[pl.BlockSpec((1, 2, 2 * n), lambda h, m: (h * 2 + m, 0, 0))],
        out_specs=pl.BlockSpec((1, 1, 2, n, n), lambda h, m: (h, m, 0, 0, 0)),
        compiler_params=pltpu.CompilerParams(vmem_limit_bytes=VMEM_LIMIT),
        name="bias_tiles",
    )(w)


def _attn(qd, kd, vd, bias_tiles, lamv, g_subln, lambda_init, bounded):
    B, S, _ = qd.shape
    H = N_DIFF_HEADS
    tq = min(TQ, S)
    body = _attn_bounded_kernel if bounded else _attn_kernel
    hs = ATTN_HEADS if bounded else 1
    scratch = [pltpu.VMEM((hs, S, 2 * DIFF_DV), bf16)] if bounded else []
    return pl.pallas_call(
        functools.partial(body, lambda_init=lambda_init),
        out_shape=jax.ShapeDtypeStruct((B, S, DIFF_V_W), bf16),
        scratch_shapes=scratch,
        grid=(B, H // hs, S // tq),
        in_specs=[pl.BlockSpec((1, tq, hs * 2 * DIFF_DQK), lambda b, h, i: (b, i, h)),
                  pl.BlockSpec((1, S, hs * 2 * DIFF_DQK), lambda b, h, i: (b, 0, h)),
                  pl.BlockSpec((1, S, hs * DIFF_DV), lambda b, h, i: (b, 0, h)),
                  pl.BlockSpec((hs, 2, 2, tq, tq), lambda b, h, i: (h, 0, 0, 0, 0)),
                  pl.BlockSpec((4, DIFF_DQK), lambda b, h, i: (0, 0)),
                  pl.BlockSpec((1, DIFF_DV), lambda b, h, i: (0, 0))],
        out_specs=pl.BlockSpec((1, tq, hs * DIFF_DV), lambda b, h, i: (b, i, h)),
        compiler_params=pltpu.CompilerParams(
            dimension_semantics=("arbitrary", "arbitrary", "arbitrary"),
            vmem_limit_bytes=VMEM_LIMIT_ATTN if bounded else VMEM_LIMIT),
        name="attn_bounded" if bounded else "attn_online",
    )(qd, kd, vd, bias_tiles, lamv, g_subln.reshape(1, DIFF_DV))


def _scores_bounded(rel_bias_table, g_qnorm, g_knorm):
    qk = DIFF_DQK ** 0.5 * jnp.max(jnp.abs(g_qnorm)) * jnp.max(jnp.abs(g_knorm)) * NORM_SLACK
    rel = jnp.max(jnp.abs(rel_bias_table - rel_bias_table[NUM_BUCKETS - 1:]))
    return qk + rel <= SAFE_SCORE


def _rows_to_tiles(x, ref):
    n = x.shape[0]
    for c in range(ROW_TILE):
        ref[pl.ds(c, n, stride=ROW_TILE), :] = x[:, c * LANES:(c + 1) * LANES]


def _tiles_to_rows(ref, n):
    return jnp.concatenate([ref[pl.ds(c, n, stride=ROW_TILE), :] for c in range(ROW_TILE)], axis=1)


def _outproj_kernel(og_ref, od_ref, x_ref, mod_ref, wo_ref, g2_ref, wr_ref, br_ref,
                    x1_ref, hp_ref, lg_ref):
    half = og_ref.shape[2]
    sub = og_ref.shape[1] // INPROJ_SUB
    for t in range(INPROJ_SUB):
        rows = slice(t * sub, (t + 1) * sub)
        mix = _mm(og_ref[0, rows, :], wo_ref[:half, :]) + _mm(od_ref[0, rows, :], wo_ref[half:, :])
        x1 = x_ref[0, rows, :] + mod_ref[0, 2:3, :] * mix
        x1_ref[0, rows, :] = x1
        ms = jnp.mean(x1 * x1, axis=-1, keepdims=True)
        y = x1 * lax.rsqrt(ms + EPS) * g2_ref[...]
        h = (y * (1.0 + mod_ref[0, 4:5, :]) + mod_ref[0, 3:4, :]).astype(bf16)
        lg_ref[rows, :] = _mm(h, wr_ref[...]) + br_ref[...]
        _rows_to_tiles(h.astype(f32), hp_ref.at[pl.ds(t * sub * ROW_TILE, sub * ROW_TILE)])


def _outproj(og, od, x, mod, w_out, g_norm2, w_router, b_router):
    B, S, D = x.shape
    assert D == ROW_TILE * LANES, "the token-tile layout needs a model row to fill one (8,128) tile"
    E = w_router.shape[1]
    tm = TM_IN
    nj = S // tm
    w_r = jnp.zeros((D, LANES), f32).at[:, :E].set(w_router).astype(bf16)
    b_r = jnp.full((1, LANES), NEG, f32).at[0, :E].set(b_router)
    const = lambda shape: pl.BlockSpec(shape, lambda b, i: (0,) * len(shape))
    return pl.pallas_call(
        _outproj_kernel,
        out_shape=[jax.ShapeDtypeStruct((B, S, D), f32),
                   jax.ShapeDtypeStruct((B * S * ROW_TILE, LANES), f32),
                   jax.ShapeDtypeStruct((B * S, LANES), f32)],
        grid=(B, nj),
        in_specs=[pl.BlockSpec((1, tm, og.shape[2]), lambda b, i: (b, i, 0)),
                  pl.BlockSpec((1, tm, od.shape[2]), lambda b, i: (b, i, 0)),
                  pl.BlockSpec((1, tm, D), lambda b, i: (b, i, 0)),
                  pl.BlockSpec((1, 6, D), lambda b, i: (b, 0, 0)),
                  const((w_out.shape[0], D)), const((1, D)), const((D, LANES)), const((1, LANES))],
        out_specs=[pl.BlockSpec((1, tm, D), lambda b, i: (b, i, 0)),
                   pl.BlockSpec((tm * ROW_TILE, LANES), lambda b, i: (b * nj + i, 0)),
                   pl.BlockSpec((tm, LANES), lambda b, i: (b * nj + i, 0))],
        compiler_params=pltpu.CompilerParams(
            dimension_semantics=("arbitrary", "arbitrary"), vmem_limit_bytes=VMEM_LIMIT),
        name="outproj",
    )(og, od, x, mod, w_out.astype(bf16), g_norm2.reshape(1, D), w_r, b_r)


def _route_kernel(lg_ref, lt_ref, ut_ref, ri_ref, rw_ref, cnt_ref, snap_ref, run_ref):
    @pl.when(pl.program_id(0) == 0)
    def _():
        run_ref[...] = jnp.zeros_like(run_ref)

    x = lg_ref[...]
    tr = x.shape[0]
    lane = lax.broadcasted_iota(jnp.int32, (tr, LANES), 1)
    lane_f = lane.astype(f32)
    vals, hots = [], []
    for _ in range(TOP_K):
        m = jnp.max(x, axis=-1, keepdims=True)
        idx = jnp.min(jnp.where(x == m, lane_f, float(LANES)), axis=-1, keepdims=True)
        hot = lane_f == idx
        x = jnp.where(hot, -jnp.inf, x)
        vals.append(m)
        hots.append(hot)
    ex = [jnp.exp(v - vals[0]) for v in vals]
    den = ex[0] + ex[1] + ex[2] + ex[3]
    sel = (hots[0] | hots[1] | hots[2] | hots[3]).astype(f32)
    rank = _mm(lt_ref[...], sel.astype(bf16)) + run_ref[...]
    run_ref[...] = run_ref[...] + jnp.sum(sel, axis=0, keepdims=True)
    cnt_ref[...] = run_ref[...]
    base = rank[0:1, :]
    snap_ref[0] = base
    n_hi, n_lo = _split(jnp.sum(sel, axis=0, keepdims=True))
    start = _mm(n_hi, ut_ref[...]) + _mm(n_lo, ut_ref[...])
    place = rank - base + start
    ri = jnp.zeros((tr, LANES), jnp.int32)
    rw = jnp.zeros((tr, LANES), f32)
    for k in range(TOP_K):
        rk = jnp.sum(jnp.where(hots[k], place, 0.0), axis=-1, keepdims=True).astype(jnp.int32)
        ri = jnp.where(lane == k, rk, ri)
        rw = jnp.where(lane == k, ex[k] / den, rw)
    ri_ref[...] = ri
    rw_ref[...] = rw


def _route(logits):
    T = logits.shape[0]
    tr = min(TR, T)
    assert tr == TD, "routing and dispatch share one token tile"
    r = jnp.arange(tr)
    lt = (r[None, :] < r[:, None]).astype(bf16)
    e = jnp.arange(LANES)
    ut = (e[:, None] < e[None, :]).astype(bf16)
    return pl.pallas_call(
        _route_kernel,
        out_shape=[jax.ShapeDtypeStruct((T, LANES), jnp.int32),
                   jax.ShapeDtypeStruct((T, LANES), f32),
                   jax.ShapeDtypeStruct((1, LANES), f32),
                   jax.ShapeDtypeStruct((T // tr, 1, LANES), f32)],
        grid=(T // tr,),
        in_specs=[pl.BlockSpec((tr, LANES), lambda i: (i, 0)),
                  pl.BlockSpec((tr, tr), lambda i: (0, 0)),
                  pl.BlockSpec((LANES, LANES), lambda i: (0, 0))],
        out_specs=[pl.BlockSpec((tr, LANES), lambda i: (i, 0)),
                   pl.BlockSpec((tr, LANES), lambda i: (i, 0)),
                   pl.BlockSpec((1, LANES), lambda i: (0, 0)),
                   pl.BlockSpec((1, 1, LANES), lambda i: (i, 0, 0))],
        scratch_shapes=[pltpu.VMEM((1, LANES), f32)],
        compiler_params=pltpu.CompilerParams(dimension_semantics=("arbitrary",)),
        name="route",
    )(logits, lt, ut)


def _run_copies(list_ref, hbm_ref, stage_ref, sem, to_hbm):
    for c, size in enumerate(RUN_SIZES):
        def one(i, carry, c=c, size=size):
            s0 = list_ref[c * N_RUN + i]
            d0 = list_ref[LIST_DST + c * N_RUN + i]
            stage = stage_ref.at[pl.ds(pl.multiple_of(s0, ROW_TILE), size * ROW_TILE)]
            rows = hbm_ref.at[pl.ds(pl.multiple_of(d0, ROW_TILE), size * ROW_TILE)]
            src, dst = (stage, rows) if to_hbm else (rows, stage)
            pltpu.make_async_copy(src, dst, sem).start(priority=c % 2)
            return carry
        lax.fori_loop(0, list_ref[LIST_CNT + c], one, 0)


def _copy_lists(run_dst, run_n, run_off):
    n_tiles = run_n.shape[0]
    size = jnp.asarray(RUN_SIZES, jnp.int32)[None, :, None]
    n = run_n[:, None, :]
    bit = (n & size) != 0
    before = n & ~(2 * size - 1)
    place = jnp.cumsum(bit, axis=-1) - 1
    pick = bit[:, :, None, :] & (place[:, :, None, :] == jnp.arange(N_RUN)[None, None, :, None])

    def compact(v):
        return jnp.sum(jnp.where(pick, v[:, :, None, :], 0), axis=-1) * ROW_TILE
    src = compact(run_off[:, None, :] + before)
    dst = compact(run_dst[:, None, :] + before)
    cnt = jnp.sum(bit, axis=-1).astype(jnp.int32)
    pad = jnp.zeros((n_tiles, LIST_LEN - LIST_CNT - len(RUN_SIZES)), jnp.int32)
    lists = jnp.concatenate([src.reshape(n_tiles, -1), dst.reshape(n_tiles, -1), cnt, pad], axis=1)
    return lists.reshape(-1).astype(jnp.int32)


def _dispatch_kernel(pend_ref, cnt_ref, nu_ref, lpos_ref, list_ref, h_ref, xs_ref,
                     zero_ref, stage_ref, sem, zsem):
    n_tok = h_ref.shape[0] // ROW_TILE
    blk_rows = FFN_BLK * ROW_TILE

    @pl.when(pl.program_id(0) == 0)
    def _():
        zero_ref[...] = jnp.zeros_like(zero_ref)
        n_exp = pend_ref.shape[0]

        def last_block(e):
            return xs_ref.at[pl.ds(pl.multiple_of((pend_ref[e] - FFN_BLK) * ROW_TILE, blk_rows), blk_rows)]

        def zfill(e, c):
            @pl.when(cnt_ref[e] > 0)
            def _():
                pltpu.make_async_copy(zero_ref, last_block(e), zsem).start()
            return c

        def zwait(e, c):
            @pl.when(cnt_ref[e] > 0)
            def _():
                pltpu.make_async_copy(zero_ref, last_block(e), zsem).wait()
            return c

        lax.fori_loop(0, n_exp, zfill, 0)
        lax.fori_loop(0, n_exp, zwait, 0)

        def tail_block(i):
            return xs_ref.at[pl.ds(pl.multiple_of(i * blk_rows, blk_rows), blk_rows)]

        def tfill(i, c):
            pltpu.make_async_copy(zero_ref, tail_block(i), zsem).start()
            return c

        def twait(i, c):
            pltpu.make_async_copy(zero_ref, tail_block(i), zsem).wait()
            return c

        n_blk = xs_ref.shape[0] // blk_rows
        lax.fori_loop(nu_ref[0], n_blk, tfill, 0)
        lax.fori_loop(nu_ref[0], n_blk, twait, 0)

    step = pl.program_id(0)
    slot = step % 2
    stage = stage_ref.at[slot]

    def place(g, c):
        for u in range(DMA_UNROLL):
            r = g * DMA_UNROLL + u
            row = h_ref[pl.ds(pl.multiple_of(r * ROW_TILE, ROW_TILE), ROW_TILE), :]
            for k in range(TOP_K):
                p = lpos_ref[r * TOP_K + k]
                stage[pl.ds(pl.multiple_of(p, ROW_TILE), ROW_TILE), :] = row
        return c
    lax.fori_loop(0, n_tok // DMA_UNROLL, place, 0)

    _run_copies(list_ref, xs_ref, stage, sem.at[slot], to_hbm=True)

    def drain(s):
        pltpu.make_async_copy(stage_ref.at[s], stage_ref.at[s], sem.at[s]).wait()

    @pl.when(step > 0)
    def _():
        drain(1 - slot)

    @pl.when(step == pl.num_programs(0) - 1)
    def _():
        drain(slot)


def _dispatch(p_ends, counts, n_used, lpos_flat, lists, hp, n_rows):
    T = hp.shape[0] // ROW_TILE
    grid_spec = pltpu.PrefetchScalarGridSpec(
        num_scalar_prefetch=3,
        grid=(T // TD,),
        in_specs=[pl.BlockSpec((TD * TOP_K,), lambda i, *_: (i,), memory_space=pltpu.SMEM),
                  pl.BlockSpec((LIST_LEN,), lambda i, *_: (i,), memory_space=pltpu.SMEM),
                  pl.BlockSpec((TD * ROW_TILE, LANES), lambda i, *_: (i, 0))],
        out_specs=pl.BlockSpec(memory_space=pl.ANY),
        scratch_shapes=[pltpu.VMEM((FFN_BLK * ROW_TILE, LANES), f32),
                        pltpu.VMEM((2, TD * TOP_K * ROW_TILE, LANES), f32),
                        pltpu.SemaphoreType.DMA((2,)), pltpu.SemaphoreType.DMA(())],
    )
    return pl.pallas_call(
        _dispatch_kernel,
        out_shape=jax.ShapeDtypeStruct((n_rows * ROW_TILE, LANES), f32),
        grid_spec=grid_spec,
        compiler_params=pltpu.CompilerParams(dimension_semantics=("arbitrary",), vmem_limit_bytes=VMEM_LIMIT),
        name="dispatch",
    )(p_ends, counts, n_used, lpos_flat, lists, hp)


def _ffn_kernel(be_ref, nu_ref, nx_ref, par_ref, val_ref, xs_ref, wgu_hbm, bgu_ref, wd_hbm, bd_ref, ys_ref,
                wgu32_ref, wd32_ref, wgu_ref, wd_ref, sem):
    i = pl.program_id(0)
    used = i < nu_ref[0]
    new_expert = (i == 0) | (be_ref[i] != be_ref[jnp.maximum(i - 1, 0)])
    slot = par_ref[i]

    def weight_copies(e, s):
        return (pltpu.make_async_copy(wgu_hbm.at[e], wgu32_ref.at[s], sem.at[0, s]),
                pltpu.make_async_copy(wd_hbm.at[e], wd32_ref.at[s], sem.at[1, s]))

    @pl.when(i == 0)
    def _():
        for cp in weight_copies(be_ref[0], 0):
            cp.start()

    @pl.when(used & new_expert)
    def _():
        for cp in weight_copies(be_ref[i], slot):
            cp.wait()

        @pl.when(nx_ref[i] >= 0)
        def _():
            for cp in weight_copies(nx_ref[i], 1 - slot):
                cp.start()

        rows = 128

        def cast(src, dst):
            def body(r, c):
                r0 = pl.multiple_of(r * rows, rows)
                dst[pl.ds(r0, rows), :] = src[slot, pl.ds(r0, rows), :].astype(bf16)
                return c
            lax.fori_loop(0, src.shape[1] // rows, body, 0)
        cast(wgu32_ref, wgu_ref)
        cast(wd32_ref, wd_ref)

    def ffn_rows(n):
        tiles = pl.ds(0, n * ROW_TILE)
        F = wd_ref.shape[0]
        xrow = _tiles_to_rows(xs_ref.at[tiles], n).astype(bf16)
        acc = None
        fc = F // 2
        for c in range(2):
            def gu(col0):
                return _mm(xrow, wgu_ref[:, col0:col0 + fc]) + bgu_ref[0, :, col0:col0 + fc]
            gate = jnp.minimum(gu(c * fc), SWIGLU_LIMIT)
            up = jnp.clip(gu(F + c * fc), -SWIGLU_LIMIT, SWIGLU_LIMIT)
            y = (up + 1.0) * (gate * jax.nn.sigmoid(SWIGLU_ALPHA * gate))
            part = _mm(y.astype(bf16), wd_ref[c * fc:(c + 1) * fc, :])
            acc = part if acc is None else acc + part
        _rows_to_tiles(acc + bd_ref[0], ys_ref.at[tiles])

    half = FFN_BLK // 2
    occupied = val_ref[i]

    @pl.when(used & (occupied > half))
    def _():
        ffn_rows(FFN_BLK)

    @pl.when(used & (occupied <= half))
    def _():
        ffn_rows(half)
        ys_ref[pl.ds(half * ROW_TILE, half * ROW_TILE), :] = jnp.zeros((half * ROW_TILE, LANES), f32)

    @pl.when(jnp.logical_not(used))
    def _():
        ys_ref[...] = jnp.zeros_like(ys_ref)


def _ffn(block_e, n_used, occupied, xs, w_gate_up, b_gate_up, w_down, b_down):
    E, D, F2 = w_gate_up.shape
    F = F2 // 2
    P = xs.shape[0] // ROW_TILE
    nb = P // FFN_BLK
    rows = FFN_BLK * ROW_TILE

    idx = jnp.arange(nb, dtype=jnp.int32)
    live = idx < n_used[0]
    later_other = (block_e[None, :] != block_e[:, None]) & (idx[None, :] > idx[:, None]) & live[None, :]
    nxt = jnp.where(jnp.any(later_other, axis=1), block_e[jnp.argmax(later_other, axis=1)], -1).astype(jnp.int32)
    starts = jnp.concatenate([jnp.ones((1,), jnp.int32), (block_e[1:] != block_e[:-1]).astype(jnp.int32)])
    parity = ((jnp.cumsum(starts) - 1) % 2).astype(jnp.int32)

    def blk(i, nu):
        return jnp.minimum(i, nu[0] - 1)

    grid_spec = pltpu.PrefetchScalarGridSpec(
        num_scalar_prefetch=5,
        grid=(nb,),
        in_specs=[pl.BlockSpec((rows, LANES), lambda i, be, nu, *_: (blk(i, nu), 0)),
                  pl.BlockSpec(memory_space=pl.ANY),
                  pl.BlockSpec((1, 1, F2), lambda i, be, nu, *_: (be[blk(i, nu)], 0, 0)),
                  pl.BlockSpec(memory_space=pl.ANY),
                  pl.BlockSpec((1, 1, D), lambda i, be, nu, *_: (be[blk(i, nu)], 0, 0))],
        out_specs=pl.BlockSpec((rows, LANES), lambda i, *_: (i, 0)),
        scratch_shapes=[pltpu.VMEM((2, D, F2), f32), pltpu.VMEM((2, F, D), f32),
                        pltpu.VMEM((D, F2), bf16), pltpu.VMEM((F, D), bf16),
                        pltpu.SemaphoreType.DMA((2, 2))],
    )
    return pl.pallas_call(
        _ffn_kernel,
        out_shape=jax.ShapeDtypeStruct((P * ROW_TILE, LANES), f32),
        grid_spec=grid_spec,
        compiler_params=pltpu.CompilerParams(
            dimension_semantics=("arbitrary",), vmem_limit_bytes=VMEM_LIMIT_FFN),
        name="ffn",
    )(block_e, n_used, nxt, parity, occupied, xs, w_gate_up, b_gate_up.reshape(E, 1, F2), w_down,
      b_down.reshape(E, 1, D))


def _combine_kernel(lpos_ref, lcur_ref, lnext_ref, ys_ref, x1_ref, rw_ref, mod_ref, o_ref,
                    stage_ref, acc_ref, wb_ref, sem):
    step = pl.program_id(0) * pl.num_programs(1) + pl.program_id(1)
    n_steps = pl.num_programs(0) * pl.num_programs(1)
    slot = step % 2

    def fetch(list_ref, s):
        _run_copies(list_ref, ys_ref, stage_ref.at[s], sem.at[s], to_hbm=False)

    @pl.when(step == 0)
    def _():
        fetch(lcur_ref, 0)

    @pl.when(step + 1 < n_steps)
    def _():
        fetch(lnext_ref, 1 - slot)

    pltpu.make_async_copy(stage_ref.at[slot], stage_ref.at[slot], sem.at[slot]).wait()

    rw = rw_ref[...]
    for k in range(TOP_K):
        wb_ref[k] = jnp.broadcast_to(rw[:, k:k + 1], (TD, LANES))

    def staged(r, k):
        p = lpos_ref[r * TOP_K + k]
        return stage_ref[slot, pl.ds(pl.multiple_of(p, ROW_TILE), ROW_TILE), :]

    def token(r, c):
        acc = wb_ref[0, pl.ds(r, 1), :] * staged(r, 0)
        for k in range(1, TOP_K):
            acc = acc + wb_ref[k, pl.ds(r, 1), :] * staged(r, k)
        acc_ref[pl.ds(pl.multiple_of(r * ROW_TILE, ROW_TILE), ROW_TILE), :] = acc
        return c
    lax.fori_loop(0, TD, token, 0, unroll=DMA_UNROLL)
    o_ref[0] = x1_ref[0] + mod_ref[0, 5:6, :] * _tiles_to_rows(acc_ref, TD)


def _combine(lpos_flat, lists, ys, x1, rw, mod):
    B, S, D = x1.shape
    nj = S // TD
    n_steps = B * nj
    return pl.pallas_call(
        _combine_kernel,
        out_shape=jax.ShapeDtypeStruct((B, S, D), f32),
        grid=(B, nj),
        in_specs=[pl.BlockSpec((TD * TOP_K,), lambda b, j: (b * nj + j,), memory_space=pltpu.SMEM),
                  pl.BlockSpec((LIST_LEN,), lambda b, j: (b * nj + j,), memory_space=pltpu.SMEM),
                  pl.BlockSpec((LIST_LEN,), lambda b, j: (jnp.minimum(b * nj + j + 1, n_steps - 1),),
                               memory_space=pltpu.SMEM),
                  pl.BlockSpec(memory_space=pl.ANY),
                  pl.BlockSpec((1, TD, D), lambda b, j: (b, j, 0)),
                  pl.BlockSpec((TD, LANES), lambda b, j: (b * nj + j, 0)),
                  pl.BlockSpec((1, 6, D), lambda b, j: (b, 0, 0))],
        out_specs=pl.BlockSpec((1, TD, D), lambda b, j: (b, j, 0)),
        scratch_shapes=[pltpu.VMEM((2, TD * TOP_K * ROW_TILE, LANES), f32),
                        pltpu.VMEM((TD * ROW_TILE, LANES), f32), pltpu.VMEM((TOP_K, TD, LANES), f32),
                        pltpu.SemaphoreType.DMA((2,))],
        compiler_params=pltpu.CompilerParams(
            dimension_semantics=("arbitrary", "arbitrary"), vmem_limit_bytes=VMEM_LIMIT),
        name="combine",
    )(lpos_flat, lists, lists, ys, x1, rw, mod)


def _moe(hp, logits, x1, mod, w_gate_up, b_gate_up, w_down, b_down):
    T = logits.shape[0]
    E = w_gate_up.shape[0]
    ri, rw, cnt, snap = _route(logits)
    lpos = (ri[:, :TOP_K] * ROW_TILE).reshape(-1)
    counts = cnt[0, :E].astype(jnp.int32)
    padded = ((counts + FFN_BLK - 1) // FFN_BLK) * FFN_BLK
    p_ends = jnp.cumsum(padded)
    p_starts = p_ends - padded
    nb = -(-T * TOP_K // FFN_BLK) + E
    n_used = jnp.maximum(p_ends[-1:] // FFN_BLK, 1).astype(jnp.int32)
    blk_start = jnp.arange(nb, dtype=jnp.int32) * FFN_BLK
    block_e = jnp.minimum(jnp.sum(p_ends[None, :] <= blk_start[:, None], axis=1), E - 1).astype(jnp.int32)
    assert E == N_RUN
    base = snap.reshape(T // TD, LANES)[:, :E].astype(jnp.int32)
    run_n = jnp.concatenate([base[1:], counts[None, :]], axis=0) - base
    run_off = jnp.cumsum(run_n, axis=1) - run_n
    run_dst = p_starts[None, :].astype(jnp.int32) + base
    lists = _copy_lists(run_dst, run_n, run_off)
    xs = _dispatch(p_ends.astype(jnp.int32), counts, n_used, lpos, lists, hp, nb * FFN_BLK)
    occupied = jnp.clip(counts[block_e] - (blk_start - p_starts[block_e]), 0, FFN_BLK).astype(jnp.int32)
    ys = _ffn(block_e, n_used, occupied, xs, w_gate_up, b_gate_up, w_down, b_down)
    return _combine(lpos, lists, ys, x1, rw, mod)


def kernel(x, c, rel_bias_table, w_ada, b_ada, g_norm1, w_in, w_gk_up, b_gk_up, g_gla_out, g_qnorm, g_knorm, lambda_q1, lambda_k1, lambda_q2, lambda_k2, g_subln, w_out, g_norm2, w_router, b_router, w_gate_up, b_gate_up, w_down, b_down):
    B, S, D = x.shape
    depth = w_ada.shape[0]
    bias_tiles = _bias_tiles(rel_bias_table, S, min(TQ, S))
    for l in range(depth):
        lambda_init = 0.8 - 0.6 * math.exp(-0.3 * l)
        mod = _ada(c, w_ada[l], b_ada[l])
        qg, kg, gk, kgt, gkt, vg, rg, qd, kd, vd = _inproj(
            x, mod, g_norm1[l], w_in[l], w_gk_up[l], b_gk_up[l], g_qnorm[l], g_knorm[l])
        og = _gla(qg, kg, gk, kgt, gkt, vg, rg, g_gla_out[l])
        lamv = jnp.stack([lambda_q1[l], lambda_k1[l], lambda_q2[l], lambda_k2[l]]).astype(f32)
        od = lax.cond(_scores_bounded(rel_bias_table, g_qnorm[l], g_knorm[l]),
                      functools.partial(_attn, lambda_init=lambda_init, bounded=True),
                      functools.partial(_attn, lambda_init=lambda_init, bounded=False),
                      qd, kd, vd, bias_tiles, lamv, g_subln[l])
        x1, hp, logits = _outproj(og, od, x, mod, w_out[l], g_norm2[l], w_router[l], b_router[l])
        x = _moe(hp, logits, x1, mod, w_gate_up[l], b_gate_up[l], w_down[l], b_down[l])
    return x
```

```python
import functools
import math

import jax
import jax.numpy as jnp
from jax import lax
from jax.experimental import pallas as pl
from jax.experimental.pallas import tpu as pltpu

f32 = jnp.float32
bf16 = jnp.bfloat16

N_GLA_HEADS = 4
GLA_DK = 64
GLA_DV = 128
GLA_GATE_RANK = 16
GLA_GATE_NORM = 16.0
GLA_CHUNK = 64
N_DIFF_HEADS = 4
DIFF_DQK = 64
DIFF_DV = 128
NUM_BUCKETS = 32
MAX_DISTANCE = 128
TOP_K = 4
SWIGLU_LIMIT = 7.0
SWIGLU_ALPHA = 1.702
EPS = 1e-6

GLA_QK_W = N_GLA_HEADS * GLA_DK
GLA_V_W = N_GLA_HEADS * GLA_DV
DIFF_QK_W = N_DIFF_HEADS * 2 * DIFF_DQK
DIFF_V_W = N_DIFF_HEADS * DIFF_DV

LANES = 128
NEG = -1e30
LOG2E = math.log2(math.e)
SAFE_SCORE = 40.0
NORM_SLACK = 1.02
VMEM_LIMIT = 48 * 1024 * 1024
VMEM_LIMIT_FFN = 58 * 1024 * 1024
VMEM_LIMIT_ATTN = 58 * 1024 * 1024

TM_IN = 512
INPROJ_SUB = 2
TG_GLA = 1024
PAIR = 2 * GLA_CHUNK
GLA_UNROLL = 4
TQ = 512
ATTN_UNROLL = 4
ATTN_HEADS = 2
TR = 512
TD = 512
ROW_TILE = 8
DMA_UNROLL = 8
N_RUN = 32
RUN_SIZES = tuple(TD >> b for b in range(TD.bit_length()))
LIST_DST = len(RUN_SIZES) * N_RUN
LIST_CNT = 2 * LIST_DST
LIST_LEN = 1024
assert LIST_CNT + len(RUN_SIZES) <= LIST_LEN
FFN_BLK = 1024
FFN_PART = 256


def _nt(a, b):
    return lax.dot_general(a, b, (((1,), (1,)), ((), ())), preferred_element_type=f32)


def _mm(a, b):
    return jnp.dot(a, b, preferred_element_type=f32)


def _split(x):
    hi = x.astype(bf16)
    lo = (x - hi.astype(f32)).astype(bf16)
    return hi, lo


def _silu(x):
    return x * jax.nn.sigmoid(x)


def _ada_kernel(c_ref, w_ref, b_ref, o_ref):
    c = c_ref[...]
    o_ref[...] = _mm(_silu(c).astype(bf16), w_ref[...].astype(bf16)) + b_ref[...]


def _ada(c, w_ada, b_ada):
    B, D = c.shape
    N = w_ada.shape[1]
    bp = ROW_TILE
    assert B <= bp
    cp = jnp.zeros((bp, D), f32).at[:B].set(c)
    tn = N // 4
    out = pl.pallas_call(
        _ada_kernel,
        out_shape=jax.ShapeDtypeStruct((bp, N), f32),
        grid=(N // tn,),
        in_specs=[pl.BlockSpec((bp, D), lambda j: (0, 0)),
                  pl.BlockSpec((D, tn), lambda j: (0, j)),
                  pl.BlockSpec((1, tn), lambda j: (0, j))],
        out_specs=pl.BlockSpec((bp, tn), lambda j: (0, j)),
        compiler_params=pltpu.CompilerParams(vmem_limit_bytes=VMEM_LIMIT),
        name="ada",
    )(cp, w_ada, b_ada.reshape(1, N))
    return out[:B].reshape(B, 6, D)


def _inproj_kernel(x_ref, mod_ref, g1_ref, wm_ref, wkt_ref, wlo_ref, wup_ref, wupt_ref,
                   bup_ref, bupt_ref, gqk_ref, grp_ref, grpt_ref,
                   qg_ref, kg_ref, gk_ref, kgt_ref, gkt_ref, vg_ref, rg_ref,
                   qd_ref, kd_ref, vd_ref):
    tm = x_ref.shape[1]
    sub = tm // INPROJ_SUB
    for t in range(INPROJ_SUB):
        _inproj_rows(slice(t * sub, (t + 1) * sub), x_ref, mod_ref, g1_ref, wm_ref, wkt_ref, wlo_ref, wup_ref,
                     wupt_ref, bup_ref, bupt_ref, gqk_ref, grp_ref, grpt_ref, qg_ref, kg_ref, gk_ref, kgt_ref,
                     gkt_ref, vg_ref, rg_ref, qd_ref, kd_ref, vd_ref)


def _inproj_rows(rows, x_ref, mod_ref, g1_ref, wm_ref, wkt_ref, wlo_ref, wup_ref, wupt_ref,
                 bup_ref, bupt_ref, gqk_ref, grp_ref, grpt_ref,
                 qg_ref, kg_ref, gk_ref, kgt_ref, gkt_ref, vg_ref, rg_ref, qd_ref, kd_ref, vd_ref):
    x = x_ref[0, rows, :]
    ms = jnp.mean(x * x, axis=-1, keepdims=True)
    y = x * lax.rsqrt(ms + EPS) * g1_ref[...]
    h = (y * (1.0 + mod_ref[0, 1:2, :]) + mod_ref[0, 0:1, :]).astype(bf16)

    def proj(a, b):
        return _mm(h, wm_ref[:, a:b])

    o = 0
    qg_ref[0, rows, :] = proj(o, o + GLA_QK_W); o += GLA_QK_W
    kg_ref[0, rows, :] = proj(o, o + GLA_QK_W); o += GLA_QK_W
    vg_ref[0, rows, :] = proj(o, o + GLA_V_W).astype(bf16); o += GLA_V_W
    rg_ref[0, rows, :] = proj(o, o + GLA_V_W); o += GLA_V_W
    qk = proj(o, o + 2 * DIFF_QK_W); o += 2 * DIFF_QK_W
    vd_ref[0, rows, :] = proj(o, o + DIFF_V_W).astype(bf16)

    slab0 = rows.start // PAIR
    kgt = _nt(wkt_ref[...], h)
    for j in range(kgt.shape[1] // PAIR):
        kgt_ref[0, slab0 + j] = kgt[:, j * PAIR:(j + 1) * PAIR]

    lo = _mm(h, wlo_ref[...]).astype(bf16)
    z = _mm(lo, wup_ref[...]) + bup_ref[...]
    gk_ref[0, rows, :] = (jnp.minimum(z, 0.0) - jnp.log1p(jnp.exp(-jnp.abs(z)))) * (1.0 / GLA_GATE_NORM)
    zt = _nt(wupt_ref[...], lo) + bupt_ref[...]
    gkt = (jnp.minimum(zt, 0.0) - jnp.log1p(jnp.exp(-jnp.abs(zt)))) * (1.0 / GLA_GATE_NORM)
    for j in range(gkt.shape[1] // PAIR):
        gkt_ref[0, slab0 + j] = gkt[:, j * PAIR:(j + 1) * PAIR]

    sq_hi, sq_lo = _split(qk * qk)
    gs = _mm(sq_hi, grp_ref[...]) + _mm(sq_lo, grp_ref[...])
    r = lax.rsqrt(gs * (1.0 / DIFF_DQK) + EPS)
    r_hi, r_lo = _split(r)
    rb = _mm(r_hi, grpt_ref[...]) + _mm(r_lo, grpt_ref[...])
    qkn = qk * rb * gqk_ref[...]
    qd_ref[0, rows, :] = qkn[:, :DIFF_QK_W].astype(bf16)
    kd_ref[0, rows, :] = qkn[:, DIFF_QK_W:].astype(bf16)


def _inproj(x, mod, g_norm1, w_in, w_gk_up, b_gk_up, g_qnorm, g_knorm):
    B, S, D = x.shape
    offs = [0]
    for w in (GLA_QK_W, GLA_QK_W, GLA_V_W, GLA_V_W, GLA_GATE_RANK, DIFF_QK_W, DIFF_QK_W, DIFF_V_W):
        offs.append(offs[-1] + w)
    w_main = jnp.concatenate([w_in[:, offs[0]:offs[4]], w_in[:, offs[5]:offs[8]]], axis=1).astype(bf16)
    w_kt = w_in[:, offs[1]:offs[2]].T.astype(bf16)
    w_lo = jnp.zeros((D, LANES), f32).at[:, :GLA_GATE_RANK].set(w_in[:, offs[4]:offs[5]]).astype(bf16)
    w_up = jnp.zeros((LANES, GLA_QK_W), f32).at[:GLA_GATE_RANK].set(w_gk_up).astype(bf16)
    w_upt = w_up.T
    b_up = b_gk_up.reshape(1, GLA_QK_W)
    b_upt = b_gk_up.reshape(GLA_QK_W, 1)
    n_grp = 2 * DIFF_QK_W // DIFF_DQK
    gqk = jnp.concatenate([jnp.tile(g_qnorm, n_grp // 2) * (DIFF_DQK ** -0.5 * LOG2E),
                           jnp.tile(g_knorm, n_grp // 2)]).reshape(1, 2 * DIFF_QK_W)
    grp = (jnp.arange(2 * DIFF_QK_W)[:, None] // DIFF_DQK == jnp.arange(LANES)[None, :]).astype(bf16)
    grpt = grp.T
    nw = w_main.shape[1]
    tm = TM_IN
    const = lambda shape: pl.BlockSpec(shape, lambda b, i: (0,) * len(shape))
    row = lambda w: pl.BlockSpec((1, tm, w), lambda b, i: (b, i, 0))
    colT = pl.BlockSpec((1, tm // PAIR, GLA_QK_W, PAIR), lambda b, i: (b, i, 0, 0))
    outs = pl.pallas_call(
        _inproj_kernel,
        out_shape=[jax.ShapeDtypeStruct((B, S, GLA_QK_W), f32),
                   jax.ShapeDtypeStruct((B, S, GLA_QK_W), f32),
                   jax.ShapeDtypeStruct((B, S, GLA_QK_W), f32),
                   jax.ShapeDtypeStruct((B, S // PAIR, GLA_QK_W, PAIR), f32),
                   jax.ShapeDtypeStruct((B, S // PAIR, GLA_QK_W, PAIR), f32),
                   jax.ShapeDtypeStruct((B, S, GLA_V_W), bf16),
                   jax.ShapeDtypeStruct((B, S, GLA_V_W), f32),
                   jax.ShapeDtypeStruct((B, S, DIFF_QK_W), bf16),
                   jax.ShapeDtypeStruct((B, S, DIFF_QK_W), bf16),
                   jax.ShapeDtypeStruct((B, S, DIFF_V_W), bf16)],
        grid=(B, S // tm),
        in_specs=[row(D),
                  pl.BlockSpec((1, 6, D), lambda b, i: (b, 0, 0)),
                  const((1, D)), const((D, nw)), const((GLA_QK_W, D)), const((D, LANES)),
                  const((LANES, GLA_QK_W)), const((GLA_QK_W, LANES)),
                  const((1, GLA_QK_W)), const((GLA_QK_W, 1)),
                  const((1, 2 * DIFF_QK_W)), const((2 * DIFF_QK_W, LANES)),
                  const((LANES, 2 * DIFF_QK_W))],
        out_specs=[row(GLA_QK_W), row(GLA_QK_W), row(GLA_QK_W), colT, colT,
                   row(GLA_V_W), row(GLA_V_W), row(DIFF_QK_W), row(DIFF_QK_W), row(DIFF_V_W)],
        compiler_params=pltpu.CompilerParams(
            dimension_semantics=("arbitrary", "arbitrary"), vmem_limit_bytes=VMEM_LIMIT),
        name="inproj",
    )(x, mod, g_norm1.reshape(1, D), w_main, w_kt, w_lo, w_up, w_upt, b_up, b_upt, gqk, grp, grpt)
    return outs


def _gla_kernel(q_ref, k_ref, g_ref, kt_ref, gt_ref, v_ref, r_ref, gout_ref, tri_ref, trit_ref,
                o_ref, s_ref, *, n_pairs):
    H, DK, DV = N_GLA_HEADS, GLA_DK, GLA_DV

    @pl.when(pl.program_id(1) == 0)
    def _():
        s_ref[...] = jnp.zeros_like(s_ref)

    tri = tri_ref[...]
    trit = trit_ref[...]
    tri_b = tri > 0
    lane_head = lax.broadcasted_iota(jnp.int32, (1, H * DK), 1) // DK
    row_head = lax.broadcasted_iota(jnp.int32, (H * PAIR, 1), 0) // PAIR
    qmask = row_head == lane_head
    row_first = lax.broadcasted_iota(jnp.int32, (PAIR, 1), 0) < GLA_CHUNK
    row_first4 = (lax.broadcasted_iota(jnp.int32, (H * PAIR, 1), 0) % PAIR) < GLA_CHUNK
    lane_first = lax.broadcasted_iota(jnp.int32, (1, PAIR), 1) < GLA_CHUNK
    scale = DK ** -0.5
    gout = gout_ref[...]

    def pair(p, carry):
        r0 = pl.multiple_of(p * PAIR, PAIR)
        q = q_ref[0, pl.ds(r0, PAIR), :]
        k = k_ref[0, pl.ds(r0, PAIR), :]
        g = g_ref[0, pl.ds(r0, PAIR), :]
        kt = kt_ref[0, p]
        gt = gt_ref[0, p]
        v = v_ref[0, pl.ds(r0, PAIR), :]

        g_hi, g_lo = _split(g)
        gc = _mm(tri, g_hi) + _mm(tri, g_lo)
        gt_hi, gt_lo = _split(gt)
        gct = _mm(gt_hi, trit) + _mm(gt_lo, trit)
        g_last = jnp.where(row_first, gc[GLA_CHUNK - 1:GLA_CHUNK, :], gc[PAIR - 1:PAIR, :])
        gl0 = gct[:, GLA_CHUNK - 1:GLA_CHUNK]
        gl1 = gct[:, PAIR - 1:PAIR]
        g_last_t = jnp.where(lane_first, gl0, gl1)

        q_e = (q * (jnp.exp(gc) * scale)).astype(bf16)
        k_e = (k * jnp.exp(-gc)).astype(bf16)
        ks_t = kt * jnp.exp(g_last_t - gct)
        ks_t0 = jnp.where(lane_first, ks_t, 0.0).astype(bf16)
        ks_t1 = jnp.where(lane_first, 0.0, ks_t).astype(bf16)
        del g_last

        qm = jnp.where(qmask, jnp.concatenate([q_e] * H, axis=0), jnp.zeros((), bf16))
        a = _nt(qm, k_e)
        s0 = s_ref[...]

        u0 = []
        u1 = []
        for h in range(H):
            v_h = v[:, h * DV:(h + 1) * DV]
            u0.append(_mm(ks_t0[h * DK:(h + 1) * DK], v_h))
            u1.append(_mm(ks_t1[h * DK:(h + 1) * DK], v_h))
        u0 = jnp.concatenate(u0, axis=0)
        u1 = jnp.concatenate(u1, axis=0)
        s1 = s0 * jnp.exp(gl0) + u0
        s_ref[...] = s1 * jnp.exp(gl1) + u1

        o_inter = jnp.where(row_first4, _mm(qm, s0.astype(bf16)), _mm(qm, s1.astype(bf16)))
        for h in range(H):
            a_h = jnp.where(tri_b, a[h * PAIR:(h + 1) * PAIR], 0.0).astype(bf16)
            o_h = _mm(a_h, v[:, h * DV:(h + 1) * DV]) + o_inter[h * PAIR:(h + 1) * PAIR]
            ms = jnp.mean(o_h * o_h, axis=-1, keepdims=True)
            o_n = o_h * lax.rsqrt(ms + EPS) * gout
            r_h = r_ref[0, pl.ds(r0, PAIR), h * DV:(h + 1) * DV]
            o_ref[0, pl.ds(r0, PAIR), h * DV:(h + 1) * DV] = (o_n * _silu(r_h)).astype(bf16)
        return carry

    lax.fori_loop(0, n_pairs, pair, 0, unroll=GLA_UNROLL)


def _gla(qg, kg, gk, kgt, gkt, vg, rg, g_gla_out):
    B, S, _ = qg.shape
    tg = min(TG_GLA, S)
    r = jnp.arange(PAIR)
    tri = ((r[:, None] // GLA_CHUNK == r[None, :] // GLA_CHUNK) & (r[None, :] <= r[:, None])).astype(bf16)
    row = lambda w: pl.BlockSpec((1, tg, w), lambda b, i: (b, i, 0))
    colT = pl.BlockSpec((1, tg // PAIR, GLA_QK_W, PAIR), lambda b, i: (b, i, 0, 0))
    const = lambda shape: pl.BlockSpec(shape, lambda b, i: (0,) * len(shape))
    return pl.pallas_call(
        functools.partial(_gla_kernel, n_pairs=tg // PAIR),
        out_shape=jax.ShapeDtypeStruct((B, S, GLA_V_W), bf16),
        grid=(B, S // tg),
        in_specs=[row(GLA_QK_W), row(GLA_QK_W), row(GLA_QK_W), colT, colT,
                  row(GLA_V_W), row(GLA_V_W), const((1, GLA_DV)),
                  const((PAIR, PAIR)), const((PAIR, PAIR))],
        out_specs=row(GLA_V_W),
        scratch_shapes=[pltpu.VMEM((GLA_QK_W, GLA_DV), f32)],
        compiler_params=pltpu.CompilerParams(
            dimension_semantics=("arbitrary", "arbitrary"), vmem_limit_bytes=VMEM_LIMIT),
        name="gla",
    )(qg, kg, gk, kgt, gkt, vg, rg, g_gla_out.reshape(1, GLA_DV), tri, tri.T)


def _attn_finish(o, gsub_ref, o_ref, lambda_init):
    ms = jnp.mean(o * o, axis=-1, keepdims=True)
    o_ref[0] = (o * lax.rsqrt(ms + EPS) * gsub_ref[...] * (1.0 - lambda_init)).astype(bf16)


def _attn_lambda(lamv_ref, lambda_init):
    lv = lamv_ref[...]
    return (jnp.exp(jnp.sum(lv[0:1] * lv[1:2], axis=-1, keepdims=True))
            - jnp.exp(jnp.sum(lv[2:3] * lv[3:4], axis=-1, keepdims=True)) + lambda_init)


def _attn_bounded_kernel(q_ref, k_ref, v_ref, bias_ref, lamv_ref, gsub_ref, o_ref, vaug_ref, *, lambda_init):
    qi = pl.program_id(2)
    tq = q_ref.shape[1]
    S = k_ref.shape[1]
    n_head = q_ref.shape[2] // (2 * DIFF_DQK)

    @pl.when(qi == 0)
    def _():
        lane = lax.broadcasted_iota(jnp.int32, (S, DIFF_DV), 1)
        for h in range(n_head):
            vaug_ref[h, :, :DIFF_DV] = v_ref[0, :, h * DIFF_DV:(h + 1) * DIFF_DV]
            vaug_ref[h, :, DIFF_DV:] = jnp.where(lane == 0, 1.0, 0.0).astype(bf16)

    lane = lax.broadcasted_iota(jnp.int32, (1, 2 * DIFF_DQK), 1)
    zero = jnp.zeros((), bf16)
    qs = []
    for h in range(n_head):
        q = q_ref[0, :, h * 2 * DIFF_DQK:(h + 1) * 2 * DIFF_DQK]
        qs.append((jnp.where(lane < DIFF_DQK, q, zero), jnp.where(lane < DIFF_DQK, zero, q)))

    def update(accs, k0, bias):
        out = []
        for h in range(n_head):
            kb = k_ref[0, pl.ds(k0, tq), h * 2 * DIFF_DQK:(h + 1) * 2 * DIFF_DQK]
            vb = vaug_ref[h, pl.ds(k0, tq), :]
            for c in range(2):
                s = _nt(qs[h][c], kb)
                if bias is not None:
                    s = s + bias[h][c]
                out.append(accs[2 * h + c] + _mm(jnp.exp2(s).astype(bf16), vb))
        return tuple(out)

    def far(kj, accs):
        return update(accs, pl.multiple_of(kj * tq, tq), None)

    def far_group(g, accs):
        for u in range(ATTN_UNROLL):
            accs = far(g * ATTN_UNROLL + u, accs)
        return accs

    def block_or_masked(accs, kj, rel):
        exists = kj >= 0
        k0 = pl.multiple_of(jnp.maximum(kj, 0) * tq, tq)
        if rel is None:
            tiles = [(jnp.where(exists, 0.0, NEG),) * 2] * n_head
        else:
            tiles = [tuple(jnp.where(exists, bias_ref[h, c, rel], NEG) for c in range(2)) for h in range(n_head)]
        return update(accs, k0, tiles)

    accs = tuple(jnp.zeros((tq, 2 * DIFF_DV), f32) for _ in range(2 * n_head))
    accs = block_or_masked(accs, qi, 1)
    accs = block_or_masked(accs, qi - 1, 0)
    for u in range(2, ATTN_UNROLL):
        accs = block_or_masked(accs, qi - u, None)
    n_far = jnp.maximum(qi + 1 - ATTN_UNROLL, 0)
    n_grp = n_far // ATTN_UNROLL
    accs = lax.fori_loop(0, n_grp, far_group, accs)
    accs = lax.fori_loop(n_grp * ATTN_UNROLL, n_far, far, accs)
    lam = _attn_lambda(lamv_ref, lambda_init)
    for h in range(n_head):
        a0, a1 = accs[2 * h], accs[2 * h + 1]
        o = a0[:, :DIFF_DV] / a0[:, DIFF_DV:DIFF_DV + 1] - lam * (a1[:, :DIFF_DV] / a1[:, DIFF_DV:DIFF_DV + 1])
        ms = jnp.mean(o * o, axis=-1, keepdims=True)
        o_ref[0, :, h * DIFF_DV:(h + 1) * DIFF_DV] = (
            o * lax.rsqrt(ms + EPS) * gsub_ref[...] * (1.0 - lambda_init)).astype(bf16)


def _attn_kernel(q_ref, k_ref, v_ref, bias_ref, lamv_ref, gsub_ref, o_ref, *, lambda_init):
    qi = pl.program_id(2)
    tq = q_ref.shape[1]
    q = q_ref[0]
    lane = lax.broadcasted_iota(jnp.int32, (1, 2 * DIFF_DQK), 1)
    zero = jnp.zeros((), bf16)
    qs = (jnp.where(lane < DIFF_DQK, q, zero), jnp.where(lane < DIFF_DQK, zero, q))

    def update(state, kb, vb, bias):
        new = []
        for c in range(2):
            m, l, acc = state[c]
            s = _nt(qs[c], kb)
            if bias is not None:
                s = s + bias[c]
            m_new = jnp.maximum(m, jnp.max(s, axis=-1, keepdims=True))
            alpha = jnp.exp2(m - m_new)
            p = jnp.exp2(s - m_new)
            l = alpha * l + jnp.sum(p, axis=-1, keepdims=True)
            acc = alpha * acc + _mm(p.astype(bf16), vb)
            new.append((m_new, l, acc))
        return tuple(new)

    init = tuple((jnp.full((tq, 1), NEG, f32), jnp.zeros((tq, 1), f32), jnp.zeros((tq, DIFF_DV), f32))
                 for _ in range(2))

    def far(kj, state):
        k0 = pl.multiple_of(kj * tq, tq)
        return update(state, k_ref[0, pl.ds(k0, tq), :], v_ref[0, pl.ds(k0, tq), :], None)

    state = lax.fori_loop(0, jnp.maximum(qi - 1, 0), far, init)

    kd0 = pl.multiple_of(qi * tq, tq)
    state = update(state, k_ref[0, pl.ds(kd0, tq), :], v_ref[0, pl.ds(kd0, tq), :],
                   (bias_ref[0, 0, 1], bias_ref[0, 1, 1]))
    kp0 = pl.multiple_of(jnp.maximum(qi - 1, 0) * tq, tq)
    has_prev = qi > 0
    state = update(state, k_ref[0, pl.ds(kp0, tq), :], v_ref[0, pl.ds(kp0, tq), :],
                   (jnp.where(has_prev, bias_ref[0, 0, 0], NEG), jnp.where(has_prev, bias_ref[0, 1, 0], NEG)))

    (_, l0, a0), (_, l1, a1) = state
    o = a0 / l0 - _attn_lambda(lamv_ref, lambda_init) * (a1 / l1)
    _attn_finish(o, gsub_ref, o_ref, lambda_init)


def _t5_bucket(n):
    max_exact = NUM_BUCKETS // 2
    nf = jnp.maximum(n, 1).astype(f32)
    large = max_exact + (jnp.log(nf / max_exact) / math.log(MAX_DISTANCE / max_exact)
                         * (NUM_BUCKETS - max_exact)).astype(jnp.int32)
    large = jnp.minimum(large, NUM_BUCKETS - 1)
    return jnp.where(n < max_exact, n, large)


def _toeplitz_kernel(w_ref, o_ref):
    n = o_ref.shape[-1]
    for t in range(2):
        rows = jnp.broadcast_to(w_ref[0, t:t + 1, :], (n, 2 * n))
        o_ref[0, 0, t] = pltpu.roll(rows, 0, 1, stride=1, stride_axis=0)[:, n:]


def _bias_tiles(rel_bias_table, S, n):
    HM = rel_bias_table.shape[1]
    assert n >= MAX_DISTANCE
    d = jnp.arange(2 * n, dtype=jnp.int32)
    by_dist = rel_bias_table[_t5_bucket(d)].astype(f32).T
    rel = (by_dist - rel_bias_table[NUM_BUCKETS - 1].astype(f32)[:, None]) * LOG2E
    i = jnp.arange(2 * n)
    w_diag = jnp.where(i[None, :] <= n, rel[:, jnp.clip(n - i, 0, 2 * n - 1)], NEG)
    w_prev = rel[:, jnp.clip(2 * n - i, 0, 2 * n - 1)]
    w = jnp.stack([w_prev, w_diag], axis=1)
    return pl.pallas_call(
        _toeplitz_kernel,
        out_shape=jax.ShapeDtypeStruct((HM // 2, 2, 2, n, n), f32),
        grid=(HM // 2, 2),
        in_specs=[pl.BlockSpec((1, 2, 2 * n), lambda h, m: (h * 2 + m, 0, 0))],
        out_specs=pl.BlockSpec((1, 1, 2, n, n), lambda h, m: (h, m, 0, 0, 0)),
        compiler_params=pltpu.CompilerParams(vmem_limit_bytes=VMEM_LIMIT),
        name="bias_tiles",
    )(w)


def _attn(qd, kd, vd, bias_tiles, lamv, g_subln, lambda_init, bounded):
    B, S, _ = qd.shape
    H = N_DIFF_HEADS
    tq = min(TQ, S)
    body = _attn_bounded_kernel if bounded else _attn_kernel
    hs = ATTN_HEADS if bounded else 1
    scratch = [pltpu.VMEM((hs, S, 2 * DIFF_DV), bf16)] if bounded else []
    return pl.pallas_call(
        functools.partial(body, lambda_init=lambda_init),
        out_shape=jax.ShapeDtypeStruct((B, S, DIFF_V_W), bf16),
        scratch_shapes=scratch,
        grid=(B, H // hs, S // tq),
        in_specs=[pl.BlockSpec((1, tq, hs * 2 * DIFF_DQK), lambda b, h, i: (b, i, h)),
                  pl.BlockSpec((1, S, hs * 2 * DIFF_DQK), lambda b, h, i: (b, 0, h)),
                  pl.BlockSpec((1, S, hs * DIFF_DV), lambda b, h, i: (b, 0, h)),
                  pl.BlockSpec((hs, 2, 2, tq, tq), lambda b, h, i: (h, 0, 0, 0, 0)),
                  pl.BlockSpec((4, DIFF_DQK), lambda b, h, i: (0, 0)),
                  pl.BlockSpec((1, DIFF_DV), lambda b, h, i: (0, 0))],
        out_specs=pl.BlockSpec((1, tq, hs * DIFF_DV), lambda b, h, i: (b, i, h)),
        compiler_params=pltpu.CompilerParams(
            dimension_semantics=("arbitrary", "arbitrary", "arbitrary"),
            vmem_limit_bytes=VMEM_LIMIT_ATTN if bounded else VMEM_LIMIT),
        name="attn_bounded" if bounded else "attn_online",
    )(qd, kd, vd, bias_tiles, lamv, g_subln.reshape(1, DIFF_DV))


def _scores_bounded(rel_bias_table, g_qnorm, g_knorm):
    qk = DIFF_DQK ** 0.5 * jnp.max(jnp.abs(g_qnorm)) * jnp.max(jnp.abs(g_knorm)) * NORM_SLACK
    rel = jnp.max(jnp.abs(rel_bias_table - rel_bias_table[NUM_BUCKETS - 1:]))
    return qk + rel <= SAFE_SCORE


def _rows_to_tiles(x, ref):
    n = x.shape[0]
    for c in range(ROW_TILE):
        ref[pl.ds(c, n, stride=ROW_TILE), :] = x[:, c * LANES:(c + 1) * LANES]


def _tiles_to_rows(ref, n):
    return jnp.concatenate([ref[pl.ds(c, n, stride=ROW_TILE), :] for c in range(ROW_TILE)], axis=1)


def _outproj_kernel(og_ref, od_ref, x_ref, mod_ref, wo_ref, g2_ref, wr_ref, br_ref,
                    x1_ref, hp_ref, lg_ref):
    half = og_ref.shape[2]
    sub = og_ref.shape[1] // INPROJ_SUB
    for t in range(INPROJ_SUB):
        rows = slice(t * sub, (t + 1) * sub)
        mix = _mm(og_ref[0, rows, :], wo_ref[:half, :]) + _mm(od_ref[0, rows, :], wo_ref[half:, :])
        x1 = x_ref[0, rows, :] + mod_ref[0, 2:3, :] * mix
        x1_ref[0, rows, :] = x1
        ms = jnp.mean(x1 * x1, axis=-1, keepdims=True)
        y = x1 * lax.rsqrt(ms + EPS) * g2_ref[...]
        h = (y * (1.0 + mod_ref[0, 4:5, :]) + mod_ref[0, 3:4, :]).astype(bf16)
        lg_ref[rows, :] = _mm(h, wr_ref[...]) + br_ref[...]
        _rows_to_tiles(h.astype(f32), hp_ref.at[pl.ds(t * sub * ROW_TILE, sub * ROW_TILE)])


def _outproj(og, od, x, mod, w_out, g_norm2, w_router, b_router):
    B, S, D = x.shape
    assert D == ROW_TILE * LANES, "the token-tile layout needs a model row to fill one (8,128) tile"
    E = w_router.shape[1]
    tm = TM_IN
    nj = S // tm
    w_r = jnp.zeros((D, LANES), f32).at[:, :E].set(w_router).astype(bf16)
    b_r = jnp.full((1, LANES), NEG, f32).at[0, :E].set(b_router)
    const = lambda shape: pl.BlockSpec(shape, lambda b, i: (0,) * len(shape))
    return pl.pallas_call(
        _outproj_kernel,
        out_shape=[jax.ShapeDtypeStruct((B, S, D), f32),
                   jax.ShapeDtypeStruct((B * S * ROW_TILE, LANES), f32),
                   jax.ShapeDtypeStruct((B * S, LANES), f32)],
        grid=(B, nj),
        in_specs=[pl.BlockSpec((1, tm, og.shape[2]), lambda b, i: (b, i, 0)),
                  pl.BlockSpec((1, tm, od.shape[2]), lambda b, i: (b, i, 0)),
                  pl.BlockSpec((1, tm, D), lambda b, i: (b, i, 0)),
                  pl.BlockSpec((1, 6, D), lambda b, i: (b, 0, 0)),
                  const((w_out.shape[0], D)), const((1, D)), const((D, LANES)), const((1, LANES))],
        out_specs=[pl.BlockSpec((1, tm, D), lambda b, i: (b, i, 0)),
                   pl.BlockSpec((tm * ROW_TILE, LANES), lambda b, i: (b * nj + i, 0)),
                   pl.BlockSpec((tm, LANES), lambda b, i: (b * nj + i, 0))],
        compiler_params=pltpu.CompilerParams(
            dimension_semantics=("arbitrary", "arbitrary"), vmem_limit_bytes=VMEM_LIMIT),
        name="outproj",
    )(og, od, x, mod, w_out.astype(bf16), g_norm2.reshape(1, D), w_r, b_r)


def _route_kernel(lg_ref, lt_ref, ut_ref, ri_ref, rw_ref, cnt_ref, snap_ref, run_ref):
    @pl.when(pl.program_id(0) == 0)
    def _():
        run_ref[...] = jnp.zeros_like(run_ref)

    x = lg_ref[...]
    tr = x.shape[0]
    lane = lax.broadcasted_iota(jnp.int32, (tr, LANES), 1)
    lane_f = lane.astype(f32)
    vals, hots = [], []
    for _ in range(TOP_K):
        m = jnp.max(x, axis=-1, keepdims=True)
        idx = jnp.min(jnp.where(x == m, lane_f, float(LANES)), axis=-1, keepdims=True)
        hot = lane_f == idx
        x = jnp.where(hot, -jnp.inf, x)
        vals.append(m)
        hots.append(hot)
    ex = [jnp.exp(v - vals[0]) for v in vals]
    den = ex[0] + ex[1] + ex[2] + ex[3]
    sel = (hots[0] | hots[1] | hots[2] | hots[3]).astype(f32)
    rank = _mm(lt_ref[...], sel.astype(bf16)) + run_ref[...]
    run_ref[...] = run_ref[...] + jnp.sum(sel, axis=0, keepdims=True)
    cnt_ref[...] = run_ref[...]
    base = rank[0:1, :]
    snap_ref[0] = base
    n_hi, n_lo = _split(jnp.sum(sel, axis=0, keepdims=True))
    start = _mm(n_hi, ut_ref[...]) + _mm(n_lo, ut_ref[...])
    place = rank - base + start
    ri = jnp.zeros((tr, LANES), jnp.int32)
    rw = jnp.zeros((tr, LANES), f32)
    for k in range(TOP_K):
        rk = jnp.sum(jnp.where(hots[k], place, 0.0), axis=-1, keepdims=True).astype(jnp.int32)
        ri = jnp.where(lane == k, rk, ri)
        rw = jnp.where(lane == k, ex[k] / den, rw)
    ri_ref[...] = ri
    rw_ref[...] = rw


def _route(logits):
    T = logits.shape[0]
    tr = min(TR, T)
    assert tr == TD, "routing and dispatch share one token tile"
    r = jnp.arange(tr)
    lt = (r[None, :] < r[:, None]).astype(bf16)
    e = jnp.arange(LANES)
    ut = (e[:, None] < e[None, :]).astype(bf16)
    return pl.pallas_call(
        _route_kernel,
        out_shape=[jax.ShapeDtypeStruct((T, LANES), jnp.int32),
                   jax.ShapeDtypeStruct((T, LANES), f32),
                   jax.ShapeDtypeStruct((1, LANES), f32),
                   jax.ShapeDtypeStruct((T // tr, 1, LANES), f32)],
        grid=(T // tr,),
        in_specs=[pl.BlockSpec((tr, LANES), lambda i: (i, 0)),
                  pl.BlockSpec((tr, tr), lambda i: (0, 0)),
                  pl.BlockSpec((LANES, LANES), lambda i: (0, 0))],
        out_specs=[pl.BlockSpec((tr, LANES), lambda i: (i, 0)),
                   pl.BlockSpec((tr, LANES), lambda i: (i, 0)),
                   pl.BlockSpec((1, LANES), lambda i: (0, 0)),
                   pl.BlockSpec((1, 1, LANES), lambda i: (i, 0, 0))],
        scratch_shapes=[pltpu.VMEM((1, LANES), f32)],
        compiler_params=pltpu.CompilerParams(dimension_semantics=("arbitrary",)),
        name="route",
    )(logits, lt, ut)


def _run_copies(list_ref, hbm_ref, stage_ref, sem, to_hbm):
    for c, size in enumerate(RUN_SIZES):
        def one(i, carry, c=c, size=size):
            s0 = list_ref[c * N_RUN + i]
            d0 = list_ref[LIST_DST + c * N_RUN + i]
            stage = stage_ref.at[pl.ds(pl.multiple_of(s0, ROW_TILE), size * ROW_TILE)]
            rows = hbm_ref.at[pl.ds(pl.multiple_of(d0, ROW_TILE), size * ROW_TILE)]
            src, dst = (stage, rows) if to_hbm else (rows, stage)
            pltpu.make_async_copy(src, dst, sem).start(priority=c % 2)
            return carry
        lax.fori_loop(0, list_ref[LIST_CNT + c], one, 0)


def _copy_lists(run_dst, run_n, run_off):
    n_tiles = run_n.shape[0]
    size = jnp.asarray(RUN_SIZES, jnp.int32)[None, :, None]
    n = run_n[:, None, :]
    bit = (n & size) != 0
    before = n & ~(2 * size - 1)
    place = jnp.cumsum(bit, axis=-1) - 1
    pick = bit[:, :, None, :] & (place[:, :, None, :] == jnp.arange(N_RUN)[None, None, :, None])

    def compact(v):
        return jnp.sum(jnp.where(pick, v[:, :, None, :], 0), axis=-1) * ROW_TILE
    src = compact(run_off[:, None, :] + before)
    dst = compact(run_dst[:, None, :] + before)
    cnt = jnp.sum(bit, axis=-1).astype(jnp.int32)
    pad = jnp.zeros((n_tiles, LIST_LEN - LIST_CNT - len(RUN_SIZES)), jnp.int32)
    lists = jnp.concatenate([src.reshape(n_tiles, -1), dst.reshape(n_tiles, -1), cnt, pad], axis=1)
    return lists.reshape(-1).astype(jnp.int32)


def _dispatch_kernel(pend_ref, cnt_ref, nu_ref, lpos_ref, list_ref, h_ref, xs_ref,
                     zero_ref, stage_ref, sem, zsem):
    n_tok = h_ref.shape[0] // ROW_TILE
    blk_rows = FFN_BLK * ROW_TILE

    @pl.when(pl.program_id(0) == 0)
    def _():
        zero_ref[...] = jnp.zeros_like(zero_ref)
        n_exp = pend_ref.shape[0]

        def last_block(e):
            return xs_ref.at[pl.ds(pl.multiple_of((pend_ref[e] - FFN_BLK) * ROW_TILE, blk_rows), blk_rows)]

        def zfill(e, c):
            @pl.when(cnt_ref[e] > 0)
            def _():
                pltpu.make_async_copy(zero_ref, last_block(e), zsem).start()
            return c

        def zwait(e, c):
            @pl.when(cnt_ref[e] > 0)
            def _():
                pltpu.make_async_copy(zero_ref, last_block(e), zsem).wait()
            return c

        lax.fori_loop(0, n_exp, zfill, 0)
        lax.fori_loop(0, n_exp, zwait, 0)

        def tail_block(i):
            return xs_ref.at[pl.ds(pl.multiple_of(i * blk_rows, blk_rows), blk_rows)]

        def tfill(i, c):
            pltpu.make_async_copy(zero_ref, tail_block(i), zsem).start()
            return c

        def twait(i, c):
            pltpu.make_async_copy(zero_ref, tail_block(i), zsem).wait()
            return c

        n_blk = xs_ref.shape[0] // blk_rows
        lax.fori_loop(nu_ref[0], n_blk, tfill, 0)
        lax.fori_loop(nu_ref[0], n_blk, twait, 0)

    step = pl.program_id(0)
    slot = step % 2
    stage = stage_ref.at[slot]

    def place(g, c):
        for u in range(DMA_UNROLL):
            r = g * DMA_UNROLL + u
            row = h_ref[pl.ds(pl.multiple_of(r * ROW_TILE, ROW_TILE), ROW_TILE), :]
            for k in range(TOP_K):
                p = lpos_ref[r * TOP_K + k]
                stage[pl.ds(pl.multiple_of(p, ROW_TILE), ROW_TILE), :] = row
        return c
    lax.fori_loop(0, n_tok // DMA_UNROLL, place, 0)

    _run_copies(list_ref, xs_ref, stage, sem.at[slot], to_hbm=True)

    def drain(s):
        pltpu.make_async_copy(stage_ref.at[s], stage_ref.at[s], sem.at[s]).wait()

    @pl.when(step > 0)
    def _():
        drain(1 - slot)

    @pl.when(step == pl.num_programs(0) - 1)
    def _():
        drain(slot)


def _dispatch(p_ends, counts, n_used, lpos_flat, lists, hp, n_rows):
    T = hp.shape[0] // ROW_TILE
    grid_spec = pltpu.PrefetchScalarGridSpec(
        num_scalar_prefetch=3,
        grid=(T // TD,),
        in_specs=[pl.BlockSpec((TD * TOP_K,), lambda i, *_: (i,), memory_space=pltpu.SMEM),
                  pl.BlockSpec((LIST_LEN,), lambda i, *_: (i,), memory_space=pltpu.SMEM),
                  pl.BlockSpec((TD * ROW_TILE, LANES), lambda i, *_: (i, 0))],
        out_specs=pl.BlockSpec(memory_space=pl.ANY),
        scratch_shapes=[pltpu.VMEM((FFN_BLK * ROW_TILE, LANES), f32),
                        pltpu.VMEM((2, TD * TOP_K * ROW_TILE, LANES), f32),
                        pltpu.SemaphoreType.DMA((2,)), pltpu.SemaphoreType.DMA(())],
    )
    return pl.pallas_call(
        _dispatch_kernel,
        out_shape=jax.ShapeDtypeStruct((n_rows * ROW_TILE, LANES), f32),
        grid_spec=grid_spec,
        compiler_params=pltpu.CompilerParams(dimension_semantics=("arbitrary",), vmem_limit_bytes=VMEM_LIMIT),
        name="dispatch",
    )(p_ends, counts, n_used, lpos_flat, lists, hp)


def _ffn_kernel(be_ref, nu_ref, nx_ref, val_ref, xs_ref, wgu_hbm, bgu_ref, wd_hbm, bd_ref, ys_ref,
                wgu32_ref, wd32_ref, wgu_ref, wd_ref, sem):
    i = pl.program_id(0)
    used = i < nu_ref[0]
    new_expert = (i == 0) | (be_ref[i] != be_ref[jnp.maximum(i - 1, 0)])

    def weight_copies(e):
        return (pltpu.make_async_copy(wgu_hbm.at[e], wgu32_ref, sem.at[0]),
                pltpu.make_async_copy(wd_hbm.at[e], wd32_ref, sem.at[1]))

    @pl.when(i == 0)
    def _():
        for cp in weight_copies(be_ref[0]):
            cp.start()

    @pl.when(used & new_expert)
    def _():
        for cp in weight_copies(be_ref[i]):
            cp.wait()

        rows = 128

        def cast(src, dst):
            def body(r, c):
                r0 = pl.multiple_of(r * rows, rows)
                dst[pl.ds(r0, rows), :] = src[pl.ds(r0, rows), :].astype(bf16)
                return c
            lax.fori_loop(0, src.shape[0] // rows, body, 0)
        cast(wgu32_ref, wgu_ref)
        cast(wd32_ref, wd_ref)

        @pl.when(nx_ref[i] >= 0)
        def _():
            for cp in weight_copies(nx_ref[i]):
                cp.start()

    def ffn_rows(n):
        tiles = pl.ds(0, n * ROW_TILE)
        F = wd_ref.shape[0]
        xrow = _tiles_to_rows(xs_ref.at[tiles], n).astype(bf16)
        acc = None
        fc = F // 2
        for c in range(2):
            def gu(col0):
                return _mm(xrow, wgu_ref[:, col0:col0 + fc]) + bgu_ref[0, :, col0:col0 + fc]
            gate = jnp.minimum(gu(c * fc), SWIGLU_LIMIT)
            up = jnp.clip(gu(F + c * fc), -SWIGLU_LIMIT, SWIGLU_LIMIT)
            y = (up + 1.0) * (gate * jax.nn.sigmoid(SWIGLU_ALPHA * gate))
            part = _mm(y.astype(bf16), wd_ref[c * fc:(c + 1) * fc, :])
            acc = part if acc is None else acc + part
        _rows_to_tiles(acc + bd_ref[0], ys_ref.at[tiles])

    parts = (val_ref[i] + FFN_PART - 1) // FFN_PART
    for j in range(1, FFN_BLK // FFN_PART + 1):
        @pl.when(used & (parts == j))
        def _(n=j * FFN_PART):
            ffn_rows(n)
            if n < FFN_BLK:
                rest = (FFN_BLK - n) * ROW_TILE
                ys_ref[pl.ds(n * ROW_TILE, rest), :] = jnp.zeros((rest, LANES), f32)

    @pl.when(jnp.logical_not(used))
    def _():
        ys_ref[...] = jnp.zeros_like(ys_ref)


def _ffn(block_e, n_used, occupied, xs, w_gate_up, b_gate_up, w_down, b_down):
    E, D, F2 = w_gate_up.shape
    F = F2 // 2
    P = xs.shape[0] // ROW_TILE
    nb = P // FFN_BLK
    rows = FFN_BLK * ROW_TILE

    idx = jnp.arange(nb, dtype=jnp.int32)
    live = idx < n_used[0]
    later_other = (block_e[None, :] != block_e[:, None]) & (idx[None, :] > idx[:, None]) & live[None, :]
    nxt = jnp.where(jnp.any(later_other, axis=1), block_e[jnp.argmax(later_other, axis=1)], -1).astype(jnp.int32)

    def blk(i, nu):
        return jnp.minimum(i, nu[0] - 1)

    grid_spec = pltpu.PrefetchScalarGridSpec(
        num_scalar_prefetch=4,
        grid=(nb,),
        in_specs=[pl.BlockSpec((rows, LANES), lambda i, be, nu, *_: (blk(i, nu), 0)),
                  pl.BlockSpec(memory_space=pl.ANY),
                  pl.BlockSpec((1, 1, F2), lambda i, be, nu, *_: (be[blk(i, nu)], 0, 0)),
                  pl.BlockSpec(memory_space=pl.ANY),
                  pl.BlockSpec((1, 1, D), lambda i, be, nu, *_: (be[blk(i, nu)], 0, 0))],
        out_specs=pl.BlockSpec((rows, LANES), lambda i, *_: (i, 0)),
        scratch_shapes=[pltpu.VMEM((D, F2), f32), pltpu.VMEM((F, D), f32),
                        pltpu.VMEM((D, F2), bf16), pltpu.VMEM((F, D), bf16),
                        pltpu.SemaphoreType.DMA((2,))],
    )
    return pl.pallas_call(
        _ffn_kernel,
        out_shape=jax.ShapeDtypeStruct((P * ROW_TILE, LANES), f32),
        grid_spec=grid_spec,
        compiler_params=pltpu.CompilerParams(
            dimension_semantics=("arbitrary",), vmem_limit_bytes=VMEM_LIMIT_FFN),
        name="ffn",
    )(block_e, n_used, nxt, occupied, xs, w_gate_up, b_gate_up.reshape(E, 1, F2), w_down,
      b_down.reshape(E, 1, D))


def _combine_kernel(lpos_ref, lcur_ref, lnext_ref, ys_ref, x1_ref, rw_ref, mod_ref, o_ref,
                    stage_ref, acc_ref, wb_ref, sem):
    step = pl.program_id(0) * pl.num_programs(1) + pl.program_id(1)
    n_steps = pl.num_programs(0) * pl.num_programs(1)
    slot = step % 2

    def fetch(list_ref, s):
        _run_copies(list_ref, ys_ref, stage_ref.at[s], sem.at[s], to_hbm=False)

    @pl.when(step == 0)
    def _():
        fetch(lcur_ref, 0)

    @pl.when(step + 1 < n_steps)
    def _():
        fetch(lnext_ref, 1 - slot)

    pltpu.make_async_copy(stage_ref.at[slot], stage_ref.at[slot], sem.at[slot]).wait()

    rw = rw_ref[...]
    for k in range(TOP_K):
        wb_ref[k] = jnp.broadcast_to(rw[:, k:k + 1], (TD, LANES))

    def staged(r, k):
        p = lpos_ref[r * TOP_K + k]
        return stage_ref[slot, pl.ds(pl.multiple_of(p, ROW_TILE), ROW_TILE), :]

    def token(r, c):
        acc = wb_ref[0, pl.ds(r, 1), :] * staged(r, 0)
        for k in range(1, TOP_K):
            acc = acc + wb_ref[k, pl.ds(r, 1), :] * staged(r, k)
        acc_ref[pl.ds(pl.multiple_of(r * ROW_TILE, ROW_TILE), ROW_TILE), :] = acc
        return c
    lax.fori_loop(0, TD, token, 0, unroll=DMA_UNROLL)
    o_ref[0] = x1_ref[0] + mod_ref[0, 5:6, :] * _tiles_to_rows(acc_ref, TD)


def _combine(lpos_flat, lists, ys, x1, rw, mod):
    B, S, D = x1.shape
    nj = S // TD
    n_steps = B * nj
    return pl.pallas_call(
        _combine_kernel,
        out_shape=jax.ShapeDtypeStruct((B, S, D), f32),
        grid=(B, nj),
        in_specs=[pl.BlockSpec((TD * TOP_K,), lambda b, j: (b * nj + j,), memory_space=pltpu.SMEM),
                  pl.BlockSpec((LIST_LEN,), lambda b, j: (b * nj + j,), memory_space=pltpu.SMEM),
                  pl.BlockSpec((LIST_LEN,), lambda b, j: (jnp.minimum(b * nj + j + 1, n_steps - 1),),
                               memory_space=pltpu.SMEM),
                  pl.BlockSpec(memory_space=pl.ANY),
                  pl.BlockSpec((1, TD, D), lambda b, j: (b, j, 0)),
                  pl.BlockSpec((TD, LANES), lambda b, j: (b * nj + j, 0)),
                  pl.BlockSpec((1, 6, D), lambda b, j: (b, 0, 0))],
        out_specs=pl.BlockSpec((1, TD, D), lambda b, j: (b, j, 0)),
        scratch_shapes=[pltpu.VMEM((2, TD * TOP_K * ROW_TILE, LANES), f32),
                        pltpu.VMEM((TD * ROW_TILE, LANES), f32), pltpu.VMEM((TOP_K, TD, LANES), f32),
                        pltpu.SemaphoreType.DMA((2,))],
        compiler_params=pltpu.CompilerParams(
            dimension_semantics=("arbitrary", "arbitrary"), vmem_limit_bytes=VMEM_LIMIT),
        name="combine",
    )(lpos_flat, lists, lists, ys, x1, rw, mod)


def _moe(hp, logits, x1, mod, w_gate_up, b_gate_up, w_down, b_down):
    T = logits.shape[0]
    E = w_gate_up.shape[0]
    ri, rw, cnt, snap = _route(logits)
    lpos = (ri[:, :TOP_K] * ROW_TILE).reshape(-1)
    counts = cnt[0, :E].astype(jnp.int32)
    padded = ((counts + FFN_BLK - 1) // FFN_BLK) * FFN_BLK
    p_ends = jnp.cumsum(padded)
    p_starts = p_ends - padded
    nb = -(-T * TOP_K // FFN_BLK) + E
    n_used = jnp.maximum(p_ends[-1:] // FFN_BLK, 1).astype(jnp.int32)
    blk_start = jnp.arange(nb, dtype=jnp.int32) * FFN_BLK
    block_e = jnp.minimum(jnp.sum(p_ends[None, :] <= blk_start[:, None], axis=1), E - 1).astype(jnp.int32)
    assert E == N_RUN
    base = snap.reshape(T // TD, LANES)[:, :E].astype(jnp.int32)
    run_n = jnp.concatenate([base[1:], counts[None, :]], axis=0) - base
    run_off = jnp.cumsum(run_n, axis=1) - run_n
    run_dst = p_starts[None, :].astype(jnp.int32) + base
    lists = _copy_lists(run_dst, run_n, run_off)
    xs = _dispatch(p_ends.astype(jnp.int32), counts, n_used, lpos, lists, hp, nb * FFN_BLK)
    occupied = jnp.clip(counts[block_e] - (blk_start - p_starts[block_e]), 0, FFN_BLK).astype(jnp.int32)
    ys = _ffn(block_e, n_used, occupied, xs, w_gate_up, b_gate_up, w_down, b_down)
    return _combine(lpos, lists, ys, x1, rw, mod)


def kernel(x, c, rel_bias_table, w_ada, b_ada, g_norm1, w_in, w_gk_up, b_gk_up, g_gla_out, g_qnorm, g_knorm, lambda_q1, lambda_k1, lambda_q2, lambda_k2, g_subln, w_out, g_norm2, w_router, b_router, w_gate_up, b_gate_up, w_down, b_down):
    B, S, D = x.shape
    depth = w_ada.shape[0]
    bias_tiles = _bias_tiles(rel_bias_table, S, min(TQ, S))
    for l in range(depth):
        lambda_init = 0.8 - 0.6 * math.exp(-0.3 * l)
        mod = _ada(c, w_ada[l], b_ada[l])
        qg, kg, gk, kgt, gkt, vg, rg, qd, kd, vd = _inproj(
            x, mod, g_norm1[l], w_in[l], w_gk_up[l], b_gk_up[l], g_qnorm[l], g_knorm[l])
        og = _gla(qg, kg, gk, kgt, gkt, vg, rg, g_gla_out[l])
        lamv = jnp.stack([lambda_q1[l], lambda_k1[l], lambda_q2[l], lambda_k2[l]]).astype(f32)
        od = lax.cond(_scores_bounded(rel_bias_table, g_qnorm[l], g_knorm[l]),
                      functools.partial(_attn, lambda_init=lambda_init, bounded=True),
                      functools.partial(_attn, lambda_init=lambda_init, bounded=False),
                      qd, kd, vd, bias_tiles, lamv, g_subln[l])
        x1, hp, logits = _outproj(og, od, x, mod, w_out[l], g_norm2[l], w_router[l], b_router[l])
        x = _moe(hp, logits, x1, mod, w_gate_up[l], b_gate_up[l], w_down[l], b_down[l])
    return x
```

```python
import functools
import math

import jax
import jax.numpy as jnp
from jax import lax
from jax.experimental import pallas as pl
from jax.experimental.pallas import tpu as pltpu

f32 = jnp.float32
bf16 = jnp.bfloat16

N_GLA_HEADS = 4
GLA_DK = 64
GLA_DV = 128
GLA_GATE_RANK = 16
GLA_GATE_NORM = 16.0
GLA_CHUNK = 64
N_DIFF_HEADS = 4
DIFF_DQK = 64
DIFF_DV = 128
NUM_BUCKETS = 32
MAX_DISTANCE = 128
TOP_K = 4
SWIGLU_LIMIT = 7.0
SWIGLU_ALPHA = 1.702
EPS = 1e-6

GLA_QK_W = N_GLA_HEADS * GLA_DK
GLA_V_W = N_GLA_HEADS * GLA_DV
DIFF_QK_W = N_DIFF_HEADS * 2 * DIFF_DQK
DIFF_V_W = N_DIFF_HEADS * DIFF_DV

LANES = 128
NEG = -1e30
LOG2E = math.log2(math.e)
SAFE_SCORE = 40.0
NORM_SLACK = 1.02
VMEM_LIMIT = 48 * 1024 * 1024
VMEM_LIMIT_FFN = 58 * 1024 * 1024
VMEM_LIMIT_ATTN = 58 * 1024 * 1024

TM_IN = 512
INPROJ_SUB = 2
TG_GLA = 1024
PAIR = 2 * GLA_CHUNK
GLA_UNROLL = 4
TQ = 512
ATTN_UNROLL = 4
ATTN_HEADS = 2
TR = 512
TD = 512
ROW_TILE = 8
DMA_UNROLL = 8
N_RUN = 32
RUN_SIZES = tuple(TD >> b for b in range(TD.bit_length()))
LIST_DST = len(RUN_SIZES) * N_RUN
LIST_CNT = 2 * LIST_DST
LIST_LEN = 1024
assert LIST_CNT + len(RUN_SIZES) <= LIST_LEN
FFN_BLK = 512


def _nt(a, b):
    return lax.dot_general(a, b, (((1,), (1,)), ((), ())), preferred_element_type=f32)


def _mm(a, b):
    return jnp.dot(a, b, preferred_element_type=f32)


def _split(x):
    hi = x.astype(bf16)
    lo = (x - hi.astype(f32)).astype(bf16)
    return hi, lo


def _silu(x):
    return x * jax.nn.sigmoid(x)


def _ada_kernel(c_ref, w_ref, b_ref, o_ref):
    c = c_ref[...]
    o_ref[...] = _mm(_silu(c).astype(bf16), w_ref[...].astype(bf16)) + b_ref[...]


def _ada(c, w_ada, b_ada):
    B, D = c.shape
    N = w_ada.shape[1]
    bp = ROW_TILE
    assert B <= bp
    cp = jnp.zeros((bp, D), f32).at[:B].set(c)
    tn = N // 4
    out = pl.pallas_call(
        _ada_kernel,
        out_shape=jax.ShapeDtypeStruct((bp, N), f32),
        grid=(N // tn,),
        in_specs=[pl.BlockSpec((bp, D), lambda j: (0, 0)),
                  pl.BlockSpec((D, tn), lambda j: (0, j)),
                  pl.BlockSpec((1, tn), lambda j: (0, j))],
        out_specs=pl.BlockSpec((bp, tn), lambda j: (0, j)),
        compiler_params=pltpu.CompilerParams(vmem_limit_bytes=VMEM_LIMIT),
        name="ada",
    )(cp, w_ada, b_ada.reshape(1, N))
    return out[:B].reshape(B, 6, D)


def _inproj_kernel(x_ref, mod_ref, g1_ref, wm_ref, wkt_ref, wlo_ref, wup_ref, wupt_ref,
                   bup_ref, bupt_ref, gqk_ref, grp_ref, grpt_ref,
                   qg_ref, kg_ref, gk_ref, kgt_ref, gkt_ref, vg_ref, rg_ref,
                   qd_ref, kd_ref, vd_ref):
    tm = x_ref.shape[1]
    sub = tm // INPROJ_SUB
    for t in range(INPROJ_SUB):
        _inproj_rows(slice(t * sub, (t + 1) * sub), x_ref, mod_ref, g1_ref, wm_ref, wkt_ref, wlo_ref, wup_ref,
                     wupt_ref, bup_ref, bupt_ref, gqk_ref, grp_ref, grpt_ref, qg_ref, kg_ref, gk_ref, kgt_ref,
                     gkt_ref, vg_ref, rg_ref, qd_ref, kd_ref, vd_ref)


def _inproj_rows(rows, x_ref, mod_ref, g1_ref, wm_ref, wkt_ref, wlo_ref, wup_ref, wupt_ref,
                 bup_ref, bupt_ref, gqk_ref, grp_ref, grpt_ref,
                 qg_ref, kg_ref, gk_ref, kgt_ref, gkt_ref, vg_ref, rg_ref, qd_ref, kd_ref, vd_ref):
    x = x_ref[0, rows, :]
    ms = jnp.mean(x * x, axis=-1, keepdims=True)
    y = x * lax.rsqrt(ms + EPS) * g1_ref[...]
    h = (y * (1.0 + mod_ref[0, 1:2, :]) + mod_ref[0, 0:1, :]).astype(bf16)

    def proj(a, b):
        return _mm(h, wm_ref[:, a:b])

    o = 0
    qg_ref[0, rows, :] = proj(o, o + GLA_QK_W); o += GLA_QK_W
    kg_ref[0, rows, :] = proj(o, o + GLA_QK_W); o += GLA_QK_W
    vg_ref[0, rows, :] = proj(o, o + GLA_V_W).astype(bf16); o += GLA_V_W
    rg_ref[0, rows, :] = proj(o, o + GLA_V_W); o += GLA_V_W
    qk = proj(o, o + 2 * DIFF_QK_W); o += 2 * DIFF_QK_W
    vd_ref[0, rows, :] = proj(o, o + DIFF_V_W).astype(bf16)

    slab0 = rows.start // PAIR
    kgt = _nt(wkt_ref[...], h)
    for j in range(kgt.shape[1] // PAIR):
        kgt_ref[0, slab0 + j] = kgt[:, j * PAIR:(j + 1) * PAIR]

    lo = _mm(h, wlo_ref[...]).astype(bf16)
    z = _mm(lo, wup_ref[...]) + bup_ref[...]
    gk_ref[0, rows, :] = (jnp.minimum(z, 0.0) - jnp.log1p(jnp.exp(-jnp.abs(z)))) * (1.0 / GLA_GATE_NORM)
    zt = _nt(wupt_ref[...], lo) + bupt_ref[...]
    gkt = (jnp.minimum(zt, 0.0) - jnp.log1p(jnp.exp(-jnp.abs(zt)))) * (1.0 / GLA_GATE_NORM)
    for j in range(gkt.shape[1] // PAIR):
        gkt_ref[0, slab0 + j] = gkt[:, j * PAIR:(j + 1) * PAIR]

    sq_hi, sq_lo = _split(qk * qk)
    gs = _mm(sq_hi, grp_ref[...]) + _mm(sq_lo, grp_ref[...])
    r = lax.rsqrt(gs * (1.0 / DIFF_DQK) + EPS)
    r_hi, r_lo = _split(r)
    rb = _mm(r_hi, grpt_ref[...]) + _mm(r_lo, grpt_ref[...])
    qkn = qk * rb * gqk_ref[...]
    qd_ref[0, rows, :] = qkn[:, :DIFF_QK_W].astype(bf16)
    kd_ref[0, rows, :] = qkn[:, DIFF_QK_W:].astype(bf16)


def _inproj(x, mod, g_norm1, w_in, w_gk_up, b_gk_up, g_qnorm, g_knorm):
    B, S, D = x.shape
    offs = [0]
    for w in (GLA_QK_W, GLA_QK_W, GLA_V_W, GLA_V_W, GLA_GATE_RANK, DIFF_QK_W, DIFF_QK_W, DIFF_V_W):
        offs.append(offs[-1] + w)
    w_main = jnp.concatenate([w_in[:, offs[0]:offs[4]], w_in[:, offs[5]:offs[8]]], axis=1).astype(bf16)
    w_kt = w_in[:, offs[1]:offs[2]].T.astype(bf16)
    w_lo = jnp.zeros((D, LANES), f32).at[:, :GLA_GATE_RANK].set(w_in[:, offs[4]:offs[5]]).astype(bf16)
    w_up = jnp.zeros((LANES, GLA_QK_W), f32).at[:GLA_GATE_RANK].set(w_gk_up).astype(bf16)
    w_upt = w_up.T
    b_up = b_gk_up.reshape(1, GLA_QK_W)
    b_upt = b_gk_up.reshape(GLA_QK_W, 1)
    n_grp = 2 * DIFF_QK_W // DIFF_DQK
    gqk = jnp.concatenate([jnp.tile(g_qnorm, n_grp // 2) * (DIFF_DQK ** -0.5 * LOG2E),
                           jnp.tile(g_knorm, n_grp // 2)]).reshape(1, 2 * DIFF_QK_W)
    grp = (jnp.arange(2 * DIFF_QK_W)[:, None] // DIFF_DQK == jnp.arange(LANES)[None, :]).astype(bf16)
    grpt = grp.T
    nw = w_main.shape[1]
    tm = TM_IN
    const = lambda shape: pl.BlockSpec(shape, lambda b, i: (0,) * len(shape))
    row = lambda w: pl.BlockSpec((1, tm, w), lambda b, i: (b, i, 0))
    colT = pl.BlockSpec((1, tm // PAIR, GLA_QK_W, PAIR), lambda b, i: (b, i, 0, 0))
    outs = pl.pallas_call(
        _inproj_kernel,
        out_shape=[jax.ShapeDtypeStruct((B, S, GLA_QK_W), f32),
                   jax.ShapeDtypeStruct((B, S, GLA_QK_W), f32),
                   jax.ShapeDtypeStruct((B, S, GLA_QK_W), f32),
                   jax.ShapeDtypeStruct((B, S // PAIR, GLA_QK_W, PAIR), f32),
                   jax.ShapeDtypeStruct((B, S // PAIR, GLA_QK_W, PAIR), f32),
                   jax.ShapeDtypeStruct((B, S, GLA_V_W), bf16),
                   jax.ShapeDtypeStruct((B, S, GLA_V_W), f32),
                   jax.ShapeDtypeStruct((B, S, DIFF_QK_W), bf16),
                   jax.ShapeDtypeStruct((B, S, DIFF_QK_W), bf16),
                   jax.ShapeDtypeStruct((B, S, DIFF_V_W), bf16)],
        grid=(B, S // tm),
        in_specs=[row(D),
                  pl.BlockSpec((1, 6, D), lambda b, i: (b, 0, 0)),
                  const((1, D)), const((D, nw)), const((GLA_QK_W, D)), const((D, LANES)),
                  const((LANES, GLA_QK_W)), const((GLA_QK_W, LANES)),
                  const((1, GLA_QK_W)), const((GLA_QK_W, 1)),
                  const((1, 2 * DIFF_QK_W)), const((2 * DIFF_QK_W, LANES)),
                  const((LANES, 2 * DIFF_QK_W))],
        out_specs=[row(GLA_QK_W), row(GLA_QK_W), row(GLA_QK_W), colT, colT,
                   row(GLA_V_W), row(GLA_V_W), row(DIFF_QK_W), row(DIFF_QK_W), row(DIFF_V_W)],
        compiler_params=pltpu.CompilerParams(
            dimension_semantics=("arbitrary", "arbitrary"), vmem_limit_bytes=VMEM_LIMIT),
        name="inproj",
    )(x, mod, g_norm1.reshape(1, D), w_main, w_kt, w_lo, w_up, w_upt, b_up, b_upt, gqk, grp, grpt)
    return outs


def _gla_kernel(q_ref, k_ref, g_ref, kt_ref, gt_ref, v_ref, r_ref, gout_ref, tri_ref, trit_ref,
                o_ref, s_ref, *, n_pairs):
    H, DK, DV = N_GLA_HEADS, GLA_DK, GLA_DV

    @pl.when(pl.program_id(1) == 0)
    def _():
        s_ref[...] = jnp.zeros_like(s_ref)

    tri = tri_ref[...]
    trit = trit_ref[...]
    tri_b = tri > 0
    lane_head = lax.broadcasted_iota(jnp.int32, (1, H * DK), 1) // DK
    row_head = lax.broadcasted_iota(jnp.int32, (H * PAIR, 1), 0) // PAIR
    qmask = row_head == lane_head
    row_first = lax.broadcasted_iota(jnp.int32, (PAIR, 1), 0) < GLA_CHUNK
    row_first4 = (lax.broadcasted_iota(jnp.int32, (H * PAIR, 1), 0) % PAIR) < GLA_CHUNK
    lane_first = lax.broadcasted_iota(jnp.int32, (1, PAIR), 1) < GLA_CHUNK
    scale = DK ** -0.5
    gout = gout_ref[...]

    def pair(p, carry):
        r0 = pl.multiple_of(p * PAIR, PAIR)
        q = q_ref[0, pl.ds(r0, PAIR), :]
        k = k_ref[0, pl.ds(r0, PAIR), :]
        g = g_ref[0, pl.ds(r0, PAIR), :]
        kt = kt_ref[0, p]
        gt = gt_ref[0, p]
        v = v_ref[0, pl.ds(r0, PAIR), :]

        g_hi, g_lo = _split(g)
        gc = _mm(tri, g_hi) + _mm(tri, g_lo)
        gt_hi, gt_lo = _split(gt)
        gct = _mm(gt_hi, trit) + _mm(gt_lo, trit)
        g_last = jnp.where(row_first, gc[GLA_CHUNK - 1:GLA_CHUNK, :], gc[PAIR - 1:PAIR, :])
        gl0 = gct[:, GLA_CHUNK - 1:GLA_CHUNK]
        gl1 = gct[:, PAIR - 1:PAIR]
        g_last_t = jnp.where(lane_first, gl0, gl1)

        q_e = (q * (jnp.exp(gc) * scale)).astype(bf16)
        k_e = (k * jnp.exp(-gc)).astype(bf16)
        ks_t = kt * jnp.exp(g_last_t - gct)
        ks_t0 = jnp.where(lane_first, ks_t, 0.0).astype(bf16)
        ks_t1 = jnp.where(lane_first, 0.0, ks_t).astype(bf16)
        del g_last

        qm = jnp.where(qmask, jnp.concatenate([q_e] * H, axis=0), jnp.zeros((), bf16))
        a = _nt(qm, k_e)
        s0 = s_ref[...]

        u0 = []
        u1 = []
        for h in range(H):
            v_h = v[:, h * DV:(h + 1) * DV]
            u0.append(_mm(ks_t0[h * DK:(h + 1) * DK], v_h))
            u1.append(_mm(ks_t1[h * DK:(h + 1) * DK], v_h))
        u0 = jnp.concatenate(u0, axis=0)
        u1 = jnp.concatenate(u1, axis=0)
        s1 = s0 * jnp.exp(gl0) + u0
        s_ref[...] = s1 * jnp.exp(gl1) + u1

        o_inter = jnp.where(row_first4, _mm(qm, s0.astype(bf16)), _mm(qm, s1.astype(bf16)))
        for h in range(H):
            a_h = jnp.where(tri_b, a[h * PAIR:(h + 1) * PAIR], 0.0).astype(bf16)
            o_h = _mm(a_h, v[:, h * DV:(h + 1) * DV]) + o_inter[h * PAIR:(h + 1) * PAIR]
            ms = jnp.mean(o_h * o_h, axis=-1, keepdims=True)
            o_n = o_h * lax.rsqrt(ms + EPS) * gout
            r_h = r_ref[0, pl.ds(r0, PAIR), h * DV:(h + 1) * DV]
            o_ref[0, pl.ds(r0, PAIR), h * DV:(h + 1) * DV] = (o_n * _silu(r_h)).astype(bf16)
        return carry

    lax.fori_loop(0, n_pairs, pair, 0, unroll=GLA_UNROLL)


def _gla(qg, kg, gk, kgt, gkt, vg, rg, g_gla_out):
    B, S, _ = qg.shape
    tg = min(TG_GLA, S)
    r = jnp.arange(PAIR)
    tri = ((r[:, None] // GLA_CHUNK == r[None, :] // GLA_CHUNK) & (r[None, :] <= r[:, None])).astype(bf16)
    row = lambda w: pl.BlockSpec((1, tg, w), lambda b, i: (b, i, 0))
    colT = pl.BlockSpec((1, tg // PAIR, GLA_QK_W, PAIR), lambda b, i: (b, i, 0, 0))
    const = lambda shape: pl.BlockSpec(shape, lambda b, i: (0,) * len(shape))
    return pl.pallas_call(
        functools.partial(_gla_kernel, n_pairs=tg // PAIR),
        out_shape=jax.ShapeDtypeStruct((B, S, GLA_V_W), bf16),
        grid=(B, S // tg),
        in_specs=[row(GLA_QK_W), row(GLA_QK_W), row(GLA_QK_W), colT, colT,
                  row(GLA_V_W), row(GLA_V_W), const((1, GLA_DV)),
                  const((PAIR, PAIR)), const((PAIR, PAIR))],
        out_specs=row(GLA_V_W),
        scratch_shapes=[pltpu.VMEM((GLA_QK_W, GLA_DV), f32)],
        compiler_params=pltpu.CompilerParams(
            dimension_semantics=("arbitrary", "arbitrary"), vmem_limit_bytes=VMEM_LIMIT),
        name="gla",
    )(qg, kg, gk, kgt, gkt, vg, rg, g_gla_out.reshape(1, GLA_DV), tri, tri.T)


def _attn_finish(o, gsub_ref, o_ref, lambda_init):
    ms = jnp.mean(o * o, axis=-1, keepdims=True)
    o_ref[0] = (o * lax.rsqrt(ms + EPS) * gsub_ref[...] * (1.0 - lambda_init)).astype(bf16)


def _attn_lambda(lamv_ref, lambda_init):
    lv = lamv_ref[...]
    return (jnp.exp(jnp.sum(lv[0:1] * lv[1:2], axis=-1, keepdims=True))
            - jnp.exp(jnp.sum(lv[2:3] * lv[3:4], axis=-1, keepdims=True)) + lambda_init)


def _attn_bounded_kernel(q_ref, k_ref, v_ref, bias_ref, lamv_ref, gsub_ref, o_ref, vaug_ref, *, lambda_init):
    qi = pl.program_id(2)
    tq = q_ref.shape[1]
    S = k_ref.shape[1]
    n_head = q_ref.shape[2] // (2 * DIFF_DQK)

    @pl.when(qi == 0)
    def _():
        lane = lax.broadcasted_iota(jnp.int32, (S, DIFF_DV), 1)
        for h in range(n_head):
            vaug_ref[h, :, :DIFF_DV] = v_ref[0, :, h * DIFF_DV:(h + 1) * DIFF_DV]
            vaug_ref[h, :, DIFF_DV:] = jnp.where(lane == 0, 1.0, 0.0).astype(bf16)

    lane = lax.broadcasted_iota(jnp.int32, (1, 2 * DIFF_DQK), 1)
    zero = jnp.zeros((), bf16)
    qs = []
    for h in range(n_head):
        q = q_ref[0, :, h * 2 * DIFF_DQK:(h + 1) * 2 * DIFF_DQK]
        qs.append((jnp.where(lane < DIFF_DQK, q, zero), jnp.where(lane < DIFF_DQK, zero, q)))

    def update(accs, k0, bias):
        out = []
        for h in range(n_head):
            kb = k_ref[0, pl.ds(k0, tq), h * 2 * DIFF_DQK:(h + 1) * 2 * DIFF_DQK]
            vb = vaug_ref[h, pl.ds(k0, tq), :]
            for c in range(2):
                s = _nt(qs[h][c], kb)
                if bias is not None:
                    s = s + bias[h][c]
                out.append(accs[2 * h + c] + _mm(jnp.exp2(s).astype(bf16), vb))
        return tuple(out)

    def far(kj, accs):
        return update(accs, pl.multiple_of(kj * tq, tq), None)

    def far_group(g, accs):
        for u in range(ATTN_UNROLL):
            accs = far(g * ATTN_UNROLL + u, accs)
        return accs

    def block_or_masked(accs, kj, rel):
        exists = kj >= 0
        k0 = pl.multiple_of(jnp.maximum(kj, 0) * tq, tq)
        if rel is None:
            tiles = [(jnp.where(exists, 0.0, NEG),) * 2] * n_head
        else:
            tiles = [tuple(jnp.where(exists, bias_ref[h, c, rel], NEG) for c in range(2)) for h in range(n_head)]
        return update(accs, k0, tiles)

    accs = tuple(jnp.zeros((tq, 2 * DIFF_DV), f32) for _ in range(2 * n_head))
    accs = block_or_masked(accs, qi, 1)
    accs = block_or_masked(accs, qi - 1, 0)
    for u in range(2, ATTN_UNROLL):
        accs = block_or_masked(accs, qi - u, None)
    n_far = jnp.maximum(qi + 1 - ATTN_UNROLL, 0)
    n_grp = n_far // ATTN_UNROLL
    accs = lax.fori_loop(0, n_grp, far_group, accs)
    accs = lax.fori_loop(n_grp * ATTN_UNROLL, n_far, far, accs)
    lam = _attn_lambda(lamv_ref, lambda_init)
    for h in range(n_head):
        a0, a1 = accs[2 * h], accs[2 * h + 1]
        o = a0[:, :DIFF_DV] / a0[:, DIFF_DV:DIFF_DV + 1] - lam * (a1[:, :DIFF_DV] / a1[:, DIFF_DV:DIFF_DV + 1])
        ms = jnp.mean(o * o, axis=-1, keepdims=True)
        o_ref[0, :, h * DIFF_DV:(h + 1) * DIFF_DV] = (
            o * lax.rsqrt(ms + EPS) * gsub_ref[...] * (1.0 - lambda_init)).astype(bf16)


def _attn_kernel(q_ref, k_ref, v_ref, bias_ref, lamv_ref, gsub_ref, o_ref, *, lambda_init):
    qi = pl.program_id(2)
    tq = q_ref.shape[1]
    q = q_ref[0]
    lane = lax.broadcasted_iota(jnp.int32, (1, 2 * DIFF_DQK), 1)
    zero = jnp.zeros((), bf16)
    qs = (jnp.where(lane < DIFF_DQK, q, zero), jnp.where(lane < DIFF_DQK, zero, q))

    def update(state, kb, vb, bias):
        new = []
        for c in range(2):
            m, l, acc = state[c]
            s = _nt(qs[c], kb)
            if bias is not None:
                s = s + bias[c]
            m_new = jnp.maximum(m, jnp.max(s, axis=-1, keepdims=True))
            alpha = jnp.exp2(m - m_new)
            p = jnp.exp2(s - m_new)
            l = alpha * l + jnp.sum(p, axis=-1, keepdims=True)
            acc = alpha * acc + _mm(p.astype(bf16), vb)
            new.append((m_new, l, acc))
        return tuple(new)

    init = tuple((jnp.full((tq, 1), NEG, f32), jnp.zeros((tq, 1), f32), jnp.zeros((tq, DIFF_DV), f32))
                 for _ in range(2))

    def far(kj, state):
        k0 = pl.multiple_of(kj * tq, tq)
        return update(state, k_ref[0, pl.ds(k0, tq), :], v_ref[0, pl.ds(k0, tq), :], None)

    state = lax.fori_loop(0, jnp.maximum(qi - 1, 0), far, init)

    kd0 = pl.multiple_of(qi * tq, tq)
    state = update(state, k_ref[0, pl.ds(kd0, tq), :], v_ref[0, pl.ds(kd0, tq), :],
                   (bias_ref[0, 0, 1], bias_ref[0, 1, 1]))
    kp0 = pl.multiple_of(jnp.maximum(qi - 1, 0) * tq, tq)
    has_prev = qi > 0
    state = update(state, k_ref[0, pl.ds(kp0, tq), :], v_ref[0, pl.ds(kp0, tq), :],
                   (jnp.where(has_prev, bias_ref[0, 0, 0], NEG), jnp.where(has_prev, bias_ref[0, 1, 0], NEG)))

    (_, l0, a0), (_, l1, a1) = state
    o = a0 / l0 - _attn_lambda(lamv_ref, lambda_init) * (a1 / l1)
    _attn_finish(o, gsub_ref, o_ref, lambda_init)


def _t5_bucket(n):
    max_exact = NUM_BUCKETS // 2
    nf = jnp.maximum(n, 1).astype(f32)
    large = max_exact + (jnp.log(nf / max_exact) / math.log(MAX_DISTANCE / max_exact)
                         * (NUM_BUCKETS - max_exact)).astype(jnp.int32)
    large = jnp.minimum(large, NUM_BUCKETS - 1)
    return jnp.where(n < max_exact, n, large)


def _toeplitz_kernel(w_ref, o_ref):
    n = o_ref.shape[-1]
    for t in range(2):
        rows = jnp.broadcast_to(w_ref[0, t:t + 1, :], (n, 2 * n))
        o_ref[0, 0, t] = pltpu.roll(rows, 0, 1, stride=1, stride_axis=0)[:, n:]


def _bias_tiles(rel_bias_table, S, n):
    HM = rel_bias_table.shape[1]
    assert n >= MAX_DISTANCE
    d = jnp.arange(2 * n, dtype=jnp.int32)
    by_dist = rel_bias_table[_t5_bucket(d)].astype(f32).T
    rel = (by_dist - rel_bias_table[NUM_BUCKETS - 1].astype(f32)[:, None]) * LOG2E
    i = jnp.arange(2 * n)
    w_diag = jnp.where(i[None, :] <= n, rel[:, jnp.clip(n - i, 0, 2 * n - 1)], NEG)
    w_prev = rel[:, jnp.clip(2 * n - i, 0, 2 * n - 1)]
    w = jnp.stack([w_prev, w_diag], axis=1)
    return pl.pallas_call(
        _toeplitz_kernel,
        out_shape=jax.ShapeDtypeStruct((HM // 2, 2, 2, n, n), f32),
        grid=(HM // 2, 2),
        in_specs=[pl.BlockSpec((1, 2, 2 * n), lambda h, m: (h * 2 + m, 0, 0))],
        out_specs=pl.BlockSpec((1, 1, 2, n, n), lambda h, m: (h, m, 0, 0, 0)),
        compiler_params=pltpu.CompilerParams(vmem_limit_bytes=VMEM_LIMIT),
        name="bias_tiles",
    )(w)


def _attn(qd, kd, vd, bias_tiles, lamv, g_subln, lambda_init, bounded):
    B, S, _ = qd.shape
    H = N_DIFF_HEADS
    tq = min(TQ, S)
    body = _attn_bounded_kernel if bounded else _attn_kernel
    hs = ATTN_HEADS if bounded else 1
    scratch = [pltpu.VMEM((hs, S, 2 * DIFF_DV), bf16)] if bounded else []
    return pl.pallas_call(
        functools.partial(body, lambda_init=lambda_init),
        out_shape=jax.ShapeDtypeStruct((B, S, DIFF_V_W), bf16),
        scratch_shapes=scratch,
        grid=(B, H // hs, S // tq),
        in_specs=[pl.BlockSpec((1, tq, hs * 2 * DIFF_DQK), lambda b, h, i: (b, i, h)),
                  pl.BlockSpec((1, S, hs * 2 * DIFF_DQK), lambda b, h, i: (b, 0, h)),
                  pl.BlockSpec((1, S, hs * DIFF_DV), lambda b, h, i: (b, 0, h)),
                  pl.BlockSpec((hs, 2, 2, tq, tq), lambda b, h, i: (h, 0, 0, 0, 0)),
                  pl.BlockSpec((4, DIFF_DQK), lambda b, h, i: (0, 0)),
                  pl.BlockSpec((1, DIFF_DV), lambda b, h, i: (0, 0))],
        out_specs=pl.BlockSpec((1, tq, hs * DIFF_DV), lambda b, h, i: (b, i, h)),
        compiler_params=pltpu.CompilerParams(
            dimension_semantics=("arbitrary", "arbitrary", "arbitrary"),
            vmem_limit_bytes=VMEM_LIMIT_ATTN if bounded else VMEM_LIMIT),
        name="attn_bounded" if bounded else "attn_online",
    )(qd, kd, vd, bias_tiles, lamv, g_subln.reshape(1, DIFF_DV))


def _scores_bounded(rel_bias_table, g_qnorm, g_knorm):
    qk = DIFF_DQK ** 0.5 * jnp.max(jnp.abs(g_qnorm)) * jnp.max(jnp.abs(g_knorm)) * NORM_SLACK
    rel = jnp.max(jnp.abs(rel_bias_table - rel_bias_table[NUM_BUCKETS - 1:]))
    return qk + rel <= SAFE_SCORE


def _rows_to_tiles(x, ref):
    n = x.shape[0]
    for c in range(ROW_TILE):
        ref[pl.ds(c, n, stride=ROW_TILE), :] = x[:, c * LANES:(c + 1) * LANES]


def _tiles_to_rows(ref, n):
    return jnp.concatenate([ref[pl.ds(c, n, stride=ROW_TILE), :] for c in range(ROW_TILE)], axis=1)


def _outproj_kernel(og_ref, od_ref, x_ref, mod_ref, wo_ref, g2_ref, wr_ref, br_ref,
                    x1_ref, hp_ref, lg_ref):
    half = og_ref.shape[2]
    sub = og_ref.shape[1] // INPROJ_SUB
    for t in range(INPROJ_SUB):
        rows = slice(t * sub, (t + 1) * sub)
        mix = _mm(og_ref[0, rows, :], wo_ref[:half, :]) + _mm(od_ref[0, rows, :], wo_ref[half:, :])
        x1 = x_ref[0, rows, :] + mod_ref[0, 2:3, :] * mix
        x1_ref[0, rows, :] = x1
        ms = jnp.mean(x1 * x1, axis=-1, keepdims=True)
        y = x1 * lax.rsqrt(ms + EPS) * g2_ref[...]
        h = (y * (1.0 + mod_ref[0, 4:5, :]) + mod_ref[0, 3:4, :]).astype(bf16)
        lg_ref[rows, :] = _mm(h, wr_ref[...]) + br_ref[...]
        _rows_to_tiles(h.astype(f32), hp_ref.at[pl.ds(t * sub * ROW_TILE, sub * ROW_TILE)])


def _outproj(og, od, x, mod, w_out, g_norm2, w_router, b_router):
    B, S, D = x.shape
    assert D == ROW_TILE * LANES, "the token-tile layout needs a model row to fill one (8,128) tile"
    E = w_router.shape[1]
    tm = TM_IN
    nj = S // tm
    w_r = jnp.zeros((D, LANES), f32).at[:, :E].set(w_router).astype(bf16)
    b_r = jnp.full((1, LANES), NEG, f32).at[0, :E].set(b_router)
    const = lambda shape: pl.BlockSpec(shape, lambda b, i: (0,) * len(shape))
    return pl.pallas_call(
        _outproj_kernel,
        out_shape=[jax.ShapeDtypeStruct((B, S, D), f32),
                   jax.ShapeDtypeStruct((B * S * ROW_TILE, LANES), f32),
                   jax.ShapeDtypeStruct((B * S, LANES), f32)],
        grid=(B, nj),
        in_specs=[pl.BlockSpec((1, tm, og.shape[2]), lambda b, i: (b, i, 0)),
                  pl.BlockSpec((1, tm, od.shape[2]), lambda b, i: (b, i, 0)),
                  pl.BlockSpec((1, tm, D), lambda b, i: (b, i, 0)),
                  pl.BlockSpec((1, 6, D), lambda b, i: (b, 0, 0)),
                  const((w_out.shape[0], D)), const((1, D)), const((D, LANES)), const((1, LANES))],
        out_specs=[pl.BlockSpec((1, tm, D), lambda b, i: (b, i, 0)),
                   pl.BlockSpec((tm * ROW_TILE, LANES), lambda b, i: (b * nj + i, 0)),
                   pl.BlockSpec((tm, LANES), lambda b, i: (b * nj + i, 0))],
        compiler_params=pltpu.CompilerParams(
            dimension_semantics=("arbitrary", "arbitrary"), vmem_limit_bytes=VMEM_LIMIT),
        name="outproj",
    )(og, od, x, mod, w_out.astype(bf16), g_norm2.reshape(1, D), w_r, b_r)


def _route_kernel(lg_ref, lt_ref, ut_ref, pk_ref, lp_ref, rw_ref, cnt_ref, snap_ref, run_ref):
    @pl.when(pl.program_id(0) == 0)
    def _():
        run_ref[...] = jnp.zeros_like(run_ref)

    x = lg_ref[...]
    tr = x.shape[0]
    lane = lax.broadcasted_iota(jnp.int32, (tr, LANES), 1)
    lane_f = lane.astype(f32)
    vals, hots = [], []
    for _ in range(TOP_K):
        m = jnp.max(x, axis=-1, keepdims=True)
        idx = jnp.min(jnp.where(x == m, lane_f, float(LANES)), axis=-1, keepdims=True)
        hot = lane_f == idx
        x = jnp.where(hot, -jnp.inf, x)
        vals.append(m)
        hots.append(hot)
    ex = [jnp.exp(v - vals[0]) for v in vals]
    den = ex[0] + ex[1] + ex[2] + ex[3]
    sel = (hots[0] | hots[1] | hots[2] | hots[3]).astype(f32)
    rank = _mm(lt_ref[...], sel.astype(bf16)) + run_ref[...]
    run_ref[...] = run_ref[...] + jnp.sum(sel, axis=0, keepdims=True)
    cnt_ref[...] = run_ref[...]
    base = rank[0:1, :]
    snap_ref[0] = base
    n_hi, n_lo = _split(jnp.sum(sel, axis=0, keepdims=True))
    start = _mm(n_hi, ut_ref[...]) + _mm(n_lo, ut_ref[...])
    place = rank - base + start
    pos = jnp.zeros((tr, LANES), f32)
    rw = jnp.zeros((tr, LANES), f32)
    for k in range(TOP_K):
        rk = jnp.sum(jnp.where(hots[k], place, 0.0), axis=-1, keepdims=True) * ROW_TILE
        pos = jnp.where(lane == k, rk, pos)
        rw = jnp.where(lane == k, ex[k] / den, rw)
    rw_ref[...] = rw
    p_hi, p_lo = _split(pos)
    lp_ref[0] = (_nt(pk_ref[...], p_hi) + _nt(pk_ref[...], p_lo)).astype(jnp.int32)


def _route(logits):
    T = logits.shape[0]
    tr = min(TR, T)
    assert tr == TD, "routing and dispatch share one token tile"
    r = jnp.arange(tr)
    lt = (r[None, :] < r[:, None]).astype(bf16)
    e = jnp.arange(LANES)
    ut = (e[:, None] < e[None, :]).astype(bf16)
    pick = (jnp.arange(ROW_TILE)[:, None] == e[None, :]).astype(bf16)
    return pl.pallas_call(
        _route_kernel,
        out_shape=[jax.ShapeDtypeStruct((T // tr, ROW_TILE, tr), jnp.int32),
                   jax.ShapeDtypeStruct((T, LANES), f32),
                   jax.ShapeDtypeStruct((1, LANES), f32),
                   jax.ShapeDtypeStruct((T // tr, 1, LANES), f32)],
        grid=(T // tr,),
        in_specs=[pl.BlockSpec((tr, LANES), lambda i: (i, 0)),
                  pl.BlockSpec((tr, tr), lambda i: (0, 0)),
                  pl.BlockSpec((LANES, LANES), lambda i: (0, 0)),
                  pl.BlockSpec((ROW_TILE, LANES), lambda i: (0, 0))],
        out_specs=[pl.BlockSpec((1, ROW_TILE, tr), lambda i: (i, 0, 0)),
                   pl.BlockSpec((tr, LANES), lambda i: (i, 0)),
                   pl.BlockSpec((1, LANES), lambda i: (0, 0)),
                   pl.BlockSpec((1, 1, LANES), lambda i: (i, 0, 0))],
        scratch_shapes=[pltpu.VMEM((1, LANES), f32)],
        compiler_params=pltpu.CompilerParams(dimension_semantics=("arbitrary",)),
        name="route",
    )(logits, lt, ut, pick)


def _run_copies(list_ref, hbm_ref, stage_ref, sem, to_hbm):
    for c, size in enumerate(RUN_SIZES):
        def one(i, carry, c=c, size=size):
            s0 = list_ref[c * N_RUN + i]
            d0 = list_ref[LIST_DST + c * N_RUN + i]
            stage = stage_ref.at[pl.ds(pl.multiple_of(s0, ROW_TILE), size * ROW_TILE)]
            rows = hbm_ref.at[pl.ds(pl.multiple_of(d0, ROW_TILE), size * ROW_TILE)]
            src, dst = (stage, rows) if to_hbm else (rows, stage)
            pltpu.make_async_copy(src, dst, sem).start(priority=c % 2)
            return carry
        lax.fori_loop(0, list_ref[LIST_CNT + c], one, 0)


def _copy_lists(run_dst, run_n, run_off):
    n_tiles = run_n.shape[0]
    size = jnp.asarray(RUN_SIZES, jnp.int32)[None, :, None]
    n = run_n[:, None, :]
    bit = (n & size) != 0
    before = n & ~(2 * size - 1)
    place = jnp.cumsum(bit, axis=-1) - 1
    pick = bit[:, :, None, :] & (place[:, :, None, :] == jnp.arange(N_RUN)[None, None, :, None])

    def compact(v):
        return jnp.sum(jnp.where(pick, v[:, :, None, :], 0), axis=-1) * ROW_TILE
    src = compact(run_off[:, None, :] + before)
    dst = compact(run_dst[:, None, :] + before)
    cnt = jnp.sum(bit, axis=-1).astype(jnp.int32)
    pad = jnp.zeros((n_tiles, LIST_LEN - LIST_CNT - len(RUN_SIZES)), jnp.int32)
    lists = jnp.concatenate([src.reshape(n_tiles, -1), dst.reshape(n_tiles, -1), cnt, pad], axis=1)
    return lists.reshape(-1).astype(jnp.int32)


def _dispatch_kernel(pend_ref, cnt_ref, nu_ref, lpos_ref, list_ref, h_ref, xs_ref,
                     zero_ref, stage_ref, sem, zsem):
    n_tok = h_ref.shape[0] // ROW_TILE
    blk_rows = FFN_BLK * ROW_TILE

    @pl.when(pl.program_id(0) == 0)
    def _():
        zero_ref[...] = jnp.zeros_like(zero_ref)
        n_exp = pend_ref.shape[0]

        def last_block(e):
            return xs_ref.at[pl.ds(pl.multiple_of((pend_ref[e] - FFN_BLK) * ROW_TILE, blk_rows), blk_rows)]

        def zfill(e, c):
            @pl.when(cnt_ref[e] > 0)
            def _():
                pltpu.make_async_copy(zero_ref, last_block(e), zsem).start()
            return c

        def zwait(e, c):
            @pl.when(cnt_ref[e] > 0)
            def _():
                pltpu.make_async_copy(zero_ref, last_block(e), zsem).wait()
            return c

        lax.fori_loop(0, n_exp, zfill, 0)
        lax.fori_loop(0, n_exp, zwait, 0)

        def tail_block(i):
            return xs_ref.at[pl.ds(pl.multiple_of(i * blk_rows, blk_rows), blk_rows)]

        def tfill(i, c):
            pltpu.make_async_copy(zero_ref, tail_block(i), zsem).start()
            return c

        def twait(i, c):
            pltpu.make_async_copy(zero_ref, tail_block(i), zsem).wait()
            return c

        n_blk = xs_ref.shape[0] // blk_rows
        lax.fori_loop(nu_ref[0], n_blk, tfill, 0)
        lax.fori_loop(nu_ref[0], n_blk, twait, 0)

    step = pl.program_id(0)
    slot = step % 2
    stage = stage_ref.at[slot]

    def place(g, c):
        for u in range(DMA_UNROLL):
            r = g * DMA_UNROLL + u
            row = h_ref[pl.ds(pl.multiple_of(r * ROW_TILE, ROW_TILE), ROW_TILE), :]
            for k in range(TOP_K):
                p = lpos_ref[k * n_tok + r]
                stage[pl.ds(pl.multiple_of(p, ROW_TILE), ROW_TILE), :] = row
        return c
    lax.fori_loop(0, n_tok // DMA_UNROLL, place, 0)

    _run_copies(list_ref, xs_ref, stage, sem.at[slot], to_hbm=True)

    def drain(s):
        pltpu.make_async_copy(stage_ref.at[s], stage_ref.at[s], sem.at[s]).wait()

    @pl.when(step > 0)
    def _():
        drain(1 - slot)

    @pl.when(step == pl.num_programs(0) - 1)
    def _():
        drain(slot)


def _dispatch(p_ends, counts, n_used, lpos_flat, lists, hp, n_rows):
    T = hp.shape[0] // ROW_TILE
    grid_spec = pltpu.PrefetchScalarGridSpec(
        num_scalar_prefetch=3,
        grid=(T // TD,),
        in_specs=[pl.BlockSpec((TD * TOP_K,), lambda i, *_: (i,), memory_space=pltpu.SMEM),
                  pl.BlockSpec((LIST_LEN,), lambda i, *_: (i,), memory_space=pltpu.SMEM),
                  pl.BlockSpec((TD * ROW_TILE, LANES), lambda i, *_: (i, 0))],
        out_specs=pl.BlockSpec(memory_space=pl.ANY),
        scratch_shapes=[pltpu.VMEM((FFN_BLK * ROW_TILE, LANES), f32),
                        pltpu.VMEM((2, TD * TOP_K * ROW_TILE, LANES), f32),
                        pltpu.SemaphoreType.DMA((2,)), pltpu.SemaphoreType.DMA(())],
    )
    return pl.pallas_call(
        _dispatch_kernel,
        out_shape=jax.ShapeDtypeStruct((n_rows * ROW_TILE, LANES), f32),
        grid_spec=grid_spec,
        compiler_params=pltpu.CompilerParams(dimension_semantics=("arbitrary",), vmem_limit_bytes=VMEM_LIMIT),
        name="dispatch",
    )(p_ends, counts, n_used, lpos_flat, lists, hp)


def _ffn_kernel(be_ref, nu_ref, nx_ref, par_ref, val_ref, xs_ref, wgu_hbm, bgu_ref, wd_hbm, bd_ref, ys_ref,
                wgu32_ref, wd32_ref, wgu_ref, wd_ref, sem):
    i = pl.program_id(0)
    used = i < nu_ref[0]
    new_expert = (i == 0) | (be_ref[i] != be_ref[jnp.maximum(i - 1, 0)])
    slot = par_ref[i]

    def weight_copies(e, s):
        return (pltpu.make_async_copy(wgu_hbm.at[e], wgu32_ref.at[s], sem.at[0, s]),
                pltpu.make_async_copy(wd_hbm.at[e], wd32_ref.at[s], sem.at[1, s]))

    @pl.when(i == 0)
    def _():
        for cp in weight_copies(be_ref[0], 0):
            cp.start()

    @pl.when(used & new_expert)
    def _():
        for cp in weight_copies(be_ref[i], slot):
            cp.wait()

        @pl.when(nx_ref[i] >= 0)
        def _():
            for cp in weight_copies(nx_ref[i], 1 - slot):
                cp.start()

        rows = 128

        def cast(src, dst):
            def body(r, c):
                r0 = pl.multiple_of(r * rows, rows)
                dst[pl.ds(r0, rows), :] = src[slot, pl.ds(r0, rows), :].astype(bf16)
                return c
            lax.fori_loop(0, src.shape[1] // rows, body, 0)
        cast(wgu32_ref, wgu_ref)
        cast(wd32_ref, wd_ref)

    def ffn_rows(n):
        tiles = pl.ds(0, n * ROW_TILE)
        F = wd_ref.shape[0]
        xrow = _tiles_to_rows(xs_ref.at[tiles], n).astype(bf16)
        acc = None
        fc = F // 2
        for c in range(2):
            def gu(col0):
                return _mm(xrow, wgu_ref[:, col0:col0 + fc]) + bgu_ref[0, :, col0:col0 + fc]
            gate = jnp.minimum(gu(c * fc), SWIGLU_LIMIT)
            up = jnp.clip(gu(F + c * fc), -SWIGLU_LIMIT, SWIGLU_LIMIT)
            y = (up + 1.0) * (gate * jax.nn.sigmoid(SWIGLU_ALPHA * gate))
            part = _mm(y.astype(bf16), wd_ref[c * fc:(c + 1) * fc, :])
            acc = part if acc is None else acc + part
        _rows_to_tiles(acc + bd_ref[0], ys_ref.at[tiles])

    half = FFN_BLK // 2
    occupied = val_ref[i]

    @pl.when(used & (occupied > half))
    def _():
        ffn_rows(FFN_BLK)

    @pl.when(used & (occupied <= half))
    def _():
        ffn_rows(half)
        ys_ref[pl.ds(half * ROW_TILE, half * ROW_TILE), :] = jnp.zeros((half * ROW_TILE, LANES), f32)

    @pl.when(jnp.logical_not(used))
    def _():
        ys_ref[...] = jnp.zeros_like(ys_ref)


def _ffn(block_e, n_used, occupied, xs, w_gate_up, b_gate_up, w_down, b_down):
    E, D, F2 = w_gate_up.shape
    F = F2 // 2
    P = xs.shape[0] // ROW_TILE
    nb = P // FFN_BLK
    rows = FFN_BLK * ROW_TILE

    idx = jnp.arange(nb, dtype=jnp.int32)
    live = idx < n_used[0]
    later_other = (block_e[None, :] != block_e[:, None]) & (idx[None, :] > idx[:, None]) & live[None, :]
    nxt = jnp.where(jnp.any(later_other, axis=1), block_e[jnp.argmax(later_other, axis=1)], -1).astype(jnp.int32)
    starts = jnp.concatenate([jnp.ones((1,), jnp.int32), (block_e[1:] != block_e[:-1]).astype(jnp.int32)])
    parity = ((jnp.cumsum(starts) - 1) % 2).astype(jnp.int32)

    def blk(i, nu):
        return jnp.minimum(i, nu[0] - 1)

    grid_spec = pltpu.PrefetchScalarGridSpec(
        num_scalar_prefetch=5,
        grid=(nb,),
        in_specs=[pl.BlockSpec((rows, LANES), lambda i, be, nu, *_: (blk(i, nu), 0)),
                  pl.BlockSpec(memory_space=pl.ANY),
                  pl.BlockSpec((1, 1, F2), lambda i, be, nu, *_: (be[blk(i, nu)], 0, 0)),
                  pl.BlockSpec(memory_space=pl.ANY),
                  pl.BlockSpec((1, 1, D), lambda i, be, nu, *_: (be[blk(i, nu)], 0, 0))],
        out_specs=pl.BlockSpec((rows, LANES), lambda i, *_: (i, 0)),
        scratch_shapes=[pltpu.VMEM((2, D, F2), f32), pltpu.VMEM((2, F, D), f32),
                        pltpu.VMEM((D, F2), bf16), pltpu.VMEM((F, D), bf16),
                        pltpu.SemaphoreType.DMA((2, 2))],
    )
    return pl.pallas_call(
        _ffn_kernel,
        out_shape=jax.ShapeDtypeStruct((P * ROW_TILE, LANES), f32),
        grid_spec=grid_spec,
        compiler_params=pltpu.CompilerParams(
            dimension_semantics=("arbitrary",), vmem_limit_bytes=VMEM_LIMIT_FFN),
        name="ffn",
    )(block_e, n_used, nxt, parity, occupied, xs, w_gate_up, b_gate_up.reshape(E, 1, F2), w_down,
      b_down.reshape(E, 1, D))


def _combine_kernel(lpos_ref, lcur_ref, lnext_ref, ys_ref, x1_ref, rw_ref, mod_ref, o_ref,
                    stage_ref, acc_ref, wb_ref, sem):
    step = pl.program_id(0) * pl.num_programs(1) + pl.program_id(1)
    n_steps = pl.num_programs(0) * pl.num_programs(1)
    slot = step % 2

    def fetch(list_ref, s):
        _run_copies(list_ref, ys_ref, stage_ref.at[s], sem.at[s], to_hbm=False)

    @pl.when(step == 0)
    def _():
        fetch(lcur_ref, 0)

    @pl.when(step + 1 < n_steps)
    def _():
        fetch(lnext_ref, 1 - slot)

    pltpu.make_async_copy(stage_ref.at[slot], stage_ref.at[slot], sem.at[slot]).wait()

    rw = rw_ref[...]
    for k in range(TOP_K):
        wb_ref[k] = jnp.broadcast_to(rw[:, k:k + 1], (TD, LANES))

    def staged(r, k):
        p = lpos_ref[k * TD + r]
        return stage_ref[slot, pl.ds(pl.multiple_of(p, ROW_TILE), ROW_TILE), :]

    def token(r, c):
        acc = wb_ref[0, pl.ds(r, 1), :] * staged(r, 0)
        for k in range(1, TOP_K):
            acc = acc + wb_ref[k, pl.ds(r, 1), :] * staged(r, k)
        acc_ref[pl.ds(pl.multiple_of(r * ROW_TILE, ROW_TILE), ROW_TILE), :] = acc
        return c
    lax.fori_loop(0, TD, token, 0, unroll=DMA_UNROLL)
    o_ref[0] = x1_ref[0] + mod_ref[0, 5:6, :] * _tiles_to_rows(acc_ref, TD)


def _combine(lpos_flat, lists, ys, x1, rw, mod):
    B, S, D = x1.shape
    nj = S // TD
    n_steps = B * nj
    return pl.pallas_call(
        _combine_kernel,
        out_shape=jax.ShapeDtypeStruct((B, S, D), f32),
        grid=(B, nj),
        in_specs=[pl.BlockSpec((TD * TOP_K,), lambda b, j: (b * nj + j,), memory_space=pltpu.SMEM),
                  pl.BlockSpec((LIST_LEN,), lambda b, j: (b * nj + j,), memory_space=pltpu.SMEM),
                  pl.BlockSpec((LIST_LEN,), lambda b, j: (jnp.minimum(b * nj + j + 1, n_steps - 1),),
                               memory_space=pltpu.SMEM),
                  pl.BlockSpec(memory_space=pl.ANY),
                  pl.BlockSpec((1, TD, D), lambda b, j: (b, j, 0)),
                  pl.BlockSpec((TD, LANES), lambda b, j: (b * nj + j, 0)),
                  pl.BlockSpec((1, 6, D), lambda b, j: (b, 0, 0))],
        out_specs=pl.BlockSpec((1, TD, D), lambda b, j: (b, j, 0)),
        scratch_shapes=[pltpu.VMEM((2, TD * TOP_K * ROW_TILE, LANES), f32),
                        pltpu.VMEM((TD * ROW_TILE, LANES), f32), pltpu.VMEM((TOP_K, TD, LANES), f32),
                        pltpu.SemaphoreType.DMA((2,))],
        compiler_params=pltpu.CompilerParams(
            dimension_semantics=("arbitrary", "arbitrary"), vmem_limit_bytes=VMEM_LIMIT),
        name="combine",
    )(lpos_flat, lists, lists, ys, x1, rw, mod)


def _moe(hp, logits, x1, mod, w_gate_up, b_gate_up, w_down, b_down):
    T = logits.shape[0]
    E = w_gate_up.shape[0]
    lp, rw, cnt, snap = _route(logits)
    lpos = lp[:, :TOP_K, :].reshape(-1)
    counts = cnt[0, :E].astype(jnp.int32)
    padded = ((counts + FFN_BLK - 1) // FFN_BLK) * FFN_BLK
    p_ends = jnp.cumsum(padded)
    p_starts = p_ends - padded
    nb = -(-T * TOP_K // FFN_BLK) + E
    n_used = jnp.maximum(p_ends[-1:] // FFN_BLK, 1).astype(jnp.int32)
    blk_start = jnp.arange(nb, dtype=jnp.int32) * FFN_BLK
    block_e = jnp.minimum(jnp.sum(p_ends[None, :] <= blk_start[:, None], axis=1), E - 1).astype(jnp.int32)
    assert E == N_RUN
    base = snap.reshape(T // TD, LANES)[:, :E].astype(jnp.int32)
    run_n = jnp.concatenate([base[1:], counts[None, :]], axis=0) - base
    run_off = jnp.cumsum(run_n, axis=1) - run_n
    run_dst = p_starts[None, :].astype(jnp.int32) + base
    lists = _copy_lists(run_dst, run_n, run_off)
    xs = _dispatch(p_ends.astype(jnp.int32), counts, n_used, lpos, lists, hp, nb * FFN_BLK)
    occupied = jnp.clip(counts[block_e] - (blk_start - p_starts[block_e]), 0, FFN_BLK).astype(jnp.int32)
    ys = _ffn(block_e, n_used, occupied, xs, w_gate_up, b_gate_up, w_down, b_down)
    return _combine(lpos, lists, ys, x1, rw, mod)


def kernel(x, c, rel_bias_table, w_ada, b_ada, g_norm1, w_in, w_gk_up, b_gk_up, g_gla_out, g_qnorm, g_knorm, lambda_q1, lambda_k1, lambda_q2, lambda_k2, g_subln, w_out, g_norm2, w_router, b_router, w_gate_up, b_gate_up, w_down, b_down):
    B, S, D = x.shape
    depth = w_ada.shape[0]
    bias_tiles = _bias_tiles(rel_bias_table, S, min(TQ, S))
    for l in range(depth):
        lambda_init = 0.8 - 0.6 * math.exp(-0.3 * l)
        mod = _ada(c, w_ada[l], b_ada[l])
        qg, kg, gk, kgt, gkt, vg, rg, qd, kd, vd = _inproj(
            x, mod, g_norm1[l], w_in[l], w_gk_up[l], b_gk_up[l], g_qnorm[l], g_knorm[l])
        og = _gla(qg, kg, gk, kgt, gkt, vg, rg, g_gla_out[l])
        lamv = jnp.stack([lambda_q1[l], lambda_k1[l], lambda_q2[l], lambda_k2[l]]).astype(f32)
        od = lax.cond(_scores_bounded(rel_bias_table, g_qnorm[l], g_knorm[l]),
                      functools.partial(_attn, lambda_init=lambda_init, bounded=True),
                      functools.partial(_attn, lambda_init=lambda_init, bounded=False),
                      qd, kd, vd, bias_tiles, lamv, g_subln[l])
        x1, hp, logits = _outproj(og, od, x, mod, w_out[l], g_norm2[l], w_router[l], b_router[l])
        x = _moe(hp, logits, x1, mod, w_gate_up[l], b_gate_up[l], w_down[l], b_down[l])
    return x
```

```python
import functools
import math

import jax
import jax.numpy as jnp
from jax import lax
from jax.experimental import pallas as pl
from jax.experimental.pallas import tpu as pltpu

f32 = jnp.float32
bf16 = jnp.bfloat16

N_GLA_HEADS = 4
GLA_DK = 64
GLA_DV = 128
GLA_GATE_RANK = 16
GLA_GATE_NORM = 16.0
GLA_CHUNK = 64
N_DIFF_HEADS = 4
DIFF_DQK = 64
DIFF_DV = 128
NUM_BUCKETS = 32
MAX_DISTANCE = 128
TOP_K = 4
SWIGLU_LIMIT = 7.0
SWIGLU_ALPHA = 1.702
EPS = 1e-6

GLA_QK_W = N_GLA_HEADS * GLA_DK
GLA_V_W = N_GLA_HEADS * GLA_DV
DIFF_QK_W = N_DIFF_HEADS * 2 * DIFF_DQK
DIFF_V_W = N_DIFF_HEADS * DIFF_DV

LANES = 128
NEG = -1e30
LOG2E = math.log2(math.e)
SAFE_SCORE = 40.0
NORM_SLACK = 1.02
VMEM_LIMIT = 48 * 1024 * 1024
VMEM_LIMIT_FFN = 58 * 1024 * 1024
VMEM_LIMIT_ATTN = 58 * 1024 * 1024

TM_IN = 512
INPROJ_SUB = 2
TG_GLA = 1024
PAIR = 2 * GLA_CHUNK
GLA_UNROLL = 4
TQ = 512
ATTN_UNROLL = 4
ATTN_HEADS = 2
TR = 512
TD = 512
ROW_TILE = 8
DMA_UNROLL = 8
N_RUN = 32
RUN_SIZES = tuple(TD >> b for b in range(TD.bit_length()))
LIST_DST = len(RUN_SIZES) * N_RUN
LIST_CNT = 2 * LIST_DST
LIST_LEN = 1024
assert LIST_CNT + len(RUN_SIZES) <= LIST_LEN
FFN_BLK = 512


def _nt(a, b):
    return lax.dot_general(a, b, (((1,), (1,)), ((), ())), preferred_element_type=f32)


def _mm(a, b):
    return jnp.dot(a, b, preferred_element_type=f32)


def _split(x):
    hi = x.astype(bf16)
    lo = (x - hi.astype(f32)).astype(bf16)
    return hi, lo


def _silu(x):
    return x * jax.nn.sigmoid(x)


def _ada_kernel(c_ref, w_ref, b_ref, o_ref):
    c = c_ref[...]
    o_ref[...] = _mm(_silu(c).astype(bf16), w_ref[...].astype(bf16)) + b_ref[...]


def _ada(c, w_ada, b_ada):
    B, D = c.shape
    N = w_ada.shape[1]
    bp = ROW_TILE
    assert B <= bp
    cp = jnp.zeros((bp, D), f32).at[:B].set(c)
    tn = N // 4
    out = pl.pallas_call(
        _ada_kernel,
        out_shape=jax.ShapeDtypeStruct((bp, N), f32),
        grid=(N // tn,),
        in_specs=[pl.BlockSpec((bp, D), lambda j: (0, 0)),
                  pl.BlockSpec((D, tn), lambda j: (0, j)),
                  pl.BlockSpec((1, tn), lambda j: (0, j))],
        out_specs=pl.BlockSpec((bp, tn), lambda j: (0, j)),
        compiler_params=pltpu.CompilerParams(vmem_limit_bytes=VMEM_LIMIT),
        name="ada",
    )(cp, w_ada, b_ada.reshape(1, N))
    return out[:B].reshape(B, 6, D)


def _inproj_kernel(x_ref, mod_ref, g1_ref, w_ref, wkt_ref, wlo_ref, wup_ref, wupt_ref,
                   bup_ref, bupt_ref, gqk_ref, grp_ref, grpt_ref,
                   qg_ref, kg_ref, gk_ref, kgt_ref, gkt_ref, vg_ref, rg_ref,
                   qd_ref, kd_ref, vd_ref, wm_ref):
    @pl.when((pl.program_id(0) == 0) & (pl.program_id(1) == 0))
    def _():
        split = wm_ref.shape[1] // 2
        wm_ref[:, :split] = w_ref[:, :split]
        wm_ref[:, split:] = w_ref[:, split + GLA_GATE_RANK:]
    tm = x_ref.shape[1]
    sub = tm // INPROJ_SUB
    for t in range(INPROJ_SUB):
        _inproj_rows(slice(t * sub, (t + 1) * sub), x_ref, mod_ref, g1_ref, wm_ref, wkt_ref, wlo_ref, wup_ref,
                     wupt_ref, bup_ref, bupt_ref, gqk_ref, grp_ref, grpt_ref, qg_ref, kg_ref, gk_ref, kgt_ref,
                     gkt_ref, vg_ref, rg_ref, qd_ref, kd_ref, vd_ref)


def _inproj_rows(rows, x_ref, mod_ref, g1_ref, wm_ref, wkt_ref, wlo_ref, wup_ref, wupt_ref,
                 bup_ref, bupt_ref, gqk_ref, grp_ref, grpt_ref,
                 qg_ref, kg_ref, gk_ref, kgt_ref, gkt_ref, vg_ref, rg_ref, qd_ref, kd_ref, vd_ref):
    x = x_ref[0, rows, :]
    ms = jnp.mean(x * x, axis=-1, keepdims=True)
    y = x * lax.rsqrt(ms + EPS) * g1_ref[...]
    h = (y * (1.0 + mod_ref[0, 1:2, :]) + mod_ref[0, 0:1, :]).astype(bf16)

    def proj(a, b):
        return _mm(h, wm_ref[:, a:b])

    o = 0
    qg_ref[0, rows, :] = proj(o, o + GLA_QK_W); o += GLA_QK_W
    kg_ref[0, rows, :] = proj(o, o + GLA_QK_W); o += GLA_QK_W
    vg_ref[0, rows, :] = proj(o, o + GLA_V_W).astype(bf16); o += GLA_V_W
    rg_ref[0, rows, :] = proj(o, o + GLA_V_W); o += GLA_V_W
    qk = proj(o, o + 2 * DIFF_QK_W); o += 2 * DIFF_QK_W
    vd_ref[0, rows, :] = proj(o, o + DIFF_V_W).astype(bf16)

    slab0 = rows.start // PAIR
    kgt = _nt(wkt_ref[...], h)
    for j in range(kgt.shape[1] // PAIR):
        kgt_ref[0, slab0 + j] = kgt[:, j * PAIR:(j + 1) * PAIR]

    lo = _mm(h, wlo_ref[...]).astype(bf16)
    z = _mm(lo, wup_ref[...]) + bup_ref[...]
    gk_ref[0, rows, :] = (jnp.minimum(z, 0.0) - jnp.log1p(jnp.exp(-jnp.abs(z)))) * (1.0 / GLA_GATE_NORM)
    zt = _nt(wupt_ref[...], lo) + bupt_ref[...]
    gkt = (jnp.minimum(zt, 0.0) - jnp.log1p(jnp.exp(-jnp.abs(zt)))) * (1.0 / GLA_GATE_NORM)
    for j in range(gkt.shape[1] // PAIR):
        gkt_ref[0, slab0 + j] = gkt[:, j * PAIR:(j + 1) * PAIR]

    sq_hi, sq_lo = _split(qk * qk)
    gs = _mm(sq_hi, grp_ref[...]) + _mm(sq_lo, grp_ref[...])
    r = lax.rsqrt(gs * (1.0 / DIFF_DQK) + EPS)
    r_hi, r_lo = _split(r)
    rb = _mm(r_hi, grpt_ref[...]) + _mm(r_lo, grpt_ref[...])
    qkn = qk * rb * gqk_ref[...]
    qd_ref[0, rows, :] = qkn[:, :DIFF_QK_W].astype(bf16)
    kd_ref[0, rows, :] = qkn[:, DIFF_QK_W:].astype(bf16)


def _inproj(x, mod, g_norm1, w_in, w_gk_up, b_gk_up, g_qnorm, g_knorm):
    B, S, D = x.shape
    offs = [0]
    for w in (GLA_QK_W, GLA_QK_W, GLA_V_W, GLA_V_W, GLA_GATE_RANK, DIFF_QK_W, DIFF_QK_W, DIFF_V_W):
        offs.append(offs[-1] + w)
    assert offs[4] == offs[8] - offs[5]
    w_all = w_in.astype(bf16)
    w_kt = w_in[:, offs[1]:offs[2]].T.astype(bf16)
    w_lo = jnp.zeros((D, LANES), f32).at[:, :GLA_GATE_RANK].set(w_in[:, offs[4]:offs[5]]).astype(bf16)
    w_up = jnp.zeros((LANES, GLA_QK_W), f32).at[:GLA_GATE_RANK].set(w_gk_up).astype(bf16)
    w_upt = w_up.T
    b_up = b_gk_up.reshape(1, GLA_QK_W)
    b_upt = b_gk_up.reshape(GLA_QK_W, 1)
    n_grp = 2 * DIFF_QK_W // DIFF_DQK
    gqk = jnp.concatenate([jnp.tile(g_qnorm, n_grp // 2) * (DIFF_DQK ** -0.5 * LOG2E),
                           jnp.tile(g_knorm, n_grp // 2)]).reshape(1, 2 * DIFF_QK_W)
    grp = (jnp.arange(2 * DIFF_QK_W)[:, None] // DIFF_DQK == jnp.arange(LANES)[None, :]).astype(bf16)
    grpt = grp.T
    nw = 2 * offs[4]
    tm = TM_IN
    const = lambda shape: pl.BlockSpec(shape, lambda b, i: (0,) * len(shape))
    row = lambda w: pl.BlockSpec((1, tm, w), lambda b, i: (b, i, 0))
    colT = pl.BlockSpec((1, tm // PAIR, GLA_QK_W, PAIR), lambda b, i: (b, i, 0, 0))
    outs = pl.pallas_call(
        _inproj_kernel,
        out_shape=[jax.ShapeDtypeStruct((B, S, GLA_QK_W), f32),
                   jax.ShapeDtypeStruct((B, S, GLA_QK_W), f32),
                   jax.ShapeDtypeStruct((B, S, GLA_QK_W), f32),
                   jax.ShapeDtypeStruct((B, S // PAIR, GLA_QK_W, PAIR), f32),
                   jax.ShapeDtypeStruct((B, S // PAIR, GLA_QK_W, PAIR), f32),
                   jax.ShapeDtypeStruct((B, S, GLA_V_W), bf16),
                   jax.ShapeDtypeStruct((B, S, GLA_V_W), f32),
                   jax.ShapeDtypeStruct((B, S, DIFF_QK_W), bf16),
                   jax.ShapeDtypeStruct((B, S, DIFF_QK_W), bf16),
                   jax.ShapeDtypeStruct((B, S, DIFF_V_W), bf16)],
        grid=(B, S // tm),
        in_specs=[row(D),
                  pl.BlockSpec((1, 6, D), lambda b, i: (b, 0, 0)),
                  const((1, D)), const((D, w_in.shape[1])), const((GLA_QK_W, D)), const((D, LANES)),
                  const((LANES, GLA_QK_W)), const((GLA_QK_W, LANES)),
                  const((1, GLA_QK_W)), const((GLA_QK_W, 1)),
                  const((1, 2 * DIFF_QK_W)), const((2 * DIFF_QK_W, LANES)),
                  const((LANES, 2 * DIFF_QK_W))],
        out_specs=[row(GLA_QK_W), row(GLA_QK_W), row(GLA_QK_W), colT, colT,
                   row(GLA_V_W), row(GLA_V_W), row(DIFF_QK_W), row(DIFF_QK_W), row(DIFF_V_W)],
        scratch_shapes=[pltpu.VMEM((D, nw), bf16)],
        compiler_params=pltpu.CompilerParams(
            dimension_semantics=("arbitrary", "arbitrary"), vmem_limit_bytes=VMEM_LIMIT),
        name="inproj",
    )(x, mod, g_norm1.reshape(1, D), w_all, w_kt, w_lo, w_up, w_upt, b_up, b_upt, gqk, grp, grpt)
    return outs


def _gla_kernel(q_ref, k_ref, g_ref, kt_ref, gt_ref, v_ref, r_ref, gout_ref, tri_ref, trit_ref,
                o_ref, s_ref, *, n_pairs):
    H, DK, DV = N_GLA_HEADS, GLA_DK, GLA_DV

    @pl.when(pl.program_id(1) == 0)
    def _():
        s_ref[...] = jnp.zeros_like(s_ref)

    tri = tri_ref[...]
    trit = trit_ref[...]
    tri_b = tri > 0
    lane_head = lax.broadcasted_iota(jnp.int32, (1, H * DK), 1) // DK
    row_head = lax.broadcasted_iota(jnp.int32, (H * PAIR, 1), 0) // PAIR
    qmask = row_head == lane_head
    row_first = lax.broadcasted_iota(jnp.int32, (PAIR, 1), 0) < GLA_CHUNK
    row_first4 = (lax.broadcasted_iota(jnp.int32, (H * PAIR, 1), 0) % PAIR) < GLA_CHUNK
    lane_first = lax.broadcasted_iota(jnp.int32, (1, PAIR), 1) < GLA_CHUNK
    scale = DK ** -0.5
    gout = gout_ref[...]

    def pair(p, carry):
        r0 = pl.multiple_of(p * PAIR, PAIR)
        q = q_ref[0, pl.ds(r0, PAIR), :]
        k = k_ref[0, pl.ds(r0, PAIR), :]
        g = g_ref[0, pl.ds(r0, PAIR), :]
        kt = kt_ref[0, p]
        gt = gt_ref[0, p]
        v = v_ref[0, pl.ds(r0, PAIR), :]

        g_hi, g_lo = _split(g)
        gc = _mm(tri, g_hi) + _mm(tri, g_lo)
        gt_hi, gt_lo = _split(gt)
        gct = _mm(gt_hi, trit) + _mm(gt_lo, trit)
        g_last = jnp.where(row_first, gc[GLA_CHUNK - 1:GLA_CHUNK, :], gc[PAIR - 1:PAIR, :])
        gl0 = gct[:, GLA_CHUNK - 1:GLA_CHUNK]
        gl1 = gct[:, PAIR - 1:PAIR]
        g_last_t = jnp.where(lane_first, gl0, gl1)

        q_e = (q * (jnp.exp(gc) * scale)).astype(bf16)
        k_e = (k * jnp.exp(-gc)).astype(bf16)
        ks_t = kt * jnp.exp(g_last_t - gct)
        ks_t0 = jnp.where(lane_first, ks_t, 0.0).astype(bf16)
        ks_t1 = jnp.where(lane_first, 0.0, ks_t).astype(bf16)
        del g_last

        qm = jnp.where(qmask, jnp.concatenate([q_e] * H, axis=0), jnp.zeros((), bf16))
        a = _nt(qm, k_e)
        s0 = s_ref[...]

        u0 = []
        u1 = []
        for h in range(H):
            v_h = v[:, h * DV:(h + 1) * DV]
            u0.append(_mm(ks_t0[h * DK:(h + 1) * DK], v_h))
            u1.append(_mm(ks_t1[h * DK:(h + 1) * DK], v_h))
        u0 = jnp.concatenate(u0, axis=0)
        u1 = jnp.concatenate(u1, axis=0)
        s1 = s0 * jnp.exp(gl0) + u0
        s_ref[...] = s1 * jnp.exp(gl1) + u1

        o_inter = jnp.where(row_first4, _mm(qm, s0.astype(bf16)), _mm(qm, s1.astype(bf16)))
        for h in range(H):
            a_h = jnp.where(tri_b, a[h * PAIR:(h + 1) * PAIR], 0.0).astype(bf16)
            o_h = _mm(a_h, v[:, h * DV:(h + 1) * DV]) + o_inter[h * PAIR:(h + 1) * PAIR]
            ms = jnp.mean(o_h * o_h, axis=-1, keepdims=True)
            o_n = o_h * lax.rsqrt(ms + EPS) * gout
            r_h = r_ref[0, pl.ds(r0, PAIR), h * DV:(h + 1) * DV]
            o_ref[0, pl.ds(r0, PAIR), h * DV:(h + 1) * DV] = (o_n * _silu(r_h)).astype(bf16)
        return carry

    lax.fori_loop(0, n_pairs, pair, 0, unroll=GLA_UNROLL)


def _gla(qg, kg, gk, kgt, gkt, vg, rg, g_gla_out):
    B, S, _ = qg.shape
    tg = min(TG_GLA, S)
    r = jnp.arange(PAIR)
    tri = ((r[:, None] // GLA_CHUNK == r[None, :] // GLA_CHUNK) & (r[None, :] <= r[:, None])).astype(bf16)
    row = lambda w: pl.BlockSpec((1, tg, w), lambda b, i: (b, i, 0))
    colT = pl.BlockSpec((1, tg // PAIR, GLA_QK_W, PAIR), lambda b, i: (b, i, 0, 0))
    const = lambda shape: pl.BlockSpec(shape, lambda b, i: (0,) * len(shape))
    return pl.pallas_call(
        functools.partial(_gla_kernel, n_pairs=tg // PAIR),
        out_shape=jax.ShapeDtypeStruct((B, S, GLA_V_W), bf16),
        grid=(B, S // tg),
        in_specs=[row(GLA_QK_W), row(GLA_QK_W), row(GLA_QK_W), colT, colT,
                  row(GLA_V_W), row(GLA_V_W), const((1, GLA_DV)),
                  const((PAIR, PAIR)), const((PAIR, PAIR))],
        out_specs=row(GLA_V_W),
        scratch_shapes=[pltpu.VMEM((GLA_QK_W, GLA_DV), f32)],
        compiler_params=pltpu.CompilerParams(
            dimension_semantics=("arbitrary", "arbitrary"), vmem_limit_bytes=VMEM_LIMIT),
        name="gla",
    )(qg, kg, gk, kgt, gkt, vg, rg, g_gla_out.reshape(1, GLA_DV), tri, tri.T)


def _attn_finish(o, gsub_ref, o_ref, lambda_init):
    ms = jnp.mean(o * o, axis=-1, keepdims=True)
    o_ref[0] = (o * lax.rsqrt(ms + EPS) * gsub_ref[...] * (1.0 - lambda_init)).astype(bf16)


def _attn_lambda(lamv_ref, lambda_init):
    lv = lamv_ref[...]
    return (jnp.exp(jnp.sum(lv[0:1] * lv[1:2], axis=-1, keepdims=True))
            - jnp.exp(jnp.sum(lv[2:3] * lv[3:4], axis=-1, keepdims=True)) + lambda_init)


def _attn_bounded_kernel(q_ref, k_ref, v_ref, bias_ref, lamv_ref, gsub_ref, o_ref, vaug_ref, *, lambda_init):
    qi = pl.program_id(2)
    tq = q_ref.shape[1]
    S = k_ref.shape[1]
    n_head = q_ref.shape[2] // (2 * DIFF_DQK)

    @pl.when(qi == 0)
    def _():
        lane = lax.broadcasted_iota(jnp.int32, (S, DIFF_DV), 1)
        for h in range(n_head):
            vaug_ref[h, :, :DIFF_DV] = v_ref[0, :, h * DIFF_DV:(h + 1) * DIFF_DV]
            vaug_ref[h, :, DIFF_DV:] = jnp.where(lane == 0, 1.0, 0.0).astype(bf16)

    lane = lax.broadcasted_iota(jnp.int32, (1, 2 * DIFF_DQK), 1)
    zero = jnp.zeros((), bf16)
    qs = []
    for h in range(n_head):
        q = q_ref[0, :, h * 2 * DIFF_DQK:(h + 1) * 2 * DIFF_DQK]
        qs.append((jnp.where(lane < DIFF_DQK, q, zero), jnp.where(lane < DIFF_DQK, zero, q)))

    def update(accs, k0, bias):
        out = []
        for h in range(n_head):
            kb = k_ref[0, pl.ds(k0, tq), h * 2 * DIFF_DQK:(h + 1) * 2 * DIFF_DQK]
            vb = vaug_ref[h, pl.ds(k0, tq), :]
            for c in range(2):
                s = _nt(qs[h][c], kb)
                if bias is not None:
                    s = s + bias[h][c]
                out.append(accs[2 * h + c] + _mm(jnp.exp2(s).astype(bf16), vb))
        return tuple(out)

    def far(kj, accs):
        return update(accs, pl.multiple_of(kj * tq, tq), None)

    def far_group(g, accs):
        for u in range(ATTN_UNROLL):
            accs = far(g * ATTN_UNROLL + u, accs)
        return accs

    def block_or_masked(accs, kj, rel):
        exists = kj >= 0
        k0 = pl.multiple_of(jnp.maximum(kj, 0) * tq, tq)
        if rel is None:
            tiles = [(jnp.where(exists, 0.0, NEG),) * 2] * n_head
        else:
            tiles = [tuple(jnp.where(exists, bias_ref[h, c, rel], NEG) for c in range(2)) for h in range(n_head)]
        return update(accs, k0, tiles)

    accs = tuple(jnp.zeros((tq, 2 * DIFF_DV), f32) for _ in range(2 * n_head))
    accs = block_or_masked(accs, qi, 1)
    accs = block_or_masked(accs, qi - 1, 0)
    for u in range(2, ATTN_UNROLL):
        accs = block_or_masked(accs, qi - u, None)
    n_far = jnp.maximum(qi + 1 - ATTN_UNROLL, 0)
    n_grp = n_far // ATTN_UNROLL
    accs = lax.fori_loop(0, n_grp, far_group, accs)
    accs = lax.fori_loop(n_grp * ATTN_UNROLL, n_far, far, accs)
    lam = _attn_lambda(lamv_ref, lambda_init)
    for h in range(n_head):
        a0, a1 = accs[2 * h], accs[2 * h + 1]
        o = a0[:, :DIFF_DV] / a0[:, DIFF_DV:DIFF_DV + 1] - lam * (a1[:, :DIFF_DV] / a1[:, DIFF_DV:DIFF_DV + 1])
        ms = jnp.mean(o * o, axis=-1, keepdims=True)
        o_ref[0, :, h * DIFF_DV:(h + 1) * DIFF_DV] = (
            o * lax.rsqrt(ms + EPS) * gsub_ref[...] * (1.0 - lambda_init)).astype(bf16)


def _attn_kernel(q_ref, k_ref, v_ref, bias_ref, lamv_ref, gsub_ref, o_ref, *, lambda_init):
    qi = pl.program_id(2)
    tq = q_ref.shape[1]
    q = q_ref[0]
    lane = lax.broadcasted_iota(jnp.int32, (1, 2 * DIFF_DQK), 1)
    zero = jnp.zeros((), bf16)
    qs = (jnp.where(lane < DIFF_DQK, q, zero), jnp.where(lane < DIFF_DQK, zero, q))

    def update(state, kb, vb, bias):
        new = []
        for c in range(2):
            m, l, acc = state[c]
            s = _nt(qs[c], kb)
            if bias is not None:
                s = s + bias[c]
            m_new = jnp.maximum(m, jnp.max(s, axis=-1, keepdims=True))
            alpha = jnp.exp2(m - m_new)
            p = jnp.exp2(s - m_new)
            l = alpha * l + jnp.sum(p, axis=-1, keepdims=True)
            acc = alpha * acc + _mm(p.astype(bf16), vb)
            new.append((m_new, l, acc))
        return tuple(new)

    init = tuple((jnp.full((tq, 1), NEG, f32), jnp.zeros((tq, 1), f32), jnp.zeros((tq, DIFF_DV), f32))
                 for _ in range(2))

    def far(kj, state):
        k0 = pl.multiple_of(kj * tq, tq)
        return update(state, k_ref[0, pl.ds(k0, tq), :], v_ref[0, pl.ds(k0, tq), :], None)

    state = lax.fori_loop(0, jnp.maximum(qi - 1, 0), far, init)

    kd0 = pl.multiple_of(qi * tq, tq)
    state = update(state, k_ref[0, pl.ds(kd0, tq), :], v_ref[0, pl.ds(kd0, tq), :],
                   (bias_ref[0, 0, 1], bias_ref[0, 1, 1]))
    kp0 = pl.multiple_of(jnp.maximum(qi - 1, 0) * tq, tq)
    has_prev = qi > 0
    state = update(state, k_ref[0, pl.ds(kp0, tq), :], v_ref[0, pl.ds(kp0, tq), :],
                   (jnp.where(has_prev, bias_ref[0, 0, 0], NEG), jnp.where(has_prev, bias_ref[0, 1, 0], NEG)))

    (_, l0, a0), (_, l1, a1) = state
    o = a0 / l0 - _attn_lambda(lamv_ref, lambda_init) * (a1 / l1)
    _attn_finish(o, gsub_ref, o_ref, lambda_init)


def _t5_bucket(n):
    max_exact = NUM_BUCKETS // 2
    nf = jnp.maximum(n, 1).astype(f32)
    large = max_exact + (jnp.log(nf / max_exact) / math.log(MAX_DISTANCE / max_exact)
                         * (NUM_BUCKETS - max_exact)).astype(jnp.int32)
    large = jnp.minimum(large, NUM_BUCKETS - 1)
    return jnp.where(n < max_exact, n, large)


def _toeplitz_kernel(w_ref, o_ref):
    n = o_ref.shape[-1]
    for t in range(2):
        rows = jnp.broadcast_to(w_ref[0, t:t + 1, :], (n, 2 * n))
        o_ref[0, 0, t] = pltpu.roll(rows, 0, 1, stride=1, stride_axis=0)[:, n:]


def _bias_tiles(rel_bias_table, S, n):
    HM = rel_bias_table.shape[1]
    assert n >= MAX_DISTANCE
    d = jnp.arange(2 * n, dtype=jnp.int32)
    in_bucket = _t5_bucket(d)[None, :] == jnp.arange(NUM_BUCKETS, dtype=jnp.int32)[:, None]
    by_dist = jnp.sum(jnp.where(in_bucket[:, None, :], rel_bias_table.astype(f32)[:, :, None], 0.0), axis=0)
    rel = (by_dist - rel_bias_table[NUM_BUCKETS - 1].astype(f32)[:, None]) * LOG2E
    w_diag = jnp.concatenate([rel[:, n::-1], jnp.full((HM, n - 1), NEG, f32)], axis=1)
    w_prev = jnp.concatenate([rel[:, :1], rel[:, :0:-1]], axis=1)
    w = jnp.stack([w_prev, w_diag], axis=1)
    return pl.pallas_call(
        _toeplitz_kernel,
        out_shape=jax.ShapeDtypeStruct((HM // 2, 2, 2, n, n), f32),
        grid=(HM // 2, 2),
        in_specs=[pl.BlockSpec((1, 2, 2 * n), lambda h, m: (h * 2 + m, 0, 0))],
        out_specs=pl.BlockSpec((1, 1, 2, n, n), lambda h, m: (h, m, 0, 0, 0)),
        compiler_params=pltpu.CompilerParams(vmem_limit_bytes=VMEM_LIMIT),
        name="bias_tiles",
    )(w)


def _attn(qd, kd, vd, bias_tiles, lamv, g_subln, lambda_init, bounded):
    B, S, _ = qd.shape
    H = N_DIFF_HEADS
    tq = min(TQ, S)
    body = _attn_bounded_kernel if bounded else _attn_kernel
    hs = ATTN_HEADS if bounded else 1
    scratch = [pltpu.VMEM((hs, S, 2 * DIFF_DV), bf16)] if bounded else []
    return pl.pallas_call(
        functools.partial(body, lambda_init=lambda_init),
        out_shape=jax.ShapeDtypeStruct((B, S, DIFF_V_W), bf16),
        scratch_shapes=scratch,
        grid=(B, H // hs, S // tq),
        in_specs=[pl.BlockSpec((1, tq, hs * 2 * DIFF_DQK), lambda b, h, i: (b, i, h)),
                  pl.BlockSpec((1, S, hs * 2 * DIFF_DQK), lambda b, h, i: (b, 0, h)),
                  pl.BlockSpec((1, S, hs * DIFF_DV), lambda b, h, i: (b, 0, h)),
                  pl.BlockSpec((hs, 2, 2, tq, tq), lambda b, h, i: (h, 0, 0, 0, 0)),
                  pl.BlockSpec((4, DIFF_DQK), lambda b, h, i: (0, 0)),
                  pl.BlockSpec((1, DIFF_DV), lambda b, h, i: (0, 0))],
        out_specs=pl.BlockSpec((1, tq, hs * DIFF_DV), lambda b, h, i: (b, i, h)),
        compiler_params=pltpu.CompilerParams(
            dimension_semantics=("arbitrary", "arbitrary", "arbitrary"),
            vmem_limit_bytes=VMEM_LIMIT_ATTN if bounded else VMEM_LIMIT),
        name="attn_bounded" if bounded else "attn_online",
    )(qd, kd, vd, bias_tiles, lamv, g_subln.reshape(1, DIFF_DV))


def _scores_bounded(rel_bias_table, g_qnorm, g_knorm):
    qk = DIFF_DQK ** 0.5 * jnp.max(jnp.abs(g_qnorm)) * jnp.max(jnp.abs(g_knorm)) * NORM_SLACK
    rel = jnp.max(jnp.abs(rel_bias_table - rel_bias_table[NUM_BUCKETS - 1:]))
    return qk + rel <= SAFE_SCORE


def _rows_to_tiles(x, ref):
    n = x.shape[0]
    for c in range(ROW_TILE):
        ref[pl.ds(c, n, stride=ROW_TILE), :] = x[:, c * LANES:(c + 1) * LANES]


def _tiles_to_rows(ref, n):
    return jnp.concatenate([ref[pl.ds(c, n, stride=ROW_TILE), :] for c in range(ROW_TILE)], axis=1)


def _outproj_kernel(og_ref, od_ref, x_ref, mod_ref, wo_ref, g2_ref, wr_ref, br_ref,
                    x1_ref, hp_ref, lg_ref):
    half = og_ref.shape[2]
    sub = og_ref.shape[1] // INPROJ_SUB
    for t in range(INPROJ_SUB):
        rows = slice(t * sub, (t + 1) * sub)
        mix = _mm(og_ref[0, rows, :], wo_ref[:half, :]) + _mm(od_ref[0, rows, :], wo_ref[half:, :])
        x1 = x_ref[0, rows, :] + mod_ref[0, 2:3, :] * mix
        x1_ref[0, rows, :] = x1
        ms = jnp.mean(x1 * x1, axis=-1, keepdims=True)
        y = x1 * lax.rsqrt(ms + EPS) * g2_ref[...]
        h = (y * (1.0 + mod_ref[0, 4:5, :]) + mod_ref[0, 3:4, :]).astype(bf16)
        lg_ref[rows, :] = _mm(h, wr_ref[...]) + br_ref[...]
        _rows_to_tiles(h.astype(f32), hp_ref.at[pl.ds(t * sub * ROW_TILE, sub * ROW_TILE)])


def _outproj(og, od, x, mod, w_out, g_norm2, w_router, b_router):
    B, S, D = x.shape
    assert D == ROW_TILE * LANES, "the token-tile layout needs a model row to fill one (8,128) tile"
    E = w_router.shape[1]
    tm = TM_IN
    nj = S // tm
    w_r = jnp.zeros((D, LANES), f32).at[:, :E].set(w_router).astype(bf16)
    b_r = jnp.full((1, LANES), NEG, f32).at[0, :E].set(b_router)
    const = lambda shape: pl.BlockSpec(shape, lambda b, i: (0,) * len(shape))
    return pl.pallas_call(
        _outproj_kernel,
        out_shape=[jax.ShapeDtypeStruct((B, S, D), f32),
                   jax.ShapeDtypeStruct((B * S * ROW_TILE, LANES), f32),
                   jax.ShapeDtypeStruct((B * S, LANES), f32)],
        grid=(B, nj),
        in_specs=[pl.BlockSpec((1, tm, og.shape[2]), lambda b, i: (b, i, 0)),
                  pl.BlockSpec((1, tm, od.shape[2]), lambda b, i: (b, i, 0)),
                  pl.BlockSpec((1, tm, D), lambda b, i: (b, i, 0)),
                  pl.BlockSpec((1, 6, D), lambda b, i: (b, 0, 0)),
                  const((w_out.shape[0], D)), const((1, D)), const((D, LANES)), const((1, LANES))],
        out_specs=[pl.BlockSpec((1, tm, D), lambda b, i: (b, i, 0)),
                   pl.BlockSpec((tm * ROW_TILE, LANES), lambda b, i: (b * nj + i, 0)),
                   pl.BlockSpec((tm, LANES), lambda b, i: (b * nj + i, 0))],
        compiler_params=pltpu.CompilerParams(
            dimension_semantics=("arbitrary", "arbitrary"), vmem_limit_bytes=VMEM_LIMIT),
        name="outproj",
    )(og, od, x, mod, w_out.astype(bf16), g_norm2.reshape(1, D), w_r, b_r)


def _route_kernel(lg_ref, lt_ref, ut_ref, pk_ref, lp_ref, rw_ref, cnt_ref, snap_ref, run_ref):
    @pl.when(pl.program_id(0) == 0)
    def _():
        run_ref[...] = jnp.zeros_like(run_ref)

    x = lg_ref[...]
    tr = x.shape[0]
    lane = lax.broadcasted_iota(jnp.int32, (tr, LANES), 1)
    lane_f = lane.astype(f32)
    vals, hots = [], []
    for _ in range(TOP_K):
        m = jnp.max(x, axis=-1, keepdims=True)
        idx = jnp.min(jnp.where(x == m, lane_f, float(LANES)), axis=-1, keepdims=True)
        hot = lane_f == idx
        x = jnp.where(hot, -jnp.inf, x)
        vals.append(m)
        hots.append(hot)
    ex = [jnp.exp(v - vals[0]) for v in vals]
    den = ex[0] + ex[1] + ex[2] + ex[3]
    sel = (hots[0] | hots[1] | hots[2] | hots[3]).astype(f32)
    rank = _mm(lt_ref[...], sel.astype(bf16)) + run_ref[...]
    run_ref[...] = run_ref[...] + jnp.sum(sel, axis=0, keepdims=True)
    cnt_ref[...] = run_ref[...]
    base = rank[0:1, :]
    snap_ref[0] = base
    n_hi, n_lo = _split(jnp.sum(sel, axis=0, keepdims=True))
    start = _mm(n_hi, ut_ref[...]) + _mm(n_lo, ut_ref[...])
    place = rank - base + start
    pos = jnp.zeros((tr, LANES), f32)
    rw = jnp.zeros((tr, LANES), f32)
    for k in range(TOP_K):
        rk = jnp.sum(jnp.where(hots[k], place, 0.0), axis=-1, keepdims=True) * ROW_TILE
        pos = jnp.where(lane == k, rk, pos)
        rw = jnp.where(lane == k, ex[k] / den, rw)
    rw_ref[...] = rw
    p_hi, p_lo = _split(pos)
    lp_ref[0] = (_nt(pk_ref[...], p_hi) + _nt(pk_ref[...], p_lo)).astype(jnp.int32)


def _route(logits):
    T = logits.shape[0]
    tr = min(TR, T)
    assert tr == TD, "routing and dispatch share one token tile"
    r = jnp.arange(tr)
    lt = (r[None, :] < r[:, None]).astype(bf16)
    e = jnp.arange(LANES)
    ut = (e[:, None] < e[None, :]).astype(bf16)
    pick = (jnp.arange(ROW_TILE)[:, None] == e[None, :]).astype(bf16)
    return pl.pallas_call(
        _route_kernel,
        out_shape=[jax.ShapeDtypeStruct((T // tr, ROW_TILE, tr), jnp.int32),
                   jax.ShapeDtypeStruct((T, LANES), f32),
                   jax.ShapeDtypeStruct((1, LANES), f32),
                   jax.ShapeDtypeStruct((T // tr, 1, LANES), f32)],
        grid=(T // tr,),
        in_specs=[pl.BlockSpec((tr, LANES), lambda i: (i, 0)),
                  pl.BlockSpec((tr, tr), lambda i: (0, 0)),
                  pl.BlockSpec((LANES, LANES), lambda i: (0, 0)),
                  pl.BlockSpec((ROW_TILE, LANES), lambda i: (0, 0))],
        out_specs=[pl.BlockSpec((1, ROW_TILE, tr), lambda i: (i, 0, 0)),
                   pl.BlockSpec((tr, LANES), lambda i: (i, 0)),
                   pl.BlockSpec((1, LANES), lambda i: (0, 0)),
                   pl.BlockSpec((1, 1, LANES), lambda i: (i, 0, 0))],
        scratch_shapes=[pltpu.VMEM((1, LANES), f32)],
        compiler_params=pltpu.CompilerParams(dimension_semantics=("arbitrary",)),
        name="route",
    )(logits, lt, ut, pick)


def _run_copies(list_ref, hbm_ref, stage_ref, sem, to_hbm):
    for c, size in enumerate(RUN_SIZES):
        def one(i, carry, c=c, size=size):
            s0 = list_ref[c * N_RUN + i]
            d0 = list_ref[LIST_DST + c * N_RUN + i]
            stage = stage_ref.at[pl.ds(pl.multiple_of(s0, ROW_TILE), size * ROW_TILE)]
            rows = hbm_ref.at[pl.ds(pl.multiple_of(d0, ROW_TILE), size * ROW_TILE)]
            src, dst = (stage, rows) if to_hbm else (rows, stage)
            pltpu.make_async_copy(src, dst, sem).start(priority=c % 2)
            return carry
        lax.fori_loop(0, list_ref[LIST_CNT + c], one, 0)


def _copy_lists(run_dst, run_n, run_off):
    n_tiles, n_exp = run_n.shape
    n_size = len(RUN_SIZES)
    size = jnp.asarray(RUN_SIZES, jnp.int32)[None, None, :]
    n = jnp.broadcast_to(run_n.T[:, :, None], (n_exp, n_tiles, n_size))
    bit = ((n & size) != 0).reshape(n_exp, -1)
    before = (n & ~(2 * size - 1)).reshape(n_exp, -1)
    place = jnp.cumsum(bit, axis=0) - 1
    pick = bit[None, :, :] & (place[None, :, :] == jnp.arange(N_RUN)[:, None, None])

    def compact(v):
        v = jnp.broadcast_to(v.T[:, :, None], (n_exp, n_tiles, n_size)).reshape(n_exp, -1) + before
        out = jnp.sum(jnp.where(pick, v[None, :, :], 0), axis=1) * ROW_TILE
        return out.reshape(N_RUN, n_tiles, n_size).transpose(1, 2, 0).reshape(n_tiles, -1)
    cnt = jnp.sum(bit, axis=0).reshape(n_tiles, n_size).astype(jnp.int32)
    pad = jnp.zeros((n_tiles, LIST_LEN - LIST_CNT - n_size), jnp.int32)
    lists = jnp.concatenate([compact(run_off), compact(run_dst), cnt, pad], axis=1)
    return lists.reshape(-1).astype(jnp.int32)


def _dispatch_kernel(pend_ref, cnt_ref, nu_ref, lpos_ref, list_ref, h_ref, xs_ref,
                     zero_ref, stage_ref, sem, zsem):
    n_tok = h_ref.shape[0] // ROW_TILE
    blk_rows = FFN_BLK * ROW_TILE

    @pl.when(pl.program_id(0) == 0)
    def _():
        zero_ref[...] = jnp.zeros_like(zero_ref)
        n_exp = pend_ref.shape[0]

        def last_block(e):
            return xs_ref.at[pl.ds(pl.multiple_of((pend_ref[e] - FFN_BLK) * ROW_TILE, blk_rows), blk_rows)]

        def zfill(e, c):
            @pl.when(cnt_ref[e] > 0)
            def _():
                pltpu.make_async_copy(zero_ref, last_block(e), zsem).start()
            return c

        def zwait(e, c):
            @pl.when(cnt_ref[e] > 0)
            def _():
                pltpu.make_async_copy(zero_ref, last_block(e), zsem).wait()
            return c

        lax.fori_loop(0, n_exp, zfill, 0)
        lax.fori_loop(0, n_exp, zwait, 0)

        def tail_block(i):
            return xs_ref.at[pl.ds(pl.multiple_of(i * blk_rows, blk_rows), blk_rows)]

        def tfill(i, c):
            pltpu.make_async_copy(zero_ref, tail_block(i), zsem).start()
            return c

        def twait(i, c):
            pltpu.make_async_copy(zero_ref, tail_block(i), zsem).wait()
            return c

        n_blk = xs_ref.shape[0] // blk_rows
        lax.fori_loop(nu_ref[0], n_blk, tfill, 0)
        lax.fori_loop(nu_ref[0], n_blk, twait, 0)

    step = pl.program_id(0)
    slot = step % 2
    stage = stage_ref.at[slot]

    def place(g, c):
        for u in range(DMA_UNROLL):
            r = g * DMA_UNROLL + u
            row = h_ref[pl.ds(pl.multiple_of(r * ROW_TILE, ROW_TILE), ROW_TILE), :]
            for k in range(TOP_K):
                p = lpos_ref[k * n_tok + r]
                stage[pl.ds(pl.multiple_of(p, ROW_TILE), ROW_TILE), :] = row
        return c
    lax.fori_loop(0, n_tok // DMA_UNROLL, place, 0)

    _run_copies(list_ref, xs_ref, stage, sem.at[slot], to_hbm=True)

    def drain(s):
        pltpu.make_async_copy(stage_ref.at[s], stage_ref.at[s], sem.at[s]).wait()

    @pl.when(step > 0)
    def _():
        drain(1 - slot)

    @pl.when(step == pl.num_programs(0) - 1)
    def _():
        drain(slot)


def _dispatch(p_ends, counts, n_used, lpos_flat, lists, hp, n_rows):
    T = hp.shape[0] // ROW_TILE
    grid_spec = pltpu.PrefetchScalarGridSpec(
        num_scalar_prefetch=3,
        grid=(T // TD,),
        in_specs=[pl.BlockSpec((TD * TOP_K,), lambda i, *_: (i,), memory_space=pltpu.SMEM),
                  pl.BlockSpec((LIST_LEN,), lambda i, *_: (i,), memory_space=pltpu.SMEM),
                  pl.BlockSpec((TD * ROW_TILE, LANES), lambda i, *_: (i, 0))],
        out_specs=pl.BlockSpec(memory_space=pl.ANY),
        scratch_shapes=[pltpu.VMEM((FFN_BLK * ROW_TILE, LANES), f32),
                        pltpu.VMEM((2, TD * TOP_K * ROW_TILE, LANES), f32),
                        pltpu.SemaphoreType.DMA((2,)), pltpu.SemaphoreType.DMA(())],
    )
    return pl.pallas_call(
        _dispatch_kernel,
        out_shape=jax.ShapeDtypeStruct((n_rows * ROW_TILE, LANES), f32),
        grid_spec=grid_spec,
        compiler_params=pltpu.CompilerParams(dimension_semantics=("arbitrary",), vmem_limit_bytes=VMEM_LIMIT),
        name="dispatch",
    )(p_ends, counts, n_used, lpos_flat, lists, hp)


def _ffn_kernel(be_ref, nu_ref, nx_ref, par_ref, val_ref, xs_ref, wgu_hbm, bgu_ref, wd_hbm, bd_ref, ys_ref,
                wgu32_ref, wd32_ref, wgu_ref, wd_ref, sem):
    i = pl.program_id(0)
    used = i < nu_ref[0]
    new_expert = (i == 0) | (be_ref[i] != be_ref[jnp.maximum(i - 1, 0)])
    slot = par_ref[i]

    def weight_copies(e, s):
        return (pltpu.make_async_copy(wgu_hbm.at[e], wgu32_ref.at[s], sem.at[0, s]),
                pltpu.make_async_copy(wd_hbm.at[e], wd32_ref.at[s], sem.at[1, s]))

    @pl.when(i == 0)
    def _():
        for cp in weight_copies(be_ref[0], 0):
            cp.start()

    @pl.when(used & new_expert)
    def _():
        for cp in weight_copies(be_ref[i], slot):
            cp.wait()

        @pl.when(nx_ref[i] >= 0)
        def _():
            for cp in weight_copies(nx_ref[i], 1 - slot):
                cp.start()

        rows = 128

        def cast(src, dst):
            def body(r, c):
                r0 = pl.multiple_of(r * rows, rows)
                dst[pl.ds(r0, rows), :] = src[slot, pl.ds(r0, rows), :].astype(bf16)
                return c
            lax.fori_loop(0, src.shape[1] // rows, body, 0)
        cast(wgu32_ref, wgu_ref)
        cast(wd32_ref, wd_ref)

    def ffn_rows(n):
        tiles = pl.ds(0, n * ROW_TILE)
        F = wd_ref.shape[0]
        xrow = _tiles_to_rows(xs_ref.at[tiles], n).astype(bf16)
        acc = None
        fc = F // 2
        for c in range(2):
            def gu(col0):
                return _mm(xrow, wgu_ref[:, col0:col0 + fc]) + bgu_ref[0, :, col0:col0 + fc]
            gate = jnp.minimum(gu(c * fc), SWIGLU_LIMIT)
            up = jnp.clip(gu(F + c * fc), -SWIGLU_LIMIT, SWIGLU_LIMIT)
            y = (up + 1.0) * (gate * jax.nn.sigmoid(SWIGLU_ALPHA * gate))
            part = _mm(y.astype(bf16), wd_ref[c * fc:(c + 1) * fc, :])
            acc = part if acc is None else acc + part
        _rows_to_tiles(acc + bd_ref[0], ys_ref.at[tiles])

    half = FFN_BLK // 2
    occupied = val_ref[i]

    @pl.when(used & (occupied > half))
    def _():
        ffn_rows(FFN_BLK)

    @pl.when(used & (occupied <= half))
    def _():
        ffn_rows(half)
        ys_ref[pl.ds(half * ROW_TILE, half * ROW_TILE), :] = jnp.zeros((half * ROW_TILE, LANES), f32)

    @pl.when(jnp.logical_not(used))
    def _():
        ys_ref[...] = jnp.zeros_like(ys_ref)


def _ffn(block_e, n_used, occupied, xs, w_gate_up, b_gate_up, w_down, b_down):
    E, D, F2 = w_gate_up.shape
    F = F2 // 2
    P = xs.shape[0] // ROW_TILE
    nb = P // FFN_BLK
    rows = FFN_BLK * ROW_TILE

    idx = jnp.arange(nb, dtype=jnp.int32)
    live = idx < n_used[0]
    later_other = (block_e[None, :] != block_e[:, None]) & (idx[None, :] > idx[:, None]) & live[None, :]
    nxt = jnp.where(jnp.any(later_other, axis=1), block_e[jnp.argmax(later_other, axis=1)], -1).astype(jnp.int32)
    starts = jnp.concatenate([jnp.ones((1,), jnp.int32), (block_e[1:] != block_e[:-1]).astype(jnp.int32)])
    parity = ((jnp.cumsum(starts) - 1) % 2).astype(jnp.int32)

    def blk(i, nu):
        return jnp.minimum(i, nu[0] - 1)

    grid_spec = pltpu.PrefetchScalarGridSpec(
        num_scalar_prefetch=5,
        grid=(nb,),
        in_specs=[pl.BlockSpec((rows, LANES), lambda i, be, nu, *_: (blk(i, nu), 0)),
                  pl.BlockSpec(memory_space=pl.ANY),
                  pl.BlockSpec((1, 1, F2), lambda i, be, nu, *_: (be[blk(i, nu)], 0, 0)),
                  pl.BlockSpec(memory_space=pl.ANY),
                  pl.BlockSpec((1, 1, D), lambda i, be, nu, *_: (be[blk(i, nu)], 0, 0))],
        out_specs=pl.BlockSpec((rows, LANES), lambda i, *_: (i, 0)),
        scratch_shapes=[pltpu.VMEM((2, D, F2), f32), pltpu.VMEM((2, F, D), f32),
                        pltpu.VMEM((D, F2), bf16), pltpu.VMEM((F, D), bf16),
                        pltpu.SemaphoreType.DMA((2, 2))],
    )
    return pl.pallas_call(
        _ffn_kernel,
        out_shape=jax.ShapeDtypeStruct((P * ROW_TILE, LANES), f32),
        grid_spec=grid_spec,
        compiler_params=pltpu.CompilerParams(
            dimension_semantics=("arbitrary",), vmem_limit_bytes=VMEM_LIMIT_FFN),
        name="ffn",
    )(block_e, n_used, nxt, parity, occupied, xs, w_gate_up, b_gate_up.reshape(E, 1, F2), w_down,
      b_down.reshape(E, 1, D))


def _combine_kernel(lpos_ref, lcur_ref, lnext_ref, ys_ref, x1_ref, rw_ref, mod_ref, o_ref,
                    stage_ref, acc_ref, wb_ref, sem):
    step = pl.program_id(0) * pl.num_programs(1) + pl.program_id(1)
    n_steps = pl.num_programs(0) * pl.num_programs(1)
    slot = step % 2

    def fetch(list_ref, s):
        _run_copies(list_ref, ys_ref, stage_ref.at[s], sem.at[s], to_hbm=False)

    @pl.when(step == 0)
    def _():
        fetch(lcur_ref, 0)

    @pl.when(step + 1 < n_steps)
    def _():
        fetch(lnext_ref, 1 - slot)

    pltpu.make_async_copy(stage_ref.at[slot], stage_ref.at[slot], sem.at[slot]).wait()

    rw = rw_ref[...]
    for k in range(TOP_K):
        wb_ref[k] = jnp.broadcast_to(rw[:, k:k + 1], (TD, LANES))

    def staged(r, k):
        p = lpos_ref[k * TD + r]
        return stage_ref[slot, pl.ds(pl.multiple_of(p, ROW_TILE), ROW_TILE), :]

    def token(r, c):
        acc = wb_ref[0, pl.ds(r, 1), :] * staged(r, 0)
        for k in range(1, TOP_K):
            acc = acc + wb_ref[k, pl.ds(r, 1), :] * staged(r, k)
        acc_ref[pl.ds(pl.multiple_of(r * ROW_TILE, ROW_TILE), ROW_TILE), :] = acc
        return c
    lax.fori_loop(0, TD, token, 0, unroll=DMA_UNROLL)
    o_ref[0] = x1_ref[0] + mod_ref[0, 5:6, :] * _tiles_to_rows(acc_ref, TD)


def _combine(lpos_flat, lists, ys, x1, rw, mod):
    B, S, D = x1.shape
    nj = S // TD
    n_steps = B * nj
    return pl.pallas_call(
        _combine_kernel,
        out_shape=jax.ShapeDtypeStruct((B, S, D), f32),
        grid=(B, nj),
        in_specs=[pl.BlockSpec((TD * TOP_K,), lambda b, j: (b * nj + j,), memory_space=pltpu.SMEM),
                  pl.BlockSpec((LIST_LEN,), lambda b, j: (b * nj + j,), memory_space=pltpu.SMEM),
                  pl.BlockSpec((LIST_LEN,), lambda b, j: (jnp.minimum(b * nj + j + 1, n_steps - 1),),
                               memory_space=pltpu.SMEM),
                  pl.BlockSpec(memory_space=pl.ANY),
                  pl.BlockSpec((1, TD, D), lambda b, j: (b, j, 0)),
                  pl.BlockSpec((TD, LANES), lambda b, j: (b * nj + j, 0)),
                  pl.BlockSpec((1, 6, D), lambda b, j: (b, 0, 0))],
        out_specs=pl.BlockSpec((1, TD, D), lambda b, j: (b, j, 0)),
        scratch_shapes=[pltpu.VMEM((2, TD * TOP_K * ROW_TILE, LANES), f32),
                        pltpu.VMEM((TD * ROW_TILE, LANES), f32), pltpu.VMEM((TOP_K, TD, LANES), f32),
                        pltpu.SemaphoreType.DMA((2,))],
        compiler_params=pltpu.CompilerParams(
            dimension_semantics=("arbitrary", "arbitrary"), vmem_limit_bytes=VMEM_LIMIT),
        name="combine",
    )(lpos_flat, lists, lists, ys, x1, rw, mod)


def _moe(hp, logits, x1, mod, w_gate_up, b_gate_up, w_down, b_down):
    T = logits.shape[0]
    E = w_gate_up.shape[0]
    lp, rw, cnt, snap = _route(logits)
    lpos = lp[:, :TOP_K, :].reshape(-1)
    counts = cnt[0, :E].astype(jnp.int32)
    padded = ((counts + FFN_BLK - 1) // FFN_BLK) * FFN_BLK
    p_ends = jnp.cumsum(padded)
    p_starts = p_ends - padded
    nb = -(-T * TOP_K // FFN_BLK) + E
    n_used = jnp.maximum(p_ends[-1:] // FFN_BLK, 1).astype(jnp.int32)
    blk_start = jnp.arange(nb, dtype=jnp.int32) * FFN_BLK
    block_e = jnp.minimum(jnp.sum(p_ends[None, :] <= blk_start[:, None], axis=1), E - 1).astype(jnp.int32)
    assert E == N_RUN
    base = snap.reshape(T // TD, LANES)[:, :E].astype(jnp.int32)
    run_n = jnp.concatenate([base[1:], counts[None, :]], axis=0) - base
    run_off = jnp.cumsum(run_n, axis=1) - run_n
    run_dst = p_starts[None, :].astype(jnp.int32) + base
    lists = _copy_lists(run_dst, run_n, run_off)
    xs = _dispatch(p_ends.astype(jnp.int32), counts, n_used, lpos, lists, hp, nb * FFN_BLK)
    mine = block_e[:, None] == jnp.arange(E, dtype=jnp.int32)[None, :]
    seg_end = jnp.sum(jnp.where(mine, (p_starts + counts)[None, :], 0), axis=1)
    occupied = jnp.clip(seg_end - blk_start, 0, FFN_BLK).astype(jnp.int32)
    ys = _ffn(block_e, n_used, occupied, xs, w_gate_up, b_gate_up, w_down, b_down)
    return _combine(lpos, lists, ys, x1, rw, mod)


def kernel(x, c, rel_bias_table, w_ada, b_ada, g_norm1, w_in, w_gk_up, b_gk_up, g_gla_out, g_qnorm, g_knorm, lambda_q1, lambda_k1, lambda_q2, lambda_k2, g_subln, w_out, g_norm2, w_router, b_router, w_gate_up, b_gate_up, w_down, b_down):
    B, S, D = x.shape
    depth = w_ada.shape[0]
    bias_tiles = _bias_tiles(rel_bias_table, S, min(TQ, S))
    for l in range(depth):
        lambda_init = 0.8 - 0.6 * math.exp(-0.3 * l)
        mod = _ada(c, w_ada[l], b_ada[l])
        qg, kg, gk, kgt, gkt, vg, rg, qd, kd, vd = _inproj(
            x, mod, g_norm1[l], w_in[l], w_gk_up[l], b_gk_up[l], g_qnorm[l], g_knorm[l])
        og = _gla(qg, kg, gk, kgt, gkt, vg, rg, g_gla_out[l])
        lamv = jnp.stack([lambda_q1[l], lambda_k1[l], lambda_q2[l], lambda_k2[l]]).astype(f32)
        od = lax.cond(_scores_bounded(rel_bias_table, g_qnorm[l], g_knorm[l]),
                      functools.partial(_attn, lambda_init=lambda_init, bounded=True),
                      functools.partial(_attn, lambda_init=lambda_init, bounded=False),
                      qd, kd, vd, bias_tiles, lamv, g_subln[l])
        x1, hp, logits = _outproj(og, od, x, mod, w_out[l], g_norm2[l], w_router[l], b_router[l])
        x = _moe(hp, logits, x1, mod, w_gate_up[l], b_gate_up[l], w_down[l], b_down[l])
    return x
```

```python
import functools
import math

import jax
import jax.numpy as jnp
from jax import lax
from jax.experimental import pallas as pl
from jax.experimental.pallas import tpu as pltpu

f32 = jnp.float32
bf16 = jnp.bfloat16

N_GLA_HEADS = 4
GLA_DK = 64
GLA_DV = 128
GLA_GATE_RANK = 16
GLA_GATE_NORM = 16.0
GLA_CHUNK = 64
N_DIFF_HEADS = 4
DIFF_DQK = 64
DIFF_DV = 128
NUM_BUCKETS = 32
MAX_DISTANCE = 128
TOP_K = 4
SWIGLU_LIMIT = 7.0
SWIGLU_ALPHA = 1.702
EPS = 1e-6

GLA_QK_W = N_GLA_HEADS * GLA_DK
GLA_V_W = N_GLA_HEADS * GLA_DV
DIFF_QK_W = N_DIFF_HEADS * 2 * DIFF_DQK
DIFF_V_W = N_DIFF_HEADS * DIFF_DV

LANES = 128
NEG = -1e30
LOG2E = math.log2(math.e)
SAFE_SCORE = 40.0
NORM_SLACK = 1.02
VMEM_LIMIT = 48 * 1024 * 1024
VMEM_LIMIT_FFN = 58 * 1024 * 1024
VMEM_LIMIT_ATTN = 58 * 1024 * 1024

TM_IN = 512
INPROJ_SUB = 2
TG_GLA = 1024
PAIR = 2 * GLA_CHUNK
GLA_UNROLL = 4
TQ = 512
ATTN_UNROLL = 4
ATTN_HEADS = 2
TR = 512
TD = 512
ROW_TILE = 8
DMA_UNROLL = 16
N_RUN = 32
RUN_SIZES = tuple(TD >> b for b in range(TD.bit_length()))
LIST_DST = len(RUN_SIZES) * N_RUN
LIST_CNT = 2 * LIST_DST
LIST_LEN = 1024
assert LIST_CNT + len(RUN_SIZES) <= LIST_LEN
FFN_BLK = 512


def _nt(a, b):
    return lax.dot_general(a, b, (((1,), (1,)), ((), ())), preferred_element_type=f32)


def _mm(a, b):
    return jnp.dot(a, b, preferred_element_type=f32)


def _split(x):
    hi = x.astype(bf16)
    lo = (x - hi.astype(f32)).astype(bf16)
    return hi, lo


def _silu(x):
    return x * jax.nn.sigmoid(x)


def _ada_kernel(c_ref, w_ref, b_ref, o_ref):
    c = c_ref[...]
    o_ref[...] = _mm(_silu(c).astype(bf16), w_ref[...].astype(bf16)) + b_ref[...]


def _ada(c, w_ada, b_ada):
    B, D = c.shape
    N = w_ada.shape[1]
    bp = ROW_TILE
    assert B <= bp
    cp = jnp.zeros((bp, D), f32).at[:B].set(c)
    tn = N // 4
    out = pl.pallas_call(
        _ada_kernel,
        out_shape=jax.ShapeDtypeStruct((bp, N), f32),
        grid=(N // tn,),
        in_specs=[pl.BlockSpec((bp, D), lambda j: (0, 0)),
                  pl.BlockSpec((D, tn), lambda j: (0, j)),
                  pl.BlockSpec((1, tn), lambda j: (0, j))],
        out_specs=pl.BlockSpec((bp, tn), lambda j: (0, j)),
        compiler_params=pltpu.CompilerParams(vmem_limit_bytes=VMEM_LIMIT),
        name="ada",
    )(cp, w_ada, b_ada.reshape(1, N))
    return out[:B].reshape(B, 6, D)


def _inproj_kernel(x_ref, mod_ref, g1_ref, w_ref, wkt_ref, wlo_ref, wup_ref, wupt_ref,
                   bup_ref, bupt_ref, gqk_ref, grp_ref, grpt_ref,
                   qg_ref, kg_ref, gk_ref, kgt_ref, gkt_ref, vg_ref, rg_ref,
                   qd_ref, kd_ref, vd_ref, wm_ref):
    @pl.when((pl.program_id(0) == 0) & (pl.program_id(1) == 0))
    def _():
        split = wm_ref.shape[1] // 2
        wm_ref[:, :split] = w_ref[:, :split]
        wm_ref[:, split:] = w_ref[:, split + GLA_GATE_RANK:]
    tm = x_ref.shape[1]
    sub = tm // INPROJ_SUB
    for t in range(INPROJ_SUB):
        _inproj_rows(slice(t * sub, (t + 1) * sub), x_ref, mod_ref, g1_ref, wm_ref, wkt_ref, wlo_ref, wup_ref,
                     wupt_ref, bup_ref, bupt_ref, gqk_ref, grp_ref, grpt_ref, qg_ref, kg_ref, gk_ref, kgt_ref,
                     gkt_ref, vg_ref, rg_ref, qd_ref, kd_ref, vd_ref)


def _inproj_rows(rows, x_ref, mod_ref, g1_ref, wm_ref, wkt_ref, wlo_ref, wup_ref, wupt_ref,
                 bup_ref, bupt_ref, gqk_ref, grp_ref, grpt_ref,
                 qg_ref, kg_ref, gk_ref, kgt_ref, gkt_ref, vg_ref, rg_ref, qd_ref, kd_ref, vd_ref):
    x = x_ref[0, rows, :]
    ms = jnp.mean(x * x, axis=-1, keepdims=True)
    y = x * lax.rsqrt(ms + EPS) * g1_ref[...]
    h = (y * (1.0 + mod_ref[0, 1:2, :]) + mod_ref[0, 0:1, :]).astype(bf16)

    def proj(a, b):
        return _mm(h, wm_ref[:, a:b])

    o = 0
    qg_ref[0, rows, :] = proj(o, o + GLA_QK_W); o += GLA_QK_W
    kg_ref[0, rows, :] = proj(o, o + GLA_QK_W); o += GLA_QK_W
    vg_ref[0, rows, :] = proj(o, o + GLA_V_W).astype(bf16); o += GLA_V_W
    rg_ref[0, rows, :] = proj(o, o + GLA_V_W); o += GLA_V_W
    qk = proj(o, o + 2 * DIFF_QK_W); o += 2 * DIFF_QK_W
    vd_ref[0, rows, :] = proj(o, o + DIFF_V_W).astype(bf16)

    slab0 = rows.start // PAIR
    kgt = _nt(wkt_ref[...], h)
    for j in range(kgt.shape[1] // PAIR):
        kgt_ref[0, slab0 + j] = kgt[:, j * PAIR:(j + 1) * PAIR]

    lo = _mm(h, wlo_ref[...]).astype(bf16)
    z = _mm(lo, wup_ref[...]) + bup_ref[...]
    gk_ref[0, rows, :] = (jnp.minimum(z, 0.0) - jnp.log1p(jnp.exp(-jnp.abs(z)))) * (1.0 / GLA_GATE_NORM)
    zt = _nt(wupt_ref[...], lo) + bupt_ref[...]
    gkt = (jnp.minimum(zt, 0.0) - jnp.log1p(jnp.exp(-jnp.abs(zt)))) * (1.0 / GLA_GATE_NORM)
    for j in range(gkt.shape[1] // PAIR):
        gkt_ref[0, slab0 + j] = gkt[:, j * PAIR:(j + 1) * PAIR]

    sq_hi, sq_lo = _split(qk * qk)
    gs = _mm(sq_hi, grp_ref[...]) + _mm(sq_lo, grp_ref[...])
    r = lax.rsqrt(gs * (1.0 / DIFF_DQK) + EPS)
    r_hi, r_lo = _split(r)
    rb = _mm(r_hi, grpt_ref[...]) + _mm(r_lo, grpt_ref[...])
    qkn = qk * rb * gqk_ref[...]
    qd_ref[0, rows, :] = qkn[:, :DIFF_QK_W].astype(bf16)
    kd_ref[0, rows, :] = qkn[:, DIFF_QK_W:].astype(bf16)


def _inproj(x, mod, g_norm1, w_in, w_gk_up, b_gk_up, g_qnorm, g_knorm):
    B, S, D = x.shape
    offs = [0]
    for w in (GLA_QK_W, GLA_QK_W, GLA_V_W, GLA_V_W, GLA_GATE_RANK, DIFF_QK_W, DIFF_QK_W, DIFF_V_W):
        offs.append(offs[-1] + w)
    assert offs[4] == offs[8] - offs[5]
    w_all = w_in.astype(bf16)
    w_kt = w_in[:, offs[1]:offs[2]].T.astype(bf16)
    w_lo = jnp.zeros((D, LANES), f32).at[:, :GLA_GATE_RANK].set(w_in[:, offs[4]:offs[5]]).astype(bf16)
    w_up = jnp.zeros((LANES, GLA_QK_W), f32).at[:GLA_GATE_RANK].set(w_gk_up).astype(bf16)
    w_upt = w_up.T
    b_up = b_gk_up.reshape(1, GLA_QK_W)
    b_upt = b_gk_up.reshape(GLA_QK_W, 1)
    n_grp = 2 * DIFF_QK_W // DIFF_DQK
    gqk = jnp.concatenate([jnp.tile(g_qnorm, n_grp // 2) * (DIFF_DQK ** -0.5 * LOG2E),
                           jnp.tile(g_knorm, n_grp // 2)]).reshape(1, 2 * DIFF_QK_W)
    grp = (jnp.arange(2 * DIFF_QK_W)[:, None] // DIFF_DQK == jnp.arange(LANES)[None, :]).astype(bf16)
    grpt = grp.T
    nw = 2 * offs[4]
    tm = TM_IN
    const = lambda shape: pl.BlockSpec(shape, lambda b, i: (0,) * len(shape))
    row = lambda w: pl.BlockSpec((1, tm, w), lambda b, i: (b, i, 0))
    colT = pl.BlockSpec((1, tm // PAIR, GLA_QK_W, PAIR), lambda b, i: (b, i, 0, 0))
    outs = pl.pallas_call(
        _inproj_kernel,
        out_shape=[jax.ShapeDtypeStruct((B, S, GLA_QK_W), f32),
                   jax.ShapeDtypeStruct((B, S, GLA_QK_W), f32),
                   jax.ShapeDtypeStruct((B, S, GLA_QK_W), f32),
                   jax.ShapeDtypeStruct((B, S // PAIR, GLA_QK_W, PAIR), f32),
                   jax.ShapeDtypeStruct((B, S // PAIR, GLA_QK_W, PAIR), f32),
                   jax.ShapeDtypeStruct((B, S, GLA_V_W), bf16),
                   jax.ShapeDtypeStruct((B, S, GLA_V_W), f32),
                   jax.ShapeDtypeStruct((B, S, DIFF_QK_W), bf16),
                   jax.ShapeDtypeStruct((B, S, DIFF_QK_W), bf16),
                   jax.ShapeDtypeStruct((B, S, DIFF_V_W), bf16)],
        grid=(B, S // tm),
        in_specs=[row(D),
                  pl.BlockSpec((1, 6, D), lambda b, i: (b, 0, 0)),
                  const((1, D)), const((D, w_in.shape[1])), const((GLA_QK_W, D)), const((D, LANES)),
                  const((LANES, GLA_QK_W)), const((GLA_QK_W, LANES)),
                  const((1, GLA_QK_W)), const((GLA_QK_W, 1)),
                  const((1, 2 * DIFF_QK_W)), const((2 * DIFF_QK_W, LANES)),
                  const((LANES, 2 * DIFF_QK_W))],
        out_specs=[row(GLA_QK_W), row(GLA_QK_W), row(GLA_QK_W), colT, colT,
                   row(GLA_V_W), row(GLA_V_W), row(DIFF_QK_W), row(DIFF_QK_W), row(DIFF_V_W)],
        scratch_shapes=[pltpu.VMEM((D, nw), bf16)],
        compiler_params=pltpu.CompilerParams(
            dimension_semantics=("arbitrary", "arbitrary"), vmem_limit_bytes=VMEM_LIMIT),
        name="inproj",
    )(x, mod, g_norm1.reshape(1, D), w_all, w_kt, w_lo, w_up, w_upt, b_up, b_upt, gqk, grp, grpt)
    return outs


def _gla_kernel(q_ref, k_ref, g_ref, kt_ref, gt_ref, v_ref, r_ref, gout_ref, tri_ref, trit_ref,
                o_ref, s_ref, *, n_pairs):
    H, DK, DV = N_GLA_HEADS, GLA_DK, GLA_DV

    @pl.when(pl.program_id(1) == 0)
    def _():
        s_ref[...] = jnp.zeros_like(s_ref)

    tri = tri_ref[...]
    trit = trit_ref[...]
    tri_b = tri > 0
    lane_head = lax.broadcasted_iota(jnp.int32, (1, H * DK), 1) // DK
    row_head = lax.broadcasted_iota(jnp.int32, (H * PAIR, 1), 0) // PAIR
    qmask = row_head == lane_head
    row_first = lax.broadcasted_iota(jnp.int32, (PAIR, 1), 0) < GLA_CHUNK
    row_first4 = (lax.broadcasted_iota(jnp.int32, (H * PAIR, 1), 0) % PAIR) < GLA_CHUNK
    lane_first = lax.broadcasted_iota(jnp.int32, (1, PAIR), 1) < GLA_CHUNK
    scale = DK ** -0.5
    gout = gout_ref[...]

    def pair(p, carry):
        r0 = pl.multiple_of(p * PAIR, PAIR)
        q = q_ref[0, pl.ds(r0, PAIR), :]
        k = k_ref[0, pl.ds(r0, PAIR), :]
        g = g_ref[0, pl.ds(r0, PAIR), :]
        kt = kt_ref[0, p]
        gt = gt_ref[0, p]
        v = v_ref[0, pl.ds(r0, PAIR), :]

        g_hi, g_lo = _split(g)
        gc = _mm(tri, g_hi) + _mm(tri, g_lo)
        gt_hi, gt_lo = _split(gt)
        gct = _mm(gt_hi, trit) + _mm(gt_lo, trit)
        g_last = jnp.where(row_first, gc[GLA_CHUNK - 1:GLA_CHUNK, :], gc[PAIR - 1:PAIR, :])
        gl0 = gct[:, GLA_CHUNK - 1:GLA_CHUNK]
        gl1 = gct[:, PAIR - 1:PAIR]
        g_last_t = jnp.where(lane_first, gl0, gl1)

        q_e = (q * (jnp.exp(gc) * scale)).astype(bf16)
        k_e = (k * jnp.exp(-gc)).astype(bf16)
        ks_t = kt * jnp.exp(g_last_t - gct)
        ks_t0 = jnp.where(lane_first, ks_t, 0.0).astype(bf16)
        ks_t1 = jnp.where(lane_first, 0.0, ks_t).astype(bf16)
        del g_last

        qm = jnp.where(qmask, jnp.concatenate([q_e] * H, axis=0), jnp.zeros((), bf16))
        a = _nt(qm, k_e)
        s0 = s_ref[...]

        u0 = []
        u1 = []
        for h in range(H):
            v_h = v[:, h * DV:(h + 1) * DV]
            u0.append(_mm(ks_t0[h * DK:(h + 1) * DK], v_h))
            u1.append(_mm(ks_t1[h * DK:(h + 1) * DK], v_h))
        u0 = jnp.concatenate(u0, axis=0)
        u1 = jnp.concatenate(u1, axis=0)
        s1 = s0 * jnp.exp(gl0) + u0
        s_ref[...] = s1 * jnp.exp(gl1) + u1

        o_inter = jnp.where(row_first4, _mm(qm, s0.astype(bf16)), _mm(qm, s1.astype(bf16)))
        for h in range(H):
            a_h = jnp.where(tri_b, a[h * PAIR:(h + 1) * PAIR], 0.0).astype(bf16)
            o_h = _mm(a_h, v[:, h * DV:(h + 1) * DV]) + o_inter[h * PAIR:(h + 1) * PAIR]
            ms = jnp.mean(o_h * o_h, axis=-1, keepdims=True)
            o_n = o_h * lax.rsqrt(ms + EPS) * gout
            r_h = r_ref[0, pl.ds(r0, PAIR), h * DV:(h + 1) * DV]
            o_ref[0, pl.ds(r0, PAIR), h * DV:(h + 1) * DV] = (o_n * _silu(r_h)).astype(bf16)
        return carry

    lax.fori_loop(0, n_pairs, pair, 0, unroll=GLA_UNROLL)


def _gla(qg, kg, gk, kgt, gkt, vg, rg, g_gla_out):
    B, S, _ = qg.shape
    tg = min(TG_GLA, S)
    r = jnp.arange(PAIR)
    tri = ((r[:, None] // GLA_CHUNK == r[None, :] // GLA_CHUNK) & (r[None, :] <= r[:, None])).astype(bf16)
    row = lambda w: pl.BlockSpec((1, tg, w), lambda b, i: (b, i, 0))
    colT = pl.BlockSpec((1, tg // PAIR, GLA_QK_W, PAIR), lambda b, i: (b, i, 0, 0))
    const = lambda shape: pl.BlockSpec(shape, lambda b, i: (0,) * len(shape))
    return pl.pallas_call(
        functools.partial(_gla_kernel, n_pairs=tg // PAIR),
        out_shape=jax.ShapeDtypeStruct((B, S, GLA_V_W), bf16),
        grid=(B, S // tg),
        in_specs=[row(GLA_QK_W), row(GLA_QK_W), row(GLA_QK_W), colT, colT,
                  row(GLA_V_W), row(GLA_V_W), const((1, GLA_DV)),
                  const((PAIR, PAIR)), const((PAIR, PAIR))],
        out_specs=row(GLA_V_W),
        scratch_shapes=[pltpu.VMEM((GLA_QK_W, GLA_DV), f32)],
        compiler_params=pltpu.CompilerParams(
            dimension_semantics=("arbitrary", "arbitrary"), vmem_limit_bytes=VMEM_LIMIT),
        name="gla",
    )(qg, kg, gk, kgt, gkt, vg, rg, g_gla_out.reshape(1, GLA_DV), tri, tri.T)


def _attn_finish(o, gsub_ref, o_ref, lambda_init):
    ms = jnp.mean(o * o, axis=-1, keepdims=True)
    o_ref[0] = (o * lax.rsqrt(ms + EPS) * gsub_ref[...] * (1.0 - lambda_init)).astype(bf16)


def _attn_lambda(lamv_ref, lambda_init):
    lv = lamv_ref[...]
    return (jnp.exp(jnp.sum(lv[0:1] * lv[1:2], axis=-1, keepdims=True))
            - jnp.exp(jnp.sum(lv[2:3] * lv[3:4], axis=-1, keepdims=True)) + lambda_init)


def _attn_bounded_kernel(q_ref, k_ref, v_ref, bias_ref, lamv_ref, gsub_ref, o_ref, vaug_ref, *, lambda_init):
    qi = pl.program_id(2)
    tq = q_ref.shape[1]
    S = k_ref.shape[1]
    n_head = q_ref.shape[2] // (2 * DIFF_DQK)

    @pl.when(qi == 0)
    def _():
        lane = lax.broadcasted_iota(jnp.int32, (S, DIFF_DV), 1)
        for h in range(n_head):
            vaug_ref[h, :, :DIFF_DV] = v_ref[0, :, h * DIFF_DV:(h + 1) * DIFF_DV]
            vaug_ref[h, :, DIFF_DV:] = jnp.where(lane == 0, 1.0, 0.0).astype(bf16)

    lane = lax.broadcasted_iota(jnp.int32, (1, 2 * DIFF_DQK), 1)
    zero = jnp.zeros((), bf16)
    qs = []
    for h in range(n_head):
        q = q_ref[0, :, h * 2 * DIFF_DQK:(h + 1) * 2 * DIFF_DQK]
        qs.append((jnp.where(lane < DIFF_DQK, q, zero), jnp.where(lane < DIFF_DQK, zero, q)))

    def update(accs, k0, bias):
        out = []
        for h in range(n_head):
            kb = k_ref[0, pl.ds(k0, tq), h * 2 * DIFF_DQK:(h + 1) * 2 * DIFF_DQK]
            vb = vaug_ref[h, pl.ds(k0, tq), :]
            for c in range(2):
                s = _nt(qs[h][c], kb)
                if bias is not None:
                    s = s + bias[h][c]
                out.append(accs[2 * h + c] + _mm(jnp.exp2(s).astype(bf16), vb))
        return tuple(out)

    def far(kj, accs):
        return update(accs, pl.multiple_of(kj * tq, tq), None)

    def far_group(g, accs):
        for u in range(ATTN_UNROLL):
            accs = far(g * ATTN_UNROLL + u, accs)
        return accs

    def block_or_masked(accs, kj, rel):
        exists = kj >= 0
        k0 = pl.multiple_of(jnp.maximum(kj, 0) * tq, tq)
        if rel is None:
            tiles = [(jnp.where(exists, 0.0, NEG),) * 2] * n_head
        else:
            tiles = [tuple(jnp.where(exists, bias_ref[h, c, rel], NEG) for c in range(2)) for h in range(n_head)]
        return update(accs, k0, tiles)

    accs = tuple(jnp.zeros((tq, 2 * DIFF_DV), f32) for _ in range(2 * n_head))
    accs = block_or_masked(accs, qi, 1)
    accs = block_or_masked(accs, qi - 1, 0)
    for u in range(2, ATTN_UNROLL):
        accs = block_or_masked(accs, qi - u, None)
    n_far = jnp.maximum(qi + 1 - ATTN_UNROLL, 0)
    n_grp = n_far // ATTN_UNROLL
    accs = lax.fori_loop(0, n_grp, far_group, accs)
    accs = lax.fori_loop(n_grp * ATTN_UNROLL, n_far, far, accs)
    lam = _attn_lambda(lamv_ref, lambda_init)
    for h in range(n_head):
        a0, a1 = accs[2 * h], accs[2 * h + 1]
        o = a0[:, :DIFF_DV] / a0[:, DIFF_DV:DIFF_DV + 1] - lam * (a1[:, :DIFF_DV] / a1[:, DIFF_DV:DIFF_DV + 1])
        ms = jnp.mean(o * o, axis=-1, keepdims=True)
        o_ref[0, :, h * DIFF_DV:(h + 1) * DIFF_DV] = (
            o * lax.rsqrt(ms + EPS) * gsub_ref[...] * (1.0 - lambda_init)).astype(bf16)


def _attn_kernel(q_ref, k_ref, v_ref, bias_ref, lamv_ref, gsub_ref, o_ref, *, lambda_init):
    qi = pl.program_id(2)
    tq = q_ref.shape[1]
    q = q_ref[0]
    lane = lax.broadcasted_iota(jnp.int32, (1, 2 * DIFF_DQK), 1)
    zero = jnp.zeros((), bf16)
    qs = (jnp.where(lane < DIFF_DQK, q, zero), jnp.where(lane < DIFF_DQK, zero, q))

    def update(state, kb, vb, bias):
        new = []
        for c in range(2):
            m, l, acc = state[c]
            s = _nt(qs[c], kb)
            if bias is not None:
                s = s + bias[c]
            m_new = jnp.maximum(m, jnp.max(s, axis=-1, keepdims=True))
            alpha = jnp.exp2(m - m_new)
            p = jnp.exp2(s - m_new)
            l = alpha * l + jnp.sum(p, axis=-1, keepdims=True)
            acc = alpha * acc + _mm(p.astype(bf16), vb)
            new.append((m_new, l, acc))
        return tuple(new)

    init = tuple((jnp.full((tq, 1), NEG, f32), jnp.zeros((tq, 1), f32), jnp.zeros((tq, DIFF_DV), f32))
                 for _ in range(2))

    def far(kj, state):
        k0 = pl.multiple_of(kj * tq, tq)
        return update(state, k_ref[0, pl.ds(k0, tq), :], v_ref[0, pl.ds(k0, tq), :], None)

    state = lax.fori_loop(0, jnp.maximum(qi - 1, 0), far, init)

    kd0 = pl.multiple_of(qi * tq, tq)
    state = update(state, k_ref[0, pl.ds(kd0, tq), :], v_ref[0, pl.ds(kd0, tq), :],
                   (bias_ref[0, 0, 1], bias_ref[0, 1, 1]))
    kp0 = pl.multiple_of(jnp.maximum(qi - 1, 0) * tq, tq)
    has_prev = qi > 0
    state = update(state, k_ref[0, pl.ds(kp0, tq), :], v_ref[0, pl.ds(kp0, tq), :],
                   (jnp.where(has_prev, bias_ref[0, 0, 0], NEG), jnp.where(has_prev, bias_ref[0, 1, 0], NEG)))

    (_, l0, a0), (_, l1, a1) = state
    o = a0 / l0 - _attn_lambda(lamv_ref, lambda_init) * (a1 / l1)
    _attn_finish(o, gsub_ref, o_ref, lambda_init)


def _t5_bucket(n):
    max_exact = NUM_BUCKETS // 2
    nf = jnp.maximum(n, 1).astype(f32)
    large = max_exact + (jnp.log(nf / max_exact) / math.log(MAX_DISTANCE / max_exact)
                         * (NUM_BUCKETS - max_exact)).astype(jnp.int32)
    large = jnp.minimum(large, NUM_BUCKETS - 1)
    return jnp.where(n < max_exact, n, large)


def _toeplitz_kernel(w_ref, o_ref):
    n = o_ref.shape[-1]
    for t in range(2):
        rows = jnp.broadcast_to(w_ref[0, t:t + 1, :], (n, 2 * n))
        o_ref[0, 0, t] = pltpu.roll(rows, 0, 1, stride=1, stride_axis=0)[:, n:]


def _bias_tiles(rel_bias_table, S, n):
    HM = rel_bias_table.shape[1]
    assert n >= MAX_DISTANCE
    d = jnp.arange(2 * n, dtype=jnp.int32)
    in_bucket = _t5_bucket(d)[None, :] == jnp.arange(NUM_BUCKETS, dtype=jnp.int32)[:, None]
    by_dist = jnp.sum(jnp.where(in_bucket[:, None, :], rel_bias_table.astype(f32)[:, :, None], 0.0), axis=0)
    rel = (by_dist - rel_bias_table[NUM_BUCKETS - 1].astype(f32)[:, None]) * LOG2E
    w_diag = jnp.concatenate([rel[:, n::-1], jnp.full((HM, n - 1), NEG, f32)], axis=1)
    w_prev = jnp.concatenate([rel[:, :1], rel[:, :0:-1]], axis=1)
    w = jnp.stack([w_prev, w_diag], axis=1)
    return pl.pallas_call(
        _toeplitz_kernel,
        out_shape=jax.ShapeDtypeStruct((HM // 2, 2, 2, n, n), f32),
        grid=(HM // 2, 2),
        in_specs=[pl.BlockSpec((1, 2, 2 * n), lambda h, m: (h * 2 + m, 0, 0))],
        out_specs=pl.BlockSpec((1, 1, 2, n, n), lambda h, m: (h, m, 0, 0, 0)),
        compiler_params=pltpu.CompilerParams(vmem_limit_bytes=VMEM_LIMIT),
        name="bias_tiles",
    )(w)


def _attn(qd, kd, vd, bias_tiles, lamv, g_subln, lambda_init, bounded):
    B, S, _ = qd.shape
    H = N_DIFF_HEADS
    tq = min(TQ, S)
    body = _attn_bounded_kernel if bounded else _attn_kernel
    hs = ATTN_HEADS if bounded else 1
    scratch = [pltpu.VMEM((hs, S, 2 * DIFF_DV), bf16)] if bounded else []
    return pl.pallas_call(
        functools.partial(body, lambda_init=lambda_init),
        out_shape=jax.ShapeDtypeStruct((B, S, DIFF_V_W), bf16),
        scratch_shapes=scratch,
        grid=(B, H // hs, S // tq),
        in_specs=[pl.BlockSpec((1, tq, hs * 2 * DIFF_DQK), lambda b, h, i: (b, i, h)),
                  pl.BlockSpec((1, S, hs * 2 * DIFF_DQK), lambda b, h, i: (b, 0, h)),
                  pl.BlockSpec((1, S, hs * DIFF_DV), lambda b, h, i: (b, 0, h)),
                  pl.BlockSpec((hs, 2, 2, tq, tq), lambda b, h, i: (h, 0, 0, 0, 0)),
                  pl.BlockSpec((4, DIFF_DQK), lambda b, h, i: (0, 0)),
                  pl.BlockSpec((1, DIFF_DV), lambda b, h, i: (0, 0))],
        out_specs=pl.BlockSpec((1, tq, hs * DIFF_DV), lambda b, h, i: (b, i, h)),
        compiler_params=pltpu.CompilerParams(
            dimension_semantics=("arbitrary", "arbitrary", "arbitrary"),
            vmem_limit_bytes=VMEM_LIMIT_ATTN if bounded else VMEM_LIMIT),
        name="attn_bounded" if bounded else "attn_online",
    )(qd, kd, vd, bias_tiles, lamv, g_subln.reshape(1, DIFF_DV))


def _scores_bounded(rel_bias_table, g_qnorm, g_knorm):
    qk = DIFF_DQK ** 0.5 * jnp.max(jnp.abs(g_qnorm)) * jnp.max(jnp.abs(g_knorm)) * NORM_SLACK
    rel = jnp.max(jnp.abs(rel_bias_table - rel_bias_table[NUM_BUCKETS - 1:]))
    return qk + rel <= SAFE_SCORE


def _rows_to_tiles(x, ref):
    n = x.shape[0]
    for c in range(ROW_TILE):
        ref[pl.ds(c, n, stride=ROW_TILE), :] = x[:, c * LANES:(c + 1) * LANES]


def _tiles_to_rows(ref, n):
    return jnp.concatenate([ref[pl.ds(c, n, stride=ROW_TILE), :] for c in range(ROW_TILE)], axis=1)


def _outproj_kernel(og_ref, od_ref, x_ref, mod_ref, wo_ref, g2_ref, wr_ref, br_ref,
                    x1_ref, hp_ref, lg_ref):
    half = og_ref.shape[2]
    sub = og_ref.shape[1] // INPROJ_SUB
    for t in range(INPROJ_SUB):
        rows = slice(t * sub, (t + 1) * sub)
        mix = _mm(og_ref[0, rows, :], wo_ref[:half, :]) + _mm(od_ref[0, rows, :], wo_ref[half:, :])
        x1 = x_ref[0, rows, :] + mod_ref[0, 2:3, :] * mix
        x1_ref[0, rows, :] = x1
        ms = jnp.mean(x1 * x1, axis=-1, keepdims=True)
        y = x1 * lax.rsqrt(ms + EPS) * g2_ref[...]
        h = (y * (1.0 + mod_ref[0, 4:5, :]) + mod_ref[0, 3:4, :]).astype(bf16)
        lg_ref[rows, :] = _mm(h, wr_ref[...]) + br_ref[...]
        _rows_to_tiles(h.astype(f32), hp_ref.at[pl.ds(t * sub * ROW_TILE, sub * ROW_TILE)])


def _outproj(og, od, x, mod, w_out, g_norm2, w_router, b_router):
    B, S, D = x.shape
    assert D == ROW_TILE * LANES, "the token-tile layout needs a model row to fill one (8,128) tile"
    E = w_router.shape[1]
    tm = TM_IN
    nj = S // tm
    w_r = jnp.zeros((D, LANES), f32).at[:, :E].set(w_router).astype(bf16)
    b_r = jnp.full((1, LANES), NEG, f32).at[0, :E].set(b_router)
    const = lambda shape: pl.BlockSpec(shape, lambda b, i: (0,) * len(shape))
    return pl.pallas_call(
        _outproj_kernel,
        out_shape=[jax.ShapeDtypeStruct((B, S, D), f32),
                   jax.ShapeDtypeStruct((B * S * ROW_TILE, LANES), f32),
                   jax.ShapeDtypeStruct((B * S, LANES), f32)],
        grid=(B, nj),
        in_specs=[pl.BlockSpec((1, tm, og.shape[2]), lambda b, i: (b, i, 0)),
                  pl.BlockSpec((1, tm, od.shape[2]), lambda b, i: (b, i, 0)),
                  pl.BlockSpec((1, tm, D), lambda b, i: (b, i, 0)),
                  pl.BlockSpec((1, 6, D), lambda b, i: (b, 0, 0)),
                  const((w_out.shape[0], D)), const((1, D)), const((D, LANES)), const((1, LANES))],
        out_specs=[pl.BlockSpec((1, tm, D), lambda b, i: (b, i, 0)),
                   pl.BlockSpec((tm * ROW_TILE, LANES), lambda b, i: (b * nj + i, 0)),
                   pl.BlockSpec((tm, LANES), lambda b, i: (b * nj + i, 0))],
        compiler_params=pltpu.CompilerParams(
            dimension_semantics=("arbitrary", "arbitrary"), vmem_limit_bytes=VMEM_LIMIT),
        name="outproj",
    )(og, od, x, mod, w_out.astype(bf16), g_norm2.reshape(1, D), w_r, b_r)


def _route_kernel(lg_ref, lt_ref, ut_ref, pk_ref, lp_ref, rw_ref, cnt_ref, snap_ref, run_ref):
    @pl.when(pl.program_id(0) == 0)
    def _():
        run_ref[...] = jnp.zeros_like(run_ref)

    x = lg_ref[...]
    tr = x.shape[0]
    lane = lax.broadcasted_iota(jnp.int32, (tr, LANES), 1)
    lane_f = lane.astype(f32)
    vals, hots = [], []
    for _ in range(TOP_K):
        m = jnp.max(x, axis=-1, keepdims=True)
        idx = jnp.min(jnp.where(x == m, lane_f, float(LANES)), axis=-1, keepdims=True)
        hot = lane_f == idx
        x = jnp.where(hot, -jnp.inf, x)
        vals.append(m)
        hots.append(hot)
    ex = [jnp.exp(v - vals[0]) for v in vals]
    den = ex[0] + ex[1] + ex[2] + ex[3]
    sel = (hots[0] | hots[1] | hots[2] | hots[3]).astype(f32)
    rank = _mm(lt_ref[...], sel.astype(bf16)) + run_ref[...]
    run_ref[...] = run_ref[...] + jnp.sum(sel, axis=0, keepdims=True)
    cnt_ref[...] = run_ref[...]
    base = rank[0:1, :]
    snap_ref[0] = base
    n_hi, n_lo = _split(jnp.sum(sel, axis=0, keepdims=True))
    start = _mm(n_hi, ut_ref[...]) + _mm(n_lo, ut_ref[...])
    place = rank - base + start
    pos = jnp.zeros((tr, LANES), f32)
    rw = jnp.zeros((tr, LANES), f32)
    for k in range(TOP_K):
        rk = jnp.sum(jnp.where(hots[k], place, 0.0), axis=-1, keepdims=True) * ROW_TILE
        pos = jnp.where(lane == k, rk, pos)
        rw = jnp.where(lane == k, ex[k] / den, rw)
    rw_ref[...] = rw
    p_hi, p_lo = _split(pos)
    lp_ref[0] = (_nt(pk_ref[...], p_hi) + _nt(pk_ref[...], p_lo)).astype(jnp.int32)


def _route(logits):
    T = logits.shape[0]
    tr = min(TR, T)
    assert tr == TD, "routing and dispatch share one token tile"
    r = jnp.arange(tr)
    lt = (r[None, :] < r[:, None]).astype(bf16)
    e = jnp.arange(LANES)
    ut = (e[:, None] < e[None, :]).astype(bf16)
    pick = (jnp.arange(ROW_TILE)[:, None] == e[None, :]).astype(bf16)
    return pl.pallas_call(
        _route_kernel,
        out_shape=[jax.ShapeDtypeStruct((T // tr, ROW_TILE, tr), jnp.int32),
                   jax.ShapeDtypeStruct((T, LANES), f32),
                   jax.ShapeDtypeStruct((1, LANES), f32),
                   jax.ShapeDtypeStruct((T // tr, 1, LANES), f32)],
        grid=(T // tr,),
        in_specs=[pl.BlockSpec((tr, LANES), lambda i: (i, 0)),
                  pl.BlockSpec((tr, tr), lambda i: (0, 0)),
                  pl.BlockSpec((LANES, LANES), lambda i: (0, 0)),
                  pl.BlockSpec((ROW_TILE, LANES), lambda i: (0, 0))],
        out_specs=[pl.BlockSpec((1, ROW_TILE, tr), lambda i: (i, 0, 0)),
                   pl.BlockSpec((tr, LANES), lambda i: (i, 0)),
                   pl.BlockSpec((1, LANES), lambda i: (0, 0)),
                   pl.BlockSpec((1, 1, LANES), lambda i: (i, 0, 0))],
        scratch_shapes=[pltpu.VMEM((1, LANES), f32)],
        compiler_params=pltpu.CompilerParams(dimension_semantics=("arbitrary",)),
        name="route",
    )(logits, lt, ut, pick)


def _run_copies(list_ref, hbm_ref, stage_ref, sem, to_hbm):
    for c, size in enumerate(RUN_SIZES):
        def one(i, carry, c=c, size=size):
            s0 = list_ref[c * N_RUN + i]
            d0 = list_ref[LIST_DST + c * N_RUN + i]
            stage = stage_ref.at[pl.ds(pl.multiple_of(s0, ROW_TILE), size * ROW_TILE)]
            rows = hbm_ref.at[pl.ds(pl.multiple_of(d0, ROW_TILE), size * ROW_TILE)]
            src, dst = (stage, rows) if to_hbm else (rows, stage)
            pltpu.make_async_copy(src, dst, sem).start(priority=c % 2)
            return carry
        lax.fori_loop(0, list_ref[LIST_CNT + c], one, 0)


def _copy_lists(run_dst, run_n, run_off):
    n_tiles, n_exp = run_n.shape
    n_size = len(RUN_SIZES)
    size = jnp.asarray(RUN_SIZES, jnp.int32)[None, None, :]
    n = jnp.broadcast_to(run_n.T[:, :, None], (n_exp, n_tiles, n_size))
    bit = ((n & size) != 0).reshape(n_exp, -1)
    before = (n & ~(2 * size - 1)).reshape(n_exp, -1)
    place = jnp.cumsum(bit, axis=0) - 1
    pick = bit[None, :, :] & (place[None, :, :] == jnp.arange(N_RUN)[:, None, None])

    def compact(v):
        v = jnp.broadcast_to(v.T[:, :, None], (n_exp, n_tiles, n_size)).reshape(n_exp, -1) + before
        out = jnp.sum(jnp.where(pick, v[None, :, :], 0), axis=1) * ROW_TILE
        return out.reshape(N_RUN, n_tiles, n_size).transpose(1, 2, 0).reshape(n_tiles, -1)
    cnt = jnp.sum(bit, axis=0).reshape(n_tiles, n_size).astype(jnp.int32)
    pad = jnp.zeros((n_tiles, LIST_LEN - LIST_CNT - n_size), jnp.int32)
    lists = jnp.concatenate([compact(run_off), compact(run_dst), cnt, pad], axis=1)
    return lists.reshape(-1).astype(jnp.int32)


def _dispatch_kernel(pend_ref, cnt_ref, nu_ref, lpos_ref, list_ref, h_ref, xs_ref,
                     zero_ref, stage_ref, sem, zsem):
    n_tok = h_ref.shape[0] // ROW_TILE
    blk_rows = FFN_BLK * ROW_TILE

    @pl.when(pl.program_id(0) == 0)
    def _():
        zero_ref[...] = jnp.zeros_like(zero_ref)
        n_exp = pend_ref.shape[0]

        def last_block(e):
            return xs_ref.at[pl.ds(pl.multiple_of((pend_ref[e] - FFN_BLK) * ROW_TILE, blk_rows), blk_rows)]

        def zfill(e, c):
            @pl.when(cnt_ref[e] > 0)
            def _():
                pltpu.make_async_copy(zero_ref, last_block(e), zsem).start()
            return c

        def zwait(e, c):
            @pl.when(cnt_ref[e] > 0)
            def _():
                pltpu.make_async_copy(zero_ref, last_block(e), zsem).wait()
            return c

        lax.fori_loop(0, n_exp, zfill, 0)
        lax.fori_loop(0, n_exp, zwait, 0)

        def tail_block(i):
            return xs_ref.at[pl.ds(pl.multiple_of(i * blk_rows, blk_rows), blk_rows)]

        def tfill(i, c):
            pltpu.make_async_copy(zero_ref, tail_block(i), zsem).start()
            return c

        def twait(i, c):
            pltpu.make_async_copy(zero_ref, tail_block(i), zsem).wait()
            return c

        n_blk = xs_ref.shape[0] // blk_rows
        lax.fori_loop(nu_ref[0], n_blk, tfill, 0)
        lax.fori_loop(nu_ref[0], n_blk, twait, 0)

    step = pl.program_id(0)
    slot = step % 2
    stage = stage_ref.at[slot]

    def place(g, c):
        for u in range(DMA_UNROLL):
            r = g * DMA_UNROLL + u
            row = h_ref[pl.ds(pl.multiple_of(r * ROW_TILE, ROW_TILE), ROW_TILE), :]
            for k in range(TOP_K):
                p = lpos_ref[k * n_tok + r]
                stage[pl.ds(pl.multiple_of(p, ROW_TILE), ROW_TILE), :] = row
        return c
    lax.fori_loop(0, n_tok // DMA_UNROLL, place, 0)

    _run_copies(list_ref, xs_ref, stage, sem.at[slot], to_hbm=True)

    def drain(s):
        pltpu.make_async_copy(stage_ref.at[s], stage_ref.at[s], sem.at[s]).wait()

    @pl.when(step > 0)
    def _():
        drain(1 - slot)

    @pl.when(step == pl.num_programs(0) - 1)
    def _():
        drain(slot)


def _dispatch(p_ends, counts, n_used, lpos_flat, lists, hp, n_rows):
    T = hp.shape[0] // ROW_TILE
    grid_spec = pltpu.PrefetchScalarGridSpec(
        num_scalar_prefetch=3,
        grid=(T // TD,),
        in_specs=[pl.BlockSpec((TD * TOP_K,), lambda i, *_: (i,), memory_space=pltpu.SMEM),
                  pl.BlockSpec((LIST_LEN,), lambda i, *_: (i,), memory_space=pltpu.SMEM),
                  pl.BlockSpec((TD * ROW_TILE, LANES), lambda i, *_: (i, 0))],
        out_specs=pl.BlockSpec(memory_space=pl.ANY),
        scratch_shapes=[pltpu.VMEM((FFN_BLK * ROW_TILE, LANES), f32),
                        pltpu.VMEM((2, TD * TOP_K * ROW_TILE, LANES), f32),
                        pltpu.SemaphoreType.DMA((2,)), pltpu.SemaphoreType.DMA(())],
    )
    return pl.pallas_call(
        _dispatch_kernel,
        out_shape=jax.ShapeDtypeStruct((n_rows * ROW_TILE, LANES), f32),
        grid_spec=grid_spec,
        compiler_params=pltpu.CompilerParams(dimension_semantics=("arbitrary",), vmem_limit_bytes=VMEM_LIMIT),
        name="dispatch",
    )(p_ends, counts, n_used, lpos_flat, lists, hp)


def _ffn_kernel(be_ref, nu_ref, nx_ref, par_ref, val_ref, xs_ref, wgu_hbm, bgu_ref, wd_hbm, bd_ref, ys_ref,
                wgu32_ref, wd32_ref, wgu_ref, wd_ref, sem):
    i = pl.program_id(0)
    used = i < nu_ref[0]
    new_expert = (i == 0) | (be_ref[i] != be_ref[jnp.maximum(i - 1, 0)])
    slot = par_ref[i]

    def weight_copies(e, s):
        return (pltpu.make_async_copy(wgu_hbm.at[e], wgu32_ref.at[s], sem.at[0, s]),
                pltpu.make_async_copy(wd_hbm.at[e], wd32_ref.at[s], sem.at[1, s]))

    @pl.when(i == 0)
    def _():
        for cp in weight_copies(be_ref[0], 0):
            cp.start()

    @pl.when(used & new_expert)
    def _():
        for cp in weight_copies(be_ref[i], slot):
            cp.wait()

        @pl.when(nx_ref[i] >= 0)
        def _():
            for cp in weight_copies(nx_ref[i], 1 - slot):
                cp.start()

        rows = 128

        def cast(src, dst):
            def body(r, c):
                r0 = pl.multiple_of(r * rows, rows)
                dst[pl.ds(r0, rows), :] = src[slot, pl.ds(r0, rows), :].astype(bf16)
                return c
            lax.fori_loop(0, src.shape[1] // rows, body, 0)
        cast(wgu32_ref, wgu_ref)
        cast(wd32_ref, wd_ref)

    def ffn_rows(n):
        tiles = pl.ds(0, n * ROW_TILE)
        F = wd_ref.shape[0]
        xrow = _tiles_to_rows(xs_ref.at[tiles], n).astype(bf16)
        acc = None
        fc = F // 2
        for c in range(2):
            def gu(col0):
                return _mm(xrow, wgu_ref[:, col0:col0 + fc]) + bgu_ref[0, :, col0:col0 + fc]
            gate = jnp.minimum(gu(c * fc), SWIGLU_LIMIT)
            up = jnp.clip(gu(F + c * fc), -SWIGLU_LIMIT, SWIGLU_LIMIT)
            y = (up + 1.0) * (gate * jax.nn.sigmoid(SWIGLU_ALPHA * gate))
            part = _mm(y.astype(bf16), wd_ref[c * fc:(c + 1) * fc, :])
            acc = part if acc is None else acc + part
        _rows_to_tiles(acc + bd_ref[0], ys_ref.at[tiles])

    half = FFN_BLK // 2
    occupied = val_ref[i]

    @pl.when(used & (occupied > half))
    def _():
        ffn_rows(FFN_BLK)

    @pl.when(used & (occupied <= half))
    def _():
        ffn_rows(half)
        ys_ref[pl.ds(half * ROW_TILE, half * ROW_TILE), :] = jnp.zeros((half * ROW_TILE, LANES), f32)

    @pl.when(jnp.logical_not(used))
    def _():
        ys_ref[...] = jnp.zeros_like(ys_ref)


def _ffn(block_e, n_used, occupied, xs, w_gate_up, b_gate_up, w_down, b_down):
    E, D, F2 = w_gate_up.shape
    F = F2 // 2
    P = xs.shape[0] // ROW_TILE
    nb = P // FFN_BLK
    rows = FFN_BLK * ROW_TILE

    idx = jnp.arange(nb, dtype=jnp.int32)
    live = idx < n_used[0]
    later_other = (block_e[None, :] != block_e[:, None]) & (idx[None, :] > idx[:, None]) & live[None, :]
    nxt = jnp.where(jnp.any(later_other, axis=1), block_e[jnp.argmax(later_other, axis=1)], -1).astype(jnp.int32)
    starts = jnp.concatenate([jnp.ones((1,), jnp.int32), (block_e[1:] != block_e[:-1]).astype(jnp.int32)])
    parity = ((jnp.cumsum(starts) - 1) % 2).astype(jnp.int32)

    def blk(i, nu):
        return jnp.minimum(i, nu[0] - 1)

    grid_spec = pltpu.PrefetchScalarGridSpec(
        num_scalar_prefetch=5,
        grid=(nb,),
        in_specs=[pl.BlockSpec((rows, LANES), lambda i, be, nu, *_: (blk(i, nu), 0)),
                  pl.BlockSpec(memory_space=pl.ANY),
                  pl.BlockSpec((1, 1, F2), lambda i, be, nu, *_: (be[blk(i, nu)], 0, 0)),
                  pl.BlockSpec(memory_space=pl.ANY),
                  pl.BlockSpec((1, 1, D), lambda i, be, nu, *_: (be[blk(i, nu)], 0, 0))],
        out_specs=pl.BlockSpec((rows, LANES), lambda i, *_: (i, 0)),
        scratch_shapes=[pltpu.VMEM((2, D, F2), f32), pltpu.VMEM((2, F, D), f32),
                        pltpu.VMEM((D, F2), bf16), pltpu.VMEM((F, D), bf16),
                        pltpu.SemaphoreType.DMA((2, 2))],
    )
    return pl.pallas_call(
        _ffn_kernel,
        out_shape=jax.ShapeDtypeStruct((P * ROW_TILE, LANES), f32),
        grid_spec=grid_spec,
        compiler_params=pltpu.CompilerParams(
            dimension_semantics=("arbitrary",), vmem_limit_bytes=VMEM_LIMIT_FFN),
        name="ffn",
    )(block_e, n_used, nxt, parity, occupied, xs, w_gate_up, b_gate_up.reshape(E, 1, F2), w_down,
      b_down.reshape(E, 1, D))


def _combine_kernel(lpos_ref, lcur_ref, lnext_ref, ys_ref, x1_ref, rw_ref, mod_ref, o_ref,
                    stage_ref, acc_ref, wb_ref, sem):
    step = pl.program_id(0) * pl.num_programs(1) + pl.program_id(1)
    n_steps = pl.num_programs(0) * pl.num_programs(1)
    slot = step % 2

    def fetch(list_ref, s):
        _run_copies(list_ref, ys_ref, stage_ref.at[s], sem.at[s], to_hbm=False)

    @pl.when(step == 0)
    def _():
        fetch(lcur_ref, 0)

    @pl.when(step + 1 < n_steps)
    def _():
        fetch(lnext_ref, 1 - slot)

    pltpu.make_async_copy(stage_ref.at[slot], stage_ref.at[slot], sem.at[slot]).wait()

    rw = rw_ref[...]
    for k in range(TOP_K):
        wb_ref[k] = jnp.broadcast_to(rw[:, k:k + 1], (TD, LANES))

    def staged(r, k):
        p = lpos_ref[k * TD + r]
        return stage_ref[slot, pl.ds(pl.multiple_of(p, ROW_TILE), ROW_TILE), :]

    def token(r, c):
        acc = wb_ref[0, pl.ds(r, 1), :] * staged(r, 0)
        for k in range(1, TOP_K):
            acc = acc + wb_ref[k, pl.ds(r, 1), :] * staged(r, k)
        acc_ref[pl.ds(pl.multiple_of(r * ROW_TILE, ROW_TILE), ROW_TILE), :] = acc
        return c
    lax.fori_loop(0, TD, token, 0, unroll=DMA_UNROLL)
    o_ref[0] = x1_ref[0] + mod_ref[0, 5:6, :] * _tiles_to_rows(acc_ref, TD)


def _combine(lpos_flat, lists, ys, x1, rw, mod):
    B, S, D = x1.shape
    nj = S // TD
    n_steps = B * nj
    return pl.pallas_call(
        _combine_kernel,
        out_shape=jax.ShapeDtypeStruct((B, S, D), f32),
        grid=(B, nj),
        in_specs=[pl.BlockSpec((TD * TOP_K,), lambda b, j: (b * nj + j,), memory_space=pltpu.SMEM),
                  pl.BlockSpec((LIST_LEN,), lambda b, j: (b * nj + j,), memory_space=pltpu.SMEM),
                  pl.BlockSpec((LIST_LEN,), lambda b, j: (jnp.minimum(b * nj + j + 1, n_steps - 1),),
                               memory_space=pltpu.SMEM),
                  pl.BlockSpec(memory_space=pl.ANY),
                  pl.BlockSpec((1, TD, D), lambda b, j: (b, j, 0)),
                  pl.BlockSpec((TD, LANES), lambda b, j: (b * nj + j, 0)),
                  pl.BlockSpec((1, 6, D), lambda b, j: (b, 0, 0))],
        out_specs=pl.BlockSpec((1, TD, D), lambda b, j: (b, j, 0)),
        scratch_shapes=[pltpu.VMEM((2, TD * TOP_K * ROW_TILE, LANES), f32),
                        pltpu.VMEM((TD * ROW_TILE, LANES), f32), pltpu.VMEM((TOP_K, TD, LANES), f32),
                        pltpu.SemaphoreType.DMA((2,))],
        compiler_params=pltpu.CompilerParams(
            dimension_semantics=("arbitrary", "arbitrary"), vmem_limit_bytes=VMEM_LIMIT),
        name="combine",
    )(lpos_flat, lists, lists, ys, x1, rw, mod)


def _moe(hp, logits, x1, mod, w_gate_up, b_gate_up, w_down, b_down):
    T = logits.shape[0]
    E = w_gate_up.shape[0]
    lp, rw, cnt, snap = _route(logits)
    lpos = lp[:, :TOP_K, :].reshape(-1)
    counts = cnt[0, :E].astype(jnp.int32)
    padded = ((counts + FFN_BLK - 1) // FFN_BLK) * FFN_BLK
    p_ends = jnp.cumsum(padded)
    p_starts = p_ends - padded
    nb = -(-T * TOP_K // FFN_BLK) + E
    n_used = jnp.maximum(p_ends[-1:] // FFN_BLK, 1).astype(jnp.int32)
    blk_start = jnp.arange(nb, dtype=jnp.int32) * FFN_BLK
    block_e = jnp.minimum(jnp.sum(p_ends[None, :] <= blk_start[:, None], axis=1), E - 1).astype(jnp.int32)
    assert E == N_RUN
    base = snap.reshape(T // TD, LANES)[:, :E].astype(jnp.int32)
    run_n = jnp.concatenate([base[1:], counts[None, :]], axis=0) - base
    run_off = jnp.cumsum(run_n, axis=1) - run_n
    run_dst = p_starts[None, :].astype(jnp.int32) + base
    lists = _copy_lists(run_dst, run_n, run_off)
    xs = _dispatch(p_ends.astype(jnp.int32), counts, n_used, lpos, lists, hp, nb * FFN_BLK)
    mine = block_e[:, None] == jnp.arange(E, dtype=jnp.int32)[None, :]
    seg_end = jnp.sum(jnp.where(mine, (p_starts + counts)[None, :], 0), axis=1)
    occupied = jnp.clip(seg_end - blk_start, 0, FFN_BLK).astype(jnp.int32)
    ys = _ffn(block_e, n_used, occupied, xs, w_gate_up, b_gate_up, w_down, b_down)
    return _combine(lpos, lists, ys, x1, rw, mod)


def kernel(x, c, rel_bias_table, w_ada, b_ada, g_norm1, w_in, w_gk_up, b_gk_up, g_gla_out, g_qnorm, g_knorm, lambda_q1, lambda_k1, lambda_q2, lambda_k2, g_subln, w_out, g_norm2, w_router, b_router, w_gate_up, b_gate_up, w_down, b_down):
    B, S, D = x.shape
    depth = w_ada.shape[0]
    bias_tiles = _bias_tiles(rel_bias_table, S, min(TQ, S))
    for l in range(depth):
        lambda_init = 0.8 - 0.6 * math.exp(-0.3 * l)
        mod = _ada(c, w_ada[l], b_ada[l])
        qg, kg, gk, kgt, gkt, vg, rg, qd, kd, vd = _inproj(
            x, mod, g_norm1[l], w_in[l], w_gk_up[l], b_gk_up[l], g_qnorm[l], g_knorm[l])
        og = _gla(qg, kg, gk, kgt, gkt, vg, rg, g_gla_out[l])
        lamv = jnp.stack([lambda_q1[l], lambda_k1[l], lambda_q2[l], lambda_k2[l]]).astype(f32)
        od = lax.cond(_scores_bounded(rel_bias_table, g_qnorm[l], g_knorm[l]),
                      functools.partial(_attn, lambda_init=lambda_init, bounded=True),
                      functools.partial(_attn, lambda_init=lambda_init, bounded=False),
                      qd, kd, vd, bias_tiles, lamv, g_subln[l])
        x1, hp, logits = _outproj(og, od, x, mod, w_out[l], g_norm2[l], w_router[l], b_router[l])
        x = _moe(hp, logits, x1, mod, w_gate_up[l], b_gate_up[l], w_down[l], b_down[l])
    return x
```

```python
import functools
import math

import jax
import jax.numpy as jnp
from jax import lax
from jax.experimental import pallas as pl
from jax.experimental.pallas import tpu as pltpu

f32 = jnp.float32
bf16 = jnp.bfloat16

N_GLA_HEADS = 4
GLA_DK = 64
GLA_DV = 128
GLA_GATE_RANK = 16
GLA_GATE_NORM = 16.0
GLA_CHUNK = 64
N_DIFF_HEADS = 4
DIFF_DQK = 64
DIFF_DV = 128
NUM_BUCKETS = 32
MAX_DISTANCE = 128
TOP_K = 4
SWIGLU_LIMIT = 7.0
SWIGLU_ALPHA = 1.702
EPS = 1e-6

GLA_QK_W = N_GLA_HEADS * GLA_DK
GLA_V_W = N_GLA_HEADS * GLA_DV
DIFF_QK_W = N_DIFF_HEADS * 2 * DIFF_DQK
DIFF_V_W = N_DIFF_HEADS * DIFF_DV

LANES = 128
NEG = -1e30
LOG2E = math.log2(math.e)
SAFE_SCORE = 40.0
NORM_SLACK = 1.02
VMEM_LIMIT = 48 * 1024 * 1024
VMEM_LIMIT_FFN = 58 * 1024 * 1024
VMEM_LIMIT_ATTN = 58 * 1024 * 1024

TM_IN = 512
INPROJ_SUB = 2
TG_GLA = 1024
PAIR = 2 * GLA_CHUNK
GLA_UNROLL = 4
TQ = 512
ATTN_UNROLL = 4
ATTN_HEADS = 2
TR = 512
TD = 512
ROW_TILE = 8
DMA_UNROLL = 8
N_RUN = 32
RUN_SIZES = tuple(TD >> b for b in range(TD.bit_length()))
LIST_DST = len(RUN_SIZES) * N_RUN
LIST_CNT = 2 * LIST_DST
LIST_LEN = 1024
assert LIST_CNT + len(RUN_SIZES) <= LIST_LEN
FFN_BLK = 512


def _nt(a, b):
    return lax.dot_general(a, b, (((1,), (1,)), ((), ())), preferred_element_type=f32)


def _mm(a, b):
    return jnp.dot(a, b, preferred_element_type=f32)


def _split(x):
    hi = x.astype(bf16)
    lo = (x - hi.astype(f32)).astype(bf16)
    return hi, lo


def _silu(x):
    return x * jax.nn.sigmoid(x)


def _ada_kernel(c_ref, w_ref, b_ref, o_ref):
    c = c_ref[...]
    o_ref[...] = _mm(_silu(c).astype(bf16), w_ref[...].astype(bf16)) + b_ref[...]


def _ada(c, w_ada, b_ada):
    B, D = c.shape
    N = w_ada.shape[1]
    bp = ROW_TILE
    assert B <= bp
    cp = jnp.zeros((bp, D), f32).at[:B].set(c)
    tn = N // 4
    out = pl.pallas_call(
        _ada_kernel,
        out_shape=jax.ShapeDtypeStruct((bp, N), f32),
        grid=(N // tn,),
        in_specs=[pl.BlockSpec((bp, D), lambda j: (0, 0)),
                  pl.BlockSpec((D, tn), lambda j: (0, j)),
                  pl.BlockSpec((1, tn), lambda j: (0, j))],
        out_specs=pl.BlockSpec((bp, tn), lambda j: (0, j)),
        compiler_params=pltpu.CompilerParams(vmem_limit_bytes=VMEM_LIMIT),
        name="ada",
    )(cp, w_ada, b_ada.reshape(1, N))
    return out[:B].reshape(B, 6, D)


def _inproj_kernel(x_ref, mod_ref, g1_ref, w_ref, wkt_ref, wlo_ref, wup_ref, wupt_ref,
                   bup_ref, bupt_ref, gqk_ref, grp_ref, grpt_ref,
                   qg_ref, kg_ref, gk_ref, kgt_ref, gkt_ref, vg_ref, rg_ref,
                   qd_ref, kd_ref, vd_ref, wm_ref):
    @pl.when((pl.program_id(0) == 0) & (pl.program_id(1) == 0))
    def _():
        split = wm_ref.shape[1] // 2
        wm_ref[:, :split] = w_ref[:, :split]
        wm_ref[:, split:] = w_ref[:, split + GLA_GATE_RANK:]
    tm = x_ref.shape[1]
    sub = tm // INPROJ_SUB
    for t in range(INPROJ_SUB):
        _inproj_rows(slice(t * sub, (t + 1) * sub), x_ref, mod_ref, g1_ref, wm_ref, wkt_ref, wlo_ref, wup_ref,
                     wupt_ref, bup_ref, bupt_ref, gqk_ref, grp_ref, grpt_ref, qg_ref, kg_ref, gk_ref, kgt_ref,
                     gkt_ref, vg_ref, rg_ref, qd_ref, kd_ref, vd_ref)


def _inproj_rows(rows, x_ref, mod_ref, g1_ref, wm_ref, wkt_ref, wlo_ref, wup_ref, wupt_ref,
                 bup_ref, bupt_ref, gqk_ref, grp_ref, grpt_ref,
                 qg_ref, kg_ref, gk_ref, kgt_ref, gkt_ref, vg_ref, rg_ref, qd_ref, kd_ref, vd_ref):
    x = x_ref[0, rows, :]
    ms = jnp.mean(x * x, axis=-1, keepdims=True)
    y = x * lax.rsqrt(ms + EPS) * g1_ref[...]
    h = (y * (1.0 + mod_ref[0, 1:2, :]) + mod_ref[0, 0:1, :]).astype(bf16)

    def proj(a, b):
        return _mm(h, wm_ref[:, a:b])

    o = 0
    qg_ref[0, rows, :] = proj(o, o + GLA_QK_W); o += GLA_QK_W
    kg_ref[0, rows, :] = proj(o, o + GLA_QK_W); o += GLA_QK_W
    vg_ref[0, rows, :] = proj(o, o + GLA_V_W).astype(bf16); o += GLA_V_W
    rg_ref[0, rows, :] = proj(o, o + GLA_V_W); o += GLA_V_W
    qk = proj(o, o + 2 * DIFF_QK_W); o += 2 * DIFF_QK_W
    vd_ref[0, rows, :] = proj(o, o + DIFF_V_W).astype(bf16)

    slab0 = rows.start // PAIR
    kgt = _nt(wkt_ref[...], h)
    for j in range(kgt.shape[1] // PAIR):
        kgt_ref[0, slab0 + j] = kgt[:, j * PAIR:(j + 1) * PAIR]

    lo = _mm(h, wlo_ref[...]).astype(bf16)
    z = _mm(lo, wup_ref[...]) + bup_ref[...]
    gk_ref[0, rows, :] = (jnp.minimum(z, 0.0) - jnp.log1p(jnp.exp(-jnp.abs(z)))) * (1.0 / GLA_GATE_NORM)
    zt = _nt(wupt_ref[...], lo) + bupt_ref[...]
    gkt = (jnp.minimum(zt, 0.0) - jnp.log1p(jnp.exp(-jnp.abs(zt)))) * (1.0 / GLA_GATE_NORM)
    for j in range(gkt.shape[1] // PAIR):
        gkt_ref[0, slab0 + j] = gkt[:, j * PAIR:(j + 1) * PAIR]

    sq_hi, sq_lo = _split(qk * qk)
    gs = _mm(sq_hi, grp_ref[...]) + _mm(sq_lo, grp_ref[...])
    r = lax.rsqrt(gs * (1.0 / DIFF_DQK) + EPS)
    r_hi, r_lo = _split(r)
    rb = _mm(r_hi, grpt_ref[...]) + _mm(r_lo, grpt_ref[...])
    qkn = qk * rb * gqk_ref[...]
    qd_ref[0, rows, :] = qkn[:, :DIFF_QK_W].astype(bf16)
    kd_ref[0, rows, :] = qkn[:, DIFF_QK_W:].astype(bf16)


def _inproj(x, mod, g_norm1, w_in, w_gk_up, b_gk_up, g_qnorm, g_knorm):
    B, S, D = x.shape
    offs = [0]
    for w in (GLA_QK_W, GLA_QK_W, GLA_V_W, GLA_V_W, GLA_GATE_RANK, DIFF_QK_W, DIFF_QK_W, DIFF_V_W):
        offs.append(offs[-1] + w)
    assert offs[4] == offs[8] - offs[5]
    w_all = w_in.astype(bf16)
    w_kt = w_in[:, offs[1]:offs[2]].T.astype(bf16)
    w_lo = jnp.zeros((D, LANES), f32).at[:, :GLA_GATE_RANK].set(w_in[:, offs[4]:offs[5]]).astype(bf16)
    w_up = jnp.zeros((LANES, GLA_QK_W), f32).at[:GLA_GATE_RANK].set(w_gk_up).astype(bf16)
    w_upt = w_up.T
    b_up = b_gk_up.reshape(1, GLA_QK_W)
    b_upt = b_gk_up.reshape(GLA_QK_W, 1)
    n_grp = 2 * DIFF_QK_W // DIFF_DQK
    gqk = jnp.concatenate([jnp.tile(g_qnorm, n_grp // 2) * (DIFF_DQK ** -0.5 * LOG2E),
                           jnp.tile(g_knorm, n_grp // 2)]).reshape(1, 2 * DIFF_QK_W)
    grp = (jnp.arange(2 * DIFF_QK_W)[:, None] // DIFF_DQK == jnp.arange(LANES)[None, :]).astype(bf16)
    grpt = grp.T
    nw = 2 * offs[4]
    tm = TM_IN
    const = lambda shape: pl.BlockSpec(shape, lambda b, i: (0,) * len(shape))
    row = lambda w: pl.BlockSpec((1, tm, w), lambda b, i: (b, i, 0))
    colT = pl.BlockSpec((1, tm // PAIR, GLA_QK_W, PAIR), lambda b, i: (b, i, 0, 0))
    outs = pl.pallas_call(
        _inproj_kernel,
        out_shape=[jax.ShapeDtypeStruct((B, S, GLA_QK_W), f32),
                   jax.ShapeDtypeStruct((B, S, GLA_QK_W), f32),
                   jax.ShapeDtypeStruct((B, S, GLA_QK_W), f32),
                   jax.ShapeDtypeStruct((B, S // PAIR, GLA_QK_W, PAIR), f32),
                   jax.ShapeDtypeStruct((B, S // PAIR, GLA_QK_W, PAIR), f32),
                   jax.ShapeDtypeStruct((B, S, GLA_V_W), bf16),
                   jax.ShapeDtypeStruct((B, S, GLA_V_W), f32),
                   jax.ShapeDtypeStruct((B, S, DIFF_QK_W), bf16),
                   jax.ShapeDtypeStruct((B, S, DIFF_QK_W), bf16),
                   jax.ShapeDtypeStruct((B, S, DIFF_V_W), bf16)],
        grid=(B, S // tm),
        in_specs=[row(D),
                  pl.BlockSpec((1, 6, D), lambda b, i: (b, 0, 0)),
                  const((1, D)), const((D, w_in.shape[1])), const((GLA_QK_W, D)), const((D, LANES)),
                  const((LANES, GLA_QK_W)), const((GLA_QK_W, LANES)),
                  const((1, GLA_QK_W)), const((GLA_QK_W, 1)),
                  const((1, 2 * DIFF_QK_W)), const((2 * DIFF_QK_W, LANES)),
                  const((LANES, 2 * DIFF_QK_W))],
        out_specs=[row(GLA_QK_W), row(GLA_QK_W), row(GLA_QK_W), colT, colT,
                   row(GLA_V_W), row(GLA_V_W), row(DIFF_QK_W), row(DIFF_QK_W), row(DIFF_V_W)],
        scratch_shapes=[pltpu.VMEM((D, nw), bf16)],
        compiler_params=pltpu.CompilerParams(
            dimension_semantics=("arbitrary", "arbitrary"), vmem_limit_bytes=VMEM_LIMIT),
        name="inproj",
    )(x, mod, g_norm1.reshape(1, D), w_all, w_kt, w_lo, w_up, w_upt, b_up, b_upt, gqk, grp, grpt)
    return outs


def _gla_kernel(q_ref, k_ref, g_ref, kt_ref, gt_ref, v_ref, r_ref, gout_ref, tri_ref, trit_ref,
                o_ref, s_ref, *, n_pairs):
    H, DK, DV = N_GLA_HEADS, GLA_DK, GLA_DV

    @pl.when(pl.program_id(1) == 0)
    def _():
        s_ref[...] = jnp.zeros_like(s_ref)

    tri = tri_ref[...]
    trit = trit_ref[...]
    tri_b = tri > 0
    lane_head = lax.broadcasted_iota(jnp.int32, (1, H * DK), 1) // DK
    row_head = lax.broadcasted_iota(jnp.int32, (H * PAIR, 1), 0) // PAIR
    qmask = row_head == lane_head
    row_first = lax.broadcasted_iota(jnp.int32, (PAIR, 1), 0) < GLA_CHUNK
    row_first4 = (lax.broadcasted_iota(jnp.int32, (H * PAIR, 1), 0) % PAIR) < GLA_CHUNK
    lane_first = lax.broadcasted_iota(jnp.int32, (1, PAIR), 1) < GLA_CHUNK
    scale = DK ** -0.5
    gout = gout_ref[...]

    def pair(p, carry):
        r0 = pl.multiple_of(p * PAIR, PAIR)
        q = q_ref[0, pl.ds(r0, PAIR), :]
        k = k_ref[0, pl.ds(r0, PAIR), :]
        g = g_ref[0, pl.ds(r0, PAIR), :]
        kt = kt_ref[0, p]
        gt = gt_ref[0, p]
        v = v_ref[0, pl.ds(r0, PAIR), :]

        g_hi, g_lo = _split(g)
        gc = _mm(tri, g_hi) + _mm(tri, g_lo)
        gt_hi, gt_lo = _split(gt)
        gct = _mm(gt_hi, trit) + _mm(gt_lo, trit)
        g_last = jnp.where(row_first, gc[GLA_CHUNK - 1:GLA_CHUNK, :], gc[PAIR - 1:PAIR, :])
        gl0 = gct[:, GLA_CHUNK - 1:GLA_CHUNK]
        gl1 = gct[:, PAIR - 1:PAIR]
        g_last_t = jnp.where(lane_first, gl0, gl1)

        q_e = (q * (jnp.exp(gc) * scale)).astype(bf16)
        k_e = (k * jnp.exp(-gc)).astype(bf16)
        ks_t = kt * jnp.exp(g_last_t - gct)
        ks_t0 = jnp.where(lane_first, ks_t, 0.0).astype(bf16)
        ks_t1 = jnp.where(lane_first, 0.0, ks_t).astype(bf16)
        del g_last

        qm = jnp.where(qmask, jnp.concatenate([q_e] * H, axis=0), jnp.zeros((), bf16))
        a = _nt(qm, k_e)
        s0 = s_ref[...]

        u0 = []
        u1 = []
        for h in range(H):
            v_h = v[:, h * DV:(h + 1) * DV]
            u0.append(_mm(ks_t0[h * DK:(h + 1) * DK], v_h))
            u1.append(_mm(ks_t1[h * DK:(h + 1) * DK], v_h))
        u0 = jnp.concatenate(u0, axis=0)
        u1 = jnp.concatenate(u1, axis=0)
        s1 = s0 * jnp.exp(gl0) + u0
        s_ref[...] = s1 * jnp.exp(gl1) + u1

        o_inter = jnp.where(row_first4, _mm(qm, s0.astype(bf16)), _mm(qm, s1.astype(bf16)))
        for h in range(H):
            a_h = jnp.where(tri_b, a[h * PAIR:(h + 1) * PAIR], 0.0).astype(bf16)
            o_h = _mm(a_h, v[:, h * DV:(h + 1) * DV]) + o_inter[h * PAIR:(h + 1) * PAIR]
            ms = jnp.mean(o_h * o_h, axis=-1, keepdims=True)
            o_n = o_h * lax.rsqrt(ms + EPS) * gout
            r_h = r_ref[0, pl.ds(r0, PAIR), h * DV:(h + 1) * DV]
            o_ref[0, pl.ds(r0, PAIR), h * DV:(h + 1) * DV] = (o_n * _silu(r_h)).astype(bf16)
        return carry

    lax.fori_loop(0, n_pairs, pair, 0, unroll=GLA_UNROLL)


def _gla(qg, kg, gk, kgt, gkt, vg, rg, g_gla_out):
    B, S, _ = qg.shape
    tg = min(TG_GLA, S)
    r = jnp.arange(PAIR)
    tri = ((r[:, None] // GLA_CHUNK == r[None, :] // GLA_CHUNK) & (r[None, :] <= r[:, None])).astype(bf16)
    row = lambda w: pl.BlockSpec((1, tg, w), lambda b, i: (b, i, 0))
    colT = pl.BlockSpec((1, tg // PAIR, GLA_QK_W, PAIR), lambda b, i: (b, i, 0, 0))
    const = lambda shape: pl.BlockSpec(shape, lambda b, i: (0,) * len(shape))
    return pl.pallas_call(
        functools.partial(_gla_kernel, n_pairs=tg // PAIR),
        out_shape=jax.ShapeDtypeStruct((B, S, GLA_V_W), bf16),
        grid=(B, S // tg),
        in_specs=[row(GLA_QK_W), row(GLA_QK_W), row(GLA_QK_W), colT, colT,
                  row(GLA_V_W), row(GLA_V_W), const((1, GLA_DV)),
                  const((PAIR, PAIR)), const((PAIR, PAIR))],
        out_specs=row(GLA_V_W),
        scratch_shapes=[pltpu.VMEM((GLA_QK_W, GLA_DV), f32)],
        compiler_params=pltpu.CompilerParams(
            dimension_semantics=("arbitrary", "arbitrary"), vmem_limit_bytes=VMEM_LIMIT),
        name="gla",
    )(qg, kg, gk, kgt, gkt, vg, rg, g_gla_out.reshape(1, GLA_DV), tri, tri.T)


def _attn_finish(o, gsub_ref, o_ref, lambda_init):
    ms = jnp.mean(o * o, axis=-1, keepdims=True)
    o_ref[0] = (o * lax.rsqrt(ms + EPS) * gsub_ref[...] * (1.0 - lambda_init)).astype(bf16)


def _attn_lambda(lamv_ref, lambda_init):
    lv = lamv_ref[...]
    return (jnp.exp(jnp.sum(lv[0:1] * lv[1:2], axis=-1, keepdims=True))
            - jnp.exp(jnp.sum(lv[2:3] * lv[3:4], axis=-1, keepdims=True)) + lambda_init)


def _attn_bounded_kernel(q_ref, k_ref, v_ref, bias_ref, lamv_ref, gsub_ref, o_ref, vaug_ref, *, lambda_init):
    qi = pl.program_id(2)
    tq = q_ref.shape[1]
    S = k_ref.shape[1]
    n_head = q_ref.shape[2] // (2 * DIFF_DQK)

    @pl.when(qi == 0)
    def _():
        lane = lax.broadcasted_iota(jnp.int32, (S, DIFF_DV), 1)
        for h in range(n_head):
            vaug_ref[h, :, :DIFF_DV] = v_ref[0, :, h * DIFF_DV:(h + 1) * DIFF_DV]
            vaug_ref[h, :, DIFF_DV:] = jnp.where(lane == 0, 1.0, 0.0).astype(bf16)

    lane = lax.broadcasted_iota(jnp.int32, (1, 2 * DIFF_DQK), 1)
    zero = jnp.zeros((), bf16)
    qs = []
    for h in range(n_head):
        q = q_ref[0, :, h * 2 * DIFF_DQK:(h + 1) * 2 * DIFF_DQK]
        qs.append((jnp.where(lane < DIFF_DQK, q, zero), jnp.where(lane < DIFF_DQK, zero, q)))

    def update(accs, k0, bias):
        out = []
        for h in range(n_head):
            kb = k_ref[0, pl.ds(k0, tq), h * 2 * DIFF_DQK:(h + 1) * 2 * DIFF_DQK]
            vb = vaug_ref[h, pl.ds(k0, tq), :]
            for c in range(2):
                s = _nt(qs[h][c], kb)
                if bias is not None:
                    s = s + bias[h][c]
                out.append(accs[2 * h + c] + _mm(jnp.exp2(s).astype(bf16), vb))
        return tuple(out)

    def far(kj, accs):
        return update(accs, pl.multiple_of(kj * tq, tq), None)

    def far_group(g, accs):
        for u in range(ATTN_UNROLL):
            accs = far(g * ATTN_UNROLL + u, accs)
        return accs

    def block_or_masked(accs, kj, rel):
        exists = kj >= 0
        k0 = pl.multiple_of(jnp.maximum(kj, 0) * tq, tq)
        if rel is None:
            tiles = [(jnp.where(exists, 0.0, NEG),) * 2] * n_head
        else:
            tiles = [tuple(jnp.where(exists, bias_ref[h, c, rel], NEG) for c in range(2)) for h in range(n_head)]
        return update(accs, k0, tiles)

    accs = tuple(jnp.zeros((tq, 2 * DIFF_DV), f32) for _ in range(2 * n_head))
    accs = block_or_masked(accs, qi, 1)
    accs = block_or_masked(accs, qi - 1, 0)
    for u in range(2, ATTN_UNROLL):
        accs = block_or_masked(accs, qi - u, None)
    n_far = jnp.maximum(qi + 1 - ATTN_UNROLL, 0)
    n_grp = n_far // ATTN_UNROLL
    accs = lax.fori_loop(0, n_grp, far_group, accs)
    accs = lax.fori_loop(n_grp * ATTN_UNROLL, n_far, far, accs)
    lam = _attn_lambda(lamv_ref, lambda_init)
    for h in range(n_head):
        a0, a1 = accs[2 * h], accs[2 * h + 1]
        o = a0[:, :DIFF_DV] / a0[:, DIFF_DV:DIFF_DV + 1] - lam * (a1[:, :DIFF_DV] / a1[:, DIFF_DV:DIFF_DV + 1])
        ms = jnp.mean(o * o, axis=-1, keepdims=True)
        o_ref[0, :, h * DIFF_DV:(h + 1) * DIFF_DV] = (
            o * lax.rsqrt(ms + EPS) * gsub_ref[...] * (1.0 - lambda_init)).astype(bf16)


def _attn_kernel(q_ref, k_ref, v_ref, bias_ref, lamv_ref, gsub_ref, o_ref, *, lambda_init):
    qi = pl.program_id(2)
    tq = q_ref.shape[1]
    q = q_ref[0]
    lane = lax.broadcasted_iota(jnp.int32, (1, 2 * DIFF_DQK), 1)
    zero = jnp.zeros((), bf16)
    qs = (jnp.where(lane < DIFF_DQK, q, zero), jnp.where(lane < DIFF_DQK, zero, q))

    def update(state, kb, vb, bias):
        new = []
        for c in range(2):
            m, l, acc = state[c]
            s = _nt(qs[c], kb)
            if bias is not None:
                s = s + bias[c]
            m_new = jnp.maximum(m, jnp.max(s, axis=-1, keepdims=True))
            alpha = jnp.exp2(m - m_new)
            p = jnp.exp2(s - m_new)
            l = alpha * l + jnp.sum(p, axis=-1, keepdims=True)
            acc = alpha * acc + _mm(p.astype(bf16), vb)
            new.append((m_new, l, acc))
        return tuple(new)

    init = tuple((jnp.full((tq, 1), NEG, f32), jnp.zeros((tq, 1), f32), jnp.zeros((tq, DIFF_DV), f32))
                 for _ in range(2))

    def far(kj, state):
        k0 = pl.multiple_of(kj * tq, tq)
        return update(state, k_ref[0, pl.ds(k0, tq), :], v_ref[0, pl.ds(k0, tq), :], None)

    state = lax.fori_loop(0, jnp.maximum(qi - 1, 0), far, init)

    kd0 = pl.multiple_of(qi * tq, tq)
    state = update(state, k_ref[0, pl.ds(kd0, tq), :], v_ref[0, pl.ds(kd0, tq), :],
                   (bias_ref[0, 0, 1], bias_ref[0, 1, 1]))
    kp0 = pl.multiple_of(jnp.maximum(qi - 1, 0) * tq, tq)
    has_prev = qi > 0
    state = update(state, k_ref[0, pl.ds(kp0, tq), :], v_ref[0, pl.ds(kp0, tq), :],
                   (jnp.where(has_prev, bias_ref[0, 0, 0], NEG), jnp.where(has_prev, bias_ref[0, 1, 0], NEG)))

    (_, l0, a0), (_, l1, a1) = state
    o = a0 / l0 - _attn_lambda(lamv_ref, lambda_init) * (a1 / l1)
    _attn_finish(o, gsub_ref, o_ref, lambda_init)


def _t5_bucket(n):
    max_exact = NUM_BUCKETS // 2
    nf = jnp.maximum(n, 1).astype(f32)
    large = max_exact + (jnp.log(nf / max_exact) / math.log(MAX_DISTANCE / max_exact)
                         * (NUM_BUCKETS - max_exact)).astype(jnp.int32)
    large = jnp.minimum(large, NUM_BUCKETS - 1)
    return jnp.where(n < max_exact, n, large)


def _toeplitz_kernel(w_ref, o_ref):
    n = o_ref.shape[-1]
    for t in range(2):
        rows = jnp.broadcast_to(w_ref[0, t:t + 1, :], (n, 2 * n))
        o_ref[0, 0, t] = pltpu.roll(rows, 0, 1, stride=1, stride_axis=0)[:, n:]


def _bias_tiles(rel_bias_table, S, n):
    HM = rel_bias_table.shape[1]
    assert n >= MAX_DISTANCE
    d = jnp.arange(2 * n, dtype=jnp.int32)
    in_bucket = _t5_bucket(d)[None, :] == jnp.arange(NUM_BUCKETS, dtype=jnp.int32)[:, None]
    by_dist = jnp.sum(jnp.where(in_bucket[:, None, :], rel_bias_table.astype(f32)[:, :, None], 0.0), axis=0)
    rel = (by_dist - rel_bias_table[NUM_BUCKETS - 1].astype(f32)[:, None]) * LOG2E
    w_diag = jnp.concatenate([rel[:, n::-1], jnp.full((HM, n - 1), NEG, f32)], axis=1)
    w_prev = jnp.concatenate([rel[:, :1], rel[:, :0:-1]], axis=1)
    w = jnp.stack([w_prev, w_diag], axis=1)
    return pl.pallas_call(
        _toeplitz_kernel,
        out_shape=jax.ShapeDtypeStruct((HM // 2, 2, 2, n, n), f32),
        grid=(HM // 2, 2),
        in_specs=[pl.BlockSpec((1, 2, 2 * n), lambda h, m: (h * 2 + m, 0, 0))],
        out_specs=pl.BlockSpec((1, 1, 2, n, n), lambda h, m: (h, m, 0, 0, 0)),
        compiler_params=pltpu.CompilerParams(vmem_limit_bytes=VMEM_LIMIT),
        name="bias_tiles",
    )(w)


def _attn(qd, kd, vd, bias_tiles, lamv, g_subln, lambda_init, bounded):
    B, S, _ = qd.shape
    H = N_DIFF_HEADS
    tq = min(TQ, S)
    body = _attn_bounded_kernel if bounded else _attn_kernel
    hs = ATTN_HEADS if bounded else 1
    scratch = [pltpu.VMEM((hs, S, 2 * DIFF_DV), bf16)] if bounded else []
    return pl.pallas_call(
        functools.partial(body, lambda_init=lambda_init),
        out_shape=jax.ShapeDtypeStruct((B, S, DIFF_V_W), bf16),
        scratch_shapes=scratch,
        grid=(B, H // hs, S // tq),
        in_specs=[pl.BlockSpec((1, tq, hs * 2 * DIFF_DQK), lambda b, h, i: (b, i, h)),
                  pl.BlockSpec((1, S, hs * 2 * DIFF_DQK), lambda b, h, i: (b, 0, h)),
                  pl.BlockSpec((1, S, hs * DIFF_DV), lambda b, h, i: (b, 0, h)),
                  pl.BlockSpec((hs, 2, 2, tq, tq), lambda b, h, i: (h, 0, 0, 0, 0)),
                  pl.BlockSpec((4, DIFF_DQK), lambda b, h, i: (0, 0)),
                  pl.BlockSpec((1, DIFF_DV), lambda b, h, i: (0, 0))],
        out_specs=pl.BlockSpec((1, tq, hs * DIFF_DV), lambda b, h, i: (b, i, h)),
        compiler_params=pltpu.CompilerParams(
            dimension_semantics=("arbitrary", "arbitrary", "arbitrary"),
            vmem_limit_bytes=VMEM_LIMIT_ATTN if bounded else VMEM_LIMIT),
        name="attn_bounded" if bounded else "attn_online",
    )(qd, kd, vd, bias_tiles, lamv, g_subln.reshape(1, DIFF_DV))


def _scores_bounded(rel_bias_table, g_qnorm, g_knorm):
    qk = DIFF_DQK ** 0.5 * jnp.max(jnp.abs(g_qnorm)) * jnp.max(jnp.abs(g_knorm)) * NORM_SLACK
    rel = jnp.max(jnp.abs(rel_bias_table - rel_bias_table[NUM_BUCKETS - 1:]))
    return qk + rel <= SAFE_SCORE


def _rows_to_tiles(x, ref):
    n = x.shape[0]
    for c in range(ROW_TILE):
        ref[pl.ds(c, n, stride=ROW_TILE), :] = x[:, c * LANES:(c + 1) * LANES]


def _tiles_to_rows(ref, n):
    return jnp.concatenate([ref[pl.ds(c, n, stride=ROW_TILE), :] for c in range(ROW_TILE)], axis=1)


def _outproj_kernel(og_ref, od_ref, x_ref, mod_ref, wo_ref, g2_ref, wr_ref, br_ref, lt_ref, ut_ref, pk_ref,
                    x1_ref, hp_ref, lp_ref, rw_ref, cnt_ref, snap_ref, lg_ref, run_ref):
    half = og_ref.shape[2]
    sub = og_ref.shape[1] // INPROJ_SUB
    for t in range(INPROJ_SUB):
        rows = slice(t * sub, (t + 1) * sub)
        mix = _mm(og_ref[0, rows, :], wo_ref[:half, :]) + _mm(od_ref[0, rows, :], wo_ref[half:, :])
        x1 = x_ref[0, rows, :] + mod_ref[0, 2:3, :] * mix
        x1_ref[0, rows, :] = x1
        ms = jnp.mean(x1 * x1, axis=-1, keepdims=True)
        y = x1 * lax.rsqrt(ms + EPS) * g2_ref[...]
        h = (y * (1.0 + mod_ref[0, 4:5, :]) + mod_ref[0, 3:4, :]).astype(bf16)
        lg_ref[rows, :] = _mm(h, wr_ref[...]) + br_ref[...]
        _rows_to_tiles(h.astype(f32), hp_ref.at[pl.ds(t * sub * ROW_TILE, sub * ROW_TILE)])
    first = (pl.program_id(0) == 0) & (pl.program_id(1) == 0)
    _route_tile(first, lg_ref, lt_ref, ut_ref, pk_ref, lp_ref, rw_ref, cnt_ref, snap_ref, run_ref)


def _outproj(og, od, x, mod, w_out, g_norm2, w_router, b_router):
    B, S, D = x.shape
    assert D == ROW_TILE * LANES, "the token-tile layout needs a model row to fill one (8,128) tile"
    E = w_router.shape[1]
    tm = TM_IN
    assert tm == TD, "projection, routing and dispatch share one token tile"
    nj = S // tm
    n_tiles = B * nj
    w_r = jnp.zeros((D, LANES), f32).at[:, :E].set(w_router).astype(bf16)
    b_r = jnp.full((1, LANES), NEG, f32).at[0, :E].set(b_router)
    r = jnp.arange(tm)
    lt = (r[None, :] < r[:, None]).astype(bf16)
    e = jnp.arange(LANES)
    ut = (e[:, None] < e[None, :]).astype(bf16)
    pick = (jnp.arange(ROW_TILE)[:, None] == e[None, :]).astype(bf16)
    const = lambda shape: pl.BlockSpec(shape, lambda b, i: (0,) * len(shape))
    return pl.pallas_call(
        _outproj_kernel,
        out_shape=[jax.ShapeDtypeStruct((B, S, D), f32),
                   jax.ShapeDtypeStruct((B * S * ROW_TILE, LANES), f32),
                   jax.ShapeDtypeStruct((n_tiles, ROW_TILE, tm), jnp.int32),
                   jax.ShapeDtypeStruct((B * S, LANES), f32),
                   jax.ShapeDtypeStruct((1, LANES), f32),
                   jax.ShapeDtypeStruct((n_tiles, 1, LANES), f32)],
        grid=(B, nj),
        in_specs=[pl.BlockSpec((1, tm, og.shape[2]), lambda b, i: (b, i, 0)),
                  pl.BlockSpec((1, tm, od.shape[2]), lambda b, i: (b, i, 0)),
                  pl.BlockSpec((1, tm, D), lambda b, i: (b, i, 0)),
                  pl.BlockSpec((1, 6, D), lambda b, i: (b, 0, 0)),
                  const((w_out.shape[0], D)), const((1, D)), const((D, LANES)), const((1, LANES)),
                  const((tm, tm)), const((LANES, LANES)), const((ROW_TILE, LANES))],
        out_specs=[pl.BlockSpec((1, tm, D), lambda b, i: (b, i, 0)),
                   pl.BlockSpec((tm * ROW_TILE, LANES), lambda b, i: (b * nj + i, 0)),
                   pl.BlockSpec((1, ROW_TILE, tm), lambda b, i: (b * nj + i, 0, 0)),
                   pl.BlockSpec((tm, LANES), lambda b, i: (b * nj + i, 0)),
                   pl.BlockSpec((1, LANES), lambda b, i: (0, 0)),
                   pl.BlockSpec((1, 1, LANES), lambda b, i: (b * nj + i, 0, 0))],
        scratch_shapes=[pltpu.VMEM((tm, LANES), f32), pltpu.VMEM((1, LANES), f32)],
        compiler_params=pltpu.CompilerParams(
            dimension_semantics=("arbitrary", "arbitrary"), vmem_limit_bytes=VMEM_LIMIT),
        name="outproj",
    )(og, od, x, mod, w_out.astype(bf16), g_norm2.reshape(1, D), w_r, b_r, lt, ut, pick)


def _route_tile(first, lg_ref, lt_ref, ut_ref, pk_ref, lp_ref, rw_ref, cnt_ref, snap_ref, run_ref):
    @pl.when(first)
    def _():
        run_ref[...] = jnp.zeros_like(run_ref)

    x = lg_ref[...]
    tr = x.shape[0]
    lane = lax.broadcasted_iota(jnp.int32, (tr, LANES), 1)
    lane_f = lane.astype(f32)
    vals, hots = [], []
    for _ in range(TOP_K):
        m = jnp.max(x, axis=-1, keepdims=True)
        idx = jnp.min(jnp.where(x == m, lane_f, float(LANES)), axis=-1, keepdims=True)
        hot = lane_f == idx
        x = jnp.where(hot, -jnp.inf, x)
        vals.append(m)
        hots.append(hot)
    ex = [jnp.exp(v - vals[0]) for v in vals]
    den = ex[0] + ex[1] + ex[2] + ex[3]
    sel = (hots[0] | hots[1] | hots[2] | hots[3]).astype(f32)
    rank = _mm(lt_ref[...], sel.astype(bf16)) + run_ref[...]
    run_ref[...] = run_ref[...] + jnp.sum(sel, axis=0, keepdims=True)
    cnt_ref[...] = run_ref[...]
    base = rank[0:1, :]
    snap_ref[0] = base
    n_hi, n_lo = _split(jnp.sum(sel, axis=0, keepdims=True))
    start = _mm(n_hi, ut_ref[...]) + _mm(n_lo, ut_ref[...])
    place = rank - base + start
    pos = jnp.zeros((tr, LANES), f32)
    rw = jnp.zeros((tr, LANES), f32)
    for k in range(TOP_K):
        rk = jnp.sum(jnp.where(hots[k], place, 0.0), axis=-1, keepdims=True) * ROW_TILE
        pos = jnp.where(lane == k, rk, pos)
        rw = jnp.where(lane == k, ex[k] / den, rw)
    rw_ref[...] = rw
    p_hi, p_lo = _split(pos)
    lp_ref[0] = (_nt(pk_ref[...], p_hi) + _nt(pk_ref[...], p_lo)).astype(jnp.int32)


def _run_copies(list_ref, hbm_ref, stage_ref, sem, to_hbm):
    for c, size in enumerate(RUN_SIZES):
        def one(i, carry, c=c, size=size):
            s0 = list_ref[c * N_RUN + i]
            d0 = list_ref[LIST_DST + c * N_RUN + i]
            stage = stage_ref.at[pl.ds(pl.multiple_of(s0, ROW_TILE), size * ROW_TILE)]
            rows = hbm_ref.at[pl.ds(pl.multiple_of(d0, ROW_TILE), size * ROW_TILE)]
            src, dst = (stage, rows) if to_hbm else (rows, stage)
            pltpu.make_async_copy(src, dst, sem).start(priority=c % 2)
            return carry
        lax.fori_loop(0, list_ref[LIST_CNT + c], one, 0)


def _copy_lists(run_dst, run_n, run_off):
    n_tiles, n_exp = run_n.shape
    n_size = len(RUN_SIZES)
    size = jnp.asarray(RUN_SIZES, jnp.int32)[None, None, :]
    n = jnp.broadcast_to(run_n.T[:, :, None], (n_exp, n_tiles, n_size))
    bit = ((n & size) != 0).reshape(n_exp, -1)
    before = (n & ~(2 * size - 1)).reshape(n_exp, -1)
    place = jnp.cumsum(bit, axis=0) - 1
    pick = bit[None, :, :] & (place[None, :, :] == jnp.arange(N_RUN)[:, None, None])

    def compact(v):
        v = jnp.broadcast_to(v.T[:, :, None], (n_exp, n_tiles, n_size)).reshape(n_exp, -1) + before
        out = jnp.sum(jnp.where(pick, v[None, :, :], 0), axis=1) * ROW_TILE
        return out.reshape(N_RUN, n_tiles, n_size).transpose(1, 2, 0).reshape(n_tiles, -1)
    cnt = jnp.sum(bit, axis=0).reshape(n_tiles, n_size).astype(jnp.int32)
    pad = jnp.zeros((n_tiles, LIST_LEN - LIST_CNT - n_size), jnp.int32)
    lists = jnp.concatenate([compact(run_off), compact(run_dst), cnt, pad], axis=1)
    return lists.reshape(-1).astype(jnp.int32)


def _dispatch_kernel(pend_ref, cnt_ref, nu_ref, lpos_ref, list_ref, h_ref, xs_ref,
                     zero_ref, stage_ref, sem, zsem):
    n_tok = h_ref.shape[0] // ROW_TILE
    blk_rows = FFN_BLK * ROW_TILE

    @pl.when(pl.program_id(0) == 0)
    def _():
        zero_ref[...] = jnp.zeros_like(zero_ref)
        n_exp = pend_ref.shape[0]

        def last_block(e):
            return xs_ref.at[pl.ds(pl.multiple_of((pend_ref[e] - FFN_BLK) * ROW_TILE, blk_rows), blk_rows)]

        def zfill(e, c):
            @pl.when(cnt_ref[e] > 0)
            def _():
                pltpu.make_async_copy(zero_ref, last_block(e), zsem).start()
            return c

        def zwait(e, c):
            @pl.when(cnt_ref[e] > 0)
            def _():
                pltpu.make_async_copy(zero_ref, last_block(e), zsem).wait()
            return c

        lax.fori_loop(0, n_exp, zfill, 0)
        lax.fori_loop(0, n_exp, zwait, 0)

        def tail_block(i):
            return xs_ref.at[pl.ds(pl.multiple_of(i * blk_rows, blk_rows), blk_rows)]

        def tfill(i, c):
            pltpu.make_async_copy(zero_ref, tail_block(i), zsem).start()
            return c

        def twait(i, c):
            pltpu.make_async_copy(zero_ref, tail_block(i), zsem).wait()
            return c

        n_blk = xs_ref.shape[0] // blk_rows
        lax.fori_loop(nu_ref[0], n_blk, tfill, 0)
        lax.fori_loop(nu_ref[0], n_blk, twait, 0)

    step = pl.program_id(0)
    slot = step % 2
    stage = stage_ref.at[slot]

    def place(g, c):
        for u in range(DMA_UNROLL):
            r = g * DMA_UNROLL + u
            row = h_ref[pl.ds(pl.multiple_of(r * ROW_TILE, ROW_TILE), ROW_TILE), :]
            for k in range(TOP_K):
                p = lpos_ref[k * n_tok + r]
                stage[pl.ds(pl.multiple_of(p, ROW_TILE), ROW_TILE), :] = row
        return c
    lax.fori_loop(0, n_tok // DMA_UNROLL, place, 0)

    _run_copies(list_ref, xs_ref, stage, sem.at[slot], to_hbm=True)

    def drain(s):
        pltpu.make_async_copy(stage_ref.at[s], stage_ref.at[s], sem.at[s]).wait()

    @pl.when(step > 0)
    def _():
        drain(1 - slot)

    @pl.when(step == pl.num_programs(0) - 1)
    def _():
        drain(slot)


def _dispatch(p_ends, counts, n_used, lpos_flat, lists, hp, n_rows):
    T = hp.shape[0] // ROW_TILE
    grid_spec = pltpu.PrefetchScalarGridSpec(
        num_scalar_prefetch=3,
        grid=(T // TD,),
        in_specs=[pl.BlockSpec((TD * TOP_K,), lambda i, *_: (i,), memory_space=pltpu.SMEM),
                  pl.BlockSpec((LIST_LEN,), lambda i, *_: (i,), memory_space=pltpu.SMEM),
                  pl.BlockSpec((TD * ROW_TILE, LANES), lambda i, *_: (i, 0))],
        out_specs=pl.BlockSpec(memory_space=pl.ANY),
        scratch_shapes=[pltpu.VMEM((FFN_BLK * ROW_TILE, LANES), f32),
                        pltpu.VMEM((2, TD * TOP_K * ROW_TILE, LANES), f32),
                        pltpu.SemaphoreType.DMA((2,)), pltpu.SemaphoreType.DMA(())],
    )
    return pl.pallas_call(
        _dispatch_kernel,
        out_shape=jax.ShapeDtypeStruct((n_rows * ROW_TILE, LANES), f32),
        grid_spec=grid_spec,
        compiler_params=pltpu.CompilerParams(dimension_semantics=("arbitrary",), vmem_limit_bytes=VMEM_LIMIT),
        name="dispatch",
    )(p_ends, counts, n_used, lpos_flat, lists, hp)


def _ffn_kernel(be_ref, nu_ref, nx_ref, par_ref, val_ref, xs_ref, wgu_hbm, bgu_ref, wd_hbm, bd_ref, ys_ref,
                wgu32_ref, wd32_ref, wgu_ref, wd_ref, sem):
    i = pl.program_id(0)
    used = i < nu_ref[0]
    new_expert = (i == 0) | (be_ref[i] != be_ref[jnp.maximum(i - 1, 0)])
    slot = par_ref[i]

    def weight_copies(e, s):
        return (pltpu.make_async_copy(wgu_hbm.at[e], wgu32_ref.at[s], sem.at[0, s]),
                pltpu.make_async_copy(wd_hbm.at[e], wd32_ref.at[s], sem.at[1, s]))

    @pl.when(i == 0)
    def _():
        for cp in weight_copies(be_ref[0], 0):
            cp.start()

    @pl.when(used & new_expert)
    def _():
        for cp in weight_copies(be_ref[i], slot):
            cp.wait()

        @pl.when(nx_ref[i] >= 0)
        def _():
            for cp in weight_copies(nx_ref[i], 1 - slot):
                cp.start()

        rows = 128

        def cast(src, dst):
            def body(r, c):
                r0 = pl.multiple_of(r * rows, rows)
                dst[pl.ds(r0, rows), :] = src[slot, pl.ds(r0, rows), :].astype(bf16)
                return c
            lax.fori_loop(0, src.shape[1] // rows, body, 0)
        cast(wgu32_ref, wgu_ref)
        cast(wd32_ref, wd_ref)

    def ffn_rows(n):
        tiles = pl.ds(0, n * ROW_TILE)
        F = wd_ref.shape[0]
        xrow = _tiles_to_rows(xs_ref.at[tiles], n).astype(bf16)
        acc = None
        fc = F // 2
        for c in range(2):
            def gu(col0):
                return _mm(xrow, wgu_ref[:, col0:col0 + fc]) + bgu_ref[0, :, col0:col0 + fc]
            gate = jnp.minimum(gu(c * fc), SWIGLU_LIMIT)
            up = jnp.clip(gu(F + c * fc), -SWIGLU_LIMIT, SWIGLU_LIMIT)
            y = (up + 1.0) * (gate * jax.nn.sigmoid(SWIGLU_ALPHA * gate))
            part = _mm(y.astype(bf16), wd_ref[c * fc:(c + 1) * fc, :])
            acc = part if acc is None else acc + part
        _rows_to_tiles(acc + bd_ref[0], ys_ref.at[tiles])

    half = FFN_BLK // 2
    occupied = val_ref[i]

    @pl.when(used & (occupied > half))
    def _():
        ffn_rows(FFN_BLK)

    @pl.when(used & (occupied <= half))
    def _():
        ffn_rows(half)
        ys_ref[pl.ds(half * ROW_TILE, half * ROW_TILE), :] = jnp.zeros((half * ROW_TILE, LANES), f32)

    @pl.when(jnp.logical_not(used))
    def _():
        ys_ref[...] = jnp.zeros_like(ys_ref)


def _ffn(block_e, n_used, occupied, xs, w_gate_up, b_gate_up, w_down, b_down):
    E, D, F2 = w_gate_up.shape
    F = F2 // 2
    P = xs.shape[0] // ROW_TILE
    nb = P // FFN_BLK
    rows = FFN_BLK * ROW_TILE

    idx = jnp.arange(nb, dtype=jnp.int32)
    live = idx < n_used[0]
    later_other = (block_e[None, :] != block_e[:, None]) & (idx[None, :] > idx[:, None]) & live[None, :]
    nxt = jnp.where(jnp.any(later_other, axis=1), block_e[jnp.argmax(later_other, axis=1)], -1).astype(jnp.int32)
    starts = jnp.concatenate([jnp.ones((1,), jnp.int32), (block_e[1:] != block_e[:-1]).astype(jnp.int32)])
    parity = ((jnp.cumsum(starts) - 1) % 2).astype(jnp.int32)

    def blk(i, nu):
        return jnp.minimum(i, nu[0] - 1)

    grid_spec = pltpu.PrefetchScalarGridSpec(
        num_scalar_prefetch=5,
        grid=(nb,),
        in_specs=[pl.BlockSpec((rows, LANES), lambda i, be, nu, *_: (blk(i, nu), 0)),
                  pl.BlockSpec(memory_space=pl.ANY),
                  pl.BlockSpec((1, 1, F2), lambda i, be, nu, *_: (be[blk(i, nu)], 0, 0)),
                  pl.BlockSpec(memory_space=pl.ANY),
                  pl.BlockSpec((1, 1, D), lambda i, be, nu, *_: (be[blk(i, nu)], 0, 0))],
        out_specs=pl.BlockSpec((rows, LANES), lambda i, *_: (i, 0)),
        scratch_shapes=[pltpu.VMEM((2, D, F2), f32), pltpu.VMEM((2, F, D), f32),
                        pltpu.VMEM((D, F2), bf16), pltpu.VMEM((F, D), bf16),
                        pltpu.SemaphoreType.DMA((2, 2))],
    )
    return pl.pallas_call(
        _ffn_kernel,
        out_shape=jax.ShapeDtypeStruct((P * ROW_TILE, LANES), f32),
        grid_spec=grid_spec,
        compiler_params=pltpu.CompilerParams(
            dimension_semantics=("arbitrary",), vmem_limit_bytes=VMEM_LIMIT_FFN),
        name="ffn",
    )(block_e, n_used, nxt, parity, occupied, xs, w_gate_up, b_gate_up.reshape(E, 1, F2), w_down,
      b_down.reshape(E, 1, D))


def _combine_kernel(lpos_ref, lcur_ref, lnext_ref, ys_ref, x1_ref, rw_ref, mod_ref, o_ref,
                    stage_ref, acc_ref, wb_ref, sem):
    step = pl.program_id(0) * pl.num_programs(1) + pl.program_id(1)
    n_steps = pl.num_programs(0) * pl.num_programs(1)
    slot = step % 2

    def fetch(list_ref, s):
        _run_copies(list_ref, ys_ref, stage_ref.at[s], sem.at[s], to_hbm=False)

    @pl.when(step == 0)
    def _():
        fetch(lcur_ref, 0)

    @pl.when(step + 1 < n_steps)
    def _():
        fetch(lnext_ref, 1 - slot)

    pltpu.make_async_copy(stage_ref.at[slot], stage_ref.at[slot], sem.at[slot]).wait()

    rw = rw_ref[...]
    for k in range(TOP_K):
        wb_ref[k] = jnp.broadcast_to(rw[:, k:k + 1], (TD, LANES))

    def staged(r, k):
        p = lpos_ref[k * TD + r]
        return stage_ref[slot, pl.ds(pl.multiple_of(p, ROW_TILE), ROW_TILE), :]

    def token(r, c):
        acc = wb_ref[0, pl.ds(r, 1), :] * staged(r, 0)
        for k in range(1, TOP_K):
            acc = acc + wb_ref[k, pl.ds(r, 1), :] * staged(r, k)
        acc_ref[pl.ds(pl.multiple_of(r * ROW_TILE, ROW_TILE), ROW_TILE), :] = acc
        return c
    lax.fori_loop(0, TD, token, 0, unroll=DMA_UNROLL)
    o_ref[0] = x1_ref[0] + mod_ref[0, 5:6, :] * _tiles_to_rows(acc_ref, TD)


def _combine(lpos_flat, lists, ys, x1, rw, mod):
    B, S, D = x1.shape
    nj = S // TD
    n_steps = B * nj
    return pl.pallas_call(
        _combine_kernel,
        out_shape=jax.ShapeDtypeStruct((B, S, D), f32),
        grid=(B, nj),
        in_specs=[pl.BlockSpec((TD * TOP_K,), lambda b, j: (b * nj + j,), memory_space=pltpu.SMEM),
                  pl.BlockSpec((LIST_LEN,), lambda b, j: (b * nj + j,), memory_space=pltpu.SMEM),
                  pl.BlockSpec((LIST_LEN,), lambda b, j: (jnp.minimum(b * nj + j + 1, n_steps - 1),),
                               memory_space=pltpu.SMEM),
                  pl.BlockSpec(memory_space=pl.ANY),
                  pl.BlockSpec((1, TD, D), lambda b, j: (b, j, 0)),
                  pl.BlockSpec((TD, LANES), lambda b, j: (b * nj + j, 0)),
                  pl.BlockSpec((1, 6, D), lambda b, j: (b, 0, 0))],
        out_specs=pl.BlockSpec((1, TD, D), lambda b, j: (b, j, 0)),
        scratch_shapes=[pltpu.VMEM((2, TD * TOP_K * ROW_TILE, LANES), f32),
                        pltpu.VMEM((TD * ROW_TILE, LANES), f32), pltpu.VMEM((TOP_K, TD, LANES), f32),
                        pltpu.SemaphoreType.DMA((2,))],
        compiler_params=pltpu.CompilerParams(
            dimension_semantics=("arbitrary", "arbitrary"), vmem_limit_bytes=VMEM_LIMIT),
        name="combine",
    )(lpos_flat, lists, lists, ys, x1, rw, mod)


def _moe(hp, lp, rw, cnt, snap, x1, mod, w_gate_up, b_gate_up, w_down, b_down):
    T = rw.shape[0]
    E = w_gate_up.shape[0]
    lpos = lp[:, :TOP_K, :].reshape(-1)
    counts = cnt[0, :E].astype(jnp.int32)
    padded = ((counts + FFN_BLK - 1) // FFN_BLK) * FFN_BLK
    p_ends = jnp.cumsum(padded)
    p_starts = p_ends - padded
    nb = -(-T * TOP_K // FFN_BLK) + E
    n_used = jnp.maximum(p_ends[-1:] // FFN_BLK, 1).astype(jnp.int32)
    blk_start = jnp.arange(nb, dtype=jnp.int32) * FFN_BLK
    block_e = jnp.minimum(jnp.sum(p_ends[None, :] <= blk_start[:, None], axis=1), E - 1).astype(jnp.int32)
    assert E == N_RUN
    base = snap.reshape(T // TD, LANES)[:, :E].astype(jnp.int32)
    run_n = jnp.concatenate([base[1:], counts[None, :]], axis=0) - base
    run_off = jnp.cumsum(run_n, axis=1) - run_n
    run_dst = p_starts[None, :].astype(jnp.int32) + base
    lists = _copy_lists(run_dst, run_n, run_off)
    xs = _dispatch(p_ends.astype(jnp.int32), counts, n_used, lpos, lists, hp, nb * FFN_BLK)
    mine = block_e[:, None] == jnp.arange(E, dtype=jnp.int32)[None, :]
    seg_end = jnp.sum(jnp.where(mine, (p_starts + counts)[None, :], 0), axis=1)
    occupied = jnp.clip(seg_end - blk_start, 0, FFN_BLK).astype(jnp.int32)
    ys = _ffn(block_e, n_used, occupied, xs, w_gate_up, b_gate_up, w_down, b_down)
    return _combine(lpos, lists, ys, x1, rw, mod)


def kernel(x, c, rel_bias_table, w_ada, b_ada, g_norm1, w_in, w_gk_up, b_gk_up, g_gla_out, g_qnorm, g_knorm, lambda_q1, lambda_k1, lambda_q2, lambda_k2, g_subln, w_out, g_norm2, w_router, b_router, w_gate_up, b_gate_up, w_down, b_down):
    B, S, D = x.shape
    depth = w_ada.shape[0]
    bias_tiles = _bias_tiles(rel_bias_table, S, min(TQ, S))
    for l in range(depth):
        lambda_init = 0.8 - 0.6 * math.exp(-0.3 * l)
        mod = _ada(c, w_ada[l], b_ada[l])
        qg, kg, gk, kgt, gkt, vg, rg, qd, kd, vd = _inproj(
            x, mod, g_norm1[l], w_in[l], w_gk_up[l], b_gk_up[l], g_qnorm[l], g_knorm[l])
        og = _gla(qg, kg, gk, kgt, gkt, vg, rg, g_gla_out[l])
        lamv = jnp.stack([lambda_q1[l], lambda_k1[l], lambda_q2[l], lambda_k2[l]]).astype(f32)
        od = lax.cond(_scores_bounded(rel_bias_table, g_qnorm[l], g_knorm[l]),
                      functools.partial(_attn, lambda_init=lambda_init, bounded=True),
                      functools.partial(_attn, lambda_init=lambda_init, bounded=False),
                      qd, kd, vd, bias_tiles, lamv, g_subln[l])
        x1, hp, lp, rw, cnt, snap = _outproj(og, od, x, mod, w_out[l], g_norm2[l], w_router[l], b_router[l])
        x = _moe(hp, lp, rw, cnt, snap, x1, mod, w_gate_up[l], b_gate_up[l], w_down[l], b_down[l])
    return x
```
